```python
import jax, jax.numpy as jnp
from jax import lax
import numpy as np

D_MODEL = 1024
BATCH = 8
SEQ = 8192
DEPTH = 4

N_EVEN = (DEPTH + 1) // 2
N_ODD = DEPTH // 2

MLA_HEADS = 8
MLA_Q_RANK = 384
MLA_KV_RANK = 256
MLA_NOPE = 64
MLA_ROPE = 32
MLA_V = 64
MLA_QK = MLA_NOPE + MLA_ROPE
Q_BLOCK = 128

RET_HEADS = 8
RET_DK = 64
RET_DV = 64
RET_CHUNK = 128
RET_DECAY_BASE = 5.0

GLA_HEADS = 4
GLA_DK = 128
GLA_DV = 256
GLA_GATE_RANK = 16
GLA_TAU = 16.0
GLA_CHUNK = 64

D_FF = 2816
CONV_W = 3

ROPE_THETA = 10000.0
EPS = 1e-6

EVEN_SPLIT = (MLA_Q_RANK, MLA_KV_RANK, MLA_ROPE,
              RET_HEADS * RET_DK, RET_HEADS * RET_DK, RET_HEADS * RET_DV, RET_HEADS * RET_DV)
EVEN_IN = MLA_Q_RANK + MLA_KV_RANK + MLA_ROPE + 2 * RET_HEADS * RET_DK + 2 * RET_HEADS * RET_DV
EVEN_OUT = MLA_HEADS * MLA_V + RET_HEADS * RET_DV
ODD_SPLIT = (GLA_HEADS * GLA_DK, GLA_HEADS * GLA_DK, GLA_HEADS * GLA_DV, GLA_HEADS * GLA_DV,
             GLA_GATE_RANK, GLA_GATE_RANK)
ODD_IN = 2 * GLA_HEADS * GLA_DK + 2 * GLA_HEADS * GLA_DV + 2 * GLA_GATE_RANK
ODD_OUT = GLA_HEADS * GLA_DV

kernel_name = "hybrid_mla_retention_gla_convffn_encoder"


def _split(p, sizes):
    outs, s = [], 0
    for n in sizes:
        outs.append(p[..., s:s + n])
        s += n
    return outs


def _rmsnorm(x, g):
    xf = x.astype(jnp.float32)
    y = xf * lax.rsqrt(jnp.mean(xf * xf, axis=-1, keepdims=True) + EPS)
    return (y * g.astype(jnp.float32)).astype(x.dtype)


def _rope(x, positions):
    half = x.shape[-1] // 2
    inv = ROPE_THETA ** (-jnp.arange(half, dtype=jnp.float32) / half)
    ang = positions.astype(jnp.float32)[:, :, None] * inv
    cos = jnp.cos(ang)[:, :, None, :]
    sin = jnp.sin(ang)[:, :, None, :]
    x1 = x[..., :half].astype(jnp.float32)
    x2 = x[..., half:].astype(jnp.float32)
    return jnp.concatenate([x1 * cos - x2 * sin, x2 * cos + x1 * sin], axis=-1).astype(x.dtype)


def _mla(cq, ckv, k_rope, positions, q_norm, kv_norm, w_uq, w_ukv, q_head_norm, k_head_norm):
    B, S, _ = cq.shape
    H = MLA_HEADS
    q = (_rmsnorm(cq, q_norm) @ w_uq).reshape(B, S, H, MLA_QK)
    kv = (_rmsnorm(ckv, kv_norm) @ w_ukv).reshape(B, S, H, MLA_NOPE + MLA_V)
    k_nope, v = kv[..., :MLA_NOPE], kv[..., MLA_NOPE:]
    k = jnp.concatenate([k_nope, jnp.broadcast_to(k_rope[:, :, None, :], (B, S, H, MLA_ROPE))], axis=-1)
    q = _rmsnorm(q, q_head_norm)
    k = _rmsnorm(k, k_head_norm)
    q = jnp.concatenate([q[..., :MLA_NOPE], _rope(q[..., MLA_NOPE:], positions)], axis=-1)
    k = jnp.concatenate([k[..., :MLA_NOPE], _rope(k[..., MLA_NOPE:], positions)], axis=-1)
    scale = MLA_QK ** -0.5
    nb = S // Q_BLOCK
    qb = q.reshape(B, nb, Q_BLOCK, H, MLA_QK).transpose(1, 0, 2, 3, 4)

    def block(qi):
        s = jnp.einsum('bqhd,bkhd->bhqk', qi, k).astype(jnp.float32) * scale
        p = jax.nn.softmax(s, axis=-1).astype(v.dtype)
        return jnp.einsum('bhqk,bkhe->bqhe', p, v)

    o = lax.map(block, qb)
    return o.transpose(1, 0, 2, 3, 4).reshape(B, S, H * MLA_V)


def _retention_dir(q, k, v, log_gamma, include_diag):
    B, S, H, dk = q.shape
    dv = v.shape[-1]
    C = RET_CHUNK
    n = S // C
    q = q.reshape(B, n, C, H, dk)
    k = k.reshape(B, n, C, H, dk)
    v = v.reshape(B, n, C, H, dv)
    lg = log_gamma.astype(jnp.float32)
    pos = jnp.arange(C, dtype=jnp.float32)
    rel = pos[:, None] - pos[None, :]
    mask = (rel >= 0) if include_diag else (rel > 0)
    decay = jnp.where(mask[None], jnp.exp(lg[:, None, None] * jnp.maximum(rel, 0.0)[None]), 0.0)
    scores = jnp.einsum('bnihd,bnjhd->bnhij', q, k) * decay
    intra = jnp.einsum('bnhij,bnjhe->bnihe', scores, v)
    zeta = jnp.exp(lg[:, None] * (C - 1 - pos)[None])
    chunk_state = jnp.einsum('bnjhd,bnjhe,hj->bnhde', k, v, zeta)
    chunk_decay = jnp.exp(lg * C)[None, :, None, None]

    def step(R, s):
        return chunk_decay * R + s, R

    R0 = jnp.zeros((B, H, dk, dv), chunk_state.dtype)
    _, R_prev = lax.scan(step, R0, jnp.moveaxis(chunk_state, 1, 0))
    R_prev = jnp.moveaxis(R_prev, 0, 1)
    xi = jnp.exp(lg[:, None] * (pos + 1.0)[None]).T
    cross = jnp.einsum('bnihd,bnhde->bnihe', q, R_prev) * xi[None, None, :, :, None]
    return (intra + cross).reshape(B, S, H, dv)


def _retention(rq, rk, rv, rg, positions, theta_fwd, theta_bwd, out_norm):
    B, S, _ = rq.shape
    H = RET_HEADS
    q = _rope(rq.reshape(B, S, H, RET_DK), positions)
    k = _rope(rk.reshape(B, S, H, RET_DK), positions) * (RET_DK ** -0.5)
    v = rv.reshape(B, S, H, RET_DV)
    lg_f = jnp.log1p(-jnp.exp2(-theta_fwd.astype(jnp.float32)))
    lg_b = jnp.log1p(-jnp.exp2(-theta_bwd.astype(jnp.float32)))
    o_f = _retention_dir(q, k, v, lg_f, True)
    o_b = jnp.flip(_retention_dir(jnp.flip(q, 1), jnp.flip(k, 1), jnp.flip(v, 1), lg_b, False), 1)
    o = _rmsnorm(o_f + o_b, out_norm)
    return jax.nn.silu(rg) * o.reshape(B, S, H * RET_DV)


def _gla_dir(q, k, v, log_a, include_diag):
    B, S, H, dk = q.shape
    dv = v.shape[-1]
    C = GLA_CHUNK
    n = S // C
    q = q.reshape(B, n, C, H, dk)
    k = k.reshape(B, n, C, H, dk)
    v = v.reshape(B, n, C, H, dv)
    b = jnp.cumsum(log_a.astype(jnp.float32).reshape(B, n, C, H, dk), axis=2)
    b_mid = b[:, :, C // 2:C // 2 + 1]
    b_last = b[:, :, -1]
    qc = q * jnp.exp(b - b_mid)
    kc = k * jnp.exp(b_mid - b)
    A = jnp.einsum('bnihd,bnjhd->bnhij', qc, kc)
    pos = jnp.arange(C)
    mask = (pos[:, None] >= pos[None, :]) if include_diag else (pos[:, None] > pos[None, :])
    A = jnp.where(mask, A, 0.0)
    intra = jnp.einsum('bnhij,bnjhe->bnihe', A, v)
    k_dec = k * jnp.exp(b_last[:, :, None] - b)
    chunk_state = jnp.einsum('bnjhd,bnjhe->bnhde', k_dec, v)
    chunk_decay = jnp.exp(b_last)

    def step(Sm, xs):
        dcy, st = xs
        return dcy[..., None] * Sm + st, Sm

    S0 = jnp.zeros((B, H, dk, dv), chunk_state.dtype)
    _, S_prev = lax.scan(step, S0, (jnp.moveaxis(chunk_decay, 1, 0), jnp.moveaxis(chunk_state, 1, 0)))
    S_prev = jnp.moveaxis(S_prev, 0, 1)
    inter = jnp.einsum('bnihd,bnhde->bnihe', q * jnp.exp(b), S_prev)
    return (intra + inter).reshape(B, S, H, dv)


def _gla(gq, gk, gv, gr, ga_f, ga_b, w_gate_fwd, b_gate_fwd, w_gate_bwd, b_gate_bwd, out_norm):
    B, S, _ = gq.shape
    H = GLA_HEADS
    q = gq.reshape(B, S, H, GLA_DK) * (GLA_DK ** -0.5)
    k = gk.reshape(B, S, H, GLA_DK)
    v = gv.reshape(B, S, H, GLA_DV)
    la_f = (jax.nn.log_sigmoid((ga_f @ w_gate_fwd + b_gate_fwd).astype(jnp.float32)) / GLA_TAU).reshape(B, S, H, GLA_DK)
    la_b = (jax.nn.log_sigmoid((ga_b @ w_gate_bwd + b_gate_bwd).astype(jnp.float32)) / GLA_TAU).reshape(B, S, H, GLA_DK)
    o_f = _gla_dir(q, k, v, la_f, True)
    o_b = jnp.flip(_gla_dir(jnp.flip(q, 1), jnp.flip(k, 1), jnp.flip(v, 1), jnp.flip(la_b, 1), False), 1)
    o = _rmsnorm(o_f + o_b, out_norm)
    return jax.nn.silu(gr) * o.reshape(B, S, H * GLA_DV)


def _conv_ffn(x, norm_g, w_up, conv_w, conv_b, w_down):
    h = _rmsnorm(x, norm_g)
    up = h @ w_up
    gate, val = up[..., :D_FF], up[..., D_FF:]
    gate = lax.conv_general_dilated(gate, conv_w[:, None, :].astype(gate.dtype), window_strides=(1,),
                                    padding='SAME', dimension_numbers=('NWC', 'WIO', 'NWC'),
                                    feature_group_count=D_FF) + conv_b
    return (jax.nn.silu(gate) * val) @ w_down


def _fwd_setup_inputs(seed: int = 0) -> dict:
    key = jax.random.key(seed)
    ks = iter(jax.random.split(key, 40))

    def dense(shape, fan_in):
        return jax.random.normal(next(ks), shape, jnp.float32) * (fan_in ** -0.5)

    def gain(shape):
        return 1.0 + 0.02 * jax.random.normal(next(ks), shape, jnp.float32)

    def small(shape, s):
        return s * jax.random.normal(next(ks), shape, jnp.float32)

    x = jax.random.normal(next(ks), (BATCH, SEQ, D_MODEL), jnp.float32)
    start = jax.random.randint(next(ks), (BATCH, 1), 0, 4096)
    positions = (start + jnp.arange(SEQ)[None, :]).astype(jnp.int32)
    ret_base = RET_DECAY_BASE + jnp.arange(RET_HEADS, dtype=jnp.float32)
    return {
        "x": x,
        "positions": positions,
        "mix_norm_even": gain((N_EVEN, D_MODEL)),
        "w_in_even": dense((N_EVEN, D_MODEL, EVEN_IN), D_MODEL),
        "mla_q_norm": gain((N_EVEN, MLA_Q_RANK)),
        "mla_kv_norm": gain((N_EVEN, MLA_KV_RANK)),
        "mla_w_uq": dense((N_EVEN, MLA_Q_RANK, MLA_HEADS * MLA_QK), MLA_Q_RANK),
        "mla_w_ukv": dense((N_EVEN, MLA_KV_RANK, MLA_HEADS * (MLA_NOPE + MLA_V)), MLA_KV_RANK),
        "mla_q_head_norm": gain((N_EVEN, MLA_QK)),
        "mla_k_head_norm": gain((N_EVEN, MLA_QK)),
        "ret_theta_fwd": ret_base + small((N_EVEN, RET_HEADS), 0.1),
        "ret_theta_bwd": ret_base + small((N_EVEN, RET_HEADS), 0.1),
        "ret_out_norm": gain((N_EVEN, RET_HEADS, RET_DV)),
        "w_out_even": dense((N_EVEN, EVEN_OUT, D_MODEL), EVEN_OUT),
        "mix_norm_odd": gain((N_ODD, D_MODEL)),
        "w_in_odd": dense((N_ODD, D_MODEL, ODD_IN), D_MODEL),
        "gla_w_gate_fwd": dense((N_ODD, GLA_GATE_RANK, GLA_HEADS * GLA_DK), GLA_GATE_RANK),
        "gla_b_gate_fwd": small((N_ODD, GLA_HEADS * GLA_DK), 0.1),
        "gla_w_gate_bwd": dense((N_ODD, GLA_GATE_RANK, GLA_HEADS * GLA_DK), GLA_GATE_RANK),
        "gla_b_gate_bwd": small((N_ODD, GLA_HEADS * GLA_DK), 0.1),
        "gla_out_norm": gain((N_ODD, GLA_HEADS, GLA_DV)),
        "w_out_odd": dense((N_ODD, ODD_OUT, D_MODEL), ODD_OUT),
        "ffn_norm": gain((DEPTH, D_MODEL)),
        "ffn_w_up": dense((DEPTH, D_MODEL, 2 * D_FF), D_MODEL),
        "ffn_conv_w": dense((DEPTH, CONV_W, D_FF), CONV_W),
        "ffn_conv_b": small((DEPTH, D_FF), 0.02),
        "ffn_w_down": dense((DEPTH, D_FF, D_MODEL), D_FF),
    }


def _fwd_reference(x, positions, mix_norm_even, w_in_even, mla_q_norm, mla_kv_norm, mla_w_uq, mla_w_ukv,
              mla_q_head_norm, mla_k_head_norm, ret_theta_fwd, ret_theta_bwd, ret_out_norm, w_out_even,
              mix_norm_odd, w_in_odd, gla_w_gate_fwd, gla_b_gate_fwd, gla_w_gate_bwd, gla_b_gate_bwd,
              gla_out_norm, w_out_odd, ffn_norm, ffn_w_up, ffn_conv_w, ffn_conv_b, ffn_w_down):
    for layer in range(DEPTH):
        i = layer // 2
        if layer % 2 == 0:
            h = _rmsnorm(x, mix_norm_even[i])
            cq, ckv, k_rope, rq, rk, rv, rg = _split(h @ w_in_even[i], EVEN_SPLIT)
            a = _mla(cq, ckv, k_rope, positions, mla_q_norm[i], mla_kv_norm[i], mla_w_uq[i], mla_w_ukv[i],
                     mla_q_head_norm[i], mla_k_head_norm[i])
            r = _retention(rq, rk, rv, rg, positions, ret_theta_fwd[i], ret_theta_bwd[i], ret_out_norm[i])
            x = x + jnp.concatenate([a, r], axis=-1) @ w_out_even[i]
        else:
            h = _rmsnorm(x, mix_norm_odd[i])
            gq, gk, gv, gr, ga_f, ga_b = _split(h @ w_in_odd[i], ODD_SPLIT)
            g = _gla(gq, gk, gv, gr, ga_f, ga_b, gla_w_gate_fwd[i], gla_b_gate_fwd[i],
                     gla_w_gate_bwd[i], gla_b_gate_bwd[i], gla_out_norm[i])
            x = x + g @ w_out_odd[i]
        x = x + _conv_ffn(x, ffn_norm[layer], ffn_w_up[layer], ffn_conv_w[layer], ffn_conv_b[layer],
                          ffn_w_down[layer])
    return x


import jax as _jax
import jax.numpy as _jnp

TWIN_FORMAT = 'train_step'
FWD_PARAMS = ['x', 'positions', 'mix_norm_even', 'w_in_even', 'mla_q_norm', 'mla_kv_norm', 'mla_w_uq', 'mla_w_ukv', 'mla_q_head_norm', 'mla_k_head_norm', 'ret_theta_fwd', 'ret_theta_bwd', 'ret_out_norm', 'w_out_even', 'mix_norm_odd', 'w_in_odd', 'gla_w_gate_fwd', 'gla_b_gate_fwd', 'gla_w_gate_bwd', 'gla_b_gate_bwd', 'gla_out_norm', 'w_out_odd', 'ffn_norm', 'ffn_w_up', 'ffn_conv_w', 'ffn_conv_b', 'ffn_w_down']
TWIN_WEIGHTS = ['mix_norm_even', 'w_in_even', 'mla_q_norm', 'mla_kv_norm', 'mla_w_uq', 'mla_w_ukv', 'mla_q_head_norm', 'mla_k_head_norm', 'ret_theta_fwd', 'ret_theta_bwd', 'ret_out_norm', 'w_out_even', 'mix_norm_odd', 'w_in_odd', 'gla_w_gate_fwd', 'gla_b_gate_fwd', 'gla_w_gate_bwd', 'gla_b_gate_bwd', 'gla_out_norm', 'w_out_odd', 'ffn_norm', 'ffn_w_up', 'ffn_conv_w', 'ffn_conv_b', 'ffn_w_down']
TWIN_DIFF_INPUT = 'x'
TWIN_INPUTS = ['x', 'positions', 'mix_norm_even', 'w_in_even', 'mla_q_norm', 'mla_kv_norm', 'mla_w_uq', 'mla_w_ukv', 'mla_q_head_norm', 'mla_k_head_norm', 'ret_theta_fwd', 'ret_theta_bwd', 'ret_out_norm', 'w_out_even', 'mix_norm_odd', 'w_in_odd', 'gla_w_gate_fwd', 'gla_b_gate_fwd', 'gla_w_gate_bwd', 'gla_b_gate_bwd', 'gla_out_norm', 'w_out_odd', 'ffn_norm', 'ffn_w_up', 'ffn_conv_w', 'ffn_conv_b', 'ffn_w_down', 'loss_target', 'm_mix_norm_even', 'm_w_in_even', 'm_mla_q_norm', 'm_mla_kv_norm', 'm_mla_w_uq', 'm_mla_w_ukv', 'm_mla_q_head_norm', 'm_mla_k_head_norm', 'm_ret_theta_fwd', 'm_ret_theta_bwd', 'm_ret_out_norm', 'm_w_out_even', 'm_mix_norm_odd', 'm_w_in_odd', 'm_gla_w_gate_fwd', 'm_gla_b_gate_fwd', 'm_gla_w_gate_bwd', 'm_gla_b_gate_bwd', 'm_gla_out_norm', 'm_w_out_odd', 'm_ffn_norm', 'm_ffn_w_up', 'm_ffn_conv_w', 'm_ffn_conv_b', 'm_ffn_w_down', 'v_mix_norm_even', 'v_w_in_even', 'v_mla_q_norm', 'v_mla_kv_norm', 'v_mla_w_uq', 'v_mla_w_ukv', 'v_mla_q_head_norm', 'v_mla_k_head_norm', 'v_ret_theta_fwd', 'v_ret_theta_bwd', 'v_ret_out_norm', 'v_w_out_even', 'v_mix_norm_odd', 'v_w_in_odd', 'v_gla_w_gate_fwd', 'v_gla_b_gate_fwd', 'v_gla_w_gate_bwd', 'v_gla_b_gate_bwd', 'v_gla_out_norm', 'v_w_out_odd', 'v_ffn_norm', 'v_ffn_w_up', 'v_ffn_conv_w', 'v_ffn_conv_b', 'v_ffn_w_down']
TWIN_OUTPUTS = ['loss', 'grad_x', 'grad_mix_norm_even', 'grad_w_in_even', 'grad_mla_q_norm', 'grad_mla_kv_norm', 'grad_mla_w_uq', 'grad_mla_w_ukv', 'grad_mla_q_head_norm', 'grad_mla_k_head_norm', 'grad_ret_theta_fwd', 'grad_ret_theta_bwd', 'grad_ret_out_norm', 'grad_w_out_even', 'grad_mix_norm_odd', 'grad_w_in_odd', 'grad_gla_w_gate_fwd', 'grad_gla_b_gate_fwd', 'grad_gla_w_gate_bwd', 'grad_gla_b_gate_bwd', 'grad_gla_out_norm', 'grad_w_out_odd', 'grad_ffn_norm', 'grad_ffn_w_up', 'grad_ffn_conv_w', 'grad_ffn_conv_b', 'grad_ffn_w_down', 'delta_mix_norm_even', 'delta_w_in_even', 'delta_mla_q_norm', 'delta_mla_kv_norm', 'delta_mla_w_uq', 'delta_mla_w_ukv', 'delta_mla_q_head_norm', 'delta_mla_k_head_norm', 'delta_ret_theta_fwd', 'delta_ret_theta_bwd', 'delta_ret_out_norm', 'delta_w_out_even', 'delta_mix_norm_odd', 'delta_w_in_odd', 'delta_gla_w_gate_fwd', 'delta_gla_b_gate_fwd', 'delta_gla_w_gate_bwd', 'delta_gla_b_gate_bwd', 'delta_gla_out_norm', 'delta_w_out_odd', 'delta_ffn_norm', 'delta_ffn_w_up', 'delta_ffn_conv_w', 'delta_ffn_conv_b', 'delta_ffn_w_down', 'new_m_mix_norm_even', 'new_m_w_in_even', 'new_m_mla_q_norm', 'new_m_mla_kv_norm', 'new_m_mla_w_uq', 'new_m_mla_w_ukv', 'new_m_mla_q_head_norm', 'new_m_mla_k_head_norm', 'new_m_ret_theta_fwd', 'new_m_ret_theta_bwd', 'new_m_ret_out_norm', 'new_m_w_out_even', 'new_m_mix_norm_odd', 'new_m_w_in_odd', 'new_m_gla_w_gate_fwd', 'new_m_gla_b_gate_fwd', 'new_m_gla_w_gate_bwd', 'new_m_gla_b_gate_bwd', 'new_m_gla_out_norm', 'new_m_w_out_odd', 'new_m_ffn_norm', 'new_m_ffn_w_up', 'new_m_ffn_conv_w', 'new_m_ffn_conv_b', 'new_m_ffn_w_down', 'new_v_mix_norm_even', 'new_v_w_in_even', 'new_v_mla_q_norm', 'new_v_mla_kv_norm', 'new_v_mla_w_uq', 'new_v_mla_w_ukv', 'new_v_mla_q_head_norm', 'new_v_mla_k_head_norm', 'new_v_ret_theta_fwd', 'new_v_ret_theta_bwd', 'new_v_ret_out_norm', 'new_v_w_out_even', 'new_v_mix_norm_odd', 'new_v_w_in_odd', 'new_v_gla_w_gate_fwd', 'new_v_gla_b_gate_fwd', 'new_v_gla_w_gate_bwd', 'new_v_gla_b_gate_bwd', 'new_v_gla_out_norm', 'new_v_w_out_odd', 'new_v_ffn_norm', 'new_v_ffn_w_up', 'new_v_ffn_conv_w', 'new_v_ffn_conv_b', 'new_v_ffn_w_down']
TWIN_LEAF_KINDS = {'loss': 'loss', 'grad_x': 'grad_x', 'grad_mix_norm_even': 'grad_w', 'grad_w_in_even': 'grad_w', 'grad_mla_q_norm': 'grad_w', 'grad_mla_kv_norm': 'grad_w', 'grad_mla_w_uq': 'grad_w', 'grad_mla_w_ukv': 'grad_w', 'grad_mla_q_head_norm': 'grad_w', 'grad_mla_k_head_norm': 'grad_w', 'grad_ret_theta_fwd': 'grad_w', 'grad_ret_theta_bwd': 'grad_w', 'grad_ret_out_norm': 'grad_w', 'grad_w_out_even': 'grad_w', 'grad_mix_norm_odd': 'grad_w', 'grad_w_in_odd': 'grad_w', 'grad_gla_w_gate_fwd': 'grad_w', 'grad_gla_b_gate_fwd': 'grad_w', 'grad_gla_w_gate_bwd': 'grad_w', 'grad_gla_b_gate_bwd': 'grad_w', 'grad_gla_out_norm': 'grad_w', 'grad_w_out_odd': 'grad_w', 'grad_ffn_norm': 'grad_w', 'grad_ffn_w_up': 'grad_w', 'grad_ffn_conv_w': 'grad_w', 'grad_ffn_conv_b': 'grad_w', 'grad_ffn_w_down': 'grad_w', 'delta_mix_norm_even': 'delta_w', 'delta_w_in_even': 'delta_w', 'delta_mla_q_norm': 'delta_w', 'delta_mla_kv_norm': 'delta_w', 'delta_mla_w_uq': 'delta_w', 'delta_mla_w_ukv': 'delta_w', 'delta_mla_q_head_norm': 'delta_w', 'delta_mla_k_head_norm': 'delta_w', 'delta_ret_theta_fwd': 'delta_w', 'delta_ret_theta_bwd': 'delta_w', 'delta_ret_out_norm': 'delta_w', 'delta_w_out_even': 'delta_w', 'delta_mix_norm_odd': 'delta_w', 'delta_w_in_odd': 'delta_w', 'delta_gla_w_gate_fwd': 'delta_w', 'delta_gla_b_gate_fwd': 'delta_w', 'delta_gla_w_gate_bwd': 'delta_w', 'delta_gla_b_gate_bwd': 'delta_w', 'delta_gla_out_norm': 'delta_w', 'delta_w_out_odd': 'delta_w', 'delta_ffn_norm': 'delta_w', 'delta_ffn_w_up': 'delta_w', 'delta_ffn_conv_w': 'delta_w', 'delta_ffn_conv_b': 'delta_w', 'delta_ffn_w_down': 'delta_w', 'new_m_mix_norm_even': 'new_m', 'new_m_w_in_even': 'new_m', 'new_m_mla_q_norm': 'new_m', 'new_m_mla_kv_norm': 'new_m', 'new_m_mla_w_uq': 'new_m', 'new_m_mla_w_ukv': 'new_m', 'new_m_mla_q_head_norm': 'new_m', 'new_m_mla_k_head_norm': 'new_m', 'new_m_ret_theta_fwd': 'new_m', 'new_m_ret_theta_bwd': 'new_m', 'new_m_ret_out_norm': 'new_m', 'new_m_w_out_even': 'new_m', 'new_m_mix_norm_odd': 'new_m', 'new_m_w_in_odd': 'new_m', 'new_m_gla_w_gate_fwd': 'new_m', 'new_m_gla_b_gate_fwd': 'new_m', 'new_m_gla_w_gate_bwd': 'new_m', 'new_m_gla_b_gate_bwd': 'new_m', 'new_m_gla_out_norm': 'new_m', 'new_m_w_out_odd': 'new_m', 'new_m_ffn_norm': 'new_m', 'new_m_ffn_w_up': 'new_m', 'new_m_ffn_conv_w': 'new_m', 'new_m_ffn_conv_b': 'new_m', 'new_m_ffn_w_down': 'new_m', 'new_v_mix_norm_even': 'new_v', 'new_v_w_in_even': 'new_v', 'new_v_mla_q_norm': 'new_v', 'new_v_mla_kv_norm': 'new_v', 'new_v_mla_w_uq': 'new_v', 'new_v_mla_w_ukv': 'new_v', 'new_v_mla_q_head_norm': 'new_v', 'new_v_mla_k_head_norm': 'new_v', 'new_v_ret_theta_fwd': 'new_v', 'new_v_ret_theta_bwd': 'new_v', 'new_v_ret_out_norm': 'new_v', 'new_v_w_out_even': 'new_v', 'new_v_mix_norm_odd': 'new_v', 'new_v_w_in_odd': 'new_v', 'new_v_gla_w_gate_fwd': 'new_v', 'new_v_gla_b_gate_fwd': 'new_v', 'new_v_gla_w_gate_bwd': 'new_v', 'new_v_gla_b_gate_bwd': 'new_v', 'new_v_gla_out_norm': 'new_v', 'new_v_w_out_odd': 'new_v', 'new_v_ffn_norm': 'new_v', 'new_v_ffn_w_up': 'new_v', 'new_v_ffn_conv_w': 'new_v', 'new_v_ffn_conv_b': 'new_v', 'new_v_ffn_w_down': 'new_v'}


def _forward(args):
    return _fwd_reference(*[args[k] for k in FWD_PARAMS])


def _output_shape():
    def fwd():
        inp = _fwd_setup_inputs(0)
        return _fwd_reference(*[inp[k] for k in FWD_PARAMS])
    out = _jax.eval_shape(fwd)
    return out.shape, out.dtype

N_MICROBATCH = 1
ADAM_LR = 0.001
ADAM_B1 = 0.9
ADAM_B2 = 0.999
ADAM_EPS = 1e-08
ADAM_WD = 0.01
ADAM_STEP = 10
PER_EXAMPLE_BATCH_AXIS = {'x': 0, 'positions': 0, 'loss_target': 0}
SHARED_INPUTS = []
_WEIGHT_DTYPES = {'mix_norm_even': _jnp.float32, 'w_in_even': _jnp.float32, 'mla_q_norm': _jnp.float32, 'mla_kv_norm': _jnp.float32, 'mla_w_uq': _jnp.float32, 'mla_w_ukv': _jnp.float32, 'mla_q_head_norm': _jnp.float32, 'mla_k_head_norm': _jnp.float32, 'ret_theta_fwd': _jnp.float32, 'ret_theta_bwd': _jnp.float32, 'ret_out_norm': _jnp.float32, 'w_out_even': _jnp.float32, 'mix_norm_odd': _jnp.float32, 'w_in_odd': _jnp.float32, 'gla_w_gate_fwd': _jnp.float32, 'gla_b_gate_fwd': _jnp.float32, 'gla_w_gate_bwd': _jnp.float32, 'gla_b_gate_bwd': _jnp.float32, 'gla_out_norm': _jnp.float32, 'w_out_odd': _jnp.float32, 'ffn_norm': _jnp.float32, 'ffn_w_up': _jnp.float32, 'ffn_conv_w': _jnp.float32, 'ffn_conv_b': _jnp.float32, 'ffn_w_down': _jnp.float32}
MOMENT_SCALE = {'mix_norm_even': 1.129207e+01, 'w_in_even': 1.014436e+00, 'mla_q_norm': 2.483268e-01, 'mla_kv_norm': 5.824738e-01, 'mla_w_uq': 1.727182e-01, 'mla_w_ukv': 2.143568e-01, 'mla_q_head_norm': 8.369228e-01, 'mla_k_head_norm': 8.389455e-01, 'ret_theta_fwd': 3.164217e+00, 'ret_theta_bwd': 5.391163e+00, 'ret_out_norm': 2.221213e+01, 'w_out_even': 8.502266e-01, 'mix_norm_odd': 2.611582e+01, 'w_in_odd': 8.207212e-01, 'gla_w_gate_fwd': 7.471812e-02, 'gla_b_gate_fwd': 3.323759e-01, 'gla_w_gate_bwd': 7.365222e-02, 'gla_b_gate_bwd': 2.963041e-01, 'gla_out_norm': 2.270272e+01, 'w_out_odd': 7.552979e-01, 'ffn_norm': 5.203505e+01, 'ffn_w_up': 5.525127e-01, 'ffn_conv_w': 5.502707e+00, 'ffn_conv_b': 6.832808e+00, 'ffn_w_down': 9.278417e-01}


def _to_microbatches(a, axis):
    t = _jnp.moveaxis(a, axis, 0)
    t = t.reshape((N_MICROBATCH, t.shape[0] // N_MICROBATCH) + t.shape[1:])
    return _jnp.moveaxis(t, 1, axis + 1)


def setup_inputs(seed: int = 0) -> dict:
    inp = _fwd_setup_inputs(seed)
    key = _jax.random.fold_in(_jax.random.key(seed), 7919)
    shape, _ = _output_shape()
    out = dict(inp)
    out["loss_target"] = _jax.random.normal(_jax.random.fold_in(key, 0), shape, _jnp.float32)
    for i, name in enumerate(TWIN_WEIGHTS):
        w = inp[name].astype(_jnp.float32)
        if MOMENT_SCALE is None:
            s = _jnp.sqrt(_jnp.mean(_jnp.square(w)) + 1e-30)
        else:
            s = MOMENT_SCALE[name]
        km, kv = _jax.random.split(_jax.random.fold_in(key, i + 1))
        out[name] = w
        out["m_" + name] = s * _jax.random.normal(km, w.shape, _jnp.float32)
        out["v_" + name] = (s * s) * _jax.random.uniform(kv, w.shape, _jnp.float32, 0.5, 1.5)
    if N_MICROBATCH > 1:
        for name, axis in PER_EXAMPLE_BATCH_AXIS.items():
            out[name] = _to_microbatches(out[name], axis)
    return {'x': out['x'], 'positions': out['positions'], 'mix_norm_even': out['mix_norm_even'], 'w_in_even': out['w_in_even'], 'mla_q_norm': out['mla_q_norm'], 'mla_kv_norm': out['mla_kv_norm'], 'mla_w_uq': out['mla_w_uq'], 'mla_w_ukv': out['mla_w_ukv'], 'mla_q_head_norm': out['mla_q_head_norm'], 'mla_k_head_norm': out['mla_k_head_norm'], 'ret_theta_fwd': out['ret_theta_fwd'], 'ret_theta_bwd': out['ret_theta_bwd'], 'ret_out_norm': out['ret_out_norm'], 'w_out_even': out['w_out_even'], 'mix_norm_odd': out['mix_norm_odd'], 'w_in_odd': out['w_in_odd'], 'gla_w_gate_fwd': out['gla_w_gate_fwd'], 'gla_b_gate_fwd': out['gla_b_gate_fwd'], 'gla_w_gate_bwd': out['gla_w_gate_bwd'], 'gla_b_gate_bwd': out['gla_b_gate_bwd'], 'gla_out_norm': out['gla_out_norm'], 'w_out_odd': out['w_out_odd'], 'ffn_norm': out['ffn_norm'], 'ffn_w_up': out['ffn_w_up'], 'ffn_conv_w': out['ffn_conv_w'], 'ffn_conv_b': out['ffn_conv_b'], 'ffn_w_down': out['ffn_w_down'], 'loss_target': out['loss_target'], 'm_mix_norm_even': out['m_mix_norm_even'], 'm_w_in_even': out['m_w_in_even'], 'm_mla_q_norm': out['m_mla_q_norm'], 'm_mla_kv_norm': out['m_mla_kv_norm'], 'm_mla_w_uq': out['m_mla_w_uq'], 'm_mla_w_ukv': out['m_mla_w_ukv'], 'm_mla_q_head_norm': out['m_mla_q_head_norm'], 'm_mla_k_head_norm': out['m_mla_k_head_norm'], 'm_ret_theta_fwd': out['m_ret_theta_fwd'], 'm_ret_theta_bwd': out['m_ret_theta_bwd'], 'm_ret_out_norm': out['m_ret_out_norm'], 'm_w_out_even': out['m_w_out_even'], 'm_mix_norm_odd': out['m_mix_norm_odd'], 'm_w_in_odd': out['m_w_in_odd'], 'm_gla_w_gate_fwd': out['m_gla_w_gate_fwd'], 'm_gla_b_gate_fwd': out['m_gla_b_gate_fwd'], 'm_gla_w_gate_bwd': out['m_gla_w_gate_bwd'], 'm_gla_b_gate_bwd': out['m_gla_b_gate_bwd'], 'm_gla_out_norm': out['m_gla_out_norm'], 'm_w_out_odd': out['m_w_out_odd'], 'm_ffn_norm': out['m_ffn_norm'], 'm_ffn_w_up': out['m_ffn_w_up'], 'm_ffn_conv_w': out['m_ffn_conv_w'], 'm_ffn_conv_b': out['m_ffn_conv_b'], 'm_ffn_w_down': out['m_ffn_w_down'], 'v_mix_norm_even': out['v_mix_norm_even'], 'v_w_in_even': out['v_w_in_even'], 'v_mla_q_norm': out['v_mla_q_norm'], 'v_mla_kv_norm': out['v_mla_kv_norm'], 'v_mla_w_uq': out['v_mla_w_uq'], 'v_mla_w_ukv': out['v_mla_w_ukv'], 'v_mla_q_head_norm': out['v_mla_q_head_norm'], 'v_mla_k_head_norm': out['v_mla_k_head_norm'], 'v_ret_theta_fwd': out['v_ret_theta_fwd'], 'v_ret_theta_bwd': out['v_ret_theta_bwd'], 'v_ret_out_norm': out['v_ret_out_norm'], 'v_w_out_even': out['v_w_out_even'], 'v_mix_norm_odd': out['v_mix_norm_odd'], 'v_w_in_odd': out['v_w_in_odd'], 'v_gla_w_gate_fwd': out['v_gla_w_gate_fwd'], 'v_gla_b_gate_fwd': out['v_gla_b_gate_fwd'], 'v_gla_w_gate_bwd': out['v_gla_w_gate_bwd'], 'v_gla_b_gate_bwd': out['v_gla_b_gate_bwd'], 'v_gla_out_norm': out['v_gla_out_norm'], 'v_w_out_odd': out['v_w_out_odd'], 'v_ffn_norm': out['v_ffn_norm'], 'v_ffn_w_up': out['v_ffn_w_up'], 'v_ffn_conv_w': out['v_ffn_conv_w'], 'v_ffn_conv_b': out['v_ffn_conv_b'], 'v_ffn_w_down': out['v_ffn_w_down']}


def _loss(weights, diff, rest, loss_target):
    with _jax.named_scope("forward"):
        args = {**rest, TWIN_DIFF_INPUT: diff, **{k: w.astype(_WEIGHT_DTYPES[k]) for k, w in weights.items()}}
        y = _forward(args)
    with _jax.named_scope("loss_head"):
        err = _jnp.square(y.astype(_jnp.float32) - loss_target)
        return 0.5 * _jnp.sum(_jnp.mean(err, axis=-1)) if err.ndim else 0.5 * err


def _adamw(w, g, m, v):
    m = ADAM_B1 * m + (1.0 - ADAM_B1) * g
    v = ADAM_B2 * v + (1.0 - ADAM_B2) * _jnp.square(g)
    m_hat = m / (1.0 - ADAM_B1 ** ADAM_STEP)
    v_hat = v / (1.0 - ADAM_B2 ** ADAM_STEP)
    delta = -ADAM_LR * (m_hat / (_jnp.sqrt(v_hat) + ADAM_EPS) + ADAM_WD * w)
    return delta, m, v


def reference(x, positions, mix_norm_even, w_in_even, mla_q_norm, mla_kv_norm, mla_w_uq, mla_w_ukv, mla_q_head_norm, mla_k_head_norm, ret_theta_fwd, ret_theta_bwd, ret_out_norm, w_out_even, mix_norm_odd, w_in_odd, gla_w_gate_fwd, gla_b_gate_fwd, gla_w_gate_bwd, gla_b_gate_bwd, gla_out_norm, w_out_odd, ffn_norm, ffn_w_up, ffn_conv_w, ffn_conv_b, ffn_w_down, loss_target, m_mix_norm_even, m_w_in_even, m_mla_q_norm, m_mla_kv_norm, m_mla_w_uq, m_mla_w_ukv, m_mla_q_head_norm, m_mla_k_head_norm, m_ret_theta_fwd, m_ret_theta_bwd, m_ret_out_norm, m_w_out_even, m_mix_norm_odd, m_w_in_odd, m_gla_w_gate_fwd, m_gla_b_gate_fwd, m_gla_w_gate_bwd, m_gla_b_gate_bwd, m_gla_out_norm, m_w_out_odd, m_ffn_norm, m_ffn_w_up, m_ffn_conv_w, m_ffn_conv_b, m_ffn_w_down, v_mix_norm_even, v_w_in_even, v_mla_q_norm, v_mla_kv_norm, v_mla_w_uq, v_mla_w_ukv, v_mla_q_head_norm, v_mla_k_head_norm, v_ret_theta_fwd, v_ret_theta_bwd, v_ret_out_norm, v_w_out_even, v_mix_norm_odd, v_w_in_odd, v_gla_w_gate_fwd, v_gla_b_gate_fwd, v_gla_w_gate_bwd, v_gla_b_gate_bwd, v_gla_out_norm, v_w_out_odd, v_ffn_norm, v_ffn_w_up, v_ffn_conv_w, v_ffn_conv_b, v_ffn_w_down):
    given = dict(x=x, positions=positions, mix_norm_even=mix_norm_even, w_in_even=w_in_even, mla_q_norm=mla_q_norm, mla_kv_norm=mla_kv_norm, mla_w_uq=mla_w_uq, mla_w_ukv=mla_w_ukv, mla_q_head_norm=mla_q_head_norm, mla_k_head_norm=mla_k_head_norm, ret_theta_fwd=ret_theta_fwd, ret_theta_bwd=ret_theta_bwd, ret_out_norm=ret_out_norm, w_out_even=w_out_even, mix_norm_odd=mix_norm_odd, w_in_odd=w_in_odd, gla_w_gate_fwd=gla_w_gate_fwd, gla_b_gate_fwd=gla_b_gate_fwd, gla_w_gate_bwd=gla_w_gate_bwd, gla_b_gate_bwd=gla_b_gate_bwd, gla_out_norm=gla_out_norm, w_out_odd=w_out_odd, ffn_norm=ffn_norm, ffn_w_up=ffn_w_up, ffn_conv_w=ffn_conv_w, ffn_conv_b=ffn_conv_b, ffn_w_down=ffn_w_down, loss_target=loss_target, m_mix_norm_even=m_mix_norm_even, m_w_in_even=m_w_in_even, m_mla_q_norm=m_mla_q_norm, m_mla_kv_norm=m_mla_kv_norm, m_mla_w_uq=m_mla_w_uq, m_mla_w_ukv=m_mla_w_ukv, m_mla_q_head_norm=m_mla_q_head_norm, m_mla_k_head_norm=m_mla_k_head_norm, m_ret_theta_fwd=m_ret_theta_fwd, m_ret_theta_bwd=m_ret_theta_bwd, m_ret_out_norm=m_ret_out_norm, m_w_out_even=m_w_out_even, m_mix_norm_odd=m_mix_norm_odd, m_w_in_odd=m_w_in_odd, m_gla_w_gate_fwd=m_gla_w_gate_fwd, m_gla_b_gate_fwd=m_gla_b_gate_fwd, m_gla_w_gate_bwd=m_gla_w_gate_bwd, m_gla_b_gate_bwd=m_gla_b_gate_bwd, m_gla_out_norm=m_gla_out_norm, m_w_out_odd=m_w_out_odd, m_ffn_norm=m_ffn_norm, m_ffn_w_up=m_ffn_w_up, m_ffn_conv_w=m_ffn_conv_w, m_ffn_conv_b=m_ffn_conv_b, m_ffn_w_down=m_ffn_w_down, v_mix_norm_even=v_mix_norm_even, v_w_in_even=v_w_in_even, v_mla_q_norm=v_mla_q_norm, v_mla_kv_norm=v_mla_kv_norm, v_mla_w_uq=v_mla_w_uq, v_mla_w_ukv=v_mla_w_ukv, v_mla_q_head_norm=v_mla_q_head_norm, v_mla_k_head_norm=v_mla_k_head_norm, v_ret_theta_fwd=v_ret_theta_fwd, v_ret_theta_bwd=v_ret_theta_bwd, v_ret_out_norm=v_ret_out_norm, v_w_out_even=v_w_out_even, v_mix_norm_odd=v_mix_norm_odd, v_w_in_odd=v_w_in_odd, v_gla_w_gate_fwd=v_gla_w_gate_fwd, v_gla_b_gate_fwd=v_gla_b_gate_fwd, v_gla_w_gate_bwd=v_gla_w_gate_bwd, v_gla_b_gate_bwd=v_gla_b_gate_bwd, v_gla_out_norm=v_gla_out_norm, v_w_out_odd=v_w_out_odd, v_ffn_norm=v_ffn_norm, v_ffn_w_up=v_ffn_w_up, v_ffn_conv_w=v_ffn_conv_w, v_ffn_conv_b=v_ffn_conv_b, v_ffn_w_down=v_ffn_w_down)
    weights = {n: given[n] for n in TWIN_WEIGHTS}
    shared = {n: given[n] for n in SHARED_INPUTS}
    per_example = {n: given[n] for n in ['x', 'positions']}
    grad_fn = _jax.value_and_grad(_loss, argnums=(0, 1))

    def one_microbatch(ex, loss_target):
        ex = dict(ex)
        diff = ex.pop(TWIN_DIFF_INPUT)
        return grad_fn(weights, diff, {**shared, **ex}, loss_target)

    if N_MICROBATCH == 1:
        loss, (grad_w, grad_x) = one_microbatch(per_example, given["loss_target"])
    else:
        def body(carry, xs):
            loss_sum, grad_sum = carry
            l_k, (gw_k, gx_k) = one_microbatch(xs[0], xs[1])
            with _jax.named_scope("update"):
                return (loss_sum + l_k, _jax.tree.map(_jnp.add, grad_sum, gw_k)), gx_k

        init = (_jnp.zeros((), _jnp.float32), _jax.tree.map(_jnp.zeros_like, weights))
        (loss, grad_w), grad_x = _jax.lax.scan(body, init, (per_example, given["loss_target"]))
    with _jax.named_scope("update"):
        delta_w, new_m, new_v = {}, {}, {}
        for n in TWIN_WEIGHTS:
            delta_w[n], new_m[n], new_v[n] = _adamw(weights[n], grad_w[n], given["m_" + n], given["v_" + n])
    return (loss, grad_x, *[grad_w[n] for n in TWIN_WEIGHTS], *[delta_w[n] for n in TWIN_WEIGHTS],
            *[new_m[n] for n in TWIN_WEIGHTS], *[new_v[n] for n in TWIN_WEIGHTS])
```

```python
import math

import jax
import jax.numpy as jnp
from jax import lax
from jax.experimental import pallas as pl
from jax.experimental.pallas import tpu as pltpu

F32 = jnp.float32
BF16 = jnp.bfloat16
MESH = pl.DeviceIdType.MESH

EPS = 1e-6
D_MODEL = 1024
DEPTH = 4
LANES = 128
MLA_H, MLA_QR, MLA_KVR, MLA_NOPE, MLA_ROPE, MLA_V = 8, 384, 256, 64, 32, 64
MLA_QK = MLA_NOPE + MLA_ROPE
MLA_SCALE = MLA_QK ** -0.5
RET_H, RET_DK, RET_DV, RET_C = 8, 64, 64, 128
GLA_H, GLA_DK, GLA_DV, GLA_R, GLA_TAU, GLA_C = 4, 128, 256, 16, 16.0, 64
D_FF = 2816
ROPE_THETA = 10000.0
LN2 = math.log(2.0)
ADAM_LR, ADAM_B1, ADAM_B2, ADAM_EPS, ADAM_WD, ADAM_STEP = 0.001, 0.9, 0.999, 1e-08, 0.01, 10

EV_RET = 4 * RET_H * LANES
EV_CQ = 512
EV_W = 5120
EV_KR_BLK = (EV_RET + EV_CQ + MLA_KVR) // LANES
OD_W = 3200
OD_GA_BLK = 3072 // LANES

VMEM_LIMIT = 56 * 1024 * 1024


def _cp(sem):
    return pltpu.CompilerParams(dimension_semantics=sem, vmem_limit_bytes=VMEM_LIMIT)


def _dot(a, b):
    return jnp.dot(a, b, preferred_element_type=F32)


def _dot_nt(a, b):
    return lax.dot_general(a, b, (((1,), (1,)), ((), ())), preferred_element_type=F32)


def _dot_tn(a, b):
    return lax.dot_general(a, b, (((0,), (0,)), ((), ())), preferred_element_type=F32)


def _bf(x):
    return x.astype(BF16)


def _split3(x):
    h1 = _bf(x)
    r1 = x - h1.astype(F32)
    h2 = _bf(r1)
    h3 = _bf(r1 - h2.astype(F32))
    return h1, h2, h3


def _tile(n, cap):
    if n <= cap:
        return n
    best = None
    for t in range(LANES, cap + 1, LANES):
        if n % t == 0:
            best = t
    assert best is not None, n
    return best


def _mm(a, b, *, ta=False, tb=False, res=None, out_dtype=F32, name):
    assert not (ta and tb)
    if ta:
        kdim, m = a.shape
    else:
        m, kdim = a.shape
    if tb:
        n, kb = b.shape
    else:
        kb, n = b.shape
    assert kb == kdim, (a.shape, b.shape, ta, tb)
    tm = _tile(m, 512)
    tn = _tile(n, 1024)
    tk = _tile(kdim, 512)
    nk = kdim // tk
    a_spec = (pl.BlockSpec((tk, tm), lambda i, j, k: (k, i)) if ta
              else pl.BlockSpec((tm, tk), lambda i, j, k: (i, k)))
    b_spec = (pl.BlockSpec((tn, tk), lambda i, j, k: (j, k)) if tb
              else pl.BlockSpec((tk, tn), lambda i, j, k: (k, j)))
    o_spec = pl.BlockSpec((tm, tn), lambda i, j, k: (i, j))
    has_res = res is not None

    def body(*refs):
        if has_res:
            a_ref, b_ref, r_ref, o_ref, acc = refs
        else:
            a_ref, b_ref, o_ref, acc = refs
        k = pl.program_id(2)

        @pl.when(k == 0)
        def _():
            acc[...] = jnp.zeros_like(acc)

        av, bv = _bf(a_ref[...]), _bf(b_ref[...])
        if ta:
            acc[...] += _dot_tn(av, bv)
        elif tb:
            acc[...] += _dot_nt(av, bv)
        else:
            acc[...] += _dot(av, bv)

        @pl.when(k == nk - 1)
        def _():
            r = acc[...]
            if has_res:
                r = r + r_ref[...]
            o_ref[...] = r.astype(o_ref.dtype)

    ins = [a, b] + ([res] if has_res else [])
    in_specs = [a_spec, b_spec] + ([o_spec] if has_res else [])
    return pl.pallas_call(
        body, name=name, grid=(m // tm, n // tn, nk),
        in_specs=in_specs, out_specs=o_spec,
        out_shape=jax.ShapeDtypeStruct((m, n), out_dtype),
        scratch_shapes=[pltpu.VMEM((tm, tn), F32)],
        compiler_params=_cp(("parallel", "parallel", "arbitrary")),
    )(*ins)


def _ew(fn, rows, pars, outs, accs=(), *, s, ts, name):
    n_in = len(rows) + len(pars)
    n_o = len(outs)

    def body(*refs):
        i = pl.program_id(0)
        vals = fn(*[r[...] for r in refs[:n_in]])
        if not isinstance(vals, (tuple, list)):
            vals = (vals,)
        assert len(vals) == n_o + len(accs), (name, len(vals))
        for r, v in zip(refs[n_in:n_in + n_o], vals[:n_o]):
            r[...] = v.astype(r.dtype)
        for r, v in zip(refs[n_in + n_o:], vals[n_o:]):
            @pl.when(i == 0)
            def _(r=r, v=v):
                r[...] = v

            @pl.when(i > 0)
            def _(r=r, v=v):
                r[...] += v

    in_specs = [sp for _, sp in rows]
    in_specs += [pl.BlockSpec(p.shape, lambda i, nd=p.ndim: (0,) * nd) for p in pars]
    out_specs = [pl.BlockSpec((ts, w), lambda i: (i, 0)) for w, _ in outs]
    out_specs += [pl.BlockSpec((r, w), lambda i: (0, 0)) for r, w in accs]
    out_shape = [jax.ShapeDtypeStruct((s, w), dt) for w, dt in outs]
    out_shape += [jax.ShapeDtypeStruct((r, w), F32) for r, w in accs]
    return pl.pallas_call(
        body, name=name, grid=(s // ts,), in_specs=in_specs, out_specs=out_specs, out_shape=out_shape,
        compiler_params=_cp(("arbitrary",)),
    )(*[a for a, _ in rows], *pars)


def _cols(arr, width, blk, ts):
    return (arr, pl.BlockSpec((ts, width), lambda i, b=blk: (i, b)))


def _lead(arr, d, ts):
    return (arr, pl.BlockSpec((None, ts, arr.shape[2]), lambda i, d=d: (d, i, 0)))


def _rowsum(x):
    return jnp.sum(x, axis=0, keepdims=True)


def _lanesum(x):
    return jnp.sum(x, axis=-1, keepdims=True)


def _gsum(x, group):
    w = x.shape[-1]
    if group == w:
        return jnp.broadcast_to(_lanesum(x), x.shape)
    parts = [jnp.broadcast_to(_lanesum(x[:, g:g + group]), (x.shape[0], group)) for g in range(0, w, group)]
    return jnp.concatenate(parts, axis=-1)


def _gn(x, gain, group, n):
    rstd = lax.rsqrt(_gsum(x * x, group) * (1.0 / n) + EPS)
    xn = x * rstd
    return xn * gain, xn, rstd


def _gn_bwd(dy, xn, rstd, gain, group, n):
    dxn = dy * gain
    dx = rstd * (dxn - xn * (_gsum(dxn * xn, group) * (1.0 / n)))
    return dx, _rowsum(dy * xn)


def _sigmoid(x):
    return 1.0 / (1.0 + jnp.exp(-x))


def _rmsnorm(x_row, g, *, n, s, ts, name):
    w = g.shape[-1]

    def fn(x, gv):
        return _gn(x, gv, w, n)[0]

    return _ew(fn, [x_row], [g], [(w, BF16)], s=s, ts=ts, name=name)[0]


def _rmsnorm_bwd(x_row, g, dh, dres, *, n, s, ts, name):
    w = g.shape[-1]
    has_res = dres is not None

    def fn(x, dhv, *rest):
        gv = rest[-1]
        _, xn, rstd = _gn(x, gv, w, n)
        dx, dg = _gn_bwd(dhv, xn, rstd, gv, w, n)
        if has_res:
            dx = dx + rest[0]
        return dx, dg

    rows = [x_row, _cols(dh, w, 0, ts)] + ([_cols(dres, w, 0, ts)] if has_res else [])
    return _ew(fn, rows, [g], [(w, F32)], [(1, w)], s=s, ts=ts, name=name)


def _rope_tables(pos, real, offset):
    half = real // 2
    inv = ROPE_THETA ** (-jnp.arange(half, dtype=F32) / half)
    ang = pos.astype(F32)[:, None] * inv
    c, sn = jnp.cos(ang), jnp.sin(ang)
    s = pos.shape[0]
    cos_t = jnp.concatenate([jnp.ones((s, offset), F32), c, c,
                             jnp.ones((s, LANES - offset - real), F32)], axis=1)
    sin_t = jnp.concatenate([jnp.zeros((s, offset), F32), -sn, sn,
                             jnp.zeros((s, LANES - offset - real), F32)], axis=1)
    return cos_t, sin_t


def _rope(x, cos_t, sin_t, real, offset):
    half = real // 2
    lane = lax.broadcasted_iota(jnp.int32, x.shape, 1)
    partner = jnp.where(lane < offset + half, pltpu.roll(x, LANES - half, 1), pltpu.roll(x, half, 1))
    return x * cos_t + partner * sin_t


def _mla_prep(q_pre, kv_pre, p_even, cos_m, sin_m, qhn, khn, *, s, ts):
    w = MLA_H * LANES

    def fn(qp, kp, vp, kr, c, sn, gq, gk):
        qs, ks = [], []
        for h in range(MLA_H):
            sl = slice(h * LANES, (h + 1) * LANES)
            qn = _gn(qp[:, sl], gq, LANES, MLA_QK)[0]
            kn = _gn(kp[:, sl] + kr, gk, LANES, MLA_QK)[0]
            qs.append(_rope(qn, c, sn, MLA_ROPE, MLA_NOPE))
            ks.append(_rope(kn, c, sn, MLA_ROPE, MLA_NOPE))
        return jnp.concatenate(qs, axis=1), jnp.concatenate(ks, axis=1), vp

    rows = [_cols(q_pre, w, 0, ts), _cols(kv_pre, w, 0, ts), _cols(kv_pre, w, 1, ts),
            _cols(p_even, LANES, EV_KR_BLK, ts), _cols(cos_m, LANES, 0, ts), _cols(sin_m, LANES, 0, ts)]
    return _ew(fn, rows, [qhn, khn], [(w, BF16)] * 3, s=s, ts=ts, name="mla_prep")


def _mla_prep_bwd(q_pre, kv_pre, p_even, cos_m, sin_m, qhn, khn, dq, dk, *, s, ts):
    w = MLA_H * LANES

    def fn(qp, kp, kr, c, sn, dqv, dkv, gq, gk):
        dqs, dks = [], []
        dkr = jnp.zeros_like(kr)
        dgq = jnp.zeros((1, LANES), F32)
        dgk = jnp.zeros((1, LANES), F32)
        for h in range(MLA_H):
            sl = slice(h * LANES, (h + 1) * LANES)
            _, qn, qr = _gn(qp[:, sl], gq, LANES, MLA_QK)
            _, kn, krs = _gn(kp[:, sl] + kr, gk, LANES, MLA_QK)
            dqn = _rope(dqv[:, sl], c, -sn, MLA_ROPE, MLA_NOPE)
            dkn = _rope(dkv[:, sl], c, -sn, MLA_ROPE, MLA_NOPE)
            dqh, g1 = _gn_bwd(dqn, qn, qr, gq, LANES, MLA_QK)
            dkh, g2 = _gn_bwd(dkn, kn, krs, gk, LANES, MLA_QK)
            dqs.append(dqh)
            dks.append(dkh)
            dkr = dkr + dkh
            dgq = dgq + g1
            dgk = dgk + g2
        return jnp.concatenate(dqs, axis=1), jnp.concatenate(dks, axis=1), dkr, dgq, dgk

    rows = [_cols(q_pre, w, 0, ts), _cols(kv_pre, w, 0, ts), _cols(p_even, LANES, EV_KR_BLK, ts),
            _cols(cos_m, LANES, 0, ts), _cols(sin_m, LANES, 0, ts), _cols(dq, w, 0, ts), _cols(dk, w, 0, ts)]
    return _ew(fn, rows, [qhn, khn], [(w, BF16), (w, BF16), (LANES, BF16)], [(1, LANES), (1, LANES)],
               s=s, ts=ts, name="mla_prep_bwd")


def _flash_fwd(q, k, v, *, tq, tk):
    s = q.shape[0]
    nq, nk = s // tq, s // tk

    def body(q_ref, k_ref, v_ref, o_ref, lse_ref, m_s, l_s, acc):
        j = pl.program_id(2)

        @pl.when(j == 0)
        def _():
            m_s[...] = jnp.full_like(m_s, -jnp.inf)
            l_s[...] = jnp.zeros_like(l_s)
            acc[...] = jnp.zeros_like(acc)

        sc = _dot_nt(q_ref[...], k_ref[...]) * MLA_SCALE
        m_prev = m_s[...]
        m_new = jnp.maximum(m_prev, jnp.max(sc, axis=-1, keepdims=True))
        alpha = jnp.exp(m_prev - m_new)
        p = jnp.exp(sc - m_new)
        l_s[...] = alpha * l_s[...] + _lanesum(p)
        acc[...] = alpha * acc[...] + _dot(_bf(p), v_ref[...])
        m_s[...] = m_new

        @pl.when(j == nk - 1)
        def _():
            o_ref[...] = (acc[...] / l_s[...]).astype(o_ref.dtype)
            lse_ref[...] = m_s[...] + jnp.log(l_s[...])

    qs = pl.BlockSpec((tq, LANES), lambda h, i, j: (i, h))
    ks = pl.BlockSpec((tk, LANES), lambda h, i, j: (j, h))
    return pl.pallas_call(
        body, name="mla_flash_fwd", grid=(MLA_H, nq, nk),
        in_specs=[qs, ks, ks],
        out_specs=[qs, pl.BlockSpec((None, tq, 1), lambda h, i, j: (h, i, 0))],
        out_shape=[jax.ShapeDtypeStruct((s, MLA_H * LANES), BF16), jax.ShapeDtypeStruct((MLA_H, s, 1), F32)],
        scratch_shapes=[pltpu.VMEM((tq, 1), F32), pltpu.VMEM((tq, 1), F32), pltpu.VMEM((tq, LANES), F32)],
        compiler_params=_cp(("parallel", "parallel", "arbitrary")),
    )(q, k, v)


def _flash_bwd_dq(q, k, v, o, do, do_blk0, lse, *, tq, tk):
    s = q.shape[0]
    nq, nk = s // tq, s // tk

    def body(q_ref, k_ref, v_ref, o_ref, do_ref, lse_ref, dq_ref, dl_ref, acc, dl_s):
        j = pl.program_id(2)

        @pl.when(j == 0)
        def _():
            dl_s[...] = _lanesum(do_ref[...] * o_ref[...].astype(F32))
            acc[...] = jnp.zeros_like(acc)

        sc = _dot_nt(q_ref[...], k_ref[...]) * MLA_SCALE
        p = jnp.exp(sc - lse_ref[...])
        dp = _dot_nt(_bf(do_ref[...]), v_ref[...])
        ds = p * (dp - dl_s[...]) * MLA_SCALE
        acc[...] += _dot(_bf(ds), k_ref[...])

        @pl.when(j == nk - 1)
        def _():
            dq_ref[...] = acc[...]
            dl_ref[...] = dl_s[...]

    qs = pl.BlockSpec((tq, LANES), lambda h, i, j: (i, h))
    ks = pl.BlockSpec((tk, LANES), lambda h, i, j: (j, h))
    dos = pl.BlockSpec((tq, LANES), lambda h, i, j: (i, do_blk0 + h))
    st = pl.BlockSpec((None, tq, 1), lambda h, i, j: (h, i, 0))
    return pl.pallas_call(
        body, name="mla_flash_bwd_dq", grid=(MLA_H, nq, nk),
        in_specs=[qs, ks, ks, qs, dos, st],
        out_specs=[qs, st],
        out_shape=[jax.ShapeDtypeStruct((s, MLA_H * LANES), F32), jax.ShapeDtypeStruct((MLA_H, s, 1), F32)],
        scratch_shapes=[pltpu.VMEM((tq, LANES), F32), pltpu.VMEM((tq, 1), F32)],
        compiler_params=_cp(("parallel", "parallel", "arbitrary")),
    )(q, k, v, o, do, lse)


def _flash_bwd_dkv(q, k, v, do, do_blk0, lse, delta, *, tq, tk):
    s = q.shape[0]
    nq, nk = s // tq, s // tk

    def body(q_ref, k_ref, v_ref, do_ref, lse_ref, dl_ref, dk_ref, dv_ref, dk_acc, dv_acc):
        i = pl.program_id(2)

        @pl.when(i == 0)
        def _():
            dk_acc[...] = jnp.zeros_like(dk_acc)
            dv_acc[...] = jnp.zeros_like(dv_acc)

        dob = _bf(do_ref[...])
        sc = _dot_nt(q_ref[...], k_ref[...]) * MLA_SCALE
        p = jnp.exp(sc - lse_ref[...])
        dv_acc[...] += _dot_tn(_bf(p), dob)
        dp = _dot_nt(dob, v_ref[...])
        ds = p * (dp - dl_ref[...]) * MLA_SCALE
        dk_acc[...] += _dot_tn(_bf(ds), q_ref[...])

        @pl.when(i == nq - 1)
        def _():
            dk_ref[...] = dk_acc[...]
            dv_ref[...] = dv_acc[...].astype(dv_ref.dtype)

    qs = pl.BlockSpec((tq, LANES), lambda h, j, i: (i, h))
    ks = pl.BlockSpec((tk, LANES), lambda h, j, i: (j, h))
    dos = pl.BlockSpec((tq, LANES), lambda h, j, i: (i, do_blk0 + h))
    st = pl.BlockSpec((None, tq, 1), lambda h, j, i: (h, i, 0))
    return pl.pallas_call(
        body, name="mla_flash_bwd_dkv", grid=(MLA_H, nk, nq),
        in_specs=[qs, ks, ks, dos, st, st],
        out_specs=[ks, ks],
        out_shape=[jax.ShapeDtypeStruct((s, MLA_H * LANES), F32), jax.ShapeDtypeStruct((s, MLA_H * LANES), BF16)],
        scratch_shapes=[pltpu.VMEM((tk, LANES), F32), pltpu.VMEM((tk, LANES), F32)],
        compiler_params=_cp(("parallel", "parallel", "arbitrary")),
    )(q, k, v, do, lse, delta)


def _ret_geometry(d, c):
    df = d.astype(F32)
    ii = lax.broadcasted_iota(jnp.int32, (c, c), 0).astype(F32)
    jj = lax.broadcasted_iota(jnp.int32, (c, c), 1).astype(F32)
    rel = (ii - jj) * (1.0 - 2.0 * df)
    mask = rel >= df
    rel0 = jnp.maximum(rel, 0.0)
    pos = lax.broadcasted_iota(jnp.int32, (c, 1), 0).astype(F32)
    ez = (c - 1 - pos) + df * (2.0 * pos - (c - 1))
    ex = (pos + 1.0) + df * (c - 1 - 2.0 * pos)
    return mask, rel0, ez, ex


def _chunk_index(n_chunks):
    return lambda d, n: n + d * (n_chunks - 1 - 2 * n)


def _ret_fwd(p_even, cos_r, sin_r, theta_l):
    s = p_even.shape[0]
    c = RET_C
    n_chunks = s // c
    w = RET_H * LANES
    cidx = _chunk_index(n_chunks)

    def body(q_ref, k_ref, v_ref, cos_ref, sin_ref, th_ref, o_ref, rp_ref, r_s):
        d = pl.program_id(0)
        n = pl.program_id(1)

        @pl.when(n == 0)
        def _():
            r_s[...] = jnp.zeros_like(r_s)

        lg = jnp.log1p(-jnp.exp(-th_ref[...] * LN2))
        mask, rel0, ez, ex = _ret_geometry(d, c)
        cs, sn = cos_ref[...], sin_ref[...]
        rp_ref[...] = r_s[...]
        for h in range(RET_H):
            sl = slice(h * LANES, (h + 1) * LANES)
            lgh = lg[:, h * LANES:h * LANES + 1]
            dm = jnp.where(mask, jnp.exp(lgh * rel0), 0.0)
            qh = _bf(_rope(q_ref[:, sl], cs, sn, RET_DK, 0))
            kf = _rope(k_ref[:, sl], cs, sn, RET_DK, 0) * (RET_DK ** -0.5)
            kh = _bf(kf)
            vh = _bf(v_ref[:, sl])
            rh = r_s[sl, :]
            a = _dot_nt(qh, kh) * dm
            o_ref[:, sl] = _dot(_bf(a), vh) + jnp.exp(lgh * ex) * _dot(qh, _bf(rh))
            zk = _bf(kf * jnp.exp(lgh * ez))
            r_s[sl, :] = jnp.exp(lgh * c) * rh + _dot_tn(zk, vh)

    def col(blk):
        return pl.BlockSpec((c, w), lambda d, n: (cidx(d, n), blk))

    tab = pl.BlockSpec((c, LANES), lambda d, n: (cidx(d, n), 0))
    return pl.pallas_call(
        body, name="ret_fwd", grid=(2, n_chunks),
        in_specs=[col(0), col(1), col(2), tab, tab, pl.BlockSpec((None, 1, w), lambda d, n: (d, 0, 0))],
        out_specs=[pl.BlockSpec((None, c, w), lambda d, n: (d, cidx(d, n), 0)),
                   pl.BlockSpec((None, None, w, LANES), lambda d, n: (d, cidx(d, n), 0, 0))],
        out_shape=[jax.ShapeDtypeStruct((2, s, w), F32), jax.ShapeDtypeStruct((2, n_chunks, w, LANES), F32)],
        scratch_shapes=[pltpu.VMEM((w, LANES), F32)],
        compiler_params=_cp(("arbitrary", "arbitrary")),
    )(p_even, p_even, p_even, cos_r, sin_r, theta_l)


def _ret_bwd(p_even, cos_r, sin_r, theta_l, theta_h, r_prev, do):
    s = p_even.shape[0]
    c = RET_C
    n_chunks = s // c
    w = RET_H * LANES
    fwd_idx = _chunk_index(n_chunks)

    def cidx(d, n):
        return fwd_idx(d, n_chunks - 1 - n)

    def body(q_ref, k_ref, v_ref, cos_ref, sin_ref, th_ref, thh_ref, rp_ref, do_ref,
             dq_ref, dk_ref, dv_ref, dth_ref, dr_s):
        d = pl.program_id(0)
        n = pl.program_id(1)

        @pl.when(n == 0)
        def _():
            dr_s[...] = jnp.zeros_like(dr_s)
            dth_ref[...] = jnp.zeros_like(dth_ref)

        lg = jnp.log1p(-jnp.exp(-th_ref[...] * LN2))
        mask, rel0, ez, ex = _ret_geometry(d, c)
        cs, sn = cos_ref[...], sin_ref[...]
        row = lax.broadcasted_iota(jnp.int32, (RET_H, LANES), 0)
        dlg = jnp.zeros((RET_H, LANES), F32)
        kscale = RET_DK ** -0.5
        for h in range(RET_H):
            sl = slice(h * LANES, (h + 1) * LANES)
            lgh = lg[:, h * LANES:h * LANES + 1]
            dm = jnp.where(mask, jnp.exp(lgh * rel0), 0.0)
            zeta = jnp.exp(lgh * ez)
            xi = jnp.exp(lgh * ex)
            gc = jnp.exp(lgh * c)
            qf = _rope(q_ref[:, sl], cs, sn, RET_DK, 0)
            qh = _bf(qf)
            kf = _rope(k_ref[:, sl], cs, sn, RET_DK, 0) * kscale
            kh = _bf(kf)
            zkf = kf * zeta
            zk = _bf(zkf)
            vh = _bf(v_ref[:, sl])
            dof = do_ref[:, sl]
            doh = _bf(dof)
            rp = rp_ref[sl, :]
            rpb = _bf(rp)
            drn = dr_s[sl, :]
            drb = _bf(drn)
            a = _dot_nt(qh, kh) * dm
            da0 = _dot_nt(doh, vh)
            da = _bf(da0 * dm)
            vdr = _dot_nt(vh, drb)
            dq_r = _dot(da, kh) + xi * _dot_nt(doh, rpb)
            dk_r = _dot_tn(da, qh) + zeta * vdr
            dv_ref[:, sl] = _dot_tn(_bf(a), doh) + _dot(zk, drb)
            dq_ref[:, sl] = _rope(dq_r, cs, -sn, RET_DK, 0)
            dk_ref[:, sl] = _rope(dk_r * kscale, cs, -sn, RET_DK, 0)
            dr_s[sl, :] = _dot_tn(_bf(qf * xi), doh) + gc * drn
            ocross = xi * _dot(qh, rpb)
            t = (jnp.sum(rel0 * a * da0, keepdims=True)
                 + jnp.sum(ex * dof * ocross, keepdims=True)
                 + c * gc * jnp.sum(drn * rp, keepdims=True)
                 + jnp.sum(ez * zkf * vdr, keepdims=True))
            dlg = jnp.where(row == h, t, dlg)
        x2 = jnp.exp(-thh_ref[...] * LN2)
        dth_ref[...] += dlg * (x2 * LN2 / (1.0 - x2))

    def col(blk):
        return pl.BlockSpec((c, w), lambda d, n: (cidx(d, n), blk))

    tab = pl.BlockSpec((c, LANES), lambda d, n: (cidx(d, n), 0))
    dirrow = pl.BlockSpec((None, c, w), lambda d, n: (d, cidx(d, n), 0))
    hrow = pl.BlockSpec((None, RET_H, LANES), lambda d, n: (d, 0, 0))
    return pl.pallas_call(
        body, name="ret_bwd", grid=(2, n_chunks),
        in_specs=[col(0), col(1), col(2), tab, tab, pl.BlockSpec((None, 1, w), lambda d, n: (d, 0, 0)), hrow,
                  pl.BlockSpec((None, None, w, LANES), lambda d, n: (d, cidx(d, n), 0, 0)),
                  pl.BlockSpec((c, w), lambda d, n: (cidx(d, n), 0))],
        out_specs=[dirrow, dirrow, dirrow, hrow],
        out_shape=[jax.ShapeDtypeStruct((2, s, w), F32)] * 3 + [jax.ShapeDtypeStruct((2, RET_H, LANES), F32)],
        scratch_shapes=[pltpu.VMEM((w, LANES), F32)],
        compiler_params=_cp(("arbitrary", "arbitrary")),
    )(p_even, p_even, p_even, cos_r, sin_r, theta_l, theta_h, r_prev, do)


def _post_fwd(o2, gate_row, gain, *, group, n, s, ts, name):
    w = o2.shape[2]

    def fn(of, ob, g, gv):
        y = _gn(of + ob, gv, group, n)[0]
        return g * _sigmoid(g) * y

    return _ew(fn, [_lead(o2, 0, ts), _lead(o2, 1, ts), gate_row], [gain], [(w, BF16)], s=s, ts=ts, name=name)[0]


def _post_bwd(o2, gate_row, gain, dr_row, *, group, n, s, ts, name):
    w = o2.shape[2]

    def fn(of, ob, g, dr, gv):
        y, xn, rstd = _gn(of + ob, gv, group, n)
        sg = _sigmoid(g)
        dy = dr * (g * sg)
        dgate = dr * y * (sg * (1.0 + g * (1.0 - sg)))
        do, dgain = _gn_bwd(dy, xn, rstd, gv, group, n)
        return do, dgate, dgain

    return _ew(fn, [_lead(o2, 0, ts), _lead(o2, 1, ts), gate_row, dr_row], [gain],
               [(w, F32), (w, BF16)], [(1, w)], s=s, ts=ts, name=name)


def _sum2(a2, *, s, ts, name):
    w = a2.shape[2]
    return _ew(lambda a, b: a + b, [_lead(a2, 0, ts), _lead(a2, 1, ts)], [], [(w, BF16)], s=s, ts=ts, name=name)[0]


def _gla_common(d, q_ref, k_ref, ga_ref, wg_ref, bg_ref):
    c = GLA_C
    df = d.astype(F32)
    ii = lax.broadcasted_iota(jnp.int32, (c, c), 0).astype(F32)
    jj = lax.broadcasted_iota(jnp.int32, (c, c), 1).astype(F32)
    rel = (ii - jj) * (1.0 - 2.0 * df)
    tri = _bf(jnp.where(rel >= 0.0, 1.0, 0.0))
    mask = rel >= df
    gab = _bf(ga_ref[...])
    z = _dot(gab, wg_ref[...]) + bg_ref[...]
    la = (jnp.minimum(z, 0.0) - jnp.log1p(jnp.exp(-jnp.abs(z)))) * (1.0 / GLA_TAU)
    l1, l2, l3 = _split3(la)
    b = _dot(tri, l1) + _dot(tri, l2) + _dot(tri, l3)
    first = d == 0
    bm = jnp.where(first, b[c // 2:c // 2 + 1], b[c // 2 - 1:c // 2])
    bl = jnp.where(first, b[c - 1:c], b[0:1])
    q = q_ref[...] * (GLA_DK ** -0.5)
    k = k_ref[...]
    e1, e2, e3, eb = jnp.exp(b - bm), jnp.exp(bm - b), jnp.exp(bl - b), jnp.exp(b)
    return dict(tri=tri, mask=mask, gab=gab, z=z, ebl=jnp.exp(bl), e1=e1, e2=e2, e3=e3, eb=eb,
                qc=q * e1, kc=k * e2, kd=k * e3, qe=q * eb, first=first)


def _col_scale(row_vec, width):
    t = jnp.broadcast_to(row_vec, (LANES, LANES)).T
    return jnp.concatenate([t] * (width // LANES), axis=1)


def _gla_fwd(p_odd, wg2, bg2):
    s = p_odd.shape[0]
    c = GLA_C
    n_chunks = s // c
    wk, wv = GLA_H * GLA_DK, GLA_H * GLA_DV
    cidx = _chunk_index(n_chunks)

    def body(q_ref, k_ref, v_ref, ga_ref, wg_ref, bg_ref, o_ref, sp_ref, s_s):
        d = pl.program_id(0)
        n = pl.program_id(1)

        @pl.when(n == 0)
        def _():
            s_s[...] = jnp.zeros_like(s_s)

        g = _gla_common(d, q_ref, k_ref, ga_ref, wg_ref, bg_ref)
        sp_ref[...] = s_s[...]
        for h in range(GLA_H):
            sl = slice(h * GLA_DK, (h + 1) * GLA_DK)
            vs = slice(h * GLA_DV, (h + 1) * GLA_DV)
            vh = _bf(v_ref[:, vs])
            sh = s_s[sl, :]
            a = jnp.where(g["mask"], _dot_nt(_bf(g["qc"][:, sl]), _bf(g["kc"][:, sl])), 0.0)
            o_ref[:, vs] = _dot(_bf(a), vh) + _dot(_bf(g["qe"][:, sl]), _bf(sh))
            s_s[sl, :] = _col_scale(g["ebl"][:, sl], GLA_DV) * sh + _dot_tn(_bf(g["kd"][:, sl]), vh)

    def col(width, blk):
        return pl.BlockSpec((c, width), lambda d, n: (cidx(d, n), blk))

    return pl.pallas_call(
        body, name="gla_fwd", grid=(2, n_chunks),
        in_specs=[col(wk, 0), col(wk, 1), col(wv, 1), col(LANES, OD_GA_BLK),
                  pl.BlockSpec((None, LANES, wk), lambda d, n: (d, 0, 0)),
                  pl.BlockSpec((None, 1, wk), lambda d, n: (d, 0, 0))],
        out_specs=[pl.BlockSpec((None, c, wv), lambda d, n: (d, cidx(d, n), 0)),
                   pl.BlockSpec((None, None, wk, GLA_DV), lambda d, n: (d, cidx(d, n), 0, 0))],
        out_shape=[jax.ShapeDtypeStruct((2, s, wv), F32), jax.ShapeDtypeStruct((2, n_chunks, wk, GLA_DV), F32)],
        scratch_shapes=[pltpu.VMEM((wk, GLA_DV), F32)],
        compiler_params=_cp(("arbitrary", "arbitrary")),
    )(p_odd, p_odd, p_odd, p_odd, wg2, bg2)


def _gla_bwd(p_odd, wg2, bg2, s_prev, do):
    s = p_odd.shape[0]
    c = GLA_C
    n_chunks = s // c
    wk, wv = GLA_H * GLA_DK, GLA_H * GLA_DV
    fwd_idx = _chunk_index(n_chunks)

    def cidx(d, n):
        return fwd_idx(d, n_chunks - 1 - n)

    def body(q_ref, k_ref, v_ref, ga_ref, wg_ref, bg_ref, sp_ref, do_ref,
             dq_ref, dk_ref, dv_ref, dga_ref, dwg_ref, dbg_ref, ds_s):
        d = pl.program_id(0)
        n = pl.program_id(1)

        @pl.when(n == 0)
        def _():
            ds_s[...] = jnp.zeros_like(ds_s)
            dwg_ref[...] = jnp.zeros_like(dwg_ref)
            dbg_ref[...] = jnp.zeros_like(dbg_ref)

        g = _gla_common(d, q_ref, k_ref, ga_ref, wg_ref, bg_ref)
        mask = g["mask"]
        ones8 = jnp.ones((8, GLA_DV), BF16)
        dbs, dbms, dbls = [], [], []
        for h in range(GLA_H):
            sl = slice(h * GLA_DK, (h + 1) * GLA_DK)
            vs = slice(h * GLA_DV, (h + 1) * GLA_DV)
            qc, kc, kd, qe = g["qc"][:, sl], g["kc"][:, sl], g["kd"][:, sl], g["qe"][:, sl]
            qcb, kcb, kdb, qeb = _bf(qc), _bf(kc), _bf(kd), _bf(qe)
            vh = _bf(v_ref[:, vs])
            doh = _bf(do_ref[:, vs])
            sp = sp_ref[sl, :]
            dsn = ds_s[sl, :]
            dsb = _bf(dsn)
            a = _bf(jnp.where(mask, _dot_nt(qcb, kcb), 0.0))
            da = _bf(jnp.where(mask, _dot_nt(doh, vh), 0.0))
            dv_ref[:, vs] = _dot_tn(a, doh) + _dot(kdb, dsb)
            dqc = _dot(da, kcb)
            dkc = _dot_tn(da, qcb)
            dqe = _dot_nt(doh, _bf(sp))
            dkd = _dot_nt(vh, dsb)
            ds_s[sl, :] = _dot_tn(qeb, doh) + _col_scale(g["ebl"][:, sl], GLA_DV) * dsn
            dq_ref[:, sl] = (dqc * g["e1"][:, sl] + dqe * g["eb"][:, sl]) * (GLA_DK ** -0.5)
            dk_ref[:, sl] = dkc * g["e2"][:, sl] + dkd * g["e3"][:, sl]
            t1, t2, t3, t4 = dqc * qc, dkc * kc, dqe * qe, dkd * kd
            dbs.append(t1 - t2 + t3 - t4)
            dbms.append(_rowsum(t2 - t1))
            m1, m2, _ = _split3(dsn * sp)
            rs = (_dot_nt(ones8, m1) + _dot_nt(ones8, m2))[0:1]
            dbls.append(_rowsum(t4) + g["ebl"][:, sl] * rs)
        db = jnp.concatenate(dbs, axis=1)
        dbm = jnp.concatenate(dbms, axis=1)
        dbl = jnp.concatenate(dbls, axis=1)
        row = lax.broadcasted_iota(jnp.int32, (c, wk), 0)
        mid = jnp.where(g["first"], c // 2, c // 2 - 1)
        last = jnp.where(g["first"], c - 1, 0)
        db = db + jnp.where(row == mid, dbm, 0.0) + jnp.where(row == last, dbl, 0.0)
        d1, d2, d3 = _split3(db)
        tri = g["tri"]
        dla = _dot_tn(tri, d1) + _dot_tn(tri, d2) + _dot_tn(tri, d3)
        dz = dla * (1.0 / GLA_TAU) * (1.0 - _sigmoid(g["z"]))
        dzb = _bf(dz)
        dga_ref[...] = _dot_nt(dzb, wg_ref[...])
        dwg_ref[...] += _dot_tn(g["gab"], dzb)
        dbg_ref[...] += _rowsum(dz)

    def col(width, blk):
        return pl.BlockSpec((c, width), lambda d, n: (cidx(d, n), blk))

    def dirrow(width):
        return pl.BlockSpec((None, c, width), lambda d, n: (d, cidx(d, n), 0))

    return pl.pallas_call(
        body, name="gla_bwd", grid=(2, n_chunks),
        in_specs=[col(wk, 0), col(wk, 1), col(wv, 1), col(LANES, OD_GA_BLK),
                  pl.BlockSpec((None, LANES, wk), lambda d, n: (d, 0, 0)),
                  pl.BlockSpec((None, 1, wk), lambda d, n: (d, 0, 0)),
                  pl.BlockSpec((None, None, wk, GLA_DV), lambda d, n: (d, cidx(d, n), 0, 0)),
                  pl.BlockSpec((c, wv), lambda d, n: (cidx(d, n), 0))],
        out_specs=[dirrow(wk), dirrow(wk), dirrow(wv), dirrow(LANES),
                   pl.BlockSpec((None, LANES, wk), lambda d, n: (d, 0, 0)),
                   pl.BlockSpec((None, 1, wk), lambda d, n: (d, 0, 0))],
        out_shape=[jax.ShapeDtypeStruct((2, s, wk), F32), jax.ShapeDtypeStruct((2, s, wk), F32),
                   jax.ShapeDtypeStruct((2, s, wv), F32), jax.ShapeDtypeStruct((2, s, LANES), F32),
                   jax.ShapeDtypeStruct((2, LANES, wk), F32), jax.ShapeDtypeStruct((2, 1, wk), F32)],
        scratch_shapes=[pltpu.VMEM((wk, GLA_DV), F32)],
        compiler_params=_cp(("arbitrary", "arbitrary")),
    )(p_odd, p_odd, p_odd, p_odd, wg2, bg2, s_prev, do)


HALO = 8


def _halo_specs(width_blk, col0, ts, s):
    r = ts // HALO
    last = s // HALO - 1
    cur = pl.BlockSpec((ts, width_blk), lambda j, i: (i, col0 + j))
    prev = pl.BlockSpec((HALO, width_blk), lambda j, i: (jnp.maximum(i * r - 1, 0), col0 + j))
    nxt = pl.BlockSpec((HALO, width_blk), lambda j, i: (jnp.minimum((i + 1) * r, last), col0 + j))
    return [prev, cur, nxt]


def _with_halo(prev_ref, cur_ref, next_ref, i, n_i):
    p = jnp.where(i == 0, 0.0, prev_ref[...])
    q = jnp.where(i == n_i - 1, 0.0, next_ref[...])
    return jnp.concatenate([p, cur_ref[...], q], axis=0)


def _shift_down(x):
    return pltpu.roll(x, 1, 0)


def _shift_up(x):
    return pltpu.roll(x, x.shape[0] - 1, 0)


def _ffn_act(up, conv_w, conv_b, *, ts):
    s = up.shape[0]
    tc = _tile(D_FF, 1408)
    nj = D_FF // tc
    n_i = s // ts

    def body(gp, gc, gn, val_ref, w_ref, b_ref, a_ref):
        i = pl.program_id(1)
        g = _with_halo(gp, gc, gn, i, n_i)
        w = w_ref[...]
        conv = w[0:1] * _shift_down(g) + w[1:2] * g + w[2:3] * _shift_up(g) + b_ref[...]
        conv = conv[HALO:HALO + ts]
        a_ref[...] = (conv * _sigmoid(conv) * val_ref[...]).astype(a_ref.dtype)

    return pl.pallas_call(
        body, name="ffn_act", grid=(nj, n_i),
        in_specs=_halo_specs(tc, 0, ts, s) + [pl.BlockSpec((ts, tc), lambda j, i: (i, nj + j)),
                                              pl.BlockSpec((3, tc), lambda j, i: (0, j)),
                                              pl.BlockSpec((1, tc), lambda j, i: (0, j))],
        out_specs=pl.BlockSpec((ts, tc), lambda j, i: (i, j)),
        out_shape=jax.ShapeDtypeStruct((s, D_FF), BF16),
        compiler_params=_cp(("parallel", "arbitrary")),
    )(up, up, up, up, conv_w, conv_b)


def _ffn_act_bwd(up, da, conv_w, conv_b, *, ts):
    s = up.shape[0]
    tc = _tile(D_FF, 1408)
    nj = D_FF // tc
    n_i = s // ts

    def body(gp, gc, gn, vp, vc, vn, dp, dc, dn, w_ref, b_ref, dg_ref, dval_ref, dw_ref, db_ref):
        i = pl.program_id(1)
        g = _with_halo(gp, gc, gn, i, n_i)
        v = _with_halo(vp, vc, vn, i, n_i)
        dav = _with_halo(dp, dc, dn, i, n_i)
        w = w_ref[...]
        gm, gpl = _shift_down(g), _shift_up(g)
        conv = w[0:1] * gm + w[1:2] * g + w[2:3] * gpl + b_ref[...]
        sg = _sigmoid(conv)
        dgc = dav * v * (sg * (1.0 + conv * (1.0 - sg)))
        dgate = w[0:1] * _shift_up(dgc) + w[1:2] * dgc + w[2:3] * _shift_down(dgc)
        ctr = slice(HALO, HALO + ts)
        dg_ref[...] = dgate[ctr].astype(dg_ref.dtype)
        dval_ref[...] = (dav[ctr] * (conv * sg)[ctr]).astype(dval_ref.dtype)
        dgc_c = dgc[ctr]
        dw = jnp.concatenate([_rowsum(dgc_c * gm[ctr]), _rowsum(dgc_c * g[ctr]), _rowsum(dgc_c * gpl[ctr])], axis=0)
        dbv = _rowsum(dgc_c)

        @pl.when(i == 0)
        def _():
            dw_ref[...] = dw
            db_ref[...] = dbv

        @pl.when(i > 0)
        def _():
            dw_ref[...] += dw
            db_ref[...] += dbv

    tile = pl.BlockSpec((ts, tc), lambda j, i: (i, j))
    return pl.pallas_call(
        body, name="ffn_act_bwd", grid=(nj, n_i),
        in_specs=(_halo_specs(tc, 0, ts, s) + _halo_specs(tc, nj, ts, s) + _halo_specs(tc, 0, ts, s)
                  + [pl.BlockSpec((3, tc), lambda j, i: (0, j)), pl.BlockSpec((1, tc), lambda j, i: (0, j))]),
        out_specs=[tile, tile, pl.BlockSpec((3, tc), lambda j, i: (0, j)), pl.BlockSpec((1, tc), lambda j, i: (0, j))],
        out_shape=[jax.ShapeDtypeStruct((s, D_FF), BF16), jax.ShapeDtypeStruct((s, D_FF), BF16),
                   jax.ShapeDtypeStruct((3, D_FF), F32), jax.ShapeDtypeStruct((1, D_FF), F32)],
        compiler_params=_cp(("parallel", "arbitrary")),
    )(up, up, up, up, up, up, da, da, da, conv_w, conv_b)


def _loss_head(y, target, *, s, ts):
    def fn(yv, tv):
        err = yv - tv
        return err * (1.0 / D_MODEL), _rowsum(err * err)

    return _ew(fn, [_cols(y, D_MODEL, 0, ts), _cols(target, D_MODEL, 0, ts)], [], [(D_MODEL, F32)],
               [(1, D_MODEL)], s=s, ts=ts, name="loss_head")


def _adamw(w, g, m, v, *, name):
    r = w.shape[0]
    ts = r if r <= 512 else 512
    assert r % ts == 0

    def fn(wv, gv, mv, vv):
        mn = ADAM_B1 * mv + (1.0 - ADAM_B1) * gv
        vn = ADAM_B2 * vv + (1.0 - ADAM_B2) * (gv * gv)
        m_hat = mn / (1.0 - ADAM_B1 ** ADAM_STEP)
        v_hat = vn / (1.0 - ADAM_B2 ** ADAM_STEP)
        delta = -ADAM_LR * (m_hat / (jnp.sqrt(v_hat) + ADAM_EPS) + ADAM_WD * wv)
        return delta, mn, vn

    rows = [_cols(a, LANES * 8, 0, ts) for a in (w, g, m, v)]
    return _ew(fn, rows, [], [(LANES * 8, F32)] * 3, s=r, ts=ts, name=name)


def _pad_heads(w, heads, real):
    lead = w.shape[:-1]
    w = w.reshape(lead + (heads, real))
    w = jnp.pad(w, [(0, 0)] * len(lead) + [(0, 0), (0, LANES - real)])
    return w.reshape(lead + (heads * LANES,))


def _pad_head_rows(w, heads, real):
    return _pad_heads(w.T, heads, real).T


def _pack_even(p):
    w_in = p["w_in"]
    z = lambda n: jnp.zeros((D_MODEL, n), w_in.dtype)
    o = 0
    parts = {}
    for nm, n in (("cq", MLA_QR), ("ckv", MLA_KVR), ("kr", MLA_ROPE), ("rq", 512), ("rk", 512), ("rv", 512), ("rg", 512)):
        parts[nm] = w_in[:, o:o + n]
        o += n
    w_in_p = jnp.concatenate(
        [_pad_heads(parts[k], RET_H, RET_DK) for k in ("rq", "rk", "rv", "rg")]
        + [parts["cq"], z(EV_CQ - MLA_QR), parts["ckv"], z(MLA_NOPE), parts["kr"], z(LANES - MLA_QK), z(LANES)], axis=1)
    w_uq = jnp.pad(_pad_heads(p["w_uq"], MLA_H, MLA_QK), ((0, EV_CQ - MLA_QR), (0, 0)))
    ukv = p["w_ukv"].reshape(MLA_KVR, MLA_H, MLA_NOPE + MLA_V)
    w_ukv = jnp.concatenate([_pad_heads(ukv[..., :MLA_NOPE].reshape(MLA_KVR, -1), MLA_H, MLA_NOPE),
                             _pad_heads(ukv[..., MLA_NOPE:].reshape(MLA_KVR, -1), MLA_H, MLA_V)], axis=1)
    w_out = jnp.concatenate([_pad_head_rows(p["w_out"][:MLA_H * MLA_V], MLA_H, MLA_V),
                             _pad_head_rows(p["w_out"][MLA_H * MLA_V:], RET_H, RET_DV)], axis=0)
    return dict(
        w_in=w_in_p, w_uq=w_uq, w_ukv=w_ukv, w_out=w_out,
        mix_g=p["mix_norm"][None, :],
        q_norm=jnp.pad(p["q_norm"], (0, EV_CQ - MLA_QR))[None, :],
        kv_norm=p["kv_norm"][None, :],
        qhn=jnp.pad(p["q_head_norm"], (0, LANES - MLA_QK))[None, :],
        khn=jnp.pad(p["k_head_norm"], (0, LANES - MLA_QK))[None, :],
        ret_gain=_pad_heads(p["ret_out_norm"].reshape(-1), RET_H, RET_DV)[None, :],
    )


def _pack_odd(p):
    w_in = p["w_in"]
    ga = w_in[:, 3072:]
    w_in_p = jnp.concatenate([w_in[:, :3072], ga, jnp.zeros((D_MODEL, LANES - 2 * GLA_R), w_in.dtype)], axis=1)
    wk = GLA_H * GLA_DK
    zf = jnp.zeros((LANES - GLA_R, wk), p["w_gate_fwd"].dtype)
    zb0 = jnp.zeros((GLA_R, wk), p["w_gate_fwd"].dtype)
    zb1 = jnp.zeros((LANES - 2 * GLA_R, wk), p["w_gate_fwd"].dtype)
    wg2 = jnp.stack([jnp.concatenate([p["w_gate_fwd"], zf], axis=0),
                     jnp.concatenate([zb0, p["w_gate_bwd"], zb1], axis=0)])
    bg2 = jnp.stack([p["b_gate_fwd"][None, :], p["b_gate_bwd"][None, :]])
    return dict(w_in=w_in_p, wg2=wg2, bg2=bg2, w_out=p["w_out"], mix_g=p["mix_norm"][None, :],
                gla_gain=p["gla_out_norm"].reshape(1, -1))


def _pack_ffn(p):
    return dict(w_up=p["w_up"], w_down=p["w_down"], conv_w=p["conv_w"], conv_b=p["conv_b"][None, :],
                norm_g=p["norm"][None, :])


_MATRICES = ("w_in", "w_uq", "w_ukv", "w_out", "wg2", "w_up", "w_down")


def _packed(pack_fn, p):
    packed = pack_fn(p)
    packed = {k: (_bf(v) if k in _MATRICES else v.astype(F32)) for k, v in packed.items()}
    shapes = {k: jax.ShapeDtypeStruct(v.shape, F32) for k, v in p.items()}
    unpack = jax.linear_transpose(pack_fn, shapes)
    return packed, lambda g: unpack(g)[0]


def _ffn_fwd(x, w, *, s, ts):
    h = _rmsnorm(_cols(x, D_MODEL, 0, ts), w["norm_g"], n=D_MODEL, s=s, ts=ts, name="ffn_norm")
    up = _mm(h, w["w_up"], name="ffn_up")
    a = _ffn_act(up, w["conv_w"], w["conv_b"], ts=ts)
    y = _mm(a, w["w_down"], res=x, name="ffn_down")
    return y, dict(x=x, h=h, up=up, a=a)


def _ffn_bwd(dy, w, sv, *, s, ts):
    da = _mm(dy, w["w_down"], tb=True, name="ffn_down_dx")
    g_down = _mm(sv["a"], dy, ta=True, name="ffn_down_dw")
    dgate, dval, g_cw, g_cb = _ffn_act_bwd(sv["up"], da, w["conv_w"], w["conv_b"], ts=ts)
    dup = jnp.concatenate([dgate, dval], axis=1)
    dh = _mm(dup, w["w_up"], tb=True, name="ffn_up_dx")
    g_up = _mm(sv["h"], dup, ta=True, name="ffn_up_dw")
    dx, g_norm = _rmsnorm_bwd(_cols(sv["x"], D_MODEL, 0, ts), w["norm_g"], dh, dy, n=D_MODEL, s=s, ts=ts,
                              name="ffn_norm_bwd")
    return dx, dict(w_up=g_up, w_down=g_down, conv_w=g_cw, conv_b=g_cb, norm_g=g_norm)


def _flash_tiles(s):
    return min(s, 512), min(s, 1024)


def _even_fwd(x, w, tabs, *, s, ts):
    cos_m, sin_m, cos_r, sin_r = tabs
    h = _rmsnorm(_cols(x, D_MODEL, 0, ts), w["mix_g"], n=D_MODEL, s=s, ts=ts, name="mix_norm")
    p = _mm(h, w["w_in"], name="even_in")
    cqn = _rmsnorm(_cols(p, EV_CQ, EV_RET // EV_CQ, ts), w["q_norm"], n=MLA_QR, s=s, ts=ts, name="mla_q_norm")
    ckvn = _rmsnorm(_cols(p, MLA_KVR, (EV_RET + EV_CQ) // MLA_KVR, ts), w["kv_norm"], n=MLA_KVR, s=s, ts=ts,
                    name="mla_kv_norm")
    q_pre = _mm(cqn, w["w_uq"], name="mla_uq")
    kv_pre = _mm(ckvn, w["w_ukv"], name="mla_ukv")
    q, k, v = _mla_prep(q_pre, kv_pre, p, cos_m, sin_m, w["qhn"], w["khn"], s=s, ts=ts)
    tq, tk = _flash_tiles(s)
    o, lse = _flash_fwd(q, k, v, tq=tq, tk=tk)
    o2, r_prev = _ret_fwd(p, cos_r, sin_r, w["theta_l"])
    r = _post_fwd(o2, _cols(p, RET_H * LANES, 3, ts), w["ret_gain"], group=LANES, n=RET_DV, s=s, ts=ts,
                  name="ret_post")
    ar = jnp.concatenate([o, r], axis=1)
    y = _mm(ar, w["w_out"], res=x, name="even_out")
    return y, dict(x=x, h=h, p=p, cqn=cqn, ckvn=ckvn, q_pre=q_pre, kv_pre=kv_pre, q=q, k=k, v=v, o=o, lse=lse,
                   o2=o2, r_prev=r_prev, ar=ar)


def _even_bwd(dy, w, sv, tabs, *, s, ts):
    cos_m, sin_m, cos_r, sin_r = tabs
    p = sv["p"]
    wh = MLA_H * LANES
    dar = _mm(dy, w["w_out"], tb=True, name="even_out_dx")
    g_out = _mm(sv["ar"], dy, ta=True, name="even_out_dw")
    tq, tk = _flash_tiles(s)
    dq, delta = _flash_bwd_dq(sv["q"], sv["k"], sv["v"], sv["o"], dar, 0, sv["lse"], tq=tq, tk=tk)
    dk, dv = _flash_bwd_dkv(sv["q"], sv["k"], sv["v"], dar, 0, sv["lse"], delta, tq=tq, tk=tk)
    dq_pre, dk_pre, dkr, g_qhn, g_khn = _mla_prep_bwd(sv["q_pre"], sv["kv_pre"], p, cos_m, sin_m, w["qhn"], w["khn"],
                                                      dq, dk, s=s, ts=ts)
    dkv_pre = jnp.concatenate([dk_pre, dv], axis=1)
    dckvn = _mm(dkv_pre, w["w_ukv"], tb=True, name="mla_ukv_dx")
    g_ukv = _mm(sv["ckvn"], dkv_pre, ta=True, name="mla_ukv_dw")
    dcqn = _mm(dq_pre, w["w_uq"], tb=True, name="mla_uq_dx")
    g_uq = _mm(sv["cqn"], dq_pre, ta=True, name="mla_uq_dw")
    dckv, g_kvn = _rmsnorm_bwd(_cols(p, MLA_KVR, (EV_RET + EV_CQ) // MLA_KVR, ts), w["kv_norm"], dckvn, None,
                               n=MLA_KVR, s=s, ts=ts, name="mla_kv_norm_bwd")
    dcq, g_qn = _rmsnorm_bwd(_cols(p, EV_CQ, EV_RET // EV_CQ, ts), w["q_norm"], dcqn, None, n=MLA_QR, s=s, ts=ts,
                             name="mla_q_norm_bwd")
    do, drg, g_gain = _post_bwd(sv["o2"], _cols(p, wh, 3, ts), w["ret_gain"], _cols(dar, wh, 1, ts),
                                group=LANES, n=RET_DV, s=s, ts=ts, name="ret_post_bwd")
    dq2, dk2, dv2, dth = _ret_bwd(p, cos_r, sin_r, w["theta_l"], w["theta_h"], sv["r_prev"], do)
    drq, drk, drv = (_sum2(a, s=s, ts=ts, name="sum_dirs_1024") for a in (dq2, dk2, dv2))
    dp = jnp.concatenate([drq, drk, drv, drg, _bf(dcq), _bf(dckv), dkr, jnp.zeros((s, LANES), BF16)], axis=1)
    dh = _mm(dp, w["w_in"], tb=True, name="even_in_dx")
    g_in = _mm(sv["h"], dp, ta=True, name="even_in_dw")
    dx, g_mix = _rmsnorm_bwd(_cols(sv["x"], D_MODEL, 0, ts), w["mix_g"], dh, dy, n=D_MODEL, s=s, ts=ts,
                             name="mix_norm_bwd")
    grads = dict(w_in=g_in, w_uq=g_uq, w_ukv=g_ukv, w_out=g_out, mix_g=g_mix, q_norm=g_qn, kv_norm=g_kvn,
                 qhn=g_qhn, khn=g_khn, ret_gain=g_gain)
    return dx, grads, dth[:, :, 0]


def _odd_fwd(x, w, *, s, ts):
    h = _rmsnorm(_cols(x, D_MODEL, 0, ts), w["mix_g"], n=D_MODEL, s=s, ts=ts, name="mix_norm")
    p = _mm(h, w["w_in"], name="odd_in")
    o2, s_prev = _gla_fwd(p, w["wg2"], w["bg2"])
    g = _post_fwd(o2, _cols(p, GLA_H * GLA_DV, 2, ts), w["gla_gain"], group=GLA_DV, n=GLA_DV, s=s, ts=ts,
                  name="gla_post")
    y = _mm(g, w["w_out"], res=x, name="odd_out")
    return y, dict(x=x, h=h, p=p, o2=o2, s_prev=s_prev, g=g)


def _odd_bwd(dy, w, sv, *, s, ts):
    p = sv["p"]
    wv = GLA_H * GLA_DV
    dg = _mm(dy, w["w_out"], tb=True, name="odd_out_dx")
    g_out = _mm(sv["g"], dy, ta=True, name="odd_out_dw")
    do, dgr, g_gain = _post_bwd(sv["o2"], _cols(p, wv, 2, ts), w["gla_gain"], _cols(dg, wv, 0, ts),
                                group=GLA_DV, n=GLA_DV, s=s, ts=ts, name="gla_post_bwd")
    dq2, dk2, dv2, dga2, g_wg, g_bg = _gla_bwd(p, w["wg2"], w["bg2"], sv["s_prev"], do)
    dq = _sum2(dq2, s=s, ts=ts, name="sum_dirs_512")
    dk = _sum2(dk2, s=s, ts=ts, name="sum_dirs_512")
    dv = _sum2(dv2, s=s, ts=ts, name="sum_dirs_1024")
    dga = _sum2(dga2, s=s, ts=ts, name="sum_dirs_128")
    dp = jnp.concatenate([dq, dk, dv, dgr, dga], axis=1)
    dh = _mm(dp, w["w_in"], tb=True, name="odd_in_dx")
    g_in = _mm(sv["h"], dp, ta=True, name="odd_in_dw")
    dx, g_mix = _rmsnorm_bwd(_cols(sv["x"], D_MODEL, 0, ts), w["mix_g"], dh, dy, n=D_MODEL, s=s, ts=ts,
                             name="mix_norm_bwd")
    return dx, dict(w_in=g_in, wg2=g_wg, bg2=g_bg, w_out=g_out, mix_g=g_mix, gla_gain=g_gain)


_EVEN_NAMES = dict(mix_norm="mix_norm_even", w_in="w_in_even", q_norm="mla_q_norm", kv_norm="mla_kv_norm",
                   w_uq="mla_w_uq", w_ukv="mla_w_ukv", q_head_norm="mla_q_head_norm", k_head_norm="mla_k_head_norm",
                   ret_out_norm="ret_out_norm", w_out="w_out_even")
_ODD_NAMES = dict(mix_norm="mix_norm_odd", w_in="w_in_odd", w_gate_fwd="gla_w_gate_fwd", b_gate_fwd="gla_b_gate_fwd",
                  w_gate_bwd="gla_w_gate_bwd", b_gate_bwd="gla_b_gate_bwd", gla_out_norm="gla_out_norm",
                  w_out="w_out_odd")
_FFN_NAMES = dict(norm="ffn_norm", w_up="ffn_w_up", conv_w="ffn_conv_w", conv_b="ffn_conv_b", w_down="ffn_w_down")


def _local_step(x, pos, target, full):
    s = x.shape[0]
    ts = min(s, 256)
    tabs = _rope_tables(pos, MLA_ROPE, MLA_NOPE) + _rope_tables(pos, RET_DK, 0)
    layers = []
    for layer in range(DEPTH):
        i = layer // 2
        names = _EVEN_NAMES if layer % 2 == 0 else _ODD_NAMES
        wm, unpack_m = _packed(_pack_even if layer % 2 == 0 else _pack_odd, {k: full[n][i] for k, n in names.items()})
        if layer % 2 == 0:
            th = jnp.stack([full["ret_theta_fwd"][i], full["ret_theta_bwd"][i]]).astype(F32)
            wm["theta_h"] = jnp.broadcast_to(th[:, :, None], (2, RET_H, LANES))
            wm["theta_l"] = wm["theta_h"].reshape(2, 1, RET_H * LANES)
        wf, unpack_f = _packed(_pack_ffn, {k: full[n][layer] for k, n in _FFN_NAMES.items()})
        layers.append((wm, unpack_m, wf, unpack_f))

    saved = []
    for layer, (wm, _, wf, _) in enumerate(layers):
        if layer % 2 == 0:
            x, sv_m = _even_fwd(x, wm, tabs, s=s, ts=ts)
        else:
            x, sv_m = _odd_fwd(x, wm, s=s, ts=ts)
        x, sv_f = _ffn_fwd(x, wf, s=s, ts=ts)
        saved.append((sv_m, sv_f))

    dy, sq = _loss_head(x, target, s=s, ts=ts)
    loss = 0.5 / D_MODEL * jnp.sum(sq)

    grads = {}

    def put(name, idx, g):
        grads.setdefault(name, {})[idx] = g

    for layer in reversed(range(DEPTH)):
        wm, unpack_m, wf, unpack_f = layers[layer]
        sv_m, sv_f = saved[layer]
        i = layer // 2
        dy, gf = _ffn_bwd(dy, wf, sv_f, s=s, ts=ts)
        for k, g in unpack_f(gf).items():
            put(_FFN_NAMES[k], layer, g)
        if layer % 2 == 0:
            dy, gm, dth = _even_bwd(dy, wm, sv_m, tabs, s=s, ts=ts)
            put("ret_theta_fwd", i, dth[0])
            put("ret_theta_bwd", i, dth[1])
            names = _EVEN_NAMES
        else:
            dy, gm = _odd_bwd(dy, wm, sv_m, s=s, ts=ts)
            names = _ODD_NAMES
        for k, g in unpack_m(gm).items():
            put(names[k], i, g)
    out = {n: jnp.stack([g[j] for j in range(len(g))]) for n, g in grads.items()}
    return loss, dy, out


HBM_SPEC = pl.BlockSpec(memory_space=pltpu.HBM)
VMEM_SPEC = pl.BlockSpec(memory_space=pltpu.VMEM)
CHIPS = 4
CORES = 2
ROW = 8 * LANES


def _xyc():
    return lax.axis_index("x"), lax.axis_index("y"), lax.axis_index("c")


def _other_chips(x, y):
    return [(1 - x, y), (x, 1 - y), (1 - x, 1 - y)]


def _remote(src, dst, send, recv, dev):
    return pltpu.make_async_remote_copy(src_ref=src, dst_ref=dst, send_sem=send, recv_sem=recv,
                                        device_id=dev, device_id_type=MESH)


def _gather_chips(big, small):
    def body(b_ref, s_ref, ob_ref, os_ref, send, recv, loc):
        x, y, c = _xyc()
        me = 2 * x + y
        pairs = ((b_ref, ob_ref), (s_ref, os_ref))
        local = [pltpu.make_async_copy(src, dst.at[me], loc.at[t]) for t, (src, dst) in enumerate(pairs)]
        for cp in local:
            cp.start()
        sends = []
        for j, (px, py) in enumerate(_other_chips(x, y)):
            for t, (src, dst) in enumerate(pairs):
                cp = _remote(src, dst.at[me], send.at[2 * j + t], recv.at[2 * j + t], (px, py, c))
                cp.start()
                sends.append(cp)
        for j, (px, py) in enumerate(_other_chips(x, y)):
            for t, (src, dst) in enumerate(pairs):
                _remote(src, dst.at[2 * px + py], send.at[2 * j + t], recv.at[2 * j + t], (px, py, c)).wait_recv()
        for cp in sends:
            cp.wait_send()
        for cp in local:
            cp.wait()

    return pl.pallas_call(
        body, name="gather_chips", in_specs=[HBM_SPEC, HBM_SPEC], out_specs=[HBM_SPEC, HBM_SPEC],
        out_shape=[jax.ShapeDtypeStruct((CHIPS,) + big.shape, big.dtype),
                   jax.ShapeDtypeStruct((CHIPS,) + small.shape, small.dtype)],
        scratch_shapes=[pltpu.SemaphoreType.DMA((6,)), pltpu.SemaphoreType.DMA((6,)), pltpu.SemaphoreType.DMA((2,))],
    )(big, small)


def _swap_cores(v):
    def body(v_ref, o_ref, send, recv):
        x, y, c = _xyc()
        cp = _remote(v_ref, o_ref, send, recv, (x, y, 1 - c))
        cp.start()
        cp.wait()

    return pl.pallas_call(
        body, name="swap_cores", in_specs=[HBM_SPEC], out_specs=HBM_SPEC,
        out_shape=jax.ShapeDtypeStruct(v.shape, v.dtype),
        scratch_shapes=[pltpu.SemaphoreType.DMA, pltpu.SemaphoreType.DMA],
    )(v)


def _scatter_chips(v):
    def body(v_ref, o_ref, send, recv, loc):
        x, y, c = _xyc()
        me = 2 * x + y
        local = pltpu.make_async_copy(v_ref.at[me], o_ref.at[me], loc)
        local.start()
        sends = []
        for j, (px, py) in enumerate(_other_chips(x, y)):
            cp = _remote(v_ref.at[2 * px + py], o_ref.at[me], send.at[j], recv.at[j], (px, py, c))
            cp.start()
            sends.append(cp)
        for j, (px, py) in enumerate(_other_chips(x, y)):
            _remote(v_ref.at[me], o_ref.at[2 * px + py], send.at[j], recv.at[j], (px, py, c)).wait_recv()
        for cp in sends:
            cp.wait_send()
        local.wait()

    return pl.pallas_call(
        body, name="scatter_chips", in_specs=[HBM_SPEC], out_specs=HBM_SPEC,
        out_shape=jax.ShapeDtypeStruct(v.shape, v.dtype),
        scratch_shapes=[pltpu.SemaphoreType.DMA((3,)), pltpu.SemaphoreType.DMA((3,)), pltpu.SemaphoreType.DMA],
    )(v)


def _gather_cores(v):
    def body(v_ref, o_ref, send, recv, loc):
        x, y, c = _xyc()
        local = pltpu.make_async_copy(v_ref, o_ref.at[c], loc)
        local.start()
        cp = _remote(v_ref, o_ref.at[c], send, recv, (x, y, 1 - c))
        cp.start()
        _remote(v_ref, o_ref.at[1 - c], send, recv, (x, y, 1 - c)).wait_recv()
        cp.wait_send()
        local.wait()

    return pl.pallas_call(
        body, name="gather_cores", in_specs=[HBM_SPEC], out_specs=HBM_SPEC,
        out_shape=jax.ShapeDtypeStruct((CORES,) + v.shape, v.dtype),
        scratch_shapes=[pltpu.SemaphoreType.DMA, pltpu.SemaphoreType.DMA, pltpu.SemaphoreType.DMA],
    )(v)


def _all_reduce_devices(v):
    n_dev = CHIPS * CORES

    def body(v_ref, o_ref, buf, send, recv):
        x, y, c = _xyc()
        me = 4 * x + 2 * y + c
        buf[pl.ds(me, 1)] = v_ref[...][None]
        sends = []
        for m in range(1, n_dev):
            px = 1 - x if m & 4 else x
            py = 1 - y if m & 2 else y
            pc = 1 - c if m & 1 else c
            cp = _remote(v_ref, buf.at[me], send.at[m - 1], recv.at[m - 1], (px, py, pc))
            cp.start()
            sends.append((cp, 4 * px + 2 * py + pc))
        for m, (cp, peer) in enumerate(sends):
            _remote(v_ref, buf.at[peer], send.at[m], recv.at[m], (x, y, c)).wait_recv()
        for cp, _ in sends:
            cp.wait_send()
        acc = buf[0]
        for k in range(1, n_dev):
            acc = acc + buf[k]
        o_ref[...] = acc

    return pl.pallas_call(
        body, name="all_reduce_devices", in_specs=[VMEM_SPEC], out_specs=VMEM_SPEC,
        out_shape=jax.ShapeDtypeStruct(v.shape, F32),
        scratch_shapes=[pltpu.VMEM((n_dev,) + v.shape, F32), pltpu.SemaphoreType.DMA((n_dev - 1,)),
                        pltpu.SemaphoreType.DMA((n_dev - 1,))],
    )(v)


def _add_rows(arrs, *, name):
    r = arrs[0].shape[0]
    ts = r if r <= 512 else 512
    assert r % ts == 0

    def fn(*vals):
        acc = vals[0]
        for v in vals[1:]:
            acc = acc + v
        return acc

    return _ew(fn, [_cols(a, ROW, 0, ts) for a in arrs], [], [(ROW, F32)], s=r, ts=ts, name=name)[0]


_SHARDED = (("w_in_even", 2), ("mla_w_uq", 2), ("mla_w_ukv", 2), ("w_out_even", 1), ("w_in_odd", 2), ("w_out_odd", 1),
            ("ffn_w_up", 2), ("ffn_w_down", 1),
            ("mix_norm_odd", 1), ("gla_w_gate_fwd", 2), ("gla_b_gate_fwd", 1), ("gla_w_gate_bwd", 2),
            ("gla_b_gate_bwd", 1), ("gla_out_norm", 2), ("ffn_conv_w", 2))
_N_MATRICES = 8
_REPLICATED = ("mix_norm_even", "mla_q_norm", "mla_kv_norm", "mla_q_head_norm", "mla_k_head_norm", "ret_theta_fwd",
               "ret_theta_bwd", "ret_out_norm", "ffn_norm", "ffn_conv_b")
_WEIGHTS = ("mix_norm_even", "w_in_even", "mla_q_norm", "mla_kv_norm", "mla_w_uq", "mla_w_ukv", "mla_q_head_norm",
            "mla_k_head_norm", "ret_theta_fwd", "ret_theta_bwd", "ret_out_norm", "w_out_even", "mix_norm_odd",
            "w_in_odd", "gla_w_gate_fwd", "gla_b_gate_fwd", "gla_w_gate_bwd", "gla_b_gate_bwd", "gla_out_norm",
            "w_out_odd", "ffn_norm", "ffn_w_up", "ffn_conv_w", "ffn_conv_b", "ffn_w_down")
_GRAD_ROWS = 2 * 512


def _flatten(arrs, row_multiple, dtype):
    flat = jnp.concatenate([a.reshape(-1).astype(dtype) for a in arrs])
    per = ROW * row_multiple
    total = -(-flat.shape[0] // per) * per
    return jnp.pad(flat, (0, total - flat.shape[0])).reshape(-1, ROW)


def _unflatten(flat, shapes):
    flat = flat.reshape(-1)
    out, o = [], 0
    for shp in shapes:
        n = math.prod(shp)
        out.append(flat[o:o + n].reshape(shp))
        o += n
    return out


def kernel(x, positions, mix_norm_even, w_in_even, mla_q_norm, mla_kv_norm, mla_w_uq, mla_w_ukv, mla_q_head_norm, mla_k_head_norm, ret_theta_fwd, ret_theta_bwd, ret_out_norm, w_out_even, mix_norm_odd, w_in_odd, gla_w_gate_fwd, gla_b_gate_fwd, gla_w_gate_bwd, gla_b_gate_bwd, gla_out_norm, w_out_odd, ffn_norm, ffn_w_up, ffn_conv_w, ffn_conv_b, ffn_w_down, loss_target, m_mix_norm_even, m_w_in_even, m_mla_q_norm, m_mla_kv_norm, m_mla_w_uq, m_mla_w_ukv, m_mla_q_head_norm, m_mla_k_head_norm, m_ret_theta_fwd, m_ret_theta_bwd, m_ret_out_norm, m_w_out_even, m_mix_norm_odd, m_w_in_odd, m_gla_w_gate_fwd, m_gla_b_gate_fwd, m_gla_w_gate_bwd, m_gla_b_gate_bwd, m_gla_out_norm, m_w_out_odd, m_ffn_norm, m_ffn_w_up, m_ffn_conv_w, m_ffn_conv_b, m_ffn_w_down, v_mix_norm_even, v_w_in_even, v_mla_q_norm, v_mla_kv_norm, v_mla_w_uq, v_mla_w_ukv, v_mla_q_head_norm, v_mla_k_head_norm, v_ret_theta_fwd, v_ret_theta_bwd, v_ret_out_norm, v_w_out_even, v_mix_norm_odd, v_w_in_odd, v_gla_w_gate_fwd, v_gla_b_gate_fwd, v_gla_w_gate_bwd, v_gla_b_gate_bwd, v_gla_out_norm, v_w_out_odd, v_ffn_norm, v_ffn_w_up, v_ffn_conv_w, v_ffn_conv_b, v_ffn_w_down):
    args = dict(locals())
    x2, pos, target = args["x"][0], args["positions"][0], args["loss_target"][0]
    c = lax.axis_index("c")

    mats = [n for n, _ in _SHARDED[:_N_MATRICES]]
    smalls = [n for n, _ in _SHARDED[_N_MATRICES:]]
    big = _flatten([args[n] for n in mats], 16, BF16)
    small = _flatten([args[n] for n in smalls], 8, F32)
    g_big, g_small = _gather_chips(big, small)
    axis = dict(_SHARDED)
    full = {n: args[n] for n in _REPLICATED}
    for names, g in ((mats, g_big), (smalls, g_small)):
        per_chip = [_unflatten(g[j], [args[n].shape for n in names]) for j in range(CHIPS)]
        for k, n in enumerate(names):
            full[n] = jnp.concatenate([per_chip[j][k] for j in range(CHIPS)], axis=axis[n])

    loss, grad_x, grads = _local_step(x2, pos, target, full)
    loss = lax.psum(loss, ("x", "y", "c"))

    names = [n for n, _ in _SHARDED]
    shard_shapes = [args[n].shape for n in names]
    pieces = [jnp.split(grads[n], CHIPS, axis=axis[n]) for n in names]
    g4 = jnp.stack([_flatten([p[j] for p in pieces], _GRAD_ROWS, F32) for j in range(CHIPS)])
    rows = g4.shape[1]
    half = rows // CORES
    g4 = g4.reshape(CHIPS, CORES, half, ROW)
    mine = lax.dynamic_index_in_dim(g4, c, axis=1, keepdims=False).reshape(CHIPS * half, ROW)
    theirs = lax.dynamic_index_in_dim(g4, 1 - c, axis=1, keepdims=False).reshape(CHIPS * half, ROW)
    chip_sum = _add_rows([mine, _swap_cores(theirs)], name="add_core_halves")
    parts = _scatter_chips(chip_sum.reshape(CHIPS, half, ROW))
    reduced_half = _add_rows([parts[j] for j in range(CHIPS)], name="add_chip_parts")
    g_shard = _gather_cores(reduced_half).reshape(rows, ROW)

    w_flat, m_flat, v_flat = (_flatten([args[pre + n] for n in names], _GRAD_ROWS, F32) for pre in ("", "m_", "v_"))
    d_flat, nm_flat, nv_flat = _adamw(w_flat, g_shard, m_flat, v_flat, name="adamw_split")
    res = {}
    for kind, flat in (("grad", g_shard), ("delta", d_flat), ("new_m", nm_flat), ("new_v", nv_flat)):
        for n, a in zip(names, _unflatten(flat, shard_shapes)):
            res[kind + "_" + n] = a

    rep_shapes = [args[n].shape for n in _REPLICATED]
    g_rep = _all_reduce_devices(_flatten([grads[n] for n in _REPLICATED], 8, F32))
    w_rep, m_rep, v_rep = (_flatten([args[pre + n] for n in _REPLICATED], 8, F32) for pre in ("", "m_", "v_"))
    d_rep, nm_rep, nv_rep = _adamw(w_rep, g_rep, m_rep, v_rep, name="adamw_replicated")
    for kind, flat in (("grad", g_rep), ("delta", d_rep), ("new_m", nm_rep), ("new_v", nv_rep)):
        for n, a in zip(_REPLICATED, _unflatten(flat, rep_shapes)):
            res[kind + "_" + n] = a

    outs = [loss, grad_x[None]]
    for kind in ("grad", "delta", "new_m", "new_v"):
        outs += [res[kind + "_" + n] for n in _WEIGHTS]
    return tuple(outs)
```

```python
import math

import jax
import jax.numpy as jnp
from jax import lax
from jax.experimental import pallas as pl
from jax.experimental.pallas import tpu as pltpu

F32 = jnp.float32
BF16 = jnp.bfloat16
MESH = pl.DeviceIdType.MESH

EPS = 1e-6
D_MODEL = 1024
DEPTH = 4
LANES = 128
MLA_H, MLA_QR, MLA_KVR, MLA_NOPE, MLA_ROPE, MLA_V = 8, 384, 256, 64, 32, 64
MLA_QK = MLA_NOPE + MLA_ROPE
MLA_SCALE = MLA_QK ** -0.5
RET_H, RET_DK, RET_DV, RET_C = 8, 64, 64, 128
GLA_H, GLA_DK, GLA_DV, GLA_R, GLA_TAU, GLA_C = 4, 128, 256, 16, 16.0, 64
D_FF = 2816
ROPE_THETA = 10000.0
LN2 = math.log(2.0)
ADAM_LR, ADAM_B1, ADAM_B2, ADAM_EPS, ADAM_WD, ADAM_STEP = 0.001, 0.9, 0.999, 1e-08, 0.01, 10

EV_RET = 4 * RET_H * LANES
EV_CQ = 512
EV_W = 5120
EV_KR_BLK = (EV_RET + EV_CQ + MLA_KVR) // LANES
OD_W = 3200
OD_GA_BLK = 3072 // LANES

VMEM_LIMIT = 56 * 1024 * 1024
MM_TILE_CAP = 1408
V_ONES = (MLA_V, MLA_V + 1)


def _cp(sem):
    return pltpu.CompilerParams(dimension_semantics=sem, vmem_limit_bytes=VMEM_LIMIT)


def _dot(a, b):
    return jnp.dot(a, b, preferred_element_type=F32)


def _dot_nt(a, b):
    return lax.dot_general(a, b, (((1,), (1,)), ((), ())), preferred_element_type=F32)


def _dot_tn(a, b):
    return lax.dot_general(a, b, (((0,), (0,)), ((), ())), preferred_element_type=F32)


def _bf(x):
    return x.astype(BF16)


def _split3(x):
    h1 = _bf(x)
    r1 = x - h1.astype(F32)
    h2 = _bf(r1)
    h3 = _bf(r1 - h2.astype(F32))
    return h1, h2, h3


def _tile(n, cap):
    if n <= cap:
        return n
    best = None
    for t in range(LANES, cap + 1, LANES):
        if n % t == 0:
            best = t
    assert best is not None, n
    return best


def _mm(a, b, *, ta=False, tb=False, res=None, out_dtype=F32, name):
    assert not (ta and tb)
    if ta:
        kdim, m = a.shape
    else:
        m, kdim = a.shape
    if tb:
        n, kb = b.shape
    else:
        kb, n = b.shape
    assert kb == kdim, (a.shape, b.shape, ta, tb)
    tm, tn, tk = _tile(m, MM_TILE_CAP), _tile(n, MM_TILE_CAP), _tile(kdim, MM_TILE_CAP)
    nk = kdim // tk
    has_res = res is not None
    vmem = (2 * tm * tk * a.dtype.itemsize + 2 * tk * tn * b.dtype.itemsize
            + 2 * tm * tn * jnp.dtype(out_dtype).itemsize + (2 * tm * tn * 4 if has_res else 0)
            + (tm * tn * 4 if nk > 1 else 0))
    assert vmem <= VMEM_LIMIT - 8 * 1024 * 1024, (name, vmem)
    a_spec = (pl.BlockSpec((tk, tm), lambda i, j, k: (k, i)) if ta
              else pl.BlockSpec((tm, tk), lambda i, j, k: (i, k)))
    b_spec = (pl.BlockSpec((tn, tk), lambda i, j, k: (j, k)) if tb
              else pl.BlockSpec((tk, tn), lambda i, j, k: (k, j)))
    o_spec = pl.BlockSpec((tm, tn), lambda i, j, k: (i, j))

    def product(a_ref, b_ref):
        av, bv = _bf(a_ref[...]), _bf(b_ref[...])
        if ta:
            return _dot_tn(av, bv)
        if tb:
            return _dot_nt(av, bv)
        return _dot(av, bv)

    def body(*refs):
        a_ref, b_ref = refs[:2]
        r_ref = refs[2] if has_res else None
        o_ref = refs[3] if has_res else refs[2]

        def finish(r):
            if has_res:
                r = r + r_ref[...]
            o_ref[...] = r.astype(o_ref.dtype)

        if nk == 1:
            finish(product(a_ref, b_ref))
            return
        acc = refs[-1]
        k = pl.program_id(2)

        @pl.when(k == 0)
        def _():
            acc[...] = product(a_ref, b_ref)

        @pl.when(k > 0)
        def _():
            acc[...] += product(a_ref, b_ref)

        @pl.when(k == nk - 1)
        def _():
            finish(acc[...])

    ins = [a, b] + ([res] if has_res else [])
    in_specs = [a_spec, b_spec] + ([o_spec] if has_res else [])
    return pl.pallas_call(
        body, name=name, grid=(m // tm, n // tn, nk),
        in_specs=in_specs, out_specs=o_spec,
        out_shape=jax.ShapeDtypeStruct((m, n), out_dtype),
        scratch_shapes=[pltpu.VMEM((tm, tn), F32)] if nk > 1 else [],
        compiler_params=_cp(("parallel", "parallel", "arbitrary")),
    )(*ins)


def _ew(fn, rows, pars, outs, accs=(), *, s, ts, name):
    n_in = len(rows) + len(pars)
    n_o = len(outs)

    def body(*refs):
        i = pl.program_id(0)
        vals = fn(*[r[...] for r in refs[:n_in]])
        if not isinstance(vals, (tuple, list)):
            vals = (vals,)
        assert len(vals) == n_o + len(accs), (name, len(vals))
        for r, v in zip(refs[n_in:n_in + n_o], vals[:n_o]):
            r[...] = v.astype(r.dtype)
        for r, v in zip(refs[n_in + n_o:], vals[n_o:]):
            @pl.when(i == 0)
            def _(r=r, v=v):
                r[...] = v

            @pl.when(i > 0)
            def _(r=r, v=v):
                r[...] += v

    in_specs = [sp for _, sp in rows]
    in_specs += [pl.BlockSpec(p.shape, lambda i, nd=p.ndim: (0,) * nd) for p in pars]
    out_specs = [pl.BlockSpec((ts, w), lambda i: (i, 0)) for w, _ in outs]
    out_specs += [pl.BlockSpec((r, w), lambda i: (0, 0)) for r, w in accs]
    out_shape = [jax.ShapeDtypeStruct((s, w), dt) for w, dt in outs]
    out_shape += [jax.ShapeDtypeStruct((r, w), F32) for r, w in accs]
    return pl.pallas_call(
        body, name=name, grid=(s // ts,), in_specs=in_specs, out_specs=out_specs, out_shape=out_shape,
        compiler_params=_cp(("arbitrary",)),
    )(*[a for a, _ in rows], *pars)


def _cols(arr, width, blk, ts):
    return (arr, pl.BlockSpec((ts, width), lambda i, b=blk: (i, b)))


def _lead(arr, d, ts):
    return (arr, pl.BlockSpec((None, ts, arr.shape[2]), lambda i, d=d: (d, i, 0)))


def _rowsum(x):
    return jnp.sum(x, axis=0, keepdims=True)


def _lanesum(x):
    return jnp.sum(x, axis=-1, keepdims=True)


def _gsum(x, group):
    w = x.shape[-1]
    if group == w:
        return jnp.broadcast_to(_lanesum(x), x.shape)
    parts = [jnp.broadcast_to(_lanesum(x[:, g:g + group]), (x.shape[0], group)) for g in range(0, w, group)]
    return jnp.concatenate(parts, axis=-1)


def _gn(x, gain, group, n):
    rstd = lax.rsqrt(_gsum(x * x, group) * (1.0 / n) + EPS)
    xn = x * rstd
    return xn * gain, xn, rstd


def _gn_bwd(dy, xn, rstd, gain, group, n):
    dxn = dy * gain
    dx = rstd * (dxn - xn * (_gsum(dxn * xn, group) * (1.0 / n)))
    return dx, _rowsum(dy * xn)


def _sigmoid(x):
    return 1.0 / (1.0 + jnp.exp(-x))


def _rmsnorm(x_row, g, *, n, s, ts, name):
    w = g.shape[-1]

    def fn(x, gv):
        return _gn(x, gv, w, n)[0]

    return _ew(fn, [x_row], [g], [(w, BF16)], s=s, ts=ts, name=name)[0]


def _rmsnorm_bwd(x_row, g, dh, dres, *, n, s, ts, name):
    w = g.shape[-1]
    has_res = dres is not None

    def fn(x, dhv, *rest):
        gv = rest[-1]
        _, xn, rstd = _gn(x, gv, w, n)
        dx, dg = _gn_bwd(dhv, xn, rstd, gv, w, n)
        if has_res:
            dx = dx + rest[0]
        return dx, dg

    rows = [x_row, _cols(dh, w, 0, ts)] + ([_cols(dres, w, 0, ts)] if has_res else [])
    return _ew(fn, rows, [g], [(w, F32)], [(1, w)], s=s, ts=ts, name=name)


def _rope_tables(pos, real, offset):
    half = real // 2
    inv = ROPE_THETA ** (-jnp.arange(half, dtype=F32) / half)
    ang = pos.astype(F32)[:, None] * inv
    c, sn = jnp.cos(ang), jnp.sin(ang)
    s = pos.shape[0]
    cos_t = jnp.concatenate([jnp.ones((s, offset), F32), c, c,
                             jnp.ones((s, LANES - offset - real), F32)], axis=1)
    sin_t = jnp.concatenate([jnp.zeros((s, offset), F32), -sn, sn,
                             jnp.zeros((s, LANES - offset - real), F32)], axis=1)
    return cos_t, sin_t


def _rope(x, cos_t, sin_t, real, offset):
    half = real // 2
    lane = lax.broadcasted_iota(jnp.int32, x.shape, 1)
    partner = jnp.where(lane < offset + half, pltpu.roll(x, LANES - half, 1), pltpu.roll(x, half, 1))
    return x * cos_t + partner * sin_t


def _mla_prep(q_pre, kv_pre, p_even, cos_m, sin_m, qhn, khn, *, s, ts):
    w = MLA_H * LANES

    def fn(qp, kp, vp, kr, c, sn, gq, gk):
        qs, ks = [], []
        for h in range(MLA_H):
            sl = slice(h * LANES, (h + 1) * LANES)
            qn = _gn(qp[:, sl], gq, LANES, MLA_QK)[0]
            kn = _gn(kp[:, sl] + kr, gk, LANES, MLA_QK)[0]
            qs.append(_rope(qn, c, sn, MLA_ROPE, MLA_NOPE) * MLA_SCALE)
            ks.append(_rope(kn, c, sn, MLA_ROPE, MLA_NOPE))
        lane = lax.broadcasted_iota(jnp.int32, vp.shape, 1) % LANES
        ones = (lane == V_ONES[0]) | (lane == V_ONES[1])
        return jnp.concatenate(qs, axis=1), jnp.concatenate(ks, axis=1), jnp.where(ones, 1.0, vp)

    rows = [_cols(q_pre, w, 0, ts), _cols(kv_pre, w, 0, ts), _cols(kv_pre, w, 1, ts),
            _cols(p_even, LANES, EV_KR_BLK, ts), _cols(cos_m, LANES, 0, ts), _cols(sin_m, LANES, 0, ts)]
    return _ew(fn, rows, [qhn, khn], [(w, BF16)] * 3, s=s, ts=ts, name="mla_prep")


def _mla_prep_bwd(q_pre, kv_pre, p_even, cos_m, sin_m, qhn, khn, dq, dk, *, s, ts):
    w = MLA_H * LANES

    def fn(qp, kp, kr, c, sn, dqv, dkv, gq, gk):
        dqs, dks = [], []
        dkr = jnp.zeros_like(kr)
        dgq = jnp.zeros((1, LANES), F32)
        dgk = jnp.zeros((1, LANES), F32)
        for h in range(MLA_H):
            sl = slice(h * LANES, (h + 1) * LANES)
            _, qn, qr = _gn(qp[:, sl], gq, LANES, MLA_QK)
            _, kn, krs = _gn(kp[:, sl] + kr, gk, LANES, MLA_QK)
            dqn = _rope(dqv[:, sl] * MLA_SCALE, c, -sn, MLA_ROPE, MLA_NOPE)
            dkn = _rope(dkv[:, sl], c, -sn, MLA_ROPE, MLA_NOPE)
            dqh, g1 = _gn_bwd(dqn, qn, qr, gq, LANES, MLA_QK)
            dkh, g2 = _gn_bwd(dkn, kn, krs, gk, LANES, MLA_QK)
            dqs.append(dqh)
            dks.append(dkh)
            dkr = dkr + dkh
            dgq = dgq + g1
            dgk = dgk + g2
        return jnp.concatenate(dqs, axis=1), jnp.concatenate(dks, axis=1), dkr, dgq, dgk

    rows = [_cols(q_pre, w, 0, ts), _cols(kv_pre, w, 0, ts), _cols(p_even, LANES, EV_KR_BLK, ts),
            _cols(cos_m, LANES, 0, ts), _cols(sin_m, LANES, 0, ts), _cols(dq, w, 0, ts), _cols(dk, w, 0, ts)]
    return _ew(fn, rows, [qhn, khn], [(w, BF16), (w, BF16), (LANES, BF16)], [(1, LANES), (1, LANES)],
               s=s, ts=ts, name="mla_prep_bwd")


def _flash_fwd(q, k, v, *, tq, tk):
    s = q.shape[0]
    nq, nk = s // tq, s // tk
    rq = min(tq, 256)

    def body(q_ref, k_ref, v_ref, o_ref, lse_ref, m_s, acc):
        j = pl.program_id(2)

        @pl.when(j == 0)
        def _():
            m_s[...] = jnp.full_like(m_s, -jnp.inf)
            acc[...] = jnp.zeros_like(acc)

        kv, vv = k_ref[...], v_ref[...]
        for r in range(0, tq, rq):
            rows = slice(r, r + rq)
            sc = _dot_nt(q_ref[rows, :], kv)
            m_prev = m_s[rows, :]
            m_new = jnp.maximum(m_prev, jnp.max(sc, axis=-1, keepdims=True))
            p = jnp.exp(sc - jnp.tile(m_new, (1, tk // LANES)))
            acc[rows, :] = jnp.exp(m_prev - m_new) * acc[rows, :] + _dot(_bf(p), vv)
            m_s[rows, :] = m_new

        @pl.when(j == nk - 1)
        def _():
            a = acc[...]
            l = a[:, V_ONES[0]:V_ONES[0] + 1]
            o_ref[...] = (a / l).astype(o_ref.dtype)
            lse_ref[...] = m_s[:, 0:1] + jnp.log(l)

    qs = pl.BlockSpec((tq, LANES), lambda h, i, j: (i, h))
    ks = pl.BlockSpec((tk, LANES), lambda h, i, j: (j, h))
    return pl.pallas_call(
        body, name="mla_flash_fwd", grid=(MLA_H, nq, nk),
        in_specs=[qs, ks, ks],
        out_specs=[qs, pl.BlockSpec((None, tq, 1), lambda h, i, j: (h, i, 0))],
        out_shape=[jax.ShapeDtypeStruct((s, MLA_H * LANES), BF16), jax.ShapeDtypeStruct((MLA_H, s, 1), F32)],
        scratch_shapes=[pltpu.VMEM((tq, LANES), F32), pltpu.VMEM((tq, LANES), F32)],
        compiler_params=_cp(("parallel", "parallel", "arbitrary")),
    )(q, k, v)


def _attn_bwd_prep(dar, o, *, s, ts):
    w = MLA_H * LANES

    def fn(dov, ov):
        outs = []
        lane = lax.broadcasted_iota(jnp.int32, (dov.shape[0], LANES), 1)
        for h in range(MLA_H):
            sl = slice(h * LANES, (h + 1) * LANES)
            d = dov[:, sl]
            delta = _lanesum(d * ov[:, sl].astype(F32))
            hi = _bf(delta).astype(F32)
            outs.append(jnp.where(lane == V_ONES[0], -hi, jnp.where(lane == V_ONES[1], hi - delta, d)))
        return jnp.concatenate(outs, axis=1)

    return _ew(fn, [_cols(dar, w, 0, ts), _cols(o, w, 0, ts)], [], [(w, BF16)], s=s, ts=ts,
               name="mla_attn_bwd_prep")[0]


def _flash_bwd(q, k, v, do, lse, *, tq, tk):
    s = q.shape[0]
    nq, nk = s // tq, s // tk

    def body(q_ref, k_ref, v_ref, do_ref, lse_ref, dq_ref, dk_ref, dv_ref, dk_acc, dv_acc):
        j = pl.program_id(1)
        i = pl.program_id(2)
        qv, kv, dov = q_ref[...], k_ref[...], do_ref[...]
        p = jnp.exp(_dot_nt(qv, kv) - lse_ref[...])
        ds = _bf(p * _dot_nt(dov, v_ref[...]))
        dv_c = _dot_tn(_bf(p), dov)
        dk_c = _dot_tn(ds, qv)
        dq_c = _dot(ds, kv)
        rows = pl.ds(pl.multiple_of(i * tq, tq), tq)

        @pl.when(i == 0)
        def _():
            dk_acc[...] = dk_c
            dv_acc[...] = dv_c

        @pl.when(i > 0)
        def _():
            dk_acc[...] += dk_c
            dv_acc[...] += dv_c

        @pl.when(j == 0)
        def _():
            dq_ref[rows, :] = dq_c

        @pl.when(j > 0)
        def _():
            dq_ref[rows, :] += dq_c

        @pl.when(i == nq - 1)
        def _():
            dk_ref[...] = dk_acc[...]
            dv_ref[...] = dv_acc[...].astype(dv_ref.dtype)

    qs = pl.BlockSpec((tq, LANES), lambda h, j, i: (i, h))
    ks = pl.BlockSpec((tk, LANES), lambda h, j, i: (j, h))
    st = pl.BlockSpec((None, tq, 1), lambda h, j, i: (h, i, 0))
    return pl.pallas_call(
        body, name="mla_flash_bwd", grid=(MLA_H, nk, nq),
        in_specs=[qs, ks, ks, qs, st],
        out_specs=[pl.BlockSpec((s, LANES), lambda h, j, i: (0, h)), ks, ks],
        out_shape=[jax.ShapeDtypeStruct((s, MLA_H * LANES), F32), jax.ShapeDtypeStruct((s, MLA_H * LANES), F32),
                   jax.ShapeDtypeStruct((s, MLA_H * LANES), BF16)],
        scratch_shapes=[pltpu.VMEM((tk, LANES), F32), pltpu.VMEM((tk, LANES), F32)],
        compiler_params=_cp(("parallel", "arbitrary", "arbitrary")),
    )(q, k, v, do, lse)


def _ret_geometry(d, c):
    df = d.astype(F32)
    ii = lax.broadcasted_iota(jnp.int32, (c, c), 0).astype(F32)
    jj = lax.broadcasted_iota(jnp.int32, (c, c), 1).astype(F32)
    rel = (ii - jj) * (1.0 - 2.0 * df)
    mask = rel >= df
    rel0 = jnp.maximum(rel, 0.0)
    pos = lax.broadcasted_iota(jnp.int32, (c, 1), 0).astype(F32)
    ez = (c - 1 - pos) + df * (2.0 * pos - (c - 1))
    ex = (pos + 1.0) + df * (c - 1 - 2.0 * pos)
    return mask, rel0, ez, ex


def _chunk_index(n_chunks):
    return lambda d, n: n + d * (n_chunks - 1 - 2 * n)


def _ret_fwd(p_even, cos_r, sin_r, theta_l):
    s = p_even.shape[0]
    c = RET_C
    n_chunks = s // c
    w = RET_H * LANES
    cidx = _chunk_index(n_chunks)

    def body(q_ref, k_ref, v_ref, cos_ref, sin_ref, th_ref, o_ref, rp_ref, r_s):
        d = pl.program_id(0)
        n = pl.program_id(1)

        @pl.when(n == 0)
        def _():
            r_s[...] = jnp.zeros_like(r_s)

        lg = jnp.log1p(-jnp.exp(-th_ref[...] * LN2))
        mask, rel0, ez, ex = _ret_geometry(d, c)
        cs, sn = cos_ref[...], sin_ref[...]
        rp_ref[...] = r_s[...]
        for h in range(RET_H):
            sl = slice(h * LANES, (h + 1) * LANES)
            lgh = lg[:, h * LANES:h * LANES + 1]
            dm = jnp.where(mask, jnp.exp(lgh * rel0), 0.0)
            qh = _bf(_rope(q_ref[:, sl], cs, sn, RET_DK, 0))
            kf = _rope(k_ref[:, sl], cs, sn, RET_DK, 0) * (RET_DK ** -0.5)
            kh = _bf(kf)
            vh = _bf(v_ref[:, sl])
            rh = r_s[sl, :]
            a = _dot_nt(qh, kh) * dm
            o_ref[:, sl] = _dot(_bf(a), vh) + jnp.exp(lgh * ex) * _dot(qh, _bf(rh))
            zk = _bf(kf * jnp.exp(lgh * ez))
            r_s[sl, :] = jnp.exp(lgh * c) * rh + _dot_tn(zk, vh)

    def col(blk):
        return pl.BlockSpec((c, w), lambda d, n: (cidx(d, n), blk))

    tab = pl.BlockSpec((c, LANES), lambda d, n: (cidx(d, n), 0))
    return pl.pallas_call(
        body, name="ret_fwd", grid=(2, n_chunks),
        in_specs=[col(0), col(1), col(2), tab, tab, pl.BlockSpec((None, 1, w), lambda d, n: (d, 0, 0))],
        out_specs=[pl.BlockSpec((None, c, w), lambda d, n: (d, cidx(d, n), 0)),
                   pl.BlockSpec((None, None, w, LANES), lambda d, n: (d, cidx(d, n), 0, 0))],
        out_shape=[jax.ShapeDtypeStruct((2, s, w), F32), jax.ShapeDtypeStruct((2, n_chunks, w, LANES), F32)],
        scratch_shapes=[pltpu.VMEM((w, LANES), F32)],
        compiler_params=_cp(("arbitrary", "arbitrary")),
    )(p_even, p_even, p_even, cos_r, sin_r, theta_l)


def _ret_bwd(p_even, cos_r, sin_r, theta_l, theta_h, r_prev, do):
    s = p_even.shape[0]
    c = RET_C
    n_chunks = s // c
    w = RET_H * LANES
    fwd_idx = _chunk_index(n_chunks)

    def cidx(d, n):
        return fwd_idx(d, n_chunks - 1 - n)

    def body(q_ref, k_ref, v_ref, cos_ref, sin_ref, th_ref, thh_ref, rp_ref, do_ref,
             dq_ref, dk_ref, dv_ref, dth_ref, dr_s):
        d = pl.program_id(0)
        n = pl.program_id(1)

        @pl.when(n == 0)
        def _():
            dr_s[...] = jnp.zeros_like(dr_s)
            dth_ref[...] = jnp.zeros_like(dth_ref)

        lg = jnp.log1p(-jnp.exp(-th_ref[...] * LN2))
        mask, rel0, ez, ex = _ret_geometry(d, c)
        cs, sn = cos_ref[...], sin_ref[...]
        row = lax.broadcasted_iota(jnp.int32, (RET_H, LANES), 0)
        dlg = jnp.zeros((RET_H, LANES), F32)
        kscale = RET_DK ** -0.5
        for h in range(RET_H):
            sl = slice(h * LANES, (h + 1) * LANES)
            lgh = lg[:, h * LANES:h * LANES + 1]
            dm = jnp.where(mask, jnp.exp(lgh * rel0), 0.0)
            zeta = jnp.exp(lgh * ez)
            xi = jnp.exp(lgh * ex)
            gc = jnp.exp(lgh * c)
            qf = _rope(q_ref[:, sl], cs, sn, RET_DK, 0)
            qh = _bf(qf)
            kf = _rope(k_ref[:, sl], cs, sn, RET_DK, 0) * kscale
            kh = _bf(kf)
            zkf = kf * zeta
            zk = _bf(zkf)
            vh = _bf(v_ref[:, sl])
            dof = do_ref[:, sl]
            doh = _bf(dof)
            rp = rp_ref[sl, :]
            rpb = _bf(rp)
            drn = dr_s[sl, :]
            drb = _bf(drn)
            a = _dot_nt(qh, kh) * dm
            da0 = _dot_nt(doh, vh)
            da = _bf(da0 * dm)
            vdr = _dot_nt(vh, drb)
            dq_r = _dot(da, kh) + xi * _dot_nt(doh, rpb)
            dk_r = _dot_tn(da, qh) + zeta * vdr
            dv_ref[:, sl] = _dot_tn(_bf(a), doh) + _dot(zk, drb)
            dq_ref[:, sl] = _rope(dq_r, cs, -sn, RET_DK, 0)
            dk_ref[:, sl] = _rope(dk_r * kscale, cs, -sn, RET_DK, 0)
            dr_s[sl, :] = _dot_tn(_bf(qf * xi), doh) + gc * drn
            ocross = xi * _dot(qh, rpb)
            t = (jnp.sum(rel0 * a * da0, keepdims=True)
                 + jnp.sum(ex * dof * ocross, keepdims=True)
                 + c * gc * jnp.sum(drn * rp, keepdims=True)
                 + jnp.sum(ez * zkf * vdr, keepdims=True))
            dlg = jnp.where(row == h, t, dlg)
        x2 = jnp.exp(-thh_ref[...] * LN2)
        dth_ref[...] += dlg * (x2 * LN2 / (1.0 - x2))

    def col(blk):
        return pl.BlockSpec((c, w), lambda d, n: (cidx(d, n), blk))

    tab = pl.BlockSpec((c, LANES), lambda d, n: (cidx(d, n), 0))
    dirrow = pl.BlockSpec((None, c, w), lambda d, n: (d, cidx(d, n), 0))
    hrow = pl.BlockSpec((None, RET_H, LANES), lambda d, n: (d, 0, 0))
    return pl.pallas_call(
        body, name="ret_bwd", grid=(2, n_chunks),
        in_specs=[col(0), col(1), col(2), tab, tab, pl.BlockSpec((None, 1, w), lambda d, n: (d, 0, 0)), hrow,
                  pl.BlockSpec((None, None, w, LANES), lambda d, n: (d, cidx(d, n), 0, 0)),
                  pl.BlockSpec((c, w), lambda d, n: (cidx(d, n), 0))],
        out_specs=[dirrow, dirrow, dirrow, hrow],
        out_shape=[jax.ShapeDtypeStruct((2, s, w), F32)] * 3 + [jax.ShapeDtypeStruct((2, RET_H, LANES), F32)],
        scratch_shapes=[pltpu.VMEM((w, LANES), F32)],
        compiler_params=_cp(("arbitrary", "arbitrary")),
    )(p_even, p_even, p_even, cos_r, sin_r, theta_l, theta_h, r_prev, do)


def _post_fwd(o2, gate_row, gain, *, group, n, s, ts, name):
    w = o2.shape[2]

    def fn(of, ob, g, gv):
        y = _gn(of + ob, gv, group, n)[0]
        return g * _sigmoid(g) * y

    return _ew(fn, [_lead(o2, 0, ts), _lead(o2, 1, ts), gate_row], [gain], [(w, BF16)], s=s, ts=ts, name=name)[0]


def _post_bwd(o2, gate_row, gain, dr_row, *, group, n, s, ts, name):
    w = o2.shape[2]

    def fn(of, ob, g, dr, gv):
        y, xn, rstd = _gn(of + ob, gv, group, n)
        sg = _sigmoid(g)
        dy = dr * (g * sg)
        dgate = dr * y * (sg * (1.0 + g * (1.0 - sg)))
        do, dgain = _gn_bwd(dy, xn, rstd, gv, group, n)
        return do, dgate, dgain

    return _ew(fn, [_lead(o2, 0, ts), _lead(o2, 1, ts), gate_row, dr_row], [gain],
               [(w, F32), (w, BF16)], [(1, w)], s=s, ts=ts, name=name)


def _sum2(a2, *, s, ts, name):
    w = a2.shape[2]
    return _ew(lambda a, b: a + b, [_lead(a2, 0, ts), _lead(a2, 1, ts)], [], [(w, BF16)], s=s, ts=ts, name=name)[0]


def _gla_common(d, q_ref, k_ref, ga_ref, wg_ref, bg_ref):
    c = GLA_C
    df = d.astype(F32)
    ii = lax.broadcasted_iota(jnp.int32, (c, c), 0).astype(F32)
    jj = lax.broadcasted_iota(jnp.int32, (c, c), 1).astype(F32)
    rel = (ii - jj) * (1.0 - 2.0 * df)
    tri = _bf(jnp.where(rel >= 0.0, 1.0, 0.0))
    mask = rel >= df
    gab = _bf(ga_ref[...])
    z = _dot(gab, wg_ref[...]) + bg_ref[...]
    la = (jnp.minimum(z, 0.0) - jnp.log1p(jnp.exp(-jnp.abs(z)))) * (1.0 / GLA_TAU)
    l1, l2, l3 = _split3(la)
    b = _dot(tri, l1) + _dot(tri, l2) + _dot(tri, l3)
    first = d == 0
    bm = jnp.where(first, b[c // 2:c // 2 + 1], b[c // 2 - 1:c // 2])
    bl = jnp.where(first, b[c - 1:c], b[0:1])
    q = q_ref[...] * (GLA_DK ** -0.5)
    k = k_ref[...]
    e1, e2, e3, eb = jnp.exp(b - bm), jnp.exp(bm - b), jnp.exp(bl - b), jnp.exp(b)
    return dict(tri=tri, mask=mask, gab=gab, z=z, ebl=jnp.exp(bl), e1=e1, e2=e2, e3=e3, eb=eb,
                qc=q * e1, kc=k * e2, kd=k * e3, qe=q * eb, first=first)


def _col_scale(row_vec, width):
    t = jnp.broadcast_to(row_vec, (LANES, LANES)).T
    return jnp.concatenate([t] * (width // LANES), axis=1)


def _gla_fwd(p_odd, wg2, bg2):
    s = p_odd.shape[0]
    c = GLA_C
    n_chunks = s // c
    wk, wv = GLA_H * GLA_DK, GLA_H * GLA_DV
    cidx = _chunk_index(n_chunks)

    def body(q_ref, k_ref, v_ref, ga_ref, wg_ref, bg_ref, o_ref, sp_ref, s_s):
        d = pl.program_id(0)
        n = pl.program_id(1)

        @pl.when(n == 0)
        def _():
            s_s[...] = jnp.zeros_like(s_s)

        g = _gla_common(d, q_ref, k_ref, ga_ref, wg_ref, bg_ref)
        sp_ref[...] = s_s[...]
        for h in range(GLA_H):
            sl = slice(h * GLA_DK, (h + 1) * GLA_DK)
            vs = slice(h * GLA_DV, (h + 1) * GLA_DV)
            vh = _bf(v_ref[:, vs])
            sh = s_s[sl, :]
            a = jnp.where(g["mask"], _dot_nt(_bf(g["qc"][:, sl]), _bf(g["kc"][:, sl])), 0.0)
            o_ref[:, vs] = _dot(_bf(a), vh) + _dot(_bf(g["qe"][:, sl]), _bf(sh))
            s_s[sl, :] = _col_scale(g["ebl"][:, sl], GLA_DV) * sh + _dot_tn(_bf(g["kd"][:, sl]), vh)

    def col(width, blk):
        return pl.BlockSpec((c, width), lambda d, n: (cidx(d, n), blk))

    return pl.pallas_call(
        body, name="gla_fwd", grid=(2, n_chunks),
        in_specs=[col(wk, 0), col(wk, 1), col(wv, 1), col(LANES, OD_GA_BLK),
                  pl.BlockSpec((None, LANES, wk), lambda d, n: (d, 0, 0)),
                  pl.BlockSpec((None, 1, wk), lambda d, n: (d, 0, 0))],
        out_specs=[pl.BlockSpec((None, c, wv), lambda d, n: (d, cidx(d, n), 0)),
                   pl.BlockSpec((None, None, wk, GLA_DV), lambda d, n: (d, cidx(d, n), 0, 0))],
        out_shape=[jax.ShapeDtypeStruct((2, s, wv), F32), jax.ShapeDtypeStruct((2, n_chunks, wk, GLA_DV), F32)],
        scratch_shapes=[pltpu.VMEM((wk, GLA_DV), F32)],
        compiler_params=_cp(("arbitrary", "arbitrary")),
    )(p_odd, p_odd, p_odd, p_odd, wg2, bg2)


def _gla_bwd(p_odd, wg2, bg2, s_prev, do):
    s = p_odd.shape[0]
    c = GLA_C
    n_chunks = s // c
    wk, wv = GLA_H * GLA_DK, GLA_H * GLA_DV
    fwd_idx = _chunk_index(n_chunks)

    def cidx(d, n):
        return fwd_idx(d, n_chunks - 1 - n)

    def body(q_ref, k_ref, v_ref, ga_ref, wg_ref, bg_ref, sp_ref, do_ref,
             dq_ref, dk_ref, dv_ref, dga_ref, dwg_ref, dbg_ref, ds_s):
        d = pl.program_id(0)
        n = pl.program_id(1)

        @pl.when(n == 0)
        def _():
            ds_s[...] = jnp.zeros_like(ds_s)
            dwg_ref[...] = jnp.zeros_like(dwg_ref)
            dbg_ref[...] = jnp.zeros_like(dbg_ref)

        g = _gla_common(d, q_ref, k_ref, ga_ref, wg_ref, bg_ref)
        mask = g["mask"]
        ones8 = jnp.ones((8, GLA_DV), BF16)
        dbs, dbms, dbls = [], [], []
        for h in range(GLA_H):
            sl = slice(h * GLA_DK, (h + 1) * GLA_DK)
            vs = slice(h * GLA_DV, (h + 1) * GLA_DV)
            qc, kc, kd, qe = g["qc"][:, sl], g["kc"][:, sl], g["kd"][:, sl], g["qe"][:, sl]
            qcb, kcb, kdb, qeb = _bf(qc), _bf(kc), _bf(kd), _bf(qe)
            vh = _bf(v_ref[:, vs])
            doh = _bf(do_ref[:, vs])
            sp = sp_ref[sl, :]
            dsn = ds_s[sl, :]
            dsb = _bf(dsn)
            a = _bf(jnp.where(mask, _dot_nt(qcb, kcb), 0.0))
            da = _bf(jnp.where(mask, _dot_nt(doh, vh), 0.0))
            dv_ref[:, vs] = _dot_tn(a, doh) + _dot(kdb, dsb)
            dqc = _dot(da, kcb)
            dkc = _dot_tn(da, qcb)
            dqe = _dot_nt(doh, _bf(sp))
            dkd = _dot_nt(vh, dsb)
            ds_s[sl, :] = _dot_tn(qeb, doh) + _col_scale(g["ebl"][:, sl], GLA_DV) * dsn
            dq_ref[:, sl] = (dqc * g["e1"][:, sl] + dqe * g["eb"][:, sl]) * (GLA_DK ** -0.5)
            dk_ref[:, sl] = dkc * g["e2"][:, sl] + dkd * g["e3"][:, sl]
            t1, t2, t3, t4 = dqc * qc, dkc * kc, dqe * qe, dkd * kd
            dbs.append(t1 - t2 + t3 - t4)
            dbms.append(_rowsum(t2 - t1))
            m1, m2, _ = _split3(dsn * sp)
            rs = (_dot_nt(ones8, m1) + _dot_nt(ones8, m2))[0:1]
            dbls.append(_rowsum(t4) + g["ebl"][:, sl] * rs)
        db = jnp.concatenate(dbs, axis=1)
        dbm = jnp.concatenate(dbms, axis=1)
        dbl = jnp.concatenate(dbls, axis=1)
        row = lax.broadcasted_iota(jnp.int32, (c, wk), 0)
        mid = jnp.where(g["first"], c // 2, c // 2 - 1)
        last = jnp.where(g["first"], c - 1, 0)
        db = db + jnp.where(row == mid, dbm, 0.0) + jnp.where(row == last, dbl, 0.0)
        d1, d2, d3 = _split3(db)
        tri = g["tri"]
        dla = _dot_tn(tri, d1) + _dot_tn(tri, d2) + _dot_tn(tri, d3)
        dz = dla * (1.0 / GLA_TAU) * (1.0 - _sigmoid(g["z"]))
        dzb = _bf(dz)
        dga_ref[...] = _dot_nt(dzb, wg_ref[...])
        dwg_ref[...] += _dot_tn(g["gab"], dzb)
        dbg_ref[...] += _rowsum(dz)

    def col(width, blk):
        return pl.BlockSpec((c, width), lambda d, n: (cidx(d, n), blk))

    def dirrow(width):
        return pl.BlockSpec((None, c, width), lambda d, n: (d, cidx(d, n), 0))

    return pl.pallas_call(
        body, name="gla_bwd", grid=(2, n_chunks),
        in_specs=[col(wk, 0), col(wk, 1), col(wv, 1), col(LANES, OD_GA_BLK),
                  pl.BlockSpec((None, LANES, wk), lambda d, n: (d, 0, 0)),
                  pl.BlockSpec((None, 1, wk), lambda d, n: (d, 0, 0)),
                  pl.BlockSpec((None, None, wk, GLA_DV), lambda d, n: (d, cidx(d, n), 0, 0)),
                  pl.BlockSpec((c, wv), lambda d, n: (cidx(d, n), 0))],
        out_specs=[dirrow(wk), dirrow(wk), dirrow(wv), dirrow(LANES),
                   pl.BlockSpec((None, LANES, wk), lambda d, n: (d, 0, 0)),
                   pl.BlockSpec((None, 1, wk), lambda d, n: (d, 0, 0))],
        out_shape=[jax.ShapeDtypeStruct((2, s, wk), F32), jax.ShapeDtypeStruct((2, s, wk), F32),
                   jax.ShapeDtypeStruct((2, s, wv), F32), jax.ShapeDtypeStruct((2, s, LANES), F32),
                   jax.ShapeDtypeStruct((2, LANES, wk), F32), jax.ShapeDtypeStruct((2, 1, wk), F32)],
        scratch_shapes=[pltpu.VMEM((wk, GLA_DV), F32)],
        compiler_params=_cp(("arbitrary", "arbitrary")),
    )(p_odd, p_odd, p_odd, p_odd, wg2, bg2, s_prev, do)


HALO = 8


def _halo_specs(width_blk, col0, ts, s):
    r = ts // HALO
    last = s // HALO - 1
    cur = pl.BlockSpec((ts, width_blk), lambda j, i: (i, col0 + j))
    prev = pl.BlockSpec((HALO, width_blk), lambda j, i: (jnp.maximum(i * r - 1, 0), col0 + j))
    nxt = pl.BlockSpec((HALO, width_blk), lambda j, i: (jnp.minimum((i + 1) * r, last), col0 + j))
    return [prev, cur, nxt]


def _with_halo(prev_ref, cur_ref, next_ref, i, n_i):
    p = jnp.where(i == 0, 0.0, prev_ref[...])
    q = jnp.where(i == n_i - 1, 0.0, next_ref[...])
    return jnp.concatenate([p, cur_ref[...], q], axis=0)


def _shift_down(x):
    return pltpu.roll(x, 1, 0)


def _shift_up(x):
    return pltpu.roll(x, x.shape[0] - 1, 0)


def _ffn_act(up, conv_w, conv_b, *, ts):
    s = up.shape[0]
    tc = _tile(D_FF, 1408)
    nj = D_FF // tc
    n_i = s // ts

    def body(gp, gc, gn, val_ref, w_ref, b_ref, a_ref):
        i = pl.program_id(1)
        g = _with_halo(gp, gc, gn, i, n_i)
        w = w_ref[...]
        conv = w[0:1] * _shift_down(g) + w[1:2] * g + w[2:3] * _shift_up(g) + b_ref[...]
        conv = conv[HALO:HALO + ts]
        a_ref[...] = (conv * _sigmoid(conv) * val_ref[...]).astype(a_ref.dtype)

    return pl.pallas_call(
        body, name="ffn_act", grid=(nj, n_i),
        in_specs=_halo_specs(tc, 0, ts, s) + [pl.BlockSpec((ts, tc), lambda j, i: (i, nj + j)),
                                              pl.BlockSpec((3, tc), lambda j, i: (0, j)),
                                              pl.BlockSpec((1, tc), lambda j, i: (0, j))],
        out_specs=pl.BlockSpec((ts, tc), lambda j, i: (i, j)),
        out_shape=jax.ShapeDtypeStruct((s, D_FF), BF16),
        compiler_params=_cp(("parallel", "arbitrary")),
    )(up, up, up, up, conv_w, conv_b)


def _ffn_act_bwd(up, da, conv_w, conv_b, *, ts):
    s = up.shape[0]
    tc = _tile(D_FF, 1408)
    nj = D_FF // tc
    n_i = s // ts

    def body(gp, gc, gn, vp, vc, vn, dp, dc, dn, w_ref, b_ref, dg_ref, dval_ref, dw_ref, db_ref):
        i = pl.program_id(1)
        g = _with_halo(gp, gc, gn, i, n_i)
        v = _with_halo(vp, vc, vn, i, n_i)
        dav = _with_halo(dp, dc, dn, i, n_i)
        w = w_ref[...]
        gm, gpl = _shift_down(g), _shift_up(g)
        conv = w[0:1] * gm + w[1:2] * g + w[2:3] * gpl + b_ref[...]
        sg = _sigmoid(conv)
        dgc = dav * v * (sg * (1.0 + conv * (1.0 - sg)))
        dgate = w[0:1] * _shift_up(dgc) + w[1:2] * dgc + w[2:3] * _shift_down(dgc)
        ctr = slice(HALO, HALO + ts)
        dg_ref[...] = dgate[ctr].astype(dg_ref.dtype)
        dval_ref[...] = (dav[ctr] * (conv * sg)[ctr]).astype(dval_ref.dtype)
        dgc_c = dgc[ctr]
        dw = jnp.concatenate([_rowsum(dgc_c * gm[ctr]), _rowsum(dgc_c * g[ctr]), _rowsum(dgc_c * gpl[ctr])], axis=0)
        dbv = _rowsum(dgc_c)

        @pl.when(i == 0)
        def _():
            dw_ref[...] = dw
            db_ref[...] = dbv

        @pl.when(i > 0)
        def _():
            dw_ref[...] += dw
            db_ref[...] += dbv

    tile = pl.BlockSpec((ts, tc), lambda j, i: (i, j))
    return pl.pallas_call(
        body, name="ffn_act_bwd", grid=(nj, n_i),
        in_specs=(_halo_specs(tc, 0, ts, s) + _halo_specs(tc, nj, ts, s) + _halo_specs(tc, 0, ts, s)
                  + [pl.BlockSpec((3, tc), lambda j, i: (0, j)), pl.BlockSpec((1, tc), lambda j, i: (0, j))]),
        out_specs=[tile, tile, pl.BlockSpec((3, tc), lambda j, i: (0, j)), pl.BlockSpec((1, tc), lambda j, i: (0, j))],
        out_shape=[jax.ShapeDtypeStruct((s, D_FF), BF16), jax.ShapeDtypeStruct((s, D_FF), BF16),
                   jax.ShapeDtypeStruct((3, D_FF), F32), jax.ShapeDtypeStruct((1, D_FF), F32)],
        compiler_params=_cp(("parallel", "arbitrary")),
    )(up, up, up, up, up, up, da, da, da, conv_w, conv_b)


def _loss_head(y, target, *, s, ts):
    def fn(yv, tv):
        err = yv - tv
        return err * (1.0 / D_MODEL), _rowsum(err * err)

    return _ew(fn, [_cols(y, D_MODEL, 0, ts), _cols(target, D_MODEL, 0, ts)], [], [(D_MODEL, F32)],
               [(1, D_MODEL)], s=s, ts=ts, name="loss_head")


def _adamw(w, g, m, v, *, name):
    r = w.shape[0]
    ts = r if r <= 512 else 512
    assert r % ts == 0

    def fn(wv, gv, mv, vv):
        mn = ADAM_B1 * mv + (1.0 - ADAM_B1) * gv
        vn = ADAM_B2 * vv + (1.0 - ADAM_B2) * (gv * gv)
        m_hat = mn / (1.0 - ADAM_B1 ** ADAM_STEP)
        v_hat = vn / (1.0 - ADAM_B2 ** ADAM_STEP)
        delta = -ADAM_LR * (m_hat / (jnp.sqrt(v_hat) + ADAM_EPS) + ADAM_WD * wv)
        return delta, mn, vn

    rows = [_cols(a, LANES * 8, 0, ts) for a in (w, g, m, v)]
    return _ew(fn, rows, [], [(LANES * 8, F32)] * 3, s=r, ts=ts, name=name)


def _pad_heads(w, heads, real):
    lead = w.shape[:-1]
    w = w.reshape(lead + (heads, real))
    w = jnp.pad(w, [(0, 0)] * len(lead) + [(0, 0), (0, LANES - real)])
    return w.reshape(lead + (heads * LANES,))


def _pad_head_rows(w, heads, real):
    return _pad_heads(w.T, heads, real).T


def _pack_even(p):
    w_in = p["w_in"]
    z = lambda n: jnp.zeros((D_MODEL, n), w_in.dtype)
    o = 0
    parts = {}
    for nm, n in (("cq", MLA_QR), ("ckv", MLA_KVR), ("kr", MLA_ROPE), ("rq", 512), ("rk", 512), ("rv", 512), ("rg", 512)):
        parts[nm] = w_in[:, o:o + n]
        o += n
    w_in_p = jnp.concatenate(
        [_pad_heads(parts[k], RET_H, RET_DK) for k in ("rq", "rk", "rv", "rg")]
        + [parts["cq"], z(EV_CQ - MLA_QR), parts["ckv"], z(MLA_NOPE), parts["kr"], z(LANES - MLA_QK), z(LANES)], axis=1)
    w_uq = jnp.pad(_pad_heads(p["w_uq"], MLA_H, MLA_QK), ((0, EV_CQ - MLA_QR), (0, 0)))
    ukv = p["w_ukv"].reshape(MLA_KVR, MLA_H, MLA_NOPE + MLA_V)
    w_ukv = jnp.concatenate([_pad_heads(ukv[..., :MLA_NOPE].reshape(MLA_KVR, -1), MLA_H, MLA_NOPE),
                             _pad_heads(ukv[..., MLA_NOPE:].reshape(MLA_KVR, -1), MLA_H, MLA_V)], axis=1)
    w_out = jnp.concatenate([_pad_head_rows(p["w_out"][:MLA_H * MLA_V], MLA_H, MLA_V),
                             _pad_head_rows(p["w_out"][MLA_H * MLA_V:], RET_H, RET_DV)], axis=0)
    return dict(
        w_in=w_in_p, w_uq=w_uq, w_ukv=w_ukv, w_out=w_out,
        mix_g=p["mix_norm"][None, :],
        q_norm=jnp.pad(p["q_norm"], (0, EV_CQ - MLA_QR))[None, :],
        kv_norm=p["kv_norm"][None, :],
        qhn=jnp.pad(p["q_head_norm"], (0, LANES - MLA_QK))[None, :],
        khn=jnp.pad(p["k_head_norm"], (0, LANES - MLA_QK))[None, :],
        ret_gain=_pad_heads(p["ret_out_norm"].reshape(-1), RET_H, RET_DV)[None, :],
    )


def _pack_odd(p):
    w_in = p["w_in"]
    ga = w_in[:, 3072:]
    w_in_p = jnp.concatenate([w_in[:, :3072], ga, jnp.zeros((D_MODEL, LANES - 2 * GLA_R), w_in.dtype)], axis=1)
    wk = GLA_H * GLA_DK
    zf = jnp.zeros((LANES - GLA_R, wk), p["w_gate_fwd"].dtype)
    zb0 = jnp.zeros((GLA_R, wk), p["w_gate_fwd"].dtype)
    zb1 = jnp.zeros((LANES - 2 * GLA_R, wk), p["w_gate_fwd"].dtype)
    wg2 = jnp.stack([jnp.concatenate([p["w_gate_fwd"], zf], axis=0),
                     jnp.concatenate([zb0, p["w_gate_bwd"], zb1], axis=0)])
    bg2 = jnp.stack([p["b_gate_fwd"][None, :], p["b_gate_bwd"][None, :]])
    return dict(w_in=w_in_p, wg2=wg2, bg2=bg2, w_out=p["w_out"], mix_g=p["mix_norm"][None, :],
                gla_gain=p["gla_out_norm"].reshape(1, -1))


def _pack_ffn(p):
    return dict(w_up=p["w_up"], w_down=p["w_down"], conv_w=p["conv_w"], conv_b=p["conv_b"][None, :],
                norm_g=p["norm"][None, :])


_MATRICES = ("w_in", "w_uq", "w_ukv", "w_out", "wg2", "w_up", "w_down")


def _packed(pack_fn, p):
    packed = pack_fn(p)
    packed = {k: (_bf(v) if k in _MATRICES else v.astype(F32)) for k, v in packed.items()}
    shapes = {k: jax.ShapeDtypeStruct(v.shape, F32) for k, v in p.items()}
    unpack = jax.linear_transpose(pack_fn, shapes)
    return packed, lambda g: unpack(g)[0]


def _ffn_fwd(x, w, *, s, ts):
    h = _rmsnorm(_cols(x, D_MODEL, 0, ts), w["norm_g"], n=D_MODEL, s=s, ts=ts, name="ffn_norm")
    up = _mm(h, w["w_up"], name="ffn_up")
    a = _ffn_act(up, w["conv_w"], w["conv_b"], ts=ts)
    y = _mm(a, w["w_down"], res=x, name="ffn_down")
    return y, dict(x=x, h=h, up=up, a=a)


def _ffn_bwd(dy, w, sv, *, s, ts):
    da = _mm(dy, w["w_down"], tb=True, name="ffn_down_dx")
    g_down = _mm(sv["a"], dy, ta=True, name="ffn_down_dw")
    dgate, dval, g_cw, g_cb = _ffn_act_bwd(sv["up"], da, w["conv_w"], w["conv_b"], ts=ts)
    dup = jnp.concatenate([dgate, dval], axis=1)
    dh = _mm(dup, w["w_up"], tb=True, name="ffn_up_dx")
    g_up = _mm(sv["h"], dup, ta=True, name="ffn_up_dw")
    dx, g_norm = _rmsnorm_bwd(_cols(sv["x"], D_MODEL, 0, ts), w["norm_g"], dh, dy, n=D_MODEL, s=s, ts=ts,
                              name="ffn_norm_bwd")
    return dx, dict(w_up=g_up, w_down=g_down, conv_w=g_cw, conv_b=g_cb, norm_g=g_norm)


def _flash_tiles(s):
    return min(s, 512), min(s, 1024)


def _even_fwd(x, w, tabs, *, s, ts):
    cos_m, sin_m, cos_r, sin_r = tabs
    h = _rmsnorm(_cols(x, D_MODEL, 0, ts), w["mix_g"], n=D_MODEL, s=s, ts=ts, name="mix_norm")
    p = _mm(h, w["w_in"], name="even_in")
    cqn = _rmsnorm(_cols(p, EV_CQ, EV_RET // EV_CQ, ts), w["q_norm"], n=MLA_QR, s=s, ts=ts, name="mla_q_norm")
    ckvn = _rmsnorm(_cols(p, MLA_KVR, (EV_RET + EV_CQ) // MLA_KVR, ts), w["kv_norm"], n=MLA_KVR, s=s, ts=ts,
                    name="mla_kv_norm")
    q_pre = _mm(cqn, w["w_uq"], name="mla_uq")
    kv_pre = _mm(ckvn, w["w_ukv"], name="mla_ukv")
    q, k, v = _mla_prep(q_pre, kv_pre, p, cos_m, sin_m, w["qhn"], w["khn"], s=s, ts=ts)
    tq, tk = _flash_tiles(s)
    o, lse = _flash_fwd(q, k, v, tq=tq, tk=tk)
    o2, r_prev = _ret_fwd(p, cos_r, sin_r, w["theta_l"])
    r = _post_fwd(o2, _cols(p, RET_H * LANES, 3, ts), w["ret_gain"], group=LANES, n=RET_DV, s=s, ts=ts,
                  name="ret_post")
    ar = jnp.concatenate([o, r], axis=1)
    y = _mm(ar, w["w_out"], res=x, name="even_out")
    return y, dict(x=x, h=h, p=p, cqn=cqn, ckvn=ckvn, q_pre=q_pre, kv_pre=kv_pre, q=q, k=k, v=v, o=o, lse=lse,
                   o2=o2, r_prev=r_prev, ar=ar)


def _even_bwd(dy, w, sv, tabs, *, s, ts):
    cos_m, sin_m, cos_r, sin_r = tabs
    p = sv["p"]
    wh = MLA_H * LANES
    dar = _mm(dy, w["w_out"], tb=True, name="even_out_dx")
    g_out = _mm(sv["ar"], dy, ta=True, name="even_out_dw")
    tq, tk = _flash_tiles(s)
    do_attn = _attn_bwd_prep(dar, sv["o"], s=s, ts=ts)
    dq, dk, dv = _flash_bwd(sv["q"], sv["k"], sv["v"], do_attn, sv["lse"], tq=tq, tk=tk)
    dq_pre, dk_pre, dkr, g_qhn, g_khn = _mla_prep_bwd(sv["q_pre"], sv["kv_pre"], p, cos_m, sin_m, w["qhn"], w["khn"],
                                                      dq, dk, s=s, ts=ts)
    dkv_pre = jnp.concatenate([dk_pre, dv], axis=1)
    dckvn = _mm(dkv_pre, w["w_ukv"], tb=True, name="mla_ukv_dx")
    g_ukv = _mm(sv["ckvn"], dkv_pre, ta=True, name="mla_ukv_dw")
    dcqn = _mm(dq_pre, w["w_uq"], tb=True, name="mla_uq_dx")
    g_uq = _mm(sv["cqn"], dq_pre, ta=True, name="mla_uq_dw")
    dckv, g_kvn = _rmsnorm_bwd(_cols(p, MLA_KVR, (EV_RET + EV_CQ) // MLA_KVR, ts), w["kv_norm"], dckvn, None,
                               n=MLA_KVR, s=s, ts=ts, name="mla_kv_norm_bwd")
    dcq, g_qn = _rmsnorm_bwd(_cols(p, EV_CQ, EV_RET // EV_CQ, ts), w["q_norm"], dcqn, None, n=MLA_QR, s=s, ts=ts,
                             name="mla_q_norm_bwd")
    do, drg, g_gain = _post_bwd(sv["o2"], _cols(p, wh, 3, ts), w["ret_gain"], _cols(dar, wh, 1, ts),
                                group=LANES, n=RET_DV, s=s, ts=ts, name="ret_post_bwd")
    dq2, dk2, dv2, dth = _ret_bwd(p, cos_r, sin_r, w["theta_l"], w["theta_h"], sv["r_prev"], do)
    drq, drk, drv = (_sum2(a, s=s, ts=ts, name="sum_dirs_1024") for a in (dq2, dk2, dv2))
    dp = jnp.concatenate([drq, drk, drv, drg, _bf(dcq), _bf(dckv), dkr, jnp.zeros((s, LANES), BF16)], axis=1)
    dh = _mm(dp, w["w_in"], tb=True, name="even_in_dx")
    g_in = _mm(sv["h"], dp, ta=True, name="even_in_dw")
    dx, g_mix = _rmsnorm_bwd(_cols(sv["x"], D_MODEL, 0, ts), w["mix_g"], dh, dy, n=D_MODEL, s=s, ts=ts,
                             name="mix_norm_bwd")
    grads = dict(w_in=g_in, w_uq=g_uq, w_ukv=g_ukv, w_out=g_out, mix_g=g_mix, q_norm=g_qn, kv_norm=g_kvn,
                 qhn=g_qhn, khn=g_khn, ret_gain=g_gain)
    return dx, grads, dth[:, :, 0]


def _odd_fwd(x, w, *, s, ts):
    h = _rmsnorm(_cols(x, D_MODEL, 0, ts), w["mix_g"], n=D_MODEL, s=s, ts=ts, name="mix_norm")
    p = _mm(h, w["w_in"], name="odd_in")
    o2, s_prev = _gla_fwd(p, w["wg2"], w["bg2"])
    g = _post_fwd(o2, _cols(p, GLA_H * GLA_DV, 2, ts), w["gla_gain"], group=GLA_DV, n=GLA_DV, s=s, ts=ts,
                  name="gla_post")
    y = _mm(g, w["w_out"], res=x, name="odd_out")
    return y, dict(x=x, h=h, p=p, o2=o2, s_prev=s_prev, g=g)


def _odd_bwd(dy, w, sv, *, s, ts):
    p = sv["p"]
    wv = GLA_H * GLA_DV
    dg = _mm(dy, w["w_out"], tb=True, name="odd_out_dx")
    g_out = _mm(sv["g"], dy, ta=True, name="odd_out_dw")
    do, dgr, g_gain = _post_bwd(sv["o2"], _cols(p, wv, 2, ts), w["gla_gain"], _cols(dg, wv, 0, ts),
                                group=GLA_DV, n=GLA_DV, s=s, ts=ts, name="gla_post_bwd")
    dq2, dk2, dv2, dga2, g_wg, g_bg = _gla_bwd(p, w["wg2"], w["bg2"], sv["s_prev"], do)
    dq = _sum2(dq2, s=s, ts=ts, name="sum_dirs_512")
    dk = _sum2(dk2, s=s, ts=ts, name="sum_dirs_512")
    dv = _sum2(dv2, s=s, ts=ts, name="sum_dirs_1024")
    dga = _sum2(dga2, s=s, ts=ts, name="sum_dirs_128")
    dp = jnp.concatenate([dq, dk, dv, dgr, dga], axis=1)
    dh = _mm(dp, w["w_in"], tb=True, name="odd_in_dx")
    g_in = _mm(sv["h"], dp, ta=True, name="odd_in_dw")
    dx, g_mix = _rmsnorm_bwd(_cols(sv["x"], D_MODEL, 0, ts), w["mix_g"], dh, dy, n=D_MODEL, s=s, ts=ts,
                             name="mix_norm_bwd")
    return dx, dict(w_in=g_in, wg2=g_wg, bg2=g_bg, w_out=g_out, mix_g=g_mix, gla_gain=g_gain)


_EVEN_NAMES = dict(mix_norm="mix_norm_even", w_in="w_in_even", q_norm="mla_q_norm", kv_norm="mla_kv_norm",
                   w_uq="mla_w_uq", w_ukv="mla_w_ukv", q_head_norm="mla_q_head_norm", k_head_norm="mla_k_head_norm",
                   ret_out_norm="ret_out_norm", w_out="w_out_even")
_ODD_NAMES = dict(mix_norm="mix_norm_odd", w_in="w_in_odd", w_gate_fwd="gla_w_gate_fwd", b_gate_fwd="gla_b_gate_fwd",
                  w_gate_bwd="gla_w_gate_bwd", b_gate_bwd="gla_b_gate_bwd", gla_out_norm="gla_out_norm",
                  w_out="w_out_odd")
_FFN_NAMES = dict(norm="ffn_norm", w_up="ffn_w_up", conv_w="ffn_conv_w", conv_b="ffn_conv_b", w_down="ffn_w_down")


def _local_step(x, pos, target, full):
    s = x.shape[0]
    ts = min(s, 256)
    tabs = _rope_tables(pos, MLA_ROPE, MLA_NOPE) + _rope_tables(pos, RET_DK, 0)
    layers = []
    for layer in range(DEPTH):
        i = layer // 2
        names = _EVEN_NAMES if layer % 2 == 0 else _ODD_NAMES
        wm, unpack_m = _packed(_pack_even if layer % 2 == 0 else _pack_odd, {k: full[n][i] for k, n in names.items()})
        if layer % 2 == 0:
            th = jnp.stack([full["ret_theta_fwd"][i], full["ret_theta_bwd"][i]]).astype(F32)
            wm["theta_h"] = jnp.broadcast_to(th[:, :, None], (2, RET_H, LANES))
            wm["theta_l"] = wm["theta_h"].reshape(2, 1, RET_H * LANES)
        wf, unpack_f = _packed(_pack_ffn, {k: full[n][layer] for k, n in _FFN_NAMES.items()})
        layers.append((wm, unpack_m, wf, unpack_f))

    saved = []
    for layer, (wm, _, wf, _) in enumerate(layers):
        if layer % 2 == 0:
            x, sv_m = _even_fwd(x, wm, tabs, s=s, ts=ts)
        else:
            x, sv_m = _odd_fwd(x, wm, s=s, ts=ts)
        x, sv_f = _ffn_fwd(x, wf, s=s, ts=ts)
        saved.append((sv_m, sv_f))

    dy, sq = _loss_head(x, target, s=s, ts=ts)
    loss = 0.5 / D_MODEL * jnp.sum(sq)

    grads = {}

    def put(name, idx, g):
        grads.setdefault(name, {})[idx] = g

    for layer in reversed(range(DEPTH)):
        wm, unpack_m, wf, unpack_f = layers[layer]
        sv_m, sv_f = saved[layer]
        i = layer // 2
        dy, gf = _ffn_bwd(dy, wf, sv_f, s=s, ts=ts)
        for k, g in unpack_f(gf).items():
            put(_FFN_NAMES[k], layer, g)
        if layer % 2 == 0:
            dy, gm, dth = _even_bwd(dy, wm, sv_m, tabs, s=s, ts=ts)
            put("ret_theta_fwd", i, dth[0])
            put("ret_theta_bwd", i, dth[1])
            names = _EVEN_NAMES
        else:
            dy, gm = _odd_bwd(dy, wm, sv_m, s=s, ts=ts)
            names = _ODD_NAMES
        for k, g in unpack_m(gm).items():
            put(names[k], i, g)
    out = {n: jnp.stack([g[j] for j in range(len(g))]) for n, g in grads.items()}
    return loss, dy, out


HBM_SPEC = pl.BlockSpec(memory_space=pltpu.HBM)
VMEM_SPEC = pl.BlockSpec(memory_space=pltpu.VMEM)
CHIPS = 4
CORES = 2
ROW = 8 * LANES


def _xyc():
    return lax.axis_index("x"), lax.axis_index("y"), lax.axis_index("c")


def _other_chips(x, y):
    return [(1 - x, y), (x, 1 - y), (1 - x, 1 - y)]


def _remote(src, dst, send, recv, dev):
    return pltpu.make_async_remote_copy(src_ref=src, dst_ref=dst, send_sem=send, recv_sem=recv,
                                        device_id=dev, device_id_type=MESH)


def _gather_chips(big, small):
    def body(b_ref, s_ref, ob_ref, os_ref, send, recv, loc):
        x, y, c = _xyc()
        me = 2 * x + y
        pairs = ((b_ref, ob_ref), (s_ref, os_ref))
        local = [pltpu.make_async_copy(src, dst.at[me], loc.at[t]) for t, (src, dst) in enumerate(pairs)]
        for cp in local:
            cp.start()
        sends = []
        for j, (px, py) in enumerate(_other_chips(x, y)):
            for t, (src, dst) in enumerate(pairs):
                cp = _remote(src, dst.at[me], send.at[2 * j + t], recv.at[2 * j + t], (px, py, c))
                cp.start()
                sends.append(cp)
        for j, (px, py) in enumerate(_other_chips(x, y)):
            for t, (src, dst) in enumerate(pairs):
                _remote(src, dst.at[2 * px + py], send.at[2 * j + t], recv.at[2 * j + t], (px, py, c)).wait_recv()
        for cp in sends:
            cp.wait_send()
        for cp in local:
            cp.wait()

    return pl.pallas_call(
        body, name="gather_chips", in_specs=[HBM_SPEC, HBM_SPEC], out_specs=[HBM_SPEC, HBM_SPEC],
        out_shape=[jax.ShapeDtypeStruct((CHIPS,) + big.shape, big.dtype),
                   jax.ShapeDtypeStruct((CHIPS,) + small.shape, small.dtype)],
        scratch_shapes=[pltpu.SemaphoreType.DMA((6,)), pltpu.SemaphoreType.DMA((6,)), pltpu.SemaphoreType.DMA((2,))],
    )(big, small)


def _swap_cores(v):
    def body(v_ref, o_ref, send, recv):
        x, y, c = _xyc()
        cp = _remote(v_ref, o_ref, send, recv, (x, y, 1 - c))
        cp.start()
        cp.wait()

    return pl.pallas_call(
        body, name="swap_cores", in_specs=[HBM_SPEC], out_specs=HBM_SPEC,
        out_shape=jax.ShapeDtypeStruct(v.shape, v.dtype),
        scratch_shapes=[pltpu.SemaphoreType.DMA, pltpu.SemaphoreType.DMA],
    )(v)


def _scatter_chips(v):
    def body(v_ref, o_ref, send, recv, loc):
        x, y, c = _xyc()
        me = 2 * x + y
        local = pltpu.make_async_copy(v_ref.at[me], o_ref.at[me], loc)
        local.start()
        sends = []
        for j, (px, py) in enumerate(_other_chips(x, y)):
            cp = _remote(v_ref.at[2 * px + py], o_ref.at[me], send.at[j], recv.at[j], (px, py, c))
            cp.start()
            sends.append(cp)
        for j, (px, py) in enumerate(_other_chips(x, y)):
            _remote(v_ref.at[me], o_ref.at[2 * px + py], send.at[j], recv.at[j], (px, py, c)).wait_recv()
        for cp in sends:
            cp.wait_send()
        local.wait()

    return pl.pallas_call(
        body, name="scatter_chips", in_specs=[HBM_SPEC], out_specs=HBM_SPEC,
        out_shape=jax.ShapeDtypeStruct(v.shape, v.dtype),
        scratch_shapes=[pltpu.SemaphoreType.DMA((3,)), pltpu.SemaphoreType.DMA((3,)), pltpu.SemaphoreType.DMA],
    )(v)


def _gather_cores(v):
    def body(v_ref, o_ref, send, recv, loc):
        x, y, c = _xyc()
        local = pltpu.make_async_copy(v_ref, o_ref.at[c], loc)
        local.start()
        cp = _remote(v_ref, o_ref.at[c], send, recv, (x, y, 1 - c))
        cp.start()
        _remote(v_ref, o_ref.at[1 - c], send, recv, (x, y, 1 - c)).wait_recv()
        cp.wait_send()
        local.wait()

    return pl.pallas_call(
        body, name="gather_cores", in_specs=[HBM_SPEC], out_specs=HBM_SPEC,
        out_shape=jax.ShapeDtypeStruct((CORES,) + v.shape, v.dtype),
        scratch_shapes=[pltpu.SemaphoreType.DMA, pltpu.SemaphoreType.DMA, pltpu.SemaphoreType.DMA],
    )(v)


def _all_reduce_devices(v):
    n_dev = CHIPS * CORES

    def body(v_ref, o_ref, buf, send, recv):
        x, y, c = _xyc()
        me = 4 * x + 2 * y + c
        buf[pl.ds(me, 1)] = v_ref[...][None]
        sends = []
        for m in range(1, n_dev):
            px = 1 - x if m & 4 else x
            py = 1 - y if m & 2 else y
            pc = 1 - c if m & 1 else c
            cp = _remote(v_ref, buf.at[me], send.at[m - 1], recv.at[m - 1], (px, py, pc))
            cp.start()
            sends.append((cp, 4 * px + 2 * py + pc))
        for m, (cp, peer) in enumerate(sends):
            _remote(v_ref, buf.at[peer], send.at[m], recv.at[m], (x, y, c)).wait_recv()
        for cp, _ in sends:
            cp.wait_send()
        acc = buf[0]
        for k in range(1, n_dev):
            acc = acc + buf[k]
        o_ref[...] = acc

    return pl.pallas_call(
        body, name="all_reduce_devices", in_specs=[VMEM_SPEC], out_specs=VMEM_SPEC,
        out_shape=jax.ShapeDtypeStruct(v.shape, F32),
        scratch_shapes=[pltpu.VMEM((n_dev,) + v.shape, F32), pltpu.SemaphoreType.DMA((n_dev - 1,)),
                        pltpu.SemaphoreType.DMA((n_dev - 1,))],
    )(v)


def _add_rows(arrs, *, name):
    r = arrs[0].shape[0]
    ts = r if r <= 512 else 512
    assert r % ts == 0

    def fn(*vals):
        acc = vals[0]
        for v in vals[1:]:
            acc = acc + v
        return acc

    return _ew(fn, [_cols(a, ROW, 0, ts) for a in arrs], [], [(ROW, F32)], s=r, ts=ts, name=name)[0]


_SHARDED = (("w_in_even", 2), ("mla_w_uq", 2), ("mla_w_ukv", 2), ("w_out_even", 1), ("w_in_odd", 2), ("w_out_odd", 1),
            ("ffn_w_up", 2), ("ffn_w_down", 1),
            ("mix_norm_odd", 1), ("gla_w_gate_fwd", 2), ("gla_b_gate_fwd", 1), ("gla_w_gate_bwd", 2),
            ("gla_b_gate_bwd", 1), ("gla_out_norm", 2), ("ffn_conv_w", 2))
_N_MATRICES = 8
_REPLICATED = ("mix_norm_even", "mla_q_norm", "mla_kv_norm", "mla_q_head_norm", "mla_k_head_norm", "ret_theta_fwd",
               "ret_theta_bwd", "ret_out_norm", "ffn_norm", "ffn_conv_b")
_WEIGHTS = ("mix_norm_even", "w_in_even", "mla_q_norm", "mla_kv_norm", "mla_w_uq", "mla_w_ukv", "mla_q_head_norm",
            "mla_k_head_norm", "ret_theta_fwd", "ret_theta_bwd", "ret_out_norm", "w_out_even", "mix_norm_odd",
            "w_in_odd", "gla_w_gate_fwd", "gla_b_gate_fwd", "gla_w_gate_bwd", "gla_b_gate_bwd", "gla_out_norm",
            "w_out_odd", "ffn_norm", "ffn_w_up", "ffn_conv_w", "ffn_conv_b", "ffn_w_down")
_GRAD_ROWS = 2 * 512


def _flatten(arrs, row_multiple, dtype):
    flat = jnp.concatenate([a.reshape(-1).astype(dtype) for a in arrs])
    per = ROW * row_multiple
    total = -(-flat.shape[0] // per) * per
    return jnp.pad(flat, (0, total - flat.shape[0])).reshape(-1, ROW)


def _unflatten(flat, shapes):
    flat = flat.reshape(-1)
    out, o = [], 0
    for shp in shapes:
        n = math.prod(shp)
        out.append(flat[o:o + n].reshape(shp))
        o += n
    return out


def kernel(x, positions, mix_norm_even, w_in_even, mla_q_norm, mla_kv_norm, mla_w_uq, mla_w_ukv, mla_q_head_norm, mla_k_head_norm, ret_theta_fwd, ret_theta_bwd, ret_out_norm, w_out_even, mix_norm_odd, w_in_odd, gla_w_gate_fwd, gla_b_gate_fwd, gla_w_gate_bwd, gla_b_gate_bwd, gla_out_norm, w_out_odd, ffn_norm, ffn_w_up, ffn_conv_w, ffn_conv_b, ffn_w_down, loss_target, m_mix_norm_even, m_w_in_even, m_mla_q_norm, m_mla_kv_norm, m_mla_w_uq, m_mla_w_ukv, m_mla_q_head_norm, m_mla_k_head_norm, m_ret_theta_fwd, m_ret_theta_bwd, m_ret_out_norm, m_w_out_even, m_mix_norm_odd, m_w_in_odd, m_gla_w_gate_fwd, m_gla_b_gate_fwd, m_gla_w_gate_bwd, m_gla_b_gate_bwd, m_gla_out_norm, m_w_out_odd, m_ffn_norm, m_ffn_w_up, m_ffn_conv_w, m_ffn_conv_b, m_ffn_w_down, v_mix_norm_even, v_w_in_even, v_mla_q_norm, v_mla_kv_norm, v_mla_w_uq, v_mla_w_ukv, v_mla_q_head_norm, v_mla_k_head_norm, v_ret_theta_fwd, v_ret_theta_bwd, v_ret_out_norm, v_w_out_even, v_mix_norm_odd, v_w_in_odd, v_gla_w_gate_fwd, v_gla_b_gate_fwd, v_gla_w_gate_bwd, v_gla_b_gate_bwd, v_gla_out_norm, v_w_out_odd, v_ffn_norm, v_ffn_w_up, v_ffn_conv_w, v_ffn_conv_b, v_ffn_w_down):
    args = dict(locals())
    x2, pos, target = args["x"][0], args["positions"][0], args["loss_target"][0]
    c = lax.axis_index("c")

    mats = [n for n, _ in _SHARDED[:_N_MATRICES]]
    smalls = [n for n, _ in _SHARDED[_N_MATRICES:]]
    big = _flatten([args[n] for n in mats], 16, BF16)
    small = _flatten([args[n] for n in smalls], 8, F32)
    g_big, g_small = _gather_chips(big, small)
    axis = dict(_SHARDED)
    full = {n: args[n] for n in _REPLICATED}
    for names, g in ((mats, g_big), (smalls, g_small)):
        per_chip = [_unflatten(g[j], [args[n].shape for n in names]) for j in range(CHIPS)]
        for k, n in enumerate(names):
            full[n] = jnp.concatenate([per_chip[j][k] for j in range(CHIPS)], axis=axis[n])

    loss, grad_x, grads = _local_step(x2, pos, target, full)
    loss = lax.psum(loss, ("x", "y", "c"))

    names = [n for n, _ in _SHARDED]
    shard_shapes = [args[n].shape for n in names]
    pieces = [jnp.split(grads[n], CHIPS, axis=axis[n]) for n in names]
    g4 = jnp.stack([_flatten([p[j] for p in pieces], _GRAD_ROWS, F32) for j in range(CHIPS)])
    rows = g4.shape[1]
    half = rows // CORES
    g4 = g4.reshape(CHIPS, CORES, half, ROW)
    mine = lax.dynamic_index_in_dim(g4, c, axis=1, keepdims=False).reshape(CHIPS * half, ROW)
    theirs = lax.dynamic_index_in_dim(g4, 1 - c, axis=1, keepdims=False).reshape(CHIPS * half, ROW)
    chip_sum = _add_rows([mine, _swap_cores(theirs)], name="add_core_halves")
    parts = _scatter_chips(chip_sum.reshape(CHIPS, half, ROW))
    reduced_half = _add_rows([parts[j] for j in range(CHIPS)], name="add_chip_parts")
    g_shard = _gather_cores(reduced_half).reshape(rows, ROW)

    w_flat, m_flat, v_flat = (_flatten([args[pre + n] for n in names], _GRAD_ROWS, F32) for pre in ("", "m_", "v_"))
    d_flat, nm_flat, nv_flat = _adamw(w_flat, g_shard, m_flat, v_flat, name="adamw_split")
    res = {}
    for kind, flat in (("grad", g_shard), ("delta", d_flat), ("new_m", nm_flat), ("new_v", nv_flat)):
        for n, a in zip(names, _unflatten(flat, shard_shapes)):
            res[kind + "_" + n] = a

    rep_shapes = [args[n].shape for n in _REPLICATED]
    g_rep = _all_reduce_devices(_flatten([grads[n] for n in _REPLICATED], 8, F32))
    w_rep, m_rep, v_rep = (_flatten([args[pre + n] for n in _REPLICATED], 8, F32) for pre in ("", "m_", "v_"))
    d_rep, nm_rep, nv_rep = _adamw(w_rep, g_rep, m_rep, v_rep, name="adamw_replicated")
    for kind, flat in (("grad", g_rep), ("delta", d_rep), ("new_m", nm_rep), ("new_v", nv_rep)):
        for n, a in zip(_REPLICATED, _unflatten(flat, rep_shapes)):
            res[kind + "_" + n] = a

    outs = [loss, grad_x[None]]
    for kind in ("grad", "delta", "new_m", "new_v"):
        outs += [res[kind + "_" + n] for n in _WEIGHTS]
    return tuple(outs)
```

```python
import math

import jax
import jax.numpy as jnp
from jax import lax
from jax.experimental import pallas as pl
from jax.experimental.pallas import tpu as pltpu

F32 = jnp.float32
BF16 = jnp.bfloat16
MESH = pl.DeviceIdType.MESH

EPS = 1e-6
D_MODEL = 1024
DEPTH = 4
LANES = 128
MLA_H, MLA_QR, MLA_KVR, MLA_NOPE, MLA_ROPE, MLA_V = 8, 384, 256, 64, 32, 64
MLA_QK = MLA_NOPE + MLA_ROPE
MLA_SCALE = MLA_QK ** -0.5
RET_H, RET_DK, RET_DV, RET_C = 8, 64, 64, 128
GLA_H, GLA_DK, GLA_DV, GLA_R, GLA_TAU, GLA_C = 4, 128, 256, 16, 16.0, 64
D_FF = 2816
ROPE_THETA = 10000.0
LN2 = math.log(2.0)
ADAM_LR, ADAM_B1, ADAM_B2, ADAM_EPS, ADAM_WD, ADAM_STEP = 0.001, 0.9, 0.999, 1e-08, 0.01, 10

EV_RET = 4 * RET_H * LANES
EV_CQ = 512
EV_W = 5120
EV_KR_BLK = (EV_RET + EV_CQ + MLA_KVR) // LANES
OD_W = 3200
OD_GA_BLK = 3072 // LANES

VMEM_LIMIT = 56 * 1024 * 1024
MM_TILE_CAP = 1408
V_ONES = (MLA_V, MLA_V + 1)


def _cp(sem):
    return pltpu.CompilerParams(dimension_semantics=sem, vmem_limit_bytes=VMEM_LIMIT)


def _dot(a, b):
    return jnp.dot(a, b, preferred_element_type=F32)


def _dot_nt(a, b):
    return lax.dot_general(a, b, (((1,), (1,)), ((), ())), preferred_element_type=F32)


def _dot_tn(a, b):
    return lax.dot_general(a, b, (((0,), (0,)), ((), ())), preferred_element_type=F32)


def _bf(x):
    return x.astype(BF16)


def _split3(x):
    h1 = _bf(x)
    r1 = x - h1.astype(F32)
    h2 = _bf(r1)
    h3 = _bf(r1 - h2.astype(F32))
    return h1, h2, h3


def _tile(n, cap):
    if n <= cap:
        return n
    best = None
    for t in range(LANES, cap + 1, LANES):
        if n % t == 0:
            best = t
    assert best is not None, n
    return best


def _mm(a, b, *, ta=False, tb=False, res=None, out_dtype=F32, b_layer=None, out_chips=False, name):
    assert not (ta and tb)
    if ta:
        kdim, m = a.shape
    else:
        m, kdim = a.shape
    if b_layer is not None:
        rows_b, cols_b = b.shape[2], b.shape[0] * b.shape[3]
    else:
        rows_b, cols_b = b.shape
    n, kb = (rows_b, cols_b) if tb else (cols_b, rows_b)
    assert kb == kdim, (a.shape, b.shape, ta, tb)
    tm, tn, tk = _tile(m, MM_TILE_CAP), _tile(n, MM_TILE_CAP), _tile(kdim, MM_TILE_CAP)
    nk = kdim // tk
    has_res = res is not None
    vmem = (2 * tm * tk * a.dtype.itemsize + 2 * tk * tn * b.dtype.itemsize
            + 2 * tm * tn * jnp.dtype(out_dtype).itemsize + (2 * tm * tn * 4 if has_res else 0)
            + (tm * tn * 4 if nk > 1 else 0))
    assert vmem <= VMEM_LIMIT - 8 * 1024 * 1024, (name, vmem)
    a_spec = (pl.BlockSpec((tk, tm), lambda i, j, k: (k, i)) if ta
              else pl.BlockSpec((tm, tk), lambda i, j, k: (i, k)))
    if b_layer is not None:
        per_chip = b.shape[3]
        if tb:
            assert tk == per_chip
            b_spec = pl.BlockSpec((None, None, tn, tk), lambda i, j, k: (k, b_layer, j, 0))
        else:
            assert tn == per_chip
            b_spec = pl.BlockSpec((None, None, tk, tn), lambda i, j, k: (j, b_layer, k, 0))
    else:
        b_spec = (pl.BlockSpec((tn, tk), lambda i, j, k: (j, k)) if tb
                  else pl.BlockSpec((tk, tn), lambda i, j, k: (k, j)))
    if out_chips:
        assert n // tn == CHIPS and not has_res
        o_spec = pl.BlockSpec((None, tm, tn), lambda i, j, k: (j, i, 0))
        out_struct = jax.ShapeDtypeStruct((CHIPS, m, tn), out_dtype)
    else:
        o_spec = pl.BlockSpec((tm, tn), lambda i, j, k: (i, j))
        out_struct = jax.ShapeDtypeStruct((m, n), out_dtype)

    def product(a_ref, b_ref):
        av, bv = _bf(a_ref[...]), _bf(b_ref[...])
        if ta:
            return _dot_tn(av, bv)
        if tb:
            return _dot_nt(av, bv)
        return _dot(av, bv)

    def body(*refs):
        a_ref, b_ref = refs[:2]
        r_ref = refs[2] if has_res else None
        o_ref = refs[3] if has_res else refs[2]

        def finish(r):
            if has_res:
                r = r + r_ref[...]
            o_ref[...] = r.astype(o_ref.dtype)

        if nk == 1:
            finish(product(a_ref, b_ref))
            return
        acc = refs[-1]
        k = pl.program_id(2)

        @pl.when(k == 0)
        def _():
            acc[...] = product(a_ref, b_ref)

        @pl.when(k > 0)
        def _():
            acc[...] += product(a_ref, b_ref)

        @pl.when(k == nk - 1)
        def _():
            finish(acc[...])

    ins = [a, b] + ([res] if has_res else [])
    in_specs = [a_spec, b_spec] + ([o_spec] if has_res else [])
    return pl.pallas_call(
        body, name=name, grid=(m // tm, n // tn, nk),
        in_specs=in_specs, out_specs=o_spec, out_shape=out_struct,
        scratch_shapes=[pltpu.VMEM((tm, tn), F32)] if nk > 1 else [],
        compiler_params=_cp(("parallel", "parallel", "arbitrary")),
    )(*ins)


def _ew(fn, rows, pars, outs, accs=(), *, s, ts, name):
    n_in = len(rows) + len(pars)
    n_o = len(outs)

    def body(*refs):
        i = pl.program_id(0)
        vals = fn(*[r[...] for r in refs[:n_in]])
        if not isinstance(vals, (tuple, list)):
            vals = (vals,)
        assert len(vals) == n_o + len(accs), (name, len(vals))
        for r, v in zip(refs[n_in:n_in + n_o], vals[:n_o]):
            r[...] = v.astype(r.dtype)
        for r, v in zip(refs[n_in + n_o:], vals[n_o:]):
            @pl.when(i == 0)
            def _(r=r, v=v):
                r[...] = v

            @pl.when(i > 0)
            def _(r=r, v=v):
                r[...] += v

    in_specs = [sp for _, sp in rows]
    in_specs += [pl.BlockSpec(p.shape, lambda i, nd=p.ndim: (0,) * nd) for p in pars]
    out_specs = [pl.BlockSpec((ts, w), lambda i: (i, 0)) for w, _ in outs]
    out_specs += [pl.BlockSpec((r, w), lambda i: (0, 0)) for r, w in accs]
    out_shape = [jax.ShapeDtypeStruct((s, w), dt) for w, dt in outs]
    out_shape += [jax.ShapeDtypeStruct((r, w), F32) for r, w in accs]
    return pl.pallas_call(
        body, name=name, grid=(s // ts,), in_specs=in_specs, out_specs=out_specs, out_shape=out_shape,
        compiler_params=_cp(("arbitrary",)),
    )(*[a for a, _ in rows], *pars)


def _cols(arr, width, blk, ts):
    return (arr, pl.BlockSpec((ts, width), lambda i, b=blk: (i, b)))


def _lead(arr, d, ts):
    return (arr, pl.BlockSpec((None, ts, arr.shape[2]), lambda i, d=d: (d, i, 0)))


def _rowsum(x):
    return jnp.sum(x, axis=0, keepdims=True)


def _lanesum(x):
    return jnp.sum(x, axis=-1, keepdims=True)


def _gsum(x, group):
    w = x.shape[-1]
    if group == w:
        return jnp.broadcast_to(_lanesum(x), x.shape)
    parts = [jnp.broadcast_to(_lanesum(x[:, g:g + group]), (x.shape[0], group)) for g in range(0, w, group)]
    return jnp.concatenate(parts, axis=-1)


def _gn(x, gain, group, n):
    rstd = lax.rsqrt(_gsum(x * x, group) * (1.0 / n) + EPS)
    xn = x * rstd
    return xn * gain, xn, rstd


def _gn_bwd(dy, xn, rstd, gain, group, n):
    dxn = dy * gain
    dx = rstd * (dxn - xn * (_gsum(dxn * xn, group) * (1.0 / n)))
    return dx, _rowsum(dy * xn)


def _sigmoid(x):
    return 1.0 / (1.0 + jnp.exp(-x))


def _rmsnorm(x_row, g, *, n, s, ts, name):
    w = g.shape[-1]

    def fn(x, gv):
        return _gn(x, gv, w, n)[0]

    return _ew(fn, [x_row], [g], [(w, BF16)], s=s, ts=ts, name=name)[0]


def _rmsnorm_bwd(x_row, g, dh, dres, *, n, s, ts, name):
    w = g.shape[-1]
    has_res = dres is not None

    def fn(x, dhv, *rest):
        gv = rest[-1]
        _, xn, rstd = _gn(x, gv, w, n)
        dx, dg = _gn_bwd(dhv, xn, rstd, gv, w, n)
        if has_res:
            dx = dx + rest[0]
        return dx, dg

    rows = [x_row, _cols(dh, w, 0, ts)] + ([_cols(dres, w, 0, ts)] if has_res else [])
    return _ew(fn, rows, [g], [(w, F32)], [(1, w)], s=s, ts=ts, name=name)


def _rope_tables(pos, real, offset):
    half = real // 2
    inv = ROPE_THETA ** (-jnp.arange(half, dtype=F32) / half)
    ang = pos.astype(F32)[:, None] * inv
    c, sn = jnp.cos(ang), jnp.sin(ang)
    s = pos.shape[0]
    cos_t = jnp.concatenate([jnp.ones((s, offset), F32), c, c,
                             jnp.ones((s, LANES - offset - real), F32)], axis=1)
    sin_t = jnp.concatenate([jnp.zeros((s, offset), F32), -sn, sn,
                             jnp.zeros((s, LANES - offset - real), F32)], axis=1)
    return cos_t, sin_t


def _rope(x, cos_t, sin_t, real, offset):
    half = real // 2
    lane = lax.broadcasted_iota(jnp.int32, x.shape, 1)
    partner = jnp.where(lane < offset + half, pltpu.roll(x, LANES - half, 1), pltpu.roll(x, half, 1))
    return x * cos_t + partner * sin_t


def _mla_prep(q_pre, kv_pre, p_even, cos_m, sin_m, qhn, khn, *, s, ts):
    w = MLA_H * LANES

    def fn(qp, kp, vp, kr, c, sn, gq, gk):
        qs, ks = [], []
        for h in range(MLA_H):
            sl = slice(h * LANES, (h + 1) * LANES)
            qn = _gn(qp[:, sl], gq, LANES, MLA_QK)[0]
            kn = _gn(kp[:, sl] + kr, gk, LANES, MLA_QK)[0]
            qs.append(_rope(qn, c, sn, MLA_ROPE, MLA_NOPE) * MLA_SCALE)
            ks.append(_rope(kn, c, sn, MLA_ROPE, MLA_NOPE))
        lane = lax.broadcasted_iota(jnp.int32, vp.shape, 1) % LANES
        ones = (lane == V_ONES[0]) | (lane == V_ONES[1])
        return jnp.concatenate(qs, axis=1), jnp.concatenate(ks, axis=1), jnp.where(ones, 1.0, vp)

    rows = [_cols(q_pre, w, 0, ts), _cols(kv_pre, w, 0, ts), _cols(kv_pre, w, 1, ts),
            _cols(p_even, LANES, EV_KR_BLK, ts), _cols(cos_m, LANES, 0, ts), _cols(sin_m, LANES, 0, ts)]
    return _ew(fn, rows, [qhn, khn], [(w, BF16)] * 3, s=s, ts=ts, name="mla_prep")


def _mla_prep_bwd(q_pre, kv_pre, p_even, cos_m, sin_m, qhn, khn, dq, dk, *, s, ts):
    w = MLA_H * LANES

    def fn(qp, kp, kr, c, sn, dqv, dkv, gq, gk):
        dqs, dks = [], []
        dkr = jnp.zeros_like(kr)
        dgq = jnp.zeros((1, LANES), F32)
        dgk = jnp.zeros((1, LANES), F32)
        for h in range(MLA_H):
            sl = slice(h * LANES, (h + 1) * LANES)
            _, qn, qr = _gn(qp[:, sl], gq, LANES, MLA_QK)
            _, kn, krs = _gn(kp[:, sl] + kr, gk, LANES, MLA_QK)
            dqn = _rope(dqv[:, sl] * MLA_SCALE, c, -sn, MLA_ROPE, MLA_NOPE)
            dkn = _rope(dkv[:, sl], c, -sn, MLA_ROPE, MLA_NOPE)
            dqh, g1 = _gn_bwd(dqn, qn, qr, gq, LANES, MLA_QK)
            dkh, g2 = _gn_bwd(dkn, kn, krs, gk, LANES, MLA_QK)
            dqs.append(dqh)
            dks.append(dkh)
            dkr = dkr + dkh
            dgq = dgq + g1
            dgk = dgk + g2
        return jnp.concatenate(dqs, axis=1), jnp.concatenate(dks, axis=1), dkr, dgq, dgk

    rows = [_cols(q_pre, w, 0, ts), _cols(kv_pre, w, 0, ts), _cols(p_even, LANES, EV_KR_BLK, ts),
            _cols(cos_m, LANES, 0, ts), _cols(sin_m, LANES, 0, ts), _cols(dq, w, 0, ts), _cols(dk, w, 0, ts)]
    return _ew(fn, rows, [qhn, khn], [(w, BF16), (w, BF16), (LANES, BF16)], [(1, LANES), (1, LANES)],
               s=s, ts=ts, name="mla_prep_bwd")


def _flash_fwd(q, k, v, *, tq, tk):
    s = q.shape[0]
    nq, nk = s // tq, s // tk
    rq = min(tq, 256)

    def body(q_ref, k_ref, v_ref, o_ref, lse_ref, m_s, acc):
        j = pl.program_id(2)

        @pl.when(j == 0)
        def _():
            m_s[...] = jnp.full_like(m_s, -jnp.inf)
            acc[...] = jnp.zeros_like(acc)

        kv, vv = k_ref[...], v_ref[...]
        for r in range(0, tq, rq):
            rows = slice(r, r + rq)
            sc = _dot_nt(q_ref[rows, :], kv)
            m_prev = m_s[rows, :]
            m_new = jnp.maximum(m_prev, jnp.max(sc, axis=-1, keepdims=True))
            p = jnp.exp(sc - jnp.tile(m_new, (1, tk // LANES)))
            acc[rows, :] = jnp.exp(m_prev - m_new) * acc[rows, :] + _dot(_bf(p), vv)
            m_s[rows, :] = m_new

        @pl.when(j == nk - 1)
        def _():
            a = acc[...]
            l = a[:, V_ONES[0]:V_ONES[0] + 1]
            o_ref[...] = (a / l).astype(o_ref.dtype)
            lse_ref[...] = m_s[:, 0:1] + jnp.log(l)

    qs = pl.BlockSpec((tq, LANES), lambda h, i, j: (i, h))
    ks = pl.BlockSpec((tk, LANES), lambda h, i, j: (j, h))
    return pl.pallas_call(
        body, name="mla_flash_fwd", grid=(MLA_H, nq, nk),
        in_specs=[qs, ks, ks],
        out_specs=[qs, pl.BlockSpec((None, tq, 1), lambda h, i, j: (h, i, 0))],
        out_shape=[jax.ShapeDtypeStruct((s, MLA_H * LANES), BF16), jax.ShapeDtypeStruct((MLA_H, s, 1), F32)],
        scratch_shapes=[pltpu.VMEM((tq, LANES), F32), pltpu.VMEM((tq, LANES), F32)],
        compiler_params=_cp(("parallel", "parallel", "arbitrary")),
    )(q, k, v)


def _attn_bwd_prep(dar, o, *, s, ts):
    w = MLA_H * LANES

    def fn(dov, ov):
        outs = []
        lane = lax.broadcasted_iota(jnp.int32, (dov.shape[0], LANES), 1)
        for h in range(MLA_H):
            sl = slice(h * LANES, (h + 1) * LANES)
            d = dov[:, sl]
            delta = _lanesum(d * ov[:, sl].astype(F32))
            hi = _bf(delta).astype(F32)
            outs.append(jnp.where(lane == V_ONES[0], -hi, jnp.where(lane == V_ONES[1], hi - delta, d)))
        return jnp.concatenate(outs, axis=1)

    return _ew(fn, [_cols(dar, w, 0, ts), _cols(o, w, 0, ts)], [], [(w, BF16)], s=s, ts=ts,
               name="mla_attn_bwd_prep")[0]


def _flash_bwd(q, k, v, do, lse, *, tq, tk):
    s = q.shape[0]
    nq, nk = s // tq, s // tk

    def body(q_ref, k_ref, v_ref, do_ref, lse_ref, dq_ref, dk_ref, dv_ref, dk_acc, dv_acc):
        j = pl.program_id(1)
        i = pl.program_id(2)
        qv, kv, dov = q_ref[...], k_ref[...], do_ref[...]
        p = jnp.exp(_dot_nt(qv, kv) - lse_ref[...])
        ds = _bf(p * _dot_nt(dov, v_ref[...]))
        dv_c = _dot_tn(_bf(p), dov)
        dk_c = _dot_tn(ds, qv)
        dq_c = _dot(ds, kv)
        rows = pl.ds(pl.multiple_of(i * tq, tq), tq)

        @pl.when(i == 0)
        def _():
            dk_acc[...] = dk_c
            dv_acc[...] = dv_c

        @pl.when(i > 0)
        def _():
            dk_acc[...] += dk_c
            dv_acc[...] += dv_c

        @pl.when(j == 0)
        def _():
            dq_ref[rows, :] = dq_c

        @pl.when(j > 0)
        def _():
            dq_ref[rows, :] += dq_c

        @pl.when(i == nq - 1)
        def _():
            dk_ref[...] = dk_acc[...]
            dv_ref[...] = dv_acc[...].astype(dv_ref.dtype)

    qs = pl.BlockSpec((tq, LANES), lambda h, j, i: (i, h))
    ks = pl.BlockSpec((tk, LANES), lambda h, j, i: (j, h))
    st = pl.BlockSpec((None, tq, 1), lambda h, j, i: (h, i, 0))
    return pl.pallas_call(
        body, name="mla_flash_bwd", grid=(MLA_H, nk, nq),
        in_specs=[qs, ks, ks, qs, st],
        out_specs=[pl.BlockSpec((s, LANES), lambda h, j, i: (0, h)), ks, ks],
        out_shape=[jax.ShapeDtypeStruct((s, MLA_H * LANES), F32), jax.ShapeDtypeStruct((s, MLA_H * LANES), F32),
                   jax.ShapeDtypeStruct((s, MLA_H * LANES), BF16)],
        scratch_shapes=[pltpu.VMEM((tk, LANES), F32), pltpu.VMEM((tk, LANES), F32)],
        compiler_params=_cp(("parallel", "arbitrary", "arbitrary")),
    )(q, k, v, do, lse)


def _ret_geometry(d, c):
    df = d.astype(F32)
    ii = lax.broadcasted_iota(jnp.int32, (c, c), 0).astype(F32)
    jj = lax.broadcasted_iota(jnp.int32, (c, c), 1).astype(F32)
    rel = (ii - jj) * (1.0 - 2.0 * df)
    mask = rel >= df
    rel0 = jnp.maximum(rel, 0.0)
    pos = lax.broadcasted_iota(jnp.int32, (c, 1), 0).astype(F32)
    ez = (c - 1 - pos) + df * (2.0 * pos - (c - 1))
    ex = (pos + 1.0) + df * (c - 1 - 2.0 * pos)
    return mask, rel0, ez, ex


def _chunk_index(n_chunks):
    return lambda d, n: n + d * (n_chunks - 1 - 2 * n)


def _ret_fwd(p_even, cos_r, sin_r, theta_l):
    s = p_even.shape[0]
    c = RET_C
    n_chunks = s // c
    w = RET_H * LANES
    cidx = _chunk_index(n_chunks)

    def body(q_ref, k_ref, v_ref, cos_ref, sin_ref, th_ref, o_ref, rp_ref, r_s):
        d = pl.program_id(0)
        n = pl.program_id(1)

        @pl.when(n == 0)
        def _():
            r_s[...] = jnp.zeros_like(r_s)

        lg = jnp.log1p(-jnp.exp(-th_ref[...] * LN2))
        mask, rel0, ez, ex = _ret_geometry(d, c)
        cs, sn = cos_ref[...], sin_ref[...]
        rp_ref[...] = r_s[...]
        for h in range(RET_H):
            sl = slice(h * LANES, (h + 1) * LANES)
            lgh = lg[:, h * LANES:h * LANES + 1]
            dm = jnp.where(mask, jnp.exp(lgh * rel0), 0.0)
            qh = _bf(_rope(q_ref[:, sl], cs, sn, RET_DK, 0))
            kf = _rope(k_ref[:, sl], cs, sn, RET_DK, 0) * (RET_DK ** -0.5)
            kh = _bf(kf)
            vh = _bf(v_ref[:, sl])
            rh = r_s[sl, :]
            a = _dot_nt(qh, kh) * dm
            o_ref[:, sl] = _dot(_bf(a), vh) + jnp.exp(lgh * ex) * _dot(qh, _bf(rh))
            zk = _bf(kf * jnp.exp(lgh * ez))
            r_s[sl, :] = jnp.exp(lgh * c) * rh + _dot_tn(zk, vh)

    def col(blk):
        return pl.BlockSpec((c, w), lambda d, n: (cidx(d, n), blk))

    tab = pl.BlockSpec((c, LANES), lambda d, n: (cidx(d, n), 0))
    return pl.pallas_call(
        body, name="ret_fwd", grid=(2, n_chunks),
        in_specs=[col(0), col(1), col(2), tab, tab, pl.BlockSpec((None, 1, w), lambda d, n: (d, 0, 0))],
        out_specs=[pl.BlockSpec((None, c, w), lambda d, n: (d, cidx(d, n), 0)),
                   pl.BlockSpec((None, None, w, LANES), lambda d, n: (d, cidx(d, n), 0, 0))],
        out_shape=[jax.ShapeDtypeStruct((2, s, w), F32), jax.ShapeDtypeStruct((2, n_chunks, w, LANES), F32)],
        scratch_shapes=[pltpu.VMEM((w, LANES), F32)],
        compiler_params=_cp(("arbitrary", "arbitrary")),
    )(p_even, p_even, p_even, cos_r, sin_r, theta_l)


def _ret_bwd(p_even, cos_r, sin_r, theta_l, theta_h, r_prev, do):
    s = p_even.shape[0]
    c = RET_C
    n_chunks = s // c
    w = RET_H * LANES
    fwd_idx = _chunk_index(n_chunks)

    def cidx(d, n):
        return fwd_idx(d, n_chunks - 1 - n)

    def body(q_ref, k_ref, v_ref, cos_ref, sin_ref, th_ref, thh_ref, rp_ref, do_ref,
             dq_ref, dk_ref, dv_ref, dth_ref, dr_s):
        d = pl.program_id(0)
        n = pl.program_id(1)

        @pl.when(n == 0)
        def _():
            dr_s[...] = jnp.zeros_like(dr_s)
            dth_ref[...] = jnp.zeros_like(dth_ref)

        lg = jnp.log1p(-jnp.exp(-th_ref[...] * LN2))
        mask, rel0, ez, ex = _ret_geometry(d, c)
        cs, sn = cos_ref[...], sin_ref[...]
        row = lax.broadcasted_iota(jnp.int32, (RET_H, LANES), 0)
        dlg = jnp.zeros((RET_H, LANES), F32)
        kscale = RET_DK ** -0.5
        for h in range(RET_H):
            sl = slice(h * LANES, (h + 1) * LANES)
            lgh = lg[:, h * LANES:h * LANES + 1]
            dm = jnp.where(mask, jnp.exp(lgh * rel0), 0.0)
            zeta = jnp.exp(lgh * ez)
            xi = jnp.exp(lgh * ex)
            gc = jnp.exp(lgh * c)
            qf = _rope(q_ref[:, sl], cs, sn, RET_DK, 0)
            qh = _bf(qf)
            kf = _rope(k_ref[:, sl], cs, sn, RET_DK, 0) * kscale
            kh = _bf(kf)
            zkf = kf * zeta
            zk = _bf(zkf)
            vh = _bf(v_ref[:, sl])
            dof = do_ref[:, sl]
            doh = _bf(dof)
            rp = rp_ref[sl, :]
            rpb = _bf(rp)
            drn = dr_s[sl, :]
            drb = _bf(drn)
            a = _dot_nt(qh, kh) * dm
            da0 = _dot_nt(doh, vh)
            da = _bf(da0 * dm)
            vdr = _dot_nt(vh, drb)
            dq_r = _dot(da, kh) + xi * _dot_nt(doh, rpb)
            dk_r = _dot_tn(da, qh) + zeta * vdr
            dv_ref[:, sl] = _dot_tn(_bf(a), doh) + _dot(zk, drb)
            dq_ref[:, sl] = _rope(dq_r, cs, -sn, RET_DK, 0)
            dk_ref[:, sl] = _rope(dk_r * kscale, cs, -sn, RET_DK, 0)
            dr_s[sl, :] = _dot_tn(_bf(qf * xi), doh) + gc * drn
            ocross = xi * _dot(qh, rpb)
            t = (jnp.sum(rel0 * a * da0, keepdims=True)
                 + jnp.sum(ex * dof * ocross, keepdims=True)
                 + c * gc * jnp.sum(drn * rp, keepdims=True)
                 + jnp.sum(ez * zkf * vdr, keepdims=True))
            dlg = jnp.where(row == h, t, dlg)
        x2 = jnp.exp(-thh_ref[...] * LN2)
        dth_ref[...] += dlg * (x2 * LN2 / (1.0 - x2))

    def col(blk):
        return pl.BlockSpec((c, w), lambda d, n: (cidx(d, n), blk))

    tab = pl.BlockSpec((c, LANES), lambda d, n: (cidx(d, n), 0))
    dirrow = pl.BlockSpec((None, c, w), lambda d, n: (d, cidx(d, n), 0))
    hrow = pl.BlockSpec((None, RET_H, LANES), lambda d, n: (d, 0, 0))
    return pl.pallas_call(
        body, name="ret_bwd", grid=(2, n_chunks),
        in_specs=[col(0), col(1), col(2), tab, tab, pl.BlockSpec((None, 1, w), lambda d, n: (d, 0, 0)), hrow,
                  pl.BlockSpec((None, None, w, LANES), lambda d, n: (d, cidx(d, n), 0, 0)),
                  pl.BlockSpec((c, w), lambda d, n: (cidx(d, n), 0))],
        out_specs=[dirrow, dirrow, dirrow, hrow],
        out_shape=[jax.ShapeDtypeStruct((2, s, w), F32)] * 3 + [jax.ShapeDtypeStruct((2, RET_H, LANES), F32)],
        scratch_shapes=[pltpu.VMEM((w, LANES), F32)],
        compiler_params=_cp(("arbitrary", "arbitrary")),
    )(p_even, p_even, p_even, cos_r, sin_r, theta_l, theta_h, r_prev, do)


def _post_fwd(o2, gate_row, gain, *, group, n, s, ts, name):
    w = o2.shape[2]

    def fn(of, ob, g, gv):
        y = _gn(of + ob, gv, group, n)[0]
        return g * _sigmoid(g) * y

    return _ew(fn, [_lead(o2, 0, ts), _lead(o2, 1, ts), gate_row], [gain], [(w, BF16)], s=s, ts=ts, name=name)[0]


def _post_bwd(o2, gate_row, gain, dr_row, *, group, n, s, ts, name):
    w = o2.shape[2]

    def fn(of, ob, g, dr, gv):
        y, xn, rstd = _gn(of + ob, gv, group, n)
        sg = _sigmoid(g)
        dy = dr * (g * sg)
        dgate = dr * y * (sg * (1.0 + g * (1.0 - sg)))
        do, dgain = _gn_bwd(dy, xn, rstd, gv, group, n)
        return do, dgate, dgain

    return _ew(fn, [_lead(o2, 0, ts), _lead(o2, 1, ts), gate_row, dr_row], [gain],
               [(w, F32), (w, BF16)], [(1, w)], s=s, ts=ts, name=name)


def _sum2(a2, *, s, ts, name):
    w = a2.shape[2]
    return _ew(lambda a, b: a + b, [_lead(a2, 0, ts), _lead(a2, 1, ts)], [], [(w, BF16)], s=s, ts=ts, name=name)[0]


def _gla_common(d, q_ref, k_ref, ga_ref, wg_ref, bg_ref):
    c = GLA_C
    df = d.astype(F32)
    ii = lax.broadcasted_iota(jnp.int32, (c, c), 0).astype(F32)
    jj = lax.broadcasted_iota(jnp.int32, (c, c), 1).astype(F32)
    rel = (ii - jj) * (1.0 - 2.0 * df)
    tri = _bf(jnp.where(rel >= 0.0, 1.0, 0.0))
    mask = rel >= df
    gab = _bf(ga_ref[...])
    z = _dot(gab, wg_ref[...]) + bg_ref[...]
    la = (jnp.minimum(z, 0.0) - jnp.log1p(jnp.exp(-jnp.abs(z)))) * (1.0 / GLA_TAU)
    l1, l2, l3 = _split3(la)
    b = _dot(tri, l1) + _dot(tri, l2) + _dot(tri, l3)
    first = d == 0
    bm = jnp.where(first, b[c // 2:c // 2 + 1], b[c // 2 - 1:c // 2])
    bl = jnp.where(first, b[c - 1:c], b[0:1])
    q = q_ref[...] * (GLA_DK ** -0.5)
    k = k_ref[...]
    e1, e2, e3, eb = jnp.exp(b - bm), jnp.exp(bm - b), jnp.exp(bl - b), jnp.exp(b)
    return dict(tri=tri, mask=mask, gab=gab, z=z, ebl=jnp.exp(bl), e1=e1, e2=e2, e3=e3, eb=eb,
                qc=q * e1, kc=k * e2, kd=k * e3, qe=q * eb, first=first)


def _col_scale(row_vec, width):
    t = jnp.broadcast_to(row_vec, (LANES, LANES)).T
    return jnp.concatenate([t] * (width // LANES), axis=1)


def _gla_fwd(p_odd, wg2, bg2):
    s = p_odd.shape[0]
    c = GLA_C
    n_chunks = s // c
    wk, wv = GLA_H * GLA_DK, GLA_H * GLA_DV
    cidx = _chunk_index(n_chunks)

    def body(q_ref, k_ref, v_ref, ga_ref, wg_ref, bg_ref, o_ref, sp_ref, s_s):
        d = pl.program_id(0)
        n = pl.program_id(1)

        @pl.when(n == 0)
        def _():
            s_s[...] = jnp.zeros_like(s_s)

        g = _gla_common(d, q_ref, k_ref, ga_ref, wg_ref, bg_ref)
        sp_ref[...] = s_s[...]
        for h in range(GLA_H):
            sl = slice(h * GLA_DK, (h + 1) * GLA_DK)
            vs = slice(h * GLA_DV, (h + 1) * GLA_DV)
            vh = _bf(v_ref[:, vs])
            sh = s_s[sl, :]
            a = jnp.where(g["mask"], _dot_nt(_bf(g["qc"][:, sl]), _bf(g["kc"][:, sl])), 0.0)
            o_ref[:, vs] = _dot(_bf(a), vh) + _dot(_bf(g["qe"][:, sl]), _bf(sh))
            s_s[sl, :] = _col_scale(g["ebl"][:, sl], GLA_DV) * sh + _dot_tn(_bf(g["kd"][:, sl]), vh)

    def col(width, blk):
        return pl.BlockSpec((c, width), lambda d, n: (cidx(d, n), blk))

    return pl.pallas_call(
        body, name="gla_fwd", grid=(2, n_chunks),
        in_specs=[col(wk, 0), col(wk, 1), col(wv, 1), col(LANES, OD_GA_BLK),
                  pl.BlockSpec((None, LANES, wk), lambda d, n: (d, 0, 0)),
                  pl.BlockSpec((None, 1, wk), lambda d, n: (d, 0, 0))],
        out_specs=[pl.BlockSpec((None, c, wv), lambda d, n: (d, cidx(d, n), 0)),
                   pl.BlockSpec((None, None, wk, GLA_DV), lambda d, n: (d, cidx(d, n), 0, 0))],
        out_shape=[jax.ShapeDtypeStruct((2, s, wv), F32), jax.ShapeDtypeStruct((2, n_chunks, wk, GLA_DV), F32)],
        scratch_shapes=[pltpu.VMEM((wk, GLA_DV), F32)],
        compiler_params=_cp(("arbitrary", "arbitrary")),
    )(p_odd, p_odd, p_odd, p_odd, wg2, bg2)


def _gla_bwd(p_odd, wg2, bg2, s_prev, do):
    s = p_odd.shape[0]
    c = GLA_C
    n_chunks = s // c
    wk, wv = GLA_H * GLA_DK, GLA_H * GLA_DV
    fwd_idx = _chunk_index(n_chunks)

    def cidx(d, n):
        return fwd_idx(d, n_chunks - 1 - n)

    def body(q_ref, k_ref, v_ref, ga_ref, wg_ref, bg_ref, sp_ref, do_ref,
             dq_ref, dk_ref, dv_ref, dga_ref, dwg_ref, dbg_ref, ds_s):
        d = pl.program_id(0)
        n = pl.program_id(1)

        @pl.when(n == 0)
        def _():
            ds_s[...] = jnp.zeros_like(ds_s)
            dwg_ref[...] = jnp.zeros_like(dwg_ref)
            dbg_ref[...] = jnp.zeros_like(dbg_ref)

        g = _gla_common(d, q_ref, k_ref, ga_ref, wg_ref, bg_ref)
        mask = g["mask"]
        ones8 = jnp.ones((8, GLA_DV), BF16)
        dbs, dbms, dbls = [], [], []
        for h in range(GLA_H):
            sl = slice(h * GLA_DK, (h + 1) * GLA_DK)
            vs = slice(h * GLA_DV, (h + 1) * GLA_DV)
            qc, kc, kd, qe = g["qc"][:, sl], g["kc"][:, sl], g["kd"][:, sl], g["qe"][:, sl]
            qcb, kcb, kdb, qeb = _bf(qc), _bf(kc), _bf(kd), _bf(qe)
            vh = _bf(v_ref[:, vs])
            doh = _bf(do_ref[:, vs])
            sp = sp_ref[sl, :]
            dsn = ds_s[sl, :]
            dsb = _bf(dsn)
            a = _bf(jnp.where(mask, _dot_nt(qcb, kcb), 0.0))
            da = _bf(jnp.where(mask, _dot_nt(doh, vh), 0.0))
            dv_ref[:, vs] = _dot_tn(a, doh) + _dot(kdb, dsb)
            dqc = _dot(da, kcb)
            dkc = _dot_tn(da, qcb)
            dqe = _dot_nt(doh, _bf(sp))
            dkd = _dot_nt(vh, dsb)
            ds_s[sl, :] = _dot_tn(qeb, doh) + _col_scale(g["ebl"][:, sl], GLA_DV) * dsn
            dq_ref[:, sl] = (dqc * g["e1"][:, sl] + dqe * g["eb"][:, sl]) * (GLA_DK ** -0.5)
            dk_ref[:, sl] = dkc * g["e2"][:, sl] + dkd * g["e3"][:, sl]
            t1, t2, t3, t4 = dqc * qc, dkc * kc, dqe * qe, dkd * kd
            dbs.append(t1 - t2 + t3 - t4)
            dbms.append(_rowsum(t2 - t1))
            m1, m2, _ = _split3(dsn * sp)
            rs = (_dot_nt(ones8, m1) + _dot_nt(ones8, m2))[0:1]
            dbls.append(_rowsum(t4) + g["ebl"][:, sl] * rs)
        db = jnp.concatenate(dbs, axis=1)
        dbm = jnp.concatenate(dbms, axis=1)
        dbl = jnp.concatenate(dbls, axis=1)
        row = lax.broadcasted_iota(jnp.int32, (c, wk), 0)
        mid = jnp.where(g["first"], c // 2, c // 2 - 1)
        last = jnp.where(g["first"], c - 1, 0)
        db = db + jnp.where(row == mid, dbm, 0.0) + jnp.where(row == last, dbl, 0.0)
        d1, d2, d3 = _split3(db)
        tri = g["tri"]
        dla = _dot_tn(tri, d1) + _dot_tn(tri, d2) + _dot_tn(tri, d3)
        dz = dla * (1.0 / GLA_TAU) * (1.0 - _sigmoid(g["z"]))
        dzb = _bf(dz)
        dga_ref[...] = _dot_nt(dzb, wg_ref[...])
        dwg_ref[...] += _dot_tn(g["gab"], dzb)
        dbg_ref[...] += _rowsum(dz)

    def col(width, blk):
        return pl.BlockSpec((c, width), lambda d, n: (cidx(d, n), blk))

    def dirrow(width):
        return pl.BlockSpec((None, c, width), lambda d, n: (d, cidx(d, n), 0))

    return pl.pallas_call(
        body, name="gla_bwd", grid=(2, n_chunks),
        in_specs=[col(wk, 0), col(wk, 1), col(wv, 1), col(LANES, OD_GA_BLK),
                  pl.BlockSpec((None, LANES, wk), lambda d, n: (d, 0, 0)),
                  pl.BlockSpec((None, 1, wk), lambda d, n: (d, 0, 0)),
                  pl.BlockSpec((None, None, wk, GLA_DV), lambda d, n: (d, cidx(d, n), 0, 0)),
                  pl.BlockSpec((c, wv), lambda d, n: (cidx(d, n), 0))],
        out_specs=[dirrow(wk), dirrow(wk), dirrow(wv), dirrow(LANES),
                   pl.BlockSpec((None, LANES, wk), lambda d, n: (d, 0, 0)),
                   pl.BlockSpec((None, 1, wk), lambda d, n: (d, 0, 0))],
        out_shape=[jax.ShapeDtypeStruct((2, s, wk), F32), jax.ShapeDtypeStruct((2, s, wk), F32),
                   jax.ShapeDtypeStruct((2, s, wv), F32), jax.ShapeDtypeStruct((2, s, LANES), F32),
                   jax.ShapeDtypeStruct((2, LANES, wk), F32), jax.ShapeDtypeStruct((2, 1, wk), F32)],
        scratch_shapes=[pltpu.VMEM((wk, GLA_DV), F32)],
        compiler_params=_cp(("arbitrary", "arbitrary")),
    )(p_odd, p_odd, p_odd, p_odd, wg2, bg2, s_prev, do)


HALO = 8


def _halo_specs(width_blk, col0, ts, s):
    r = ts // HALO
    last = s // HALO - 1
    cur = pl.BlockSpec((ts, width_blk), lambda j, i: (i, col0 + j))
    prev = pl.BlockSpec((HALO, width_blk), lambda j, i: (jnp.maximum(i * r - 1, 0), col0 + j))
    nxt = pl.BlockSpec((HALO, width_blk), lambda j, i: (jnp.minimum((i + 1) * r, last), col0 + j))
    return [prev, cur, nxt]


def _with_halo(prev_ref, cur_ref, next_ref, i, n_i):
    p = jnp.where(i == 0, 0.0, prev_ref[...])
    q = jnp.where(i == n_i - 1, 0.0, next_ref[...])
    return jnp.concatenate([p, cur_ref[...], q], axis=0)


def _shift_down(x):
    return pltpu.roll(x, 1, 0)


def _shift_up(x):
    return pltpu.roll(x, x.shape[0] - 1, 0)


def _ffn_act(up, conv_w, conv_b, *, ts):
    s = up.shape[0]
    tc = _tile(D_FF, 1408)
    nj = D_FF // tc
    n_i = s // ts

    def body(gp, gc, gn, val_ref, w_ref, b_ref, a_ref):
        i = pl.program_id(1)
        g = _with_halo(gp, gc, gn, i, n_i)
        w = w_ref[...]
        conv = w[0:1] * _shift_down(g) + w[1:2] * g + w[2:3] * _shift_up(g) + b_ref[...]
        conv = conv[HALO:HALO + ts]
        a_ref[...] = (conv * _sigmoid(conv) * val_ref[...]).astype(a_ref.dtype)

    return pl.pallas_call(
        body, name="ffn_act", grid=(nj, n_i),
        in_specs=_halo_specs(tc, 0, ts, s) + [pl.BlockSpec((ts, tc), lambda j, i: (i, nj + j)),
                                              pl.BlockSpec((3, tc), lambda j, i: (0, j)),
                                              pl.BlockSpec((1, tc), lambda j, i: (0, j))],
        out_specs=pl.BlockSpec((ts, tc), lambda j, i: (i, j)),
        out_shape=jax.ShapeDtypeStruct((s, D_FF), BF16),
        compiler_params=_cp(("parallel", "arbitrary")),
    )(up, up, up, up, conv_w, conv_b)


def _ffn_act_bwd(up, da, conv_w, conv_b, *, ts):
    s = up.shape[0]
    tc = _tile(D_FF, 1408)
    nj = D_FF // tc
    n_i = s // ts

    def body(gp, gc, gn, vp, vc, vn, dp, dc, dn, w_ref, b_ref, dg_ref, dval_ref, dw_ref, db_ref):
        i = pl.program_id(1)
        g = _with_halo(gp, gc, gn, i, n_i)
        v = _with_halo(vp, vc, vn, i, n_i)
        dav = _with_halo(dp, dc, dn, i, n_i)
        w = w_ref[...]
        gm, gpl = _shift_down(g), _shift_up(g)
        conv = w[0:1] * gm + w[1:2] * g + w[2:3] * gpl + b_ref[...]
        sg = _sigmoid(conv)
        dgc = dav * v * (sg * (1.0 + conv * (1.0 - sg)))
        dgate = w[0:1] * _shift_up(dgc) + w[1:2] * dgc + w[2:3] * _shift_down(dgc)
        ctr = slice(HALO, HALO + ts)
        dg_ref[...] = dgate[ctr].astype(dg_ref.dtype)
        dval_ref[...] = (dav[ctr] * (conv * sg)[ctr]).astype(dval_ref.dtype)
        dgc_c = dgc[ctr]
        dw = jnp.concatenate([_rowsum(dgc_c * gm[ctr]), _rowsum(dgc_c * g[ctr]), _rowsum(dgc_c * gpl[ctr])], axis=0)
        dbv = _rowsum(dgc_c)

        @pl.when(i == 0)
        def _():
            dw_ref[...] = dw
            db_ref[...] = dbv

        @pl.when(i > 0)
        def _():
            dw_ref[...] += dw
            db_ref[...] += dbv

    tile = pl.BlockSpec((ts, tc), lambda j, i: (i, j))
    return pl.pallas_call(
        body, name="ffn_act_bwd", grid=(nj, n_i),
        in_specs=(_halo_specs(tc, 0, ts, s) + _halo_specs(tc, nj, ts, s) + _halo_specs(tc, 0, ts, s)
                  + [pl.BlockSpec((3, tc), lambda j, i: (0, j)), pl.BlockSpec((1, tc), lambda j, i: (0, j))]),
        out_specs=[tile, tile, pl.BlockSpec((3, tc), lambda j, i: (0, j)), pl.BlockSpec((1, tc), lambda j, i: (0, j))],
        out_shape=[jax.ShapeDtypeStruct((s, D_FF), BF16), jax.ShapeDtypeStruct((s, D_FF), BF16),
                   jax.ShapeDtypeStruct((3, D_FF), F32), jax.ShapeDtypeStruct((1, D_FF), F32)],
        compiler_params=_cp(("parallel", "arbitrary")),
    )(up, up, up, up, up, up, da, da, da, conv_w, conv_b)


def _loss_head(y, target, *, s, ts):
    def fn(yv, tv):
        err = yv - tv
        return err * (1.0 / D_MODEL), _rowsum(err * err)

    return _ew(fn, [_cols(y, D_MODEL, 0, ts), _cols(target, D_MODEL, 0, ts)], [], [(D_MODEL, F32)],
               [(1, D_MODEL)], s=s, ts=ts, name="loss_head")


def _rows_tile(r, width):
    ts = r
    while ts * width * 4 > (1 << 20) and ts % 16 == 0:
        ts //= 2
    return ts


def _adamw(w, g, m, v, *, ts, name):
    r, width = w.shape
    assert r % ts == 0

    def fn(wv, gv, mv, vv):
        mn = ADAM_B1 * mv + (1.0 - ADAM_B1) * gv
        vn = ADAM_B2 * vv + (1.0 - ADAM_B2) * (gv * gv)
        m_hat = mn / (1.0 - ADAM_B1 ** ADAM_STEP)
        v_hat = vn / (1.0 - ADAM_B2 ** ADAM_STEP)
        delta = -ADAM_LR * (m_hat / (jnp.sqrt(v_hat) + ADAM_EPS) + ADAM_WD * wv)
        return delta, mn, vn

    rows = [_cols(a, width, 0, ts) for a in (w, g, m, v)]
    return _ew(fn, rows, [], [(width, F32)] * 3, s=r, ts=ts, name=name)


def _pad_heads(w, heads, real):
    lead = w.shape[:-1]
    w = w.reshape(lead + (heads, real))
    w = jnp.pad(w, [(0, 0)] * len(lead) + [(0, 0), (0, LANES - real)])
    return w.reshape(lead + (heads * LANES,))


def _pad_head_rows(w, heads, real):
    return _pad_heads(w.T, heads, real).T


def _pack_even(p):
    w_in = p["w_in"]
    z = lambda n: jnp.zeros((D_MODEL, n), w_in.dtype)
    o = 0
    parts = {}
    for nm, n in (("cq", MLA_QR), ("ckv", MLA_KVR), ("kr", MLA_ROPE), ("rq", 512), ("rk", 512), ("rv", 512), ("rg", 512)):
        parts[nm] = w_in[:, o:o + n]
        o += n
    w_in_p = jnp.concatenate(
        [_pad_heads(parts[k], RET_H, RET_DK) for k in ("rq", "rk", "rv", "rg")]
        + [parts["cq"], z(EV_CQ - MLA_QR), parts["ckv"], z(MLA_NOPE), parts["kr"], z(LANES - MLA_QK), z(LANES)], axis=1)
    w_uq = jnp.pad(_pad_heads(p["w_uq"], MLA_H, MLA_QK), ((0, EV_CQ - MLA_QR), (0, 0)))
    ukv = p["w_ukv"].reshape(MLA_KVR, MLA_H, MLA_NOPE + MLA_V)
    w_ukv = jnp.concatenate([_pad_heads(ukv[..., :MLA_NOPE].reshape(MLA_KVR, -1), MLA_H, MLA_NOPE),
                             _pad_heads(ukv[..., MLA_NOPE:].reshape(MLA_KVR, -1), MLA_H, MLA_V)], axis=1)
    w_out = jnp.concatenate([_pad_head_rows(p["w_out"][:MLA_H * MLA_V], MLA_H, MLA_V),
                             _pad_head_rows(p["w_out"][MLA_H * MLA_V:], RET_H, RET_DV)], axis=0)
    return dict(
        w_in=w_in_p, w_uq=w_uq, w_ukv=w_ukv, w_out=w_out,
        mix_g=p["mix_norm"][None, :],
        q_norm=jnp.pad(p["q_norm"], (0, EV_CQ - MLA_QR))[None, :],
        kv_norm=p["kv_norm"][None, :],
        qhn=jnp.pad(p["q_head_norm"], (0, LANES - MLA_QK))[None, :],
        khn=jnp.pad(p["k_head_norm"], (0, LANES - MLA_QK))[None, :],
        ret_gain=_pad_heads(p["ret_out_norm"].reshape(-1), RET_H, RET_DV)[None, :],
    )


def _pack_odd(p):
    w_in = p["w_in"]
    ga = w_in[:, 3072:]
    w_in_p = jnp.concatenate([w_in[:, :3072], ga, jnp.zeros((D_MODEL, LANES - 2 * GLA_R), w_in.dtype)], axis=1)
    wk = GLA_H * GLA_DK
    zf = jnp.zeros((LANES - GLA_R, wk), p["w_gate_fwd"].dtype)
    zb0 = jnp.zeros((GLA_R, wk), p["w_gate_fwd"].dtype)
    zb1 = jnp.zeros((LANES - 2 * GLA_R, wk), p["w_gate_fwd"].dtype)
    wg2 = jnp.stack([jnp.concatenate([p["w_gate_fwd"], zf], axis=0),
                     jnp.concatenate([zb0, p["w_gate_bwd"], zb1], axis=0)])
    bg2 = jnp.stack([p["b_gate_fwd"][None, :], p["b_gate_bwd"][None, :]])
    return dict(w_in=w_in_p, wg2=wg2, bg2=bg2, w_out=p["w_out"], mix_g=p["mix_norm"][None, :],
                gla_gain=p["gla_out_norm"].reshape(1, -1))


_MATRICES = ("w_in", "w_uq", "w_ukv", "w_out", "wg2")


def _packed(pack_fn, p):
    packed = pack_fn(p)
    packed = {k: (_bf(v) if k in _MATRICES else v.astype(F32)) for k, v in packed.items()}
    shapes = {k: jax.ShapeDtypeStruct(v.shape, F32) for k, v in p.items()}
    unpack = jax.linear_transpose(pack_fn, shapes)
    return packed, lambda g: unpack(g)[0]


def _ffn_fwd(x, w, *, s, ts):
    h = _rmsnorm(_cols(x, D_MODEL, 0, ts), w["norm_g"], n=D_MODEL, s=s, ts=ts, name="ffn_norm")
    up = _mm(h, w["w_up4"], b_layer=w["layer"], name="ffn_up")
    a = _ffn_act(up, w["conv_w"], w["conv_b"], ts=ts)
    y = _mm(a, w["w_down"], res=x, name="ffn_down")
    return y, dict(x=x, h=h, up=up, a=a)


def _ffn_bwd(dy, w, sv, *, s, ts):
    da = _mm(dy, w["w_down"], tb=True, name="ffn_down_dx")
    g_down = _mm(sv["a"], dy, ta=True, name="ffn_down_dw")
    dgate, dval, g_cw, g_cb = _ffn_act_bwd(sv["up"], da, w["conv_w"], w["conv_b"], ts=ts)
    dup = jnp.concatenate([dgate, dval], axis=1)
    dh = _mm(dup, w["w_up4"], tb=True, b_layer=w["layer"], name="ffn_up_dx")
    g_up = _mm(sv["h"], dup, ta=True, out_chips=True, name="ffn_up_dw")
    dx, g_norm = _rmsnorm_bwd(_cols(sv["x"], D_MODEL, 0, ts), w["norm_g"], dh, dy, n=D_MODEL, s=s, ts=ts,
                              name="ffn_norm_bwd")
    return dx, dict(w_up=g_up, w_down=g_down, conv_w=g_cw, conv_b=g_cb, norm_g=g_norm)


def _flash_tiles(s):
    return min(s, 512), min(s, 1024)


def _even_fwd(x, w, tabs, *, s, ts):
    cos_m, sin_m, cos_r, sin_r = tabs
    h = _rmsnorm(_cols(x, D_MODEL, 0, ts), w["mix_g"], n=D_MODEL, s=s, ts=ts, name="mix_norm")
    p = _mm(h, w["w_in"], name="even_in")
    cqn = _rmsnorm(_cols(p, EV_CQ, EV_RET // EV_CQ, ts), w["q_norm"], n=MLA_QR, s=s, ts=ts, name="mla_q_norm")
    ckvn = _rmsnorm(_cols(p, MLA_KVR, (EV_RET + EV_CQ) // MLA_KVR, ts), w["kv_norm"], n=MLA_KVR, s=s, ts=ts,
                    name="mla_kv_norm")
    q_pre = _mm(cqn, w["w_uq"], name="mla_uq")
    kv_pre = _mm(ckvn, w["w_ukv"], name="mla_ukv")
    q, k, v = _mla_prep(q_pre, kv_pre, p, cos_m, sin_m, w["qhn"], w["khn"], s=s, ts=ts)
    tq, tk = _flash_tiles(s)
    o, lse = _flash_fwd(q, k, v, tq=tq, tk=tk)
    o2, r_prev = _ret_fwd(p, cos_r, sin_r, w["theta_l"])
    r = _post_fwd(o2, _cols(p, RET_H * LANES, 3, ts), w["ret_gain"], group=LANES, n=RET_DV, s=s, ts=ts,
                  name="ret_post")
    ar = jnp.concatenate([o, r], axis=1)
    y = _mm(ar, w["w_out"], res=x, name="even_out")
    return y, dict(x=x, h=h, p=p, cqn=cqn, ckvn=ckvn, q_pre=q_pre, kv_pre=kv_pre, q=q, k=k, v=v, o=o, lse=lse,
                   o2=o2, r_prev=r_prev, ar=ar)


def _even_bwd(dy, w, sv, tabs, *, s, ts):
    cos_m, sin_m, cos_r, sin_r = tabs
    p = sv["p"]
    wh = MLA_H * LANES
    dar = _mm(dy, w["w_out"], tb=True, name="even_out_dx")
    g_out = _mm(sv["ar"], dy, ta=True, name="even_out_dw")
    tq, tk = _flash_tiles(s)
    do_attn = _attn_bwd_prep(dar, sv["o"], s=s, ts=ts)
    dq, dk, dv = _flash_bwd(sv["q"], sv["k"], sv["v"], do_attn, sv["lse"], tq=tq, tk=tk)
    dq_pre, dk_pre, dkr, g_qhn, g_khn = _mla_prep_bwd(sv["q_pre"], sv["kv_pre"], p, cos_m, sin_m, w["qhn"], w["khn"],
                                                      dq, dk, s=s, ts=ts)
    dkv_pre = jnp.concatenate([dk_pre, dv], axis=1)
    dckvn = _mm(dkv_pre, w["w_ukv"], tb=True, name="mla_ukv_dx")
    g_ukv = _mm(sv["ckvn"], dkv_pre, ta=True, name="mla_ukv_dw")
    dcqn = _mm(dq_pre, w["w_uq"], tb=True, name="mla_uq_dx")
    g_uq = _mm(sv["cqn"], dq_pre, ta=True, name="mla_uq_dw")
    dckv, g_kvn = _rmsnorm_bwd(_cols(p, MLA_KVR, (EV_RET + EV_CQ) // MLA_KVR, ts), w["kv_norm"], dckvn, None,
                               n=MLA_KVR, s=s, ts=ts, name="mla_kv_norm_bwd")
    dcq, g_qn = _rmsnorm_bwd(_cols(p, EV_CQ, EV_RET // EV_CQ, ts), w["q_norm"], dcqn, None, n=MLA_QR, s=s, ts=ts,
                             name="mla_q_norm_bwd")
    do, drg, g_gain = _post_bwd(sv["o2"], _cols(p, wh, 3, ts), w["ret_gain"], _cols(dar, wh, 1, ts),
                                group=LANES, n=RET_DV, s=s, ts=ts, name="ret_post_bwd")
    dq2, dk2, dv2, dth = _ret_bwd(p, cos_r, sin_r, w["theta_l"], w["theta_h"], sv["r_prev"], do)
    drq, drk, drv = (_sum2(a, s=s, ts=ts, name="sum_dirs_1024") for a in (dq2, dk2, dv2))
    dp = jnp.concatenate([drq, drk, drv, drg, _bf(dcq), _bf(dckv), dkr, jnp.zeros((s, LANES), BF16)], axis=1)
    dh = _mm(dp, w["w_in"], tb=True, name="even_in_dx")
    g_in = _mm(sv["h"], dp, ta=True, name="even_in_dw")
    dx, g_mix = _rmsnorm_bwd(_cols(sv["x"], D_MODEL, 0, ts), w["mix_g"], dh, dy, n=D_MODEL, s=s, ts=ts,
                             name="mix_norm_bwd")
    grads = dict(w_in=g_in, w_uq=g_uq, w_ukv=g_ukv, w_out=g_out, mix_g=g_mix, q_norm=g_qn, kv_norm=g_kvn,
                 qhn=g_qhn, khn=g_khn, ret_gain=g_gain)
    return dx, grads, dth[:, :, 0]


def _odd_fwd(x, w, *, s, ts):
    h = _rmsnorm(_cols(x, D_MODEL, 0, ts), w["mix_g"], n=D_MODEL, s=s, ts=ts, name="mix_norm")
    p = _mm(h, w["w_in"], name="odd_in")
    o2, s_prev = _gla_fwd(p, w["wg2"], w["bg2"])
    g = _post_fwd(o2, _cols(p, GLA_H * GLA_DV, 2, ts), w["gla_gain"], group=GLA_DV, n=GLA_DV, s=s, ts=ts,
                  name="gla_post")
    y = _mm(g, w["w_out"], res=x, name="odd_out")
    return y, dict(x=x, h=h, p=p, o2=o2, s_prev=s_prev, g=g)


def _odd_bwd(dy, w, sv, *, s, ts):
    p = sv["p"]
    wv = GLA_H * GLA_DV
    dg = _mm(dy, w["w_out"], tb=True, name="odd_out_dx")
    g_out = _mm(sv["g"], dy, ta=True, name="odd_out_dw")
    do, dgr, g_gain = _post_bwd(sv["o2"], _cols(p, wv, 2, ts), w["gla_gain"], _cols(dg, wv, 0, ts),
                                group=GLA_DV, n=GLA_DV, s=s, ts=ts, name="gla_post_bwd")
    dq2, dk2, dv2, dga2, g_wg, g_bg = _gla_bwd(p, w["wg2"], w["bg2"], sv["s_prev"], do)
    dq = _sum2(dq2, s=s, ts=ts, name="sum_dirs_512")
    dk = _sum2(dk2, s=s, ts=ts, name="sum_dirs_512")
    dv = _sum2(dv2, s=s, ts=ts, name="sum_dirs_1024")
    dga = _sum2(dga2, s=s, ts=ts, name="sum_dirs_128")
    dp = jnp.concatenate([dq, dk, dv, dgr, dga], axis=1)
    dh = _mm(dp, w["w_in"], tb=True, name="odd_in_dx")
    g_in = _mm(sv["h"], dp, ta=True, name="odd_in_dw")
    dx, g_mix = _rmsnorm_bwd(_cols(sv["x"], D_MODEL, 0, ts), w["mix_g"], dh, dy, n=D_MODEL, s=s, ts=ts,
                             name="mix_norm_bwd")
    return dx, dict(w_in=g_in, wg2=g_wg, bg2=g_bg, w_out=g_out, mix_g=g_mix, gla_gain=g_gain)


_EVEN_NAMES = dict(mix_norm="mix_norm_even", w_in="w_in_even", q_norm="mla_q_norm", kv_norm="mla_kv_norm",
                   w_uq="mla_w_uq", w_ukv="mla_w_ukv", q_head_norm="mla_q_head_norm", k_head_norm="mla_k_head_norm",
                   ret_out_norm="ret_out_norm", w_out="w_out_even")
_ODD_NAMES = dict(mix_norm="mix_norm_odd", w_in="w_in_odd", w_gate_fwd="gla_w_gate_fwd", b_gate_fwd="gla_b_gate_fwd",
                  w_gate_bwd="gla_w_gate_bwd", b_gate_bwd="gla_b_gate_bwd", gla_out_norm="gla_out_norm",
                  w_out="w_out_odd")

def _local_step(x, pos, target, full):
    s = x.shape[0]
    ts = min(s, 256)
    tabs = _rope_tables(pos, MLA_ROPE, MLA_NOPE) + _rope_tables(pos, RET_DK, 0)
    layers = []
    for layer in range(DEPTH):
        i = layer // 2
        names = _EVEN_NAMES if layer % 2 == 0 else _ODD_NAMES
        wm, unpack_m = _packed(_pack_even if layer % 2 == 0 else _pack_odd, {k: full[n][i] for k, n in names.items()})
        if layer % 2 == 0:
            th = jnp.stack([full["ret_theta_fwd"][i], full["ret_theta_bwd"][i]]).astype(F32)
            wm["theta_h"] = jnp.broadcast_to(th[:, :, None], (2, RET_H, LANES))
            wm["theta_l"] = wm["theta_h"].reshape(2, 1, RET_H * LANES)
        wf = dict(layer=layer, w_up4=full["ffn_w_up"], w_down=_bf(full["ffn_w_down"][layer]),
                  conv_w=full["ffn_conv_w"][layer].astype(F32), conv_b=full["ffn_conv_b"][layer][None, :].astype(F32),
                  norm_g=full["ffn_norm"][layer][None, :].astype(F32))
        layers.append((wm, unpack_m, wf))

    saved = []
    for layer, (wm, _, wf) in enumerate(layers):
        if layer % 2 == 0:
            x, sv_m = _even_fwd(x, wm, tabs, s=s, ts=ts)
        else:
            x, sv_m = _odd_fwd(x, wm, s=s, ts=ts)
        x, sv_f = _ffn_fwd(x, wf, s=s, ts=ts)
        saved.append((sv_m, sv_f))

    dy, sq = _loss_head(x, target, s=s, ts=ts)
    loss = 0.5 / D_MODEL * jnp.sum(sq)

    grads = {}

    def put(name, idx, g):
        grads.setdefault(name, {})[idx] = g

    for layer in reversed(range(DEPTH)):
        wm, unpack_m, wf = layers[layer]
        sv_m, sv_f = saved[layer]
        i = layer // 2
        dy, gf = _ffn_bwd(dy, wf, sv_f, s=s, ts=ts)
        put("ffn_w_up", layer, gf["w_up"])
        put("ffn_w_down", layer, gf["w_down"])
        put("ffn_conv_w", layer, gf["conv_w"])
        put("ffn_conv_b", layer, gf["conv_b"][0])
        put("ffn_norm", layer, gf["norm_g"][0])
        if layer % 2 == 0:
            dy, gm, dth = _even_bwd(dy, wm, sv_m, tabs, s=s, ts=ts)
            put("ret_theta_fwd", i, dth[0])
            put("ret_theta_bwd", i, dth[1])
            names = _EVEN_NAMES
        else:
            dy, gm = _odd_bwd(dy, wm, sv_m, s=s, ts=ts)
            names = _ODD_NAMES
        for k, g in unpack_m(gm).items():
            put(names[k], i, g)
    return loss, dy, {n: [g[j] for j in range(len(g))] for n, g in grads.items()}


HBM_SPEC = pl.BlockSpec(memory_space=pltpu.HBM)
VMEM_SPEC = pl.BlockSpec(memory_space=pltpu.VMEM)
CHIPS = 4
CORES = 2
ROW = 8 * LANES


def _xyc():
    return lax.axis_index("x"), lax.axis_index("y"), lax.axis_index("c")


def _other_chips(x, y):
    return [(1 - x, y), (x, 1 - y), (1 - x, 1 - y)]


def _remote(src, dst, send, recv, dev):
    return pltpu.make_async_remote_copy(src_ref=src, dst_ref=dst, send_sem=send, recv_sem=recv,
                                        device_id=dev, device_id_type=MESH)


def _sems(n):
    return pltpu.SemaphoreType.DMA((n,))


def _gather_chips(arrs):
    n = len(arrs)

    def body(*refs):
        ins, outs = refs[:n], refs[n:2 * n]
        send, recv, loc = refs[2 * n:]
        x, y, c = _xyc()
        me = 2 * x + y
        local = [pltpu.make_async_copy(ins[t], outs[t].at[me], loc.at[t]) for t in range(n)]
        for cp in local:
            cp.start()
        sends = []
        for j, (px, py) in enumerate(_other_chips(x, y)):
            for t in range(n):
                cp = _remote(ins[t], outs[t].at[me], send.at[n * j + t], recv.at[n * j + t], (px, py, c))
                cp.start()
                sends.append(cp)
        for j, (px, py) in enumerate(_other_chips(x, y)):
            for t in range(n):
                _remote(ins[t], outs[t].at[2 * px + py], send.at[n * j + t], recv.at[n * j + t], (px, py, c)).wait_recv()
        for cp in sends:
            cp.wait_send()
        for cp in local:
            cp.wait()

    return pl.pallas_call(
        body, name="gather_chips", in_specs=[HBM_SPEC] * n, out_specs=[HBM_SPEC] * n,
        out_shape=[jax.ShapeDtypeStruct((CHIPS,) + a.shape, a.dtype) for a in arrs],
        scratch_shapes=[_sems(3 * n), _sems(3 * n), _sems(n)],
    )(*arrs)


def _half_rows(ref, axis, half, which):
    idx = (slice(None),) * axis + (pl.ds(pl.multiple_of(which * half, 8), half),)
    return ref.at[idx]


def _swap_halves(groups):
    flat = [(w, l, a) for w, layers in enumerate(groups) for l, a in enumerate(layers)]
    n, nw = len(flat), len(groups)

    def body(*refs):
        ins, kept, got = refs[:n], refs[n:n + nw], refs[n + nw:n + 2 * nw]
        send, recv, loc = refs[n + 2 * nw:]
        x, y, c = _xyc()
        copies = []
        for t, (w, l, a) in enumerate(flat):
            half = a.shape[1] // CORES
            lc = pltpu.make_async_copy(_half_rows(ins[t], 1, half, c), kept[w].at[:, l], loc.at[t])
            rc = _remote(_half_rows(ins[t], 1, half, 1 - c), got[w].at[:, l], send.at[t], recv.at[t], (x, y, 1 - c))
            lc.start()
            rc.start()
            copies += [lc, rc]
        for cp in copies:
            cp.wait()

    def stack(layers):
        ch, r, cc = layers[0].shape
        return jax.ShapeDtypeStruct((ch, len(layers), r // CORES, cc), layers[0].dtype)

    outs = pl.pallas_call(
        body, name="swap_halves", in_specs=[HBM_SPEC] * n, out_specs=[HBM_SPEC] * (2 * nw),
        out_shape=[stack(g) for g in groups] * 2,
        scratch_shapes=[_sems(n), _sems(n), _sems(n)],
    )(*[a for _, _, a in flat])
    return outs[:nw], outs[nw:]


def _scatter_chips(arrs):
    n = len(arrs)

    def body(*refs):
        ins, outs = refs[:n], refs[n:2 * n]
        send, recv, loc = refs[2 * n:]
        x, y, c = _xyc()
        me = 2 * x + y
        copies = []
        for t in range(n):
            cp = pltpu.make_async_copy(ins[t].at[me], outs[t].at[me], loc.at[t])
            cp.start()
            copies.append(cp)
        sends = []
        for j, (px, py) in enumerate(_other_chips(x, y)):
            for t in range(n):
                cp = _remote(ins[t].at[2 * px + py], outs[t].at[me], send.at[n * j + t], recv.at[n * j + t], (px, py, c))
                cp.start()
                sends.append(cp)
        for j, (px, py) in enumerate(_other_chips(x, y)):
            for t in range(n):
                _remote(ins[t].at[me], outs[t].at[2 * px + py], send.at[n * j + t], recv.at[n * j + t],
                        (px, py, c)).wait_recv()
        for cp in sends:
            cp.wait_send()
        for cp in copies:
            cp.wait()

    return pl.pallas_call(
        body, name="scatter_chips", in_specs=[HBM_SPEC] * n, out_specs=[HBM_SPEC] * n,
        out_shape=[jax.ShapeDtypeStruct(a.shape, a.dtype) for a in arrs],
        scratch_shapes=[_sems(3 * n), _sems(3 * n), _sems(n)],
    )(*arrs)


def _gather_cores(arrs):
    n = len(arrs)

    def body(*refs):
        ins, outs = refs[:n], refs[n:2 * n]
        send, recv, loc = refs[2 * n:]
        x, y, c = _xyc()
        copies, sends = [], []
        for t in range(n):
            half = arrs[t].shape[1]
            lc = pltpu.make_async_copy(ins[t], _half_rows(outs[t], 1, half, c), loc.at[t])
            rc = _remote(ins[t], _half_rows(outs[t], 1, half, c), send.at[t], recv.at[t], (x, y, 1 - c))
            lc.start()
            rc.start()
            copies.append(lc)
            sends.append(rc)
        for t in range(n):
            half = arrs[t].shape[1]
            _remote(ins[t], _half_rows(outs[t], 1, half, 1 - c), send.at[t], recv.at[t], (x, y, 1 - c)).wait_recv()
        for cp in sends:
            cp.wait_send()
        for cp in copies:
            cp.wait()

    return pl.pallas_call(
        body, name="gather_cores", in_specs=[HBM_SPEC] * n, out_specs=[HBM_SPEC] * n,
        out_shape=[jax.ShapeDtypeStruct((a.shape[0], CORES * a.shape[1], a.shape[2]), a.dtype) for a in arrs],
        scratch_shapes=[_sems(n), _sems(n), _sems(n)],
    )(*arrs)


def _all_reduce_devices(v):
    n_dev = CHIPS * CORES

    def body(v_ref, o_ref, buf, send, recv):
        x, y, c = _xyc()
        me = 4 * x + 2 * y + c
        buf[pl.ds(me, 1)] = v_ref[...][None]
        sends = []
        for m in range(1, n_dev):
            px = 1 - x if m & 4 else x
            py = 1 - y if m & 2 else y
            pc = 1 - c if m & 1 else c
            cp = _remote(v_ref, buf.at[me], send.at[m - 1], recv.at[m - 1], (px, py, pc))
            cp.start()
            sends.append((cp, 4 * px + 2 * py + pc))
        for m, (cp, peer) in enumerate(sends):
            _remote(v_ref, buf.at[peer], send.at[m], recv.at[m], (x, y, c)).wait_recv()
        for cp, _ in sends:
            cp.wait_send()
        acc = buf[0]
        for k in range(1, n_dev):
            acc = acc + buf[k]
        o_ref[...] = acc

    return pl.pallas_call(
        body, name="all_reduce_devices", in_specs=[VMEM_SPEC], out_specs=VMEM_SPEC,
        out_shape=jax.ShapeDtypeStruct(v.shape, F32),
        scratch_shapes=[pltpu.VMEM((n_dev,) + v.shape, F32), pltpu.SemaphoreType.DMA((n_dev - 1,)),
                        pltpu.SemaphoreType.DMA((n_dev - 1,))],
    )(v)


def _add_rows(a, parts, *, ts, name):
    r = a.shape[0] // parts
    width = a.shape[1]
    assert r % ts == 0
    nb = r // ts

    def fn(*vals):
        acc = vals[0]
        for v in vals[1:]:
            acc = acc + v
        return acc

    rows = [(a, pl.BlockSpec((ts, width), lambda i, o=j * nb: (i + o, 0))) for j in range(parts)]
    return _ew(fn, rows, [], [(width, F32)], s=r, ts=ts, name=name)[0]


def _add_pair(a, b, *, ts, name):
    width = a.shape[1]
    return _ew(lambda u, v: u + v, [_cols(a, width, 0, ts), _cols(b, width, 0, ts)], [], [(width, F32)],
               s=a.shape[0], ts=ts, name=name)[0]


_SHARDED = (("w_in_even", 2), ("mla_w_uq", 2), ("mla_w_ukv", 2), ("w_out_even", 1), ("w_in_odd", 2), ("w_out_odd", 1),
            ("ffn_w_up", 2), ("ffn_w_down", 1),
            ("mix_norm_odd", 1), ("gla_w_gate_fwd", 2), ("gla_b_gate_fwd", 1), ("gla_w_gate_bwd", 2),
            ("gla_b_gate_bwd", 1), ("gla_out_norm", 2), ("ffn_conv_w", 2))
_N_MATRICES = 8
_REPLICATED = ("mix_norm_even", "mla_q_norm", "mla_kv_norm", "mla_q_head_norm", "mla_k_head_norm", "ret_theta_fwd",
               "ret_theta_bwd", "ret_out_norm", "ffn_norm", "ffn_conv_b")
_WEIGHTS = ("mix_norm_even", "w_in_even", "mla_q_norm", "mla_kv_norm", "mla_w_uq", "mla_w_ukv", "mla_q_head_norm",
            "mla_k_head_norm", "ret_theta_fwd", "ret_theta_bwd", "ret_out_norm", "w_out_even", "mix_norm_odd",
            "w_in_odd", "gla_w_gate_fwd", "gla_b_gate_fwd", "gla_w_gate_bwd", "gla_b_gate_bwd", "gla_out_norm",
            "w_out_odd", "ffn_norm", "ffn_w_up", "ffn_conv_w", "ffn_conv_b", "ffn_w_down")


def _flatten(arrs, row_multiple, dtype):
    flat = jnp.concatenate([a.reshape(-1).astype(dtype) for a in arrs])
    per = ROW * row_multiple
    total = -(-flat.shape[0] // per) * per
    return jnp.pad(flat, (0, total - flat.shape[0])).reshape(-1, ROW)


def _unflatten(flat, shapes):
    flat = flat.reshape(-1)
    out, o = [], 0
    for shp in shapes:
        n = math.prod(shp)
        out.append(flat[o:o + n].reshape(shp))
        o += n
    return out


def kernel(x, positions, mix_norm_even, w_in_even, mla_q_norm, mla_kv_norm, mla_w_uq, mla_w_ukv, mla_q_head_norm, mla_k_head_norm, ret_theta_fwd, ret_theta_bwd, ret_out_norm, w_out_even, mix_norm_odd, w_in_odd, gla_w_gate_fwd, gla_b_gate_fwd, gla_w_gate_bwd, gla_b_gate_bwd, gla_out_norm, w_out_odd, ffn_norm, ffn_w_up, ffn_conv_w, ffn_conv_b, ffn_w_down, loss_target, m_mix_norm_even, m_w_in_even, m_mla_q_norm, m_mla_kv_norm, m_mla_w_uq, m_mla_w_ukv, m_mla_q_head_norm, m_mla_k_head_norm, m_ret_theta_fwd, m_ret_theta_bwd, m_ret_out_norm, m_w_out_even, m_mix_norm_odd, m_w_in_odd, m_gla_w_gate_fwd, m_gla_b_gate_fwd, m_gla_w_gate_bwd, m_gla_b_gate_bwd, m_gla_out_norm, m_w_out_odd, m_ffn_norm, m_ffn_w_up, m_ffn_conv_w, m_ffn_conv_b, m_ffn_w_down, v_mix_norm_even, v_w_in_even, v_mla_q_norm, v_mla_kv_norm, v_mla_w_uq, v_mla_w_ukv, v_mla_q_head_norm, v_mla_k_head_norm, v_ret_theta_fwd, v_ret_theta_bwd, v_ret_out_norm, v_w_out_even, v_mix_norm_odd, v_w_in_odd, v_gla_w_gate_fwd, v_gla_b_gate_fwd, v_gla_w_gate_bwd, v_gla_b_gate_bwd, v_gla_out_norm, v_w_out_odd, v_ffn_norm, v_ffn_w_up, v_ffn_conv_w, v_ffn_conv_b, v_ffn_w_down):
    args = dict(locals())
    x2, pos, target = args["x"][0], args["positions"][0], args["loss_target"][0]
    axis = dict(_SHARDED)
    mats = [n for n, _ in _SHARDED[:_N_MATRICES]]
    smalls = [n for n, _ in _SHARDED[_N_MATRICES:]]
    small_shapes = [args[n].shape for n in smalls]

    gathered = _gather_chips([_bf(args[n]) for n in mats] + [_flatten([args[n] for n in smalls], 2 * HALO, F32)])
    full = {n: args[n] for n in _REPLICATED}
    for n, g in zip(mats, gathered):
        full[n] = g if n == "ffn_w_up" else jnp.concatenate([g[j] for j in range(CHIPS)], axis=axis[n])
    per_chip = [_unflatten(gathered[-1][j], small_shapes) for j in range(CHIPS)]
    for k, n in enumerate(smalls):
        full[n] = jnp.concatenate([per_chip[j][k] for j in range(CHIPS)], axis=axis[n])

    loss, grad_x, grads = _local_step(x2, pos, target, full)
    loss = lax.psum(loss, ("x", "y", "c"))

    def by_chip(n, g):
        if n == "ffn_w_up":
            return g
        if axis[n] == 1:
            return g.reshape((CHIPS, g.shape[0] // CHIPS) + g.shape[1:])
        return jnp.stack(jnp.split(g, CHIPS, axis=axis[n] - 1))

    groups = [[by_chip(n, g) for g in grads[n]] for n in mats]
    small_parts = [jnp.split(jnp.stack(grads[n]), CHIPS, axis=axis[n]) for n in smalls]
    groups.append([jnp.stack([_flatten([p[j] for p in small_parts], 2 * HALO, F32) for j in range(CHIPS)])])
    names = mats + ["small"]
    kept, got = _swap_halves(groups)
    tiles, chip_sums = [], []
    for n, a, b in zip(names, kept, got):
        ch, nl, half, cols = a.shape
        ts = _rows_tile(half, cols)
        tiles.append(ts)
        chip_sums.append(_add_pair(a.reshape(-1, cols), b.reshape(-1, cols), ts=ts,
                                   name="add_core_halves_" + n).reshape(a.shape))
    parts = _scatter_chips(chip_sums)
    sums = [_add_rows(p.reshape(-1, p.shape[-1]), CHIPS, ts=ts, name="add_chip_parts_" + n).reshape(p.shape[1:])
            for n, p, ts in zip(names, parts, tiles)]
    reduced = _gather_cores(sums)

    res = {}

    def update(n, w, g, m, v, ts):
        cols = g.shape[-1]
        outs = _adamw(w.reshape(-1, cols), g.reshape(-1, cols), m.reshape(-1, cols), v.reshape(-1, cols), ts=ts,
                      name="adamw_" + n)
        return [g] + [o.reshape(g.shape) for o in outs]

    kinds = ("grad", "delta", "new_m", "new_v")
    for n, g, ts in zip(mats, reduced, tiles):
        for kind, a in zip(kinds, update(n, args[n], g, args["m_" + n], args["v_" + n], ts)):
            res[kind + "_" + n] = a
    w_s, m_s, v_s = (_flatten([args[pre + n] for n in smalls], 2 * HALO, F32) for pre in ("", "m_", "v_"))
    for kind, flat in zip(kinds, update("small", w_s, reduced[-1][0], m_s, v_s, tiles[-1])):
        for n, a in zip(smalls, _unflatten(flat, small_shapes)):
            res[kind + "_" + n] = a

    rep_shapes = [args[n].shape for n in _REPLICATED]
    g_rep = _all_reduce_devices(_flatten([jnp.stack(grads[n]) for n in _REPLICATED], HALO, F32))
    w_rep, m_rep, v_rep = (_flatten([args[pre + n] for n in _REPLICATED], HALO, F32) for pre in ("", "m_", "v_"))
    for kind, flat in zip(kinds, update("replicated", w_rep, g_rep, m_rep, v_rep, g_rep.shape[0])):
        for n, a in zip(_REPLICATED, _unflatten(flat, rep_shapes)):
            res[kind + "_" + n] = a

    outs = [loss, grad_x[None]]
    for kind in ("grad", "delta", "new_m", "new_v"):
        outs += [res[kind + "_" + n] for n in _WEIGHTS]
    return tuple(outs)
```

```python
import math

import jax
import jax.numpy as jnp
from jax import lax
from jax.experimental import pallas as pl
from jax.experimental.pallas import tpu as pltpu

F32 = jnp.float32
BF16 = jnp.bfloat16
MESH = pl.DeviceIdType.MESH

EPS = 1e-6
D_MODEL = 1024
DEPTH = 4
LANES = 128
MLA_H, MLA_QR, MLA_KVR, MLA_NOPE, MLA_ROPE, MLA_V = 8, 384, 256, 64, 32, 64
MLA_QK = MLA_NOPE + MLA_ROPE
MLA_SCALE = MLA_QK ** -0.5
RET_H, RET_DK, RET_DV, RET_C = 8, 64, 64, 128
GLA_H, GLA_DK, GLA_DV, GLA_R, GLA_TAU, GLA_C = 4, 128, 256, 16, 16.0, 64
D_FF = 2816
ROPE_THETA = 10000.0
LN2 = math.log(2.0)
ADAM_LR, ADAM_B1, ADAM_B2, ADAM_EPS, ADAM_WD, ADAM_STEP = 0.001, 0.9, 0.999, 1e-08, 0.01, 10

EV_RET = 4 * RET_H * LANES
EV_CQ = 512
EV_W = 5120
EV_KR_BLK = (EV_RET + EV_CQ + MLA_KVR) // LANES
OD_W = 3200
OD_GA_BLK = 3072 // LANES

VMEM_LIMIT = 56 * 1024 * 1024
MM_TILE_CAP = 1408
V_ONES = (MLA_V, MLA_V + 1)


def _cp(sem):
    return pltpu.CompilerParams(dimension_semantics=sem, vmem_limit_bytes=VMEM_LIMIT)


def _dot(a, b):
    return jnp.dot(a, b, preferred_element_type=F32)


def _dot_nt(a, b):
    return lax.dot_general(a, b, (((1,), (1,)), ((), ())), preferred_element_type=F32)


def _dot_tn(a, b):
    return lax.dot_general(a, b, (((0,), (0,)), ((), ())), preferred_element_type=F32)


def _bf(x):
    return x.astype(BF16)


def _split3(x):
    h1 = _bf(x)
    r1 = x - h1.astype(F32)
    h2 = _bf(r1)
    h3 = _bf(r1 - h2.astype(F32))
    return h1, h2, h3


def _tile(n, cap):
    if n <= cap:
        return n
    best = None
    for t in range(LANES, cap + 1, LANES):
        if n % t == 0:
            best = t
    assert best is not None, n
    return best


def _mm(a, b, *, ta=False, tb=False, res=None, out_dtype=F32, b_layer=None, out_chips=False, name):
    assert not (ta and tb)
    if ta:
        kdim, m = a.shape
    else:
        m, kdim = a.shape
    if b_layer is not None:
        rows_b, cols_b = b.shape[2], b.shape[0] * b.shape[3]
    else:
        rows_b, cols_b = b.shape
    n, kb = (rows_b, cols_b) if tb else (cols_b, rows_b)
    assert kb == kdim, (a.shape, b.shape, ta, tb)
    tm, tn, tk = _tile(m, MM_TILE_CAP), _tile(n, MM_TILE_CAP), _tile(kdim, MM_TILE_CAP)
    nk = kdim // tk
    has_res = res is not None
    vmem = (2 * tm * tk * a.dtype.itemsize + 2 * tk * tn * b.dtype.itemsize
            + 2 * tm * tn * jnp.dtype(out_dtype).itemsize + (2 * tm * tn * 4 if has_res else 0)
            + (tm * tn * 4 if nk > 1 else 0))
    assert vmem <= VMEM_LIMIT - 8 * 1024 * 1024, (name, vmem)
    a_spec = (pl.BlockSpec((tk, tm), lambda i, j, k: (k, i)) if ta
              else pl.BlockSpec((tm, tk), lambda i, j, k: (i, k)))
    if b_layer is not None:
        per_chip = b.shape[3]
        if tb:
            assert tk == per_chip
            b_spec = pl.BlockSpec((None, None, tn, tk), lambda i, j, k: (k, b_layer, j, 0))
        else:
            assert tn == per_chip
            b_spec = pl.BlockSpec((None, None, tk, tn), lambda i, j, k: (j, b_layer, k, 0))
    else:
        b_spec = (pl.BlockSpec((tn, tk), lambda i, j, k: (j, k)) if tb
                  else pl.BlockSpec((tk, tn), lambda i, j, k: (k, j)))
    if out_chips:
        assert n // tn == CHIPS and not has_res
        o_spec = pl.BlockSpec((None, tm, tn), lambda i, j, k: (j, i, 0))
        out_struct = jax.ShapeDtypeStruct((CHIPS, m, tn), out_dtype)
    else:
        o_spec = pl.BlockSpec((tm, tn), lambda i, j, k: (i, j))
        out_struct = jax.ShapeDtypeStruct((m, n), out_dtype)

    def product(a_ref, b_ref):
        av, bv = _bf(a_ref[...]), _bf(b_ref[...])
        if ta:
            return _dot_tn(av, bv)
        if tb:
            return _dot_nt(av, bv)
        return _dot(av, bv)

    def body(*refs):
        a_ref, b_ref = refs[:2]
        r_ref = refs[2] if has_res else None
        o_ref = refs[3] if has_res else refs[2]

        def finish(r):
            if has_res:
                r = r + r_ref[...]
            o_ref[...] = r.astype(o_ref.dtype)

        if nk == 1:
            finish(product(a_ref, b_ref))
            return
        acc = refs[-1]
        k = pl.program_id(2)

        @pl.when(k == 0)
        def _():
            acc[...] = product(a_ref, b_ref)

        @pl.when(k > 0)
        def _():
            acc[...] += product(a_ref, b_ref)

        @pl.when(k == nk - 1)
        def _():
            finish(acc[...])

    ins = [a, b] + ([res] if has_res else [])
    in_specs = [a_spec, b_spec] + ([o_spec] if has_res else [])
    return pl.pallas_call(
        body, name=name, grid=(m // tm, n // tn, nk),
        in_specs=in_specs, out_specs=o_spec, out_shape=out_struct,
        scratch_shapes=[pltpu.VMEM((tm, tn), F32)] if nk > 1 else [],
        compiler_params=_cp(("parallel", "parallel", "arbitrary")),
    )(*ins)


def _ew(fn, rows, pars, outs, accs=(), *, s, ts, name):
    n_in = len(rows) + len(pars)
    n_o = len(outs)

    def body(*refs):
        i = pl.program_id(0)
        vals = fn(*[r[...] for r in refs[:n_in]])
        if not isinstance(vals, (tuple, list)):
            vals = (vals,)
        assert len(vals) == n_o + len(accs), (name, len(vals))
        for r, v in zip(refs[n_in:n_in + n_o], vals[:n_o]):
            r[...] = v.astype(r.dtype)
        for r, v in zip(refs[n_in + n_o:], vals[n_o:]):
            @pl.when(i == 0)
            def _(r=r, v=v):
                r[...] = v

            @pl.when(i > 0)
            def _(r=r, v=v):
                r[...] += v

    in_specs = [sp for _, sp in rows]
    in_specs += [pl.BlockSpec(p.shape, lambda i, nd=p.ndim: (0,) * nd) for p in pars]
    out_specs = [pl.BlockSpec((ts, w), lambda i: (i, 0)) for w, _ in outs]
    out_specs += [pl.BlockSpec((r, w), lambda i: (0, 0)) for r, w in accs]
    out_shape = [jax.ShapeDtypeStruct((s, w), dt) for w, dt in outs]
    out_shape += [jax.ShapeDtypeStruct((r, w), F32) for r, w in accs]
    return pl.pallas_call(
        body, name=name, grid=(s // ts,), in_specs=in_specs, out_specs=out_specs, out_shape=out_shape,
        compiler_params=_cp(("arbitrary",)),
    )(*[a for a, _ in rows], *pars)


def _cols(arr, width, blk, ts):
    return (arr, pl.BlockSpec((ts, width), lambda i, b=blk: (i, b)))


def _lead(arr, d, ts):
    return (arr, pl.BlockSpec((None, ts, arr.shape[2]), lambda i, d=d: (d, i, 0)))


def _rowsum(x):
    return jnp.sum(x, axis=0, keepdims=True)


def _lanesum(x):
    return jnp.sum(x, axis=-1, keepdims=True)


def _gsum(x, group):
    w = x.shape[-1]
    if group == w:
        return jnp.broadcast_to(_lanesum(x), x.shape)
    parts = [jnp.broadcast_to(_lanesum(x[:, g:g + group]), (x.shape[0], group)) for g in range(0, w, group)]
    return jnp.concatenate(parts, axis=-1)


def _gn(x, gain, group, n):
    rstd = lax.rsqrt(_gsum(x * x, group) * (1.0 / n) + EPS)
    xn = x * rstd
    return xn * gain, xn, rstd


def _gn_bwd(dy, xn, rstd, gain, group, n):
    dxn = dy * gain
    dx = rstd * (dxn - xn * (_gsum(dxn * xn, group) * (1.0 / n)))
    return dx, _rowsum(dy * xn)


def _sigmoid(x):
    return 1.0 / (1.0 + jnp.exp(-x))


def _rmsnorm(x_row, g, *, n, s, ts, name):
    w = g.shape[-1]

    def fn(x, gv):
        return _gn(x, gv, w, n)[0]

    return _ew(fn, [x_row], [g], [(w, BF16)], s=s, ts=ts, name=name)[0]


def _rmsnorm_bwd(x_row, g, dh, dres, *, n, s, ts, name):
    w = g.shape[-1]
    has_res = dres is not None

    def fn(x, dhv, *rest):
        gv = rest[-1]
        _, xn, rstd = _gn(x, gv, w, n)
        dx, dg = _gn_bwd(dhv, xn, rstd, gv, w, n)
        if has_res:
            dx = dx + rest[0]
        return dx, dg

    rows = [x_row, _cols(dh, w, 0, ts)] + ([_cols(dres, w, 0, ts)] if has_res else [])
    return _ew(fn, rows, [g], [(w, F32)], [(1, w)], s=s, ts=ts, name=name)


def _rope_tables(pos, real, offset):
    half = real // 2
    inv = ROPE_THETA ** (-jnp.arange(half, dtype=F32) / half)
    ang = pos.astype(F32)[:, None] * inv
    c, sn = jnp.cos(ang), jnp.sin(ang)
    s = pos.shape[0]
    cos_t = jnp.concatenate([jnp.ones((s, offset), F32), c, c,
                             jnp.ones((s, LANES - offset - real), F32)], axis=1)
    sin_t = jnp.concatenate([jnp.zeros((s, offset), F32), -sn, sn,
                             jnp.zeros((s, LANES - offset - real), F32)], axis=1)
    return cos_t, sin_t


def _rope(x, cos_t, sin_t, real, offset):
    half = real // 2
    lane = lax.broadcasted_iota(jnp.int32, x.shape, 1)
    partner = jnp.where(lane < offset + half, pltpu.roll(x, LANES - half, 1), pltpu.roll(x, half, 1))
    return x * cos_t + partner * sin_t


def _mla_prep(q_pre, kv_pre, p_even, cos_m, sin_m, qhn, khn, *, s, ts):
    w = MLA_H * LANES

    def fn(qp, kp, vp, kr, c, sn, gq, gk):
        qs, ks = [], []
        for h in range(MLA_H):
            sl = slice(h * LANES, (h + 1) * LANES)
            qn = _gn(qp[:, sl], gq, LANES, MLA_QK)[0]
            kn = _gn(kp[:, sl] + kr, gk, LANES, MLA_QK)[0]
            qs.append(_rope(qn, c, sn, MLA_ROPE, MLA_NOPE) * MLA_SCALE)
            ks.append(_rope(kn, c, sn, MLA_ROPE, MLA_NOPE))
        lane = lax.broadcasted_iota(jnp.int32, vp.shape, 1) % LANES
        ones = (lane == V_ONES[0]) | (lane == V_ONES[1])
        return jnp.concatenate(qs, axis=1), jnp.concatenate(ks, axis=1), jnp.where(ones, 1.0, vp)

    rows = [_cols(q_pre, w, 0, ts), _cols(kv_pre, w, 0, ts), _cols(kv_pre, w, 1, ts),
            _cols(p_even, LANES, EV_KR_BLK, ts), _cols(cos_m, LANES, 0, ts), _cols(sin_m, LANES, 0, ts)]
    return _ew(fn, rows, [qhn, khn], [(w, BF16)] * 3, s=s, ts=ts, name="mla_prep")


def _mla_prep_bwd(q_pre, kv_pre, p_even, cos_m, sin_m, qhn, khn, dq, dk, *, s, ts):
    w = MLA_H * LANES

    def fn(qp, kp, kr, c, sn, dqv, dkv, gq, gk):
        dqs, dks = [], []
        dkr = jnp.zeros_like(kr)
        dgq = jnp.zeros((1, LANES), F32)
        dgk = jnp.zeros((1, LANES), F32)
        for h in range(MLA_H):
            sl = slice(h * LANES, (h + 1) * LANES)
            _, qn, qr = _gn(qp[:, sl], gq, LANES, MLA_QK)
            _, kn, krs = _gn(kp[:, sl] + kr, gk, LANES, MLA_QK)
            dqn = _rope(dqv[:, sl] * MLA_SCALE, c, -sn, MLA_ROPE, MLA_NOPE)
            dkn = _rope(dkv[:, sl], c, -sn, MLA_ROPE, MLA_NOPE)
            dqh, g1 = _gn_bwd(dqn, qn, qr, gq, LANES, MLA_QK)
            dkh, g2 = _gn_bwd(dkn, kn, krs, gk, LANES, MLA_QK)
            dqs.append(dqh)
            dks.append(dkh)
            dkr = dkr + dkh
            dgq = dgq + g1
            dgk = dgk + g2
        return jnp.concatenate(dqs, axis=1), jnp.concatenate(dks, axis=1), dkr, dgq, dgk

    rows = [_cols(q_pre, w, 0, ts), _cols(kv_pre, w, 0, ts), _cols(p_even, LANES, EV_KR_BLK, ts),
            _cols(cos_m, LANES, 0, ts), _cols(sin_m, LANES, 0, ts), _cols(dq, w, 0, ts), _cols(dk, w, 0, ts)]
    return _ew(fn, rows, [qhn, khn], [(w, BF16), (w, BF16), (LANES, BF16)], [(1, LANES), (1, LANES)],
               s=s, ts=ts, name="mla_prep_bwd")


def _flash_fwd(q, k, v, *, tq, tk):
    s = q.shape[0]
    nq, nk = s // tq, s // tk
    rq = min(tq, 256)

    def body(q_ref, k_ref, v_ref, o_ref, lse_ref, m_s, acc):
        j = pl.program_id(2)

        @pl.when(j == 0)
        def _():
            m_s[...] = jnp.full_like(m_s, -jnp.inf)
            acc[...] = jnp.zeros_like(acc)

        kv, vv = k_ref[...], v_ref[...]
        for r in range(0, tq, rq):
            rows = slice(r, r + rq)
            sc = _dot_nt(q_ref[rows, :], kv)
            m_prev = m_s[rows, :]
            m_new = jnp.maximum(m_prev, jnp.max(sc, axis=-1, keepdims=True))
            p = jnp.exp(sc - jnp.tile(m_new, (1, tk // LANES)))
            acc[rows, :] = jnp.exp(m_prev - m_new) * acc[rows, :] + _dot(_bf(p), vv)
            m_s[rows, :] = m_new

        @pl.when(j == nk - 1)
        def _():
            a = acc[...]
            l = a[:, V_ONES[0]:V_ONES[0] + 1]
            o_ref[...] = (a / l).astype(o_ref.dtype)
            lse_ref[...] = m_s[:, 0:1] + jnp.log(l)

    qs = pl.BlockSpec((tq, LANES), lambda h, i, j: (i, h))
    ks = pl.BlockSpec((tk, LANES), lambda h, i, j: (j, h))
    return pl.pallas_call(
        body, name="mla_flash_fwd", grid=(MLA_H, nq, nk),
        in_specs=[qs, ks, ks],
        out_specs=[qs, pl.BlockSpec((None, tq, 1), lambda h, i, j: (h, i, 0))],
        out_shape=[jax.ShapeDtypeStruct((s, MLA_H * LANES), BF16), jax.ShapeDtypeStruct((MLA_H, s, 1), F32)],
        scratch_shapes=[pltpu.VMEM((tq, LANES), F32), pltpu.VMEM((tq, LANES), F32)],
        compiler_params=_cp(("parallel", "parallel", "arbitrary")),
    )(q, k, v)


def _attn_bwd_prep(dar, o, *, s, ts):
    w = MLA_H * LANES

    def fn(dov, ov):
        outs = []
        lane = lax.broadcasted_iota(jnp.int32, (dov.shape[0], LANES), 1)
        for h in range(MLA_H):
            sl = slice(h * LANES, (h + 1) * LANES)
            d = dov[:, sl]
            delta = _lanesum(d * ov[:, sl].astype(F32))
            hi = _bf(delta).astype(F32)
            outs.append(jnp.where(lane == V_ONES[0], -hi, jnp.where(lane == V_ONES[1], hi - delta, d)))
        return jnp.concatenate(outs, axis=1)

    return _ew(fn, [_cols(dar, w, 0, ts), _cols(o, w, 0, ts)], [], [(w, BF16)], s=s, ts=ts,
               name="mla_attn_bwd_prep")[0]


def _flash_bwd(q, k, v, do, lse, *, tq, tk):
    s = q.shape[0]
    nq, nk = s // tq, s // tk

    def body(q_ref, k_ref, v_ref, do_ref, lse_ref, dq_ref, dk_ref, dv_ref, dk_acc, dv_acc):
        j = pl.program_id(1)
        i = pl.program_id(2)
        qv, kv, dov = q_ref[...], k_ref[...], do_ref[...]
        p = jnp.exp(_dot_nt(qv, kv) - lse_ref[...])
        ds = _bf(p * _dot_nt(dov, v_ref[...]))
        dv_c = _dot_tn(_bf(p), dov)
        dk_c = _dot_tn(ds, qv)
        dq_c = _dot(ds, kv)
        rows = pl.ds(pl.multiple_of(i * tq, tq), tq)

        @pl.when(i == 0)
        def _():
            dk_acc[...] = dk_c
            dv_acc[...] = dv_c

        @pl.when(i > 0)
        def _():
            dk_acc[...] += dk_c
            dv_acc[...] += dv_c

        @pl.when(j == 0)
        def _():
            dq_ref[rows, :] = dq_c

        @pl.when(j > 0)
        def _():
            dq_ref[rows, :] += dq_c

        @pl.when(i == nq - 1)
        def _():
            dk_ref[...] = dk_acc[...]
            dv_ref[...] = dv_acc[...].astype(dv_ref.dtype)

    qs = pl.BlockSpec((tq, LANES), lambda h, j, i: (i, h))
    ks = pl.BlockSpec((tk, LANES), lambda h, j, i: (j, h))
    st = pl.BlockSpec((None, tq, 1), lambda h, j, i: (h, i, 0))
    return pl.pallas_call(
        body, name="mla_flash_bwd", grid=(MLA_H, nk, nq),
        in_specs=[qs, ks, ks, qs, st],
        out_specs=[pl.BlockSpec((s, LANES), lambda h, j, i: (0, h)), ks, ks],
        out_shape=[jax.ShapeDtypeStruct((s, MLA_H * LANES), F32), jax.ShapeDtypeStruct((s, MLA_H * LANES), F32),
                   jax.ShapeDtypeStruct((s, MLA_H * LANES), BF16)],
        scratch_shapes=[pltpu.VMEM((tk, LANES), F32), pltpu.VMEM((tk, LANES), F32)],
        compiler_params=_cp(("parallel", "arbitrary", "arbitrary")),
    )(q, k, v, do, lse)


def _ret_geometry(d, c):
    df = d.astype(F32)
    ii = lax.broadcasted_iota(jnp.int32, (c, c), 0).astype(F32)
    jj = lax.broadcasted_iota(jnp.int32, (c, c), 1).astype(F32)
    rel = (ii - jj) * (1.0 - 2.0 * df)
    mask = rel >= df
    rel0 = jnp.maximum(rel, 0.0)
    pos = lax.broadcasted_iota(jnp.int32, (c, 1), 0).astype(F32)
    ez = (c - 1 - pos) + df * (2.0 * pos - (c - 1))
    ex = (pos + 1.0) + df * (c - 1 - 2.0 * pos)
    return mask, rel0, ez, ex


def _chunk_index(n_chunks):
    return lambda d, n: n + d * (n_chunks - 1 - 2 * n)


def _ret_fwd(p_even, cos_r, sin_r, theta_l):
    s = p_even.shape[0]
    c = RET_C
    n_chunks = s // c
    w = RET_H * LANES
    cidx = _chunk_index(n_chunks)

    def body(q_ref, k_ref, v_ref, cos_ref, sin_ref, th_ref, o_ref, rp_ref, r_s):
        d = pl.program_id(0)
        n = pl.program_id(1)

        @pl.when(n == 0)
        def _():
            r_s[...] = jnp.zeros_like(r_s)

        lg = jnp.log1p(-jnp.exp(-th_ref[...] * LN2))
        mask, rel0, ez, ex = _ret_geometry(d, c)
        cs, sn = cos_ref[...], sin_ref[...]
        rp_ref[...] = r_s[...]
        for h in range(RET_H):
            sl = slice(h * LANES, (h + 1) * LANES)
            lgh = lg[:, h * LANES:h * LANES + 1]
            dm = jnp.where(mask, jnp.exp(lgh * rel0), 0.0)
            qh = _bf(_rope(q_ref[:, sl], cs, sn, RET_DK, 0))
            kf = _rope(k_ref[:, sl], cs, sn, RET_DK, 0) * (RET_DK ** -0.5)
            kh = _bf(kf)
            vh = _bf(v_ref[:, sl])
            rh = r_s[sl, :]
            a = _dot_nt(qh, kh) * dm
            o_ref[:, sl] = _dot(_bf(a), vh) + jnp.exp(lgh * ex) * _dot(qh, _bf(rh))
            zk = _bf(kf * jnp.exp(lgh * ez))
            r_s[sl, :] = jnp.exp(lgh * c) * rh + _dot_tn(zk, vh)

    def col(blk):
        return pl.BlockSpec((c, w), lambda d, n: (cidx(d, n), blk))

    tab = pl.BlockSpec((c, LANES), lambda d, n: (cidx(d, n), 0))
    return pl.pallas_call(
        body, name="ret_fwd", grid=(2, n_chunks),
        in_specs=[col(0), col(1), col(2), tab, tab, pl.BlockSpec((None, 1, w), lambda d, n: (d, 0, 0))],
        out_specs=[pl.BlockSpec((None, c, w), lambda d, n: (d, cidx(d, n), 0)),
                   pl.BlockSpec((None, None, w, LANES), lambda d, n: (d, cidx(d, n), 0, 0))],
        out_shape=[jax.ShapeDtypeStruct((2, s, w), F32), jax.ShapeDtypeStruct((2, n_chunks, w, LANES), F32)],
        scratch_shapes=[pltpu.VMEM((w, LANES), F32)],
        compiler_params=_cp(("arbitrary", "arbitrary")),
    )(p_even, p_even, p_even, cos_r, sin_r, theta_l)


def _ret_bwd(p_even, cos_r, sin_r, theta_l, theta_h, r_prev, do):
    s = p_even.shape[0]
    c = RET_C
    n_chunks = s // c
    w = RET_H * LANES
    fwd_idx = _chunk_index(n_chunks)

    def cidx(d, n):
        return fwd_idx(d, n_chunks - 1 - n)

    def body(q_ref, k_ref, v_ref, cos_ref, sin_ref, th_ref, thh_ref, rp_ref, do_ref,
             dq_ref, dk_ref, dv_ref, dth_ref, dr_s):
        d = pl.program_id(0)
        n = pl.program_id(1)

        @pl.when(n == 0)
        def _():
            dr_s[...] = jnp.zeros_like(dr_s)
            dth_ref[...] = jnp.zeros_like(dth_ref)

        lg = jnp.log1p(-jnp.exp(-th_ref[...] * LN2))
        mask, rel0, ez, ex = _ret_geometry(d, c)
        cs, sn = cos_ref[...], sin_ref[...]
        row = lax.broadcasted_iota(jnp.int32, (RET_H, LANES), 0)
        dlg = jnp.zeros((RET_H, LANES), F32)
        kscale = RET_DK ** -0.5
        for h in range(RET_H):
            sl = slice(h * LANES, (h + 1) * LANES)
            lgh = lg[:, h * LANES:h * LANES + 1]
            dm = jnp.where(mask, jnp.exp(lgh * rel0), 0.0)
            zeta = jnp.exp(lgh * ez)
            xi = jnp.exp(lgh * ex)
            gc = jnp.exp(lgh * c)
            qf = _rope(q_ref[:, sl], cs, sn, RET_DK, 0)
            qh = _bf(qf)
            kf = _rope(k_ref[:, sl], cs, sn, RET_DK, 0) * kscale
            kh = _bf(kf)
            zkf = kf * zeta
            zk = _bf(zkf)
            vh = _bf(v_ref[:, sl])
            dof = do_ref[:, sl]
            doh = _bf(dof)
            rp = rp_ref[sl, :]
            rpb = _bf(rp)
            drn = dr_s[sl, :]
            drb = _bf(drn)
            a = _dot_nt(qh, kh) * dm
            da0 = _dot_nt(doh, vh)
            da = _bf(da0 * dm)
            vdr = _dot_nt(vh, drb)
            dq_r = _dot(da, kh) + xi * _dot_nt(doh, rpb)
            dk_r = _dot_tn(da, qh) + zeta * vdr
            dv_ref[:, sl] = _dot_tn(_bf(a), doh) + _dot(zk, drb)
            dq_ref[:, sl] = _rope(dq_r, cs, -sn, RET_DK, 0)
            dk_ref[:, sl] = _rope(dk_r * kscale, cs, -sn, RET_DK, 0)
            dr_s[sl, :] = _dot_tn(_bf(qf * xi), doh) + gc * drn
            ocross = xi * _dot(qh, rpb)
            t = (jnp.sum(rel0 * a * da0, keepdims=True)
                 + jnp.sum(ex * dof * ocross, keepdims=True)
                 + c * gc * jnp.sum(drn * rp, keepdims=True)
                 + jnp.sum(ez * zkf * vdr, keepdims=True))
            dlg = jnp.where(row == h, t, dlg)
        x2 = jnp.exp(-thh_ref[...] * LN2)
        dth_ref[...] += dlg * (x2 * LN2 / (1.0 - x2))

    def col(blk):
        return pl.BlockSpec((c, w), lambda d, n: (cidx(d, n), blk))

    tab = pl.BlockSpec((c, LANES), lambda d, n: (cidx(d, n), 0))
    dirrow = pl.BlockSpec((None, c, w), lambda d, n: (d, cidx(d, n), 0))
    hrow = pl.BlockSpec((None, RET_H, LANES), lambda d, n: (d, 0, 0))
    return pl.pallas_call(
        body, name="ret_bwd", grid=(2, n_chunks),
        in_specs=[col(0), col(1), col(2), tab, tab, pl.BlockSpec((None, 1, w), lambda d, n: (d, 0, 0)), hrow,
                  pl.BlockSpec((None, None, w, LANES), lambda d, n: (d, cidx(d, n), 0, 0)),
                  pl.BlockSpec((c, w), lambda d, n: (cidx(d, n), 0))],
        out_specs=[dirrow, dirrow, dirrow, hrow],
        out_shape=[jax.ShapeDtypeStruct((2, s, w), F32)] * 3 + [jax.ShapeDtypeStruct((2, RET_H, LANES), F32)],
        scratch_shapes=[pltpu.VMEM((w, LANES), F32)],
        compiler_params=_cp(("arbitrary", "arbitrary")),
    )(p_even, p_even, p_even, cos_r, sin_r, theta_l, theta_h, r_prev, do)


def _post_fwd(o2, gate_row, gain, *, group, n, s, ts, name):
    w = o2.shape[2]

    def fn(of, ob, g, gv):
        y = _gn(of + ob, gv, group, n)[0]
        return g * _sigmoid(g) * y

    return _ew(fn, [_lead(o2, 0, ts), _lead(o2, 1, ts), gate_row], [gain], [(w, BF16)], s=s, ts=ts, name=name)[0]


def _post_bwd(o2, gate_row, gain, dr_row, *, group, n, s, ts, name):
    w = o2.shape[2]

    def fn(of, ob, g, dr, gv):
        y, xn, rstd = _gn(of + ob, gv, group, n)
        sg = _sigmoid(g)
        dy = dr * (g * sg)
        dgate = dr * y * (sg * (1.0 + g * (1.0 - sg)))
        do, dgain = _gn_bwd(dy, xn, rstd, gv, group, n)
        return do, dgate, dgain

    return _ew(fn, [_lead(o2, 0, ts), _lead(o2, 1, ts), gate_row, dr_row], [gain],
               [(w, F32), (w, BF16)], [(1, w)], s=s, ts=ts, name=name)


def _sum2(a2, *, s, ts, name):
    w = a2.shape[2]
    return _ew(lambda a, b: a + b, [_lead(a2, 0, ts), _lead(a2, 1, ts)], [], [(w, BF16)], s=s, ts=ts, name=name)[0]


def _gla_common(d, q_ref, k_ref, ga_ref, wg_ref, bg_ref):
    c = GLA_C
    df = d.astype(F32)
    ii = lax.broadcasted_iota(jnp.int32, (c, c), 0).astype(F32)
    jj = lax.broadcasted_iota(jnp.int32, (c, c), 1).astype(F32)
    rel = (ii - jj) * (1.0 - 2.0 * df)
    tri = _bf(jnp.where(rel >= 0.0, 1.0, 0.0))
    mask = rel >= df
    gab = _bf(ga_ref[...])
    z = _dot(gab, wg_ref[...]) + bg_ref[...]
    la = (jnp.minimum(z, 0.0) - jnp.log1p(jnp.exp(-jnp.abs(z)))) * (1.0 / GLA_TAU)
    l1, l2, l3 = _split3(la)
    b = _dot(tri, l1) + _dot(tri, l2) + _dot(tri, l3)
    first = d == 0
    bm = jnp.where(first, b[c // 2:c // 2 + 1], b[c // 2 - 1:c // 2])
    bl = jnp.where(first, b[c - 1:c], b[0:1])
    q = q_ref[...] * (GLA_DK ** -0.5)
    k = k_ref[...]
    e1, e2, e3, eb = jnp.exp(b - bm), jnp.exp(bm - b), jnp.exp(bl - b), jnp.exp(b)
    return dict(tri=tri, mask=mask, gab=gab, z=z, ebl=jnp.exp(bl), e1=e1, e2=e2, e3=e3, eb=eb,
                qc=q * e1, kc=k * e2, kd=k * e3, qe=q * eb, first=first)


def _col_scale(row_vec, width):
    t = jnp.broadcast_to(row_vec, (LANES, LANES)).T
    return jnp.concatenate([t] * (width // LANES), axis=1)


def _gla_fwd(p_odd, wg2, bg2):
    s = p_odd.shape[0]
    c = GLA_C
    n_chunks = s // c
    wk, wv = GLA_H * GLA_DK, GLA_H * GLA_DV
    cidx = _chunk_index(n_chunks)

    def body(q_ref, k_ref, v_ref, ga_ref, wg_ref, bg_ref, o_ref, sp_ref, s_s):
        d = pl.program_id(0)
        n = pl.program_id(1)

        @pl.when(n == 0)
        def _():
            s_s[...] = jnp.zeros_like(s_s)

        g = _gla_common(d, q_ref, k_ref, ga_ref, wg_ref, bg_ref)
        sp_ref[...] = s_s[...]
        for h in range(GLA_H):
            sl = slice(h * GLA_DK, (h + 1) * GLA_DK)
            vs = slice(h * GLA_DV, (h + 1) * GLA_DV)
            vh = _bf(v_ref[:, vs])
            sh = s_s[sl, :]
            a = jnp.where(g["mask"], _dot_nt(_bf(g["qc"][:, sl]), _bf(g["kc"][:, sl])), 0.0)
            o_ref[:, vs] = _dot(_bf(a), vh) + _dot(_bf(g["qe"][:, sl]), _bf(sh))
            s_s[sl, :] = _col_scale(g["ebl"][:, sl], GLA_DV) * sh + _dot_tn(_bf(g["kd"][:, sl]), vh)

    def col(width, blk):
        return pl.BlockSpec((c, width), lambda d, n: (cidx(d, n), blk))

    return pl.pallas_call(
        body, name="gla_fwd", grid=(2, n_chunks),
        in_specs=[col(wk, 0), col(wk, 1), col(wv, 1), col(LANES, OD_GA_BLK),
                  pl.BlockSpec((None, LANES, wk), lambda d, n: (d, 0, 0)),
                  pl.BlockSpec((None, 1, wk), lambda d, n: (d, 0, 0))],
        out_specs=[pl.BlockSpec((None, c, wv), lambda d, n: (d, cidx(d, n), 0)),
                   pl.BlockSpec((None, None, wk, GLA_DV), lambda d, n: (d, cidx(d, n), 0, 0))],
        out_shape=[jax.ShapeDtypeStruct((2, s, wv), F32), jax.ShapeDtypeStruct((2, n_chunks, wk, GLA_DV), F32)],
        scratch_shapes=[pltpu.VMEM((wk, GLA_DV), F32)],
        compiler_params=_cp(("arbitrary", "arbitrary")),
    )(p_odd, p_odd, p_odd, p_odd, wg2, bg2)


def _gla_bwd(p_odd, wg2, bg2, s_prev, do):
    s = p_odd.shape[0]
    c = GLA_C
    n_chunks = s // c
    wk, wv = GLA_H * GLA_DK, GLA_H * GLA_DV
    fwd_idx = _chunk_index(n_chunks)

    def cidx(d, n):
        return fwd_idx(d, n_chunks - 1 - n)

    def body(q_ref, k_ref, v_ref, ga_ref, wg_ref, bg_ref, sp_ref, do_ref,
             dq_ref, dk_ref, dv_ref, dga_ref, dwg_ref, dbg_ref, ds_s):
        d = pl.program_id(0)
        n = pl.program_id(1)

        @pl.when(n == 0)
        def _():
            ds_s[...] = jnp.zeros_like(ds_s)
            dwg_ref[...] = jnp.zeros_like(dwg_ref)
            dbg_ref[...] = jnp.zeros_like(dbg_ref)

        g = _gla_common(d, q_ref, k_ref, ga_ref, wg_ref, bg_ref)
        mask = g["mask"]
        ones8 = jnp.ones((8, GLA_DV), BF16)
        dbs, dbms, dbls = [], [], []
        for h in range(GLA_H):
            sl = slice(h * GLA_DK, (h + 1) * GLA_DK)
            vs = slice(h * GLA_DV, (h + 1) * GLA_DV)
            qc, kc, kd, qe = g["qc"][:, sl], g["kc"][:, sl], g["kd"][:, sl], g["qe"][:, sl]
            qcb, kcb, kdb, qeb = _bf(qc), _bf(kc), _bf(kd), _bf(qe)
            vh = _bf(v_ref[:, vs])
            doh = _bf(do_ref[:, vs])
            sp = sp_ref[sl, :]
            dsn = ds_s[sl, :]
            dsb = _bf(dsn)
            a = _bf(jnp.where(mask, _dot_nt(qcb, kcb), 0.0))
            da = _bf(jnp.where(mask, _dot_nt(doh, vh), 0.0))
            dv_ref[:, vs] = _dot_tn(a, doh) + _dot(kdb, dsb)
            dqc = _dot(da, kcb)
            dkc = _dot_tn(da, qcb)
            dqe = _dot_nt(doh, _bf(sp))
            dkd = _dot_nt(vh, dsb)
            ds_s[sl, :] = _dot_tn(qeb, doh) + _col_scale(g["ebl"][:, sl], GLA_DV) * dsn
            dq_ref[:, sl] = (dqc * g["e1"][:, sl] + dqe * g["eb"][:, sl]) * (GLA_DK ** -0.5)
            dk_ref[:, sl] = dkc * g["e2"][:, sl] + dkd * g["e3"][:, sl]
            t1, t2, t3, t4 = dqc * qc, dkc * kc, dqe * qe, dkd * kd
            dbs.append(t1 - t2 + t3 - t4)
            dbms.append(_rowsum(t2 - t1))
            m1, m2, _ = _split3(dsn * sp)
            rs = (_dot_nt(ones8, m1) + _dot_nt(ones8, m2))[0:1]
            dbls.append(_rowsum(t4) + g["ebl"][:, sl] * rs)
        db = jnp.concatenate(dbs, axis=1)
        dbm = jnp.concatenate(dbms, axis=1)
        dbl = jnp.concatenate(dbls, axis=1)
        row = lax.broadcasted_iota(jnp.int32, (c, wk), 0)
        mid = jnp.where(g["first"], c // 2, c // 2 - 1)
        last = jnp.where(g["first"], c - 1, 0)
        db = db + jnp.where(row == mid, dbm, 0.0) + jnp.where(row == last, dbl, 0.0)
        d1, d2, d3 = _split3(db)
        tri = g["tri"]
        dla = _dot_tn(tri, d1) + _dot_tn(tri, d2) + _dot_tn(tri, d3)
        dz = dla * (1.0 / GLA_TAU) * (1.0 - _sigmoid(g["z"]))
        dzb = _bf(dz)
        dga_ref[...] = _dot_nt(dzb, wg_ref[...])
        dwg_ref[...] += _dot_tn(g["gab"], dzb)
        dbg_ref[...] += _rowsum(dz)

    def col(width, blk):
        return pl.BlockSpec((c, width), lambda d, n: (cidx(d, n), blk))

    def dirrow(width):
        return pl.BlockSpec((None, c, width), lambda d, n: (d, cidx(d, n), 0))

    return pl.pallas_call(
        body, name="gla_bwd", grid=(2, n_chunks),
        in_specs=[col(wk, 0), col(wk, 1), col(wv, 1), col(LANES, OD_GA_BLK),
                  pl.BlockSpec((None, LANES, wk), lambda d, n: (d, 0, 0)),
                  pl.BlockSpec((None, 1, wk), lambda d, n: (d, 0, 0)),
                  pl.BlockSpec((None, None, wk, GLA_DV), lambda d, n: (d, cidx(d, n), 0, 0)),
                  pl.BlockSpec((c, wv), lambda d, n: (cidx(d, n), 0))],
        out_specs=[dirrow(wk), dirrow(wk), dirrow(wv), dirrow(LANES),
                   pl.BlockSpec((None, LANES, wk), lambda d, n: (d, 0, 0)),
                   pl.BlockSpec((None, 1, wk), lambda d, n: (d, 0, 0))],
        out_shape=[jax.ShapeDtypeStruct((2, s, wk), F32), jax.ShapeDtypeStruct((2, s, wk), F32),
                   jax.ShapeDtypeStruct((2, s, wv), F32), jax.ShapeDtypeStruct((2, s, LANES), F32),
                   jax.ShapeDtypeStruct((2, LANES, wk), F32), jax.ShapeDtypeStruct((2, 1, wk), F32)],
        scratch_shapes=[pltpu.VMEM((wk, GLA_DV), F32)],
        compiler_params=_cp(("arbitrary", "arbitrary")),
    )(p_odd, p_odd, p_odd, p_odd, wg2, bg2, s_prev, do)


HALO = 8


def _halo_specs(width_blk, col0, ts, s):
    r = ts // HALO
    last = s // HALO - 1
    cur = pl.BlockSpec((ts, width_blk), lambda j, i: (i, col0 + j))
    prev = pl.BlockSpec((HALO, width_blk), lambda j, i: (jnp.maximum(i * r - 1, 0), col0 + j))
    nxt = pl.BlockSpec((HALO, width_blk), lambda j, i: (jnp.minimum((i + 1) * r, last), col0 + j))
    return [prev, cur, nxt]


def _with_halo(prev_ref, cur_ref, next_ref, i, n_i):
    p = jnp.where(i == 0, 0.0, prev_ref[...])
    q = jnp.where(i == n_i - 1, 0.0, next_ref[...])
    return jnp.concatenate([p, cur_ref[...], q], axis=0)


def _shift_down(x):
    return pltpu.roll(x, 1, 0)


def _shift_up(x):
    return pltpu.roll(x, x.shape[0] - 1, 0)


def _ffn_act(up, conv_w, conv_b, *, ts):
    s = up.shape[0]
    tc = _tile(D_FF, 1408)
    nj = D_FF // tc
    n_i = s // ts

    def body(gp, gc, gn, val_ref, w_ref, b_ref, a_ref):
        i = pl.program_id(1)
        g = _with_halo(gp, gc, gn, i, n_i)
        w = w_ref[...]
        conv = w[0:1] * _shift_down(g) + w[1:2] * g + w[2:3] * _shift_up(g) + b_ref[...]
        conv = conv[HALO:HALO + ts]
        a_ref[...] = (conv * _sigmoid(conv) * val_ref[...]).astype(a_ref.dtype)

    return pl.pallas_call(
        body, name="ffn_act", grid=(nj, n_i),
        in_specs=_halo_specs(tc, 0, ts, s) + [pl.BlockSpec((ts, tc), lambda j, i: (i, nj + j)),
                                              pl.BlockSpec((3, tc), lambda j, i: (0, j)),
                                              pl.BlockSpec((1, tc), lambda j, i: (0, j))],
        out_specs=pl.BlockSpec((ts, tc), lambda j, i: (i, j)),
        out_shape=jax.ShapeDtypeStruct((s, D_FF), BF16),
        compiler_params=_cp(("parallel", "arbitrary")),
    )(up, up, up, up, conv_w, conv_b)


def _ffn_act_bwd(up, da, conv_w, conv_b, *, ts):
    s = up.shape[0]
    tc = _tile(D_FF, 1408)
    nj = D_FF // tc
    n_i = s // ts

    def body(gp, gc, gn, vp, vc, vn, dp, dc, dn, w_ref, b_ref, dg_ref, dval_ref, dw_ref, db_ref):
        i = pl.program_id(1)
        g = _with_halo(gp, gc, gn, i, n_i)
        v = _with_halo(vp, vc, vn, i, n_i)
        dav = _with_halo(dp, dc, dn, i, n_i)
        w = w_ref[...]
        gm, gpl = _shift_down(g), _shift_up(g)
        conv = w[0:1] * gm + w[1:2] * g + w[2:3] * gpl + b_ref[...]
        sg = _sigmoid(conv)
        dgc = dav * v * (sg * (1.0 + conv * (1.0 - sg)))
        dgate = w[0:1] * _shift_up(dgc) + w[1:2] * dgc + w[2:3] * _shift_down(dgc)
        ctr = slice(HALO, HALO + ts)
        dg_ref[...] = dgate[ctr].astype(dg_ref.dtype)
        dval_ref[...] = (dav[ctr] * (conv * sg)[ctr]).astype(dval_ref.dtype)
        dgc_c = dgc[ctr]
        dw = jnp.concatenate([_rowsum(dgc_c * gm[ctr]), _rowsum(dgc_c * g[ctr]), _rowsum(dgc_c * gpl[ctr])], axis=0)
        dbv = _rowsum(dgc_c)

        @pl.when(i == 0)
        def _():
            dw_ref[...] = dw
            db_ref[...] = dbv

        @pl.when(i > 0)
        def _():
            dw_ref[...] += dw
            db_ref[...] += dbv

    tile = pl.BlockSpec((ts, tc), lambda j, i: (i, j))
    return pl.pallas_call(
        body, name="ffn_act_bwd", grid=(nj, n_i),
        in_specs=(_halo_specs(tc, 0, ts, s) + _halo_specs(tc, nj, ts, s) + _halo_specs(tc, 0, ts, s)
                  + [pl.BlockSpec((3, tc), lambda j, i: (0, j)), pl.BlockSpec((1, tc), lambda j, i: (0, j))]),
        out_specs=[tile, tile, pl.BlockSpec((3, tc), lambda j, i: (0, j)), pl.BlockSpec((1, tc), lambda j, i: (0, j))],
        out_shape=[jax.ShapeDtypeStruct((s, D_FF), BF16), jax.ShapeDtypeStruct((s, D_FF), BF16),
                   jax.ShapeDtypeStruct((3, D_FF), F32), jax.ShapeDtypeStruct((1, D_FF), F32)],
        compiler_params=_cp(("parallel", "arbitrary")),
    )(up, up, up, up, up, up, da, da, da, conv_w, conv_b)


def _loss_head(y, target, *, s, ts):
    def fn(yv, tv):
        err = yv - tv
        return err * (1.0 / D_MODEL), _rowsum(err * err)

    return _ew(fn, [_cols(y, D_MODEL, 0, ts), _cols(target, D_MODEL, 0, ts)], [], [(D_MODEL, F32)],
               [(1, D_MODEL)], s=s, ts=ts, name="loss_head")


def _rows_tile(r, width):
    ts = r
    while ts * width * 4 > (1 << 20) and ts % 16 == 0:
        ts //= 2
    return ts


def _adamw(w, g, m, v, *, ts, name):
    r, width = w.shape
    assert r % ts == 0

    def fn(wv, gv, mv, vv):
        mn = ADAM_B1 * mv + (1.0 - ADAM_B1) * gv
        vn = ADAM_B2 * vv + (1.0 - ADAM_B2) * (gv * gv)
        m_hat = mn / (1.0 - ADAM_B1 ** ADAM_STEP)
        v_hat = vn / (1.0 - ADAM_B2 ** ADAM_STEP)
        delta = -ADAM_LR * (m_hat / (jnp.sqrt(v_hat) + ADAM_EPS) + ADAM_WD * wv)
        return delta, mn, vn

    rows = [_cols(a, width, 0, ts) for a in (w, g, m, v)]
    return _ew(fn, rows, [], [(width, F32)] * 3, s=r, ts=ts, name=name)


def _pad_heads(w, heads, real):
    lead = w.shape[:-1]
    w = w.reshape(lead + (heads, real))
    w = jnp.pad(w, [(0, 0)] * len(lead) + [(0, 0), (0, LANES - real)])
    return w.reshape(lead + (heads * LANES,))


def _pad_head_rows(w, heads, real):
    return _pad_heads(w.T, heads, real).T


def _pack_even(p):
    w_in = p["w_in"]
    z = lambda n: jnp.zeros((D_MODEL, n), w_in.dtype)
    o = 0
    parts = {}
    for nm, n in (("cq", MLA_QR), ("ckv", MLA_KVR), ("kr", MLA_ROPE), ("rq", 512), ("rk", 512), ("rv", 512), ("rg", 512)):
        parts[nm] = w_in[:, o:o + n]
        o += n
    w_in_p = jnp.concatenate(
        [_pad_heads(parts[k], RET_H, RET_DK) for k in ("rq", "rk", "rv", "rg")]
        + [parts["cq"], z(EV_CQ - MLA_QR), parts["ckv"], z(MLA_NOPE), parts["kr"], z(LANES - MLA_QK), z(LANES)], axis=1)
    w_uq = jnp.pad(_pad_heads(p["w_uq"], MLA_H, MLA_QK), ((0, EV_CQ - MLA_QR), (0, 0)))
    ukv = p["w_ukv"].reshape(MLA_KVR, MLA_H, MLA_NOPE + MLA_V)
    w_ukv = jnp.concatenate([_pad_heads(ukv[..., :MLA_NOPE].reshape(MLA_KVR, -1), MLA_H, MLA_NOPE),
                             _pad_heads(ukv[..., MLA_NOPE:].reshape(MLA_KVR, -1), MLA_H, MLA_V)], axis=1)
    w_out = jnp.concatenate([_pad_head_rows(p["w_out"][:MLA_H * MLA_V], MLA_H, MLA_V),
                             _pad_head_rows(p["w_out"][MLA_H * MLA_V:], RET_H, RET_DV)], axis=0)
    return dict(
        w_in=w_in_p, w_uq=w_uq, w_ukv=w_ukv, w_out=w_out,
        mix_g=p["mix_norm"][None, :],
        q_norm=jnp.pad(p["q_norm"], (0, EV_CQ - MLA_QR))[None, :],
        kv_norm=p["kv_norm"][None, :],
        qhn=jnp.pad(p["q_head_norm"], (0, LANES - MLA_QK))[None, :],
        khn=jnp.pad(p["k_head_norm"], (0, LANES - MLA_QK))[None, :],
        ret_gain=_pad_heads(p["ret_out_norm"].reshape(-1), RET_H, RET_DV)[None, :],
    )


def _pack_odd(p):
    w_in = p["w_in"]
    ga = w_in[:, 3072:]
    w_in_p = jnp.concatenate([w_in[:, :3072], ga, jnp.zeros((D_MODEL, LANES - 2 * GLA_R), w_in.dtype)], axis=1)
    wk = GLA_H * GLA_DK
    zf = jnp.zeros((LANES - GLA_R, wk), p["w_gate_fwd"].dtype)
    zb0 = jnp.zeros((GLA_R, wk), p["w_gate_fwd"].dtype)
    zb1 = jnp.zeros((LANES - 2 * GLA_R, wk), p["w_gate_fwd"].dtype)
    wg2 = jnp.stack([jnp.concatenate([p["w_gate_fwd"], zf], axis=0),
                     jnp.concatenate([zb0, p["w_gate_bwd"], zb1], axis=0)])
    bg2 = jnp.stack([p["b_gate_fwd"][None, :], p["b_gate_bwd"][None, :]])
    return dict(w_in=w_in_p, wg2=wg2, bg2=bg2, w_out=p["w_out"], mix_g=p["mix_norm"][None, :],
                gla_gain=p["gla_out_norm"].reshape(1, -1))


_MATRICES = ("w_in", "w_uq", "w_ukv", "w_out", "wg2")


def _packed(pack_fn, p):
    packed = pack_fn(p)
    packed = {k: (_bf(v) if k in _MATRICES else v.astype(F32)) for k, v in packed.items()}
    shapes = {k: jax.ShapeDtypeStruct(v.shape, F32) for k, v in p.items()}
    unpack = jax.linear_transpose(pack_fn, shapes)
    return packed, lambda g: unpack(g)[0]


def _ffn_fwd(x, w, *, s, ts):
    h = _rmsnorm(_cols(x, D_MODEL, 0, ts), w["norm_g"], n=D_MODEL, s=s, ts=ts, name="ffn_norm")
    up = _mm(h, w["w_up4"], b_layer=w["layer"], name="ffn_up")
    a = _ffn_act(up, w["conv_w"], w["conv_b"], ts=ts)
    y = _mm(a, w["w_down"], res=x, name="ffn_down")
    return y, dict(x=x, h=h, up=up, a=a)


def _ffn_bwd(dy, w, sv, *, s, ts):
    da = _mm(dy, w["w_down"], tb=True, name="ffn_down_dx")
    g_down = _mm(sv["a"], dy, ta=True, name="ffn_down_dw")
    dgate, dval, g_cw, g_cb = _ffn_act_bwd(sv["up"], da, w["conv_w"], w["conv_b"], ts=ts)
    dup = jnp.concatenate([dgate, dval], axis=1)
    dh = _mm(dup, w["w_up4"], tb=True, b_layer=w["layer"], name="ffn_up_dx")
    g_up = _mm(sv["h"], dup, ta=True, out_chips=True, name="ffn_up_dw")
    dx, g_norm = _rmsnorm_bwd(_cols(sv["x"], D_MODEL, 0, ts), w["norm_g"], dh, dy, n=D_MODEL, s=s, ts=ts,
                              name="ffn_norm_bwd")
    return dx, dict(w_up=g_up, w_down=g_down, conv_w=g_cw, conv_b=g_cb, norm_g=g_norm)


def _flash_tiles(s):
    return min(s, 512), min(s, 1024)


def _even_fwd(x, w, tabs, *, s, ts):
    cos_m, sin_m, cos_r, sin_r = tabs
    h = _rmsnorm(_cols(x, D_MODEL, 0, ts), w["mix_g"], n=D_MODEL, s=s, ts=ts, name="mix_norm")
    p = _mm(h, w["w_in"], name="even_in")
    cqn = _rmsnorm(_cols(p, EV_CQ, EV_RET // EV_CQ, ts), w["q_norm"], n=MLA_QR, s=s, ts=ts, name="mla_q_norm")
    ckvn = _rmsnorm(_cols(p, MLA_KVR, (EV_RET + EV_CQ) // MLA_KVR, ts), w["kv_norm"], n=MLA_KVR, s=s, ts=ts,
                    name="mla_kv_norm")
    q_pre = _mm(cqn, w["w_uq"], name="mla_uq")
    kv_pre = _mm(ckvn, w["w_ukv"], name="mla_ukv")
    q, k, v = _mla_prep(q_pre, kv_pre, p, cos_m, sin_m, w["qhn"], w["khn"], s=s, ts=ts)
    tq, tk = _flash_tiles(s)
    o, lse = _flash_fwd(q, k, v, tq=tq, tk=tk)
    o2, r_prev = _ret_fwd(p, cos_r, sin_r, w["theta_l"])
    r = _post_fwd(o2, _cols(p, RET_H * LANES, 3, ts), w["ret_gain"], group=LANES, n=RET_DV, s=s, ts=ts,
                  name="ret_post")
    ar = jnp.concatenate([o, r], axis=1)
    y = _mm(ar, w["w_out"], res=x, name="even_out")
    return y, dict(x=x, h=h, p=p, cqn=cqn, ckvn=ckvn, q_pre=q_pre, kv_pre=kv_pre, q=q, k=k, v=v, o=o, lse=lse,
                   o2=o2, r_prev=r_prev, ar=ar)


def _even_bwd(dy, w, sv, tabs, *, s, ts):
    cos_m, sin_m, cos_r, sin_r = tabs
    p = sv["p"]
    wh = MLA_H * LANES
    dar = _mm(dy, w["w_out"], tb=True, name="even_out_dx")
    g_out = _mm(sv["ar"], dy, ta=True, name="even_out_dw")
    tq, tk = _flash_tiles(s)
    do_attn = _attn_bwd_prep(dar, sv["o"], s=s, ts=ts)
    dq, dk, dv = _flash_bwd(sv["q"], sv["k"], sv["v"], do_attn, sv["lse"], tq=tq, tk=tk)
    dq_pre, dk_pre, dkr, g_qhn, g_khn = _mla_prep_bwd(sv["q_pre"], sv["kv_pre"], p, cos_m, sin_m, w["qhn"], w["khn"],
                                                      dq, dk, s=s, ts=ts)
    dkv_pre = jnp.concatenate([dk_pre, dv], axis=1)
    dckvn = _mm(dkv_pre, w["w_ukv"], tb=True, name="mla_ukv_dx")
    g_ukv = _mm(sv["ckvn"], dkv_pre, ta=True, name="mla_ukv_dw")
    dcqn = _mm(dq_pre, w["w_uq"], tb=True, name="mla_uq_dx")
    g_uq = _mm(sv["cqn"], dq_pre, ta=True, name="mla_uq_dw")
    dckv, g_kvn = _rmsnorm_bwd(_cols(p, MLA_KVR, (EV_RET + EV_CQ) // MLA_KVR, ts), w["kv_norm"], dckvn, None,
                               n=MLA_KVR, s=s, ts=ts, name="mla_kv_norm_bwd")
    dcq, g_qn = _rmsnorm_bwd(_cols(p, EV_CQ, EV_RET // EV_CQ, ts), w["q_norm"], dcqn, None, n=MLA_QR, s=s, ts=ts,
                             name="mla_q_norm_bwd")
    do, drg, g_gain = _post_bwd(sv["o2"], _cols(p, wh, 3, ts), w["ret_gain"], _cols(dar, wh, 1, ts),
                                group=LANES, n=RET_DV, s=s, ts=ts, name="ret_post_bwd")
    dq2, dk2, dv2, dth = _ret_bwd(p, cos_r, sin_r, w["theta_l"], w["theta_h"], sv["r_prev"], do)
    drq, drk, drv = (_sum2(a, s=s, ts=ts, name="sum_dirs_1024") for a in (dq2, dk2, dv2))
    dp = jnp.concatenate([drq, drk, drv, drg, _bf(dcq), _bf(dckv), dkr, jnp.zeros((s, LANES), BF16)], axis=1)
    dh = _mm(dp, w["w_in"], tb=True, name="even_in_dx")
    g_in = _mm(sv["h"], dp, ta=True, name="even_in_dw")
    dx, g_mix = _rmsnorm_bwd(_cols(sv["x"], D_MODEL, 0, ts), w["mix_g"], dh, dy, n=D_MODEL, s=s, ts=ts,
                             name="mix_norm_bwd")
    grads = dict(w_in=g_in, w_uq=g_uq, w_ukv=g_ukv, w_out=g_out, mix_g=g_mix, q_norm=g_qn, kv_norm=g_kvn,
                 qhn=g_qhn, khn=g_khn, ret_gain=g_gain)
    return dx, grads, dth[:, :, 0]


def _odd_fwd(x, w, *, s, ts):
    h = _rmsnorm(_cols(x, D_MODEL, 0, ts), w["mix_g"], n=D_MODEL, s=s, ts=ts, name="mix_norm")
    p = _mm(h, w["w_in"], name="odd_in")
    o2, s_prev = _gla_fwd(p, w["wg2"], w["bg2"])
    g = _post_fwd(o2, _cols(p, GLA_H * GLA_DV, 2, ts), w["gla_gain"], group=GLA_DV, n=GLA_DV, s=s, ts=ts,
                  name="gla_post")
    y = _mm(g, w["w_out"], res=x, name="odd_out")
    return y, dict(x=x, h=h, p=p, o2=o2, s_prev=s_prev, g=g)


def _odd_bwd(dy, w, sv, *, s, ts):
    p = sv["p"]
    wv = GLA_H * GLA_DV
    dg = _mm(dy, w["w_out"], tb=True, name="odd_out_dx")
    g_out = _mm(sv["g"], dy, ta=True, name="odd_out_dw")
    do, dgr, g_gain = _post_bwd(sv["o2"], _cols(p, wv, 2, ts), w["gla_gain"], _cols(dg, wv, 0, ts),
                                group=GLA_DV, n=GLA_DV, s=s, ts=ts, name="gla_post_bwd")
    dq2, dk2, dv2, dga2, g_wg, g_bg = _gla_bwd(p, w["wg2"], w["bg2"], sv["s_prev"], do)
    dq = _sum2(dq2, s=s, ts=ts, name="sum_dirs_512")
    dk = _sum2(dk2, s=s, ts=ts, name="sum_dirs_512")
    dv = _sum2(dv2, s=s, ts=ts, name="sum_dirs_1024")
    dga = _sum2(dga2, s=s, ts=ts, name="sum_dirs_128")
    dp = jnp.concatenate([dq, dk, dv, dgr, dga], axis=1)
    dh = _mm(dp, w["w_in"], tb=True, name="odd_in_dx")
    g_in = _mm(sv["h"], dp, ta=True, name="odd_in_dw")
    dx, g_mix = _rmsnorm_bwd(_cols(sv["x"], D_MODEL, 0, ts), w["mix_g"], dh, dy, n=D_MODEL, s=s, ts=ts,
                             name="mix_norm_bwd")
    return dx, dict(w_in=g_in, wg2=g_wg, bg2=g_bg, w_out=g_out, mix_g=g_mix, gla_gain=g_gain)


_EVEN_NAMES = dict(mix_norm="mix_norm_even", w_in="w_in_even", q_norm="mla_q_norm", kv_norm="mla_kv_norm",
                   w_uq="mla_w_uq", w_ukv="mla_w_ukv", q_head_norm="mla_q_head_norm", k_head_norm="mla_k_head_norm",
                   ret_out_norm="ret_out_norm", w_out="w_out_even")
_ODD_NAMES = dict(mix_norm="mix_norm_odd", w_in="w_in_odd", w_gate_fwd="gla_w_gate_fwd", b_gate_fwd="gla_b_gate_fwd",
                  w_gate_bwd="gla_w_gate_bwd", b_gate_bwd="gla_b_gate_bwd", gla_out_norm="gla_out_norm",
                  w_out="w_out_odd")

def _local_step(x, pos, target, full):
    s = x.shape[0]
    ts = min(s, 256)
    tabs = _rope_tables(pos, MLA_ROPE, MLA_NOPE) + _rope_tables(pos, RET_DK, 0)
    layers = []
    for layer in range(DEPTH):
        i = layer // 2
        names = _EVEN_NAMES if layer % 2 == 0 else _ODD_NAMES
        wm, unpack_m = _packed(_pack_even if layer % 2 == 0 else _pack_odd, {k: full[n][i] for k, n in names.items()})
        if layer % 2 == 0:
            th = jnp.stack([full["ret_theta_fwd"][i], full["ret_theta_bwd"][i]]).astype(F32)
            wm["theta_h"] = jnp.broadcast_to(th[:, :, None], (2, RET_H, LANES))
            wm["theta_l"] = wm["theta_h"].reshape(2, 1, RET_H * LANES)
        wf = dict(layer=layer, w_up4=full["ffn_w_up"], w_down=_bf(full["ffn_w_down"][layer]),
                  conv_w=full["ffn_conv_w"][layer].astype(F32), conv_b=full["ffn_conv_b"][layer][None, :].astype(F32),
                  norm_g=full["ffn_norm"][layer][None, :].astype(F32))
        layers.append((wm, unpack_m, wf))

    saved = []
    for layer, (wm, _, wf) in enumerate(layers):
        if layer % 2 == 0:
            x, sv_m = _even_fwd(x, wm, tabs, s=s, ts=ts)
        else:
            x, sv_m = _odd_fwd(x, wm, s=s, ts=ts)
        x, sv_f = _ffn_fwd(x, wf, s=s, ts=ts)
        saved.append((sv_m, sv_f))

    dy, sq = _loss_head(x, target, s=s, ts=ts)
    loss = 0.5 / D_MODEL * jnp.sum(sq)

    grads = {}

    def put(name, idx, g):
        grads.setdefault(name, {})[idx] = g

    for layer in reversed(range(DEPTH)):
        wm, unpack_m, wf = layers[layer]
        sv_m, sv_f = saved[layer]
        i = layer // 2
        dy, gf = _ffn_bwd(dy, wf, sv_f, s=s, ts=ts)
        put("ffn_w_up", layer, gf["w_up"])
        put("ffn_w_down", layer, gf["w_down"])
        put("ffn_conv_w", layer, gf["conv_w"])
        put("ffn_conv_b", layer, gf["conv_b"][0])
        put("ffn_norm", layer, gf["norm_g"][0])
        if layer % 2 == 0:
            dy, gm, dth = _even_bwd(dy, wm, sv_m, tabs, s=s, ts=ts)
            put("ret_theta_fwd", i, dth[0])
            put("ret_theta_bwd", i, dth[1])
            names = _EVEN_NAMES
        else:
            dy, gm = _odd_bwd(dy, wm, sv_m, s=s, ts=ts)
            names = _ODD_NAMES
        for k, g in unpack_m(gm).items():
            put(names[k], i, g)
    return loss, dy, {n: [g[j] for j in range(len(g))] for n, g in grads.items()}


HBM_SPEC = pl.BlockSpec(memory_space=pltpu.HBM)
VMEM_SPEC = pl.BlockSpec(memory_space=pltpu.VMEM)
CHIPS = 4
CORES = 2
ROW = 8 * LANES


def _xyc():
    return lax.axis_index("x"), lax.axis_index("y"), lax.axis_index("c")


def _other_chips(x, y):
    return [(1 - x, y), (x, 1 - y), (1 - x, 1 - y)]


def _remote(src, dst, send, recv, dev):
    return pltpu.make_async_remote_copy(src_ref=src, dst_ref=dst, send_sem=send, recv_sem=recv,
                                        device_id=dev, device_id_type=MESH)


def _sems(n):
    return pltpu.SemaphoreType.DMA((n,))


def _gather_chips(arrs):
    n = len(arrs)

    def body(*refs):
        ins, outs = refs[:n], refs[n:2 * n]
        send, recv, loc = refs[2 * n:]
        x, y, c = _xyc()
        me = 2 * x + y
        local = [pltpu.make_async_copy(ins[t], outs[t].at[me], loc.at[t]) for t in range(n)]
        for cp in local:
            cp.start()
        sends = []
        for j, (px, py) in enumerate(_other_chips(x, y)):
            for t in range(n):
                cp = _remote(ins[t], outs[t].at[me], send.at[n * j + t], recv.at[n * j + t], (px, py, c))
                cp.start()
                sends.append(cp)
        for j, (px, py) in enumerate(_other_chips(x, y)):
            for t in range(n):
                _remote(ins[t], outs[t].at[2 * px + py], send.at[n * j + t], recv.at[n * j + t], (px, py, c)).wait_recv()
        for cp in sends:
            cp.wait_send()
        for cp in local:
            cp.wait()

    return pl.pallas_call(
        body, name="gather_chips", in_specs=[HBM_SPEC] * n, out_specs=[HBM_SPEC] * n,
        out_shape=[jax.ShapeDtypeStruct((CHIPS,) + a.shape, a.dtype) for a in arrs],
        scratch_shapes=[_sems(3 * n), _sems(3 * n), _sems(n)],
    )(*arrs)


def _half_rows(ref, axis, half, which):
    idx = (slice(None),) * axis + (pl.ds(pl.multiple_of(which * half, 8), half),)
    return ref.at[idx]


def _swap_halves(arrs):
    n = len(arrs)

    def body(*refs):
        ins, outs = refs[:n], refs[n:2 * n]
        send, recv = refs[2 * n:]
        x, y, c = _xyc()
        copies = []
        for t in range(n):
            half = arrs[t].shape[2] // CORES
            cp = _remote(_half_rows(ins[t], 2, half, 1 - c), outs[t], send.at[t], recv.at[t], (x, y, 1 - c))
            cp.start()
            copies.append(cp)
        for cp in copies:
            cp.wait()

    return pl.pallas_call(
        body, name="swap_halves", in_specs=[HBM_SPEC] * n, out_specs=[HBM_SPEC] * n,
        out_shape=[jax.ShapeDtypeStruct(a.shape[:2] + (a.shape[2] // CORES, a.shape[3]), a.dtype) for a in arrs],
        scratch_shapes=[_sems(n), _sems(n)],
    )(*arrs)


def _add_core_halves(a, got, core, *, ts, name):
    ch, nl, r, cols = a.shape
    half = r // CORES
    nb = half // ts

    def body(core_ref, a_ref, g_ref, o_ref):
        o_ref[...] = a_ref[...] + g_ref[...]

    rows = pl.BlockSpec((ts, cols), lambda g, i, cr: (g * nb + i, 0))
    return pl.pallas_call(
        body, name=name, out_shape=jax.ShapeDtypeStruct((ch * nl * half, cols), F32),
        grid_spec=pltpu.PrefetchScalarGridSpec(
            num_scalar_prefetch=1, grid=(ch * nl, nb),
            in_specs=[pl.BlockSpec((ts, cols), lambda g, i, cr: (g * (r // ts) + cr[0] * nb + i, 0)), rows],
            out_specs=rows),
        compiler_params=_cp(("arbitrary", "arbitrary")),
    )(core, a.reshape(-1, cols), got.reshape(-1, cols)).reshape(got.shape)


def _add_chip_parts(parts, core, *, ts, name):
    ch, nl, half, cols = parts.shape
    nb = half // ts
    r = half * CORES

    def body(core_ref, *refs):
        acc = refs[0][...]
        for p in refs[1:ch]:
            acc = acc + p[...]
        refs[ch][...] = acc

    return pl.pallas_call(
        body, name=name, out_shape=jax.ShapeDtypeStruct((nl * r, cols), F32),
        grid_spec=pltpu.PrefetchScalarGridSpec(
            num_scalar_prefetch=1, grid=(nl, nb),
            in_specs=[pl.BlockSpec((ts, cols), lambda l, i, cr, j=j: ((j * nl + l) * nb + i, 0)) for j in range(ch)],
            out_specs=pl.BlockSpec((ts, cols), lambda l, i, cr: (l * (r // ts) + cr[0] * nb + i, 0))),
        compiler_params=_cp(("arbitrary", "arbitrary")),
    )(core, *[parts.reshape(-1, cols)] * ch).reshape(nl, r, cols)


def _scatter_chips(arrs):
    n = len(arrs)

    def body(*refs):
        ins, outs = refs[:n], refs[n:2 * n]
        send, recv, loc = refs[2 * n:]
        x, y, c = _xyc()
        me = 2 * x + y
        copies = []
        for t in range(n):
            cp = pltpu.make_async_copy(ins[t].at[me], outs[t].at[me], loc.at[t])
            cp.start()
            copies.append(cp)
        sends = []
        for j, (px, py) in enumerate(_other_chips(x, y)):
            for t in range(n):
                cp = _remote(ins[t].at[2 * px + py], outs[t].at[me], send.at[n * j + t], recv.at[n * j + t], (px, py, c))
                cp.start()
                sends.append(cp)
        for j, (px, py) in enumerate(_other_chips(x, y)):
            for t in range(n):
                _remote(ins[t].at[me], outs[t].at[2 * px + py], send.at[n * j + t], recv.at[n * j + t],
                        (px, py, c)).wait_recv()
        for cp in sends:
            cp.wait_send()
        for cp in copies:
            cp.wait()

    return pl.pallas_call(
        body, name="scatter_chips", in_specs=[HBM_SPEC] * n, out_specs=[HBM_SPEC] * n,
        out_shape=[jax.ShapeDtypeStruct(a.shape, a.dtype) for a in arrs],
        scratch_shapes=[_sems(3 * n), _sems(3 * n), _sems(n)],
    )(*arrs)


def _gather_cores(arrs):
    n = len(arrs)

    def body(*refs):
        ins, outs = refs[:n], refs[n:2 * n]
        send, recv = refs[2 * n:]
        x, y, c = _xyc()
        sends = []
        for t in range(n):
            half = arrs[t].shape[1] // CORES
            cp = _remote(_half_rows(ins[t], 1, half, c), _half_rows(outs[t], 1, half, c), send.at[t], recv.at[t],
                         (x, y, 1 - c))
            cp.start()
            sends.append(cp)
        for t in range(n):
            half = arrs[t].shape[1] // CORES
            _remote(_half_rows(ins[t], 1, half, 1 - c), _half_rows(outs[t], 1, half, 1 - c), send.at[t], recv.at[t],
                    (x, y, 1 - c)).wait_recv()
        for cp in sends:
            cp.wait_send()

    return pl.pallas_call(
        body, name="gather_cores", in_specs=[HBM_SPEC] * n, out_specs=[HBM_SPEC] * n,
        out_shape=[jax.ShapeDtypeStruct(a.shape, a.dtype) for a in arrs],
        input_output_aliases={t: t for t in range(n)},
        scratch_shapes=[_sems(n), _sems(n)],
    )(*arrs)


def _all_reduce_devices(v):
    n_dev = CHIPS * CORES

    def body(v_ref, o_ref, buf, send, recv):
        x, y, c = _xyc()
        me = 4 * x + 2 * y + c
        buf[pl.ds(me, 1)] = v_ref[...][None]
        sends = []
        for m in range(1, n_dev):
            px = 1 - x if m & 4 else x
            py = 1 - y if m & 2 else y
            pc = 1 - c if m & 1 else c
            cp = _remote(v_ref, buf.at[me], send.at[m - 1], recv.at[m - 1], (px, py, pc))
            cp.start()
            sends.append((cp, 4 * px + 2 * py + pc))
        for m, (cp, peer) in enumerate(sends):
            _remote(v_ref, buf.at[peer], send.at[m], recv.at[m], (x, y, c)).wait_recv()
        for cp, _ in sends:
            cp.wait_send()
        acc = buf[0]
        for k in range(1, n_dev):
            acc = acc + buf[k]
        o_ref[...] = acc

    return pl.pallas_call(
        body, name="all_reduce_devices", in_specs=[VMEM_SPEC], out_specs=VMEM_SPEC,
        out_shape=jax.ShapeDtypeStruct(v.shape, F32),
        scratch_shapes=[pltpu.VMEM((n_dev,) + v.shape, F32), pltpu.SemaphoreType.DMA((n_dev - 1,)),
                        pltpu.SemaphoreType.DMA((n_dev - 1,))],
    )(v)


_SHARDED = (("w_in_even", 2), ("mla_w_uq", 2), ("mla_w_ukv", 2), ("w_out_even", 1), ("w_in_odd", 2), ("w_out_odd", 1),
            ("ffn_w_up", 2), ("ffn_w_down", 1),
            ("mix_norm_odd", 1), ("gla_w_gate_fwd", 2), ("gla_b_gate_fwd", 1), ("gla_w_gate_bwd", 2),
            ("gla_b_gate_bwd", 1), ("gla_out_norm", 2), ("ffn_conv_w", 2))
_N_MATRICES = 8
_REPLICATED = ("mix_norm_even", "mla_q_norm", "mla_kv_norm", "mla_q_head_norm", "mla_k_head_norm", "ret_theta_fwd",
               "ret_theta_bwd", "ret_out_norm", "ffn_norm", "ffn_conv_b")
_WEIGHTS = ("mix_norm_even", "w_in_even", "mla_q_norm", "mla_kv_norm", "mla_w_uq", "mla_w_ukv", "mla_q_head_norm",
            "mla_k_head_norm", "ret_theta_fwd", "ret_theta_bwd", "ret_out_norm", "w_out_even", "mix_norm_odd",
            "w_in_odd", "gla_w_gate_fwd", "gla_b_gate_fwd", "gla_w_gate_bwd", "gla_b_gate_bwd", "gla_out_norm",
            "w_out_odd", "ffn_norm", "ffn_w_up", "ffn_conv_w", "ffn_conv_b", "ffn_w_down")


def _flatten(arrs, row_multiple, dtype):
    flat = jnp.concatenate([a.reshape(-1).astype(dtype) for a in arrs])
    per = ROW * row_multiple
    total = -(-flat.shape[0] // per) * per
    return jnp.pad(flat, (0, total - flat.shape[0])).reshape(-1, ROW)


def _unflatten(flat, shapes):
    flat = flat.reshape(-1)
    out, o = [], 0
    for shp in shapes:
        n = math.prod(shp)
        out.append(flat[o:o + n].reshape(shp))
        o += n
    return out


def kernel(x, positions, mix_norm_even, w_in_even, mla_q_norm, mla_kv_norm, mla_w_uq, mla_w_ukv, mla_q_head_norm, mla_k_head_norm, ret_theta_fwd, ret_theta_bwd, ret_out_norm, w_out_even, mix_norm_odd, w_in_odd, gla_w_gate_fwd, gla_b_gate_fwd, gla_w_gate_bwd, gla_b_gate_bwd, gla_out_norm, w_out_odd, ffn_norm, ffn_w_up, ffn_conv_w, ffn_conv_b, ffn_w_down, loss_target, m_mix_norm_even, m_w_in_even, m_mla_q_norm, m_mla_kv_norm, m_mla_w_uq, m_mla_w_ukv, m_mla_q_head_norm, m_mla_k_head_norm, m_ret_theta_fwd, m_ret_theta_bwd, m_ret_out_norm, m_w_out_even, m_mix_norm_odd, m_w_in_odd, m_gla_w_gate_fwd, m_gla_b_gate_fwd, m_gla_w_gate_bwd, m_gla_b_gate_bwd, m_gla_out_norm, m_w_out_odd, m_ffn_norm, m_ffn_w_up, m_ffn_conv_w, m_ffn_conv_b, m_ffn_w_down, v_mix_norm_even, v_w_in_even, v_mla_q_norm, v_mla_kv_norm, v_mla_w_uq, v_mla_w_ukv, v_mla_q_head_norm, v_mla_k_head_norm, v_ret_theta_fwd, v_ret_theta_bwd, v_ret_out_norm, v_w_out_even, v_mix_norm_odd, v_w_in_odd, v_gla_w_gate_fwd, v_gla_b_gate_fwd, v_gla_w_gate_bwd, v_gla_b_gate_bwd, v_gla_out_norm, v_w_out_odd, v_ffn_norm, v_ffn_w_up, v_ffn_conv_w, v_ffn_conv_b, v_ffn_w_down):
    args = dict(locals())
    x2, pos, target = args["x"][0], args["positions"][0], args["loss_target"][0]
    axis = dict(_SHARDED)
    mats = [n for n, _ in _SHARDED[:_N_MATRICES]]
    smalls = [n for n, _ in _SHARDED[_N_MATRICES:]]
    small_shapes = [args[n].shape for n in smalls]

    gathered = _gather_chips([_bf(args[n]) for n in mats] + [_flatten([args[n] for n in smalls], 2 * HALO, F32)])
    full = {n: args[n] for n in _REPLICATED}
    for n, g in zip(mats, gathered):
        full[n] = g if n == "ffn_w_up" else jnp.concatenate([g[j] for j in range(CHIPS)], axis=axis[n])
    per_chip = [_unflatten(gathered[-1][j], small_shapes) for j in range(CHIPS)]
    for k, n in enumerate(smalls):
        full[n] = jnp.concatenate([per_chip[j][k] for j in range(CHIPS)], axis=axis[n])

    loss, grad_x, grads = _local_step(x2, pos, target, full)
    loss = lax.psum(loss, ("x", "y", "c"))

    def by_chip(n, g):
        if n == "ffn_w_up":
            return g
        if axis[n] == 1:
            return g.reshape((CHIPS, g.shape[0] // CHIPS) + g.shape[1:])
        return jnp.stack(jnp.split(g, CHIPS, axis=axis[n] - 1))

    core = lax.axis_index("c").astype(jnp.int32).reshape(1)
    stacked = [jnp.stack([by_chip(n, g) for g in grads[n]], axis=1) for n in mats]
    small_parts = [jnp.split(jnp.stack(grads[n]), CHIPS, axis=axis[n]) for n in smalls]
    stacked.append(jnp.stack([_flatten([p[j] for p in small_parts], 2 * HALO, F32) for j in range(CHIPS)])[:, None])
    names = mats + ["small"]
    tiles = [_rows_tile(a.shape[2] // CORES, a.shape[3]) for a in stacked]
    got = _swap_halves(stacked)
    chip_sums = [_add_core_halves(a, b, core, ts=ts, name="add_core_halves_" + n)
                 for n, a, b, ts in zip(names, stacked, got, tiles)]
    parts = _scatter_chips(chip_sums)
    sums = [_add_chip_parts(p, core, ts=ts, name="add_chip_parts_" + n) for n, p, ts in zip(names, parts, tiles)]
    reduced = _gather_cores(sums)

    res = {}

    def update(n, w, g, m, v, ts):
        cols = g.shape[-1]
        outs = _adamw(w.reshape(-1, cols), g.reshape(-1, cols), m.reshape(-1, cols), v.reshape(-1, cols), ts=ts,
                      name="adamw_" + n)
        return [g] + [o.reshape(g.shape) for o in outs]

    kinds = ("grad", "delta", "new_m", "new_v")
    for n, g, ts in zip(mats, reduced, tiles):
        for kind, a in zip(kinds, update(n, args[n], g, args["m_" + n], args["v_" + n], ts)):
            res[kind + "_" + n] = a
    w_s, m_s, v_s = (_flatten([args[pre + n] for n in smalls], 2 * HALO, F32) for pre in ("", "m_", "v_"))
    for kind, flat in zip(kinds, update("small", w_s, reduced[-1][0], m_s, v_s, tiles[-1])):
        for n, a in zip(smalls, _unflatten(flat, small_shapes)):
            res[kind + "_" + n] = a

    rep_shapes = [args[n].shape for n in _REPLICATED]
    g_rep = _all_reduce_devices(_flatten([jnp.stack(grads[n]) for n in _REPLICATED], HALO, F32))
    w_rep, m_rep, v_rep = (_flatten([args[pre + n] for n in _REPLICATED], HALO, F32) for pre in ("", "m_", "v_"))
    for kind, flat in zip(kinds, update("replicated", w_rep, g_rep, m_rep, v_rep, g_rep.shape[0])):
        for n, a in zip(_REPLICATED, _unflatten(flat, rep_shapes)):
            res[kind + "_" + n] = a

    outs = [loss, grad_x[None]]
    for kind in ("grad", "delta", "new_m", "new_v"):
        outs += [res[kind + "_" + n] for n in _WEIGHTS]
    return tuple(outs)
```

```python
import math

import jax
import jax.numpy as jnp
from jax import lax
from jax.experimental import pallas as pl
from jax.experimental.pallas import tpu as pltpu

F32 = jnp.float32
BF16 = jnp.bfloat16
MESH = pl.DeviceIdType.MESH

EPS = 1e-6
D_MODEL = 1024
DEPTH = 4
LANES = 128
MLA_H, MLA_QR, MLA_KVR, MLA_NOPE, MLA_ROPE, MLA_V = 8, 384, 256, 64, 32, 64
MLA_QK = MLA_NOPE + MLA_ROPE
MLA_SCALE = MLA_QK ** -0.5
RET_H, RET_DK, RET_DV, RET_C = 8, 64, 64, 128
GLA_H, GLA_DK, GLA_DV, GLA_R, GLA_TAU, GLA_C = 4, 128, 256, 16, 16.0, 64
D_FF = 2816
ROPE_THETA = 10000.0
LN2 = math.log(2.0)
ADAM_LR, ADAM_B1, ADAM_B2, ADAM_EPS, ADAM_WD, ADAM_STEP = 0.001, 0.9, 0.999, 1e-08, 0.01, 10

EV_RET = 4 * RET_H * LANES
EV_CQ = 512
EV_W = 5120
EV_KR_BLK = (EV_RET + EV_CQ + MLA_KVR) // LANES
OD_W = 3200
OD_GA_BLK = 3072 // LANES

VMEM_LIMIT = 56 * 1024 * 1024
MM_TILE_CAP = 1408
V_ONES = (MLA_V, MLA_V + 1)


def _cp(sem):
    return pltpu.CompilerParams(dimension_semantics=sem, vmem_limit_bytes=VMEM_LIMIT)


def _dot(a, b):
    return jnp.dot(a, b, preferred_element_type=F32)


def _dot_nt(a, b):
    return lax.dot_general(a, b, (((1,), (1,)), ((), ())), preferred_element_type=F32)


def _dot_tn(a, b):
    return lax.dot_general(a, b, (((0,), (0,)), ((), ())), preferred_element_type=F32)


def _bf(x):
    return x.astype(BF16)


def _split3(x):
    h1 = _bf(x)
    r1 = x - h1.astype(F32)
    h2 = _bf(r1)
    h3 = _bf(r1 - h2.astype(F32))
    return h1, h2, h3


def _tile(n, cap):
    if n <= cap:
        return n
    best = None
    for t in range(LANES, cap + 1, LANES):
        if n % t == 0:
            best = t
    assert best is not None, n
    return best


def _mm(a, b, *, ta=False, tb=False, res=None, out_dtype=F32, b_layer=None, out_chips=False, name):
    assert not (ta and tb)
    if ta:
        kdim, m = a.shape
    else:
        m, kdim = a.shape
    if b_layer is not None:
        rows_b, cols_b = b.shape[2], b.shape[0] * b.shape[3]
    else:
        rows_b, cols_b = b.shape
    n, kb = (rows_b, cols_b) if tb else (cols_b, rows_b)
    assert kb == kdim, (a.shape, b.shape, ta, tb)
    tm, tn, tk = _tile(m, MM_TILE_CAP), _tile(n, MM_TILE_CAP), _tile(kdim, MM_TILE_CAP)
    nk = kdim // tk
    has_res = res is not None
    vmem = (2 * tm * tk * a.dtype.itemsize + 2 * tk * tn * b.dtype.itemsize
            + 2 * tm * tn * jnp.dtype(out_dtype).itemsize + (2 * tm * tn * 4 if has_res else 0)
            + (tm * tn * 4 if nk > 1 else 0))
    assert vmem <= VMEM_LIMIT - 8 * 1024 * 1024, (name, vmem)
    a_spec = (pl.BlockSpec((tk, tm), lambda i, j, k: (k, i)) if ta
              else pl.BlockSpec((tm, tk), lambda i, j, k: (i, k)))
    if b_layer is not None:
        per_chip = b.shape[3]
        if tb:
            assert tk == per_chip
            b_spec = pl.BlockSpec((None, None, tn, tk), lambda i, j, k: (k, b_layer, j, 0))
        else:
            assert tn == per_chip
            b_spec = pl.BlockSpec((None, None, tk, tn), lambda i, j, k: (j, b_layer, k, 0))
    else:
        b_spec = (pl.BlockSpec((tn, tk), lambda i, j, k: (j, k)) if tb
                  else pl.BlockSpec((tk, tn), lambda i, j, k: (k, j)))
    if out_chips:
        assert n // tn == CHIPS and not has_res
        o_spec = pl.BlockSpec((None, tm, tn), lambda i, j, k: (j, i, 0))
        out_struct = jax.ShapeDtypeStruct((CHIPS, m, tn), out_dtype)
    else:
        o_spec = pl.BlockSpec((tm, tn), lambda i, j, k: (i, j))
        out_struct = jax.ShapeDtypeStruct((m, n), out_dtype)

    def product(a_ref, b_ref):
        av, bv = _bf(a_ref[...]), _bf(b_ref[...])
        if ta:
            return _dot_tn(av, bv)
        if tb:
            return _dot_nt(av, bv)
        return _dot(av, bv)

    def body(*refs):
        a_ref, b_ref = refs[:2]
        r_ref = refs[2] if has_res else None
        o_ref = refs[3] if has_res else refs[2]

        def finish(r):
            if has_res:
                r = r + r_ref[...]
            o_ref[...] = r.astype(o_ref.dtype)

        if nk == 1:
            finish(product(a_ref, b_ref))
            return
        acc = refs[-1]
        k = pl.program_id(2)

        @pl.when(k == 0)
        def _():
            acc[...] = product(a_ref, b_ref)

        @pl.when(k > 0)
        def _():
            acc[...] += product(a_ref, b_ref)

        @pl.when(k == nk - 1)
        def _():
            finish(acc[...])

    ins = [a, b] + ([res] if has_res else [])
    in_specs = [a_spec, b_spec] + ([o_spec] if has_res else [])
    return pl.pallas_call(
        body, name=name, grid=(m // tm, n // tn, nk),
        in_specs=in_specs, out_specs=o_spec, out_shape=out_struct,
        scratch_shapes=[pltpu.VMEM((tm, tn), F32)] if nk > 1 else [],
        compiler_params=_cp(("parallel", "parallel", "arbitrary")),
    )(*ins)


def _ew(fn, rows, pars, outs, accs=(), *, s, ts, name):
    n_in = len(rows) + len(pars)
    n_o = len(outs)

    def body(*refs):
        i = pl.program_id(0)
        vals = fn(*[r[...] for r in refs[:n_in]])
        if not isinstance(vals, (tuple, list)):
            vals = (vals,)
        assert len(vals) == n_o + len(accs), (name, len(vals))
        for r, v in zip(refs[n_in:n_in + n_o], vals[:n_o]):
            r[...] = v.astype(r.dtype)
        for r, v in zip(refs[n_in + n_o:], vals[n_o:]):
            @pl.when(i == 0)
            def _(r=r, v=v):
                r[...] = v

            @pl.when(i > 0)
            def _(r=r, v=v):
                r[...] += v

    in_specs = [sp for _, sp in rows]
    in_specs += [pl.BlockSpec(p.shape, lambda i, nd=p.ndim: (0,) * nd) for p in pars]
    out_specs = [pl.BlockSpec((ts, w), lambda i: (i, 0)) for w, _ in outs]
    out_specs += [pl.BlockSpec((r, w), lambda i: (0, 0)) for r, w in accs]
    out_shape = [jax.ShapeDtypeStruct((s, w), dt) for w, dt in outs]
    out_shape += [jax.ShapeDtypeStruct((r, w), F32) for r, w in accs]
    return pl.pallas_call(
        body, name=name, grid=(s // ts,), in_specs=in_specs, out_specs=out_specs, out_shape=out_shape,
        compiler_params=_cp(("arbitrary",)),
    )(*[a for a, _ in rows], *pars)


def _cols(arr, width, blk, ts):
    return (arr, pl.BlockSpec((ts, width), lambda i, b=blk: (i, b)))


def _lead(arr, d, ts):
    return (arr, pl.BlockSpec((None, ts, arr.shape[2]), lambda i, d=d: (d, i, 0)))


def _rowsum(x):
    return jnp.sum(x, axis=0, keepdims=True)


def _lanesum(x):
    return jnp.sum(x, axis=-1, keepdims=True)


def _gsum(x, group):
    w = x.shape[-1]
    if group == w:
        return jnp.broadcast_to(_lanesum(x), x.shape)
    parts = [jnp.broadcast_to(_lanesum(x[:, g:g + group]), (x.shape[0], group)) for g in range(0, w, group)]
    return jnp.concatenate(parts, axis=-1)


def _gn(x, gain, group, n):
    rstd = lax.rsqrt(_gsum(x * x, group) * (1.0 / n) + EPS)
    xn = x * rstd
    return xn * gain, xn, rstd


def _gn_bwd(dy, xn, rstd, gain, group, n):
    dxn = dy * gain
    dx = rstd * (dxn - xn * (_gsum(dxn * xn, group) * (1.0 / n)))
    return dx, _rowsum(dy * xn)


def _sigmoid(x):
    return 1.0 / (1.0 + jnp.exp(-x))


def _rmsnorm(x_row, g, *, n, s, ts, name):
    w = g.shape[-1]

    def fn(x, gv):
        return _gn(x, gv, w, n)[0]

    return _ew(fn, [x_row], [g], [(w, BF16)], s=s, ts=ts, name=name)[0]


def _rmsnorm_bwd(x_row, g, dh, dres, *, n, s, ts, name):
    w = g.shape[-1]
    has_res = dres is not None

    def fn(x, dhv, *rest):
        gv = rest[-1]
        _, xn, rstd = _gn(x, gv, w, n)
        dx, dg = _gn_bwd(dhv, xn, rstd, gv, w, n)
        if has_res:
            dx = dx + rest[0]
        return dx, dg

    rows = [x_row, _cols(dh, w, 0, ts)] + ([_cols(dres, w, 0, ts)] if has_res else [])
    return _ew(fn, rows, [g], [(w, F32)], [(1, w)], s=s, ts=ts, name=name)


def _rope_tables(pos, real, offset):
    half = real // 2
    inv = ROPE_THETA ** (-jnp.arange(half, dtype=F32) / half)
    ang = pos.astype(F32)[:, None] * inv
    c, sn = jnp.cos(ang), jnp.sin(ang)
    s = pos.shape[0]
    cos_t = jnp.concatenate([jnp.ones((s, offset), F32), c, c,
                             jnp.ones((s, LANES - offset - real), F32)], axis=1)
    sin_t = jnp.concatenate([jnp.zeros((s, offset), F32), -sn, sn,
                             jnp.zeros((s, LANES - offset - real), F32)], axis=1)
    return cos_t, sin_t


def _rope(x, cos_t, sin_t, real, offset):
    half = real // 2
    lane = lax.broadcasted_iota(jnp.int32, x.shape, 1)
    partner = jnp.where(lane < offset + half, pltpu.roll(x, LANES - half, 1), pltpu.roll(x, half, 1))
    return x * cos_t + partner * sin_t


def _mla_prep(q_pre, kv_pre, p_even, cos_m, sin_m, qhn, khn, *, s, ts):
    w = MLA_H * LANES

    def fn(qp, kp, vp, kr, c, sn, gq, gk):
        qs, ks = [], []
        for h in range(MLA_H):
            sl = slice(h * LANES, (h + 1) * LANES)
            qn = _gn(qp[:, sl], gq, LANES, MLA_QK)[0]
            kn = _gn(kp[:, sl] + kr, gk, LANES, MLA_QK)[0]
            qs.append(_rope(qn, c, sn, MLA_ROPE, MLA_NOPE) * MLA_SCALE)
            ks.append(_rope(kn, c, sn, MLA_ROPE, MLA_NOPE))
        lane = lax.broadcasted_iota(jnp.int32, vp.shape, 1) % LANES
        ones = (lane == V_ONES[0]) | (lane == V_ONES[1])
        return jnp.concatenate(qs, axis=1), jnp.concatenate(ks, axis=1), jnp.where(ones, 1.0, vp)

    rows = [_cols(q_pre, w, 0, ts), _cols(kv_pre, w, 0, ts), _cols(kv_pre, w, 1, ts),
            _cols(p_even, LANES, EV_KR_BLK, ts), _cols(cos_m, LANES, 0, ts), _cols(sin_m, LANES, 0, ts)]
    return _ew(fn, rows, [qhn, khn], [(w, BF16)] * 3, s=s, ts=ts, name="mla_prep")


def _mla_prep_bwd(q_pre, kv_pre, p_even, cos_m, sin_m, qhn, khn, dq, dk, *, s, ts):
    w = MLA_H * LANES

    def fn(qp, kp, kr, c, sn, dqv, dkv, gq, gk):
        dqs, dks = [], []
        dkr = jnp.zeros_like(kr)
        dgq = jnp.zeros((1, LANES), F32)
        dgk = jnp.zeros((1, LANES), F32)
        for h in range(MLA_H):
            sl = slice(h * LANES, (h + 1) * LANES)
            _, qn, qr = _gn(qp[:, sl], gq, LANES, MLA_QK)
            _, kn, krs = _gn(kp[:, sl] + kr, gk, LANES, MLA_QK)
            dqn = _rope(dqv[:, sl] * MLA_SCALE, c, -sn, MLA_ROPE, MLA_NOPE)
            dkn = _rope(dkv[:, sl], c, -sn, MLA_ROPE, MLA_NOPE)
            dqh, g1 = _gn_bwd(dqn, qn, qr, gq, LANES, MLA_QK)
            dkh, g2 = _gn_bwd(dkn, kn, krs, gk, LANES, MLA_QK)
            dqs.append(dqh)
            dks.append(dkh)
            dkr = dkr + dkh
            dgq = dgq + g1
            dgk = dgk + g2
        return jnp.concatenate(dqs, axis=1), jnp.concatenate(dks, axis=1), dkr, dgq, dgk

    rows = [_cols(q_pre, w, 0, ts), _cols(kv_pre, w, 0, ts), _cols(p_even, LANES, EV_KR_BLK, ts),
            _cols(cos_m, LANES, 0, ts), _cols(sin_m, LANES, 0, ts), _cols(dq, w, 0, ts), _cols(dk, w, 0, ts)]
    return _ew(fn, rows, [qhn, khn], [(w, BF16), (w, BF16), (LANES, BF16)], [(1, LANES), (1, LANES)],
               s=s, ts=ts, name="mla_prep_bwd")


def _flash_fwd(q, k, v, *, tq, tk, side=()):
    s = q.shape[0]
    nq, nk = s // tq, s // tk
    rq = min(tq, 256)
    ns = len(side)

    def body(*refs):
        q_ref, k_ref, v_ref = refs[:3]
        o_ref, lse_ref = refs[3 + ns:5 + ns]
        m_s, acc = refs[5 + 2 * ns:7 + 2 * ns]
        h, i, j = pl.program_id(0), pl.program_id(1), pl.program_id(2)
        if ns:
            local, sends, arrivals = _gather_copies(side, refs[3:3 + ns], refs[5 + ns:5 + 2 * ns], *refs[7 + 2 * ns:])

            @pl.when((h == 0) & (i == 0) & (j == 0))
            def _():
                for cp in local + sends:
                    cp.start()

        @pl.when(j == 0)
        def _():
            m_s[...] = jnp.full_like(m_s, -jnp.inf)
            acc[...] = jnp.zeros_like(acc)

        kv, vv = k_ref[...], v_ref[...]
        for r in range(0, tq, rq):
            rows = slice(r, r + rq)
            sc = _dot_nt(q_ref[rows, :], kv)
            m_prev = m_s[rows, :]
            m_new = jnp.maximum(m_prev, jnp.max(sc, axis=-1, keepdims=True))
            p = jnp.exp(sc - jnp.tile(m_new, (1, tk // LANES)))
            acc[rows, :] = jnp.exp(m_prev - m_new) * acc[rows, :] + _dot(_bf(p), vv)
            m_s[rows, :] = m_new

        @pl.when(j == nk - 1)
        def _():
            a = acc[...]
            l = a[:, V_ONES[0]:V_ONES[0] + 1]
            o_ref[...] = (a / l).astype(o_ref.dtype)
            lse_ref[...] = m_s[:, 0:1] + jnp.log(l)

        if ns:
            @pl.when((h == MLA_H - 1) & (i == nq - 1) & (j == nk - 1))
            def _():
                for cp in arrivals:
                    cp.wait_recv()
                for cp in sends:
                    cp.wait_send()
                for cp in local:
                    cp.wait()

    qs = pl.BlockSpec((tq, LANES), lambda h, i, j: (i, h))
    ks = pl.BlockSpec((tk, LANES), lambda h, i, j: (j, h))
    outs = pl.pallas_call(
        body, name="mla_flash_fwd_gather" if ns else "mla_flash_fwd", grid=(MLA_H, nq, nk),
        in_specs=[qs, ks, ks] + [HBM_SPEC] * ns,
        out_specs=[qs, pl.BlockSpec((None, tq, 1), lambda h, i, j: (h, i, 0))] + [HBM_SPEC] * ns,
        out_shape=[jax.ShapeDtypeStruct((s, MLA_H * LANES), BF16), jax.ShapeDtypeStruct((MLA_H, s, 1), F32)]
        + _gather_shapes(side),
        scratch_shapes=[pltpu.VMEM((tq, LANES), F32), pltpu.VMEM((tq, LANES), F32)]
        + ([_sems(3 * ns), _sems(3 * ns), _sems(ns)] if ns else []),
        compiler_params=_cp(("arbitrary",) * 3 if ns else ("parallel", "parallel", "arbitrary")),
    )(q, k, v, *[a for a, _, _ in side])
    return outs[0], outs[1], list(outs[2:])


def _attn_bwd_prep(dar, o, *, s, ts):
    w = MLA_H * LANES

    def fn(dov, ov):
        outs = []
        lane = lax.broadcasted_iota(jnp.int32, (dov.shape[0], LANES), 1)
        for h in range(MLA_H):
            sl = slice(h * LANES, (h + 1) * LANES)
            d = dov[:, sl]
            delta = _lanesum(d * ov[:, sl].astype(F32))
            hi = _bf(delta).astype(F32)
            outs.append(jnp.where(lane == V_ONES[0], -hi, jnp.where(lane == V_ONES[1], hi - delta, d)))
        return jnp.concatenate(outs, axis=1)

    return _ew(fn, [_cols(dar, w, 0, ts), _cols(o, w, 0, ts)], [], [(w, BF16)], s=s, ts=ts,
               name="mla_attn_bwd_prep")[0]


def _flash_bwd(q, k, v, do, lse, *, tq, tk):
    s = q.shape[0]
    nq, nk = s // tq, s // tk

    def body(q_ref, k_ref, v_ref, do_ref, lse_ref, dq_ref, dk_ref, dv_ref, dk_acc, dv_acc):
        j = pl.program_id(1)
        i = pl.program_id(2)
        qv, kv, dov = q_ref[...], k_ref[...], do_ref[...]
        p = jnp.exp(_dot_nt(qv, kv) - lse_ref[...])
        ds = _bf(p * _dot_nt(dov, v_ref[...]))
        dv_c = _dot_tn(_bf(p), dov)
        dk_c = _dot_tn(ds, qv)
        dq_c = _dot(ds, kv)
        rows = pl.ds(pl.multiple_of(i * tq, tq), tq)

        @pl.when(i == 0)
        def _():
            dk_acc[...] = dk_c
            dv_acc[...] = dv_c

        @pl.when(i > 0)
        def _():
            dk_acc[...] += dk_c
            dv_acc[...] += dv_c

        @pl.when(j == 0)
        def _():
            dq_ref[rows, :] = dq_c

        @pl.when(j > 0)
        def _():
            dq_ref[rows, :] += dq_c

        @pl.when(i == nq - 1)
        def _():
            dk_ref[...] = dk_acc[...]
            dv_ref[...] = dv_acc[...].astype(dv_ref.dtype)

    qs = pl.BlockSpec((tq, LANES), lambda h, j, i: (i, h))
    ks = pl.BlockSpec((tk, LANES), lambda h, j, i: (j, h))
    st = pl.BlockSpec((None, tq, 1), lambda h, j, i: (h, i, 0))
    return pl.pallas_call(
        body, name="mla_flash_bwd", grid=(MLA_H, nk, nq),
        in_specs=[qs, ks, ks, qs, st],
        out_specs=[pl.BlockSpec((s, LANES), lambda h, j, i: (0, h)), ks, ks],
        out_shape=[jax.ShapeDtypeStruct((s, MLA_H * LANES), F32), jax.ShapeDtypeStruct((s, MLA_H * LANES), F32),
                   jax.ShapeDtypeStruct((s, MLA_H * LANES), BF16)],
        scratch_shapes=[pltpu.VMEM((tk, LANES), F32), pltpu.VMEM((tk, LANES), F32)],
        compiler_params=_cp(("parallel", "arbitrary", "arbitrary")),
    )(q, k, v, do, lse)


def _ret_geometry(d, c):
    df = d.astype(F32)
    ii = lax.broadcasted_iota(jnp.int32, (c, c), 0).astype(F32)
    jj = lax.broadcasted_iota(jnp.int32, (c, c), 1).astype(F32)
    rel = (ii - jj) * (1.0 - 2.0 * df)
    mask = rel >= df
    rel0 = jnp.maximum(rel, 0.0)
    pos = lax.broadcasted_iota(jnp.int32, (c, 1), 0).astype(F32)
    ez = (c - 1 - pos) + df * (2.0 * pos - (c - 1))
    ex = (pos + 1.0) + df * (c - 1 - 2.0 * pos)
    return mask, rel0, ez, ex


def _chunk_index(n_chunks):
    return lambda d, n: n + d * (n_chunks - 1 - 2 * n)


def _ret_fwd(p_even, cos_r, sin_r, theta_l):
    s = p_even.shape[0]
    c = RET_C
    n_chunks = s // c
    w = RET_H * LANES
    cidx = _chunk_index(n_chunks)

    def body(q_ref, k_ref, v_ref, cos_ref, sin_ref, th_ref, o_ref, rp_ref, r_s):
        d = pl.program_id(0)
        n = pl.program_id(1)

        @pl.when(n == 0)
        def _():
            r_s[...] = jnp.zeros_like(r_s)

        lg = jnp.log1p(-jnp.exp(-th_ref[...] * LN2))
        mask, rel0, ez, ex = _ret_geometry(d, c)
        cs, sn = cos_ref[...], sin_ref[...]
        rp_ref[...] = r_s[...]
        for h in range(RET_H):
            sl = slice(h * LANES, (h + 1) * LANES)
            lgh = lg[:, h * LANES:h * LANES + 1]
            dm = jnp.where(mask, jnp.exp(lgh * rel0), 0.0)
            qh = _bf(_rope(q_ref[:, sl], cs, sn, RET_DK, 0))
            kf = _rope(k_ref[:, sl], cs, sn, RET_DK, 0) * (RET_DK ** -0.5)
            kh = _bf(kf)
            vh = _bf(v_ref[:, sl])
            rh = r_s[sl, :]
            a = _dot_nt(qh, kh) * dm
            o_ref[:, sl] = _dot(_bf(a), vh) + jnp.exp(lgh * ex) * _dot(qh, _bf(rh))
            zk = _bf(kf * jnp.exp(lgh * ez))
            r_s[sl, :] = jnp.exp(lgh * c) * rh + _dot_tn(zk, vh)

    def col(blk):
        return pl.BlockSpec((c, w), lambda d, n: (cidx(d, n), blk))

    tab = pl.BlockSpec((c, LANES), lambda d, n: (cidx(d, n), 0))
    return pl.pallas_call(
        body, name="ret_fwd", grid=(2, n_chunks),
        in_specs=[col(0), col(1), col(2), tab, tab, pl.BlockSpec((None, 1, w), lambda d, n: (d, 0, 0))],
        out_specs=[pl.BlockSpec((None, c, w), lambda d, n: (d, cidx(d, n), 0)),
                   pl.BlockSpec((None, None, w, LANES), lambda d, n: (d, cidx(d, n), 0, 0))],
        out_shape=[jax.ShapeDtypeStruct((2, s, w), F32), jax.ShapeDtypeStruct((2, n_chunks, w, LANES), F32)],
        scratch_shapes=[pltpu.VMEM((w, LANES), F32)],
        compiler_params=_cp(("arbitrary", "arbitrary")),
    )(p_even, p_even, p_even, cos_r, sin_r, theta_l)


def _ret_bwd(p_even, cos_r, sin_r, theta_l, theta_h, r_prev, do):
    s = p_even.shape[0]
    c = RET_C
    n_chunks = s // c
    w = RET_H * LANES
    fwd_idx = _chunk_index(n_chunks)

    def cidx(d, n):
        return fwd_idx(d, n_chunks - 1 - n)

    def body(q_ref, k_ref, v_ref, cos_ref, sin_ref, th_ref, thh_ref, rp_ref, do_ref,
             dq_ref, dk_ref, dv_ref, dth_ref, dr_s):
        d = pl.program_id(0)
        n = pl.program_id(1)

        @pl.when(n == 0)
        def _():
            dr_s[...] = jnp.zeros_like(dr_s)
            dth_ref[...] = jnp.zeros_like(dth_ref)

        lg = jnp.log1p(-jnp.exp(-th_ref[...] * LN2))
        mask, rel0, ez, ex = _ret_geometry(d, c)
        cs, sn = cos_ref[...], sin_ref[...]
        row = lax.broadcasted_iota(jnp.int32, (RET_H, LANES), 0)
        dlg = jnp.zeros((RET_H, LANES), F32)
        kscale = RET_DK ** -0.5
        for h in range(RET_H):
            sl = slice(h * LANES, (h + 1) * LANES)
            lgh = lg[:, h * LANES:h * LANES + 1]
            dm = jnp.where(mask, jnp.exp(lgh * rel0), 0.0)
            zeta = jnp.exp(lgh * ez)
            xi = jnp.exp(lgh * ex)
            gc = jnp.exp(lgh * c)
            qf = _rope(q_ref[:, sl], cs, sn, RET_DK, 0)
            qh = _bf(qf)
            kf = _rope(k_ref[:, sl], cs, sn, RET_DK, 0) * kscale
            kh = _bf(kf)
            zkf = kf * zeta
            zk = _bf(zkf)
            vh = _bf(v_ref[:, sl])
            dof = do_ref[:, sl]
            doh = _bf(dof)
            rp = rp_ref[sl, :]
            rpb = _bf(rp)
            drn = dr_s[sl, :]
            drb = _bf(drn)
            a = _dot_nt(qh, kh) * dm
            da0 = _dot_nt(doh, vh)
            da = _bf(da0 * dm)
            vdr = _dot_nt(vh, drb)
            dq_r = _dot(da, kh) + xi * _dot_nt(doh, rpb)
            dk_r = _dot_tn(da, qh) + zeta * vdr
            dv_ref[:, sl] = _dot_tn(_bf(a), doh) + _dot(zk, drb)
            dq_ref[:, sl] = _rope(dq_r, cs, -sn, RET_DK, 0)
            dk_ref[:, sl] = _rope(dk_r * kscale, cs, -sn, RET_DK, 0)
            dr_s[sl, :] = _dot_tn(_bf(qf * xi), doh) + gc * drn
            ocross = xi * _dot(qh, rpb)
            t = (jnp.sum(rel0 * a * da0, keepdims=True)
                 + jnp.sum(ex * dof * ocross, keepdims=True)
                 + c * gc * jnp.sum(drn * rp, keepdims=True)
                 + jnp.sum(ez * zkf * vdr, keepdims=True))
            dlg = jnp.where(row == h, t, dlg)
        x2 = jnp.exp(-thh_ref[...] * LN2)
        dth_ref[...] += dlg * (x2 * LN2 / (1.0 - x2))

    def col(blk):
        return pl.BlockSpec((c, w), lambda d, n: (cidx(d, n), blk))

    tab = pl.BlockSpec((c, LANES), lambda d, n: (cidx(d, n), 0))
    dirrow = pl.BlockSpec((None, c, w), lambda d, n: (d, cidx(d, n), 0))
    hrow = pl.BlockSpec((None, RET_H, LANES), lambda d, n: (d, 0, 0))
    return pl.pallas_call(
        body, name="ret_bwd", grid=(2, n_chunks),
        in_specs=[col(0), col(1), col(2), tab, tab, pl.BlockSpec((None, 1, w), lambda d, n: (d, 0, 0)), hrow,
                  pl.BlockSpec((None, None, w, LANES), lambda d, n: (d, cidx(d, n), 0, 0)),
                  pl.BlockSpec((c, w), lambda d, n: (cidx(d, n), 0))],
        out_specs=[dirrow, dirrow, dirrow, hrow],
        out_shape=[jax.ShapeDtypeStruct((2, s, w), F32)] * 3 + [jax.ShapeDtypeStruct((2, RET_H, LANES), F32)],
        scratch_shapes=[pltpu.VMEM((w, LANES), F32)],
        compiler_params=_cp(("arbitrary", "arbitrary")),
    )(p_even, p_even, p_even, cos_r, sin_r, theta_l, theta_h, r_prev, do)


def _post_fwd(o2, gate_row, gain, *, group, n, s, ts, name):
    w = o2.shape[2]

    def fn(of, ob, g, gv):
        y = _gn(of + ob, gv, group, n)[0]
        return g * _sigmoid(g) * y

    return _ew(fn, [_lead(o2, 0, ts), _lead(o2, 1, ts), gate_row], [gain], [(w, BF16)], s=s, ts=ts, name=name)[0]


def _post_bwd(o2, gate_row, gain, dr_row, *, group, n, s, ts, name):
    w = o2.shape[2]

    def fn(of, ob, g, dr, gv):
        y, xn, rstd = _gn(of + ob, gv, group, n)
        sg = _sigmoid(g)
        dy = dr * (g * sg)
        dgate = dr * y * (sg * (1.0 + g * (1.0 - sg)))
        do, dgain = _gn_bwd(dy, xn, rstd, gv, group, n)
        return do, dgate, dgain

    return _ew(fn, [_lead(o2, 0, ts), _lead(o2, 1, ts), gate_row, dr_row], [gain],
               [(w, F32), (w, BF16)], [(1, w)], s=s, ts=ts, name=name)


def _sum2(a2, *, s, ts, name):
    w = a2.shape[2]
    return _ew(lambda a, b: a + b, [_lead(a2, 0, ts), _lead(a2, 1, ts)], [], [(w, BF16)], s=s, ts=ts, name=name)[0]


def _gla_common(d, q_ref, k_ref, ga_ref, wg_ref, bg_ref):
    c = GLA_C
    df = d.astype(F32)
    ii = lax.broadcasted_iota(jnp.int32, (c, c), 0).astype(F32)
    jj = lax.broadcasted_iota(jnp.int32, (c, c), 1).astype(F32)
    rel = (ii - jj) * (1.0 - 2.0 * df)
    tri = _bf(jnp.where(rel >= 0.0, 1.0, 0.0))
    mask = rel >= df
    gab = _bf(ga_ref[...])
    z = _dot(gab, wg_ref[...]) + bg_ref[...]
    la = (jnp.minimum(z, 0.0) - jnp.log1p(jnp.exp(-jnp.abs(z)))) * (1.0 / GLA_TAU)
    l1, l2, l3 = _split3(la)
    b = _dot(tri, l1) + _dot(tri, l2) + _dot(tri, l3)
    first = d == 0
    bm = jnp.where(first, b[c // 2:c // 2 + 1], b[c // 2 - 1:c // 2])
    bl = jnp.where(first, b[c - 1:c], b[0:1])
    q = q_ref[...] * (GLA_DK ** -0.5)
    k = k_ref[...]
    e1, e2, e3, eb = jnp.exp(b - bm), jnp.exp(bm - b), jnp.exp(bl - b), jnp.exp(b)
    return dict(tri=tri, mask=mask, gab=gab, z=z, ebl=jnp.exp(bl), e1=e1, e2=e2, e3=e3, eb=eb,
                qc=q * e1, kc=k * e2, kd=k * e3, qe=q * eb, first=first)


def _col_scale(row_vec, width):
    t = jnp.broadcast_to(row_vec, (LANES, LANES)).T
    return jnp.concatenate([t] * (width // LANES), axis=1)


def _gla_fwd(p_odd, wg2, bg2):
    s = p_odd.shape[0]
    c = GLA_C
    n_chunks = s // c
    wk, wv = GLA_H * GLA_DK, GLA_H * GLA_DV
    cidx = _chunk_index(n_chunks)

    def body(q_ref, k_ref, v_ref, ga_ref, wg_ref, bg_ref, o_ref, sp_ref, s_s):
        d = pl.program_id(0)
        n = pl.program_id(1)

        @pl.when(n == 0)
        def _():
            s_s[...] = jnp.zeros_like(s_s)

        g = _gla_common(d, q_ref, k_ref, ga_ref, wg_ref, bg_ref)
        sp_ref[...] = s_s[...]
        for h in range(GLA_H):
            sl = slice(h * GLA_DK, (h + 1) * GLA_DK)
            vs = slice(h * GLA_DV, (h + 1) * GLA_DV)
            vh = _bf(v_ref[:, vs])
            sh = s_s[sl, :]
            a = jnp.where(g["mask"], _dot_nt(_bf(g["qc"][:, sl]), _bf(g["kc"][:, sl])), 0.0)
            o_ref[:, vs] = _dot(_bf(a), vh) + _dot(_bf(g["qe"][:, sl]), _bf(sh))
            s_s[sl, :] = _col_scale(g["ebl"][:, sl], GLA_DV) * sh + _dot_tn(_bf(g["kd"][:, sl]), vh)

    def col(width, blk):
        return pl.BlockSpec((c, width), lambda d, n: (cidx(d, n), blk))

    return pl.pallas_call(
        body, name="gla_fwd", grid=(2, n_chunks),
        in_specs=[col(wk, 0), col(wk, 1), col(wv, 1), col(LANES, OD_GA_BLK),
                  pl.BlockSpec((None, LANES, wk), lambda d, n: (d, 0, 0)),
                  pl.BlockSpec((None, 1, wk), lambda d, n: (d, 0, 0))],
        out_specs=[pl.BlockSpec((None, c, wv), lambda d, n: (d, cidx(d, n), 0)),
                   pl.BlockSpec((None, None, wk, GLA_DV), lambda d, n: (d, cidx(d, n), 0, 0))],
        out_shape=[jax.ShapeDtypeStruct((2, s, wv), F32), jax.ShapeDtypeStruct((2, n_chunks, wk, GLA_DV), F32)],
        scratch_shapes=[pltpu.VMEM((wk, GLA_DV), F32)],
        compiler_params=_cp(("arbitrary", "arbitrary")),
    )(p_odd, p_odd, p_odd, p_odd, wg2, bg2)


def _gla_bwd(p_odd, wg2, bg2, s_prev, do):
    s = p_odd.shape[0]
    c = GLA_C
    n_chunks = s // c
    wk, wv = GLA_H * GLA_DK, GLA_H * GLA_DV
    fwd_idx = _chunk_index(n_chunks)

    def cidx(d, n):
        return fwd_idx(d, n_chunks - 1 - n)

    def body(q_ref, k_ref, v_ref, ga_ref, wg_ref, bg_ref, sp_ref, do_ref,
             dq_ref, dk_ref, dv_ref, dga_ref, dwg_ref, dbg_ref, ds_s):
        d = pl.program_id(0)
        n = pl.program_id(1)

        @pl.when(n == 0)
        def _():
            ds_s[...] = jnp.zeros_like(ds_s)
            dwg_ref[...] = jnp.zeros_like(dwg_ref)
            dbg_ref[...] = jnp.zeros_like(dbg_ref)

        g = _gla_common(d, q_ref, k_ref, ga_ref, wg_ref, bg_ref)
        mask = g["mask"]
        ones8 = jnp.ones((8, GLA_DV), BF16)
        dbs, dbms, dbls = [], [], []
        for h in range(GLA_H):
            sl = slice(h * GLA_DK, (h + 1) * GLA_DK)
            vs = slice(h * GLA_DV, (h + 1) * GLA_DV)
            qc, kc, kd, qe = g["qc"][:, sl], g["kc"][:, sl], g["kd"][:, sl], g["qe"][:, sl]
            qcb, kcb, kdb, qeb = _bf(qc), _bf(kc), _bf(kd), _bf(qe)
            vh = _bf(v_ref[:, vs])
            doh = _bf(do_ref[:, vs])
            sp = sp_ref[sl, :]
            dsn = ds_s[sl, :]
            dsb = _bf(dsn)
            a = _bf(jnp.where(mask, _dot_nt(qcb, kcb), 0.0))
            da = _bf(jnp.where(mask, _dot_nt(doh, vh), 0.0))
            dv_ref[:, vs] = _dot_tn(a, doh) + _dot(kdb, dsb)
            dqc = _dot(da, kcb)
            dkc = _dot_tn(da, qcb)
            dqe = _dot_nt(doh, _bf(sp))
            dkd = _dot_nt(vh, dsb)
            ds_s[sl, :] = _dot_tn(qeb, doh) + _col_scale(g["ebl"][:, sl], GLA_DV) * dsn
            dq_ref[:, sl] = (dqc * g["e1"][:, sl] + dqe * g["eb"][:, sl]) * (GLA_DK ** -0.5)
            dk_ref[:, sl] = dkc * g["e2"][:, sl] + dkd * g["e3"][:, sl]
            t1, t2, t3, t4 = dqc * qc, dkc * kc, dqe * qe, dkd * kd
            dbs.append(t1 - t2 + t3 - t4)
            dbms.append(_rowsum(t2 - t1))
            m1, m2, _ = _split3(dsn * sp)
            rs = (_dot_nt(ones8, m1) + _dot_nt(ones8, m2))[0:1]
            dbls.append(_rowsum(t4) + g["ebl"][:, sl] * rs)
        db = jnp.concatenate(dbs, axis=1)
        dbm = jnp.concatenate(dbms, axis=1)
        dbl = jnp.concatenate(dbls, axis=1)
        row = lax.broadcasted_iota(jnp.int32, (c, wk), 0)
        mid = jnp.where(g["first"], c // 2, c // 2 - 1)
        last = jnp.where(g["first"], c - 1, 0)
        db = db + jnp.where(row == mid, dbm, 0.0) + jnp.where(row == last, dbl, 0.0)
        d1, d2, d3 = _split3(db)
        tri = g["tri"]
        dla = _dot_tn(tri, d1) + _dot_tn(tri, d2) + _dot_tn(tri, d3)
        dz = dla * (1.0 / GLA_TAU) * (1.0 - _sigmoid(g["z"]))
        dzb = _bf(dz)
        dga_ref[...] = _dot_nt(dzb, wg_ref[...])
        dwg_ref[...] += _dot_tn(g["gab"], dzb)
        dbg_ref[...] += _rowsum(dz)

    def col(width, blk):
        return pl.BlockSpec((c, width), lambda d, n: (cidx(d, n), blk))

    def dirrow(width):
        return pl.BlockSpec((None, c, width), lambda d, n: (d, cidx(d, n), 0))

    return pl.pallas_call(
        body, name="gla_bwd", grid=(2, n_chunks),
        in_specs=[col(wk, 0), col(wk, 1), col(wv, 1), col(LANES, OD_GA_BLK),
                  pl.BlockSpec((None, LANES, wk), lambda d, n: (d, 0, 0)),
                  pl.BlockSpec((None, 1, wk), lambda d, n: (d, 0, 0)),
                  pl.BlockSpec((None, None, wk, GLA_DV), lambda d, n: (d, cidx(d, n), 0, 0)),
                  pl.BlockSpec((c, wv), lambda d, n: (cidx(d, n), 0))],
        out_specs=[dirrow(wk), dirrow(wk), dirrow(wv), dirrow(LANES),
                   pl.BlockSpec((None, LANES, wk), lambda d, n: (d, 0, 0)),
                   pl.BlockSpec((None, 1, wk), lambda d, n: (d, 0, 0))],
        out_shape=[jax.ShapeDtypeStruct((2, s, wk), F32), jax.ShapeDtypeStruct((2, s, wk), F32),
                   jax.ShapeDtypeStruct((2, s, wv), F32), jax.ShapeDtypeStruct((2, s, LANES), F32),
                   jax.ShapeDtypeStruct((2, LANES, wk), F32), jax.ShapeDtypeStruct((2, 1, wk), F32)],
        scratch_shapes=[pltpu.VMEM((wk, GLA_DV), F32)],
        compiler_params=_cp(("arbitrary", "arbitrary")),
    )(p_odd, p_odd, p_odd, p_odd, wg2, bg2, s_prev, do)


HALO = 8


def _halo_specs(width_blk, col0, ts, s):
    r = ts // HALO
    last = s // HALO - 1
    cur = pl.BlockSpec((ts, width_blk), lambda j, i: (i, col0 + j))
    prev = pl.BlockSpec((HALO, width_blk), lambda j, i: (jnp.maximum(i * r - 1, 0), col0 + j))
    nxt = pl.BlockSpec((HALO, width_blk), lambda j, i: (jnp.minimum((i + 1) * r, last), col0 + j))
    return [prev, cur, nxt]


def _with_halo(prev_ref, cur_ref, next_ref, i, n_i):
    p = jnp.where(i == 0, 0.0, prev_ref[...])
    q = jnp.where(i == n_i - 1, 0.0, next_ref[...])
    return jnp.concatenate([p, cur_ref[...], q], axis=0)


def _shift_down(x):
    return pltpu.roll(x, 1, 0)


def _shift_up(x):
    return pltpu.roll(x, x.shape[0] - 1, 0)


def _ffn_act(up, conv_w, conv_b, *, ts):
    s = up.shape[0]
    tc = _tile(D_FF, 1408)
    nj = D_FF // tc
    n_i = s // ts

    def body(gp, gc, gn, val_ref, w_ref, b_ref, a_ref):
        i = pl.program_id(1)
        g = _with_halo(gp, gc, gn, i, n_i)
        w = w_ref[...]
        conv = w[0:1] * _shift_down(g) + w[1:2] * g + w[2:3] * _shift_up(g) + b_ref[...]
        conv = conv[HALO:HALO + ts]
        a_ref[...] = (conv * _sigmoid(conv) * val_ref[...]).astype(a_ref.dtype)

    return pl.pallas_call(
        body, name="ffn_act", grid=(nj, n_i),
        in_specs=_halo_specs(tc, 0, ts, s) + [pl.BlockSpec((ts, tc), lambda j, i: (i, nj + j)),
                                              pl.BlockSpec((3, tc), lambda j, i: (0, j)),
                                              pl.BlockSpec((1, tc), lambda j, i: (0, j))],
        out_specs=pl.BlockSpec((ts, tc), lambda j, i: (i, j)),
        out_shape=jax.ShapeDtypeStruct((s, D_FF), BF16),
        compiler_params=_cp(("parallel", "arbitrary")),
    )(up, up, up, up, conv_w, conv_b)


def _ffn_act_bwd(up, da, conv_w, conv_b, *, ts):
    s = up.shape[0]
    tc = _tile(D_FF, 1408)
    nj = D_FF // tc
    n_i = s // ts

    def body(gp, gc, gn, vp, vc, vn, dp, dc, dn, w_ref, b_ref, dg_ref, dval_ref, dw_ref, db_ref):
        i = pl.program_id(1)
        g = _with_halo(gp, gc, gn, i, n_i)
        v = _with_halo(vp, vc, vn, i, n_i)
        dav = _with_halo(dp, dc, dn, i, n_i)
        w = w_ref[...]
        gm, gpl = _shift_down(g), _shift_up(g)
        conv = w[0:1] * gm + w[1:2] * g + w[2:3] * gpl + b_ref[...]
        sg = _sigmoid(conv)
        dgc = dav * v * (sg * (1.0 + conv * (1.0 - sg)))
        dgate = w[0:1] * _shift_up(dgc) + w[1:2] * dgc + w[2:3] * _shift_down(dgc)
        ctr = slice(HALO, HALO + ts)
        dg_ref[...] = dgate[ctr].astype(dg_ref.dtype)
        dval_ref[...] = (dav[ctr] * (conv * sg)[ctr]).astype(dval_ref.dtype)
        dgc_c = dgc[ctr]
        dw = jnp.concatenate([_rowsum(dgc_c * gm[ctr]), _rowsum(dgc_c * g[ctr]), _rowsum(dgc_c * gpl[ctr])], axis=0)
        dbv = _rowsum(dgc_c)

        @pl.when(i == 0)
        def _():
            dw_ref[...] = dw
            db_ref[...] = dbv

        @pl.when(i > 0)
        def _():
            dw_ref[...] += dw
            db_ref[...] += dbv

    tile = pl.BlockSpec((ts, tc), lambda j, i: (i, j))
    return pl.pallas_call(
        body, name="ffn_act_bwd", grid=(nj, n_i),
        in_specs=(_halo_specs(tc, 0, ts, s) + _halo_specs(tc, nj, ts, s) + _halo_specs(tc, 0, ts, s)
                  + [pl.BlockSpec((3, tc), lambda j, i: (0, j)), pl.BlockSpec((1, tc), lambda j, i: (0, j))]),
        out_specs=[tile, tile, pl.BlockSpec((3, tc), lambda j, i: (0, j)), pl.BlockSpec((1, tc), lambda j, i: (0, j))],
        out_shape=[jax.ShapeDtypeStruct((s, D_FF), BF16), jax.ShapeDtypeStruct((s, D_FF), BF16),
                   jax.ShapeDtypeStruct((3, D_FF), F32), jax.ShapeDtypeStruct((1, D_FF), F32)],
        compiler_params=_cp(("parallel", "arbitrary")),
    )(up, up, up, up, up, up, da, da, da, conv_w, conv_b)


def _loss_head(y, target, *, s, ts):
    def fn(yv, tv):
        err = yv - tv
        return err * (1.0 / D_MODEL), _rowsum(err * err)

    return _ew(fn, [_cols(y, D_MODEL, 0, ts), _cols(target, D_MODEL, 0, ts)], [], [(D_MODEL, F32)],
               [(1, D_MODEL)], s=s, ts=ts, name="loss_head")


def _rows_tile(r, width):
    ts = r
    while ts * width * 4 > (1 << 20) and ts % 16 == 0:
        ts //= 2
    return ts


def _adamw(w, g, m, v, *, ts, name):
    r, width = w.shape
    assert r % ts == 0

    def fn(wv, gv, mv, vv):
        mn = ADAM_B1 * mv + (1.0 - ADAM_B1) * gv
        vn = ADAM_B2 * vv + (1.0 - ADAM_B2) * (gv * gv)
        m_hat = mn / (1.0 - ADAM_B1 ** ADAM_STEP)
        v_hat = vn / (1.0 - ADAM_B2 ** ADAM_STEP)
        delta = -ADAM_LR * (m_hat / (jnp.sqrt(v_hat) + ADAM_EPS) + ADAM_WD * wv)
        return delta, mn, vn

    rows = [_cols(a, width, 0, ts) for a in (w, g, m, v)]
    return _ew(fn, rows, [], [(width, F32)] * 3, s=r, ts=ts, name=name)


def _pad_heads(w, heads, real):
    lead = w.shape[:-1]
    w = w.reshape(lead + (heads, real))
    w = jnp.pad(w, [(0, 0)] * len(lead) + [(0, 0), (0, LANES - real)])
    return w.reshape(lead + (heads * LANES,))


def _pad_head_rows(w, heads, real):
    return _pad_heads(w.T, heads, real).T


def _pack_even(p):
    w_in = p["w_in"]
    z = lambda n: jnp.zeros((D_MODEL, n), w_in.dtype)
    o = 0
    parts = {}
    for nm, n in (("cq", MLA_QR), ("ckv", MLA_KVR), ("kr", MLA_ROPE), ("rq", 512), ("rk", 512), ("rv", 512), ("rg", 512)):
        parts[nm] = w_in[:, o:o + n]
        o += n
    w_in_p = jnp.concatenate(
        [_pad_heads(parts[k], RET_H, RET_DK) for k in ("rq", "rk", "rv", "rg")]
        + [parts["cq"], z(EV_CQ - MLA_QR), parts["ckv"], z(MLA_NOPE), parts["kr"], z(LANES - MLA_QK), z(LANES)], axis=1)
    w_uq = jnp.pad(_pad_heads(p["w_uq"], MLA_H, MLA_QK), ((0, EV_CQ - MLA_QR), (0, 0)))
    ukv = p["w_ukv"].reshape(MLA_KVR, MLA_H, MLA_NOPE + MLA_V)
    w_ukv = jnp.concatenate([_pad_heads(ukv[..., :MLA_NOPE].reshape(MLA_KVR, -1), MLA_H, MLA_NOPE),
                             _pad_heads(ukv[..., MLA_NOPE:].reshape(MLA_KVR, -1), MLA_H, MLA_V)], axis=1)
    w_out = jnp.concatenate([_pad_head_rows(p["w_out"][:MLA_H * MLA_V], MLA_H, MLA_V),
                             _pad_head_rows(p["w_out"][MLA_H * MLA_V:], RET_H, RET_DV)], axis=0)
    return dict(
        w_in=w_in_p, w_uq=w_uq, w_ukv=w_ukv, w_out=w_out,
        mix_g=p["mix_norm"][None, :],
        q_norm=jnp.pad(p["q_norm"], (0, EV_CQ - MLA_QR))[None, :],
        kv_norm=p["kv_norm"][None, :],
        qhn=jnp.pad(p["q_head_norm"], (0, LANES - MLA_QK))[None, :],
        khn=jnp.pad(p["k_head_norm"], (0, LANES - MLA_QK))[None, :],
        ret_gain=_pad_heads(p["ret_out_norm"].reshape(-1), RET_H, RET_DV)[None, :],
    )


def _pack_odd(p):
    w_in = p["w_in"]
    ga = w_in[:, 3072:]
    w_in_p = jnp.concatenate([w_in[:, :3072], ga, jnp.zeros((D_MODEL, LANES - 2 * GLA_R), w_in.dtype)], axis=1)
    wk = GLA_H * GLA_DK
    zf = jnp.zeros((LANES - GLA_R, wk), p["w_gate_fwd"].dtype)
    zb0 = jnp.zeros((GLA_R, wk), p["w_gate_fwd"].dtype)
    zb1 = jnp.zeros((LANES - 2 * GLA_R, wk), p["w_gate_fwd"].dtype)
    wg2 = jnp.stack([jnp.concatenate([p["w_gate_fwd"], zf], axis=0),
                     jnp.concatenate([zb0, p["w_gate_bwd"], zb1], axis=0)])
    bg2 = jnp.stack([p["b_gate_fwd"][None, :], p["b_gate_bwd"][None, :]])
    return dict(w_in=w_in_p, wg2=wg2, bg2=bg2, w_out=p["w_out"], mix_g=p["mix_norm"][None, :],
                gla_gain=p["gla_out_norm"].reshape(1, -1))


_MATRICES = ("w_in", "w_uq", "w_ukv", "w_out", "wg2")


def _packed(pack_fn, p):
    packed = pack_fn(p)
    packed = {k: (_bf(v) if k in _MATRICES else v.astype(F32)) for k, v in packed.items()}
    shapes = {k: jax.ShapeDtypeStruct(v.shape, F32) for k, v in p.items()}
    unpack = jax.linear_transpose(pack_fn, shapes)
    return packed, lambda g: unpack(g)[0]


def _ffn_fwd(x, w, *, s, ts):
    h = _rmsnorm(_cols(x, D_MODEL, 0, ts), w["norm_g"], n=D_MODEL, s=s, ts=ts, name="ffn_norm")
    up = _mm(h, w["w_up4"], b_layer=w["layer"], name="ffn_up")
    a = _ffn_act(up, w["conv_w"], w["conv_b"], ts=ts)
    y = _mm(a, w["w_down"], res=x, name="ffn_down")
    return y, dict(x=x, h=h, up=up, a=a)


def _ffn_bwd(dy, w, sv, *, s, ts):
    da = _mm(dy, w["w_down"], tb=True, name="ffn_down_dx")
    g_down = _mm(sv["a"], dy, ta=True, name="ffn_down_dw")
    dgate, dval, g_cw, g_cb = _ffn_act_bwd(sv["up"], da, w["conv_w"], w["conv_b"], ts=ts)
    dup = jnp.concatenate([dgate, dval], axis=1)
    dh = _mm(dup, w["w_up4"], tb=True, b_layer=w["layer"], name="ffn_up_dx")
    g_up = _mm(sv["h"], dup, ta=True, out_chips=True, name="ffn_up_dw")
    dx, g_norm = _rmsnorm_bwd(_cols(sv["x"], D_MODEL, 0, ts), w["norm_g"], dh, dy, n=D_MODEL, s=s, ts=ts,
                              name="ffn_norm_bwd")
    return dx, dict(w_up=g_up, w_down=g_down, conv_w=g_cw, conv_b=g_cb, norm_g=g_norm)


def _flash_tiles(s):
    return min(s, 512), min(s, 1024)


def _even_fwd(x, w, tabs, *, s, ts, side=()):
    cos_m, sin_m, cos_r, sin_r = tabs
    h = _rmsnorm(_cols(x, D_MODEL, 0, ts), w["mix_g"], n=D_MODEL, s=s, ts=ts, name="mix_norm")
    p = _mm(h, w["w_in"], name="even_in")
    cqn = _rmsnorm(_cols(p, EV_CQ, EV_RET // EV_CQ, ts), w["q_norm"], n=MLA_QR, s=s, ts=ts, name="mla_q_norm")
    ckvn = _rmsnorm(_cols(p, MLA_KVR, (EV_RET + EV_CQ) // MLA_KVR, ts), w["kv_norm"], n=MLA_KVR, s=s, ts=ts,
                    name="mla_kv_norm")
    q_pre = _mm(cqn, w["w_uq"], name="mla_uq")
    kv_pre = _mm(ckvn, w["w_ukv"], name="mla_ukv")
    q, k, v = _mla_prep(q_pre, kv_pre, p, cos_m, sin_m, w["qhn"], w["khn"], s=s, ts=ts)
    tq, tk = _flash_tiles(s)
    o, lse, gathered = _flash_fwd(q, k, v, tq=tq, tk=tk, side=side)
    o2, r_prev = _ret_fwd(p, cos_r, sin_r, w["theta_l"])
    r = _post_fwd(o2, _cols(p, RET_H * LANES, 3, ts), w["ret_gain"], group=LANES, n=RET_DV, s=s, ts=ts,
                  name="ret_post")
    ar = jnp.concatenate([o, r], axis=1)
    y = _mm(ar, w["w_out"], res=x, name="even_out")
    return y, dict(x=x, h=h, p=p, cqn=cqn, ckvn=ckvn, q_pre=q_pre, kv_pre=kv_pre, q=q, k=k, v=v, o=o, lse=lse,
                   o2=o2, r_prev=r_prev, ar=ar), gathered


def _even_bwd(dy, w, sv, tabs, *, s, ts):
    cos_m, sin_m, cos_r, sin_r = tabs
    p = sv["p"]
    wh = MLA_H * LANES
    dar = _mm(dy, w["w_out"], tb=True, name="even_out_dx")
    g_out = _mm(sv["ar"], dy, ta=True, name="even_out_dw")
    tq, tk = _flash_tiles(s)
    do_attn = _attn_bwd_prep(dar, sv["o"], s=s, ts=ts)
    dq, dk, dv = _flash_bwd(sv["q"], sv["k"], sv["v"], do_attn, sv["lse"], tq=tq, tk=tk)
    dq_pre, dk_pre, dkr, g_qhn, g_khn = _mla_prep_bwd(sv["q_pre"], sv["kv_pre"], p, cos_m, sin_m, w["qhn"], w["khn"],
                                                      dq, dk, s=s, ts=ts)
    dkv_pre = jnp.concatenate([dk_pre, dv], axis=1)
    dckvn = _mm(dkv_pre, w["w_ukv"], tb=True, name="mla_ukv_dx")
    g_ukv = _mm(sv["ckvn"], dkv_pre, ta=True, name="mla_ukv_dw")
    dcqn = _mm(dq_pre, w["w_uq"], tb=True, name="mla_uq_dx")
    g_uq = _mm(sv["cqn"], dq_pre, ta=True, name="mla_uq_dw")
    dckv, g_kvn = _rmsnorm_bwd(_cols(p, MLA_KVR, (EV_RET + EV_CQ) // MLA_KVR, ts), w["kv_norm"], dckvn, None,
                               n=MLA_KVR, s=s, ts=ts, name="mla_kv_norm_bwd")
    dcq, g_qn = _rmsnorm_bwd(_cols(p, EV_CQ, EV_RET // EV_CQ, ts), w["q_norm"], dcqn, None, n=MLA_QR, s=s, ts=ts,
                             name="mla_q_norm_bwd")
    do, drg, g_gain = _post_bwd(sv["o2"], _cols(p, wh, 3, ts), w["ret_gain"], _cols(dar, wh, 1, ts),
                                group=LANES, n=RET_DV, s=s, ts=ts, name="ret_post_bwd")
    dq2, dk2, dv2, dth = _ret_bwd(p, cos_r, sin_r, w["theta_l"], w["theta_h"], sv["r_prev"], do)
    drq, drk, drv = (_sum2(a, s=s, ts=ts, name="sum_dirs_1024") for a in (dq2, dk2, dv2))
    dp = jnp.concatenate([drq, drk, drv, drg, _bf(dcq), _bf(dckv), dkr, jnp.zeros((s, LANES), BF16)], axis=1)
    dh = _mm(dp, w["w_in"], tb=True, name="even_in_dx")
    g_in = _mm(sv["h"], dp, ta=True, name="even_in_dw")
    dx, g_mix = _rmsnorm_bwd(_cols(sv["x"], D_MODEL, 0, ts), w["mix_g"], dh, dy, n=D_MODEL, s=s, ts=ts,
                             name="mix_norm_bwd")
    grads = dict(w_in=g_in, w_uq=g_uq, w_ukv=g_ukv, w_out=g_out, mix_g=g_mix, q_norm=g_qn, kv_norm=g_kvn,
                 qhn=g_qhn, khn=g_khn, ret_gain=g_gain)
    return dx, grads, dth[:, :, 0]


def _odd_fwd(x, w, *, s, ts):
    h = _rmsnorm(_cols(x, D_MODEL, 0, ts), w["mix_g"], n=D_MODEL, s=s, ts=ts, name="mix_norm")
    p = _mm(h, w["w_in"], name="odd_in")
    o2, s_prev = _gla_fwd(p, w["wg2"], w["bg2"])
    g = _post_fwd(o2, _cols(p, GLA_H * GLA_DV, 2, ts), w["gla_gain"], group=GLA_DV, n=GLA_DV, s=s, ts=ts,
                  name="gla_post")
    y = _mm(g, w["w_out"], res=x, name="odd_out")
    return y, dict(x=x, h=h, p=p, o2=o2, s_prev=s_prev, g=g)


def _odd_bwd(dy, w, sv, *, s, ts):
    p = sv["p"]
    wv = GLA_H * GLA_DV
    dg = _mm(dy, w["w_out"], tb=True, name="odd_out_dx")
    g_out = _mm(sv["g"], dy, ta=True, name="odd_out_dw")
    do, dgr, g_gain = _post_bwd(sv["o2"], _cols(p, wv, 2, ts), w["gla_gain"], _cols(dg, wv, 0, ts),
                                group=GLA_DV, n=GLA_DV, s=s, ts=ts, name="gla_post_bwd")
    dq2, dk2, dv2, dga2, g_wg, g_bg = _gla_bwd(p, w["wg2"], w["bg2"], sv["s_prev"], do)
    dq = _sum2(dq2, s=s, ts=ts, name="sum_dirs_512")
    dk = _sum2(dk2, s=s, ts=ts, name="sum_dirs_512")
    dv = _sum2(dv2, s=s, ts=ts, name="sum_dirs_1024")
    dga = _sum2(dga2, s=s, ts=ts, name="sum_dirs_128")
    dp = jnp.concatenate([dq, dk, dv, dgr, dga], axis=1)
    dh = _mm(dp, w["w_in"], tb=True, name="odd_in_dx")
    g_in = _mm(sv["h"], dp, ta=True, name="odd_in_dw")
    dx, g_mix = _rmsnorm_bwd(_cols(sv["x"], D_MODEL, 0, ts), w["mix_g"], dh, dy, n=D_MODEL, s=s, ts=ts,
                             name="mix_norm_bwd")
    return dx, dict(w_in=g_in, wg2=g_wg, bg2=g_bg, w_out=g_out, mix_g=g_mix, gla_gain=g_gain)


_EVEN_NAMES = dict(mix_norm="mix_norm_even", w_in="w_in_even", q_norm="mla_q_norm", kv_norm="mla_kv_norm",
                   w_uq="mla_w_uq", w_ukv="mla_w_ukv", q_head_norm="mla_q_head_norm", k_head_norm="mla_k_head_norm",
                   ret_out_norm="ret_out_norm", w_out="w_out_even")
_ODD_NAMES = dict(mix_norm="mix_norm_odd", w_in="w_in_odd", w_gate_fwd="gla_w_gate_fwd", b_gate_fwd="gla_b_gate_fwd",
                  w_gate_bwd="gla_w_gate_bwd", b_gate_bwd="gla_b_gate_bwd", gla_out_norm="gla_out_norm",
                  w_out="w_out_odd")

def _local_step(x, pos, target, full, side=(), finish=None):
    s = x.shape[0]
    ts = min(s, 256)
    tabs = _rope_tables(pos, MLA_ROPE, MLA_NOPE) + _rope_tables(pos, RET_DK, 0)

    def layer_weights(layer):
        i = layer // 2
        names = _EVEN_NAMES if layer % 2 == 0 else _ODD_NAMES
        wm, unpack_m = _packed(_pack_even if layer % 2 == 0 else _pack_odd, {k: full[n][i] for k, n in names.items()})
        if layer % 2 == 0:
            th = jnp.stack([full["ret_theta_fwd"][i], full["ret_theta_bwd"][i]]).astype(F32)
            wm["theta_h"] = jnp.broadcast_to(th[:, :, None], (2, RET_H, LANES))
            wm["theta_l"] = wm["theta_h"].reshape(2, 1, RET_H * LANES)
        w_up4, index = full["ffn_w_up"][layer]
        wf = dict(layer=index, w_up4=w_up4, w_down=_bf(full["ffn_w_down"][layer]),
                  conv_w=full["ffn_conv_w"][layer].astype(F32), conv_b=full["ffn_conv_b"][layer][None, :].astype(F32),
                  norm_g=full["ffn_norm"][layer][None, :].astype(F32))
        return wm, unpack_m, wf

    layers, saved = [], []
    for layer in range(DEPTH):
        layers.append(layer_weights(layer))
        wm, _, wf = layers[-1]
        if layer % 2 == 0:
            x, sv_m, gathered = _even_fwd(x, wm, tabs, s=s, ts=ts, side=side if layer == 0 else ())
            if layer == 0 and finish is not None:
                full = finish(gathered)
        else:
            x, sv_m = _odd_fwd(x, wm, s=s, ts=ts)
        x, sv_f = _ffn_fwd(x, wf, s=s, ts=ts)
        saved.append((sv_m, sv_f))

    dy, sq = _loss_head(x, target, s=s, ts=ts)
    loss = 0.5 / D_MODEL * jnp.sum(sq)

    grads = {}

    def put(name, idx, g):
        grads.setdefault(name, {})[idx] = g

    for layer in reversed(range(DEPTH)):
        wm, unpack_m, wf = layers[layer]
        sv_m, sv_f = saved[layer]
        i = layer // 2
        dy, gf = _ffn_bwd(dy, wf, sv_f, s=s, ts=ts)
        put("ffn_w_up", layer, gf["w_up"])
        put("ffn_w_down", layer, gf["w_down"])
        put("ffn_conv_w", layer, gf["conv_w"])
        put("ffn_conv_b", layer, gf["conv_b"][0])
        put("ffn_norm", layer, gf["norm_g"][0])
        if layer % 2 == 0:
            dy, gm, dth = _even_bwd(dy, wm, sv_m, tabs, s=s, ts=ts)
            put("ret_theta_fwd", i, dth[0])
            put("ret_theta_bwd", i, dth[1])
            names = _EVEN_NAMES
        else:
            dy, gm = _odd_bwd(dy, wm, sv_m, s=s, ts=ts)
            names = _ODD_NAMES
        for k, g in unpack_m(gm).items():
            put(names[k], i, g)
    return loss, dy, {n: [g[j] for j in range(len(g))] for n, g in grads.items()}


HBM_SPEC = pl.BlockSpec(memory_space=pltpu.HBM)
VMEM_SPEC = pl.BlockSpec(memory_space=pltpu.VMEM)
CHIPS = 4
CORES = 2
ROW = 8 * LANES


def _xyc():
    return lax.axis_index("x"), lax.axis_index("y"), lax.axis_index("c")


def _other_chips(x, y):
    return [(1 - x, y), (x, 1 - y), (1 - x, 1 - y)]


def _remote(src, dst, send, recv, dev):
    return pltpu.make_async_remote_copy(src_ref=src, dst_ref=dst, send_sem=send, recv_sem=recv,
                                        device_id=dev, device_id_type=MESH)


def _sems(n):
    return pltpu.SemaphoreType.DMA((n,))


def _gather_copies(side, srcs, lands, send, recv, loc):
    n = len(side)
    x, y, c = _xyc()
    me = 2 * x + y
    local, sends, arrivals = [], [], []
    for t, (_, first, count) in enumerate(side):
        src = srcs[t].at[pl.ds(first, count)]
        local.append(pltpu.make_async_copy(src, lands[t].at[me], loc.at[t]))
        for j, (px, py) in enumerate(_other_chips(x, y)):
            k = n * j + t
            sends.append(_remote(src, lands[t].at[me], send.at[k], recv.at[k], (px, py, c)))
            arrivals.append(_remote(src, lands[t].at[2 * px + py], send.at[k], recv.at[k], (px, py, c)))
    return local, sends, arrivals


def _gather_shapes(side):
    return [jax.ShapeDtypeStruct((CHIPS, count) + a.shape[1:], a.dtype) for a, _, count in side]


def _gather_chips(side):
    n = len(side)

    def body(*refs):
        local, sends, arrivals = _gather_copies(side, refs[:n], refs[n:2 * n], *refs[2 * n:])
        for cp in local + sends:
            cp.start()
        for cp in arrivals:
            cp.wait_recv()
        for cp in sends:
            cp.wait_send()
        for cp in local:
            cp.wait()

    return pl.pallas_call(
        body, name="gather_chips", in_specs=[HBM_SPEC] * n, out_specs=[HBM_SPEC] * n,
        out_shape=_gather_shapes(side),
        scratch_shapes=[_sems(3 * n), _sems(3 * n), _sems(n)],
    )(*[a for a, _, _ in side])


def _half_rows(ref, axis, half, which):
    idx = (slice(None),) * axis + (pl.ds(pl.multiple_of(which * half, 8), half),)
    return ref.at[idx]


def _swap_halves(arrs):
    n = len(arrs)

    def body(*refs):
        ins, outs = refs[:n], refs[n:2 * n]
        send, recv = refs[2 * n:]
        x, y, c = _xyc()
        copies = []
        for t in range(n):
            half = arrs[t].shape[2] // CORES
            cp = _remote(_half_rows(ins[t], 2, half, 1 - c), outs[t], send.at[t], recv.at[t], (x, y, 1 - c))
            cp.start()
            copies.append(cp)
        for cp in copies:
            cp.wait()

    return pl.pallas_call(
        body, name="swap_halves", in_specs=[HBM_SPEC] * n, out_specs=[HBM_SPEC] * n,
        out_shape=[jax.ShapeDtypeStruct(a.shape[:2] + (a.shape[2] // CORES, a.shape[3]), a.dtype) for a in arrs],
        scratch_shapes=[_sems(n), _sems(n)],
    )(*arrs)


def _add_core_halves(a, got, core, *, ts, name):
    ch, nl, r, cols = a.shape
    half = r // CORES
    nb = half // ts

    def body(core_ref, a_ref, g_ref, o_ref):
        o_ref[...] = (a_ref[...] + g_ref[...]).astype(o_ref.dtype)

    rows = pl.BlockSpec((ts, cols), lambda g, i, cr: (g * nb + i, 0))
    return pl.pallas_call(
        body, name=name, out_shape=jax.ShapeDtypeStruct((ch * nl * half, cols), BF16),
        grid_spec=pltpu.PrefetchScalarGridSpec(
            num_scalar_prefetch=1, grid=(ch * nl, nb),
            in_specs=[pl.BlockSpec((ts, cols), lambda g, i, cr: (g * (r // ts) + cr[0] * nb + i, 0)), rows],
            out_specs=rows),
        compiler_params=_cp(("arbitrary", "arbitrary")),
    )(core, a.reshape(-1, cols), got.reshape(-1, cols)).reshape(got.shape)


def _add_chip_parts(parts, core, *, ts, name):
    ch, nl, half, cols = parts.shape
    nb = half // ts
    r = half * CORES

    def body(core_ref, *refs):
        acc = refs[0][...].astype(F32)
        for p in refs[1:ch]:
            acc = acc + p[...].astype(F32)
        refs[ch][...] = acc

    return pl.pallas_call(
        body, name=name, out_shape=jax.ShapeDtypeStruct((nl * r, cols), F32),
        grid_spec=pltpu.PrefetchScalarGridSpec(
            num_scalar_prefetch=1, grid=(nl, nb),
            in_specs=[pl.BlockSpec((ts, cols), lambda l, i, cr, j=j: ((j * nl + l) * nb + i, 0)) for j in range(ch)],
            out_specs=pl.BlockSpec((ts, cols), lambda l, i, cr: (l * (r // ts) + cr[0] * nb + i, 0))),
        compiler_params=_cp(("arbitrary", "arbitrary")),
    )(core, *[parts.reshape(-1, cols)] * ch).reshape(nl, r, cols)


def _scatter_chips(arrs):
    n = len(arrs)

    def body(*refs):
        ins, outs = refs[:n], refs[n:2 * n]
        send, recv, loc = refs[2 * n:]
        x, y, c = _xyc()
        me = 2 * x + y
        copies = []
        for t in range(n):
            cp = pltpu.make_async_copy(ins[t].at[me], outs[t].at[me], loc.at[t])
            cp.start()
            copies.append(cp)
        sends = []
        for j, (px, py) in enumerate(_other_chips(x, y)):
            for t in range(n):
                cp = _remote(ins[t].at[2 * px + py], outs[t].at[me], send.at[n * j + t], recv.at[n * j + t], (px, py, c))
                cp.start()
                sends.append(cp)
        for j, (px, py) in enumerate(_other_chips(x, y)):
            for t in range(n):
                _remote(ins[t].at[me], outs[t].at[2 * px + py], send.at[n * j + t], recv.at[n * j + t],
                        (px, py, c)).wait_recv()
        for cp in sends:
            cp.wait_send()
        for cp in copies:
            cp.wait()

    return pl.pallas_call(
        body, name="scatter_chips", in_specs=[HBM_SPEC] * n, out_specs=[HBM_SPEC] * n,
        out_shape=[jax.ShapeDtypeStruct(a.shape, a.dtype) for a in arrs],
        scratch_shapes=[_sems(3 * n), _sems(3 * n), _sems(n)],
    )(*arrs)


def _gather_cores(arrs):
    n = len(arrs)

    def body(*refs):
        ins, outs = refs[:n], refs[n:2 * n]
        send, recv = refs[2 * n:]
        x, y, c = _xyc()
        sends = []
        for t in range(n):
            half = arrs[t].shape[1] // CORES
            cp = _remote(_half_rows(ins[t], 1, half, c), _half_rows(outs[t], 1, half, c), send.at[t], recv.at[t],
                         (x, y, 1 - c))
            cp.start()
            sends.append(cp)
        for t in range(n):
            half = arrs[t].shape[1] // CORES
            _remote(_half_rows(ins[t], 1, half, 1 - c), _half_rows(outs[t], 1, half, 1 - c), send.at[t], recv.at[t],
                    (x, y, 1 - c)).wait_recv()
        for cp in sends:
            cp.wait_send()

    return pl.pallas_call(
        body, name="gather_cores", in_specs=[HBM_SPEC] * n, out_specs=[HBM_SPEC] * n,
        out_shape=[jax.ShapeDtypeStruct(a.shape, a.dtype) for a in arrs],
        input_output_aliases={t: t for t in range(n)},
        scratch_shapes=[_sems(n), _sems(n)],
    )(*arrs)


def _all_reduce_devices(v):
    n_dev = CHIPS * CORES

    def body(v_ref, o_ref, buf, send, recv):
        x, y, c = _xyc()
        me = 4 * x + 2 * y + c
        buf[pl.ds(me, 1)] = v_ref[...][None]
        sends = []
        for m in range(1, n_dev):
            px = 1 - x if m & 4 else x
            py = 1 - y if m & 2 else y
            pc = 1 - c if m & 1 else c
            cp = _remote(v_ref, buf.at[me], send.at[m - 1], recv.at[m - 1], (px, py, pc))
            cp.start()
            sends.append((cp, 4 * px + 2 * py + pc))
        for m, (cp, peer) in enumerate(sends):
            _remote(v_ref, buf.at[peer], send.at[m], recv.at[m], (x, y, c)).wait_recv()
        for cp, _ in sends:
            cp.wait_send()
        acc = buf[0]
        for k in range(1, n_dev):
            acc = acc + buf[k]
        o_ref[...] = acc

    return pl.pallas_call(
        body, name="all_reduce_devices", in_specs=[VMEM_SPEC], out_specs=VMEM_SPEC,
        out_shape=jax.ShapeDtypeStruct(v.shape, F32),
        scratch_shapes=[pltpu.VMEM((n_dev,) + v.shape, F32), pltpu.SemaphoreType.DMA((n_dev - 1,)),
                        pltpu.SemaphoreType.DMA((n_dev - 1,))],
    )(v)


_SHARDED = (("w_in_even", 2), ("mla_w_uq", 2), ("mla_w_ukv", 2), ("w_out_even", 1), ("w_in_odd", 2), ("w_out_odd", 1),
            ("ffn_w_up", 2), ("ffn_w_down", 1),
            ("mix_norm_odd", 1), ("gla_w_gate_fwd", 2), ("gla_b_gate_fwd", 1), ("gla_w_gate_bwd", 2),
            ("gla_b_gate_bwd", 1), ("gla_out_norm", 2), ("ffn_conv_w", 2))
_N_MATRICES = 8
_REPLICATED = ("mix_norm_even", "mla_q_norm", "mla_kv_norm", "mla_q_head_norm", "mla_k_head_norm", "ret_theta_fwd",
               "ret_theta_bwd", "ret_out_norm", "ffn_norm", "ffn_conv_b")
_WEIGHTS = ("mix_norm_even", "w_in_even", "mla_q_norm", "mla_kv_norm", "mla_w_uq", "mla_w_ukv", "mla_q_head_norm",
            "mla_k_head_norm", "ret_theta_fwd", "ret_theta_bwd", "ret_out_norm", "w_out_even", "mix_norm_odd",
            "w_in_odd", "gla_w_gate_fwd", "gla_b_gate_fwd", "gla_w_gate_bwd", "gla_b_gate_bwd", "gla_out_norm",
            "w_out_odd", "ffn_norm", "ffn_w_up", "ffn_conv_w", "ffn_conv_b", "ffn_w_down")


def _flatten(arrs, row_multiple, dtype):
    flat = jnp.concatenate([a.reshape(-1).astype(dtype) for a in arrs])
    per = ROW * row_multiple
    total = -(-flat.shape[0] // per) * per
    return jnp.pad(flat, (0, total - flat.shape[0])).reshape(-1, ROW)


def _unflatten(flat, shapes):
    flat = flat.reshape(-1)
    out, o = [], 0
    for shp in shapes:
        n = math.prod(shp)
        out.append(flat[o:o + n].reshape(shp))
        o += n
    return out


def kernel(x, positions, mix_norm_even, w_in_even, mla_q_norm, mla_kv_norm, mla_w_uq, mla_w_ukv, mla_q_head_norm, mla_k_head_norm, ret_theta_fwd, ret_theta_bwd, ret_out_norm, w_out_even, mix_norm_odd, w_in_odd, gla_w_gate_fwd, gla_b_gate_fwd, gla_w_gate_bwd, gla_b_gate_bwd, gla_out_norm, w_out_odd, ffn_norm, ffn_w_up, ffn_conv_w, ffn_conv_b, ffn_w_down, loss_target, m_mix_norm_even, m_w_in_even, m_mla_q_norm, m_mla_kv_norm, m_mla_w_uq, m_mla_w_ukv, m_mla_q_head_norm, m_mla_k_head_norm, m_ret_theta_fwd, m_ret_theta_bwd, m_ret_out_norm, m_w_out_even, m_mix_norm_odd, m_w_in_odd, m_gla_w_gate_fwd, m_gla_b_gate_fwd, m_gla_w_gate_bwd, m_gla_b_gate_bwd, m_gla_out_norm, m_w_out_odd, m_ffn_norm, m_ffn_w_up, m_ffn_conv_w, m_ffn_conv_b, m_ffn_w_down, v_mix_norm_even, v_w_in_even, v_mla_q_norm, v_mla_kv_norm, v_mla_w_uq, v_mla_w_ukv, v_mla_q_head_norm, v_mla_k_head_norm, v_ret_theta_fwd, v_ret_theta_bwd, v_ret_out_norm, v_w_out_even, v_mix_norm_odd, v_w_in_odd, v_gla_w_gate_fwd, v_gla_b_gate_fwd, v_gla_w_gate_bwd, v_gla_b_gate_bwd, v_gla_out_norm, v_w_out_odd, v_ffn_norm, v_ffn_w_up, v_ffn_conv_w, v_ffn_conv_b, v_ffn_w_down):
    args = dict(locals())
    x2, pos, target = args["x"][0], args["positions"][0], args["loss_target"][0]
    axis = dict(_SHARDED)
    mats = [n for n, _ in _SHARDED[:_N_MATRICES]]
    smalls = [n for n, _ in _SHARDED[_N_MATRICES:]]
    small_shapes = [args[n].shape for n in smalls]

    local = {n: _bf(args[n]) for n in mats}
    first_layers = {n: (0, 0 if n.endswith("_odd") else 1) for n in mats}
    now = [(local[n],) + first_layers[n] for n in mats if first_layers[n][1]]
    later = [(local[n], first_layers[n][1], args[n].shape[0] - first_layers[n][1]) for n in mats]
    small_block = _flatten([args[n] for n in smalls], 2 * HALO, F32)
    got_now = _gather_chips(now + [(small_block, 0, small_block.shape[0])])
    per_chip = [_unflatten(got_now[-1][j], small_shapes) for j in range(CHIPS)]
    base = {n: args[n] for n in _REPLICATED}
    for k, n in enumerate(smalls):
        base[n] = jnp.concatenate([per_chip[j][k] for j in range(CHIPS)], axis=axis[n])

    def whole(stacks):
        full = dict(base)
        for n, per_layer in stacks.items():
            if n == "ffn_w_up":
                full[n] = per_layer
            else:
                full[n] = [None if st is None else jnp.concatenate([st[j, l] for j in range(CHIPS)], axis=axis[n] - 1)
                           for st, l in per_layer]
        return full

    stacks = {n: [(None, 0)] * args[n].shape[0] for n in mats}
    for (a, first, count), st in zip(now, got_now):
        n = next(m for m in mats if local[m] is a)
        stacks[n] = [(st, l) for l in range(count)] + stacks[n][count:]

    def finish(got_later):
        for (a, first, count), st in zip(later, got_later):
            n = next(m for m in mats if local[m] is a)
            stacks[n] = stacks[n][:first] + [(st, l) for l in range(count)]
        return whole(stacks)

    loss, grad_x, grads = _local_step(x2, pos, target, whole(stacks), side=later, finish=finish)
    loss = lax.psum(loss, ("x", "y", "c"))

    def by_chip(n, g):
        if n == "ffn_w_up":
            return g
        if axis[n] == 1:
            return g.reshape((CHIPS, g.shape[0] // CHIPS) + g.shape[1:])
        return jnp.stack(jnp.split(g, CHIPS, axis=axis[n] - 1))

    core = lax.axis_index("c").astype(jnp.int32).reshape(1)
    stacked = [jnp.stack([by_chip(n, g) for g in grads[n]], axis=1) for n in mats]
    small_parts = [jnp.split(jnp.stack(grads[n]), CHIPS, axis=axis[n]) for n in smalls]
    stacked.append(jnp.stack([_flatten([p[j] for p in small_parts], 2 * HALO, F32) for j in range(CHIPS)])[:, None])
    names = mats + ["small"]
    tiles = [_rows_tile(a.shape[2] // CORES, a.shape[3]) for a in stacked]
    got = _swap_halves(stacked)
    chip_sums = [_add_core_halves(a, b, core, ts=ts, name="add_core_halves_" + n)
                 for n, a, b, ts in zip(names, stacked, got, tiles)]
    parts = _scatter_chips(chip_sums)
    sums = [_add_chip_parts(p, core, ts=ts, name="add_chip_parts_" + n) for n, p, ts in zip(names, parts, tiles)]
    reduced = _gather_cores(sums)

    res = {}

    def update(n, w, g, m, v, ts):
        cols = g.shape[-1]
        outs = _adamw(w.reshape(-1, cols), g.reshape(-1, cols), m.reshape(-1, cols), v.reshape(-1, cols), ts=ts,
                      name="adamw_" + n)
        return [g] + [o.reshape(g.shape) for o in outs]

    kinds = ("grad", "delta", "new_m", "new_v")
    for n, g, ts in zip(mats, reduced, tiles):
        for kind, a in zip(kinds, update(n, args[n], g, args["m_" + n], args["v_" + n], ts)):
            res[kind + "_" + n] = a
    w_s, m_s, v_s = (_flatten([args[pre + n] for n in smalls], 2 * HALO, F32) for pre in ("", "m_", "v_"))
    for kind, flat in zip(kinds, update("small", w_s, reduced[-1][0], m_s, v_s, tiles[-1])):
        for n, a in zip(smalls, _unflatten(flat, small_shapes)):
            res[kind + "_" + n] = a

    rep_shapes = [args[n].shape for n in _REPLICATED]
    g_rep = _all_reduce_devices(_flatten([jnp.stack(grads[n]) for n in _REPLICATED], HALO, F32))
    w_rep, m_rep, v_rep = (_flatten([args[pre + n] for n in _REPLICATED], HALO, F32) for pre in ("", "m_", "v_"))
    for kind, flat in zip(kinds, update("replicated", w_rep, g_rep, m_rep, v_rep, g_rep.shape[0])):
        for n, a in zip(_REPLICATED, _unflatten(flat, rep_shapes)):
            res[kind + "_" + n] = a

    outs = [loss, grad_x[None]]
    for kind in ("grad", "delta", "new_m", "new_v"):
        outs += [res[kind + "_" + n] for n in _WEIGHTS]
    return tuple(outs)
```

```python
import math

import jax
import jax.numpy as jnp
from jax import lax
from jax.experimental import pallas as pl
from jax.experimental.pallas import tpu as pltpu

F32 = jnp.float32
BF16 = jnp.bfloat16
MESH = pl.DeviceIdType.MESH

EPS = 1e-6
D_MODEL = 1024
DEPTH = 4
LANES = 128
MLA_H, MLA_QR, MLA_KVR, MLA_NOPE, MLA_ROPE, MLA_V = 8, 384, 256, 64, 32, 64
MLA_QK = MLA_NOPE + MLA_ROPE
MLA_SCALE = MLA_QK ** -0.5
RET_H, RET_DK, RET_DV, RET_C = 8, 64, 64, 128
GLA_H, GLA_DK, GLA_DV, GLA_R, GLA_TAU, GLA_C = 4, 128, 256, 16, 16.0, 64
D_FF = 2816
ROPE_THETA = 10000.0
LN2 = math.log(2.0)
ADAM_LR, ADAM_B1, ADAM_B2, ADAM_EPS, ADAM_WD, ADAM_STEP = 0.001, 0.9, 0.999, 1e-08, 0.01, 10

EV_RET = 4 * RET_H * LANES
EV_CQ = 512
EV_W = 5120
EV_KR_BLK = (EV_RET + EV_CQ + MLA_KVR) // LANES
OD_W = 3200
OD_GA_BLK = 3072 // LANES

VMEM_LIMIT = 56 * 1024 * 1024
MM_TILE_CAP = 1408
V_ONES = (MLA_V, MLA_V + 1)
FLASH_FWD_ROWS = 1024
FLASH_BWD_ROWS = 512
FLASH_KEYS = 1024


def _cp(sem):
    return pltpu.CompilerParams(dimension_semantics=sem, vmem_limit_bytes=VMEM_LIMIT)


def _dot(a, b):
    return jnp.dot(a, b, preferred_element_type=F32)


def _dot_nt(a, b):
    return lax.dot_general(a, b, (((1,), (1,)), ((), ())), preferred_element_type=F32)


def _dot_tn(a, b):
    return lax.dot_general(a, b, (((0,), (0,)), ((), ())), preferred_element_type=F32)


def _bf(x):
    return x.astype(BF16)


def _split3(x):
    h1 = _bf(x)
    r1 = x - h1.astype(F32)
    h2 = _bf(r1)
    h3 = _bf(r1 - h2.astype(F32))
    return h1, h2, h3


def _tile(n, cap):
    if n <= cap:
        return n
    best = None
    for t in range(LANES, cap + 1, LANES):
        if n % t == 0:
            best = t
    assert best is not None, n
    return best


def _mm(a, b, *, ta=False, tb=False, res=None, out_dtype=F32, b_layer=None, out_chips=False, name):
    assert not (ta and tb)
    if ta:
        kdim, m = a.shape
    else:
        m, kdim = a.shape
    if b_layer is not None:
        rows_b, cols_b = b.shape[2], b.shape[0] * b.shape[3]
    else:
        rows_b, cols_b = b.shape
    n, kb = (rows_b, cols_b) if tb else (cols_b, rows_b)
    assert kb == kdim, (a.shape, b.shape, ta, tb)
    tm, tn, tk = _tile(m, MM_TILE_CAP), _tile(n, MM_TILE_CAP), _tile(kdim, MM_TILE_CAP)
    nk = kdim // tk
    has_res = res is not None
    vmem = (2 * tm * tk * a.dtype.itemsize + 2 * tk * tn * b.dtype.itemsize
            + 2 * tm * tn * jnp.dtype(out_dtype).itemsize + (2 * tm * tn * 4 if has_res else 0)
            + (tm * tn * 4 if nk > 1 else 0))
    assert vmem <= VMEM_LIMIT - 8 * 1024 * 1024, (name, vmem)
    a_spec = (pl.BlockSpec((tk, tm), lambda i, j, k: (k, i)) if ta
              else pl.BlockSpec((tm, tk), lambda i, j, k: (i, k)))
    if b_layer is not None:
        per_chip = b.shape[3]
        if tb:
            assert tk == per_chip
            b_spec = pl.BlockSpec((None, None, tn, tk), lambda i, j, k: (k, b_layer, j, 0))
        else:
            assert tn == per_chip
            b_spec = pl.BlockSpec((None, None, tk, tn), lambda i, j, k: (j, b_layer, k, 0))
    else:
        b_spec = (pl.BlockSpec((tn, tk), lambda i, j, k: (j, k)) if tb
                  else pl.BlockSpec((tk, tn), lambda i, j, k: (k, j)))
    if out_chips:
        assert n // tn == CHIPS and not has_res
        o_spec = pl.BlockSpec((None, tm, tn), lambda i, j, k: (j, i, 0))
        out_struct = jax.ShapeDtypeStruct((CHIPS, m, tn), out_dtype)
    else:
        o_spec = pl.BlockSpec((tm, tn), lambda i, j, k: (i, j))
        out_struct = jax.ShapeDtypeStruct((m, n), out_dtype)

    def product(a_ref, b_ref):
        av, bv = _bf(a_ref[...]), _bf(b_ref[...])
        if ta:
            return _dot_tn(av, bv)
        if tb:
            return _dot_nt(av, bv)
        return _dot(av, bv)

    def body(*refs):
        a_ref, b_ref = refs[:2]
        r_ref = refs[2] if has_res else None
        o_ref = refs[3] if has_res else refs[2]

        def finish(r):
            if has_res:
                r = r + r_ref[...]
            o_ref[...] = r.astype(o_ref.dtype)

        if nk == 1:
            finish(product(a_ref, b_ref))
            return
        acc = refs[-1]
        k = pl.program_id(2)

        @pl.when(k == 0)
        def _():
            acc[...] = product(a_ref, b_ref)

        @pl.when(k > 0)
        def _():
            acc[...] += product(a_ref, b_ref)

        @pl.when(k == nk - 1)
        def _():
            finish(acc[...])

    ins = [a, b] + ([res] if has_res else [])
    in_specs = [a_spec, b_spec] + ([o_spec] if has_res else [])
    return pl.pallas_call(
        body, name=name, grid=(m // tm, n // tn, nk),
        in_specs=in_specs, out_specs=o_spec, out_shape=out_struct,
        scratch_shapes=[pltpu.VMEM((tm, tn), F32)] if nk > 1 else [],
        compiler_params=_cp(("parallel", "parallel", "arbitrary")),
    )(*ins)


def _ew(fn, rows, pars, outs, accs=(), *, s, ts, name):
    n_in = len(rows) + len(pars)
    n_o = len(outs)

    def body(*refs):
        i = pl.program_id(0)
        vals = fn(*[r[...] for r in refs[:n_in]])
        if not isinstance(vals, (tuple, list)):
            vals = (vals,)
        assert len(vals) == n_o + len(accs), (name, len(vals))
        for r, v in zip(refs[n_in:n_in + n_o], vals[:n_o]):
            r[...] = v.astype(r.dtype)
        for r, v in zip(refs[n_in + n_o:], vals[n_o:]):
            @pl.when(i == 0)
            def _(r=r, v=v):
                r[...] = v

            @pl.when(i > 0)
            def _(r=r, v=v):
                r[...] += v

    in_specs = [sp for _, sp in rows]
    in_specs += [pl.BlockSpec(p.shape, lambda i, nd=p.ndim: (0,) * nd) for p in pars]
    out_specs = [pl.BlockSpec((ts, w), lambda i: (i, 0)) for w, _ in outs]
    out_specs += [pl.BlockSpec((r, w), lambda i: (0, 0)) for r, w in accs]
    out_shape = [jax.ShapeDtypeStruct((s, w), dt) for w, dt in outs]
    out_shape += [jax.ShapeDtypeStruct((r, w), F32) for r, w in accs]
    return pl.pallas_call(
        body, name=name, grid=(s // ts,), in_specs=in_specs, out_specs=out_specs, out_shape=out_shape,
        compiler_params=_cp(("arbitrary",)),
    )(*[a for a, _ in rows], *pars)


def _cols(arr, width, blk, ts):
    return (arr, pl.BlockSpec((ts, width), lambda i, b=blk: (i, b)))


def _lead(arr, d, ts):
    return (arr, pl.BlockSpec((None, ts, arr.shape[2]), lambda i, d=d: (d, i, 0)))


def _rowsum(x):
    return jnp.sum(x, axis=0, keepdims=True)


def _lanesum(x):
    return jnp.sum(x, axis=-1, keepdims=True)


def _gsum(x, group):
    w = x.shape[-1]
    if group == w:
        return jnp.broadcast_to(_lanesum(x), x.shape)
    parts = [jnp.broadcast_to(_lanesum(x[:, g:g + group]), (x.shape[0], group)) for g in range(0, w, group)]
    return jnp.concatenate(parts, axis=-1)


def _gn(x, gain, group, n):
    rstd = lax.rsqrt(_gsum(x * x, group) * (1.0 / n) + EPS)
    xn = x * rstd
    return xn * gain, xn, rstd


def _gn_bwd(dy, xn, rstd, gain, group, n):
    dxn = dy * gain
    dx = rstd * (dxn - xn * (_gsum(dxn * xn, group) * (1.0 / n)))
    return dx, _rowsum(dy * xn)


def _sigmoid(x):
    return 1.0 / (1.0 + jnp.exp(-x))


def _rmsnorm(x_row, g, *, n, s, ts, name):
    w = g.shape[-1]

    def fn(x, gv):
        return _gn(x, gv, w, n)[0]

    return _ew(fn, [x_row], [g], [(w, BF16)], s=s, ts=ts, name=name)[0]


def _rmsnorm_bwd(x_row, g, dh, dres, *, n, s, ts, name):
    w = g.shape[-1]
    has_res = dres is not None

    def fn(x, dhv, *rest):
        gv = rest[-1]
        _, xn, rstd = _gn(x, gv, w, n)
        dx, dg = _gn_bwd(dhv, xn, rstd, gv, w, n)
        if has_res:
            dx = dx + rest[0]
        return dx, dg

    rows = [x_row, _cols(dh, w, 0, ts)] + ([_cols(dres, w, 0, ts)] if has_res else [])
    return _ew(fn, rows, [g], [(w, F32)], [(1, w)], s=s, ts=ts, name=name)


def _rope_tables(pos, real, offset):
    half = real // 2
    inv = ROPE_THETA ** (-jnp.arange(half, dtype=F32) / half)
    ang = pos.astype(F32)[:, None] * inv
    c, sn = jnp.cos(ang), jnp.sin(ang)
    s = pos.shape[0]
    cos_t = jnp.concatenate([jnp.ones((s, offset), F32), c, c,
                             jnp.ones((s, LANES - offset - real), F32)], axis=1)
    sin_t = jnp.concatenate([jnp.zeros((s, offset), F32), -sn, sn,
                             jnp.zeros((s, LANES - offset - real), F32)], axis=1)
    return cos_t, sin_t


def _rope(x, cos_t, sin_t, real, offset):
    half = real // 2
    lane = lax.broadcasted_iota(jnp.int32, x.shape, 1)
    partner = jnp.where(lane < offset + half, pltpu.roll(x, LANES - half, 1), pltpu.roll(x, half, 1))
    return x * cos_t + partner * sin_t


def _mla_prep(q_pre, kv_pre, p_even, cos_m, sin_m, qhn, khn, *, s, ts):
    w = MLA_H * LANES

    def fn(qp, kp, vp, kr, c, sn, gq, gk):
        qs, ks = [], []
        for h in range(MLA_H):
            sl = slice(h * LANES, (h + 1) * LANES)
            qn = _gn(qp[:, sl], gq, LANES, MLA_QK)[0]
            kn = _gn(kp[:, sl] + kr, gk, LANES, MLA_QK)[0]
            qs.append(_rope(qn, c, sn, MLA_ROPE, MLA_NOPE) * MLA_SCALE)
            ks.append(_rope(kn, c, sn, MLA_ROPE, MLA_NOPE))
        lane = lax.broadcasted_iota(jnp.int32, vp.shape, 1) % LANES
        ones = (lane == V_ONES[0]) | (lane == V_ONES[1])
        return jnp.concatenate(qs, axis=1), jnp.concatenate(ks, axis=1), jnp.where(ones, 1.0, vp)

    rows = [_cols(q_pre, w, 0, ts), _cols(kv_pre, w, 0, ts), _cols(kv_pre, w, 1, ts),
            _cols(p_even, LANES, EV_KR_BLK, ts), _cols(cos_m, LANES, 0, ts), _cols(sin_m, LANES, 0, ts)]
    return _ew(fn, rows, [qhn, khn], [(w, BF16)] * 3, s=s, ts=ts, name="mla_prep")


def _mla_prep_bwd(q_pre, kv_pre, p_even, cos_m, sin_m, qhn, khn, dq, dk, *, s, ts):
    w = MLA_H * LANES

    def fn(qp, kp, kr, c, sn, dqv, dkv, gq, gk):
        dqs, dks = [], []
        dkr = jnp.zeros_like(kr)
        dgq = jnp.zeros((1, LANES), F32)
        dgk = jnp.zeros((1, LANES), F32)
        for h in range(MLA_H):
            sl = slice(h * LANES, (h + 1) * LANES)
            _, qn, qr = _gn(qp[:, sl], gq, LANES, MLA_QK)
            _, kn, krs = _gn(kp[:, sl] + kr, gk, LANES, MLA_QK)
            dqn = _rope(dqv[:, sl] * MLA_SCALE, c, -sn, MLA_ROPE, MLA_NOPE)
            dkn = _rope(dkv[:, sl], c, -sn, MLA_ROPE, MLA_NOPE)
            dqh, g1 = _gn_bwd(dqn, qn, qr, gq, LANES, MLA_QK)
            dkh, g2 = _gn_bwd(dkn, kn, krs, gk, LANES, MLA_QK)
            dqs.append(dqh)
            dks.append(dkh)
            dkr = dkr + dkh
            dgq = dgq + g1
            dgk = dgk + g2
        return jnp.concatenate(dqs, axis=1), jnp.concatenate(dks, axis=1), dkr, dgq, dgk

    rows = [_cols(q_pre, w, 0, ts), _cols(kv_pre, w, 0, ts), _cols(p_even, LANES, EV_KR_BLK, ts),
            _cols(cos_m, LANES, 0, ts), _cols(sin_m, LANES, 0, ts), _cols(dq, w, 0, ts), _cols(dk, w, 0, ts)]
    return _ew(fn, rows, [qhn, khn], [(w, BF16), (w, BF16), (LANES, BF16)], [(1, LANES), (1, LANES)],
               s=s, ts=ts, name="mla_prep_bwd")


def _flash_fwd(q, k, v, *, tq, tk, side=()):
    s = q.shape[0]
    nq, nk = s // tq, s // tk
    rq = tq
    ns = len(side)

    def body(*refs):
        q_ref, k_ref, v_ref = refs[:3]
        o_ref, lse_ref = refs[3 + ns:5 + ns]
        m_s, acc = refs[5 + 2 * ns:7 + 2 * ns]
        h, i, j = pl.program_id(0), pl.program_id(1), pl.program_id(2)
        if ns:
            local, sends, arrivals = _gather_copies(side, refs[3:3 + ns], refs[5 + ns:5 + 2 * ns], *refs[7 + 2 * ns:])

            @pl.when((h == 0) & (i == 0) & (j == 0))
            def _():
                for cp in local + sends:
                    cp.start()

        @pl.when(j == 0)
        def _():
            m_s[...] = jnp.full_like(m_s, -jnp.inf)
            acc[...] = jnp.zeros_like(acc)

        kv, vv = k_ref[...], v_ref[...]
        for r in range(0, tq, rq):
            rows = slice(r, r + rq)
            sc = _dot_nt(q_ref[rows, :], kv)
            m_prev = m_s[rows, :]
            m_new = jnp.maximum(m_prev, jnp.max(sc, axis=-1, keepdims=True))
            p = jnp.exp(sc - jnp.tile(m_new, (1, tk // LANES)))
            acc[rows, :] = jnp.exp(m_prev - m_new) * acc[rows, :] + _dot(_bf(p), vv)
            m_s[rows, :] = m_new

        @pl.when(j == nk - 1)
        def _():
            a = acc[...]
            l = a[:, V_ONES[0]:V_ONES[0] + 1]
            o_ref[...] = (a / l).astype(o_ref.dtype)
            lse_ref[...] = m_s[:, 0:1] + jnp.log(l)

        if ns:
            @pl.when((h == MLA_H - 1) & (i == nq - 1) & (j == nk - 1))
            def _():
                for cp in arrivals:
                    cp.wait_recv()
                for cp in sends:
                    cp.wait_send()
                for cp in local:
                    cp.wait()

    qs = pl.BlockSpec((tq, LANES), lambda h, i, j: (i, h))
    ks = pl.BlockSpec((tk, LANES), lambda h, i, j: (j, h))
    outs = pl.pallas_call(
        body, name="mla_flash_fwd_gather" if ns else "mla_flash_fwd", grid=(MLA_H, nq, nk),
        in_specs=[qs, ks, ks] + [HBM_SPEC] * ns,
        out_specs=[qs, pl.BlockSpec((None, tq, 1), lambda h, i, j: (h, i, 0))] + [HBM_SPEC] * ns,
        out_shape=[jax.ShapeDtypeStruct((s, MLA_H * LANES), BF16), jax.ShapeDtypeStruct((MLA_H, s, 1), F32)]
        + _gather_shapes(side),
        scratch_shapes=[pltpu.VMEM((tq, LANES), F32), pltpu.VMEM((tq, LANES), F32)]
        + ([_sems(3 * ns), _sems(3 * ns), _sems(ns)] if ns else []),
        compiler_params=_cp(("arbitrary",) * 3 if ns else ("parallel", "parallel", "arbitrary")),
    )(q, k, v, *[a for a, _, _ in side])
    return outs[0], outs[1], list(outs[2:])


def _attn_bwd_prep(dar, o, *, s, ts):
    w = MLA_H * LANES

    def fn(dov, ov):
        outs = []
        lane = lax.broadcasted_iota(jnp.int32, (dov.shape[0], LANES), 1)
        for h in range(MLA_H):
            sl = slice(h * LANES, (h + 1) * LANES)
            d = dov[:, sl]
            delta = _lanesum(d * ov[:, sl].astype(F32))
            hi = _bf(delta).astype(F32)
            outs.append(jnp.where(lane == V_ONES[0], -hi, jnp.where(lane == V_ONES[1], hi - delta, d)))
        return jnp.concatenate(outs, axis=1)

    return _ew(fn, [_cols(dar, w, 0, ts), _cols(o, w, 0, ts)], [], [(w, BF16)], s=s, ts=ts,
               name="mla_attn_bwd_prep")[0]


def _flash_bwd(q, k, v, do, lse, *, tq, tk):
    s = q.shape[0]
    nq, nk = s // tq, s // tk
    rq = tq

    def body(q_ref, k_ref, v_ref, do_ref, lse_ref, dq_ref, dk_ref, dv_ref, dk_acc, dv_acc):
        j = pl.program_id(1)
        i = pl.program_id(2)
        kv, vv = k_ref[...], v_ref[...]
        dv_c = dk_c = None
        dq_parts = []
        for r in range(0, tq, rq):
            qv, dov = q_ref[r:r + rq, :], do_ref[r:r + rq, :]
            p = jnp.exp(_dot_nt(qv, kv) - lse_ref[r:r + rq, :])
            ds = _bf(p * _dot_nt(dov, vv))
            dv_r = _dot_tn(_bf(p), dov)
            dk_r = _dot_tn(ds, qv)
            dv_c = dv_r if dv_c is None else dv_c + dv_r
            dk_c = dk_r if dk_c is None else dk_c + dk_r
            dq_parts.append(_dot(ds, kv))
        dq_c = jnp.concatenate(dq_parts, axis=0)
        rows = pl.ds(pl.multiple_of(i * tq, tq), tq)

        @pl.when(i == 0)
        def _():
            dk_acc[...] = dk_c
            dv_acc[...] = dv_c

        @pl.when(i > 0)
        def _():
            dk_acc[...] += dk_c
            dv_acc[...] += dv_c

        @pl.when(j == 0)
        def _():
            dq_ref[rows, :] = dq_c

        @pl.when(j > 0)
        def _():
            dq_ref[rows, :] += dq_c

        @pl.when(i == nq - 1)
        def _():
            dk_ref[...] = dk_acc[...]
            dv_ref[...] = dv_acc[...].astype(dv_ref.dtype)

    qs = pl.BlockSpec((tq, LANES), lambda h, j, i: (i, h))
    ks = pl.BlockSpec((tk, LANES), lambda h, j, i: (j, h))
    st = pl.BlockSpec((None, tq, 1), lambda h, j, i: (h, i, 0))
    return pl.pallas_call(
        body, name="mla_flash_bwd", grid=(MLA_H, nk, nq),
        in_specs=[qs, ks, ks, qs, st],
        out_specs=[pl.BlockSpec((s, LANES), lambda h, j, i: (0, h)), ks, ks],
        out_shape=[jax.ShapeDtypeStruct((s, MLA_H * LANES), F32), jax.ShapeDtypeStruct((s, MLA_H * LANES), F32),
                   jax.ShapeDtypeStruct((s, MLA_H * LANES), BF16)],
        scratch_shapes=[pltpu.VMEM((tk, LANES), F32), pltpu.VMEM((tk, LANES), F32)],
        compiler_params=_cp(("parallel", "arbitrary", "arbitrary")),
    )(q, k, v, do, lse)


def _ret_geometry(d, c):
    df = d.astype(F32)
    ii = lax.broadcasted_iota(jnp.int32, (c, c), 0).astype(F32)
    jj = lax.broadcasted_iota(jnp.int32, (c, c), 1).astype(F32)
    rel = (ii - jj) * (1.0 - 2.0 * df)
    mask = rel >= df
    rel0 = jnp.maximum(rel, 0.0)
    pos = lax.broadcasted_iota(jnp.int32, (c, 1), 0).astype(F32)
    ez = (c - 1 - pos) + df * (2.0 * pos - (c - 1))
    ex = (pos + 1.0) + df * (c - 1 - 2.0 * pos)
    return mask, rel0, ez, ex


def _chunk_index(n_chunks):
    return lambda d, n: n + d * (n_chunks - 1 - 2 * n)


def _ret_fwd(p_even, cos_r, sin_r, theta_l):
    s = p_even.shape[0]
    c = RET_C
    n_chunks = s // c
    w = RET_H * LANES
    cidx = _chunk_index(n_chunks)

    def body(q_ref, k_ref, v_ref, cos_ref, sin_ref, th_ref, o_ref, rp_ref, r_s):
        d = pl.program_id(0)
        n = pl.program_id(1)

        @pl.when(n == 0)
        def _():
            r_s[...] = jnp.zeros_like(r_s)

        lg = jnp.log1p(-jnp.exp(-th_ref[...] * LN2))
        mask, rel0, ez, ex = _ret_geometry(d, c)
        cs, sn = cos_ref[...], sin_ref[...]
        rp_ref[...] = r_s[...]
        for h in range(RET_H):
            sl = slice(h * LANES, (h + 1) * LANES)
            lgh = lg[:, h * LANES:h * LANES + 1]
            dm = jnp.where(mask, jnp.exp(lgh * rel0), 0.0)
            qh = _bf(_rope(q_ref[:, sl], cs, sn, RET_DK, 0))
            kf = _rope(k_ref[:, sl], cs, sn, RET_DK, 0) * (RET_DK ** -0.5)
            kh = _bf(kf)
            vh = _bf(v_ref[:, sl])
            rh = r_s[sl, :]
            a = _dot_nt(qh, kh) * dm
            o_ref[:, sl] = _dot(_bf(a), vh) + jnp.exp(lgh * ex) * _dot(qh, _bf(rh))
            zk = _bf(kf * jnp.exp(lgh * ez))
            r_s[sl, :] = jnp.exp(lgh * c) * rh + _dot_tn(zk, vh)

    def col(blk):
        return pl.BlockSpec((c, w), lambda d, n: (cidx(d, n), blk))

    tab = pl.BlockSpec((c, LANES), lambda d, n: (cidx(d, n), 0))
    return pl.pallas_call(
        body, name="ret_fwd", grid=(2, n_chunks),
        in_specs=[col(0), col(1), col(2), tab, tab, pl.BlockSpec((None, 1, w), lambda d, n: (d, 0, 0))],
        out_specs=[pl.BlockSpec((None, c, w), lambda d, n: (d, cidx(d, n), 0)),
                   pl.BlockSpec((None, None, w, LANES), lambda d, n: (d, cidx(d, n), 0, 0))],
        out_shape=[jax.ShapeDtypeStruct((2, s, w), F32), jax.ShapeDtypeStruct((2, n_chunks, w, LANES), F32)],
        scratch_shapes=[pltpu.VMEM((w, LANES), F32)],
        compiler_params=_cp(("arbitrary", "arbitrary")),
    )(p_even, p_even, p_even, cos_r, sin_r, theta_l)


def _ret_bwd(p_even, cos_r, sin_r, theta_l, theta_h, r_prev, do):
    s = p_even.shape[0]
    c = RET_C
    n_chunks = s // c
    w = RET_H * LANES
    fwd_idx = _chunk_index(n_chunks)

    def cidx(d, n):
        return fwd_idx(d, n_chunks - 1 - n)

    def body(q_ref, k_ref, v_ref, cos_ref, sin_ref, th_ref, thh_ref, rp_ref, do_ref,
             dq_ref, dk_ref, dv_ref, dth_ref, dr_s):
        d = pl.program_id(0)
        n = pl.program_id(1)

        @pl.when(n == 0)
        def _():
            dr_s[...] = jnp.zeros_like(dr_s)
            dth_ref[...] = jnp.zeros_like(dth_ref)

        lg = jnp.log1p(-jnp.exp(-th_ref[...] * LN2))
        mask, rel0, ez, ex = _ret_geometry(d, c)
        cs, sn = cos_ref[...], sin_ref[...]
        row = lax.broadcasted_iota(jnp.int32, (RET_H, LANES), 0)
        dlg = jnp.zeros((RET_H, LANES), F32)
        kscale = RET_DK ** -0.5
        for h in range(RET_H):
            sl = slice(h * LANES, (h + 1) * LANES)
            lgh = lg[:, h * LANES:h * LANES + 1]
            dm = jnp.where(mask, jnp.exp(lgh * rel0), 0.0)
            zeta = jnp.exp(lgh * ez)
            xi = jnp.exp(lgh * ex)
            gc = jnp.exp(lgh * c)
            qf = _rope(q_ref[:, sl], cs, sn, RET_DK, 0)
            qh = _bf(qf)
            kf = _rope(k_ref[:, sl], cs, sn, RET_DK, 0) * kscale
            kh = _bf(kf)
            zkf = kf * zeta
            zk = _bf(zkf)
            vh = _bf(v_ref[:, sl])
            dof = do_ref[:, sl]
            doh = _bf(dof)
            rp = rp_ref[sl, :]
            rpb = _bf(rp)
            drn = dr_s[sl, :]
            drb = _bf(drn)
            a = _dot_nt(qh, kh) * dm
            da0 = _dot_nt(doh, vh)
            da = _bf(da0 * dm)
            vdr = _dot_nt(vh, drb)
            dq_r = _dot(da, kh) + xi * _dot_nt(doh, rpb)
            dk_r = _dot_tn(da, qh) + zeta * vdr
            dv_ref[:, sl] = _dot_tn(_bf(a), doh) + _dot(zk, drb)
            dq_ref[:, sl] = _rope(dq_r, cs, -sn, RET_DK, 0)
            dk_ref[:, sl] = _rope(dk_r * kscale, cs, -sn, RET_DK, 0)
            dr_s[sl, :] = _dot_tn(_bf(qf * xi), doh) + gc * drn
            ocross = xi * _dot(qh, rpb)
            t = (jnp.sum(rel0 * a * da0, keepdims=True)
                 + jnp.sum(ex * dof * ocross, keepdims=True)
                 + c * gc * jnp.sum(drn * rp, keepdims=True)
                 + jnp.sum(ez * zkf * vdr, keepdims=True))
            dlg = jnp.where(row == h, t, dlg)
        x2 = jnp.exp(-thh_ref[...] * LN2)
        dth_ref[...] += dlg * (x2 * LN2 / (1.0 - x2))

    def col(blk):
        return pl.BlockSpec((c, w), lambda d, n: (cidx(d, n), blk))

    tab = pl.BlockSpec((c, LANES), lambda d, n: (cidx(d, n), 0))
    dirrow = pl.BlockSpec((None, c, w), lambda d, n: (d, cidx(d, n), 0))
    hrow = pl.BlockSpec((None, RET_H, LANES), lambda d, n: (d, 0, 0))
    return pl.pallas_call(
        body, name="ret_bwd", grid=(2, n_chunks),
        in_specs=[col(0), col(1), col(2), tab, tab, pl.BlockSpec((None, 1, w), lambda d, n: (d, 0, 0)), hrow,
                  pl.BlockSpec((None, None, w, LANES), lambda d, n: (d, cidx(d, n), 0, 0)),
                  pl.BlockSpec((c, w), lambda d, n: (cidx(d, n), 0))],
        out_specs=[dirrow, dirrow, dirrow, hrow],
        out_shape=[jax.ShapeDtypeStruct((2, s, w), F32)] * 3 + [jax.ShapeDtypeStruct((2, RET_H, LANES), F32)],
        scratch_shapes=[pltpu.VMEM((w, LANES), F32)],
        compiler_params=_cp(("arbitrary", "arbitrary")),
    )(p_even, p_even, p_even, cos_r, sin_r, theta_l, theta_h, r_prev, do)


def _post_fwd(o2, gate_row, gain, *, group, n, s, ts, name):
    w = o2.shape[2]

    def fn(of, ob, g, gv):
        y = _gn(of + ob, gv, group, n)[0]
        return g * _sigmoid(g) * y

    return _ew(fn, [_lead(o2, 0, ts), _lead(o2, 1, ts), gate_row], [gain], [(w, BF16)], s=s, ts=ts, name=name)[0]


def _post_bwd(o2, gate_row, gain, dr_row, *, group, n, s, ts, name):
    w = o2.shape[2]

    def fn(of, ob, g, dr, gv):
        y, xn, rstd = _gn(of + ob, gv, group, n)
        sg = _sigmoid(g)
        dy = dr * (g * sg)
        dgate = dr * y * (sg * (1.0 + g * (1.0 - sg)))
        do, dgain = _gn_bwd(dy, xn, rstd, gv, group, n)
        return do, dgate, dgain

    return _ew(fn, [_lead(o2, 0, ts), _lead(o2, 1, ts), gate_row, dr_row], [gain],
               [(w, F32), (w, BF16)], [(1, w)], s=s, ts=ts, name=name)


def _sum2(a2, *, s, ts, name):
    w = a2.shape[2]
    return _ew(lambda a, b: a + b, [_lead(a2, 0, ts), _lead(a2, 1, ts)], [], [(w, BF16)], s=s, ts=ts, name=name)[0]


def _gla_common(d, q_ref, k_ref, ga_ref, wg_ref, bg_ref):
    c = GLA_C
    df = d.astype(F32)
    ii = lax.broadcasted_iota(jnp.int32, (c, c), 0).astype(F32)
    jj = lax.broadcasted_iota(jnp.int32, (c, c), 1).astype(F32)
    rel = (ii - jj) * (1.0 - 2.0 * df)
    tri = _bf(jnp.where(rel >= 0.0, 1.0, 0.0))
    mask = rel >= df
    gab = _bf(ga_ref[...])
    z = _dot(gab, wg_ref[...]) + bg_ref[...]
    la = (jnp.minimum(z, 0.0) - jnp.log1p(jnp.exp(-jnp.abs(z)))) * (1.0 / GLA_TAU)
    l1, l2, l3 = _split3(la)
    b = _dot(tri, l1) + _dot(tri, l2) + _dot(tri, l3)
    first = d == 0
    bm = jnp.where(first, b[c // 2:c // 2 + 1], b[c // 2 - 1:c // 2])
    bl = jnp.where(first, b[c - 1:c], b[0:1])
    q = q_ref[...] * (GLA_DK ** -0.5)
    k = k_ref[...]
    e1, e2, e3, eb = jnp.exp(b - bm), jnp.exp(bm - b), jnp.exp(bl - b), jnp.exp(b)
    return dict(tri=tri, mask=mask, gab=gab, z=z, ebl=jnp.exp(bl), e1=e1, e2=e2, e3=e3, eb=eb,
                qc=q * e1, kc=k * e2, kd=k * e3, qe=q * eb, first=first)


def _col_scale(row_vec, width):
    t = jnp.broadcast_to(row_vec, (LANES, LANES)).T
    return jnp.concatenate([t] * (width // LANES), axis=1)


def _gla_fwd(p_odd, wg2, bg2):
    s = p_odd.shape[0]
    c = GLA_C
    n_chunks = s // c
    wk, wv = GLA_H * GLA_DK, GLA_H * GLA_DV
    cidx = _chunk_index(n_chunks)

    def body(q_ref, k_ref, v_ref, ga_ref, wg_ref, bg_ref, o_ref, sp_ref, s_s):
        d = pl.program_id(0)
        n = pl.program_id(1)

        @pl.when(n == 0)
        def _():
            s_s[...] = jnp.zeros_like(s_s)

        g = _gla_common(d, q_ref, k_ref, ga_ref, wg_ref, bg_ref)
        sp_ref[...] = s_s[...]
        for h in range(GLA_H):
            sl = slice(h * GLA_DK, (h + 1) * GLA_DK)
            vs = slice(h * GLA_DV, (h + 1) * GLA_DV)
            vh = _bf(v_ref[:, vs])
            sh = s_s[sl, :]
            a = jnp.where(g["mask"], _dot_nt(_bf(g["qc"][:, sl]), _bf(g["kc"][:, sl])), 0.0)
            o_ref[:, vs] = _dot(_bf(a), vh) + _dot(_bf(g["qe"][:, sl]), _bf(sh))
            s_s[sl, :] = _col_scale(g["ebl"][:, sl], GLA_DV) * sh + _dot_tn(_bf(g["kd"][:, sl]), vh)

    def col(width, blk):
        return pl.BlockSpec((c, width), lambda d, n: (cidx(d, n), blk))

    return pl.pallas_call(
        body, name="gla_fwd", grid=(2, n_chunks),
        in_specs=[col(wk, 0), col(wk, 1), col(wv, 1), col(LANES, OD_GA_BLK),
                  pl.BlockSpec((None, LANES, wk), lambda d, n: (d, 0, 0)),
                  pl.BlockSpec((None, 1, wk), lambda d, n: (d, 0, 0))],
        out_specs=[pl.BlockSpec((None, c, wv), lambda d, n: (d, cidx(d, n), 0)),
                   pl.BlockSpec((None, None, wk, GLA_DV), lambda d, n: (d, cidx(d, n), 0, 0))],
        out_shape=[jax.ShapeDtypeStruct((2, s, wv), F32), jax.ShapeDtypeStruct((2, n_chunks, wk, GLA_DV), F32)],
        scratch_shapes=[pltpu.VMEM((wk, GLA_DV), F32)],
        compiler_params=_cp(("arbitrary", "arbitrary")),
    )(p_odd, p_odd, p_odd, p_odd, wg2, bg2)


def _gla_bwd(p_odd, wg2, bg2, s_prev, do):
    s = p_odd.shape[0]
    c = GLA_C
    n_chunks = s // c
    wk, wv = GLA_H * GLA_DK, GLA_H * GLA_DV
    fwd_idx = _chunk_index(n_chunks)

    def cidx(d, n):
        return fwd_idx(d, n_chunks - 1 - n)

    def body(q_ref, k_ref, v_ref, ga_ref, wg_ref, bg_ref, sp_ref, do_ref,
             dq_ref, dk_ref, dv_ref, dga_ref, dwg_ref, dbg_ref, ds_s):
        d = pl.program_id(0)
        n = pl.program_id(1)

        @pl.when(n == 0)
        def _():
            ds_s[...] = jnp.zeros_like(ds_s)
            dwg_ref[...] = jnp.zeros_like(dwg_ref)
            dbg_ref[...] = jnp.zeros_like(dbg_ref)

        g = _gla_common(d, q_ref, k_ref, ga_ref, wg_ref, bg_ref)
        mask = g["mask"]
        ones8 = jnp.ones((8, GLA_DV), BF16)
        dbs, dbms, dbls = [], [], []
        for h in range(GLA_H):
            sl = slice(h * GLA_DK, (h + 1) * GLA_DK)
            vs = slice(h * GLA_DV, (h + 1) * GLA_DV)
            qc, kc, kd, qe = g["qc"][:, sl], g["kc"][:, sl], g["kd"][:, sl], g["qe"][:, sl]
            qcb, kcb, kdb, qeb = _bf(qc), _bf(kc), _bf(kd), _bf(qe)
            vh = _bf(v_ref[:, vs])
            doh = _bf(do_ref[:, vs])
            sp = sp_ref[sl, :]
            dsn = ds_s[sl, :]
            dsb = _bf(dsn)
            a = _bf(jnp.where(mask, _dot_nt(qcb, kcb), 0.0))
            da = _bf(jnp.where(mask, _dot_nt(doh, vh), 0.0))
            dv_ref[:, vs] = _dot_tn(a, doh) + _dot(kdb, dsb)
            dqc = _dot(da, kcb)
            dkc = _dot_tn(da, qcb)
            dqe = _dot_nt(doh, _bf(sp))
            dkd = _dot_nt(vh, dsb)
            ds_s[sl, :] = _dot_tn(qeb, doh) + _col_scale(g["ebl"][:, sl], GLA_DV) * dsn
            dq_ref[:, sl] = (dqc * g["e1"][:, sl] + dqe * g["eb"][:, sl]) * (GLA_DK ** -0.5)
            dk_ref[:, sl] = dkc * g["e2"][:, sl] + dkd * g["e3"][:, sl]
            t1, t2, t3, t4 = dqc * qc, dkc * kc, dqe * qe, dkd * kd
            dbs.append(t1 - t2 + t3 - t4)
            dbms.append(_rowsum(t2 - t1))
            m1, m2, _ = _split3(dsn * sp)
            rs = (_dot_nt(ones8, m1) + _dot_nt(ones8, m2))[0:1]
            dbls.append(_rowsum(t4) + g["ebl"][:, sl] * rs)
        db = jnp.concatenate(dbs, axis=1)
        dbm = jnp.concatenate(dbms, axis=1)
        dbl = jnp.concatenate(dbls, axis=1)
        row = lax.broadcasted_iota(jnp.int32, (c, wk), 0)
        mid = jnp.where(g["first"], c // 2, c // 2 - 1)
        last = jnp.where(g["first"], c - 1, 0)
        db = db + jnp.where(row == mid, dbm, 0.0) + jnp.where(row == last, dbl, 0.0)
        d1, d2, d3 = _split3(db)
        tri = g["tri"]
        dla = _dot_tn(tri, d1) + _dot_tn(tri, d2) + _dot_tn(tri, d3)
        dz = dla * (1.0 / GLA_TAU) * (1.0 - _sigmoid(g["z"]))
        dzb = _bf(dz)
        dga_ref[...] = _dot_nt(dzb, wg_ref[...])
        dwg_ref[...] += _dot_tn(g["gab"], dzb)
        dbg_ref[...] += _rowsum(dz)

    def col(width, blk):
        return pl.BlockSpec((c, width), lambda d, n: (cidx(d, n), blk))

    def dirrow(width):
        return pl.BlockSpec((None, c, width), lambda d, n: (d, cidx(d, n), 0))

    return pl.pallas_call(
        body, name="gla_bwd", grid=(2, n_chunks),
        in_specs=[col(wk, 0), col(wk, 1), col(wv, 1), col(LANES, OD_GA_BLK),
                  pl.BlockSpec((None, LANES, wk), lambda d, n: (d, 0, 0)),
                  pl.BlockSpec((None, 1, wk), lambda d, n: (d, 0, 0)),
                  pl.BlockSpec((None, None, wk, GLA_DV), lambda d, n: (d, cidx(d, n), 0, 0)),
                  pl.BlockSpec((c, wv), lambda d, n: (cidx(d, n), 0))],
        out_specs=[dirrow(wk), dirrow(wk), dirrow(wv), dirrow(LANES),
                   pl.BlockSpec((None, LANES, wk), lambda d, n: (d, 0, 0)),
                   pl.BlockSpec((None, 1, wk), lambda d, n: (d, 0, 0))],
        out_shape=[jax.ShapeDtypeStruct((2, s, wk), F32), jax.ShapeDtypeStruct((2, s, wk), F32),
                   jax.ShapeDtypeStruct((2, s, wv), F32), jax.ShapeDtypeStruct((2, s, LANES), F32),
                   jax.ShapeDtypeStruct((2, LANES, wk), F32), jax.ShapeDtypeStruct((2, 1, wk), F32)],
        scratch_shapes=[pltpu.VMEM((wk, GLA_DV), F32)],
        compiler_params=_cp(("arbitrary", "arbitrary")),
    )(p_odd, p_odd, p_odd, p_odd, wg2, bg2, s_prev, do)


HALO = 8


def _halo_specs(width_blk, col0, ts, s):
    r = ts // HALO
    last = s // HALO - 1
    cur = pl.BlockSpec((ts, width_blk), lambda j, i: (i, col0 + j))
    prev = pl.BlockSpec((HALO, width_blk), lambda j, i: (jnp.maximum(i * r - 1, 0), col0 + j))
    nxt = pl.BlockSpec((HALO, width_blk), lambda j, i: (jnp.minimum((i + 1) * r, last), col0 + j))
    return [prev, cur, nxt]


def _with_halo(prev_ref, cur_ref, next_ref, i, n_i):
    p = jnp.where(i == 0, 0.0, prev_ref[...])
    q = jnp.where(i == n_i - 1, 0.0, next_ref[...])
    return jnp.concatenate([p, cur_ref[...], q], axis=0)


def _shift_down(x):
    return pltpu.roll(x, 1, 0)


def _shift_up(x):
    return pltpu.roll(x, x.shape[0] - 1, 0)


def _ffn_act(up, conv_w, conv_b, *, ts):
    s = up.shape[0]
    tc = _tile(D_FF, 1408)
    nj = D_FF // tc
    n_i = s // ts

    def body(gp, gc, gn, val_ref, w_ref, b_ref, a_ref):
        i = pl.program_id(1)
        g = _with_halo(gp, gc, gn, i, n_i)
        w = w_ref[...]
        conv = w[0:1] * _shift_down(g) + w[1:2] * g + w[2:3] * _shift_up(g) + b_ref[...]
        conv = conv[HALO:HALO + ts]
        a_ref[...] = (conv * _sigmoid(conv) * val_ref[...]).astype(a_ref.dtype)

    return pl.pallas_call(
        body, name="ffn_act", grid=(nj, n_i),
        in_specs=_halo_specs(tc, 0, ts, s) + [pl.BlockSpec((ts, tc), lambda j, i: (i, nj + j)),
                                              pl.BlockSpec((3, tc), lambda j, i: (0, j)),
                                              pl.BlockSpec((1, tc), lambda j, i: (0, j))],
        out_specs=pl.BlockSpec((ts, tc), lambda j, i: (i, j)),
        out_shape=jax.ShapeDtypeStruct((s, D_FF), BF16),
        compiler_params=_cp(("parallel", "arbitrary")),
    )(up, up, up, up, conv_w, conv_b)


def _ffn_act_bwd(up, da, conv_w, conv_b, *, ts):
    s = up.shape[0]
    tc = _tile(D_FF, 1408)
    nj = D_FF // tc
    n_i = s // ts

    def body(gp, gc, gn, vp, vc, vn, dp, dc, dn, w_ref, b_ref, dg_ref, dval_ref, dw_ref, db_ref):
        i = pl.program_id(1)
        g = _with_halo(gp, gc, gn, i, n_i)
        v = _with_halo(vp, vc, vn, i, n_i)
        dav = _with_halo(dp, dc, dn, i, n_i)
        w = w_ref[...]
        gm, gpl = _shift_down(g), _shift_up(g)
        conv = w[0:1] * gm + w[1:2] * g + w[2:3] * gpl + b_ref[...]
        sg = _sigmoid(conv)
        dgc = dav * v * (sg * (1.0 + conv * (1.0 - sg)))
        dgate = w[0:1] * _shift_up(dgc) + w[1:2] * dgc + w[2:3] * _shift_down(dgc)
        ctr = slice(HALO, HALO + ts)
        dg_ref[...] = dgate[ctr].astype(dg_ref.dtype)
        dval_ref[...] = (dav[ctr] * (conv * sg)[ctr]).astype(dval_ref.dtype)
        dgc_c = dgc[ctr]
        dw = jnp.concatenate([_rowsum(dgc_c * gm[ctr]), _rowsum(dgc_c * g[ctr]), _rowsum(dgc_c * gpl[ctr])], axis=0)
        dbv = _rowsum(dgc_c)

        @pl.when(i == 0)
        def _():
            dw_ref[...] = dw
            db_ref[...] = dbv

        @pl.when(i > 0)
        def _():
            dw_ref[...] += dw
            db_ref[...] += dbv

    tile = pl.BlockSpec((ts, tc), lambda j, i: (i, j))
    return pl.pallas_call(
        body, name="ffn_act_bwd", grid=(nj, n_i),
        in_specs=(_halo_specs(tc, 0, ts, s) + _halo_specs(tc, nj, ts, s) + _halo_specs(tc, 0, ts, s)
                  + [pl.BlockSpec((3, tc), lambda j, i: (0, j)), pl.BlockSpec((1, tc), lambda j, i: (0, j))]),
        out_specs=[tile, tile, pl.BlockSpec((3, tc), lambda j, i: (0, j)), pl.BlockSpec((1, tc), lambda j, i: (0, j))],
        out_shape=[jax.ShapeDtypeStruct((s, D_FF), BF16), jax.ShapeDtypeStruct((s, D_FF), BF16),
                   jax.ShapeDtypeStruct((3, D_FF), F32), jax.ShapeDtypeStruct((1, D_FF), F32)],
        compiler_params=_cp(("parallel", "arbitrary")),
    )(up, up, up, up, up, up, da, da, da, conv_w, conv_b)


def _loss_head(y, target, *, s, ts):
    def fn(yv, tv):
        err = yv - tv
        return err * (1.0 / D_MODEL), _rowsum(err * err)

    return _ew(fn, [_cols(y, D_MODEL, 0, ts), _cols(target, D_MODEL, 0, ts)], [], [(D_MODEL, F32)],
               [(1, D_MODEL)], s=s, ts=ts, name="loss_head")


def _rows_tile(r, width):
    ts = r
    while ts * width * 4 > (1 << 20) and ts % 16 == 0:
        ts //= 2
    return ts


def _adamw(w, g, m, v, *, ts, name):
    r, width = w.shape
    assert r % ts == 0

    def fn(wv, gv, mv, vv):
        mn = ADAM_B1 * mv + (1.0 - ADAM_B1) * gv
        vn = ADAM_B2 * vv + (1.0 - ADAM_B2) * (gv * gv)
        m_hat = mn / (1.0 - ADAM_B1 ** ADAM_STEP)
        v_hat = vn / (1.0 - ADAM_B2 ** ADAM_STEP)
        delta = -ADAM_LR * (m_hat / (jnp.sqrt(v_hat) + ADAM_EPS) + ADAM_WD * wv)
        return delta, mn, vn

    rows = [_cols(a, width, 0, ts) for a in (w, g, m, v)]
    return _ew(fn, rows, [], [(width, F32)] * 3, s=r, ts=ts, name=name)


def _pad_heads(w, heads, real):
    lead = w.shape[:-1]
    w = w.reshape(lead + (heads, real))
    w = jnp.pad(w, [(0, 0)] * len(lead) + [(0, 0), (0, LANES - real)])
    return w.reshape(lead + (heads * LANES,))


def _pad_head_rows(w, heads, real):
    return _pad_heads(w.T, heads, real).T


def _pack_even(p):
    w_in = p["w_in"]
    z = lambda n: jnp.zeros((D_MODEL, n), w_in.dtype)
    o = 0
    parts = {}
    for nm, n in (("cq", MLA_QR), ("ckv", MLA_KVR), ("kr", MLA_ROPE), ("rq", 512), ("rk", 512), ("rv", 512), ("rg", 512)):
        parts[nm] = w_in[:, o:o + n]
        o += n
    w_in_p = jnp.concatenate(
        [_pad_heads(parts[k], RET_H, RET_DK) for k in ("rq", "rk", "rv", "rg")]
        + [parts["cq"], z(EV_CQ - MLA_QR), parts["ckv"], z(MLA_NOPE), parts["kr"], z(LANES - MLA_QK), z(LANES)], axis=1)
    w_uq = jnp.pad(_pad_heads(p["w_uq"], MLA_H, MLA_QK), ((0, EV_CQ - MLA_QR), (0, 0)))
    ukv = p["w_ukv"].reshape(MLA_KVR, MLA_H, MLA_NOPE + MLA_V)
    w_ukv = jnp.concatenate([_pad_heads(ukv[..., :MLA_NOPE].reshape(MLA_KVR, -1), MLA_H, MLA_NOPE),
                             _pad_heads(ukv[..., MLA_NOPE:].reshape(MLA_KVR, -1), MLA_H, MLA_V)], axis=1)
    w_out = jnp.concatenate([_pad_head_rows(p["w_out"][:MLA_H * MLA_V], MLA_H, MLA_V),
                             _pad_head_rows(p["w_out"][MLA_H * MLA_V:], RET_H, RET_DV)], axis=0)
    return dict(
        w_in=w_in_p, w_uq=w_uq, w_ukv=w_ukv, w_out=w_out,
        mix_g=p["mix_norm"][None, :],
        q_norm=jnp.pad(p["q_norm"], (0, EV_CQ - MLA_QR))[None, :],
        kv_norm=p["kv_norm"][None, :],
        qhn=jnp.pad(p["q_head_norm"], (0, LANES - MLA_QK))[None, :],
        khn=jnp.pad(p["k_head_norm"], (0, LANES - MLA_QK))[None, :],
        ret_gain=_pad_heads(p["ret_out_norm"].reshape(-1), RET_H, RET_DV)[None, :],
    )


def _pack_odd(p):
    w_in = p["w_in"]
    ga = w_in[:, 3072:]
    w_in_p = jnp.concatenate([w_in[:, :3072], ga, jnp.zeros((D_MODEL, LANES - 2 * GLA_R), w_in.dtype)], axis=1)
    wk = GLA_H * GLA_DK
    zf = jnp.zeros((LANES - GLA_R, wk), p["w_gate_fwd"].dtype)
    zb0 = jnp.zeros((GLA_R, wk), p["w_gate_fwd"].dtype)
    zb1 = jnp.zeros((LANES - 2 * GLA_R, wk), p["w_gate_fwd"].dtype)
    wg2 = jnp.stack([jnp.concatenate([p["w_gate_fwd"], zf], axis=0),
                     jnp.concatenate([zb0, p["w_gate_bwd"], zb1], axis=0)])
    bg2 = jnp.stack([p["b_gate_fwd"][None, :], p["b_gate_bwd"][None, :]])
    return dict(w_in=w_in_p, wg2=wg2, bg2=bg2, w_out=p["w_out"], mix_g=p["mix_norm"][None, :],
                gla_gain=p["gla_out_norm"].reshape(1, -1))


_MATRICES = ("w_in", "w_uq", "w_ukv", "w_out", "wg2")


def _packed(pack_fn, p):
    packed = pack_fn(p)
    packed = {k: (_bf(v) if k in _MATRICES else v.astype(F32)) for k, v in packed.items()}
    shapes = {k: jax.ShapeDtypeStruct(v.shape, F32) for k, v in p.items()}
    unpack = jax.linear_transpose(pack_fn, shapes)
    return packed, lambda g: unpack(g)[0]


def _ffn_fwd(x, w, *, s, ts):
    h = _rmsnorm(_cols(x, D_MODEL, 0, ts), w["norm_g"], n=D_MODEL, s=s, ts=ts, name="ffn_norm")
    up = _mm(h, w["w_up4"], b_layer=w["layer"], name="ffn_up")
    a = _ffn_act(up, w["conv_w"], w["conv_b"], ts=ts)
    y = _mm(a, w["w_down"], res=x, name="ffn_down")
    return y, dict(x=x, h=h, up=up, a=a)


def _ffn_bwd(dy, w, sv, *, s, ts):
    da = _mm(dy, w["w_down"], tb=True, name="ffn_down_dx")
    g_down = _mm(sv["a"], dy, ta=True, name="ffn_down_dw")
    dgate, dval, g_cw, g_cb = _ffn_act_bwd(sv["up"], da, w["conv_w"], w["conv_b"], ts=ts)
    dup = jnp.concatenate([dgate, dval], axis=1)
    dh = _mm(dup, w["w_up4"], tb=True, b_layer=w["layer"], name="ffn_up_dx")
    g_up = _mm(sv["h"], dup, ta=True, out_chips=True, name="ffn_up_dw")
    dx, g_norm = _rmsnorm_bwd(_cols(sv["x"], D_MODEL, 0, ts), w["norm_g"], dh, dy, n=D_MODEL, s=s, ts=ts,
                              name="ffn_norm_bwd")
    return dx, dict(w_up=g_up, w_down=g_down, conv_w=g_cw, conv_b=g_cb, norm_g=g_norm)


def _even_fwd(x, w, tabs, *, s, ts, side=()):
    cos_m, sin_m, cos_r, sin_r = tabs
    h = _rmsnorm(_cols(x, D_MODEL, 0, ts), w["mix_g"], n=D_MODEL, s=s, ts=ts, name="mix_norm")
    p = _mm(h, w["w_in"], name="even_in")
    cqn = _rmsnorm(_cols(p, EV_CQ, EV_RET // EV_CQ, ts), w["q_norm"], n=MLA_QR, s=s, ts=ts, name="mla_q_norm")
    ckvn = _rmsnorm(_cols(p, MLA_KVR, (EV_RET + EV_CQ) // MLA_KVR, ts), w["kv_norm"], n=MLA_KVR, s=s, ts=ts,
                    name="mla_kv_norm")
    q_pre = _mm(cqn, w["w_uq"], name="mla_uq")
    kv_pre = _mm(ckvn, w["w_ukv"], name="mla_ukv")
    q, k, v = _mla_prep(q_pre, kv_pre, p, cos_m, sin_m, w["qhn"], w["khn"], s=s, ts=ts)
    o, lse, gathered = _flash_fwd(q, k, v, tq=min(s, FLASH_FWD_ROWS), tk=min(s, FLASH_KEYS), side=side)
    o2, r_prev = _ret_fwd(p, cos_r, sin_r, w["theta_l"])
    r = _post_fwd(o2, _cols(p, RET_H * LANES, 3, ts), w["ret_gain"], group=LANES, n=RET_DV, s=s, ts=ts,
                  name="ret_post")
    ar = jnp.concatenate([o, r], axis=1)
    y = _mm(ar, w["w_out"], res=x, name="even_out")
    return y, dict(x=x, h=h, p=p, cqn=cqn, ckvn=ckvn, q_pre=q_pre, kv_pre=kv_pre, q=q, k=k, v=v, o=o, lse=lse,
                   o2=o2, r_prev=r_prev, ar=ar), gathered


def _even_bwd(dy, w, sv, tabs, *, s, ts):
    cos_m, sin_m, cos_r, sin_r = tabs
    p = sv["p"]
    wh = MLA_H * LANES
    dar = _mm(dy, w["w_out"], tb=True, name="even_out_dx")
    g_out = _mm(sv["ar"], dy, ta=True, name="even_out_dw")
    do_attn = _attn_bwd_prep(dar, sv["o"], s=s, ts=ts)
    dq, dk, dv = _flash_bwd(sv["q"], sv["k"], sv["v"], do_attn, sv["lse"], tq=min(s, FLASH_BWD_ROWS),
                            tk=min(s, FLASH_KEYS))
    dq_pre, dk_pre, dkr, g_qhn, g_khn = _mla_prep_bwd(sv["q_pre"], sv["kv_pre"], p, cos_m, sin_m, w["qhn"], w["khn"],
                                                      dq, dk, s=s, ts=ts)
    dkv_pre = jnp.concatenate([dk_pre, dv], axis=1)
    dckvn = _mm(dkv_pre, w["w_ukv"], tb=True, name="mla_ukv_dx")
    g_ukv = _mm(sv["ckvn"], dkv_pre, ta=True, name="mla_ukv_dw")
    dcqn = _mm(dq_pre, w["w_uq"], tb=True, name="mla_uq_dx")
    g_uq = _mm(sv["cqn"], dq_pre, ta=True, name="mla_uq_dw")
    dckv, g_kvn = _rmsnorm_bwd(_cols(p, MLA_KVR, (EV_RET + EV_CQ) // MLA_KVR, ts), w["kv_norm"], dckvn, None,
                               n=MLA_KVR, s=s, ts=ts, name="mla_kv_norm_bwd")
    dcq, g_qn = _rmsnorm_bwd(_cols(p, EV_CQ, EV_RET // EV_CQ, ts), w["q_norm"], dcqn, None, n=MLA_QR, s=s, ts=ts,
                             name="mla_q_norm_bwd")
    do, drg, g_gain = _post_bwd(sv["o2"], _cols(p, wh, 3, ts), w["ret_gain"], _cols(dar, wh, 1, ts),
                                group=LANES, n=RET_DV, s=s, ts=ts, name="ret_post_bwd")
    dq2, dk2, dv2, dth = _ret_bwd(p, cos_r, sin_r, w["theta_l"], w["theta_h"], sv["r_prev"], do)
    drq, drk, drv = (_sum2(a, s=s, ts=ts, name="sum_dirs_1024") for a in (dq2, dk2, dv2))
    dp = jnp.concatenate([drq, drk, drv, drg, _bf(dcq), _bf(dckv), dkr, jnp.zeros((s, LANES), BF16)], axis=1)
    dh = _mm(dp, w["w_in"], tb=True, name="even_in_dx")
    g_in = _mm(sv["h"], dp, ta=True, name="even_in_dw")
    dx, g_mix = _rmsnorm_bwd(_cols(sv["x"], D_MODEL, 0, ts), w["mix_g"], dh, dy, n=D_MODEL, s=s, ts=ts,
                             name="mix_norm_bwd")
    grads = dict(w_in=g_in, w_uq=g_uq, w_ukv=g_ukv, w_out=g_out, mix_g=g_mix, q_norm=g_qn, kv_norm=g_kvn,
                 qhn=g_qhn, khn=g_khn, ret_gain=g_gain)
    return dx, grads, dth[:, :, 0]


def _odd_fwd(x, w, *, s, ts):
    h = _rmsnorm(_cols(x, D_MODEL, 0, ts), w["mix_g"], n=D_MODEL, s=s, ts=ts, name="mix_norm")
    p = _mm(h, w["w_in"], name="odd_in")
    o2, s_prev = _gla_fwd(p, w["wg2"], w["bg2"])
    g = _post_fwd(o2, _cols(p, GLA_H * GLA_DV, 2, ts), w["gla_gain"], group=GLA_DV, n=GLA_DV, s=s, ts=ts,
                  name="gla_post")
    y = _mm(g, w["w_out"], res=x, name="odd_out")
    return y, dict(x=x, h=h, p=p, o2=o2, s_prev=s_prev, g=g)


def _odd_bwd(dy, w, sv, *, s, ts):
    p = sv["p"]
    wv = GLA_H * GLA_DV
    dg = _mm(dy, w["w_out"], tb=True, name="odd_out_dx")
    g_out = _mm(sv["g"], dy, ta=True, name="odd_out_dw")
    do, dgr, g_gain = _post_bwd(sv["o2"], _cols(p, wv, 2, ts), w["gla_gain"], _cols(dg, wv, 0, ts),
                                group=GLA_DV, n=GLA_DV, s=s, ts=ts, name="gla_post_bwd")
    dq2, dk2, dv2, dga2, g_wg, g_bg = _gla_bwd(p, w["wg2"], w["bg2"], sv["s_prev"], do)
    dq = _sum2(dq2, s=s, ts=ts, name="sum_dirs_512")
    dk = _sum2(dk2, s=s, ts=ts, name="sum_dirs_512")
    dv = _sum2(dv2, s=s, ts=ts, name="sum_dirs_1024")
    dga = _sum2(dga2, s=s, ts=ts, name="sum_dirs_128")
    dp = jnp.concatenate([dq, dk, dv, dgr, dga], axis=1)
    dh = _mm(dp, w["w_in"], tb=True, name="odd_in_dx")
    g_in = _mm(sv["h"], dp, ta=True, name="odd_in_dw")
    dx, g_mix = _rmsnorm_bwd(_cols(sv["x"], D_MODEL, 0, ts), w["mix_g"], dh, dy, n=D_MODEL, s=s, ts=ts,
                             name="mix_norm_bwd")
    return dx, dict(w_in=g_in, wg2=g_wg, bg2=g_bg, w_out=g_out, mix_g=g_mix, gla_gain=g_gain)


_EVEN_NAMES = dict(mix_norm="mix_norm_even", w_in="w_in_even", q_norm="mla_q_norm", kv_norm="mla_kv_norm",
                   w_uq="mla_w_uq", w_ukv="mla_w_ukv", q_head_norm="mla_q_head_norm", k_head_norm="mla_k_head_norm",
                   ret_out_norm="ret_out_norm", w_out="w_out_even")
_ODD_NAMES = dict(mix_norm="mix_norm_odd", w_in="w_in_odd", w_gate_fwd="gla_w_gate_fwd", b_gate_fwd="gla_b_gate_fwd",
                  w_gate_bwd="gla_w_gate_bwd", b_gate_bwd="gla_b_gate_bwd", gla_out_norm="gla_out_norm",
                  w_out="w_out_odd")

def _local_step(x, pos, target, full, side=(), finish=None):
    s = x.shape[0]
    ts = min(s, 256)
    tabs = _rope_tables(pos, MLA_ROPE, MLA_NOPE) + _rope_tables(pos, RET_DK, 0)

    def layer_weights(layer):
        i = layer // 2
        names = _EVEN_NAMES if layer % 2 == 0 else _ODD_NAMES
        wm, unpack_m = _packed(_pack_even if layer % 2 == 0 else _pack_odd, {k: full[n][i] for k, n in names.items()})
        if layer % 2 == 0:
            th = jnp.stack([full["ret_theta_fwd"][i], full["ret_theta_bwd"][i]]).astype(F32)
            wm["theta_h"] = jnp.broadcast_to(th[:, :, None], (2, RET_H, LANES))
            wm["theta_l"] = wm["theta_h"].reshape(2, 1, RET_H * LANES)
        w_up4, index = full["ffn_w_up"][layer]
        wf = dict(layer=index, w_up4=w_up4, w_down=_bf(full["ffn_w_down"][layer]),
                  conv_w=full["ffn_conv_w"][layer].astype(F32), conv_b=full["ffn_conv_b"][layer][None, :].astype(F32),
                  norm_g=full["ffn_norm"][layer][None, :].astype(F32))
        return wm, unpack_m, wf

    layers, saved = [], []
    for layer in range(DEPTH):
        layers.append(layer_weights(layer))
        wm, _, wf = layers[-1]
        if layer % 2 == 0:
            x, sv_m, gathered = _even_fwd(x, wm, tabs, s=s, ts=ts, side=side if layer == 0 else ())
            if layer == 0 and finish is not None:
                full = finish(gathered)
        else:
            x, sv_m = _odd_fwd(x, wm, s=s, ts=ts)
        x, sv_f = _ffn_fwd(x, wf, s=s, ts=ts)
        saved.append((sv_m, sv_f))

    dy, sq = _loss_head(x, target, s=s, ts=ts)
    loss = 0.5 / D_MODEL * jnp.sum(sq)

    grads = {}

    def put(name, idx, g):
        grads.setdefault(name, {})[idx] = g

    for layer in reversed(range(DEPTH)):
        wm, unpack_m, wf = layers[layer]
        sv_m, sv_f = saved[layer]
        i = layer // 2
        dy, gf = _ffn_bwd(dy, wf, sv_f, s=s, ts=ts)
        put("ffn_w_up", layer, gf["w_up"])
        put("ffn_w_down", layer, gf["w_down"])
        put("ffn_conv_w", layer, gf["conv_w"])
        put("ffn_conv_b", layer, gf["conv_b"][0])
        put("ffn_norm", layer, gf["norm_g"][0])
        if layer % 2 == 0:
            dy, gm, dth = _even_bwd(dy, wm, sv_m, tabs, s=s, ts=ts)
            put("ret_theta_fwd", i, dth[0])
            put("ret_theta_bwd", i, dth[1])
            names = _EVEN_NAMES
        else:
            dy, gm = _odd_bwd(dy, wm, sv_m, s=s, ts=ts)
            names = _ODD_NAMES
        for k, g in unpack_m(gm).items():
            put(names[k], i, g)
    return loss, dy, {n: [g[j] for j in range(len(g))] for n, g in grads.items()}


HBM_SPEC = pl.BlockSpec(memory_space=pltpu.HBM)
VMEM_SPEC = pl.BlockSpec(memory_space=pltpu.VMEM)
CHIPS = 4
CORES = 2
ROW = 8 * LANES


def _xyc():
    return lax.axis_index("x"), lax.axis_index("y"), lax.axis_index("c")


def _other_chips(x, y):
    return [(1 - x, y), (x, 1 - y), (1 - x, 1 - y)]


def _remote(src, dst, send, recv, dev):
    return pltpu.make_async_remote_copy(src_ref=src, dst_ref=dst, send_sem=send, recv_sem=recv,
                                        device_id=dev, device_id_type=MESH)


def _sems(n):
    return pltpu.SemaphoreType.DMA((n,))


def _gather_copies(side, srcs, lands, send, recv, loc):
    n = len(side)
    x, y, c = _xyc()
    me = 2 * x + y
    local, sends, arrivals = [], [], []
    for t, (_, first, count) in enumerate(side):
        src = srcs[t].at[pl.ds(first, count)]
        local.append(pltpu.make_async_copy(src, lands[t].at[me], loc.at[t]))
        for j, (px, py) in enumerate(_other_chips(x, y)):
            k = n * j + t
            sends.append(_remote(src, lands[t].at[me], send.at[k], recv.at[k], (px, py, c)))
            arrivals.append(_remote(src, lands[t].at[2 * px + py], send.at[k], recv.at[k], (px, py, c)))
    return local, sends, arrivals


def _gather_shapes(side):
    return [jax.ShapeDtypeStruct((CHIPS, count) + a.shape[1:], a.dtype) for a, _, count in side]


def _gather_chips(side):
    n = len(side)

    def body(*refs):
        local, sends, arrivals = _gather_copies(side, refs[:n], refs[n:2 * n], *refs[2 * n:])
        for cp in local + sends:
            cp.start()
        for cp in arrivals:
            cp.wait_recv()
        for cp in sends:
            cp.wait_send()
        for cp in local:
            cp.wait()

    return pl.pallas_call(
        body, name="gather_chips", in_specs=[HBM_SPEC] * n, out_specs=[HBM_SPEC] * n,
        out_shape=_gather_shapes(side),
        scratch_shapes=[_sems(3 * n), _sems(3 * n), _sems(n)],
    )(*[a for a, _, _ in side])


def _half_rows(ref, axis, half, which):
    idx = (slice(None),) * axis + (pl.ds(pl.multiple_of(which * half, 8), half),)
    return ref.at[idx]


def _swap_halves(arrs):
    n = len(arrs)

    def body(*refs):
        ins, outs = refs[:n], refs[n:2 * n]
        send, recv = refs[2 * n:]
        x, y, c = _xyc()
        copies = []
        for t in range(n):
            half = arrs[t].shape[2] // CORES
            cp = _remote(_half_rows(ins[t], 2, half, 1 - c), outs[t], send.at[t], recv.at[t], (x, y, 1 - c))
            cp.start()
            copies.append(cp)
        for cp in copies:
            cp.wait()

    return pl.pallas_call(
        body, name="swap_halves", in_specs=[HBM_SPEC] * n, out_specs=[HBM_SPEC] * n,
        out_shape=[jax.ShapeDtypeStruct(a.shape[:2] + (a.shape[2] // CORES, a.shape[3]), a.dtype) for a in arrs],
        scratch_shapes=[_sems(n), _sems(n)],
    )(*arrs)


def _add_core_halves(a, got, core, *, ts, name):
    ch, nl, r, cols = a.shape
    half = r // CORES
    nb = half // ts

    def body(core_ref, a_ref, g_ref, o_ref):
        o_ref[...] = (a_ref[...] + g_ref[...]).astype(o_ref.dtype)

    rows = pl.BlockSpec((ts, cols), lambda g, i, cr: (g * nb + i, 0))
    return pl.pallas_call(
        body, name=name, out_shape=jax.ShapeDtypeStruct((ch * nl * half, cols), BF16),
        grid_spec=pltpu.PrefetchScalarGridSpec(
            num_scalar_prefetch=1, grid=(ch * nl, nb),
            in_specs=[pl.BlockSpec((ts, cols), lambda g, i, cr: (g * (r // ts) + cr[0] * nb + i, 0)), rows],
            out_specs=rows),
        compiler_params=_cp(("arbitrary", "arbitrary")),
    )(core, a.reshape(-1, cols), got.reshape(-1, cols)).reshape(got.shape)


def _add_chip_parts(parts, core, *, ts, name):
    ch, nl, half, cols = parts.shape
    nb = half // ts
    r = half * CORES

    def body(core_ref, *refs):
        acc = refs[0][...].astype(F32)
        for p in refs[1:ch]:
            acc = acc + p[...].astype(F32)
        refs[ch][...] = acc

    return pl.pallas_call(
        body, name=name, out_shape=jax.ShapeDtypeStruct((nl * r, cols), F32),
        grid_spec=pltpu.PrefetchScalarGridSpec(
            num_scalar_prefetch=1, grid=(nl, nb),
            in_specs=[pl.BlockSpec((ts, cols), lambda l, i, cr, j=j: ((j * nl + l) * nb + i, 0)) for j in range(ch)],
            out_specs=pl.BlockSpec((ts, cols), lambda l, i, cr: (l * (r // ts) + cr[0] * nb + i, 0))),
        compiler_params=_cp(("arbitrary", "arbitrary")),
    )(core, *[parts.reshape(-1, cols)] * ch).reshape(nl, r, cols)


def _scatter_chips(arrs):
    n = len(arrs)

    def body(*refs):
        ins, outs = refs[:n], refs[n:2 * n]
        send, recv, loc = refs[2 * n:]
        x, y, c = _xyc()
        me = 2 * x + y
        copies = []
        for t in range(n):
            cp = pltpu.make_async_copy(ins[t].at[me], outs[t].at[me], loc.at[t])
            cp.start()
            copies.append(cp)
        sends = []
        for j, (px, py) in enumerate(_other_chips(x, y)):
            for t in range(n):
                cp = _remote(ins[t].at[2 * px + py], outs[t].at[me], send.at[n * j + t], recv.at[n * j + t], (px, py, c))
                cp.start()
                sends.append(cp)
        for j, (px, py) in enumerate(_other_chips(x, y)):
            for t in range(n):
                _remote(ins[t].at[me], outs[t].at[2 * px + py], send.at[n * j + t], recv.at[n * j + t],
                        (px, py, c)).wait_recv()
        for cp in sends:
            cp.wait_send()
        for cp in copies:
            cp.wait()

    return pl.pallas_call(
        body, name="scatter_chips", in_specs=[HBM_SPEC] * n, out_specs=[HBM_SPEC] * n,
        out_shape=[jax.ShapeDtypeStruct(a.shape, a.dtype) for a in arrs],
        scratch_shapes=[_sems(3 * n), _sems(3 * n), _sems(n)],
    )(*arrs)


def _gather_cores(arrs):
    n = len(arrs)

    def body(*refs):
        ins, outs = refs[:n], refs[n:2 * n]
        send, recv = refs[2 * n:]
        x, y, c = _xyc()
        sends = []
        for t in range(n):
            half = arrs[t].shape[1] // CORES
            cp = _remote(_half_rows(ins[t], 1, half, c), _half_rows(outs[t], 1, half, c), send.at[t], recv.at[t],
                         (x, y, 1 - c))
            cp.start()
            sends.append(cp)
        for t in range(n):
            half = arrs[t].shape[1] // CORES
            _remote(_half_rows(ins[t], 1, half, 1 - c), _half_rows(outs[t], 1, half, 1 - c), send.at[t], recv.at[t],
                    (x, y, 1 - c)).wait_recv()
        for cp in sends:
            cp.wait_send()

    return pl.pallas_call(
        body, name="gather_cores", in_specs=[HBM_SPEC] * n, out_specs=[HBM_SPEC] * n,
        out_shape=[jax.ShapeDtypeStruct(a.shape, a.dtype) for a in arrs],
        input_output_aliases={t: t for t in range(n)},
        scratch_shapes=[_sems(n), _sems(n)],
    )(*arrs)


def _all_reduce_devices(v):
    n_dev = CHIPS * CORES

    def body(v_ref, o_ref, buf, send, recv):
        x, y, c = _xyc()
        me = 4 * x + 2 * y + c
        buf[pl.ds(me, 1)] = v_ref[...][None]
        sends = []
        for m in range(1, n_dev):
            px = 1 - x if m & 4 else x
            py = 1 - y if m & 2 else y
            pc = 1 - c if m & 1 else c
            cp = _remote(v_ref, buf.at[me], send.at[m - 1], recv.at[m - 1], (px, py, pc))
            cp.start()
            sends.append((cp, 4 * px + 2 * py + pc))
        for m, (cp, peer) in enumerate(sends):
            _remote(v_ref, buf.at[peer], send.at[m], recv.at[m], (x, y, c)).wait_recv()
        for cp, _ in sends:
            cp.wait_send()
        acc = buf[0]
        for k in range(1, n_dev):
            acc = acc + buf[k]
        o_ref[...] = acc

    return pl.pallas_call(
        body, name="all_reduce_devices", in_specs=[VMEM_SPEC], out_specs=VMEM_SPEC,
        out_shape=jax.ShapeDtypeStruct(v.shape, F32),
        scratch_shapes=[pltpu.VMEM((n_dev,) + v.shape, F32), pltpu.SemaphoreType.DMA((n_dev - 1,)),
                        pltpu.SemaphoreType.DMA((n_dev - 1,))],
    )(v)


_SHARDED = (("w_in_even", 2), ("mla_w_uq", 2), ("mla_w_ukv", 2), ("w_out_even", 1), ("w_in_odd", 2), ("w_out_odd", 1),
            ("ffn_w_up", 2), ("ffn_w_down", 1),
            ("mix_norm_odd", 1), ("gla_w_gate_fwd", 2), ("gla_b_gate_fwd", 1), ("gla_w_gate_bwd", 2),
            ("gla_b_gate_bwd", 1), ("gla_out_norm", 2), ("ffn_conv_w", 2))
_N_MATRICES = 8
_REPLICATED = ("mix_norm_even", "mla_q_norm", "mla_kv_norm", "mla_q_head_norm", "mla_k_head_norm", "ret_theta_fwd",
               "ret_theta_bwd", "ret_out_norm", "ffn_norm", "ffn_conv_b")
_WEIGHTS = ("mix_norm_even", "w_in_even", "mla_q_norm", "mla_kv_norm", "mla_w_uq", "mla_w_ukv", "mla_q_head_norm",
            "mla_k_head_norm", "ret_theta_fwd", "ret_theta_bwd", "ret_out_norm", "w_out_even", "mix_norm_odd",
            "w_in_odd", "gla_w_gate_fwd", "gla_b_gate_fwd", "gla_w_gate_bwd", "gla_b_gate_bwd", "gla_out_norm",
            "w_out_odd", "ffn_norm", "ffn_w_up", "ffn_conv_w", "ffn_conv_b", "ffn_w_down")


def _flatten(arrs, row_multiple, dtype):
    flat = jnp.concatenate([a.reshape(-1).astype(dtype) for a in arrs])
    per = ROW * row_multiple
    total = -(-flat.shape[0] // per) * per
    return jnp.pad(flat, (0, total - flat.shape[0])).reshape(-1, ROW)


def _unflatten(flat, shapes):
    flat = flat.reshape(-1)
    out, o = [], 0
    for shp in shapes:
        n = math.prod(shp)
        out.append(flat[o:o + n].reshape(shp))
        o += n
    return out


def kernel(x, positions, mix_norm_even, w_in_even, mla_q_norm, mla_kv_norm, mla_w_uq, mla_w_ukv, mla_q_head_norm, mla_k_head_norm, ret_theta_fwd, ret_theta_bwd, ret_out_norm, w_out_even, mix_norm_odd, w_in_odd, gla_w_gate_fwd, gla_b_gate_fwd, gla_w_gate_bwd, gla_b_gate_bwd, gla_out_norm, w_out_odd, ffn_norm, ffn_w_up, ffn_conv_w, ffn_conv_b, ffn_w_down, loss_target, m_mix_norm_even, m_w_in_even, m_mla_q_norm, m_mla_kv_norm, m_mla_w_uq, m_mla_w_ukv, m_mla_q_head_norm, m_mla_k_head_norm, m_ret_theta_fwd, m_ret_theta_bwd, m_ret_out_norm, m_w_out_even, m_mix_norm_odd, m_w_in_odd, m_gla_w_gate_fwd, m_gla_b_gate_fwd, m_gla_w_gate_bwd, m_gla_b_gate_bwd, m_gla_out_norm, m_w_out_odd, m_ffn_norm, m_ffn_w_up, m_ffn_conv_w, m_ffn_conv_b, m_ffn_w_down, v_mix_norm_even, v_w_in_even, v_mla_q_norm, v_mla_kv_norm, v_mla_w_uq, v_mla_w_ukv, v_mla_q_head_norm, v_mla_k_head_norm, v_ret_theta_fwd, v_ret_theta_bwd, v_ret_out_norm, v_w_out_even, v_mix_norm_odd, v_w_in_odd, v_gla_w_gate_fwd, v_gla_b_gate_fwd, v_gla_w_gate_bwd, v_gla_b_gate_bwd, v_gla_out_norm, v_w_out_odd, v_ffn_norm, v_ffn_w_up, v_ffn_conv_w, v_ffn_conv_b, v_ffn_w_down):
    args = dict(locals())
    x2, pos, target = args["x"][0], args["positions"][0], args["loss_target"][0]
    axis = dict(_SHARDED)
    mats = [n for n, _ in _SHARDED[:_N_MATRICES]]
    smalls = [n for n, _ in _SHARDED[_N_MATRICES:]]
    small_shapes = [args[n].shape for n in smalls]

    local = {n: _bf(args[n]) for n in mats}
    first_layers = {n: (0, 0 if n.endswith("_odd") else 1) for n in mats}
    now = [(local[n],) + first_layers[n] for n in mats if first_layers[n][1]]
    later = [(local[n], first_layers[n][1], args[n].shape[0] - first_layers[n][1]) for n in mats]
    small_block = _flatten([args[n] for n in smalls], 2 * HALO, F32)
    got_now = _gather_chips(now + [(small_block, 0, small_block.shape[0])])
    per_chip = [_unflatten(got_now[-1][j], small_shapes) for j in range(CHIPS)]
    base = {n: args[n] for n in _REPLICATED}
    for k, n in enumerate(smalls):
        base[n] = jnp.concatenate([per_chip[j][k] for j in range(CHIPS)], axis=axis[n])

    def whole(stacks):
        full = dict(base)
        for n, per_layer in stacks.items():
            if n == "ffn_w_up":
                full[n] = per_layer
            else:
                full[n] = [None if st is None else jnp.concatenate([st[j, l] for j in range(CHIPS)], axis=axis[n] - 1)
                           for st, l in per_layer]
        return full

    stacks = {n: [(None, 0)] * args[n].shape[0] for n in mats}
    for (a, first, count), st in zip(now, got_now):
        n = next(m for m in mats if local[m] is a)
        stacks[n] = [(st, l) for l in range(count)] + stacks[n][count:]

    def finish(got_later):
        for (a, first, count), st in zip(later, got_later):
            n = next(m for m in mats if local[m] is a)
            stacks[n] = stacks[n][:first] + [(st, l) for l in range(count)]
        return whole(stacks)

    loss, grad_x, grads = _local_step(x2, pos, target, whole(stacks), side=later, finish=finish)
    loss = lax.psum(loss, ("x", "y", "c"))

    def by_chip(n, g):
        if n == "ffn_w_up":
            return g
        if axis[n] == 1:
            return g.reshape((CHIPS, g.shape[0] // CHIPS) + g.shape[1:])
        return jnp.stack(jnp.split(g, CHIPS, axis=axis[n] - 1))

    core = lax.axis_index("c").astype(jnp.int32).reshape(1)
    stacked = [jnp.stack([by_chip(n, g) for g in grads[n]], axis=1) for n in mats]
    small_parts = [jnp.split(jnp.stack(grads[n]), CHIPS, axis=axis[n]) for n in smalls]
    stacked.append(jnp.stack([_flatten([p[j] for p in small_parts], 2 * HALO, F32) for j in range(CHIPS)])[:, None])
    names = mats + ["small"]
    tiles = [_rows_tile(a.shape[2] // CORES, a.shape[3]) for a in stacked]
    got = _swap_halves(stacked)
    chip_sums = [_add_core_halves(a, b, core, ts=ts, name="add_core_halves_" + n)
                 for n, a, b, ts in zip(names, stacked, got, tiles)]
    parts = _scatter_chips(chip_sums)
    sums = [_add_chip_parts(p, core, ts=ts, name="add_chip_parts_" + n) for n, p, ts in zip(names, parts, tiles)]
    reduced = _gather_cores(sums)

    res = {}

    def update(n, w, g, m, v, ts):
        cols = g.shape[-1]
        outs = _adamw(w.reshape(-1, cols), g.reshape(-1, cols), m.reshape(-1, cols), v.reshape(-1, cols), ts=ts,
                      name="adamw_" + n)
        return [g] + [o.reshape(g.shape) for o in outs]

    kinds = ("grad", "delta", "new_m", "new_v")
    for n, g, ts in zip(mats, reduced, tiles):
        for kind, a in zip(kinds, update(n, args[n], g, args["m_" + n], args["v_" + n], ts)):
            res[kind + "_" + n] = a
    w_s, m_s, v_s = (_flatten([args[pre + n] for n in smalls], 2 * HALO, F32) for pre in ("", "m_", "v_"))
    for kind, flat in zip(kinds, update("small", w_s, reduced[-1][0], m_s, v_s, tiles[-1])):
        for n, a in zip(smalls, _unflatten(flat, small_shapes)):
            res[kind + "_" + n] = a

    rep_shapes = [args[n].shape for n in _REPLICATED]
    g_rep = _all_reduce_devices(_flatten([jnp.stack(grads[n]) for n in _REPLICATED], HALO, F32))
    w_rep, m_rep, v_rep = (_flatten([args[pre + n] for n in _REPLICATED], HALO, F32) for pre in ("", "m_", "v_"))
    for kind, flat in zip(kinds, update("replicated", w_rep, g_rep, m_rep, v_rep, g_rep.shape[0])):
        for n, a in zip(_REPLICATED, _unflatten(flat, rep_shapes)):
            res[kind + "_" + n] = a

    outs = [loss, grad_x[None]]
    for kind in ("grad", "delta", "new_m", "new_v"):
        outs += [res[kind + "_" + n] for n in _WEIGHTS]
    return tuple(outs)
```

```python
import math

import jax
import jax.numpy as jnp
from jax import lax
from jax.experimental import pallas as pl
from jax.experimental.pallas import tpu as pltpu

F32 = jnp.float32
BF16 = jnp.bfloat16
MESH = pl.DeviceIdType.MESH

EPS = 1e-6
D_MODEL = 1024
DEPTH = 4
LANES = 128
MLA_H, MLA_QR, MLA_KVR, MLA_NOPE, MLA_ROPE, MLA_V = 8, 384, 256, 64, 32, 64
MLA_QK = MLA_NOPE + MLA_ROPE
MLA_SCALE = MLA_QK ** -0.5
RET_H, RET_DK, RET_DV, RET_C = 8, 64, 64, 128
GLA_H, GLA_DK, GLA_DV, GLA_R, GLA_TAU, GLA_C = 4, 128, 256, 16, 16.0, 64
D_FF = 2816
ROPE_THETA = 10000.0
LN2 = math.log(2.0)
ADAM_LR, ADAM_B1, ADAM_B2, ADAM_EPS, ADAM_WD, ADAM_STEP = 0.001, 0.9, 0.999, 1e-08, 0.01, 10

EV_RET = 4 * RET_H * LANES
EV_CQ = 512
EV_W = 5120
EV_KR_BLK = (EV_RET + EV_CQ + MLA_KVR) // LANES
OD_W = 3200
OD_GA_BLK = 3072 // LANES

VMEM_LIMIT = 56 * 1024 * 1024
MM_TILE_CAP = 1408
V_ONES = (MLA_V, MLA_V + 1)
FLASH_FWD_ROWS = 1024
FLASH_BWD_ROWS = 1024
FLASH_KEYS = 1024


def _cp(sem):
    return pltpu.CompilerParams(dimension_semantics=sem, vmem_limit_bytes=VMEM_LIMIT)


def _dot(a, b):
    return jnp.dot(a, b, preferred_element_type=F32)


def _dot_nt(a, b):
    return lax.dot_general(a, b, (((1,), (1,)), ((), ())), preferred_element_type=F32)


def _dot_tn(a, b):
    return lax.dot_general(a, b, (((0,), (0,)), ((), ())), preferred_element_type=F32)


def _bf(x):
    return x.astype(BF16)


def _split3(x):
    h1 = _bf(x)
    r1 = x - h1.astype(F32)
    h2 = _bf(r1)
    h3 = _bf(r1 - h2.astype(F32))
    return h1, h2, h3


def _tile(n, cap):
    if n <= cap:
        return n
    best = None
    for t in range(LANES, cap + 1, LANES):
        if n % t == 0:
            best = t
    assert best is not None, n
    return best


def _mm(a, b, *, ta=False, tb=False, res=None, out_dtype=F32, b_layer=None, out_chips=False, name):
    assert not (ta and tb)
    if ta:
        kdim, m = a.shape
    else:
        m, kdim = a.shape
    if b_layer is not None:
        rows_b, cols_b = b.shape[2], b.shape[0] * b.shape[3]
    else:
        rows_b, cols_b = b.shape
    n, kb = (rows_b, cols_b) if tb else (cols_b, rows_b)
    assert kb == kdim, (a.shape, b.shape, ta, tb)
    tm, tn, tk = _tile(m, MM_TILE_CAP), _tile(n, MM_TILE_CAP), _tile(kdim, MM_TILE_CAP)
    nk = kdim // tk
    has_res = res is not None
    vmem = (2 * tm * tk * a.dtype.itemsize + 2 * tk * tn * b.dtype.itemsize
            + 2 * tm * tn * jnp.dtype(out_dtype).itemsize + (2 * tm * tn * 4 if has_res else 0)
            + (tm * tn * 4 if nk > 1 else 0))
    assert vmem <= VMEM_LIMIT - 8 * 1024 * 1024, (name, vmem)
    a_spec = (pl.BlockSpec((tk, tm), lambda i, j, k: (k, i)) if ta
              else pl.BlockSpec((tm, tk), lambda i, j, k: (i, k)))
    if b_layer is not None:
        per_chip = b.shape[3]
        if tb:
            assert tk == per_chip
            b_spec = pl.BlockSpec((None, None, tn, tk), lambda i, j, k: (k, b_layer, j, 0))
        else:
            assert tn == per_chip
            b_spec = pl.BlockSpec((None, None, tk, tn), lambda i, j, k: (j, b_layer, k, 0))
    else:
        b_spec = (pl.BlockSpec((tn, tk), lambda i, j, k: (j, k)) if tb
                  else pl.BlockSpec((tk, tn), lambda i, j, k: (k, j)))
    if out_chips:
        assert n // tn == CHIPS and not has_res
        o_spec = pl.BlockSpec((None, tm, tn), lambda i, j, k: (j, i, 0))
        out_struct = jax.ShapeDtypeStruct((CHIPS, m, tn), out_dtype)
    else:
        o_spec = pl.BlockSpec((tm, tn), lambda i, j, k: (i, j))
        out_struct = jax.ShapeDtypeStruct((m, n), out_dtype)

    def product(a_ref, b_ref):
        av, bv = _bf(a_ref[...]), _bf(b_ref[...])
        if ta:
            return _dot_tn(av, bv)
        if tb:
            return _dot_nt(av, bv)
        return _dot(av, bv)

    def body(*refs):
        a_ref, b_ref = refs[:2]
        r_ref = refs[2] if has_res else None
        o_ref = refs[3] if has_res else refs[2]

        def finish(r):
            if has_res:
                r = r + r_ref[...]
            o_ref[...] = r.astype(o_ref.dtype)

        if nk == 1:
            finish(product(a_ref, b_ref))
            return
        acc = refs[-1]
        k = pl.program_id(2)

        @pl.when(k == 0)
        def _():
            acc[...] = product(a_ref, b_ref)

        @pl.when(k > 0)
        def _():
            acc[...] += product(a_ref, b_ref)

        @pl.when(k == nk - 1)
        def _():
            finish(acc[...])

    ins = [a, b] + ([res] if has_res else [])
    in_specs = [a_spec, b_spec] + ([o_spec] if has_res else [])
    return pl.pallas_call(
        body, name=name, grid=(m // tm, n // tn, nk),
        in_specs=in_specs, out_specs=o_spec, out_shape=out_struct,
        scratch_shapes=[pltpu.VMEM((tm, tn), F32)] if nk > 1 else [],
        compiler_params=_cp(("parallel", "parallel", "arbitrary")),
    )(*ins)


def _ew(fn, rows, pars, outs, accs=(), *, s, ts, name):
    n_in = len(rows) + len(pars)
    n_o = len(outs)

    def body(*refs):
        i = pl.program_id(0)
        vals = fn(*[r[...] for r in refs[:n_in]])
        if not isinstance(vals, (tuple, list)):
            vals = (vals,)
        assert len(vals) == n_o + len(accs), (name, len(vals))
        for r, v in zip(refs[n_in:n_in + n_o], vals[:n_o]):
            r[...] = v.astype(r.dtype)
        for r, v in zip(refs[n_in + n_o:], vals[n_o:]):
            @pl.when(i == 0)
            def _(r=r, v=v):
                r[...] = v

            @pl.when(i > 0)
            def _(r=r, v=v):
                r[...] += v

    in_specs = [sp for _, sp in rows]
    in_specs += [pl.BlockSpec(p.shape, lambda i, nd=p.ndim: (0,) * nd) for p in pars]
    out_specs = [pl.BlockSpec((ts, w), lambda i: (i, 0)) for w, _ in outs]
    out_specs += [pl.BlockSpec((r, w), lambda i: (0, 0)) for r, w in accs]
    out_shape = [jax.ShapeDtypeStruct((s, w), dt) for w, dt in outs]
    out_shape += [jax.ShapeDtypeStruct((r, w), F32) for r, w in accs]
    return pl.pallas_call(
        body, name=name, grid=(s // ts,), in_specs=in_specs, out_specs=out_specs, out_shape=out_shape,
        compiler_params=_cp(("arbitrary",)),
    )(*[a for a, _ in rows], *pars)


def _cols(arr, width, blk, ts):
    return (arr, pl.BlockSpec((ts, width), lambda i, b=blk: (i, b)))


def _lead(pair, d, ts):
    return _cols(pair[d], pair[d].shape[1], 0, ts)


def _rowsum(x):
    return jnp.sum(x, axis=0, keepdims=True)


def _lanesum(x):
    return jnp.sum(x, axis=-1, keepdims=True)


def _gsum(x, group):
    w = x.shape[-1]
    if group == w:
        return jnp.broadcast_to(_lanesum(x), x.shape)
    parts = [jnp.broadcast_to(_lanesum(x[:, g:g + group]), (x.shape[0], group)) for g in range(0, w, group)]
    return jnp.concatenate(parts, axis=-1)


def _gn(x, gain, group, n):
    rstd = lax.rsqrt(_gsum(x * x, group) * (1.0 / n) + EPS)
    xn = x * rstd
    return xn * gain, xn, rstd


def _gn_bwd(dy, xn, rstd, gain, group, n):
    dxn = dy * gain
    dx = rstd * (dxn - xn * (_gsum(dxn * xn, group) * (1.0 / n)))
    return dx, _rowsum(dy * xn)


def _sigmoid(x):
    return 1.0 / (1.0 + jnp.exp(-x))


def _rmsnorm(x_row, g, *, n, s, ts, name):
    w = g.shape[-1]

    def fn(x, gv):
        return _gn(x, gv, w, n)[0]

    return _ew(fn, [x_row], [g], [(w, BF16)], s=s, ts=ts, name=name)[0]


def _rmsnorm_bwd(x_row, g, dh, dres, *, n, s, ts, name):
    w = g.shape[-1]
    has_res = dres is not None

    def fn(x, dhv, *rest):
        gv = rest[-1]
        _, xn, rstd = _gn(x, gv, w, n)
        dx, dg = _gn_bwd(dhv, xn, rstd, gv, w, n)
        if has_res:
            dx = dx + rest[0]
        return dx, dg

    rows = [x_row, _cols(dh, w, 0, ts)] + ([_cols(dres, w, 0, ts)] if has_res else [])
    return _ew(fn, rows, [g], [(w, F32)], [(1, w)], s=s, ts=ts, name=name)


def _rope_tables(pos, real, offset):
    half = real // 2
    inv = ROPE_THETA ** (-jnp.arange(half, dtype=F32) / half)
    ang = pos.astype(F32)[:, None] * inv
    c, sn = jnp.cos(ang), jnp.sin(ang)
    s = pos.shape[0]
    cos_t = jnp.concatenate([jnp.ones((s, offset), F32), c, c,
                             jnp.ones((s, LANES - offset - real), F32)], axis=1)
    sin_t = jnp.concatenate([jnp.zeros((s, offset), F32), -sn, sn,
                             jnp.zeros((s, LANES - offset - real), F32)], axis=1)
    return cos_t, sin_t


def _rope(x, cos_t, sin_t, real, offset):
    half = real // 2
    lane = lax.broadcasted_iota(jnp.int32, x.shape, 1)
    partner = jnp.where(lane < offset + half, pltpu.roll(x, LANES - half, 1), pltpu.roll(x, half, 1))
    return x * cos_t + partner * sin_t


def _mla_prep(q_pre, kv_pre, p_even, cos_m, sin_m, qhn, khn, *, s, ts):
    w = MLA_H * LANES

    def fn(qp, kp, vp, kr, c, sn, gq, gk):
        qs, ks = [], []
        for h in range(MLA_H):
            sl = slice(h * LANES, (h + 1) * LANES)
            qn = _gn(qp[:, sl], gq, LANES, MLA_QK)[0]
            kn = _gn(kp[:, sl] + kr, gk, LANES, MLA_QK)[0]
            qs.append(_rope(qn, c, sn, MLA_ROPE, MLA_NOPE) * MLA_SCALE)
            ks.append(_rope(kn, c, sn, MLA_ROPE, MLA_NOPE))
        lane = lax.broadcasted_iota(jnp.int32, vp.shape, 1) % LANES
        ones = (lane == V_ONES[0]) | (lane == V_ONES[1])
        return jnp.concatenate(qs, axis=1), jnp.concatenate(ks, axis=1), jnp.where(ones, 1.0, vp)

    rows = [_cols(q_pre, w, 0, ts), _cols(kv_pre, w, 0, ts), _cols(kv_pre, w, 1, ts),
            _cols(p_even, LANES, EV_KR_BLK, ts), _cols(cos_m, LANES, 0, ts), _cols(sin_m, LANES, 0, ts)]
    return _ew(fn, rows, [qhn, khn], [(w, BF16)] * 3, s=s, ts=ts, name="mla_prep")


def _mla_prep_bwd(q_pre, kv_pre, p_even, cos_m, sin_m, qhn, khn, dq, dk, *, s, ts):
    w = MLA_H * LANES

    def fn(qp, kp, kr, c, sn, dqv, dkv, gq, gk):
        dqs, dks = [], []
        dkr = jnp.zeros_like(kr)
        dgq = jnp.zeros((1, LANES), F32)
        dgk = jnp.zeros((1, LANES), F32)
        for h in range(MLA_H):
            sl = slice(h * LANES, (h + 1) * LANES)
            _, qn, qr = _gn(qp[:, sl], gq, LANES, MLA_QK)
            _, kn, krs = _gn(kp[:, sl] + kr, gk, LANES, MLA_QK)
            dqn = _rope(dqv[:, sl] * MLA_SCALE, c, -sn, MLA_ROPE, MLA_NOPE)
            dkn = _rope(dkv[:, sl], c, -sn, MLA_ROPE, MLA_NOPE)
            dqh, g1 = _gn_bwd(dqn, qn, qr, gq, LANES, MLA_QK)
            dkh, g2 = _gn_bwd(dkn, kn, krs, gk, LANES, MLA_QK)
            dqs.append(dqh)
            dks.append(dkh)
            dkr = dkr + dkh
            dgq = dgq + g1
            dgk = dgk + g2
        return jnp.concatenate(dqs, axis=1), jnp.concatenate(dks, axis=1), dkr, dgq, dgk

    rows = [_cols(q_pre, w, 0, ts), _cols(kv_pre, w, 0, ts), _cols(p_even, LANES, EV_KR_BLK, ts),
            _cols(cos_m, LANES, 0, ts), _cols(sin_m, LANES, 0, ts), _cols(dq, w, 0, ts), _cols(dk, w, 0, ts)]
    return _ew(fn, rows, [qhn, khn], [(w, BF16), (w, BF16), (LANES, BF16)], [(1, LANES), (1, LANES)],
               s=s, ts=ts, name="mla_prep_bwd")


def _flash_fwd(q, k, v, *, tq, tk, side=()):
    s = q.shape[0]
    nq, nk = s // tq, s // tk
    rq = tq
    ns = len(side)

    def body(*refs):
        q_ref, k_ref, v_ref = refs[:3]
        o_ref, lse_ref = refs[3 + ns:5 + ns]
        m_s, acc = refs[5 + 2 * ns:7 + 2 * ns]
        h, i, j = pl.program_id(0), pl.program_id(1), pl.program_id(2)
        if ns:
            local, sends, arrivals = _gather_copies(side, refs[3:3 + ns], refs[5 + ns:5 + 2 * ns], *refs[7 + 2 * ns:])

            @pl.when((h == 0) & (i == 0) & (j == 0))
            def _():
                for cp in local + sends:
                    cp.start()

        @pl.when(j == 0)
        def _():
            m_s[...] = jnp.full_like(m_s, -jnp.inf)
            acc[...] = jnp.zeros_like(acc)

        kv, vv = k_ref[...], v_ref[...]
        for r in range(0, tq, rq):
            rows = slice(r, r + rq)
            sc = _dot_nt(q_ref[rows, :], kv)
            m_prev = m_s[rows, :]
            m_new = jnp.maximum(m_prev, jnp.max(sc, axis=-1, keepdims=True))
            p = jnp.exp(sc - jnp.tile(m_new, (1, tk // LANES)))
            acc[rows, :] = jnp.exp(m_prev - m_new) * acc[rows, :] + _dot(_bf(p), vv)
            m_s[rows, :] = m_new

        @pl.when(j == nk - 1)
        def _():
            a = acc[...]
            l = a[:, V_ONES[0]:V_ONES[0] + 1]
            o_ref[...] = (a / l).astype(o_ref.dtype)
            lse_ref[...] = m_s[:, 0:1] + jnp.log(l)

        if ns:
            @pl.when((h == MLA_H - 1) & (i == nq - 1) & (j == nk - 1))
            def _():
                for cp in arrivals:
                    cp.wait_recv()
                for cp in sends:
                    cp.wait_send()
                for cp in local:
                    cp.wait()

    qs = pl.BlockSpec((tq, LANES), lambda h, i, j: (i, h))
    ks = pl.BlockSpec((tk, LANES), lambda h, i, j: (j, h))
    outs = pl.pallas_call(
        body, name="mla_flash_fwd_gather" if ns else "mla_flash_fwd", grid=(MLA_H, nq, nk),
        in_specs=[qs, ks, ks] + [HBM_SPEC] * ns,
        out_specs=[qs, pl.BlockSpec((None, tq, 1), lambda h, i, j: (h, i, 0))] + [HBM_SPEC] * ns,
        out_shape=[jax.ShapeDtypeStruct((s, MLA_H * LANES), BF16), jax.ShapeDtypeStruct((MLA_H, s, 1), F32)]
        + _gather_shapes(side),
        scratch_shapes=[pltpu.VMEM((tq, LANES), F32), pltpu.VMEM((tq, LANES), F32)]
        + ([_sems(3 * ns), _sems(3 * ns), _sems(ns)] if ns else []),
        compiler_params=_cp(("arbitrary",) * 3 if ns else ("parallel", "parallel", "arbitrary")),
    )(q, k, v, *[a for a, _, _ in side])
    return outs[0], outs[1], list(outs[2:])


def _attn_bwd_prep(dar, o, *, s, ts):
    w = MLA_H * LANES

    def fn(dov, ov):
        outs = []
        lane = lax.broadcasted_iota(jnp.int32, (dov.shape[0], LANES), 1)
        for h in range(MLA_H):
            sl = slice(h * LANES, (h + 1) * LANES)
            d = dov[:, sl]
            delta = _lanesum(d * ov[:, sl].astype(F32))
            hi = _bf(delta).astype(F32)
            outs.append(jnp.where(lane == V_ONES[0], -hi, jnp.where(lane == V_ONES[1], hi - delta, d)))
        return jnp.concatenate(outs, axis=1)

    return _ew(fn, [_cols(dar, w, 0, ts), _cols(o, w, 0, ts)], [], [(w, BF16)], s=s, ts=ts,
               name="mla_attn_bwd_prep")[0]


def _flash_bwd(q, k, v, do, lse, *, tq, tk):
    s = q.shape[0]
    nq, nk = s // tq, s // tk
    rq = tq

    def body(q_ref, k_ref, v_ref, do_ref, lse_ref, dq_ref, dk_ref, dv_ref, dk_acc, dv_acc):
        j = pl.program_id(1)
        i = pl.program_id(2)
        kv, vv = k_ref[...], v_ref[...]
        dv_c = dk_c = None
        dq_parts = []
        for r in range(0, tq, rq):
            qv, dov = q_ref[r:r + rq, :], do_ref[r:r + rq, :]
            p = jnp.exp(_dot_nt(qv, kv) - lse_ref[r:r + rq, :])
            ds = _bf(p * _dot_nt(dov, vv))
            dv_r = _dot_tn(_bf(p), dov)
            dk_r = _dot_tn(ds, qv)
            dv_c = dv_r if dv_c is None else dv_c + dv_r
            dk_c = dk_r if dk_c is None else dk_c + dk_r
            dq_parts.append(_dot(ds, kv))
        dq_c = jnp.concatenate(dq_parts, axis=0)
        rows = pl.ds(pl.multiple_of(i * tq, tq), tq)

        @pl.when(i == 0)
        def _():
            dk_acc[...] = dk_c
            dv_acc[...] = dv_c

        @pl.when(i > 0)
        def _():
            dk_acc[...] += dk_c
            dv_acc[...] += dv_c

        @pl.when(j == 0)
        def _():
            dq_ref[rows, :] = dq_c

        @pl.when(j > 0)
        def _():
            dq_ref[rows, :] += dq_c

        @pl.when(i == nq - 1)
        def _():
            dk_ref[...] = dk_acc[...]
            dv_ref[...] = dv_acc[...].astype(dv_ref.dtype)

    qs = pl.BlockSpec((tq, LANES), lambda h, j, i: (i, h))
    ks = pl.BlockSpec((tk, LANES), lambda h, j, i: (j, h))
    st = pl.BlockSpec((None, tq, 1), lambda h, j, i: (h, i, 0))
    return pl.pallas_call(
        body, name="mla_flash_bwd", grid=(MLA_H, nk, nq),
        in_specs=[qs, ks, ks, qs, st],
        out_specs=[pl.BlockSpec((s, LANES), lambda h, j, i: (0, h)), ks, ks],
        out_shape=[jax.ShapeDtypeStruct((s, MLA_H * LANES), F32), jax.ShapeDtypeStruct((s, MLA_H * LANES), F32),
                   jax.ShapeDtypeStruct((s, MLA_H * LANES), BF16)],
        scratch_shapes=[pltpu.VMEM((tk, LANES), F32), pltpu.VMEM((tk, LANES), F32)],
        compiler_params=_cp(("parallel", "arbitrary", "arbitrary")),
    )(q, k, v, do, lse)


def _ret_geometry(d, c):
    df = float(d)
    ii = lax.broadcasted_iota(jnp.int32, (c, c), 0).astype(F32)
    jj = lax.broadcasted_iota(jnp.int32, (c, c), 1).astype(F32)
    rel = (ii - jj) * (1.0 - 2.0 * df)
    mask = rel >= df
    rel0 = jnp.maximum(rel, 0.0)
    pos = lax.broadcasted_iota(jnp.int32, (c, 1), 0).astype(F32)
    ez = (c - 1 - pos) + df * (2.0 * pos - (c - 1))
    ex = (pos + 1.0) + df * (c - 1 - 2.0 * pos)
    return mask, rel0, ez, ex


def _chunk_index(n_chunks):
    return lambda d, n: n + d * (n_chunks - 1 - 2 * n)


def _ret_fwd(p_even, cos_r, sin_r, theta_l):
    s = p_even.shape[0]
    c = RET_C
    n_chunks = s // c
    w = RET_H * LANES
    cidx = _chunk_index(n_chunks)

    def body(*refs):
        n = pl.program_id(0)

        @pl.when(n == 0)
        def _():
            for r_s in refs[16:18]:
                r_s[...] = jnp.zeros_like(r_s)

        stores = []
        for d in range(2):
            stores += one(d, *refs[6 * d:6 * d + 6], *refs[12 + 2 * d:14 + 2 * d], refs[16 + d])
        for ref, val in stores:
            ref[...] = val

    def one(d, q_ref, k_ref, v_ref, cos_ref, sin_ref, th_ref, o_ref, rp_ref, r_s):
        lg = jnp.log1p(-jnp.exp(-th_ref[...] * LN2))
        mask, rel0, ez, ex = _ret_geometry(d, c)
        cs, sn = cos_ref[...], sin_ref[...]
        r_all = r_s[...]
        outs, states = [], []
        for h in range(RET_H):
            sl = slice(h * LANES, (h + 1) * LANES)
            lgh = lg[:, h * LANES:h * LANES + 1]
            dm = jnp.where(mask, jnp.exp(lgh * rel0), 0.0)
            qh = _bf(_rope(q_ref[:, sl], cs, sn, RET_DK, 0))
            kf = _rope(k_ref[:, sl], cs, sn, RET_DK, 0) * (RET_DK ** -0.5)
            kh = _bf(kf)
            vh = _bf(v_ref[:, sl])
            rh = r_all[sl, :]
            a = _dot_nt(qh, kh) * dm
            outs.append(_dot(_bf(a), vh) + jnp.exp(lgh * ex) * _dot(qh, _bf(rh)))
            zk = _bf(kf * jnp.exp(lgh * ez))
            states.append(jnp.exp(lgh * c) * rh + _dot_tn(zk, vh))
        return [(rp_ref, r_all), (o_ref, jnp.concatenate(outs, axis=1)), (r_s, jnp.concatenate(states, axis=0))]

    def ins(d):
        col = lambda blk: pl.BlockSpec((c, w), lambda n: (cidx(d, n), blk))
        tab = pl.BlockSpec((c, LANES), lambda n: (cidx(d, n), 0))
        return [col(0), col(1), col(2), tab, tab, pl.BlockSpec((None, 1, w), lambda n: (d, 0, 0))]

    def outs(d):
        return [pl.BlockSpec((c, w), lambda n: (cidx(d, n), 0)),
                pl.BlockSpec((None, w, LANES), lambda n: (cidx(d, n), 0, 0))]

    o_f, r_f, o_b, r_b = pl.pallas_call(
        body, name="ret_fwd", grid=(n_chunks,),
        in_specs=ins(0) + ins(1), out_specs=outs(0) + outs(1),
        out_shape=[jax.ShapeDtypeStruct((s, w), F32), jax.ShapeDtypeStruct((n_chunks, w, LANES), F32)] * 2,
        scratch_shapes=[pltpu.VMEM((w, LANES), F32)] * 2,
        compiler_params=_cp(("arbitrary",)),
    )(*[p_even, p_even, p_even, cos_r, sin_r, theta_l] * 2)
    return (o_f, o_b), (r_f, r_b)


def _ret_bwd(p_even, cos_r, sin_r, theta_l, theta_h, r_prev, do):
    s = p_even.shape[0]
    c = RET_C
    n_chunks = s // c
    w = RET_H * LANES
    fwd_idx = _chunk_index(n_chunks)

    def cidx(d, n):
        return fwd_idx(d, n_chunks - 1 - n)

    def body(*refs):
        n = pl.program_id(0)

        @pl.when(n == 0)
        def _():
            for d in range(2):
                refs[26 + d][...] = jnp.zeros_like(refs[26 + d])
                refs[21 + 4 * d][...] = jnp.zeros_like(refs[21 + 4 * d])

        stores = []
        for d in range(2):
            stores += one(d, *refs[9 * d:9 * d + 9], *refs[18 + 4 * d:22 + 4 * d], refs[26 + d])
        for ref, val, accumulate in stores:
            if accumulate:
                ref[...] += val
            else:
                ref[...] = val

    def one(d, q_ref, k_ref, v_ref, cos_ref, sin_ref, th_ref, thh_ref, rp_ref, do_ref,
            dq_ref, dk_ref, dv_ref, dth_ref, dr_s):
        lg = jnp.log1p(-jnp.exp(-th_ref[...] * LN2))
        mask, rel0, ez, ex = _ret_geometry(d, c)
        cs, sn = cos_ref[...], sin_ref[...]
        rp_all, dr_all = rp_ref[...], dr_s[...]
        row = lax.broadcasted_iota(jnp.int32, (RET_H, LANES), 0)
        dlg = jnp.zeros((RET_H, LANES), F32)
        kscale = RET_DK ** -0.5
        dqs, dks, dvs, drs = [], [], [], []
        for h in range(RET_H):
            sl = slice(h * LANES, (h + 1) * LANES)
            lgh = lg[:, h * LANES:h * LANES + 1]
            dm = jnp.where(mask, jnp.exp(lgh * rel0), 0.0)
            zeta = jnp.exp(lgh * ez)
            xi = jnp.exp(lgh * ex)
            gc = jnp.exp(lgh * c)
            qf = _rope(q_ref[:, sl], cs, sn, RET_DK, 0)
            qh = _bf(qf)
            kf = _rope(k_ref[:, sl], cs, sn, RET_DK, 0) * kscale
            kh = _bf(kf)
            zkf = kf * zeta
            zk = _bf(zkf)
            vh = _bf(v_ref[:, sl])
            dof = do_ref[:, sl]
            doh = _bf(dof)
            rp = rp_all[sl, :]
            rpb = _bf(rp)
            drn = dr_all[sl, :]
            drb = _bf(drn)
            a = _dot_nt(qh, kh) * dm
            da0 = _dot_nt(doh, vh)
            da = _bf(da0 * dm)
            vdr = _dot_nt(vh, drb)
            dq_r = _dot(da, kh) + xi * _dot_nt(doh, rpb)
            dk_r = _dot_tn(da, qh) + zeta * vdr
            dvs.append(_dot_tn(_bf(a), doh) + _dot(zk, drb))
            dqs.append(_rope(dq_r, cs, -sn, RET_DK, 0))
            dks.append(_rope(dk_r * kscale, cs, -sn, RET_DK, 0))
            drs.append(_dot_tn(_bf(qf * xi), doh) + gc * drn)
            ocross = xi * _dot(qh, rpb)
            t = (jnp.sum(rel0 * a * da0, keepdims=True)
                 + jnp.sum(ex * dof * ocross, keepdims=True)
                 + c * gc * jnp.sum(drn * rp, keepdims=True)
                 + jnp.sum(ez * zkf * vdr, keepdims=True))
            dlg = jnp.where(row == h, t, dlg)
        x2 = jnp.exp(-thh_ref[...] * LN2)
        return [(dq_ref, jnp.concatenate(dqs, axis=1), False), (dk_ref, jnp.concatenate(dks, axis=1), False),
                (dv_ref, jnp.concatenate(dvs, axis=1), False), (dr_s, jnp.concatenate(drs, axis=0), False),
                (dth_ref, dlg * (x2 * LN2 / (1.0 - x2)), True)]

    def ins(d):
        col = lambda blk: pl.BlockSpec((c, w), lambda n: (cidx(d, n), blk))
        tab = pl.BlockSpec((c, LANES), lambda n: (cidx(d, n), 0))
        return [col(0), col(1), col(2), tab, tab, pl.BlockSpec((None, 1, w), lambda n: (d, 0, 0)),
                pl.BlockSpec((None, RET_H, LANES), lambda n: (d, 0, 0)),
                pl.BlockSpec((None, w, LANES), lambda n: (cidx(d, n), 0, 0)), col(0)]

    def outs(d):
        row = pl.BlockSpec((c, w), lambda n: (cidx(d, n), 0))
        return [row, row, row, pl.BlockSpec((RET_H, LANES), lambda n: (0, 0))]

    res = pl.pallas_call(
        body, name="ret_bwd", grid=(n_chunks,),
        in_specs=ins(0) + ins(1), out_specs=outs(0) + outs(1),
        out_shape=([jax.ShapeDtypeStruct((s, w), F32)] * 3 + [jax.ShapeDtypeStruct((RET_H, LANES), F32)]) * 2,
        scratch_shapes=[pltpu.VMEM((w, LANES), F32)] * 2,
        compiler_params=_cp(("arbitrary",)),
    )(*[a for d in range(2) for a in (p_even, p_even, p_even, cos_r, sin_r, theta_l, theta_h, r_prev[d], do)])
    return (res[0], res[4]), (res[1], res[5]), (res[2], res[6]), jnp.stack([res[3], res[7]])


def _post_fwd(o2, gate_row, gain, *, group, n, s, ts, name):
    w = o2[0].shape[1]

    def fn(of, ob, g, gv):
        y = _gn(of + ob, gv, group, n)[0]
        return g * _sigmoid(g) * y

    return _ew(fn, [_lead(o2, 0, ts), _lead(o2, 1, ts), gate_row], [gain], [(w, BF16)], s=s, ts=ts, name=name)[0]


def _post_bwd(o2, gate_row, gain, dr_row, *, group, n, s, ts, name):
    w = o2[0].shape[1]

    def fn(of, ob, g, dr, gv):
        y, xn, rstd = _gn(of + ob, gv, group, n)
        sg = _sigmoid(g)
        dy = dr * (g * sg)
        dgate = dr * y * (sg * (1.0 + g * (1.0 - sg)))
        do, dgain = _gn_bwd(dy, xn, rstd, gv, group, n)
        return do, dgate, dgain

    return _ew(fn, [_lead(o2, 0, ts), _lead(o2, 1, ts), gate_row, dr_row], [gain],
               [(w, F32), (w, BF16)], [(1, w)], s=s, ts=ts, name=name)


def _sum2(a2, *, s, ts, name):
    w = a2[0].shape[1]
    return _ew(lambda a, b: a + b, [_lead(a2, 0, ts), _lead(a2, 1, ts)], [], [(w, BF16)], s=s, ts=ts, name=name)[0]


def _gla_common(d, q_ref, k_ref, ga_ref, wg_ref, bg_ref):
    c = GLA_C
    df = float(d)
    ii = lax.broadcasted_iota(jnp.int32, (c, c), 0).astype(F32)
    jj = lax.broadcasted_iota(jnp.int32, (c, c), 1).astype(F32)
    rel = (ii - jj) * (1.0 - 2.0 * df)
    tri = _bf(jnp.where(rel >= 0.0, 1.0, 0.0))
    mask = rel >= df
    gab = _bf(ga_ref[...])
    z = _dot(gab, wg_ref[...]) + bg_ref[...]
    la = (jnp.minimum(z, 0.0) - jnp.log1p(jnp.exp(-jnp.abs(z)))) * (1.0 / GLA_TAU)
    l1, l2, l3 = _split3(la)
    b = _dot(tri, l1) + _dot(tri, l2) + _dot(tri, l3)
    first = d == 0
    bm = b[c // 2:c // 2 + 1] if first else b[c // 2 - 1:c // 2]
    bl = b[c - 1:c] if first else b[0:1]
    q = q_ref[...] * (GLA_DK ** -0.5)
    k = k_ref[...]
    e1, e2, e3, eb = jnp.exp(b - bm), jnp.exp(bm - b), jnp.exp(bl - b), jnp.exp(b)
    return dict(tri=tri, mask=mask, gab=gab, z=z, ebl=jnp.exp(bl), e1=e1, e2=e2, e3=e3, eb=eb,
                qc=q * e1, kc=k * e2, kd=k * e3, qe=q * eb, first=first)


def _col_scale(row_vec, width):
    t = jnp.broadcast_to(row_vec, (LANES, LANES)).T
    return jnp.concatenate([t] * (width // LANES), axis=1)


def _gla_fwd(p_odd, wg2, bg2):
    s = p_odd.shape[0]
    c = GLA_C
    n_chunks = s // c
    wk, wv = GLA_H * GLA_DK, GLA_H * GLA_DV
    cidx = _chunk_index(n_chunks)

    def body(*refs):
        n = pl.program_id(0)

        @pl.when(n == 0)
        def _():
            for s_s in refs[16:18]:
                s_s[...] = jnp.zeros_like(s_s)

        stores = []
        for d in range(2):
            stores += one(d, *refs[6 * d:6 * d + 6], *refs[12 + 2 * d:14 + 2 * d], refs[16 + d])
        for ref, val in stores:
            ref[...] = val

    def one(d, q_ref, k_ref, v_ref, ga_ref, wg_ref, bg_ref, o_ref, sp_ref, s_s):
        g = _gla_common(d, q_ref, k_ref, ga_ref, wg_ref, bg_ref)
        s_all = s_s[...]
        outs, states = [], []
        for h in range(GLA_H):
            sl = slice(h * GLA_DK, (h + 1) * GLA_DK)
            vs = slice(h * GLA_DV, (h + 1) * GLA_DV)
            vh = _bf(v_ref[:, vs])
            sh = s_all[sl, :]
            a = jnp.where(g["mask"], _dot_nt(_bf(g["qc"][:, sl]), _bf(g["kc"][:, sl])), 0.0)
            outs.append(_dot(_bf(a), vh) + _dot(_bf(g["qe"][:, sl]), _bf(sh)))
            states.append(_col_scale(g["ebl"][:, sl], GLA_DV) * sh + _dot_tn(_bf(g["kd"][:, sl]), vh))
        return [(sp_ref, s_all), (o_ref, jnp.concatenate(outs, axis=1)), (s_s, jnp.concatenate(states, axis=0))]

    def ins(d):
        col = lambda width, blk: pl.BlockSpec((c, width), lambda n: (cidx(d, n), blk))
        return [col(wk, 0), col(wk, 1), col(wv, 1), col(LANES, OD_GA_BLK),
                pl.BlockSpec((None, LANES, wk), lambda n: (d, 0, 0)), pl.BlockSpec((None, 1, wk), lambda n: (d, 0, 0))]

    def outs(d):
        return [pl.BlockSpec((c, wv), lambda n: (cidx(d, n), 0)),
                pl.BlockSpec((None, wk, GLA_DV), lambda n: (cidx(d, n), 0, 0))]

    o_f, s_f, o_b, s_b = pl.pallas_call(
        body, name="gla_fwd", grid=(n_chunks,),
        in_specs=ins(0) + ins(1), out_specs=outs(0) + outs(1),
        out_shape=[jax.ShapeDtypeStruct((s, wv), F32), jax.ShapeDtypeStruct((n_chunks, wk, GLA_DV), F32)] * 2,
        scratch_shapes=[pltpu.VMEM((wk, GLA_DV), F32)] * 2,
        compiler_params=_cp(("arbitrary",)),
    )(*[p_odd, p_odd, p_odd, p_odd, wg2, bg2] * 2)
    return (o_f, o_b), (s_f, s_b)


def _gla_bwd(p_odd, wg2, bg2, s_prev, do):
    s = p_odd.shape[0]
    c = GLA_C
    n_chunks = s // c
    wk, wv = GLA_H * GLA_DK, GLA_H * GLA_DV
    fwd_idx = _chunk_index(n_chunks)

    def cidx(d, n):
        return fwd_idx(d, n_chunks - 1 - n)

    def body(*refs):
        n = pl.program_id(0)

        @pl.when(n == 0)
        def _():
            for d in range(2):
                for r in (refs[28 + d], refs[20 + 6 * d], refs[21 + 6 * d]):
                    r[...] = jnp.zeros_like(r)

        stores = []
        for d in range(2):
            stores += one(d, *refs[8 * d:8 * d + 8], *refs[16 + 6 * d:22 + 6 * d], refs[28 + d])
        for ref, val, accumulate in stores:
            if accumulate:
                ref[...] += val
            else:
                ref[...] = val

    def one(d, q_ref, k_ref, v_ref, ga_ref, wg_ref, bg_ref, sp_ref, do_ref,
            dq_ref, dk_ref, dv_ref, dga_ref, dwg_ref, dbg_ref, ds_s):
        g = _gla_common(d, q_ref, k_ref, ga_ref, wg_ref, bg_ref)
        mask = g["mask"]
        ones8 = jnp.ones((8, GLA_DV), BF16)
        sp_all, ds_all = sp_ref[...], ds_s[...]
        dbs, dbms, dbls = [], [], []
        dqs, dks, dvs, dss = [], [], [], []
        for h in range(GLA_H):
            sl = slice(h * GLA_DK, (h + 1) * GLA_DK)
            vs = slice(h * GLA_DV, (h + 1) * GLA_DV)
            qc, kc, kd, qe = g["qc"][:, sl], g["kc"][:, sl], g["kd"][:, sl], g["qe"][:, sl]
            qcb, kcb, kdb, qeb = _bf(qc), _bf(kc), _bf(kd), _bf(qe)
            vh = _bf(v_ref[:, vs])
            doh = _bf(do_ref[:, vs])
            sp = sp_all[sl, :]
            dsn = ds_all[sl, :]
            dsb = _bf(dsn)
            a = _bf(jnp.where(mask, _dot_nt(qcb, kcb), 0.0))
            da = _bf(jnp.where(mask, _dot_nt(doh, vh), 0.0))
            dvs.append(_dot_tn(a, doh) + _dot(kdb, dsb))
            dqc = _dot(da, kcb)
            dkc = _dot_tn(da, qcb)
            dqe = _dot_nt(doh, _bf(sp))
            dkd = _dot_nt(vh, dsb)
            dss.append(_dot_tn(qeb, doh) + _col_scale(g["ebl"][:, sl], GLA_DV) * dsn)
            dqs.append((dqc * g["e1"][:, sl] + dqe * g["eb"][:, sl]) * (GLA_DK ** -0.5))
            dks.append(dkc * g["e2"][:, sl] + dkd * g["e3"][:, sl])
            t1, t2, t3, t4 = dqc * qc, dkc * kc, dqe * qe, dkd * kd
            dbs.append(t1 - t2 + t3 - t4)
            dbms.append(_rowsum(t2 - t1))
            m1, m2, _ = _split3(dsn * sp)
            rs = (_dot_nt(ones8, m1) + _dot_nt(ones8, m2))[0:1]
            dbls.append(_rowsum(t4) + g["ebl"][:, sl] * rs)
        db = jnp.concatenate(dbs, axis=1)
        dbm = jnp.concatenate(dbms, axis=1)
        dbl = jnp.concatenate(dbls, axis=1)
        row = lax.broadcasted_iota(jnp.int32, (c, wk), 0)
        mid = jnp.where(g["first"], c // 2, c // 2 - 1)
        last = jnp.where(g["first"], c - 1, 0)
        db = db + jnp.where(row == mid, dbm, 0.0) + jnp.where(row == last, dbl, 0.0)
        d1, d2, d3 = _split3(db)
        tri = g["tri"]
        dla = _dot_tn(tri, d1) + _dot_tn(tri, d2) + _dot_tn(tri, d3)
        dz = dla * (1.0 / GLA_TAU) * (1.0 - _sigmoid(g["z"]))
        dzb = _bf(dz)
        return [(dq_ref, jnp.concatenate(dqs, axis=1), False), (dk_ref, jnp.concatenate(dks, axis=1), False),
                (dv_ref, jnp.concatenate(dvs, axis=1), False), (ds_s, jnp.concatenate(dss, axis=0), False),
                (dga_ref, _dot_nt(dzb, wg_ref[...]), False), (dwg_ref, _dot_tn(g["gab"], dzb), True),
                (dbg_ref, _rowsum(dz), True)]

    def ins(d):
        col = lambda width, blk: pl.BlockSpec((c, width), lambda n: (cidx(d, n), blk))
        return [col(wk, 0), col(wk, 1), col(wv, 1), col(LANES, OD_GA_BLK),
                pl.BlockSpec((None, LANES, wk), lambda n: (d, 0, 0)), pl.BlockSpec((None, 1, wk), lambda n: (d, 0, 0)),
                pl.BlockSpec((None, wk, GLA_DV), lambda n: (cidx(d, n), 0, 0)), col(wv, 0)]

    def outs(d):
        row = lambda width: pl.BlockSpec((c, width), lambda n: (cidx(d, n), 0))
        return [row(wk), row(wk), row(wv), row(LANES),
                pl.BlockSpec((LANES, wk), lambda n: (0, 0)), pl.BlockSpec((1, wk), lambda n: (0, 0))]

    shapes = [jax.ShapeDtypeStruct((s, wk), F32), jax.ShapeDtypeStruct((s, wk), F32), jax.ShapeDtypeStruct((s, wv), F32),
              jax.ShapeDtypeStruct((s, LANES), F32), jax.ShapeDtypeStruct((LANES, wk), F32),
              jax.ShapeDtypeStruct((1, wk), F32)]
    res = pl.pallas_call(
        body, name="gla_bwd", grid=(n_chunks,),
        in_specs=ins(0) + ins(1), out_specs=outs(0) + outs(1), out_shape=shapes * 2,
        scratch_shapes=[pltpu.VMEM((wk, GLA_DV), F32)] * 2,
        compiler_params=_cp(("arbitrary",)),
    )(*[a for d in range(2) for a in (p_odd, p_odd, p_odd, p_odd, wg2, bg2, s_prev[d], do)])
    pair = lambda k: (res[k], res[6 + k])
    return pair(0), pair(1), pair(2), pair(3), jnp.stack(pair(4)), jnp.stack(pair(5))


HALO = 8


def _halo_specs(width_blk, col0, ts, s):
    r = ts // HALO
    last = s // HALO - 1
    cur = pl.BlockSpec((ts, width_blk), lambda j, i: (i, col0 + j))
    prev = pl.BlockSpec((HALO, width_blk), lambda j, i: (jnp.maximum(i * r - 1, 0), col0 + j))
    nxt = pl.BlockSpec((HALO, width_blk), lambda j, i: (jnp.minimum((i + 1) * r, last), col0 + j))
    return [prev, cur, nxt]


def _with_halo(prev_ref, cur_ref, next_ref, i, n_i):
    p = jnp.where(i == 0, 0.0, prev_ref[...])
    q = jnp.where(i == n_i - 1, 0.0, next_ref[...])
    return jnp.concatenate([p, cur_ref[...], q], axis=0)


def _shift_down(x):
    return pltpu.roll(x, 1, 0)


def _shift_up(x):
    return pltpu.roll(x, x.shape[0] - 1, 0)


def _ffn_act(up, conv_w, conv_b, *, ts):
    s = up.shape[0]
    tc = _tile(D_FF, 1408)
    nj = D_FF // tc
    n_i = s // ts

    def body(gp, gc, gn, val_ref, w_ref, b_ref, a_ref):
        i = pl.program_id(1)
        g = _with_halo(gp, gc, gn, i, n_i)
        w = w_ref[...]
        conv = w[0:1] * _shift_down(g) + w[1:2] * g + w[2:3] * _shift_up(g) + b_ref[...]
        conv = conv[HALO:HALO + ts]
        a_ref[...] = (conv * _sigmoid(conv) * val_ref[...]).astype(a_ref.dtype)

    return pl.pallas_call(
        body, name="ffn_act", grid=(nj, n_i),
        in_specs=_halo_specs(tc, 0, ts, s) + [pl.BlockSpec((ts, tc), lambda j, i: (i, nj + j)),
                                              pl.BlockSpec((3, tc), lambda j, i: (0, j)),
                                              pl.BlockSpec((1, tc), lambda j, i: (0, j))],
        out_specs=pl.BlockSpec((ts, tc), lambda j, i: (i, j)),
        out_shape=jax.ShapeDtypeStruct((s, D_FF), BF16),
        compiler_params=_cp(("parallel", "arbitrary")),
    )(up, up, up, up, conv_w, conv_b)


def _ffn_act_bwd(up, da, conv_w, conv_b, *, ts):
    s = up.shape[0]
    tc = _tile(D_FF, 1408)
    nj = D_FF // tc
    n_i = s // ts

    def body(gp, gc, gn, vp, vc, vn, dp, dc, dn, w_ref, b_ref, dg_ref, dval_ref, dw_ref, db_ref):
        i = pl.program_id(1)
        g = _with_halo(gp, gc, gn, i, n_i)
        v = _with_halo(vp, vc, vn, i, n_i)
        dav = _with_halo(dp, dc, dn, i, n_i)
        w = w_ref[...]
        gm, gpl = _shift_down(g), _shift_up(g)
        conv = w[0:1] * gm + w[1:2] * g + w[2:3] * gpl + b_ref[...]
        sg = _sigmoid(conv)
        dgc = dav * v * (sg * (1.0 + conv * (1.0 - sg)))
        dgate = w[0:1] * _shift_up(dgc) + w[1:2] * dgc + w[2:3] * _shift_down(dgc)
        ctr = slice(HALO, HALO + ts)
        dg_ref[...] = dgate[ctr].astype(dg_ref.dtype)
        dval_ref[...] = (dav[ctr] * (conv * sg)[ctr]).astype(dval_ref.dtype)
        dgc_c = dgc[ctr]
        dw = jnp.concatenate([_rowsum(dgc_c * gm[ctr]), _rowsum(dgc_c * g[ctr]), _rowsum(dgc_c * gpl[ctr])], axis=0)
        dbv = _rowsum(dgc_c)

        @pl.when(i == 0)
        def _():
            dw_ref[...] = dw
            db_ref[...] = dbv

        @pl.when(i > 0)
        def _():
            dw_ref[...] += dw
            db_ref[...] += dbv

    tile = pl.BlockSpec((ts, tc), lambda j, i: (i, j))
    return pl.pallas_call(
        body, name="ffn_act_bwd", grid=(nj, n_i),
        in_specs=(_halo_specs(tc, 0, ts, s) + _halo_specs(tc, nj, ts, s) + _halo_specs(tc, 0, ts, s)
                  + [pl.BlockSpec((3, tc), lambda j, i: (0, j)), pl.BlockSpec((1, tc), lambda j, i: (0, j))]),
        out_specs=[tile, tile, pl.BlockSpec((3, tc), lambda j, i: (0, j)), pl.BlockSpec((1, tc), lambda j, i: (0, j))],
        out_shape=[jax.ShapeDtypeStruct((s, D_FF), BF16), jax.ShapeDtypeStruct((s, D_FF), BF16),
                   jax.ShapeDtypeStruct((3, D_FF), F32), jax.ShapeDtypeStruct((1, D_FF), F32)],
        compiler_params=_cp(("parallel", "arbitrary")),
    )(up, up, up, up, up, up, da, da, da, conv_w, conv_b)


def _loss_head(y, target, *, s, ts):
    def fn(yv, tv):
        err = yv - tv
        return err * (1.0 / D_MODEL), _rowsum(err * err)

    return _ew(fn, [_cols(y, D_MODEL, 0, ts), _cols(target, D_MODEL, 0, ts)], [], [(D_MODEL, F32)],
               [(1, D_MODEL)], s=s, ts=ts, name="loss_head")


def _rows_tile(r, width):
    ts = r
    while ts * width * 4 > (1 << 20) and ts % 16 == 0:
        ts //= 2
    return ts


def _adamw(w, g, m, v, *, ts, name):
    r, width = w.shape
    assert r % ts == 0

    def fn(wv, gv, mv, vv):
        mn = ADAM_B1 * mv + (1.0 - ADAM_B1) * gv
        vn = ADAM_B2 * vv + (1.0 - ADAM_B2) * (gv * gv)
        m_hat = mn / (1.0 - ADAM_B1 ** ADAM_STEP)
        v_hat = vn / (1.0 - ADAM_B2 ** ADAM_STEP)
        delta = -ADAM_LR * (m_hat / (jnp.sqrt(v_hat) + ADAM_EPS) + ADAM_WD * wv)
        return delta, mn, vn

    rows = [_cols(a, width, 0, ts) for a in (w, g, m, v)]
    return _ew(fn, rows, [], [(width, F32)] * 3, s=r, ts=ts, name=name)


def _pad_heads(w, heads, real):
    lead = w.shape[:-1]
    w = w.reshape(lead + (heads, real))
    w = jnp.pad(w, [(0, 0)] * len(lead) + [(0, 0), (0, LANES - real)])
    return w.reshape(lead + (heads * LANES,))


def _pad_head_rows(w, heads, real):
    return _pad_heads(w.T, heads, real).T


def _pack_even(p):
    w_in = p["w_in"]
    z = lambda n: jnp.zeros((D_MODEL, n), w_in.dtype)
    o = 0
    parts = {}
    for nm, n in (("cq", MLA_QR), ("ckv", MLA_KVR), ("kr", MLA_ROPE), ("rq", 512), ("rk", 512), ("rv", 512), ("rg", 512)):
        parts[nm] = w_in[:, o:o + n]
        o += n
    w_in_p = jnp.concatenate(
        [_pad_heads(parts[k], RET_H, RET_DK) for k in ("rq", "rk", "rv", "rg")]
        + [parts["cq"], z(EV_CQ - MLA_QR), parts["ckv"], z(MLA_NOPE), parts["kr"], z(LANES - MLA_QK), z(LANES)], axis=1)
    w_uq = jnp.pad(_pad_heads(p["w_uq"], MLA_H, MLA_QK), ((0, EV_CQ - MLA_QR), (0, 0)))
    ukv = p["w_ukv"].reshape(MLA_KVR, MLA_H, MLA_NOPE + MLA_V)
    w_ukv = jnp.concatenate([_pad_heads(ukv[..., :MLA_NOPE].reshape(MLA_KVR, -1), MLA_H, MLA_NOPE),
                             _pad_heads(ukv[..., MLA_NOPE:].reshape(MLA_KVR, -1), MLA_H, MLA_V)], axis=1)
    w_out = jnp.concatenate([_pad_head_rows(p["w_out"][:MLA_H * MLA_V], MLA_H, MLA_V),
                             _pad_head_rows(p["w_out"][MLA_H * MLA_V:], RET_H, RET_DV)], axis=0)
    return dict(
        w_in=w_in_p, w_uq=w_uq, w_ukv=w_ukv, w_out=w_out,
        mix_g=p["mix_norm"][None, :],
        q_norm=jnp.pad(p["q_norm"], (0, EV_CQ - MLA_QR))[None, :],
        kv_norm=p["kv_norm"][None, :],
        qhn=jnp.pad(p["q_head_norm"], (0, LANES - MLA_QK))[None, :],
        khn=jnp.pad(p["k_head_norm"], (0, LANES - MLA_QK))[None, :],
        ret_gain=_pad_heads(p["ret_out_norm"].reshape(-1), RET_H, RET_DV)[None, :],
    )


def _pack_odd(p):
    w_in = p["w_in"]
    ga = w_in[:, 3072:]
    w_in_p = jnp.concatenate([w_in[:, :3072], ga, jnp.zeros((D_MODEL, LANES - 2 * GLA_R), w_in.dtype)], axis=1)
    wk = GLA_H * GLA_DK
    zf = jnp.zeros((LANES - GLA_R, wk), p["w_gate_fwd"].dtype)
    zb0 = jnp.zeros((GLA_R, wk), p["w_gate_fwd"].dtype)
    zb1 = jnp.zeros((LANES - 2 * GLA_R, wk), p["w_gate_fwd"].dtype)
    wg2 = jnp.stack([jnp.concatenate([p["w_gate_fwd"], zf], axis=0),
                     jnp.concatenate([zb0, p["w_gate_bwd"], zb1], axis=0)])
    bg2 = jnp.stack([p["b_gate_fwd"][None, :], p["b_gate_bwd"][None, :]])
    return dict(w_in=w_in_p, wg2=wg2, bg2=bg2, w_out=p["w_out"], mix_g=p["mix_norm"][None, :],
                gla_gain=p["gla_out_norm"].reshape(1, -1))


_MATRICES = ("w_in", "w_uq", "w_ukv", "w_out", "wg2")


def _packed(pack_fn, p):
    packed = pack_fn(p)
    packed = {k: (_bf(v) if k in _MATRICES else v.astype(F32)) for k, v in packed.items()}
    shapes = {k: jax.ShapeDtypeStruct(v.shape, F32) for k, v in p.items()}
    unpack = jax.linear_transpose(pack_fn, shapes)
    return packed, lambda g: unpack(g)[0]


def _ffn_fwd(x, w, *, s, ts):
    h = _rmsnorm(_cols(x, D_MODEL, 0, ts), w["norm_g"], n=D_MODEL, s=s, ts=ts, name="ffn_norm")
    up = _mm(h, w["w_up4"], b_layer=w["layer"], name="ffn_up")
    a = _ffn_act(up, w["conv_w"], w["conv_b"], ts=ts)
    y = _mm(a, w["w_down"], res=x, name="ffn_down")
    return y, dict(x=x, h=h, up=up, a=a)


def _ffn_bwd(dy, w, sv, *, s, ts):
    da = _mm(dy, w["w_down"], tb=True, name="ffn_down_dx")
    g_down = _mm(sv["a"], dy, ta=True, name="ffn_down_dw")
    dgate, dval, g_cw, g_cb = _ffn_act_bwd(sv["up"], da, w["conv_w"], w["conv_b"], ts=ts)
    dup = jnp.concatenate([dgate, dval], axis=1)
    dh = _mm(dup, w["w_up4"], tb=True, b_layer=w["layer"], name="ffn_up_dx")
    g_up = _mm(sv["h"], dup, ta=True, out_chips=True, name="ffn_up_dw")
    dx, g_norm = _rmsnorm_bwd(_cols(sv["x"], D_MODEL, 0, ts), w["norm_g"], dh, dy, n=D_MODEL, s=s, ts=ts,
                              name="ffn_norm_bwd")
    return dx, dict(w_up=g_up, w_down=g_down, conv_w=g_cw, conv_b=g_cb, norm_g=g_norm)


def _even_fwd(x, w, tabs, *, s, ts, side=()):
    cos_m, sin_m, cos_r, sin_r = tabs
    h = _rmsnorm(_cols(x, D_MODEL, 0, ts), w["mix_g"], n=D_MODEL, s=s, ts=ts, name="mix_norm")
    p = _mm(h, w["w_in"], name="even_in")
    cqn = _rmsnorm(_cols(p, EV_CQ, EV_RET // EV_CQ, ts), w["q_norm"], n=MLA_QR, s=s, ts=ts, name="mla_q_norm")
    ckvn = _rmsnorm(_cols(p, MLA_KVR, (EV_RET + EV_CQ) // MLA_KVR, ts), w["kv_norm"], n=MLA_KVR, s=s, ts=ts,
                    name="mla_kv_norm")
    q_pre = _mm(cqn, w["w_uq"], name="mla_uq")
    kv_pre = _mm(ckvn, w["w_ukv"], name="mla_ukv")
    q, k, v = _mla_prep(q_pre, kv_pre, p, cos_m, sin_m, w["qhn"], w["khn"], s=s, ts=ts)
    o, lse, gathered = _flash_fwd(q, k, v, tq=min(s, FLASH_FWD_ROWS), tk=min(s, FLASH_KEYS), side=side)
    o2, r_prev = _ret_fwd(p, cos_r, sin_r, w["theta_l"])
    r = _post_fwd(o2, _cols(p, RET_H * LANES, 3, ts), w["ret_gain"], group=LANES, n=RET_DV, s=s, ts=ts,
                  name="ret_post")
    ar = jnp.concatenate([o, r], axis=1)
    y = _mm(ar, w["w_out"], res=x, name="even_out")
    return y, dict(x=x, h=h, p=p, cqn=cqn, ckvn=ckvn, q_pre=q_pre, kv_pre=kv_pre, q=q, k=k, v=v, o=o, lse=lse,
                   o2=o2, r_prev=r_prev, ar=ar), gathered


def _even_bwd(dy, w, sv, tabs, *, s, ts):
    cos_m, sin_m, cos_r, sin_r = tabs
    p = sv["p"]
    wh = MLA_H * LANES
    dar = _mm(dy, w["w_out"], tb=True, name="even_out_dx")
    g_out = _mm(sv["ar"], dy, ta=True, name="even_out_dw")
    do_attn = _attn_bwd_prep(dar, sv["o"], s=s, ts=ts)
    dq, dk, dv = _flash_bwd(sv["q"], sv["k"], sv["v"], do_attn, sv["lse"], tq=min(s, FLASH_BWD_ROWS),
                            tk=min(s, FLASH_KEYS))
    dq_pre, dk_pre, dkr, g_qhn, g_khn = _mla_prep_bwd(sv["q_pre"], sv["kv_pre"], p, cos_m, sin_m, w["qhn"], w["khn"],
                                                      dq, dk, s=s, ts=ts)
    dkv_pre = jnp.concatenate([dk_pre, dv], axis=1)
    dckvn = _mm(dkv_pre, w["w_ukv"], tb=True, name="mla_ukv_dx")
    g_ukv = _mm(sv["ckvn"], dkv_pre, ta=True, name="mla_ukv_dw")
    dcqn = _mm(dq_pre, w["w_uq"], tb=True, name="mla_uq_dx")
    g_uq = _mm(sv["cqn"], dq_pre, ta=True, name="mla_uq_dw")
    dckv, g_kvn = _rmsnorm_bwd(_cols(p, MLA_KVR, (EV_RET + EV_CQ) // MLA_KVR, ts), w["kv_norm"], dckvn, None,
                               n=MLA_KVR, s=s, ts=ts, name="mla_kv_norm_bwd")
    dcq, g_qn = _rmsnorm_bwd(_cols(p, EV_CQ, EV_RET // EV_CQ, ts), w["q_norm"], dcqn, None, n=MLA_QR, s=s, ts=ts,
                             name="mla_q_norm_bwd")
    do, drg, g_gain = _post_bwd(sv["o2"], _cols(p, wh, 3, ts), w["ret_gain"], _cols(dar, wh, 1, ts),
                                group=LANES, n=RET_DV, s=s, ts=ts, name="ret_post_bwd")
    dq2, dk2, dv2, dth = _ret_bwd(p, cos_r, sin_r, w["theta_l"], w["theta_h"], sv["r_prev"], do)
    drq, drk, drv = (_sum2(a, s=s, ts=ts, name="sum_dirs_1024") for a in (dq2, dk2, dv2))
    dp = jnp.concatenate([drq, drk, drv, drg, _bf(dcq), _bf(dckv), dkr, jnp.zeros((s, LANES), BF16)], axis=1)
    dh = _mm(dp, w["w_in"], tb=True, name="even_in_dx")
    g_in = _mm(sv["h"], dp, ta=True, name="even_in_dw")
    dx, g_mix = _rmsnorm_bwd(_cols(sv["x"], D_MODEL, 0, ts), w["mix_g"], dh, dy, n=D_MODEL, s=s, ts=ts,
                             name="mix_norm_bwd")
    grads = dict(w_in=g_in, w_uq=g_uq, w_ukv=g_ukv, w_out=g_out, mix_g=g_mix, q_norm=g_qn, kv_norm=g_kvn,
                 qhn=g_qhn, khn=g_khn, ret_gain=g_gain)
    return dx, grads, dth[:, :, 0]


def _odd_fwd(x, w, *, s, ts):
    h = _rmsnorm(_cols(x, D_MODEL, 0, ts), w["mix_g"], n=D_MODEL, s=s, ts=ts, name="mix_norm")
    p = _mm(h, w["w_in"], name="odd_in")
    o2, s_prev = _gla_fwd(p, w["wg2"], w["bg2"])
    g = _post_fwd(o2, _cols(p, GLA_H * GLA_DV, 2, ts), w["gla_gain"], group=GLA_DV, n=GLA_DV, s=s, ts=ts,
                  name="gla_post")
    y = _mm(g, w["w_out"], res=x, name="odd_out")
    return y, dict(x=x, h=h, p=p, o2=o2, s_prev=s_prev, g=g)


def _odd_bwd(dy, w, sv, *, s, ts):
    p = sv["p"]
    wv = GLA_H * GLA_DV
    dg = _mm(dy, w["w_out"], tb=True, name="odd_out_dx")
    g_out = _mm(sv["g"], dy, ta=True, name="odd_out_dw")
    do, dgr, g_gain = _post_bwd(sv["o2"], _cols(p, wv, 2, ts), w["gla_gain"], _cols(dg, wv, 0, ts),
                                group=GLA_DV, n=GLA_DV, s=s, ts=ts, name="gla_post_bwd")
    dq2, dk2, dv2, dga2, g_wg, g_bg = _gla_bwd(p, w["wg2"], w["bg2"], sv["s_prev"], do)
    dq = _sum2(dq2, s=s, ts=ts, name="sum_dirs_512")
    dk = _sum2(dk2, s=s, ts=ts, name="sum_dirs_512")
    dv = _sum2(dv2, s=s, ts=ts, name="sum_dirs_1024")
    dga = _sum2(dga2, s=s, ts=ts, name="sum_dirs_128")
    dp = jnp.concatenate([dq, dk, dv, dgr, dga], axis=1)
    dh = _mm(dp, w["w_in"], tb=True, name="odd_in_dx")
    g_in = _mm(sv["h"], dp, ta=True, name="odd_in_dw")
    dx, g_mix = _rmsnorm_bwd(_cols(sv["x"], D_MODEL, 0, ts), w["mix_g"], dh, dy, n=D_MODEL, s=s, ts=ts,
                             name="mix_norm_bwd")
    return dx, dict(w_in=g_in, wg2=g_wg, bg2=g_bg, w_out=g_out, mix_g=g_mix, gla_gain=g_gain)


_EVEN_NAMES = dict(mix_norm="mix_norm_even", w_in="w_in_even", q_norm="mla_q_norm", kv_norm="mla_kv_norm",
                   w_uq="mla_w_uq", w_ukv="mla_w_ukv", q_head_norm="mla_q_head_norm", k_head_norm="mla_k_head_norm",
                   ret_out_norm="ret_out_norm", w_out="w_out_even")
_ODD_NAMES = dict(mix_norm="mix_norm_odd", w_in="w_in_odd", w_gate_fwd="gla_w_gate_fwd", b_gate_fwd="gla_b_gate_fwd",
                  w_gate_bwd="gla_w_gate_bwd", b_gate_bwd="gla_b_gate_bwd", gla_out_norm="gla_out_norm",
                  w_out="w_out_odd")

def _local_step(x, pos, target, full, side=(), finish=None):
    s = x.shape[0]
    ts = min(s, 256)
    tabs = _rope_tables(pos, MLA_ROPE, MLA_NOPE) + _rope_tables(pos, RET_DK, 0)

    def layer_weights(layer):
        i = layer // 2
        names = _EVEN_NAMES if layer % 2 == 0 else _ODD_NAMES
        wm, unpack_m = _packed(_pack_even if layer % 2 == 0 else _pack_odd, {k: full[n][i] for k, n in names.items()})
        if layer % 2 == 0:
            th = jnp.stack([full["ret_theta_fwd"][i], full["ret_theta_bwd"][i]]).astype(F32)
            wm["theta_h"] = jnp.broadcast_to(th[:, :, None], (2, RET_H, LANES))
            wm["theta_l"] = wm["theta_h"].reshape(2, 1, RET_H * LANES)
        w_up4, index = full["ffn_w_up"][layer]
        wf = dict(layer=index, w_up4=w_up4, w_down=_bf(full["ffn_w_down"][layer]),
                  conv_w=full["ffn_conv_w"][layer].astype(F32), conv_b=full["ffn_conv_b"][layer][None, :].astype(F32),
                  norm_g=full["ffn_norm"][layer][None, :].astype(F32))
        return wm, unpack_m, wf

    layers, saved = [], []
    for layer in range(DEPTH):
        layers.append(layer_weights(layer))
        wm, _, wf = layers[-1]
        if layer % 2 == 0:
            x, sv_m, gathered = _even_fwd(x, wm, tabs, s=s, ts=ts, side=side if layer == 0 else ())
            if layer == 0 and finish is not None:
                full = finish(gathered)
        else:
            x, sv_m = _odd_fwd(x, wm, s=s, ts=ts)
        x, sv_f = _ffn_fwd(x, wf, s=s, ts=ts)
        saved.append((sv_m, sv_f))

    dy, sq = _loss_head(x, target, s=s, ts=ts)
    loss = 0.5 / D_MODEL * jnp.sum(sq)

    grads = {}

    def put(name, idx, g):
        grads.setdefault(name, {})[idx] = g

    for layer in reversed(range(DEPTH)):
        wm, unpack_m, wf = layers[layer]
        sv_m, sv_f = saved[layer]
        i = layer // 2
        dy, gf = _ffn_bwd(dy, wf, sv_f, s=s, ts=ts)
        put("ffn_w_up", layer, gf["w_up"])
        put("ffn_w_down", layer, gf["w_down"])
        put("ffn_conv_w", layer, gf["conv_w"])
        put("ffn_conv_b", layer, gf["conv_b"][0])
        put("ffn_norm", layer, gf["norm_g"][0])
        if layer % 2 == 0:
            dy, gm, dth = _even_bwd(dy, wm, sv_m, tabs, s=s, ts=ts)
            put("ret_theta_fwd", i, dth[0])
            put("ret_theta_bwd", i, dth[1])
            names = _EVEN_NAMES
        else:
            dy, gm = _odd_bwd(dy, wm, sv_m, s=s, ts=ts)
            names = _ODD_NAMES
        for k, g in unpack_m(gm).items():
            put(names[k], i, g)
    return loss, dy, {n: [g[j] for j in range(len(g))] for n, g in grads.items()}


HBM_SPEC = pl.BlockSpec(memory_space=pltpu.HBM)
VMEM_SPEC = pl.BlockSpec(memory_space=pltpu.VMEM)
CHIPS = 4
CORES = 2
ROW = 8 * LANES


def _xyc():
    return lax.axis_index("x"), lax.axis_index("y"), lax.axis_index("c")


def _other_chips(x, y):
    return [(1 - x, y), (x, 1 - y), (1 - x, 1 - y)]


def _remote(src, dst, send, recv, dev):
    return pltpu.make_async_remote_copy(src_ref=src, dst_ref=dst, send_sem=send, recv_sem=recv,
                                        device_id=dev, device_id_type=MESH)


def _sems(n):
    return pltpu.SemaphoreType.DMA((n,))


def _gather_copies(side, srcs, lands, send, recv, loc):
    n = len(side)
    x, y, c = _xyc()
    me = 2 * x + y
    local, sends, arrivals = [], [], []
    for t, (_, first, count) in enumerate(side):
        src = srcs[t].at[pl.ds(first, count)]
        local.append(pltpu.make_async_copy(src, lands[t].at[me], loc.at[t]))
        for j, (px, py) in enumerate(_other_chips(x, y)):
            k = n * j + t
            sends.append(_remote(src, lands[t].at[me], send.at[k], recv.at[k], (px, py, c)))
            arrivals.append(_remote(src, lands[t].at[2 * px + py], send.at[k], recv.at[k], (px, py, c)))
    return local, sends, arrivals


def _gather_shapes(side):
    return [jax.ShapeDtypeStruct((CHIPS, count) + a.shape[1:], a.dtype) for a, _, count in side]


def _gather_chips(side):
    n = len(side)

    def body(*refs):
        local, sends, arrivals = _gather_copies(side, refs[:n], refs[n:2 * n], *refs[2 * n:])
        for cp in local + sends:
            cp.start()
        for cp in arrivals:
            cp.wait_recv()
        for cp in sends:
            cp.wait_send()
        for cp in local:
            cp.wait()

    return pl.pallas_call(
        body, name="gather_chips", in_specs=[HBM_SPEC] * n, out_specs=[HBM_SPEC] * n,
        out_shape=_gather_shapes(side),
        scratch_shapes=[_sems(3 * n), _sems(3 * n), _sems(n)],
    )(*[a for a, _, _ in side])


def _half_rows(ref, axis, half, which):
    idx = (slice(None),) * axis + (pl.ds(pl.multiple_of(which * half, 8), half),)
    return ref.at[idx]


def _swap_halves(arrs):
    n = len(arrs)

    def body(*refs):
        ins, outs = refs[:n], refs[n:2 * n]
        send, recv = refs[2 * n:]
        x, y, c = _xyc()
        copies = []
        for t in range(n):
            half = arrs[t].shape[2] // CORES
            cp = _remote(_half_rows(ins[t], 2, half, 1 - c), outs[t], send.at[t], recv.at[t], (x, y, 1 - c))
            cp.start()
            copies.append(cp)
        for cp in copies:
            cp.wait()

    return pl.pallas_call(
        body, name="swap_halves", in_specs=[HBM_SPEC] * n, out_specs=[HBM_SPEC] * n,
        out_shape=[jax.ShapeDtypeStruct(a.shape[:2] + (a.shape[2] // CORES, a.shape[3]), a.dtype) for a in arrs],
        scratch_shapes=[_sems(n), _sems(n)],
    )(*arrs)


def _add_core_halves(a, got, core, *, ts, name):
    ch, nl, r, cols = a.shape
    half = r // CORES
    nb = half // ts

    def body(core_ref, a_ref, g_ref, o_ref):
        o_ref[...] = (a_ref[...] + g_ref[...]).astype(o_ref.dtype)

    rows = pl.BlockSpec((ts, cols), lambda g, i, cr: (g * nb + i, 0))
    return pl.pallas_call(
        body, name=name, out_shape=jax.ShapeDtypeStruct((ch * nl * half, cols), BF16),
        grid_spec=pltpu.PrefetchScalarGridSpec(
            num_scalar_prefetch=1, grid=(ch * nl, nb),
            in_specs=[pl.BlockSpec((ts, cols), lambda g, i, cr: (g * (r // ts) + cr[0] * nb + i, 0)), rows],
            out_specs=rows),
        compiler_params=_cp(("arbitrary", "arbitrary")),
    )(core, a.reshape(-1, cols), got.reshape(-1, cols)).reshape(got.shape)


def _add_chip_parts(parts, core, *, ts, name):
    ch, nl, half, cols = parts.shape
    nb = half // ts
    r = half * CORES

    def body(core_ref, *refs):
        acc = refs[0][...].astype(F32)
        for p in refs[1:ch]:
            acc = acc + p[...].astype(F32)
        refs[ch][...] = acc

    return pl.pallas_call(
        body, name=name, out_shape=jax.ShapeDtypeStruct((nl * r, cols), F32),
        grid_spec=pltpu.PrefetchScalarGridSpec(
            num_scalar_prefetch=1, grid=(nl, nb),
            in_specs=[pl.BlockSpec((ts, cols), lambda l, i, cr, j=j: ((j * nl + l) * nb + i, 0)) for j in range(ch)],
            out_specs=pl.BlockSpec((ts, cols), lambda l, i, cr: (l * (r // ts) + cr[0] * nb + i, 0))),
        compiler_params=_cp(("arbitrary", "arbitrary")),
    )(core, *[parts.reshape(-1, cols)] * ch).reshape(nl, r, cols)


def _scatter_chips(arrs):
    n = len(arrs)

    def body(*refs):
        ins, outs = refs[:n], refs[n:2 * n]
        send, recv, loc = refs[2 * n:]
        x, y, c = _xyc()
        me = 2 * x + y
        copies = []
        for t in range(n):
            cp = pltpu.make_async_copy(ins[t].at[me], outs[t].at[me], loc.at[t])
            cp.start()
            copies.append(cp)
        sends = []
        for j, (px, py) in enumerate(_other_chips(x, y)):
            for t in range(n):
                cp = _remote(ins[t].at[2 * px + py], outs[t].at[me], send.at[n * j + t], recv.at[n * j + t], (px, py, c))
                cp.start()
                sends.append(cp)
        for j, (px, py) in enumerate(_other_chips(x, y)):
            for t in range(n):
                _remote(ins[t].at[me], outs[t].at[2 * px + py], send.at[n * j + t], recv.at[n * j + t],
                        (px, py, c)).wait_recv()
        for cp in sends:
            cp.wait_send()
        for cp in copies:
            cp.wait()

    return pl.pallas_call(
        body, name="scatter_chips", in_specs=[HBM_SPEC] * n, out_specs=[HBM_SPEC] * n,
        out_shape=[jax.ShapeDtypeStruct(a.shape, a.dtype) for a in arrs],
        scratch_shapes=[_sems(3 * n), _sems(3 * n), _sems(n)],
    )(*arrs)


def _gather_cores(arrs):
    n = len(arrs)

    def body(*refs):
        ins, outs = refs[:n], refs[n:2 * n]
        send, recv = refs[2 * n:]
        x, y, c = _xyc()
        sends = []
        for t in range(n):
            half = arrs[t].shape[1] // CORES
            cp = _remote(_half_rows(ins[t], 1, half, c), _half_rows(outs[t], 1, half, c), send.at[t], recv.at[t],
                         (x, y, 1 - c))
            cp.start()
            sends.append(cp)
        for t in range(n):
            half = arrs[t].shape[1] // CORES
            _remote(_half_rows(ins[t], 1, half, 1 - c), _half_rows(outs[t], 1, half, 1 - c), send.at[t], recv.at[t],
                    (x, y, 1 - c)).wait_recv()
        for cp in sends:
            cp.wait_send()

    return pl.pallas_call(
        body, name="gather_cores", in_specs=[HBM_SPEC] * n, out_specs=[HBM_SPEC] * n,
        out_shape=[jax.ShapeDtypeStruct(a.shape, a.dtype) for a in arrs],
        input_output_aliases={t: t for t in range(n)},
        scratch_shapes=[_sems(n), _sems(n)],
    )(*arrs)


def _all_reduce_devices(v):
    n_dev = CHIPS * CORES

    def body(v_ref, o_ref, buf, send, recv):
        x, y, c = _xyc()
        me = 4 * x + 2 * y + c
        buf[pl.ds(me, 1)] = v_ref[...][None]
        sends = []
        for m in range(1, n_dev):
            px = 1 - x if m & 4 else x
            py = 1 - y if m & 2 else y
            pc = 1 - c if m & 1 else c
            cp = _remote(v_ref, buf.at[me], send.at[m - 1], recv.at[m - 1], (px, py, pc))
            cp.start()
            sends.append((cp, 4 * px + 2 * py + pc))
        for m, (cp, peer) in enumerate(sends):
            _remote(v_ref, buf.at[peer], send.at[m], recv.at[m], (x, y, c)).wait_recv()
        for cp, _ in sends:
            cp.wait_send()
        acc = buf[0]
        for k in range(1, n_dev):
            acc = acc + buf[k]
        o_ref[...] = acc

    return pl.pallas_call(
        body, name="all_reduce_devices", in_specs=[VMEM_SPEC], out_specs=VMEM_SPEC,
        out_shape=jax.ShapeDtypeStruct(v.shape, F32),
        scratch_shapes=[pltpu.VMEM((n_dev,) + v.shape, F32), pltpu.SemaphoreType.DMA((n_dev - 1,)),
                        pltpu.SemaphoreType.DMA((n_dev - 1,))],
    )(v)


_SHARDED = (("w_in_even", 2), ("mla_w_uq", 2), ("mla_w_ukv", 2), ("w_out_even", 1), ("w_in_odd", 2), ("w_out_odd", 1),
            ("ffn_w_up", 2), ("ffn_w_down", 1),
            ("mix_norm_odd", 1), ("gla_w_gate_fwd", 2), ("gla_b_gate_fwd", 1), ("gla_w_gate_bwd", 2),
            ("gla_b_gate_bwd", 1), ("gla_out_norm", 2), ("ffn_conv_w", 2))
_N_MATRICES = 8
_REPLICATED = ("mix_norm_even", "mla_q_norm", "mla_kv_norm", "mla_q_head_norm", "mla_k_head_norm", "ret_theta_fwd",
               "ret_theta_bwd", "ret_out_norm", "ffn_norm", "ffn_conv_b")
_WEIGHTS = ("mix_norm_even", "w_in_even", "mla_q_norm", "mla_kv_norm", "mla_w_uq", "mla_w_ukv", "mla_q_head_norm",
            "mla_k_head_norm", "ret_theta_fwd", "ret_theta_bwd", "ret_out_norm", "w_out_even", "mix_norm_odd",
            "w_in_odd", "gla_w_gate_fwd", "gla_b_gate_fwd", "gla_w_gate_bwd", "gla_b_gate_bwd", "gla_out_norm",
            "w_out_odd", "ffn_norm", "ffn_w_up", "ffn_conv_w", "ffn_conv_b", "ffn_w_down")


def _flatten(arrs, row_multiple, dtype):
    flat = jnp.concatenate([a.reshape(-1).astype(dtype) for a in arrs])
    per = ROW * row_multiple
    total = -(-flat.shape[0] // per) * per
    return jnp.pad(flat, (0, total - flat.shape[0])).reshape(-1, ROW)


def _unflatten(flat, shapes):
    flat = flat.reshape(-1)
    out, o = [], 0
    for shp in shapes:
        n = math.prod(shp)
        out.append(flat[o:o + n].reshape(shp))
        o += n
    return out


def kernel(x, positions, mix_norm_even, w_in_even, mla_q_norm, mla_kv_norm, mla_w_uq, mla_w_ukv, mla_q_head_norm, mla_k_head_norm, ret_theta_fwd, ret_theta_bwd, ret_out_norm, w_out_even, mix_norm_odd, w_in_odd, gla_w_gate_fwd, gla_b_gate_fwd, gla_w_gate_bwd, gla_b_gate_bwd, gla_out_norm, w_out_odd, ffn_norm, ffn_w_up, ffn_conv_w, ffn_conv_b, ffn_w_down, loss_target, m_mix_norm_even, m_w_in_even, m_mla_q_norm, m_mla_kv_norm, m_mla_w_uq, m_mla_w_ukv, m_mla_q_head_norm, m_mla_k_head_norm, m_ret_theta_fwd, m_ret_theta_bwd, m_ret_out_norm, m_w_out_even, m_mix_norm_odd, m_w_in_odd, m_gla_w_gate_fwd, m_gla_b_gate_fwd, m_gla_w_gate_bwd, m_gla_b_gate_bwd, m_gla_out_norm, m_w_out_odd, m_ffn_norm, m_ffn_w_up, m_ffn_conv_w, m_ffn_conv_b, m_ffn_w_down, v_mix_norm_even, v_w_in_even, v_mla_q_norm, v_mla_kv_norm, v_mla_w_uq, v_mla_w_ukv, v_mla_q_head_norm, v_mla_k_head_norm, v_ret_theta_fwd, v_ret_theta_bwd, v_ret_out_norm, v_w_out_even, v_mix_norm_odd, v_w_in_odd, v_gla_w_gate_fwd, v_gla_b_gate_fwd, v_gla_w_gate_bwd, v_gla_b_gate_bwd, v_gla_out_norm, v_w_out_odd, v_ffn_norm, v_ffn_w_up, v_ffn_conv_w, v_ffn_conv_b, v_ffn_w_down):
    args = dict(locals())
    x2, pos, target = args["x"][0], args["positions"][0], args["loss_target"][0]
    axis = dict(_SHARDED)
    mats = [n for n, _ in _SHARDED[:_N_MATRICES]]
    smalls = [n for n, _ in _SHARDED[_N_MATRICES:]]
    small_shapes = [args[n].shape for n in smalls]

    local = {n: _bf(args[n]) for n in mats}
    first_layers = {n: (0, 0 if n.endswith("_odd") else 1) for n in mats}
    now = [(local[n],) + first_layers[n] for n in mats if first_layers[n][1]]
    later = [(local[n], first_layers[n][1], args[n].shape[0] - first_layers[n][1]) for n in mats]
    small_block = _flatten([args[n] for n in smalls], 2 * HALO, F32)
    got_now = _gather_chips(now + [(small_block, 0, small_block.shape[0])])
    per_chip = [_unflatten(got_now[-1][j], small_shapes) for j in range(CHIPS)]
    base = {n: args[n] for n in _REPLICATED}
    for k, n in enumerate(smalls):
        base[n] = jnp.concatenate([per_chip[j][k] for j in range(CHIPS)], axis=axis[n])

    def whole(stacks):
        full = dict(base)
        for n, per_layer in stacks.items():
            if n == "ffn_w_up":
                full[n] = per_layer
            else:
                full[n] = [None if st is None else jnp.concatenate([st[j, l] for j in range(CHIPS)], axis=axis[n] - 1)
                           for st, l in per_layer]
        return full

    stacks = {n: [(None, 0)] * args[n].shape[0] for n in mats}
    for (a, first, count), st in zip(now, got_now):
        n = next(m for m in mats if local[m] is a)
        stacks[n] = [(st, l) for l in range(count)] + stacks[n][count:]

    def finish(got_later):
        for (a, first, count), st in zip(later, got_later):
            n = next(m for m in mats if local[m] is a)
            stacks[n] = stacks[n][:first] + [(st, l) for l in range(count)]
        return whole(stacks)

    loss, grad_x, grads = _local_step(x2, pos, target, whole(stacks), side=later, finish=finish)
    loss = lax.psum(loss, ("x", "y", "c"))

    def by_chip(n, g):
        if n == "ffn_w_up":
            return g
        if axis[n] == 1:
            return g.reshape((CHIPS, g.shape[0] // CHIPS) + g.shape[1:])
        return jnp.stack(jnp.split(g, CHIPS, axis=axis[n] - 1))

    core = lax.axis_index("c").astype(jnp.int32).reshape(1)
    stacked = [jnp.stack([by_chip(n, g) for g in grads[n]], axis=1) for n in mats]
    small_parts = [jnp.split(jnp.stack(grads[n]), CHIPS, axis=axis[n]) for n in smalls]
    stacked.append(jnp.stack([_flatten([p[j] for p in small_parts], 2 * HALO, F32) for j in range(CHIPS)])[:, None])
    names = mats + ["small"]
    tiles = [_rows_tile(a.shape[2] // CORES, a.shape[3]) for a in stacked]
    got = _swap_halves(stacked)
    chip_sums = [_add_core_halves(a, b, core, ts=ts, name="add_core_halves_" + n)
                 for n, a, b, ts in zip(names, stacked, got, tiles)]
    parts = _scatter_chips(chip_sums)
    sums = [_add_chip_parts(p, core, ts=ts, name="add_chip_parts_" + n) for n, p, ts in zip(names, parts, tiles)]
    reduced = _gather_cores(sums)

    res = {}

    def update(n, w, g, m, v, ts):
        cols = g.shape[-1]
        outs = _adamw(w.reshape(-1, cols), g.reshape(-1, cols), m.reshape(-1, cols), v.reshape(-1, cols), ts=ts,
                      name="adamw_" + n)
        return [g] + [o.reshape(g.shape) for o in outs]

    kinds = ("grad", "delta", "new_m", "new_v")
    for n, g, ts in zip(mats, reduced, tiles):
        for kind, a in zip(kinds, update(n, args[n], g, args["m_" + n], args["v_" + n], ts)):
            res[kind + "_" + n] = a
    w_s, m_s, v_s = (_flatten([args[pre + n] for n in smalls], 2 * HALO, F32) for pre in ("", "m_", "v_"))
    for kind, flat in zip(kinds, update("small", w_s, reduced[-1][0], m_s, v_s, tiles[-1])):
        for n, a in zip(smalls, _unflatten(flat, small_shapes)):
            res[kind + "_" + n] = a

    rep_shapes = [args[n].shape for n in _REPLICATED]
    g_rep = _all_reduce_devices(_flatten([jnp.stack(grads[n]) for n in _REPLICATED], HALO, F32))
    w_rep, m_rep, v_rep = (_flatten([args[pre + n] for n in _REPLICATED], HALO, F32) for pre in ("", "m_", "v_"))
    for kind, flat in zip(kinds, update("replicated", w_rep, g_rep, m_rep, v_rep, g_rep.shape[0])):
        for n, a in zip(_REPLICATED, _unflatten(flat, rep_shapes)):
            res[kind + "_" + n] = a

    outs = [loss, grad_x[None]]
    for kind in ("grad", "delta", "new_m", "new_v"):
        outs += [res[kind + "_" + n] for n in _WEIGHTS]
    return tuple(outs)
```

```python
import math

import jax
import jax.numpy as jnp
from jax import lax
from jax.experimental import pallas as pl
from jax.experimental.pallas import tpu as pltpu

F32 = jnp.float32
BF16 = jnp.bfloat16
MESH = pl.DeviceIdType.MESH

EPS = 1e-6
D_MODEL = 1024
DEPTH = 4
LANES = 128
MLA_H, MLA_QR, MLA_KVR, MLA_NOPE, MLA_ROPE, MLA_V = 8, 384, 256, 64, 32, 64
MLA_QK = MLA_NOPE + MLA_ROPE
MLA_SCALE = MLA_QK ** -0.5
RET_H, RET_DK, RET_DV, RET_C = 8, 64, 64, 128
GLA_H, GLA_DK, GLA_DV, GLA_R, GLA_TAU, GLA_C = 4, 128, 256, 16, 16.0, 64
D_FF = 2816
ROPE_THETA = 10000.0
LN2 = math.log(2.0)
ADAM_LR, ADAM_B1, ADAM_B2, ADAM_EPS, ADAM_WD, ADAM_STEP = 0.001, 0.9, 0.999, 1e-08, 0.01, 10

EV_RET = 4 * RET_H * LANES
EV_CQ = 512
EV_W = 5120
EV_KR_BLK = (EV_RET + EV_CQ + MLA_KVR) // LANES
OD_W = 3200
OD_GA_BLK = 3072 // LANES

VMEM_LIMIT = 56 * 1024 * 1024
MM_TILE_CAP = 1408
V_ONES = (MLA_V, MLA_V + 1)
FLASH_FWD_ROWS = 1024
FLASH_BWD_ROWS = 1024
FLASH_KEYS = 1024


def _cp(sem):
    return pltpu.CompilerParams(dimension_semantics=sem, vmem_limit_bytes=VMEM_LIMIT)


def _dot(a, b):
    return jnp.dot(a, b, preferred_element_type=F32)


def _dot_nt(a, b):
    return lax.dot_general(a, b, (((1,), (1,)), ((), ())), preferred_element_type=F32)


def _dot_tn(a, b):
    return lax.dot_general(a, b, (((0,), (0,)), ((), ())), preferred_element_type=F32)


def _bf(x):
    return x.astype(BF16)


def _split3(x):
    h1 = _bf(x)
    r1 = x - h1.astype(F32)
    h2 = _bf(r1)
    h3 = _bf(r1 - h2.astype(F32))
    return h1, h2, h3


def _tile(n, cap):
    if n <= cap:
        return n
    best = None
    for t in range(LANES, cap + 1, LANES):
        if n % t == 0:
            best = t
    assert best is not None, n
    return best


def _mm(a, b, *, ta=False, tb=False, res=None, out_dtype=F32, b_layer=None, out_chips=False, name):
    assert not (ta and tb)
    if ta:
        kdim, m = a.shape
    else:
        m, kdim = a.shape
    if b_layer is not None:
        rows_b, cols_b = b.shape[2], b.shape[0] * b.shape[3]
    else:
        rows_b, cols_b = b.shape
    n, kb = (rows_b, cols_b) if tb else (cols_b, rows_b)
    assert kb == kdim, (a.shape, b.shape, ta, tb)
    tm, tn, tk = _tile(m, MM_TILE_CAP), _tile(n, MM_TILE_CAP), _tile(kdim, MM_TILE_CAP)
    nk = kdim // tk
    has_res = res is not None
    vmem = (2 * tm * tk * a.dtype.itemsize + 2 * tk * tn * b.dtype.itemsize
            + 2 * tm * tn * jnp.dtype(out_dtype).itemsize + (2 * tm * tn * 4 if has_res else 0)
            + (tm * tn * 4 if nk > 1 else 0))
    assert vmem <= VMEM_LIMIT - 8 * 1024 * 1024, (name, vmem)
    a_spec = (pl.BlockSpec((tk, tm), lambda i, j, k: (k, i)) if ta
              else pl.BlockSpec((tm, tk), lambda i, j, k: (i, k)))
    if b_layer is not None:
        per_chip = b.shape[3]
        if tb:
            assert tk == per_chip
            b_spec = pl.BlockSpec((None, None, tn, tk), lambda i, j, k: (k, b_layer, j, 0))
        else:
            assert tn == per_chip
            b_spec = pl.BlockSpec((None, None, tk, tn), lambda i, j, k: (j, b_layer, k, 0))
    else:
        b_spec = (pl.BlockSpec((tn, tk), lambda i, j, k: (j, k)) if tb
                  else pl.BlockSpec((tk, tn), lambda i, j, k: (k, j)))
    if out_chips:
        assert n // tn == CHIPS and not has_res
        o_spec = pl.BlockSpec((None, tm, tn), lambda i, j, k: (j, i, 0))
        out_struct = jax.ShapeDtypeStruct((CHIPS, m, tn), out_dtype)
    else:
        o_spec = pl.BlockSpec((tm, tn), lambda i, j, k: (i, j))
        out_struct = jax.ShapeDtypeStruct((m, n), out_dtype)

    def product(a_ref, b_ref):
        av, bv = _bf(a_ref[...]), _bf(b_ref[...])
        if ta:
            return _dot_tn(av, bv)
        if tb:
            return _dot_nt(av, bv)
        return _dot(av, bv)

    def body(*refs):
        a_ref, b_ref = refs[:2]
        r_ref = refs[2] if has_res else None
        o_ref = refs[3] if has_res else refs[2]

        def finish(r):
            if has_res:
                r = r + r_ref[...]
            o_ref[...] = r.astype(o_ref.dtype)

        if nk == 1:
            finish(product(a_ref, b_ref))
            return
        acc = refs[-1]
        k = pl.program_id(2)

        @pl.when(k == 0)
        def _():
            acc[...] = product(a_ref, b_ref)

        @pl.when(k > 0)
        def _():
            acc[...] += product(a_ref, b_ref)

        @pl.when(k == nk - 1)
        def _():
            finish(acc[...])

    ins = [a, b] + ([res] if has_res else [])
    in_specs = [a_spec, b_spec] + ([o_spec] if has_res else [])
    return pl.pallas_call(
        body, name=name, grid=(m // tm, n // tn, nk),
        in_specs=in_specs, out_specs=o_spec, out_shape=out_struct,
        scratch_shapes=[pltpu.VMEM((tm, tn), F32)] if nk > 1 else [],
        compiler_params=_cp(("parallel", "parallel", "arbitrary")),
    )(*ins)


def _ew(fn, rows, pars, outs, accs=(), *, s, ts, name):
    n_in = len(rows) + len(pars)
    n_o = len(outs)

    def body(*refs):
        i = pl.program_id(0)
        vals = fn(*[r[...] for r in refs[:n_in]])
        if not isinstance(vals, (tuple, list)):
            vals = (vals,)
        assert len(vals) == n_o + len(accs), (name, len(vals))
        for r, v in zip(refs[n_in:n_in + n_o], vals[:n_o]):
            r[...] = v.astype(r.dtype)
        for r, v in zip(refs[n_in + n_o:], vals[n_o:]):
            @pl.when(i == 0)
            def _(r=r, v=v):
                r[...] = v

            @pl.when(i > 0)
            def _(r=r, v=v):
                r[...] += v

    in_specs = [sp for _, sp in rows]
    in_specs += [pl.BlockSpec(p.shape, lambda i, nd=p.ndim: (0,) * nd) for p in pars]
    out_specs = [pl.BlockSpec((ts, w), lambda i: (i, 0)) for w, _ in outs]
    out_specs += [pl.BlockSpec((r, w), lambda i: (0, 0)) for r, w in accs]
    out_shape = [jax.ShapeDtypeStruct((s, w), dt) for w, dt in outs]
    out_shape += [jax.ShapeDtypeStruct((r, w), F32) for r, w in accs]
    return pl.pallas_call(
        body, name=name, grid=(s // ts,), in_specs=in_specs, out_specs=out_specs, out_shape=out_shape,
        compiler_params=_cp(("arbitrary",)),
    )(*[a for a, _ in rows], *pars)


def _cols(arr, width, blk, ts):
    return (arr, pl.BlockSpec((ts, width), lambda i, b=blk: (i, b)))


def _lead(pair, d, ts):
    return _cols(pair[d], pair[d].shape[1], 0, ts)


def _rowsum(x):
    return jnp.sum(x, axis=0, keepdims=True)


def _lanesum(x):
    return jnp.sum(x, axis=-1, keepdims=True)


def _gsum(x, group):
    w = x.shape[-1]
    if group == w:
        return jnp.broadcast_to(_lanesum(x), x.shape)
    parts = [jnp.broadcast_to(_lanesum(x[:, g:g + group]), (x.shape[0], group)) for g in range(0, w, group)]
    return jnp.concatenate(parts, axis=-1)


def _gn(x, gain, group, n):
    rstd = lax.rsqrt(_gsum(x * x, group) * (1.0 / n) + EPS)
    xn = x * rstd
    return xn * gain, xn, rstd


def _gn_bwd(dy, xn, rstd, gain, group, n):
    dxn = dy * gain
    dx = rstd * (dxn - xn * (_gsum(dxn * xn, group) * (1.0 / n)))
    return dx, _rowsum(dy * xn)


def _sigmoid(x):
    return 1.0 / (1.0 + jnp.exp(-x))


def _rmsnorm(x_row, g, *, n, s, ts, name):
    w = g.shape[-1]

    def fn(x, gv):
        return _gn(x, gv, w, n)[0]

    return _ew(fn, [x_row], [g], [(w, BF16)], s=s, ts=ts, name=name)[0]


def _rmsnorm_bwd(x_row, g, dh, dres, *, n, s, ts, name):
    w = g.shape[-1]
    has_res = dres is not None

    def fn(x, dhv, *rest):
        gv = rest[-1]
        _, xn, rstd = _gn(x, gv, w, n)
        dx, dg = _gn_bwd(dhv, xn, rstd, gv, w, n)
        if has_res:
            dx = dx + rest[0]
        return dx, dg

    rows = [x_row, _cols(dh, w, 0, ts)] + ([_cols(dres, w, 0, ts)] if has_res else [])
    return _ew(fn, rows, [g], [(w, F32)], [(1, w)], s=s, ts=ts, name=name)


def _rope_tables(pos, real, offset):
    half = real // 2
    inv = ROPE_THETA ** (-jnp.arange(half, dtype=F32) / half)
    ang = pos.astype(F32)[:, None] * inv
    c, sn = jnp.cos(ang), jnp.sin(ang)
    s = pos.shape[0]
    cos_t = jnp.concatenate([jnp.ones((s, offset), F32), c, c,
                             jnp.ones((s, LANES - offset - real), F32)], axis=1)
    sin_t = jnp.concatenate([jnp.zeros((s, offset), F32), -sn, sn,
                             jnp.zeros((s, LANES - offset - real), F32)], axis=1)
    return cos_t, sin_t


def _rope(x, cos_t, sin_t, real, offset):
    half = real // 2
    lane = lax.broadcasted_iota(jnp.int32, x.shape, 1)
    partner = jnp.where(lane < offset + half, pltpu.roll(x, LANES - half, 1), pltpu.roll(x, half, 1))
    return x * cos_t + partner * sin_t


def _mla_prep(q_pre, kv_pre, p_even, cos_m, sin_m, qhn, khn, *, s, ts):
    w = MLA_H * LANES

    def fn(qp, kp, vp, kr, c, sn, gq, gk):
        qs, ks = [], []
        for h in range(MLA_H):
            sl = slice(h * LANES, (h + 1) * LANES)
            qn = _gn(qp[:, sl], gq, LANES, MLA_QK)[0]
            kn = _gn(kp[:, sl] + kr, gk, LANES, MLA_QK)[0]
            qs.append(_rope(qn, c, sn, MLA_ROPE, MLA_NOPE) * MLA_SCALE)
            ks.append(_rope(kn, c, sn, MLA_ROPE, MLA_NOPE))
        lane = lax.broadcasted_iota(jnp.int32, vp.shape, 1) % LANES
        ones = (lane == V_ONES[0]) | (lane == V_ONES[1])
        return jnp.concatenate(qs, axis=1), jnp.concatenate(ks, axis=1), jnp.where(ones, 1.0, vp)

    rows = [_cols(q_pre, w, 0, ts), _cols(kv_pre, w, 0, ts), _cols(kv_pre, w, 1, ts),
            _cols(p_even, LANES, EV_KR_BLK, ts), _cols(cos_m, LANES, 0, ts), _cols(sin_m, LANES, 0, ts)]
    return _ew(fn, rows, [qhn, khn], [(w, BF16)] * 3, s=s, ts=ts, name="mla_prep")


def _mla_prep_bwd(q_pre, kv_pre, p_even, cos_m, sin_m, qhn, khn, dq, dk, *, s, ts):
    w = MLA_H * LANES

    def fn(qp, kp, kr, c, sn, dqv, dkv, gq, gk):
        dqs, dks = [], []
        dkr = jnp.zeros_like(kr)
        dgq = jnp.zeros((1, LANES), F32)
        dgk = jnp.zeros((1, LANES), F32)
        for h in range(MLA_H):
            sl = slice(h * LANES, (h + 1) * LANES)
            _, qn, qr = _gn(qp[:, sl], gq, LANES, MLA_QK)
            _, kn, krs = _gn(kp[:, sl] + kr, gk, LANES, MLA_QK)
            dqn = _rope(dqv[:, sl] * MLA_SCALE, c, -sn, MLA_ROPE, MLA_NOPE)
            dkn = _rope(dkv[:, sl], c, -sn, MLA_ROPE, MLA_NOPE)
            dqh, g1 = _gn_bwd(dqn, qn, qr, gq, LANES, MLA_QK)
            dkh, g2 = _gn_bwd(dkn, kn, krs, gk, LANES, MLA_QK)
            dqs.append(dqh)
            dks.append(dkh)
            dkr = dkr + dkh
            dgq = dgq + g1
            dgk = dgk + g2
        return jnp.concatenate(dqs, axis=1), jnp.concatenate(dks, axis=1), dkr, dgq, dgk

    rows = [_cols(q_pre, w, 0, ts), _cols(kv_pre, w, 0, ts), _cols(p_even, LANES, EV_KR_BLK, ts),
            _cols(cos_m, LANES, 0, ts), _cols(sin_m, LANES, 0, ts), _cols(dq, w, 0, ts), _cols(dk, w, 0, ts)]
    return _ew(fn, rows, [qhn, khn], [(w, BF16), (w, BF16), (LANES, BF16)], [(1, LANES), (1, LANES)],
               s=s, ts=ts, name="mla_prep_bwd")


def _flash_fwd(q, k, v, *, tq, tk, side=()):
    s = q.shape[0]
    nq, nk = s // tq, s // tk
    rq = tq
    ns = len(side)

    def body(*refs):
        q_ref, k_ref, v_ref = refs[:3]
        o_ref, lse_ref = refs[3 + ns:5 + ns]
        m_s, acc = refs[5 + 2 * ns:7 + 2 * ns]
        h, i, j = pl.program_id(0), pl.program_id(1), pl.program_id(2)
        if ns:
            local, sends, arrivals = _gather_copies(side, refs[3:3 + ns], refs[5 + ns:5 + 2 * ns], *refs[7 + 2 * ns:])

            @pl.when((h == 0) & (i == 0) & (j == 0))
            def _():
                for cp in local + sends:
                    cp.start()

        @pl.when(j == 0)
        def _():
            m_s[...] = jnp.full_like(m_s, -jnp.inf)
            acc[...] = jnp.zeros_like(acc)

        kv, vv = k_ref[...], v_ref[...]
        for r in range(0, tq, rq):
            rows = slice(r, r + rq)
            sc = _dot_nt(q_ref[rows, :], kv)
            m_prev = m_s[rows, :]
            m_new = jnp.maximum(m_prev, jnp.max(sc, axis=-1, keepdims=True))
            p = jnp.exp(sc - jnp.tile(m_new, (1, tk // LANES)))
            acc[rows, :] = jnp.exp(m_prev - m_new) * acc[rows, :] + _dot(_bf(p), vv)
            m_s[rows, :] = m_new

        @pl.when(j == nk - 1)
        def _():
            a = acc[...]
            l = a[:, V_ONES[0]:V_ONES[0] + 1]
            o_ref[...] = (a / l).astype(o_ref.dtype)
            lse_ref[...] = (m_s[...] + jnp.log(jnp.broadcast_to(l, (tq, LANES)))).T[0:1, :]

        if ns:
            @pl.when((h == MLA_H - 1) & (i == nq - 1) & (j == nk - 1))
            def _():
                for cp in arrivals:
                    cp.wait_recv()
                for cp in sends:
                    cp.wait_send()
                for cp in local:
                    cp.wait()

    qs = pl.BlockSpec((tq, LANES), lambda h, i, j: (i, h))
    ks = pl.BlockSpec((tk, LANES), lambda h, i, j: (j, h))
    outs = pl.pallas_call(
        body, name="mla_flash_fwd_gather" if ns else "mla_flash_fwd", grid=(MLA_H, nq, nk),
        in_specs=[qs, ks, ks] + [HBM_SPEC] * ns,
        out_specs=[qs, pl.BlockSpec((None, 1, tq), lambda h, i, j: (h, 0, i))] + [HBM_SPEC] * ns,
        out_shape=[jax.ShapeDtypeStruct((s, MLA_H * LANES), BF16), jax.ShapeDtypeStruct((MLA_H, 1, s), F32)]
        + _gather_shapes(side),
        scratch_shapes=[pltpu.VMEM((tq, LANES), F32), pltpu.VMEM((tq, LANES), F32)]
        + ([_sems(3 * ns), _sems(3 * ns), _sems(ns)] if ns else []),
        compiler_params=_cp(("arbitrary",) * 3 if ns else ("parallel", "parallel", "arbitrary")),
    )(q, k, v, *[a for a, _, _ in side])
    return outs[0], outs[1], list(outs[2:])


def _attn_bwd_prep(dar, o, *, s, ts):
    w = MLA_H * LANES

    def fn(dov, ov):
        outs = []
        lane = lax.broadcasted_iota(jnp.int32, (dov.shape[0], LANES), 1)
        for h in range(MLA_H):
            sl = slice(h * LANES, (h + 1) * LANES)
            d = dov[:, sl]
            delta = _lanesum(d * ov[:, sl].astype(F32))
            hi = _bf(delta).astype(F32)
            outs.append(jnp.where(lane == V_ONES[0], -hi, jnp.where(lane == V_ONES[1], hi - delta, d)))
        return jnp.concatenate(outs, axis=1)

    return _ew(fn, [_cols(dar, w, 0, ts), _cols(o, w, 0, ts)], [], [(w, BF16)], s=s, ts=ts,
               name="mla_attn_bwd_prep")[0]


def _flash_bwd(q, k, v, do, lse, *, tq, tk):
    s = q.shape[0]
    nq, nk = s // tq, s // tk

    def body(q_ref, k_ref, v_ref, do_ref, lse_ref, dq_ref, dk_ref, dv_ref, dk_acc, dv_acc):
        j = pl.program_id(1)
        i = pl.program_id(2)
        qv, kv, vv, dov = q_ref[...], k_ref[...], v_ref[...], do_ref[...]
        pt = jnp.exp(_dot_nt(kv, qv) - lse_ref[...])
        dst = _bf(pt * _dot_nt(vv, dov))
        dv_c = _dot(_bf(pt), dov)
        dk_c = _dot(dst, qv)
        dq_c = _dot_tn(dst, kv)
        rows = pl.ds(pl.multiple_of(i * tq, tq), tq)

        @pl.when(i == 0)
        def _():
            dk_acc[...] = dk_c
            dv_acc[...] = dv_c

        @pl.when(i > 0)
        def _():
            dk_acc[...] += dk_c
            dv_acc[...] += dv_c

        @pl.when(j == 0)
        def _():
            dq_ref[rows, :] = dq_c

        @pl.when(j > 0)
        def _():
            dq_ref[rows, :] += dq_c

        @pl.when(i == nq - 1)
        def _():
            dk_ref[...] = dk_acc[...]
            dv_ref[...] = dv_acc[...].astype(dv_ref.dtype)

    qs = pl.BlockSpec((tq, LANES), lambda h, j, i: (i, h))
    ks = pl.BlockSpec((tk, LANES), lambda h, j, i: (j, h))
    st = pl.BlockSpec((None, 1, tq), lambda h, j, i: (h, 0, i))
    return pl.pallas_call(
        body, name="mla_flash_bwd", grid=(MLA_H, nk, nq),
        in_specs=[qs, ks, ks, qs, st],
        out_specs=[pl.BlockSpec((s, LANES), lambda h, j, i: (0, h)), ks, ks],
        out_shape=[jax.ShapeDtypeStruct((s, MLA_H * LANES), F32), jax.ShapeDtypeStruct((s, MLA_H * LANES), F32),
                   jax.ShapeDtypeStruct((s, MLA_H * LANES), BF16)],
        scratch_shapes=[pltpu.VMEM((tk, LANES), F32), pltpu.VMEM((tk, LANES), F32)],
        compiler_params=_cp(("parallel", "arbitrary", "arbitrary")),
    )(q, k, v, do, lse)


def _ret_geometry(d, c):
    df = float(d)
    ii = lax.broadcasted_iota(jnp.int32, (c, c), 0).astype(F32)
    jj = lax.broadcasted_iota(jnp.int32, (c, c), 1).astype(F32)
    rel = (ii - jj) * (1.0 - 2.0 * df)
    mask = rel >= df
    rel0 = jnp.maximum(rel, 0.0)
    pos = lax.broadcasted_iota(jnp.int32, (c, 1), 0).astype(F32)
    ez = (c - 1 - pos) + df * (2.0 * pos - (c - 1))
    ex = (pos + 1.0) + df * (c - 1 - 2.0 * pos)
    return mask, rel0, ez, ex


def _chunk_index(n_chunks):
    return lambda d, n: n + d * (n_chunks - 1 - 2 * n)


def _ret_fwd(p_even, cos_r, sin_r, theta_l):
    s = p_even.shape[0]
    c = RET_C
    n_chunks = s // c
    w = RET_H * LANES
    cidx = _chunk_index(n_chunks)

    def body(*refs):
        n = pl.program_id(0)

        @pl.when(n == 0)
        def _():
            for r_s in refs[16:18]:
                r_s[...] = jnp.zeros_like(r_s)

        stores = []
        for d in range(2):
            stores += one(d, *refs[6 * d:6 * d + 6], *refs[12 + 2 * d:14 + 2 * d], refs[16 + d])
        for ref, val in stores:
            ref[...] = val

    def one(d, q_ref, k_ref, v_ref, cos_ref, sin_ref, th_ref, o_ref, rp_ref, r_s):
        lg = jnp.log1p(-jnp.exp(-th_ref[...] * LN2))
        mask, rel0, ez, ex = _ret_geometry(d, c)
        cs, sn = cos_ref[...], sin_ref[...]
        r_all = r_s[...]
        outs, states = [], []
        for h in range(RET_H):
            sl = slice(h * LANES, (h + 1) * LANES)
            lgh = lg[:, h * LANES:h * LANES + 1]
            dm = jnp.where(mask, jnp.exp(lgh * rel0), 0.0)
            qh = _bf(_rope(q_ref[:, sl], cs, sn, RET_DK, 0))
            kf = _rope(k_ref[:, sl], cs, sn, RET_DK, 0) * (RET_DK ** -0.5)
            kh = _bf(kf)
            vh = _bf(v_ref[:, sl])
            rh = r_all[sl, :]
            a = _dot_nt(qh, kh) * dm
            outs.append(_dot(_bf(a), vh) + jnp.exp(lgh * ex) * _dot(qh, _bf(rh)))
            zk = _bf(kf * jnp.exp(lgh * ez))
            states.append(jnp.exp(lgh * c) * rh + _dot_tn(zk, vh))
        return [(rp_ref, r_all), (o_ref, jnp.concatenate(outs, axis=1)), (r_s, jnp.concatenate(states, axis=0))]

    def ins(d):
        col = lambda blk: pl.BlockSpec((c, w), lambda n: (cidx(d, n), blk))
        tab = pl.BlockSpec((c, LANES), lambda n: (cidx(d, n), 0))
        return [col(0), col(1), col(2), tab, tab, pl.BlockSpec((None, 1, w), lambda n: (d, 0, 0))]

    def outs(d):
        return [pl.BlockSpec((c, w), lambda n: (cidx(d, n), 0)),
                pl.BlockSpec((None, w, LANES), lambda n: (cidx(d, n), 0, 0))]

    o_f, r_f, o_b, r_b = pl.pallas_call(
        body, name="ret_fwd", grid=(n_chunks,),
        in_specs=ins(0) + ins(1), out_specs=outs(0) + outs(1),
        out_shape=[jax.ShapeDtypeStruct((s, w), F32), jax.ShapeDtypeStruct((n_chunks, w, LANES), F32)] * 2,
        scratch_shapes=[pltpu.VMEM((w, LANES), F32)] * 2,
        compiler_params=_cp(("arbitrary",)),
    )(*[p_even, p_even, p_even, cos_r, sin_r, theta_l] * 2)
    return (o_f, o_b), (r_f, r_b)


def _ret_bwd(p_even, cos_r, sin_r, theta_l, theta_h, r_prev, do):
    s = p_even.shape[0]
    c = RET_C
    n_chunks = s // c
    w = RET_H * LANES
    fwd_idx = _chunk_index(n_chunks)

    def cidx(d, n):
        return fwd_idx(d, n_chunks - 1 - n)

    def body(*refs):
        n = pl.program_id(0)

        @pl.when(n == 0)
        def _():
            for d in range(2):
                refs[26 + d][...] = jnp.zeros_like(refs[26 + d])
                refs[21 + 4 * d][...] = jnp.zeros_like(refs[21 + 4 * d])

        stores = []
        for d in range(2):
            stores += one(d, *refs[9 * d:9 * d + 9], *refs[18 + 4 * d:22 + 4 * d], refs[26 + d])
        for ref, val, accumulate in stores:
            if accumulate:
                ref[...] += val
            else:
                ref[...] = val

    def one(d, q_ref, k_ref, v_ref, cos_ref, sin_ref, th_ref, thh_ref, rp_ref, do_ref,
            dq_ref, dk_ref, dv_ref, dth_ref, dr_s):
        lg = jnp.log1p(-jnp.exp(-th_ref[...] * LN2))
        mask, rel0, ez, ex = _ret_geometry(d, c)
        cs, sn = cos_ref[...], sin_ref[...]
        rp_all, dr_all = rp_ref[...], dr_s[...]
        row = lax.broadcasted_iota(jnp.int32, (RET_H, LANES), 0)
        dlg = jnp.zeros((RET_H, LANES), F32)
        kscale = RET_DK ** -0.5
        dqs, dks, dvs, drs = [], [], [], []
        for h in range(RET_H):
            sl = slice(h * LANES, (h + 1) * LANES)
            lgh = lg[:, h * LANES:h * LANES + 1]
            dm = jnp.where(mask, jnp.exp(lgh * rel0), 0.0)
            zeta = jnp.exp(lgh * ez)
            xi = jnp.exp(lgh * ex)
            gc = jnp.exp(lgh * c)
            qf = _rope(q_ref[:, sl], cs, sn, RET_DK, 0)
            qh = _bf(qf)
            kf = _rope(k_ref[:, sl], cs, sn, RET_DK, 0) * kscale
            kh = _bf(kf)
            zkf = kf * zeta
            zk = _bf(zkf)
            vh = _bf(v_ref[:, sl])
            dof = do_ref[:, sl]
            doh = _bf(dof)
            rp = rp_all[sl, :]
            rpb = _bf(rp)
            drn = dr_all[sl, :]
            drb = _bf(drn)
            a = _dot_nt(qh, kh) * dm
            da0 = _dot_nt(doh, vh)
            da = _bf(da0 * dm)
            vdr = _dot_nt(vh, drb)
            dq_r = _dot(da, kh) + xi * _dot_nt(doh, rpb)
            dk_r = _dot_tn(da, qh) + zeta * vdr
            dvs.append(_dot_tn(_bf(a), doh) + _dot(zk, drb))
            dqs.append(_rope(dq_r, cs, -sn, RET_DK, 0))
            dks.append(_rope(dk_r * kscale, cs, -sn, RET_DK, 0))
            drs.append(_dot_tn(_bf(qf * xi), doh) + gc * drn)
            ocross = xi * _dot(qh, rpb)
            t = (jnp.sum(rel0 * a * da0, keepdims=True)
                 + jnp.sum(ex * dof * ocross, keepdims=True)
                 + c * gc * jnp.sum(drn * rp, keepdims=True)
                 + jnp.sum(ez * zkf * vdr, keepdims=True))
            dlg = jnp.where(row == h, t, dlg)
        x2 = jnp.exp(-thh_ref[...] * LN2)
        return [(dq_ref, jnp.concatenate(dqs, axis=1), False), (dk_ref, jnp.concatenate(dks, axis=1), False),
                (dv_ref, jnp.concatenate(dvs, axis=1), False), (dr_s, jnp.concatenate(drs, axis=0), False),
                (dth_ref, dlg * (x2 * LN2 / (1.0 - x2)), True)]

    def ins(d):
        col = lambda blk: pl.BlockSpec((c, w), lambda n: (cidx(d, n), blk))
        tab = pl.BlockSpec((c, LANES), lambda n: (cidx(d, n), 0))
        return [col(0), col(1), col(2), tab, tab, pl.BlockSpec((None, 1, w), lambda n: (d, 0, 0)),
                pl.BlockSpec((None, RET_H, LANES), lambda n: (d, 0, 0)),
                pl.BlockSpec((None, w, LANES), lambda n: (cidx(d, n), 0, 0)), col(0)]

    def outs(d):
        row = pl.BlockSpec((c, w), lambda n: (cidx(d, n), 0))
        return [row, row, row, pl.BlockSpec((RET_H, LANES), lambda n: (0, 0))]

    res = pl.pallas_call(
        body, name="ret_bwd", grid=(n_chunks,),
        in_specs=ins(0) + ins(1), out_specs=outs(0) + outs(1),
        out_shape=([jax.ShapeDtypeStruct((s, w), F32)] * 3 + [jax.ShapeDtypeStruct((RET_H, LANES), F32)]) * 2,
        scratch_shapes=[pltpu.VMEM((w, LANES), F32)] * 2,
        compiler_params=_cp(("arbitrary",)),
    )(*[a for d in range(2) for a in (p_even, p_even, p_even, cos_r, sin_r, theta_l, theta_h, r_prev[d], do)])
    return (res[0], res[4]), (res[1], res[5]), (res[2], res[6]), jnp.stack([res[3], res[7]])


def _post_fwd(o2, gate_row, gain, *, group, n, s, ts, name):
    w = o2[0].shape[1]

    def fn(of, ob, g, gv):
        y = _gn(of + ob, gv, group, n)[0]
        return g * _sigmoid(g) * y

    return _ew(fn, [_lead(o2, 0, ts), _lead(o2, 1, ts), gate_row], [gain], [(w, BF16)], s=s, ts=ts, name=name)[0]


def _post_bwd(o2, gate_row, gain, dr_row, *, group, n, s, ts, name):
    w = o2[0].shape[1]

    def fn(of, ob, g, dr, gv):
        y, xn, rstd = _gn(of + ob, gv, group, n)
        sg = _sigmoid(g)
        dy = dr * (g * sg)
        dgate = dr * y * (sg * (1.0 + g * (1.0 - sg)))
        do, dgain = _gn_bwd(dy, xn, rstd, gv, group, n)
        return do, dgate, dgain

    return _ew(fn, [_lead(o2, 0, ts), _lead(o2, 1, ts), gate_row, dr_row], [gain],
               [(w, F32), (w, BF16)], [(1, w)], s=s, ts=ts, name=name)


def _sum2(a2, *, s, ts, name):
    w = a2[0].shape[1]
    return _ew(lambda a, b: a + b, [_lead(a2, 0, ts), _lead(a2, 1, ts)], [], [(w, BF16)], s=s, ts=ts, name=name)[0]


def _gla_common(d, q_ref, k_ref, ga_ref, wg_ref, bg_ref):
    c = GLA_C
    df = float(d)
    ii = lax.broadcasted_iota(jnp.int32, (c, c), 0).astype(F32)
    jj = lax.broadcasted_iota(jnp.int32, (c, c), 1).astype(F32)
    rel = (ii - jj) * (1.0 - 2.0 * df)
    tri = _bf(jnp.where(rel >= 0.0, 1.0, 0.0))
    mask = rel >= df
    gab = _bf(ga_ref[...])
    z = _dot(gab, wg_ref[...]) + bg_ref[...]
    la = (jnp.minimum(z, 0.0) - jnp.log1p(jnp.exp(-jnp.abs(z)))) * (1.0 / GLA_TAU)
    l1, l2, l3 = _split3(la)
    b = _dot(tri, l1) + _dot(tri, l2) + _dot(tri, l3)
    first = d == 0
    bm = b[c // 2:c // 2 + 1] if first else b[c // 2 - 1:c // 2]
    bl = b[c - 1:c] if first else b[0:1]
    q = q_ref[...] * (GLA_DK ** -0.5)
    k = k_ref[...]
    e1, e2, e3, eb = jnp.exp(b - bm), jnp.exp(bm - b), jnp.exp(bl - b), jnp.exp(b)
    return dict(tri=tri, mask=mask, gab=gab, z=z, ebl=jnp.exp(bl), e1=e1, e2=e2, e3=e3, eb=eb,
                qc=q * e1, kc=k * e2, kd=k * e3, qe=q * eb, first=first)


def _col_scale(row_vec, width):
    t = jnp.broadcast_to(row_vec, (LANES, LANES)).T
    return jnp.concatenate([t] * (width // LANES), axis=1)


def _gla_fwd(p_odd, wg2, bg2):
    s = p_odd.shape[0]
    c = GLA_C
    n_chunks = s // c
    wk, wv = GLA_H * GLA_DK, GLA_H * GLA_DV
    cidx = _chunk_index(n_chunks)

    def body(*refs):
        n = pl.program_id(0)

        @pl.when(n == 0)
        def _():
            for s_s in refs[16:18]:
                s_s[...] = jnp.zeros_like(s_s)

        stores = []
        for d in range(2):
            stores += one(d, *refs[6 * d:6 * d + 6], *refs[12 + 2 * d:14 + 2 * d], refs[16 + d])
        for ref, val in stores:
            ref[...] = val

    def one(d, q_ref, k_ref, v_ref, ga_ref, wg_ref, bg_ref, o_ref, sp_ref, s_s):
        g = _gla_common(d, q_ref, k_ref, ga_ref, wg_ref, bg_ref)
        s_all = s_s[...]
        outs, states = [], []
        for h in range(GLA_H):
            sl = slice(h * GLA_DK, (h + 1) * GLA_DK)
            vs = slice(h * GLA_DV, (h + 1) * GLA_DV)
            vh = _bf(v_ref[:, vs])
            sh = s_all[sl, :]
            a = jnp.where(g["mask"], _dot_nt(_bf(g["qc"][:, sl]), _bf(g["kc"][:, sl])), 0.0)
            outs.append(_dot(_bf(a), vh) + _dot(_bf(g["qe"][:, sl]), _bf(sh)))
            states.append(_col_scale(g["ebl"][:, sl], GLA_DV) * sh + _dot_tn(_bf(g["kd"][:, sl]), vh))
        return [(sp_ref, s_all), (o_ref, jnp.concatenate(outs, axis=1)), (s_s, jnp.concatenate(states, axis=0))]

    def ins(d):
        col = lambda width, blk: pl.BlockSpec((c, width), lambda n: (cidx(d, n), blk))
        return [col(wk, 0), col(wk, 1), col(wv, 1), col(LANES, OD_GA_BLK),
                pl.BlockSpec((None, LANES, wk), lambda n: (d, 0, 0)), pl.BlockSpec((None, 1, wk), lambda n: (d, 0, 0))]

    def outs(d):
        return [pl.BlockSpec((c, wv), lambda n: (cidx(d, n), 0)),
                pl.BlockSpec((None, wk, GLA_DV), lambda n: (cidx(d, n), 0, 0))]

    o_f, s_f, o_b, s_b = pl.pallas_call(
        body, name="gla_fwd", grid=(n_chunks,),
        in_specs=ins(0) + ins(1), out_specs=outs(0) + outs(1),
        out_shape=[jax.ShapeDtypeStruct((s, wv), F32), jax.ShapeDtypeStruct((n_chunks, wk, GLA_DV), F32)] * 2,
        scratch_shapes=[pltpu.VMEM((wk, GLA_DV), F32)] * 2,
        compiler_params=_cp(("arbitrary",)),
    )(*[p_odd, p_odd, p_odd, p_odd, wg2, bg2] * 2)
    return (o_f, o_b), (s_f, s_b)


def _gla_bwd(p_odd, wg2, bg2, s_prev, do):
    s = p_odd.shape[0]
    c = GLA_C
    n_chunks = s // c
    wk, wv = GLA_H * GLA_DK, GLA_H * GLA_DV
    fwd_idx = _chunk_index(n_chunks)

    def cidx(d, n):
        return fwd_idx(d, n_chunks - 1 - n)

    def body(*refs):
        n = pl.program_id(0)

        @pl.when(n == 0)
        def _():
            for d in range(2):
                for r in (refs[28 + d], refs[20 + 6 * d], refs[21 + 6 * d]):
                    r[...] = jnp.zeros_like(r)

        stores = []
        for d in range(2):
            stores += one(d, *refs[8 * d:8 * d + 8], *refs[16 + 6 * d:22 + 6 * d], refs[28 + d])
        for ref, val, accumulate in stores:
            if accumulate:
                ref[...] += val
            else:
                ref[...] = val

    def one(d, q_ref, k_ref, v_ref, ga_ref, wg_ref, bg_ref, sp_ref, do_ref,
            dq_ref, dk_ref, dv_ref, dga_ref, dwg_ref, dbg_ref, ds_s):
        g = _gla_common(d, q_ref, k_ref, ga_ref, wg_ref, bg_ref)
        mask = g["mask"]
        ones8 = jnp.ones((8, GLA_DV), BF16)
        sp_all, ds_all = sp_ref[...], ds_s[...]
        dbs, dbms, dbls = [], [], []
        dqs, dks, dvs, dss = [], [], [], []
        for h in range(GLA_H):
            sl = slice(h * GLA_DK, (h + 1) * GLA_DK)
            vs = slice(h * GLA_DV, (h + 1) * GLA_DV)
            qc, kc, kd, qe = g["qc"][:, sl], g["kc"][:, sl], g["kd"][:, sl], g["qe"][:, sl]
            qcb, kcb, kdb, qeb = _bf(qc), _bf(kc), _bf(kd), _bf(qe)
            vh = _bf(v_ref[:, vs])
            doh = _bf(do_ref[:, vs])
            sp = sp_all[sl, :]
            dsn = ds_all[sl, :]
            dsb = _bf(dsn)
            a = _bf(jnp.where(mask, _dot_nt(qcb, kcb), 0.0))
            da = _bf(jnp.where(mask, _dot_nt(doh, vh), 0.0))
            dvs.append(_dot_tn(a, doh) + _dot(kdb, dsb))
            dqc = _dot(da, kcb)
            dkc = _dot_tn(da, qcb)
            dqe = _dot_nt(doh, _bf(sp))
            dkd = _dot_nt(vh, dsb)
            dss.append(_dot_tn(qeb, doh) + _col_scale(g["ebl"][:, sl], GLA_DV) * dsn)
            dqs.append((dqc * g["e1"][:, sl] + dqe * g["eb"][:, sl]) * (GLA_DK ** -0.5))
            dks.append(dkc * g["e2"][:, sl] + dkd * g["e3"][:, sl])
            t1, t2, t3, t4 = dqc * qc, dkc * kc, dqe * qe, dkd * kd
            dbs.append(t1 - t2 + t3 - t4)
            dbms.append(_rowsum(t2 - t1))
            m1, m2, _ = _split3(dsn * sp)
            rs = (_dot_nt(ones8, m1) + _dot_nt(ones8, m2))[0:1]
            dbls.append(_rowsum(t4) + g["ebl"][:, sl] * rs)
        db = jnp.concatenate(dbs, axis=1)
        dbm = jnp.concatenate(dbms, axis=1)
        dbl = jnp.concatenate(dbls, axis=1)
        row = lax.broadcasted_iota(jnp.int32, (c, wk), 0)
        mid = jnp.where(g["first"], c // 2, c // 2 - 1)
        last = jnp.where(g["first"], c - 1, 0)
        db = db + jnp.where(row == mid, dbm, 0.0) + jnp.where(row == last, dbl, 0.0)
        d1, d2, d3 = _split3(db)
        tri = g["tri"]
        dla = _dot_tn(tri, d1) + _dot_tn(tri, d2) + _dot_tn(tri, d3)
        dz = dla * (1.0 / GLA_TAU) * (1.0 - _sigmoid(g["z"]))
        dzb = _bf(dz)
        return [(dq_ref, jnp.concatenate(dqs, axis=1), False), (dk_ref, jnp.concatenate(dks, axis=1), False),
                (dv_ref, jnp.concatenate(dvs, axis=1), False), (ds_s, jnp.concatenate(dss, axis=0), False),
                (dga_ref, _dot_nt(dzb, wg_ref[...]), False), (dwg_ref, _dot_tn(g["gab"], dzb), True),
                (dbg_ref, _rowsum(dz), True)]

    def ins(d):
        col = lambda width, blk: pl.BlockSpec((c, width), lambda n: (cidx(d, n), blk))
        return [col(wk, 0), col(wk, 1), col(wv, 1), col(LANES, OD_GA_BLK),
                pl.BlockSpec((None, LANES, wk), lambda n: (d, 0, 0)), pl.BlockSpec((None, 1, wk), lambda n: (d, 0, 0)),
                pl.BlockSpec((None, wk, GLA_DV), lambda n: (cidx(d, n), 0, 0)), col(wv, 0)]

    def outs(d):
        row = lambda width: pl.BlockSpec((c, width), lambda n: (cidx(d, n), 0))
        return [row(wk), row(wk), row(wv), row(LANES),
                pl.BlockSpec((LANES, wk), lambda n: (0, 0)), pl.BlockSpec((1, wk), lambda n: (0, 0))]

    shapes = [jax.ShapeDtypeStruct((s, wk), F32), jax.ShapeDtypeStruct((s, wk), F32), jax.ShapeDtypeStruct((s, wv), F32),
              jax.ShapeDtypeStruct((s, LANES), F32), jax.ShapeDtypeStruct((LANES, wk), F32),
              jax.ShapeDtypeStruct((1, wk), F32)]
    res = pl.pallas_call(
        body, name="gla_bwd", grid=(n_chunks,),
        in_specs=ins(0) + ins(1), out_specs=outs(0) + outs(1), out_shape=shapes * 2,
        scratch_shapes=[pltpu.VMEM((wk, GLA_DV), F32)] * 2,
        compiler_params=_cp(("arbitrary",)),
    )(*[a for d in range(2) for a in (p_odd, p_odd, p_odd, p_odd, wg2, bg2, s_prev[d], do)])
    pair = lambda k: (res[k], res[6 + k])
    return pair(0), pair(1), pair(2), pair(3), jnp.stack(pair(4)), jnp.stack(pair(5))


HALO = 8


def _halo_specs(width_blk, col0, ts, s):
    r = ts // HALO
    last = s // HALO - 1
    cur = pl.BlockSpec((ts, width_blk), lambda j, i: (i, col0 + j))
    prev = pl.BlockSpec((HALO, width_blk), lambda j, i: (jnp.maximum(i * r - 1, 0), col0 + j))
    nxt = pl.BlockSpec((HALO, width_blk), lambda j, i: (jnp.minimum((i + 1) * r, last), col0 + j))
    return [prev, cur, nxt]


def _with_halo(prev_ref, cur_ref, next_ref, i, n_i):
    p = jnp.where(i == 0, 0.0, prev_ref[...])
    q = jnp.where(i == n_i - 1, 0.0, next_ref[...])
    return jnp.concatenate([p, cur_ref[...], q], axis=0)


def _shift_down(x):
    return pltpu.roll(x, 1, 0)


def _shift_up(x):
    return pltpu.roll(x, x.shape[0] - 1, 0)


def _ffn_act(up, conv_w, conv_b, *, ts):
    s = up.shape[0]
    tc = _tile(D_FF, 1408)
    nj = D_FF // tc
    n_i = s // ts

    def body(gp, gc, gn, val_ref, w_ref, b_ref, a_ref):
        i = pl.program_id(1)
        g = _with_halo(gp, gc, gn, i, n_i)
        w = w_ref[...]
        conv = w[0:1] * _shift_down(g) + w[1:2] * g + w[2:3] * _shift_up(g) + b_ref[...]
        conv = conv[HALO:HALO + ts]
        a_ref[...] = (conv * _sigmoid(conv) * val_ref[...]).astype(a_ref.dtype)

    return pl.pallas_call(
        body, name="ffn_act", grid=(nj, n_i),
        in_specs=_halo_specs(tc, 0, ts, s) + [pl.BlockSpec((ts, tc), lambda j, i: (i, nj + j)),
                                              pl.BlockSpec((3, tc), lambda j, i: (0, j)),
                                              pl.BlockSpec((1, tc), lambda j, i: (0, j))],
        out_specs=pl.BlockSpec((ts, tc), lambda j, i: (i, j)),
        out_shape=jax.ShapeDtypeStruct((s, D_FF), BF16),
        compiler_params=_cp(("parallel", "arbitrary")),
    )(up, up, up, up, conv_w, conv_b)


def _ffn_act_bwd(up, da, conv_w, conv_b, *, ts):
    s = up.shape[0]
    tc = _tile(D_FF, 1408)
    nj = D_FF // tc
    n_i = s // ts

    def body(gp, gc, gn, vp, vc, vn, dp, dc, dn, w_ref, b_ref, dg_ref, dval_ref, dw_ref, db_ref):
        i = pl.program_id(1)
        g = _with_halo(gp, gc, gn, i, n_i)
        v = _with_halo(vp, vc, vn, i, n_i)
        dav = _with_halo(dp, dc, dn, i, n_i)
        w = w_ref[...]
        gm, gpl = _shift_down(g), _shift_up(g)
        conv = w[0:1] * gm + w[1:2] * g + w[2:3] * gpl + b_ref[...]
        sg = _sigmoid(conv)
        dgc = dav * v * (sg * (1.0 + conv * (1.0 - sg)))
        dgate = w[0:1] * _shift_up(dgc) + w[1:2] * dgc + w[2:3] * _shift_down(dgc)
        ctr = slice(HALO, HALO + ts)
        dg_ref[...] = dgate[ctr].astype(dg_ref.dtype)
        dval_ref[...] = (dav[ctr] * (conv * sg)[ctr]).astype(dval_ref.dtype)
        dgc_c = dgc[ctr]
        dw = jnp.concatenate([_rowsum(dgc_c * gm[ctr]), _rowsum(dgc_c * g[ctr]), _rowsum(dgc_c * gpl[ctr])], axis=0)
        dbv = _rowsum(dgc_c)

        @pl.when(i == 0)
        def _():
            dw_ref[...] = dw
            db_ref[...] = dbv

        @pl.when(i > 0)
        def _():
            dw_ref[...] += dw
            db_ref[...] += dbv

    tile = pl.BlockSpec((ts, tc), lambda j, i: (i, j))
    return pl.pallas_call(
        body, name="ffn_act_bwd", grid=(nj, n_i),
        in_specs=(_halo_specs(tc, 0, ts, s) + _halo_specs(tc, nj, ts, s) + _halo_specs(tc, 0, ts, s)
                  + [pl.BlockSpec((3, tc), lambda j, i: (0, j)), pl.BlockSpec((1, tc), lambda j, i: (0, j))]),
        out_specs=[tile, tile, pl.BlockSpec((3, tc), lambda j, i: (0, j)), pl.BlockSpec((1, tc), lambda j, i: (0, j))],
        out_shape=[jax.ShapeDtypeStruct((s, D_FF), BF16), jax.ShapeDtypeStruct((s, D_FF), BF16),
                   jax.ShapeDtypeStruct((3, D_FF), F32), jax.ShapeDtypeStruct((1, D_FF), F32)],
        compiler_params=_cp(("parallel", "arbitrary")),
    )(up, up, up, up, up, up, da, da, da, conv_w, conv_b)


def _loss_head(y, target, *, s, ts):
    def fn(yv, tv):
        err = yv - tv
        return err * (1.0 / D_MODEL), _rowsum(err * err)

    return _ew(fn, [_cols(y, D_MODEL, 0, ts), _cols(target, D_MODEL, 0, ts)], [], [(D_MODEL, F32)],
               [(1, D_MODEL)], s=s, ts=ts, name="loss_head")


def _rows_tile(r, width):
    ts = r
    while ts * width * 4 > (1 << 20) and ts % 16 == 0:
        ts //= 2
    return ts


def _adamw(w, g, m, v, *, ts, name):
    r, width = w.shape
    assert r % ts == 0

    def fn(wv, gv, mv, vv):
        mn = ADAM_B1 * mv + (1.0 - ADAM_B1) * gv
        vn = ADAM_B2 * vv + (1.0 - ADAM_B2) * (gv * gv)
        m_hat = mn / (1.0 - ADAM_B1 ** ADAM_STEP)
        v_hat = vn / (1.0 - ADAM_B2 ** ADAM_STEP)
        delta = -ADAM_LR * (m_hat / (jnp.sqrt(v_hat) + ADAM_EPS) + ADAM_WD * wv)
        return delta, mn, vn

    rows = [_cols(a, width, 0, ts) for a in (w, g, m, v)]
    return _ew(fn, rows, [], [(width, F32)] * 3, s=r, ts=ts, name=name)


def _pad_heads(w, heads, real):
    lead = w.shape[:-1]
    w = w.reshape(lead + (heads, real))
    w = jnp.pad(w, [(0, 0)] * len(lead) + [(0, 0), (0, LANES - real)])
    return w.reshape(lead + (heads * LANES,))


def _pad_head_rows(w, heads, real):
    return _pad_heads(w.T, heads, real).T


def _pack_even(p):
    w_in = p["w_in"]
    z = lambda n: jnp.zeros((D_MODEL, n), w_in.dtype)
    o = 0
    parts = {}
    for nm, n in (("cq", MLA_QR), ("ckv", MLA_KVR), ("kr", MLA_ROPE), ("rq", 512), ("rk", 512), ("rv", 512), ("rg", 512)):
        parts[nm] = w_in[:, o:o + n]
        o += n
    w_in_p = jnp.concatenate(
        [_pad_heads(parts[k], RET_H, RET_DK) for k in ("rq", "rk", "rv", "rg")]
        + [parts["cq"], z(EV_CQ - MLA_QR), parts["ckv"], z(MLA_NOPE), parts["kr"], z(LANES - MLA_QK), z(LANES)], axis=1)
    w_uq = jnp.pad(_pad_heads(p["w_uq"], MLA_H, MLA_QK), ((0, EV_CQ - MLA_QR), (0, 0)))
    ukv = p["w_ukv"].reshape(MLA_KVR, MLA_H, MLA_NOPE + MLA_V)
    w_ukv = jnp.concatenate([_pad_heads(ukv[..., :MLA_NOPE].reshape(MLA_KVR, -1), MLA_H, MLA_NOPE),
                             _pad_heads(ukv[..., MLA_NOPE:].reshape(MLA_KVR, -1), MLA_H, MLA_V)], axis=1)
    w_out = jnp.concatenate([_pad_head_rows(p["w_out"][:MLA_H * MLA_V], MLA_H, MLA_V),
                             _pad_head_rows(p["w_out"][MLA_H * MLA_V:], RET_H, RET_DV)], axis=0)
    return dict(
        w_in=w_in_p, w_uq=w_uq, w_ukv=w_ukv, w_out=w_out,
        mix_g=p["mix_norm"][None, :],
        q_norm=jnp.pad(p["q_norm"], (0, EV_CQ - MLA_QR))[None, :],
        kv_norm=p["kv_norm"][None, :],
        qhn=jnp.pad(p["q_head_norm"], (0, LANES - MLA_QK))[None, :],
        khn=jnp.pad(p["k_head_norm"], (0, LANES - MLA_QK))[None, :],
        ret_gain=_pad_heads(p["ret_out_norm"].reshape(-1), RET_H, RET_DV)[None, :],
    )


def _pack_odd(p):
    w_in = p["w_in"]
    ga = w_in[:, 3072:]
    w_in_p = jnp.concatenate([w_in[:, :3072], ga, jnp.zeros((D_MODEL, LANES - 2 * GLA_R), w_in.dtype)], axis=1)
    wk = GLA_H * GLA_DK
    zf = jnp.zeros((LANES - GLA_R, wk), p["w_gate_fwd"].dtype)
    zb0 = jnp.zeros((GLA_R, wk), p["w_gate_fwd"].dtype)
    zb1 = jnp.zeros((LANES - 2 * GLA_R, wk), p["w_gate_fwd"].dtype)
    wg2 = jnp.stack([jnp.concatenate([p["w_gate_fwd"], zf], axis=0),
                     jnp.concatenate([zb0, p["w_gate_bwd"], zb1], axis=0)])
    bg2 = jnp.stack([p["b_gate_fwd"][None, :], p["b_gate_bwd"][None, :]])
    return dict(w_in=w_in_p, wg2=wg2, bg2=bg2, w_out=p["w_out"], mix_g=p["mix_norm"][None, :],
                gla_gain=p["gla_out_norm"].reshape(1, -1))


_MATRICES = ("w_in", "w_uq", "w_ukv", "w_out", "wg2")


def _packed(pack_fn, p):
    packed = pack_fn(p)
    packed = {k: (_bf(v) if k in _MATRICES else v.astype(F32)) for k, v in packed.items()}
    shapes = {k: jax.ShapeDtypeStruct(v.shape, F32) for k, v in p.items()}
    unpack = jax.linear_transpose(pack_fn, shapes)
    return packed, lambda g: unpack(g)[0]


def _ffn_fwd(x, w, *, s, ts):
    h = _rmsnorm(_cols(x, D_MODEL, 0, ts), w["norm_g"], n=D_MODEL, s=s, ts=ts, name="ffn_norm")
    up = _mm(h, w["w_up4"], b_layer=w["layer"], name="ffn_up")
    a = _ffn_act(up, w["conv_w"], w["conv_b"], ts=ts)
    y = _mm(a, w["w_down"], res=x, name="ffn_down")
    return y, dict(x=x, h=h, up=up, a=a)


def _ffn_bwd(dy, w, sv, *, s, ts):
    da = _mm(dy, w["w_down"], tb=True, name="ffn_down_dx")
    g_down = _mm(sv["a"], dy, ta=True, name="ffn_down_dw")
    dgate, dval, g_cw, g_cb = _ffn_act_bwd(sv["up"], da, w["conv_w"], w["conv_b"], ts=ts)
    dup = jnp.concatenate([dgate, dval], axis=1)
    dh = _mm(dup, w["w_up4"], tb=True, b_layer=w["layer"], name="ffn_up_dx")
    g_up = _mm(sv["h"], dup, ta=True, out_chips=True, name="ffn_up_dw")
    dx, g_norm = _rmsnorm_bwd(_cols(sv["x"], D_MODEL, 0, ts), w["norm_g"], dh, dy, n=D_MODEL, s=s, ts=ts,
                              name="ffn_norm_bwd")
    return dx, dict(w_up=g_up, w_down=g_down, conv_w=g_cw, conv_b=g_cb, norm_g=g_norm)


def _even_fwd(x, w, tabs, *, s, ts, side=()):
    cos_m, sin_m, cos_r, sin_r = tabs
    h = _rmsnorm(_cols(x, D_MODEL, 0, ts), w["mix_g"], n=D_MODEL, s=s, ts=ts, name="mix_norm")
    p = _mm(h, w["w_in"], name="even_in")
    cqn = _rmsnorm(_cols(p, EV_CQ, EV_RET // EV_CQ, ts), w["q_norm"], n=MLA_QR, s=s, ts=ts, name="mla_q_norm")
    ckvn = _rmsnorm(_cols(p, MLA_KVR, (EV_RET + EV_CQ) // MLA_KVR, ts), w["kv_norm"], n=MLA_KVR, s=s, ts=ts,
                    name="mla_kv_norm")
    q_pre = _mm(cqn, w["w_uq"], name="mla_uq")
    kv_pre = _mm(ckvn, w["w_ukv"], name="mla_ukv")
    q, k, v = _mla_prep(q_pre, kv_pre, p, cos_m, sin_m, w["qhn"], w["khn"], s=s, ts=ts)
    o, lse, gathered = _flash_fwd(q, k, v, tq=min(s, FLASH_FWD_ROWS), tk=min(s, FLASH_KEYS), side=side)
    o2, r_prev = _ret_fwd(p, cos_r, sin_r, w["theta_l"])
    r = _post_fwd(o2, _cols(p, RET_H * LANES, 3, ts), w["ret_gain"], group=LANES, n=RET_DV, s=s, ts=ts,
                  name="ret_post")
    ar = jnp.concatenate([o, r], axis=1)
    y = _mm(ar, w["w_out"], res=x, name="even_out")
    return y, dict(x=x, h=h, p=p, cqn=cqn, ckvn=ckvn, q_pre=q_pre, kv_pre=kv_pre, q=q, k=k, v=v, o=o, lse=lse,
                   o2=o2, r_prev=r_prev, ar=ar), gathered


def _even_bwd(dy, w, sv, tabs, *, s, ts):
    cos_m, sin_m, cos_r, sin_r = tabs
    p = sv["p"]
    wh = MLA_H * LANES
    dar = _mm(dy, w["w_out"], tb=True, name="even_out_dx")
    g_out = _mm(sv["ar"], dy, ta=True, name="even_out_dw")
    do_attn = _attn_bwd_prep(dar, sv["o"], s=s, ts=ts)
    dq, dk, dv = _flash_bwd(sv["q"], sv["k"], sv["v"], do_attn, sv["lse"], tq=min(s, FLASH_BWD_ROWS),
                            tk=min(s, FLASH_KEYS))
    dq_pre, dk_pre, dkr, g_qhn, g_khn = _mla_prep_bwd(sv["q_pre"], sv["kv_pre"], p, cos_m, sin_m, w["qhn"], w["khn"],
                                                      dq, dk, s=s, ts=ts)
    dkv_pre = jnp.concatenate([dk_pre, dv], axis=1)
    dckvn = _mm(dkv_pre, w["w_ukv"], tb=True, name="mla_ukv_dx")
    g_ukv = _mm(sv["ckvn"], dkv_pre, ta=True, name="mla_ukv_dw")
    dcqn = _mm(dq_pre, w["w_uq"], tb=True, name="mla_uq_dx")
    g_uq = _mm(sv["cqn"], dq_pre, ta=True, name="mla_uq_dw")
    dckv, g_kvn = _rmsnorm_bwd(_cols(p, MLA_KVR, (EV_RET + EV_CQ) // MLA_KVR, ts), w["kv_norm"], dckvn, None,
                               n=MLA_KVR, s=s, ts=ts, name="mla_kv_norm_bwd")
    dcq, g_qn = _rmsnorm_bwd(_cols(p, EV_CQ, EV_RET // EV_CQ, ts), w["q_norm"], dcqn, None, n=MLA_QR, s=s, ts=ts,
                             name="mla_q_norm_bwd")
    do, drg, g_gain = _post_bwd(sv["o2"], _cols(p, wh, 3, ts), w["ret_gain"], _cols(dar, wh, 1, ts),
                                group=LANES, n=RET_DV, s=s, ts=ts, name="ret_post_bwd")
    dq2, dk2, dv2, dth = _ret_bwd(p, cos_r, sin_r, w["theta_l"], w["theta_h"], sv["r_prev"], do)
    drq, drk, drv = (_sum2(a, s=s, ts=ts, name="sum_dirs_1024") for a in (dq2, dk2, dv2))
    dp = jnp.concatenate([drq, drk, drv, drg, _bf(dcq), _bf(dckv), dkr, jnp.zeros((s, LANES), BF16)], axis=1)
    dh = _mm(dp, w["w_in"], tb=True, name="even_in_dx")
    g_in = _mm(sv["h"], dp, ta=True, name="even_in_dw")
    dx, g_mix = _rmsnorm_bwd(_cols(sv["x"], D_MODEL, 0, ts), w["mix_g"], dh, dy, n=D_MODEL, s=s, ts=ts,
                             name="mix_norm_bwd")
    grads = dict(w_in=g_in, w_uq=g_uq, w_ukv=g_ukv, w_out=g_out, mix_g=g_mix, q_norm=g_qn, kv_norm=g_kvn,
                 qhn=g_qhn, khn=g_khn, ret_gain=g_gain)
    return dx, grads, dth[:, :, 0]


def _odd_fwd(x, w, *, s, ts):
    h = _rmsnorm(_cols(x, D_MODEL, 0, ts), w["mix_g"], n=D_MODEL, s=s, ts=ts, name="mix_norm")
    p = _mm(h, w["w_in"], name="odd_in")
    o2, s_prev = _gla_fwd(p, w["wg2"], w["bg2"])
    g = _post_fwd(o2, _cols(p, GLA_H * GLA_DV, 2, ts), w["gla_gain"], group=GLA_DV, n=GLA_DV, s=s, ts=ts,
                  name="gla_post")
    y = _mm(g, w["w_out"], res=x, name="odd_out")
    return y, dict(x=x, h=h, p=p, o2=o2, s_prev=s_prev, g=g)


def _odd_bwd(dy, w, sv, *, s, ts):
    p = sv["p"]
    wv = GLA_H * GLA_DV
    dg = _mm(dy, w["w_out"], tb=True, name="odd_out_dx")
    g_out = _mm(sv["g"], dy, ta=True, name="odd_out_dw")
    do, dgr, g_gain = _post_bwd(sv["o2"], _cols(p, wv, 2, ts), w["gla_gain"], _cols(dg, wv, 0, ts),
                                group=GLA_DV, n=GLA_DV, s=s, ts=ts, name="gla_post_bwd")
    dq2, dk2, dv2, dga2, g_wg, g_bg = _gla_bwd(p, w["wg2"], w["bg2"], sv["s_prev"], do)
    dq = _sum2(dq2, s=s, ts=ts, name="sum_dirs_512")
    dk = _sum2(dk2, s=s, ts=ts, name="sum_dirs_512")
    dv = _sum2(dv2, s=s, ts=ts, name="sum_dirs_1024")
    dga = _sum2(dga2, s=s, ts=ts, name="sum_dirs_128")
    dp = jnp.concatenate([dq, dk, dv, dgr, dga], axis=1)
    dh = _mm(dp, w["w_in"], tb=True, name="odd_in_dx")
    g_in = _mm(sv["h"], dp, ta=True, name="odd_in_dw")
    dx, g_mix = _rmsnorm_bwd(_cols(sv["x"], D_MODEL, 0, ts), w["mix_g"], dh, dy, n=D_MODEL, s=s, ts=ts,
                             name="mix_norm_bwd")
    return dx, dict(w_in=g_in, wg2=g_wg, bg2=g_bg, w_out=g_out, mix_g=g_mix, gla_gain=g_gain)


_EVEN_NAMES = dict(mix_norm="mix_norm_even", w_in="w_in_even", q_norm="mla_q_norm", kv_norm="mla_kv_norm",
                   w_uq="mla_w_uq", w_ukv="mla_w_ukv", q_head_norm="mla_q_head_norm", k_head_norm="mla_k_head_norm",
                   ret_out_norm="ret_out_norm", w_out="w_out_even")
_ODD_NAMES = dict(mix_norm="mix_norm_odd", w_in="w_in_odd", w_gate_fwd="gla_w_gate_fwd", b_gate_fwd="gla_b_gate_fwd",
                  w_gate_bwd="gla_w_gate_bwd", b_gate_bwd="gla_b_gate_bwd", gla_out_norm="gla_out_norm",
                  w_out="w_out_odd")

def _local_step(x, pos, target, full, side=(), finish=None):
    s = x.shape[0]
    ts = min(s, 256)
    tabs = _rope_tables(pos, MLA_ROPE, MLA_NOPE) + _rope_tables(pos, RET_DK, 0)

    def layer_weights(layer):
        i = layer // 2
        names = _EVEN_NAMES if layer % 2 == 0 else _ODD_NAMES
        wm, unpack_m = _packed(_pack_even if layer % 2 == 0 else _pack_odd, {k: full[n][i] for k, n in names.items()})
        if layer % 2 == 0:
            th = jnp.stack([full["ret_theta_fwd"][i], full["ret_theta_bwd"][i]]).astype(F32)
            wm["theta_h"] = jnp.broadcast_to(th[:, :, None], (2, RET_H, LANES))
            wm["theta_l"] = wm["theta_h"].reshape(2, 1, RET_H * LANES)
        w_up4, index = full["ffn_w_up"][layer]
        wf = dict(layer=index, w_up4=w_up4, w_down=_bf(full["ffn_w_down"][layer]),
                  conv_w=full["ffn_conv_w"][layer].astype(F32), conv_b=full["ffn_conv_b"][layer][None, :].astype(F32),
                  norm_g=full["ffn_norm"][layer][None, :].astype(F32))
        return wm, unpack_m, wf

    layers, saved = [], []
    for layer in range(DEPTH):
        layers.append(layer_weights(layer))
        wm, _, wf = layers[-1]
        if layer % 2 == 0:
            x, sv_m, gathered = _even_fwd(x, wm, tabs, s=s, ts=ts, side=side if layer == 0 else ())
            if layer == 0 and finish is not None:
                full = finish(gathered)
        else:
            x, sv_m = _odd_fwd(x, wm, s=s, ts=ts)
        x, sv_f = _ffn_fwd(x, wf, s=s, ts=ts)
        saved.append((sv_m, sv_f))

    dy, sq = _loss_head(x, target, s=s, ts=ts)
    loss = 0.5 / D_MODEL * jnp.sum(sq)

    grads = {}

    def put(name, idx, g):
        grads.setdefault(name, {})[idx] = g

    for layer in reversed(range(DEPTH)):
        wm, unpack_m, wf = layers[layer]
        sv_m, sv_f = saved[layer]
        i = layer // 2
        dy, gf = _ffn_bwd(dy, wf, sv_f, s=s, ts=ts)
        put("ffn_w_up", layer, gf["w_up"])
        put("ffn_w_down", layer, gf["w_down"])
        put("ffn_conv_w", layer, gf["conv_w"])
        put("ffn_conv_b", layer, gf["conv_b"][0])
        put("ffn_norm", layer, gf["norm_g"][0])
        if layer % 2 == 0:
            dy, gm, dth = _even_bwd(dy, wm, sv_m, tabs, s=s, ts=ts)
            put("ret_theta_fwd", i, dth[0])
            put("ret_theta_bwd", i, dth[1])
            names = _EVEN_NAMES
        else:
            dy, gm = _odd_bwd(dy, wm, sv_m, s=s, ts=ts)
            names = _ODD_NAMES
        for k, g in unpack_m(gm).items():
            put(names[k], i, g)
    return loss, dy, {n: [g[j] for j in range(len(g))] for n, g in grads.items()}


HBM_SPEC = pl.BlockSpec(memory_space=pltpu.HBM)
VMEM_SPEC = pl.BlockSpec(memory_space=pltpu.VMEM)
CHIPS = 4
CORES = 2
ROW = 8 * LANES


def _xyc():
    return lax.axis_index("x"), lax.axis_index("y"), lax.axis_index("c")


def _other_chips(x, y):
    return [(1 - x, y), (x, 1 - y), (1 - x, 1 - y)]


def _remote(src, dst, send, recv, dev):
    return pltpu.make_async_remote_copy(src_ref=src, dst_ref=dst, send_sem=send, recv_sem=recv,
                                        device_id=dev, device_id_type=MESH)


def _sems(n):
    return pltpu.SemaphoreType.DMA((n,))


def _gather_copies(side, srcs, lands, send, recv, loc):
    n = len(side)
    x, y, c = _xyc()
    me = 2 * x + y
    local, sends, arrivals = [], [], []
    for t, (_, first, count) in enumerate(side):
        src = srcs[t].at[pl.ds(first, count)]
        local.append(pltpu.make_async_copy(src, lands[t].at[me], loc.at[t]))
        for j, (px, py) in enumerate(_other_chips(x, y)):
            k = n * j + t
            sends.append(_remote(src, lands[t].at[me], send.at[k], recv.at[k], (px, py, c)))
            arrivals.append(_remote(src, lands[t].at[2 * px + py], send.at[k], recv.at[k], (px, py, c)))
    return local, sends, arrivals


def _gather_shapes(side):
    return [jax.ShapeDtypeStruct((CHIPS, count) + a.shape[1:], a.dtype) for a, _, count in side]


def _gather_chips(side):
    n = len(side)

    def body(*refs):
        local, sends, arrivals = _gather_copies(side, refs[:n], refs[n:2 * n], *refs[2 * n:])
        for cp in local + sends:
            cp.start()
        for cp in arrivals:
            cp.wait_recv()
        for cp in sends:
            cp.wait_send()
        for cp in local:
            cp.wait()

    return pl.pallas_call(
        body, name="gather_chips", in_specs=[HBM_SPEC] * n, out_specs=[HBM_SPEC] * n,
        out_shape=_gather_shapes(side),
        scratch_shapes=[_sems(3 * n), _sems(3 * n), _sems(n)],
    )(*[a for a, _, _ in side])


def _half_rows(ref, axis, half, which):
    idx = (slice(None),) * axis + (pl.ds(pl.multiple_of(which * half, 8), half),)
    return ref.at[idx]


def _swap_halves(arrs):
    n = len(arrs)

    def body(*refs):
        ins, outs = refs[:n], refs[n:2 * n]
        send, recv = refs[2 * n:]
        x, y, c = _xyc()
        copies = []
        for t in range(n):
            half = arrs[t].shape[2] // CORES
            cp = _remote(_half_rows(ins[t], 2, half, 1 - c), outs[t], send.at[t], recv.at[t], (x, y, 1 - c))
            cp.start()
            copies.append(cp)
        for cp in copies:
            cp.wait()

    return pl.pallas_call(
        body, name="swap_halves", in_specs=[HBM_SPEC] * n, out_specs=[HBM_SPEC] * n,
        out_shape=[jax.ShapeDtypeStruct(a.shape[:2] + (a.shape[2] // CORES, a.shape[3]), a.dtype) for a in arrs],
        scratch_shapes=[_sems(n), _sems(n)],
    )(*arrs)


def _add_core_halves(a, got, core, *, ts, name):
    ch, nl, r, cols = a.shape
    half = r // CORES
    nb = half // ts

    def body(core_ref, a_ref, g_ref, o_ref):
        o_ref[...] = (a_ref[...] + g_ref[...]).astype(o_ref.dtype)

    rows = pl.BlockSpec((ts, cols), lambda g, i, cr: (g * nb + i, 0))
    return pl.pallas_call(
        body, name=name, out_shape=jax.ShapeDtypeStruct((ch * nl * half, cols), BF16),
        grid_spec=pltpu.PrefetchScalarGridSpec(
            num_scalar_prefetch=1, grid=(ch * nl, nb),
            in_specs=[pl.BlockSpec((ts, cols), lambda g, i, cr: (g * (r // ts) + cr[0] * nb + i, 0)), rows],
            out_specs=rows),
        compiler_params=_cp(("arbitrary", "arbitrary")),
    )(core, a.reshape(-1, cols), got.reshape(-1, cols)).reshape(got.shape)


def _add_chip_parts(parts, core, *, ts, name):
    ch, nl, half, cols = parts.shape
    nb = half // ts
    r = half * CORES

    def body(core_ref, *refs):
        acc = refs[0][...].astype(F32)
        for p in refs[1:ch]:
            acc = acc + p[...].astype(F32)
        refs[ch][...] = acc

    return pl.pallas_call(
        body, name=name, out_shape=jax.ShapeDtypeStruct((nl * r, cols), F32),
        grid_spec=pltpu.PrefetchScalarGridSpec(
            num_scalar_prefetch=1, grid=(nl, nb),
            in_specs=[pl.BlockSpec((ts, cols), lambda l, i, cr, j=j: ((j * nl + l) * nb + i, 0)) for j in range(ch)],
            out_specs=pl.BlockSpec((ts, cols), lambda l, i, cr: (l * (r // ts) + cr[0] * nb + i, 0))),
        compiler_params=_cp(("arbitrary", "arbitrary")),
    )(core, *[parts.reshape(-1, cols)] * ch).reshape(nl, r, cols)


def _scatter_chips(arrs):
    n = len(arrs)

    def body(*refs):
        ins, outs = refs[:n], refs[n:2 * n]
        send, recv, loc = refs[2 * n:]
        x, y, c = _xyc()
        me = 2 * x + y
        copies = []
        for t in range(n):
            cp = pltpu.make_async_copy(ins[t].at[me], outs[t].at[me], loc.at[t])
            cp.start()
            copies.append(cp)
        sends = []
        for j, (px, py) in enumerate(_other_chips(x, y)):
            for t in range(n):
                cp = _remote(ins[t].at[2 * px + py], outs[t].at[me], send.at[n * j + t], recv.at[n * j + t], (px, py, c))
                cp.start()
                sends.append(cp)
        for j, (px, py) in enumerate(_other_chips(x, y)):
            for t in range(n):
                _remote(ins[t].at[me], outs[t].at[2 * px + py], send.at[n * j + t], recv.at[n * j + t],
                        (px, py, c)).wait_recv()
        for cp in sends:
            cp.wait_send()
        for cp in copies:
            cp.wait()

    return pl.pallas_call(
        body, name="scatter_chips", in_specs=[HBM_SPEC] * n, out_specs=[HBM_SPEC] * n,
        out_shape=[jax.ShapeDtypeStruct(a.shape, a.dtype) for a in arrs],
        scratch_shapes=[_sems(3 * n), _sems(3 * n), _sems(n)],
    )(*arrs)


def _gather_cores(arrs):
    n = len(arrs)

    def body(*refs):
        ins, outs = refs[:n], refs[n:2 * n]
        send, recv = refs[2 * n:]
        x, y, c = _xyc()
        sends = []
        for t in range(n):
            half = arrs[t].shape[1] // CORES
            cp = _remote(_half_rows(ins[t], 1, half, c), _half_rows(outs[t], 1, half, c), send.at[t], recv.at[t],
                         (x, y, 1 - c))
            cp.start()
            sends.append(cp)
        for t in range(n):
            half = arrs[t].shape[1] // CORES
            _remote(_half_rows(ins[t], 1, half, 1 - c), _half_rows(outs[t], 1, half, 1 - c), send.at[t], recv.at[t],
                    (x, y, 1 - c)).wait_recv()
        for cp in sends:
            cp.wait_send()

    return pl.pallas_call(
        body, name="gather_cores", in_specs=[HBM_SPEC] * n, out_specs=[HBM_SPEC] * n,
        out_shape=[jax.ShapeDtypeStruct(a.shape, a.dtype) for a in arrs],
        input_output_aliases={t: t for t in range(n)},
        scratch_shapes=[_sems(n), _sems(n)],
    )(*arrs)


def _all_reduce_devices(v):
    n_dev = CHIPS * CORES

    def body(v_ref, o_ref, buf, send, recv):
        x, y, c = _xyc()
        me = 4 * x + 2 * y + c
        buf[pl.ds(me, 1)] = v_ref[...][None]
        sends = []
        for m in range(1, n_dev):
            px = 1 - x if m & 4 else x
            py = 1 - y if m & 2 else y
            pc = 1 - c if m & 1 else c
            cp = _remote(v_ref, buf.at[me], send.at[m - 1], recv.at[m - 1], (px, py, pc))
            cp.start()
            sends.append((cp, 4 * px + 2 * py + pc))
        for m, (cp, peer) in enumerate(sends):
            _remote(v_ref, buf.at[peer], send.at[m], recv.at[m], (x, y, c)).wait_recv()
        for cp, _ in sends:
            cp.wait_send()
        acc = buf[0]
        for k in range(1, n_dev):
            acc = acc + buf[k]
        o_ref[...] = acc

    return pl.pallas_call(
        body, name="all_reduce_devices", in_specs=[VMEM_SPEC], out_specs=VMEM_SPEC,
        out_shape=jax.ShapeDtypeStruct(v.shape, F32),
        scratch_shapes=[pltpu.VMEM((n_dev,) + v.shape, F32), pltpu.SemaphoreType.DMA((n_dev - 1,)),
                        pltpu.SemaphoreType.DMA((n_dev - 1,))],
    )(v)


_SHARDED = (("w_in_even", 2), ("mla_w_uq", 2), ("mla_w_ukv", 2), ("w_out_even", 1), ("w_in_odd", 2), ("w_out_odd", 1),
            ("ffn_w_up", 2), ("ffn_w_down", 1),
            ("mix_norm_odd", 1), ("gla_w_gate_fwd", 2), ("gla_b_gate_fwd", 1), ("gla_w_gate_bwd", 2),
            ("gla_b_gate_bwd", 1), ("gla_out_norm", 2), ("ffn_conv_w", 2))
_N_MATRICES = 8
_REPLICATED = ("mix_norm_even", "mla_q_norm", "mla_kv_norm", "mla_q_head_norm", "mla_k_head_norm", "ret_theta_fwd",
               "ret_theta_bwd", "ret_out_norm", "ffn_norm", "ffn_conv_b")
_WEIGHTS = ("mix_norm_even", "w_in_even", "mla_q_norm", "mla_kv_norm", "mla_w_uq", "mla_w_ukv", "mla_q_head_norm",
            "mla_k_head_norm", "ret_theta_fwd", "ret_theta_bwd", "ret_out_norm", "w_out_even", "mix_norm_odd",
            "w_in_odd", "gla_w_gate_fwd", "gla_b_gate_fwd", "gla_w_gate_bwd", "gla_b_gate_bwd", "gla_out_norm",
            "w_out_odd", "ffn_norm", "ffn_w_up", "ffn_conv_w", "ffn_conv_b", "ffn_w_down")


def _flatten(arrs, row_multiple, dtype):
    flat = jnp.concatenate([a.reshape(-1).astype(dtype) for a in arrs])
    per = ROW * row_multiple
    total = -(-flat.shape[0] // per) * per
    return jnp.pad(flat, (0, total - flat.shape[0])).reshape(-1, ROW)


def _unflatten(flat, shapes):
    flat = flat.reshape(-1)
    out, o = [], 0
    for shp in shapes:
        n = math.prod(shp)
        out.append(flat[o:o + n].reshape(shp))
        o += n
    return out


def kernel(x, positions, mix_norm_even, w_in_even, mla_q_norm, mla_kv_norm, mla_w_uq, mla_w_ukv, mla_q_head_norm, mla_k_head_norm, ret_theta_fwd, ret_theta_bwd, ret_out_norm, w_out_even, mix_norm_odd, w_in_odd, gla_w_gate_fwd, gla_b_gate_fwd, gla_w_gate_bwd, gla_b_gate_bwd, gla_out_norm, w_out_odd, ffn_norm, ffn_w_up, ffn_conv_w, ffn_conv_b, ffn_w_down, loss_target, m_mix_norm_even, m_w_in_even, m_mla_q_norm, m_mla_kv_norm, m_mla_w_uq, m_mla_w_ukv, m_mla_q_head_norm, m_mla_k_head_norm, m_ret_theta_fwd, m_ret_theta_bwd, m_ret_out_norm, m_w_out_even, m_mix_norm_odd, m_w_in_odd, m_gla_w_gate_fwd, m_gla_b_gate_fwd, m_gla_w_gate_bwd, m_gla_b_gate_bwd, m_gla_out_norm, m_w_out_odd, m_ffn_norm, m_ffn_w_up, m_ffn_conv_w, m_ffn_conv_b, m_ffn_w_down, v_mix_norm_even, v_w_in_even, v_mla_q_norm, v_mla_kv_norm, v_mla_w_uq, v_mla_w_ukv, v_mla_q_head_norm, v_mla_k_head_norm, v_ret_theta_fwd, v_ret_theta_bwd, v_ret_out_norm, v_w_out_even, v_mix_norm_odd, v_w_in_odd, v_gla_w_gate_fwd, v_gla_b_gate_fwd, v_gla_w_gate_bwd, v_gla_b_gate_bwd, v_gla_out_norm, v_w_out_odd, v_ffn_norm, v_ffn_w_up, v_ffn_conv_w, v_ffn_conv_b, v_ffn_w_down):
    args = dict(locals())
    x2, pos, target = args["x"][0], args["positions"][0], args["loss_target"][0]
    axis = dict(_SHARDED)
    mats = [n for n, _ in _SHARDED[:_N_MATRICES]]
    smalls = [n for n, _ in _SHARDED[_N_MATRICES:]]
    small_shapes = [args[n].shape for n in smalls]

    local = {n: _bf(args[n]) for n in mats}
    first_layers = {n: (0, 0 if n.endswith("_odd") else 1) for n in mats}
    now = [(local[n],) + first_layers[n] for n in mats if first_layers[n][1]]
    later = [(local[n], first_layers[n][1], args[n].shape[0] - first_layers[n][1]) for n in mats]
    small_block = _flatten([args[n] for n in smalls], 2 * HALO, F32)
    got_now = _gather_chips(now + [(small_block, 0, small_block.shape[0])])
    per_chip = [_unflatten(got_now[-1][j], small_shapes) for j in range(CHIPS)]
    base = {n: args[n] for n in _REPLICATED}
    for k, n in enumerate(smalls):
        base[n] = jnp.concatenate([per_chip[j][k] for j in range(CHIPS)], axis=axis[n])

    def whole(stacks):
        full = dict(base)
        for n, per_layer in stacks.items():
            if n == "ffn_w_up":
                full[n] = per_layer
            else:
                full[n] = [None if st is None else jnp.concatenate([st[j, l] for j in range(CHIPS)], axis=axis[n] - 1)
                           for st, l in per_layer]
        return full

    stacks = {n: [(None, 0)] * args[n].shape[0] for n in mats}
    for (a, first, count), st in zip(now, got_now):
        n = next(m for m in mats if local[m] is a)
        stacks[n] = [(st, l) for l in range(count)] + stacks[n][count:]

    def finish(got_later):
        for (a, first, count), st in zip(later, got_later):
            n = next(m for m in mats if local[m] is a)
            stacks[n] = stacks[n][:first] + [(st, l) for l in range(count)]
        return whole(stacks)

    loss, grad_x, grads = _local_step(x2, pos, target, whole(stacks), side=later, finish=finish)
    loss = lax.psum(loss, ("x", "y", "c"))

    def by_chip(n, g):
        if n == "ffn_w_up":
            return g
        if axis[n] == 1:
            return g.reshape((CHIPS, g.shape[0] // CHIPS) + g.shape[1:])
        return jnp.stack(jnp.split(g, CHIPS, axis=axis[n] - 1))

    core = lax.axis_index("c").astype(jnp.int32).reshape(1)
    stacked = [jnp.stack([by_chip(n, g) for g in grads[n]], axis=1) for n in mats]
    small_parts = [jnp.split(jnp.stack(grads[n]), CHIPS, axis=axis[n]) for n in smalls]
    stacked.append(jnp.stack([_flatten([p[j] for p in small_parts], 2 * HALO, F32) for j in range(CHIPS)])[:, None])
    names = mats + ["small"]
    tiles = [_rows_tile(a.shape[2] // CORES, a.shape[3]) for a in stacked]
    got = _swap_halves(stacked)
    chip_sums = [_add_core_halves(a, b, core, ts=ts, name="add_core_halves_" + n)
                 for n, a, b, ts in zip(names, stacked, got, tiles)]
    parts = _scatter_chips(chip_sums)
    sums = [_add_chip_parts(p, core, ts=ts, name="add_chip_parts_" + n) for n, p, ts in zip(names, parts, tiles)]
    reduced = _gather_cores(sums)

    res = {}

    def update(n, w, g, m, v, ts):
        cols = g.shape[-1]
        outs = _adamw(w.reshape(-1, cols), g.reshape(-1, cols), m.reshape(-1, cols), v.reshape(-1, cols), ts=ts,
                      name="adamw_" + n)
        return [g] + [o.reshape(g.shape) for o in outs]

    kinds = ("grad", "delta", "new_m", "new_v")
    for n, g, ts in zip(mats, reduced, tiles):
        for kind, a in zip(kinds, update(n, args[n], g, args["m_" + n], args["v_" + n], ts)):
            res[kind + "_" + n] = a
    w_s, m_s, v_s = (_flatten([args[pre + n] for n in smalls], 2 * HALO, F32) for pre in ("", "m_", "v_"))
    for kind, flat in zip(kinds, update("small", w_s, reduced[-1][0], m_s, v_s, tiles[-1])):
        for n, a in zip(smalls, _unflatten(flat, small_shapes)):
            res[kind + "_" + n] = a

    rep_shapes = [args[n].shape for n in _REPLICATED]
    g_rep = _all_reduce_devices(_flatten([jnp.stack(grads[n]) for n in _REPLICATED], HALO, F32))
    w_rep, m_rep, v_rep = (_flatten([args[pre + n] for n in _REPLICATED], HALO, F32) for pre in ("", "m_", "v_"))
    for kind, flat in zip(kinds, update("replicated", w_rep, g_rep, m_rep, v_rep, g_rep.shape[0])):
        for n, a in zip(_REPLICATED, _unflatten(flat, rep_shapes)):
            res[kind + "_" + n] = a

    outs = [loss, grad_x[None]]
    for kind in ("grad", "delta", "new_m", "new_v"):
        outs += [res[kind + "_" + n] for n in _WEIGHTS]
    return tuple(outs)
```

```python
import math

import jax
import jax.numpy as jnp
from jax import lax
from jax.experimental import pallas as pl
from jax.experimental.pallas import tpu as pltpu

F32 = jnp.float32
BF16 = jnp.bfloat16
MESH = pl.DeviceIdType.MESH

EPS = 1e-6
D_MODEL = 1024
DEPTH = 4
LANES = 128
MLA_H, MLA_QR, MLA_KVR, MLA_NOPE, MLA_ROPE, MLA_V = 8, 384, 256, 64, 32, 64
MLA_QK = MLA_NOPE + MLA_ROPE
MLA_SCALE = MLA_QK ** -0.5
RET_H, RET_DK, RET_DV, RET_C = 8, 64, 64, 128
GLA_H, GLA_DK, GLA_DV, GLA_R, GLA_TAU, GLA_C = 4, 128, 256, 16, 16.0, 64
D_FF = 2816
ROPE_THETA = 10000.0
LN2 = math.log(2.0)
ADAM_LR, ADAM_B1, ADAM_B2, ADAM_EPS, ADAM_WD, ADAM_STEP = 0.001, 0.9, 0.999, 1e-08, 0.01, 10

EV_RET = 4 * RET_H * LANES
EV_CQ = 512
EV_W = 5120
EV_KR_BLK = (EV_RET + EV_CQ + MLA_KVR) // LANES
OD_W = 3200
OD_GA_BLK = 3072 // LANES

VMEM_LIMIT = 56 * 1024 * 1024
MM_TILE_CAP = 1408
V_ONES = (MLA_V, MLA_V + 1)
FLASH_FWD_ROWS = 1024
FLASH_BWD_ROWS = 1024
FLASH_KEYS = 1024


def _cp(sem):
    return pltpu.CompilerParams(dimension_semantics=sem, vmem_limit_bytes=VMEM_LIMIT)


def _dot(a, b):
    return jnp.dot(a, b, preferred_element_type=F32)


def _dot_nt(a, b):
    return lax.dot_general(a, b, (((1,), (1,)), ((), ())), preferred_element_type=F32)


def _dot_tn(a, b):
    return lax.dot_general(a, b, (((0,), (0,)), ((), ())), preferred_element_type=F32)


def _bf(x):
    return x.astype(BF16)


def _split3(x):
    h1 = _bf(x)
    r1 = x - h1.astype(F32)
    h2 = _bf(r1)
    h3 = _bf(r1 - h2.astype(F32))
    return h1, h2, h3


def _tile(n, cap):
    if n <= cap:
        return n
    best = None
    for t in range(LANES, cap + 1, LANES):
        if n % t == 0:
            best = t
    assert best is not None, n
    return best


def _mm(a, b, *, ta=False, tb=False, res=None, out_dtype=F32, b_layer=None, out_chips=False, halves=None, name):
    assert not (ta and tb)
    if halves == "a":
        assert not ta
        m, kdim = a.shape[1], 2 * a.shape[2]
    elif ta:
        kdim, m = a.shape
    else:
        m, kdim = a.shape
    if b_layer is not None:
        rows_b, cols_b = b.shape[2], b.shape[0] * b.shape[3]
    elif halves == "b":
        assert not tb
        rows_b, cols_b = b.shape[1], 2 * b.shape[2]
    else:
        rows_b, cols_b = b.shape
    n, kb = (rows_b, cols_b) if tb else (cols_b, rows_b)
    assert kb == kdim, (a.shape, b.shape, ta, tb)
    tm, tn, tk = _tile(m, MM_TILE_CAP), _tile(n, MM_TILE_CAP), _tile(kdim, MM_TILE_CAP)
    nk = kdim // tk
    has_res = res is not None
    vmem = (2 * tm * tk * a.dtype.itemsize + 2 * tk * tn * b.dtype.itemsize
            + 2 * tm * tn * jnp.dtype(out_dtype).itemsize + (2 * tm * tn * 4 if has_res else 0)
            + (tm * tn * 4 if nk > 1 else 0))
    assert vmem <= VMEM_LIMIT - 8 * 1024 * 1024, (name, vmem)
    a_spec = (pl.BlockSpec((tk, tm), lambda i, j, k: (k, i)) if ta
              else pl.BlockSpec((tm, tk), lambda i, j, k: (i, k)))
    if halves == "a":
        per_half = a.shape[2] // tk
        a_spec = pl.BlockSpec((None, tm, tk), lambda i, j, k: (k // per_half, i, k % per_half))
    if halves == "b":
        per_half = b.shape[2] // tn
        b_spec = pl.BlockSpec((None, tk, tn), lambda i, j, k: (j // per_half, k, j % per_half))
    elif b_layer is not None:
        per_chip = b.shape[3]
        if tb:
            assert tk == per_chip
            b_spec = pl.BlockSpec((None, None, tn, tk), lambda i, j, k: (k, b_layer, j, 0))
        else:
            assert tn == per_chip
            b_spec = pl.BlockSpec((None, None, tk, tn), lambda i, j, k: (j, b_layer, k, 0))
    else:
        b_spec = (pl.BlockSpec((tn, tk), lambda i, j, k: (j, k)) if tb
                  else pl.BlockSpec((tk, tn), lambda i, j, k: (k, j)))
    if out_chips:
        assert n // tn == CHIPS and not has_res
        o_spec = pl.BlockSpec((None, tm, tn), lambda i, j, k: (j, i, 0))
        out_struct = jax.ShapeDtypeStruct((CHIPS, m, tn), out_dtype)
    else:
        o_spec = pl.BlockSpec((tm, tn), lambda i, j, k: (i, j))
        out_struct = jax.ShapeDtypeStruct((m, n), out_dtype)

    def product(a_ref, b_ref):
        av, bv = _bf(a_ref[...]), _bf(b_ref[...])
        if ta:
            return _dot_tn(av, bv)
        if tb:
            return _dot_nt(av, bv)
        return _dot(av, bv)

    def body(*refs):
        a_ref, b_ref = refs[:2]
        r_ref = refs[2] if has_res else None
        o_ref = refs[3] if has_res else refs[2]

        def finish(r):
            if has_res:
                r = r + r_ref[...]
            o_ref[...] = r.astype(o_ref.dtype)

        if nk == 1:
            finish(product(a_ref, b_ref))
            return
        acc = refs[-1]
        k = pl.program_id(2)

        @pl.when(k == 0)
        def _():
            acc[...] = product(a_ref, b_ref)

        @pl.when(k > 0)
        def _():
            acc[...] += product(a_ref, b_ref)

        @pl.when(k == nk - 1)
        def _():
            finish(acc[...])

    ins = [a, b] + ([res] if has_res else [])
    in_specs = [a_spec, b_spec] + ([o_spec] if has_res else [])
    return pl.pallas_call(
        body, name=name, grid=(m // tm, n // tn, nk),
        in_specs=in_specs, out_specs=o_spec, out_shape=out_struct,
        scratch_shapes=[pltpu.VMEM((tm, tn), F32)] if nk > 1 else [],
        compiler_params=_cp(("parallel", "parallel", "arbitrary")),
    )(*ins)


def _ew(fn, rows, pars, outs, accs=(), *, s, ts, name):
    n_in = len(rows) + len(pars)
    n_o = len(outs)

    def body(*refs):
        i = pl.program_id(0)
        vals = fn(*[r[...] for r in refs[:n_in]])
        if not isinstance(vals, (tuple, list)):
            vals = (vals,)
        assert len(vals) == n_o + len(accs), (name, len(vals))
        for r, v in zip(refs[n_in:n_in + n_o], vals[:n_o]):
            r[...] = v.astype(r.dtype)
        for r, v in zip(refs[n_in + n_o:], vals[n_o:]):
            @pl.when(i == 0)
            def _(r=r, v=v):
                r[...] = v

            @pl.when(i > 0)
            def _(r=r, v=v):
                r[...] += v

    in_specs = [sp for _, sp in rows]
    in_specs += [pl.BlockSpec(p.shape, lambda i, nd=p.ndim: (0,) * nd) for p in pars]
    out_specs = [pl.BlockSpec((ts, w), lambda i: (i, 0)) for w, _ in outs]
    out_specs += [pl.BlockSpec((r, w), lambda i: (0, 0)) for r, w in accs]
    out_shape = [jax.ShapeDtypeStruct((s, w), dt) for w, dt in outs]
    out_shape += [jax.ShapeDtypeStruct((r, w), F32) for r, w in accs]
    return pl.pallas_call(
        body, name=name, grid=(s // ts,), in_specs=in_specs, out_specs=out_specs, out_shape=out_shape,
        compiler_params=_cp(("arbitrary",)),
    )(*[a for a, _ in rows], *pars)


def _cols(arr, width, blk, ts):
    return (arr, pl.BlockSpec((ts, width), lambda i, b=blk: (i, b)))


def _lead(pair, d, ts):
    return _cols(pair[d], pair[d].shape[1], 0, ts)


def _rowsum(x):
    return jnp.sum(x, axis=0, keepdims=True)


def _lanesum(x):
    return jnp.sum(x, axis=-1, keepdims=True)


def _gsum(x, group):
    w = x.shape[-1]
    if group == w:
        return jnp.broadcast_to(_lanesum(x), x.shape)
    parts = [jnp.broadcast_to(_lanesum(x[:, g:g + group]), (x.shape[0], group)) for g in range(0, w, group)]
    return jnp.concatenate(parts, axis=-1)


def _gn(x, gain, group, n):
    rstd = lax.rsqrt(_gsum(x * x, group) * (1.0 / n) + EPS)
    xn = x * rstd
    return xn * gain, xn, rstd


def _gn_bwd(dy, xn, rstd, gain, group, n):
    dxn = dy * gain
    dx = rstd * (dxn - xn * (_gsum(dxn * xn, group) * (1.0 / n)))
    return dx, _rowsum(dy * xn)


def _sigmoid(x):
    return 1.0 / (1.0 + jnp.exp(-x))


def _rmsnorm(x_row, g, *, n, s, ts, name):
    w = g.shape[-1]

    def fn(x, gv):
        return _gn(x, gv, w, n)[0]

    return _ew(fn, [x_row], [g], [(w, BF16)], s=s, ts=ts, name=name)[0]


def _rmsnorm_bwd(x_row, g, dh, dres, *, n, s, ts, name):
    w = g.shape[-1]
    has_res = dres is not None

    def fn(x, dhv, *rest):
        gv = rest[-1]
        _, xn, rstd = _gn(x, gv, w, n)
        dx, dg = _gn_bwd(dhv, xn, rstd, gv, w, n)
        if has_res:
            dx = dx + rest[0]
        return dx, dg

    rows = [x_row, _cols(dh, w, 0, ts)] + ([_cols(dres, w, 0, ts)] if has_res else [])
    return _ew(fn, rows, [g], [(w, F32)], [(1, w)], s=s, ts=ts, name=name)


def _rope_tables(pos, real, offset):
    half = real // 2
    inv = ROPE_THETA ** (-jnp.arange(half, dtype=F32) / half)
    ang = pos.astype(F32)[:, None] * inv
    c, sn = jnp.cos(ang), jnp.sin(ang)
    s = pos.shape[0]
    cos_t = jnp.concatenate([jnp.ones((s, offset), F32), c, c,
                             jnp.ones((s, LANES - offset - real), F32)], axis=1)
    sin_t = jnp.concatenate([jnp.zeros((s, offset), F32), -sn, sn,
                             jnp.zeros((s, LANES - offset - real), F32)], axis=1)
    return cos_t, sin_t


def _rope(x, cos_t, sin_t, real, offset):
    half = real // 2
    lane = lax.broadcasted_iota(jnp.int32, x.shape, 1)
    partner = jnp.where(lane < offset + half, pltpu.roll(x, LANES - half, 1), pltpu.roll(x, half, 1))
    return x * cos_t + partner * sin_t


def _mla_prep(q_pre, kv_pre, p_even, cos_m, sin_m, qhn, khn, *, s, ts):
    w = MLA_H * LANES

    def fn(qp, kp, vp, kr, c, sn, gq, gk):
        qs, ks = [], []
        for h in range(MLA_H):
            sl = slice(h * LANES, (h + 1) * LANES)
            qn = _gn(qp[:, sl], gq, LANES, MLA_QK)[0]
            kn = _gn(kp[:, sl] + kr, gk, LANES, MLA_QK)[0]
            qs.append(_rope(qn, c, sn, MLA_ROPE, MLA_NOPE) * MLA_SCALE)
            ks.append(_rope(kn, c, sn, MLA_ROPE, MLA_NOPE))
        lane = lax.broadcasted_iota(jnp.int32, vp.shape, 1) % LANES
        ones = (lane == V_ONES[0]) | (lane == V_ONES[1])
        return jnp.concatenate(qs, axis=1), jnp.concatenate(ks, axis=1), jnp.where(ones, 1.0, vp)

    rows = [_cols(q_pre, w, 0, ts), _cols(kv_pre, w, 0, ts), _cols(kv_pre, w, 1, ts),
            _cols(p_even, LANES, EV_KR_BLK, ts), _cols(cos_m, LANES, 0, ts), _cols(sin_m, LANES, 0, ts)]
    return _ew(fn, rows, [qhn, khn], [(w, BF16)] * 3, s=s, ts=ts, name="mla_prep")


def _mla_prep_bwd(q_pre, kv_pre, p_even, cos_m, sin_m, qhn, khn, dq, dk, *, s, ts):
    w = MLA_H * LANES

    def fn(qp, kp, kr, c, sn, dqv, dkv, gq, gk):
        dqs, dks = [], []
        dkr = jnp.zeros_like(kr)
        dgq = jnp.zeros((1, LANES), F32)
        dgk = jnp.zeros((1, LANES), F32)
        for h in range(MLA_H):
            sl = slice(h * LANES, (h + 1) * LANES)
            _, qn, qr = _gn(qp[:, sl], gq, LANES, MLA_QK)
            _, kn, krs = _gn(kp[:, sl] + kr, gk, LANES, MLA_QK)
            dqn = _rope(dqv[:, sl] * MLA_SCALE, c, -sn, MLA_ROPE, MLA_NOPE)
            dkn = _rope(dkv[:, sl], c, -sn, MLA_ROPE, MLA_NOPE)
            dqh, g1 = _gn_bwd(dqn, qn, qr, gq, LANES, MLA_QK)
            dkh, g2 = _gn_bwd(dkn, kn, krs, gk, LANES, MLA_QK)
            dqs.append(dqh)
            dks.append(dkh)
            dkr = dkr + dkh
            dgq = dgq + g1
            dgk = dgk + g2
        return jnp.concatenate(dqs, axis=1), jnp.concatenate(dks, axis=1), dkr, dgq, dgk

    rows = [_cols(q_pre, w, 0, ts), _cols(kv_pre, w, 0, ts), _cols(p_even, LANES, EV_KR_BLK, ts),
            _cols(cos_m, LANES, 0, ts), _cols(sin_m, LANES, 0, ts), _cols(dq, w, 0, ts), _cols(dk, w, 0, ts)]
    return _ew(fn, rows, [qhn, khn], [(w, BF16), (w, BF16), (LANES, BF16)], [(1, LANES), (1, LANES)],
               s=s, ts=ts, name="mla_prep_bwd")


def _flash_fwd(q, k, v, *, tq, tk, side=()):
    s = q.shape[0]
    nq, nk = s // tq, s // tk
    rq = tq
    ns = len(side)

    def body(*refs):
        q_ref, k_ref, v_ref = refs[:3]
        o_ref, lse_ref = refs[3 + ns:5 + ns]
        m_s, acc = refs[5 + 2 * ns:7 + 2 * ns]
        h, i, j = pl.program_id(0), pl.program_id(1), pl.program_id(2)
        if ns:
            local, sends, arrivals = _gather_copies(side, refs[3:3 + ns], refs[5 + ns:5 + 2 * ns], *refs[7 + 2 * ns:])

            @pl.when((h == 0) & (i == 0) & (j == 0))
            def _():
                for cp in local + sends:
                    cp.start()

        @pl.when(j == 0)
        def _():
            m_s[...] = jnp.full_like(m_s, -jnp.inf)
            acc[...] = jnp.zeros_like(acc)

        kv, vv = k_ref[...], v_ref[...]
        for r in range(0, tq, rq):
            rows = slice(r, r + rq)
            sc = _dot_nt(q_ref[rows, :], kv)
            m_prev = m_s[rows, :]
            m_new = jnp.maximum(m_prev, jnp.max(sc, axis=-1, keepdims=True))
            p = jnp.exp(sc - jnp.tile(m_new, (1, tk // LANES)))
            acc[rows, :] = jnp.exp(m_prev - m_new) * acc[rows, :] + _dot(_bf(p), vv)
            m_s[rows, :] = m_new

        @pl.when(j == nk - 1)
        def _():
            a = acc[...]
            l = a[:, V_ONES[0]:V_ONES[0] + 1]
            o_ref[...] = (a / l).astype(o_ref.dtype)
            lse_ref[...] = (m_s[...] + jnp.log(jnp.broadcast_to(l, (tq, LANES)))).T[0:1, :]

        if ns:
            @pl.when((h == MLA_H - 1) & (i == nq - 1) & (j == nk - 1))
            def _():
                for cp in arrivals:
                    cp.wait_recv()
                for cp in sends:
                    cp.wait_send()
                for cp in local:
                    cp.wait()

    qs = pl.BlockSpec((tq, LANES), lambda h, i, j: (i, h))
    ks = pl.BlockSpec((tk, LANES), lambda h, i, j: (j, h))
    outs = pl.pallas_call(
        body, name="mla_flash_fwd_gather" if ns else "mla_flash_fwd", grid=(MLA_H, nq, nk),
        in_specs=[qs, ks, ks] + [HBM_SPEC] * ns,
        out_specs=[qs, pl.BlockSpec((None, 1, tq), lambda h, i, j: (h, 0, i))] + [HBM_SPEC] * ns,
        out_shape=[jax.ShapeDtypeStruct((s, MLA_H * LANES), BF16), jax.ShapeDtypeStruct((MLA_H, 1, s), F32)]
        + _gather_shapes(side),
        scratch_shapes=[pltpu.VMEM((tq, LANES), F32), pltpu.VMEM((tq, LANES), F32)]
        + ([_sems(3 * ns), _sems(3 * ns), _sems(ns)] if ns else []),
        compiler_params=_cp(("arbitrary",) * 3 if ns else ("parallel", "parallel", "arbitrary")),
    )(q, k, v, *[a for a, _, _ in side])
    return outs[0], outs[1], list(outs[2:])


def _attn_bwd_prep(dar, o, *, s, ts):
    w = MLA_H * LANES

    def fn(dov, ov):
        outs = []
        lane = lax.broadcasted_iota(jnp.int32, (dov.shape[0], LANES), 1)
        for h in range(MLA_H):
            sl = slice(h * LANES, (h + 1) * LANES)
            d = dov[:, sl]
            delta = _lanesum(d * ov[:, sl].astype(F32))
            hi = _bf(delta).astype(F32)
            outs.append(jnp.where(lane == V_ONES[0], -hi, jnp.where(lane == V_ONES[1], hi - delta, d)))
        return jnp.concatenate(outs, axis=1)

    return _ew(fn, [_cols(dar, w, 0, ts), _cols(o, w, 0, ts)], [], [(w, BF16)], s=s, ts=ts,
               name="mla_attn_bwd_prep")[0]


def _flash_bwd(q, k, v, do, lse, *, tq, tk):
    s = q.shape[0]
    nq, nk = s // tq, s // tk

    def body(q_ref, k_ref, v_ref, do_ref, lse_ref, dq_ref, dk_ref, dv_ref, dk_acc, dv_acc):
        j = pl.program_id(1)
        i = pl.program_id(2)
        qv, kv, vv, dov = q_ref[...], k_ref[...], v_ref[...], do_ref[...]
        pt = jnp.exp(_dot_nt(kv, qv) - lse_ref[...])
        dst = _bf(pt * _dot_nt(vv, dov))
        dv_c = _dot(_bf(pt), dov)
        dk_c = _dot(dst, qv)
        dq_c = _dot_tn(dst, kv)
        rows = pl.ds(pl.multiple_of(i * tq, tq), tq)

        @pl.when(i == 0)
        def _():
            dk_acc[...] = dk_c
            dv_acc[...] = dv_c

        @pl.when(i > 0)
        def _():
            dk_acc[...] += dk_c
            dv_acc[...] += dv_c

        @pl.when(j == 0)
        def _():
            dq_ref[rows, :] = dq_c

        @pl.when(j > 0)
        def _():
            dq_ref[rows, :] += dq_c

        @pl.when(i == nq - 1)
        def _():
            dk_ref[...] = dk_acc[...]
            dv_ref[...] = dv_acc[...].astype(dv_ref.dtype)

    qs = pl.BlockSpec((tq, LANES), lambda h, j, i: (i, h))
    ks = pl.BlockSpec((tk, LANES), lambda h, j, i: (j, h))
    st = pl.BlockSpec((None, 1, tq), lambda h, j, i: (h, 0, i))
    return pl.pallas_call(
        body, name="mla_flash_bwd", grid=(MLA_H, nk, nq),
        in_specs=[qs, ks, ks, qs, st],
        out_specs=[pl.BlockSpec((s, LANES), lambda h, j, i: (0, h)), ks, ks],
        out_shape=[jax.ShapeDtypeStruct((s, MLA_H * LANES), F32), jax.ShapeDtypeStruct((s, MLA_H * LANES), F32),
                   jax.ShapeDtypeStruct((s, MLA_H * LANES), BF16)],
        scratch_shapes=[pltpu.VMEM((tk, LANES), F32), pltpu.VMEM((tk, LANES), F32)],
        compiler_params=_cp(("parallel", "arbitrary", "arbitrary")),
    )(q, k, v, do, lse)


def _ret_geometry(d, c):
    df = float(d)
    ii = lax.broadcasted_iota(jnp.int32, (c, c), 0).astype(F32)
    jj = lax.broadcasted_iota(jnp.int32, (c, c), 1).astype(F32)
    rel = (ii - jj) * (1.0 - 2.0 * df)
    mask = rel >= df
    rel0 = jnp.maximum(rel, 0.0)
    pos = lax.broadcasted_iota(jnp.int32, (c, 1), 0).astype(F32)
    ez = (c - 1 - pos) + df * (2.0 * pos - (c - 1))
    ex = (pos + 1.0) + df * (c - 1 - 2.0 * pos)
    return mask, rel0, ez, ex


def _chunk_index(n_chunks):
    return lambda d, n: n + d * (n_chunks - 1 - 2 * n)


def _ret_fwd(p_even, cos_r, sin_r, theta_l):
    s = p_even.shape[0]
    c = RET_C
    n_chunks = s // c
    w = RET_H * LANES
    cidx = _chunk_index(n_chunks)

    def body(*refs):
        n = pl.program_id(0)

        @pl.when(n == 0)
        def _():
            for r_s in refs[16:18]:
                r_s[...] = jnp.zeros_like(r_s)

        stores = []
        for d in range(2):
            stores += one(d, *refs[6 * d:6 * d + 6], *refs[12 + 2 * d:14 + 2 * d], refs[16 + d])
        for ref, val in stores:
            ref[...] = val

    def one(d, q_ref, k_ref, v_ref, cos_ref, sin_ref, th_ref, o_ref, rp_ref, r_s):
        lg = jnp.log1p(-jnp.exp(-th_ref[...] * LN2))
        mask, rel0, ez, ex = _ret_geometry(d, c)
        cs, sn = cos_ref[...], sin_ref[...]
        r_all = r_s[...]
        outs, states = [], []
        for h in range(RET_H):
            sl = slice(h * LANES, (h + 1) * LANES)
            lgh = lg[:, h * LANES:h * LANES + 1]
            dm = jnp.where(mask, jnp.exp(lgh * rel0), 0.0)
            qh = _bf(_rope(q_ref[:, sl], cs, sn, RET_DK, 0))
            kf = _rope(k_ref[:, sl], cs, sn, RET_DK, 0) * (RET_DK ** -0.5)
            kh = _bf(kf)
            vh = _bf(v_ref[:, sl])
            rh = r_all[sl, :]
            a = _dot_nt(qh, kh) * dm
            outs.append(_dot(_bf(a), vh) + jnp.exp(lgh * ex) * _dot(qh, _bf(rh)))
            zk = _bf(kf * jnp.exp(lgh * ez))
            states.append(jnp.exp(lgh * c) * rh + _dot_tn(zk, vh))
        return [(rp_ref, r_all), (o_ref, jnp.concatenate(outs, axis=1)), (r_s, jnp.concatenate(states, axis=0))]

    def ins(d):
        col = lambda blk: pl.BlockSpec((c, w), lambda n: (cidx(d, n), blk))
        tab = pl.BlockSpec((c, LANES), lambda n: (cidx(d, n), 0))
        return [col(0), col(1), col(2), tab, tab, pl.BlockSpec((None, 1, w), lambda n: (d, 0, 0))]

    def outs(d):
        return [pl.BlockSpec((c, w), lambda n: (cidx(d, n), 0)),
                pl.BlockSpec((None, w, LANES), lambda n: (cidx(d, n), 0, 0))]

    o_f, r_f, o_b, r_b = pl.pallas_call(
        body, name="ret_fwd", grid=(n_chunks,),
        in_specs=ins(0) + ins(1), out_specs=outs(0) + outs(1),
        out_shape=[jax.ShapeDtypeStruct((s, w), F32), jax.ShapeDtypeStruct((n_chunks, w, LANES), F32)] * 2,
        scratch_shapes=[pltpu.VMEM((w, LANES), F32)] * 2,
        compiler_params=_cp(("arbitrary",)),
    )(*[p_even, p_even, p_even, cos_r, sin_r, theta_l] * 2)
    return (o_f, o_b), (r_f, r_b)


def _ret_bwd(p_even, cos_r, sin_r, theta_l, theta_h, r_prev, do):
    s = p_even.shape[0]
    c = RET_C
    n_chunks = s // c
    w = RET_H * LANES
    fwd_idx = _chunk_index(n_chunks)

    def cidx(d, n):
        return fwd_idx(d, n_chunks - 1 - n)

    def body(*refs):
        n = pl.program_id(0)

        @pl.when(n == 0)
        def _():
            for d in range(2):
                refs[26 + d][...] = jnp.zeros_like(refs[26 + d])
                refs[21 + 4 * d][...] = jnp.zeros_like(refs[21 + 4 * d])

        stores = []
        for d in range(2):
            stores += one(d, *refs[9 * d:9 * d + 9], *refs[18 + 4 * d:22 + 4 * d], refs[26 + d])
        for ref, val, accumulate in stores:
            if accumulate:
                ref[...] += val
            else:
                ref[...] = val

    def one(d, q_ref, k_ref, v_ref, cos_ref, sin_ref, th_ref, thh_ref, rp_ref, do_ref,
            dq_ref, dk_ref, dv_ref, dth_ref, dr_s):
        lg = jnp.log1p(-jnp.exp(-th_ref[...] * LN2))
        mask, rel0, ez, ex = _ret_geometry(d, c)
        cs, sn = cos_ref[...], sin_ref[...]
        rp_all, dr_all = rp_ref[...], dr_s[...]
        row = lax.broadcasted_iota(jnp.int32, (RET_H, LANES), 0)
        dlg = jnp.zeros((RET_H, LANES), F32)
        kscale = RET_DK ** -0.5
        dqs, dks, dvs, drs = [], [], [], []
        for h in range(RET_H):
            sl = slice(h * LANES, (h + 1) * LANES)
            lgh = lg[:, h * LANES:h * LANES + 1]
            dm = jnp.where(mask, jnp.exp(lgh * rel0), 0.0)
            zeta = jnp.exp(lgh * ez)
            xi = jnp.exp(lgh * ex)
            gc = jnp.exp(lgh * c)
            qf = _rope(q_ref[:, sl], cs, sn, RET_DK, 0)
            qh = _bf(qf)
            kf = _rope(k_ref[:, sl], cs, sn, RET_DK, 0) * kscale
            kh = _bf(kf)
            zkf = kf * zeta
            zk = _bf(zkf)
            vh = _bf(v_ref[:, sl])
            dof = do_ref[:, sl]
            doh = _bf(dof)
            rp = rp_all[sl, :]
            rpb = _bf(rp)
            drn = dr_all[sl, :]
            drb = _bf(drn)
            a = _dot_nt(qh, kh) * dm
            da0 = _dot_nt(doh, vh)
            da = _bf(da0 * dm)
            vdr = _dot_nt(vh, drb)
            dq_r = _dot(da, kh) + xi * _dot_nt(doh, rpb)
            dk_r = _dot_tn(da, qh) + zeta * vdr
            dvs.append(_dot_tn(_bf(a), doh) + _dot(zk, drb))
            dqs.append(_rope(dq_r, cs, -sn, RET_DK, 0))
            dks.append(_rope(dk_r * kscale, cs, -sn, RET_DK, 0))
            drs.append(_dot_tn(_bf(qf * xi), doh) + gc * drn)
            ocross = xi * _dot(qh, rpb)
            t = (jnp.sum(rel0 * a * da0, keepdims=True)
                 + jnp.sum(ex * dof * ocross, keepdims=True)
                 + c * gc * jnp.sum(drn * rp, keepdims=True)
                 + jnp.sum(ez * zkf * vdr, keepdims=True))
            dlg = jnp.where(row == h, t, dlg)
        x2 = jnp.exp(-thh_ref[...] * LN2)
        return [(dq_ref, jnp.concatenate(dqs, axis=1), False), (dk_ref, jnp.concatenate(dks, axis=1), False),
                (dv_ref, jnp.concatenate(dvs, axis=1), False), (dr_s, jnp.concatenate(drs, axis=0), False),
                (dth_ref, dlg * (x2 * LN2 / (1.0 - x2)), True)]

    def ins(d):
        col = lambda blk: pl.BlockSpec((c, w), lambda n: (cidx(d, n), blk))
        tab = pl.BlockSpec((c, LANES), lambda n: (cidx(d, n), 0))
        return [col(0), col(1), col(2), tab, tab, pl.BlockSpec((None, 1, w), lambda n: (d, 0, 0)),
                pl.BlockSpec((None, RET_H, LANES), lambda n: (d, 0, 0)),
                pl.BlockSpec((None, w, LANES), lambda n: (cidx(d, n), 0, 0)), col(0)]

    def outs(d):
        row = pl.BlockSpec((c, w), lambda n: (cidx(d, n), 0))
        return [row, row, row, pl.BlockSpec((RET_H, LANES), lambda n: (0, 0))]

    res = pl.pallas_call(
        body, name="ret_bwd", grid=(n_chunks,),
        in_specs=ins(0) + ins(1), out_specs=outs(0) + outs(1),
        out_shape=([jax.ShapeDtypeStruct((s, w), F32)] * 3 + [jax.ShapeDtypeStruct((RET_H, LANES), F32)]) * 2,
        scratch_shapes=[pltpu.VMEM((w, LANES), F32)] * 2,
        compiler_params=_cp(("arbitrary",)),
    )(*[a for d in range(2) for a in (p_even, p_even, p_even, cos_r, sin_r, theta_l, theta_h, r_prev[d], do)])
    return (res[0], res[4]), (res[1], res[5]), (res[2], res[6]), jnp.stack([res[3], res[7]])


def _post_fwd(o2, gate_row, gain, *, group, n, s, ts, name):
    w = o2[0].shape[1]

    def fn(of, ob, g, gv):
        y = _gn(of + ob, gv, group, n)[0]
        return g * _sigmoid(g) * y

    return _ew(fn, [_lead(o2, 0, ts), _lead(o2, 1, ts), gate_row], [gain], [(w, BF16)], s=s, ts=ts, name=name)[0]


def _post_bwd(o2, gate_row, gain, dr_row, *, group, n, s, ts, name):
    w = o2[0].shape[1]

    def fn(of, ob, g, dr, gv):
        y, xn, rstd = _gn(of + ob, gv, group, n)
        sg = _sigmoid(g)
        dy = dr * (g * sg)
        dgate = dr * y * (sg * (1.0 + g * (1.0 - sg)))
        do, dgain = _gn_bwd(dy, xn, rstd, gv, group, n)
        return do, dgate, dgain

    return _ew(fn, [_lead(o2, 0, ts), _lead(o2, 1, ts), gate_row, dr_row], [gain],
               [(w, F32), (w, BF16)], [(1, w)], s=s, ts=ts, name=name)


def _sum2(a2, *, s, ts, name):
    w = a2[0].shape[1]
    return _ew(lambda a, b: a + b, [_lead(a2, 0, ts), _lead(a2, 1, ts)], [], [(w, BF16)], s=s, ts=ts, name=name)[0]


def _gla_common(d, q_ref, k_ref, ga_ref, wg_ref, bg_ref):
    c = GLA_C
    df = float(d)
    ii = lax.broadcasted_iota(jnp.int32, (c, c), 0).astype(F32)
    jj = lax.broadcasted_iota(jnp.int32, (c, c), 1).astype(F32)
    rel = (ii - jj) * (1.0 - 2.0 * df)
    tri = _bf(jnp.where(rel >= 0.0, 1.0, 0.0))
    mask = rel >= df
    gab = _bf(ga_ref[...])
    z = _dot(gab, wg_ref[...]) + bg_ref[...]
    la = (jnp.minimum(z, 0.0) - jnp.log1p(jnp.exp(-jnp.abs(z)))) * (1.0 / GLA_TAU)
    l1, l2, l3 = _split3(la)
    b = _dot(tri, l1) + _dot(tri, l2) + _dot(tri, l3)
    first = d == 0
    bm = b[c // 2:c // 2 + 1] if first else b[c // 2 - 1:c // 2]
    bl = b[c - 1:c] if first else b[0:1]
    q = q_ref[...] * (GLA_DK ** -0.5)
    k = k_ref[...]
    e1, e2, e3, eb = jnp.exp(b - bm), jnp.exp(bm - b), jnp.exp(bl - b), jnp.exp(b)
    return dict(tri=tri, mask=mask, gab=gab, z=z, ebl=jnp.exp(bl), e1=e1, e2=e2, e3=e3, eb=eb,
                qc=q * e1, kc=k * e2, kd=k * e3, qe=q * eb, first=first)


def _col_scale(row_vec, width):
    t = jnp.broadcast_to(row_vec, (LANES, LANES)).T
    return jnp.concatenate([t] * (width // LANES), axis=1)


def _gla_fwd(p_odd, wg2, bg2):
    s = p_odd.shape[0]
    c = GLA_C
    n_chunks = s // c
    wk, wv = GLA_H * GLA_DK, GLA_H * GLA_DV
    cidx = _chunk_index(n_chunks)

    def body(*refs):
        n = pl.program_id(0)

        @pl.when(n == 0)
        def _():
            for s_s in refs[16:18]:
                s_s[...] = jnp.zeros_like(s_s)

        stores = []
        for d in range(2):
            stores += one(d, *refs[6 * d:6 * d + 6], *refs[12 + 2 * d:14 + 2 * d], refs[16 + d])
        for ref, val in stores:
            ref[...] = val

    def one(d, q_ref, k_ref, v_ref, ga_ref, wg_ref, bg_ref, o_ref, sp_ref, s_s):
        g = _gla_common(d, q_ref, k_ref, ga_ref, wg_ref, bg_ref)
        s_all = s_s[...]
        outs, states = [], []
        for h in range(GLA_H):
            sl = slice(h * GLA_DK, (h + 1) * GLA_DK)
            vs = slice(h * GLA_DV, (h + 1) * GLA_DV)
            vh = _bf(v_ref[:, vs])
            sh = s_all[sl, :]
            a = jnp.where(g["mask"], _dot_nt(_bf(g["qc"][:, sl]), _bf(g["kc"][:, sl])), 0.0)
            outs.append(_dot(_bf(a), vh) + _dot(_bf(g["qe"][:, sl]), _bf(sh)))
            states.append(_col_scale(g["ebl"][:, sl], GLA_DV) * sh + _dot_tn(_bf(g["kd"][:, sl]), vh))
        return [(sp_ref, s_all), (o_ref, jnp.concatenate(outs, axis=1)), (s_s, jnp.concatenate(states, axis=0))]

    def ins(d):
        col = lambda width, blk: pl.BlockSpec((c, width), lambda n: (cidx(d, n), blk))
        return [col(wk, 0), col(wk, 1), col(wv, 1), col(LANES, OD_GA_BLK),
                pl.BlockSpec((None, LANES, wk), lambda n: (d, 0, 0)), pl.BlockSpec((None, 1, wk), lambda n: (d, 0, 0))]

    def outs(d):
        return [pl.BlockSpec((c, wv), lambda n: (cidx(d, n), 0)),
                pl.BlockSpec((None, wk, GLA_DV), lambda n: (cidx(d, n), 0, 0))]

    o_f, s_f, o_b, s_b = pl.pallas_call(
        body, name="gla_fwd", grid=(n_chunks,),
        in_specs=ins(0) + ins(1), out_specs=outs(0) + outs(1),
        out_shape=[jax.ShapeDtypeStruct((s, wv), F32), jax.ShapeDtypeStruct((n_chunks, wk, GLA_DV), F32)] * 2,
        scratch_shapes=[pltpu.VMEM((wk, GLA_DV), F32)] * 2,
        compiler_params=_cp(("arbitrary",)),
    )(*[p_odd, p_odd, p_odd, p_odd, wg2, bg2] * 2)
    return (o_f, o_b), (s_f, s_b)


def _gla_bwd(p_odd, wg2, bg2, s_prev, do):
    s = p_odd.shape[0]
    c = GLA_C
    n_chunks = s // c
    wk, wv = GLA_H * GLA_DK, GLA_H * GLA_DV
    fwd_idx = _chunk_index(n_chunks)

    def cidx(d, n):
        return fwd_idx(d, n_chunks - 1 - n)

    def body(*refs):
        n = pl.program_id(0)

        @pl.when(n == 0)
        def _():
            for d in range(2):
                for r in (refs[28 + d], refs[20 + 6 * d], refs[21 + 6 * d]):
                    r[...] = jnp.zeros_like(r)

        stores = []
        for d in range(2):
            stores += one(d, *refs[8 * d:8 * d + 8], *refs[16 + 6 * d:22 + 6 * d], refs[28 + d])
        for ref, val, accumulate in stores:
            if accumulate:
                ref[...] += val
            else:
                ref[...] = val

    def one(d, q_ref, k_ref, v_ref, ga_ref, wg_ref, bg_ref, sp_ref, do_ref,
            dq_ref, dk_ref, dv_ref, dga_ref, dwg_ref, dbg_ref, ds_s):
        g = _gla_common(d, q_ref, k_ref, ga_ref, wg_ref, bg_ref)
        mask = g["mask"]
        ones8 = jnp.ones((8, GLA_DV), BF16)
        sp_all, ds_all = sp_ref[...], ds_s[...]
        dbs, dbms, dbls = [], [], []
        dqs, dks, dvs, dss = [], [], [], []
        for h in range(GLA_H):
            sl = slice(h * GLA_DK, (h + 1) * GLA_DK)
            vs = slice(h * GLA_DV, (h + 1) * GLA_DV)
            qc, kc, kd, qe = g["qc"][:, sl], g["kc"][:, sl], g["kd"][:, sl], g["qe"][:, sl]
            qcb, kcb, kdb, qeb = _bf(qc), _bf(kc), _bf(kd), _bf(qe)
            vh = _bf(v_ref[:, vs])
            doh = _bf(do_ref[:, vs])
            sp = sp_all[sl, :]
            dsn = ds_all[sl, :]
            dsb = _bf(dsn)
            a = _bf(jnp.where(mask, _dot_nt(qcb, kcb), 0.0))
            da = _bf(jnp.where(mask, _dot_nt(doh, vh), 0.0))
            dvs.append(_dot_tn(a, doh) + _dot(kdb, dsb))
            dqc = _dot(da, kcb)
            dkc = _dot_tn(da, qcb)
            dqe = _dot_nt(doh, _bf(sp))
            dkd = _dot_nt(vh, dsb)
            dss.append(_dot_tn(qeb, doh) + _col_scale(g["ebl"][:, sl], GLA_DV) * dsn)
            dqs.append((dqc * g["e1"][:, sl] + dqe * g["eb"][:, sl]) * (GLA_DK ** -0.5))
            dks.append(dkc * g["e2"][:, sl] + dkd * g["e3"][:, sl])
            t1, t2, t3, t4 = dqc * qc, dkc * kc, dqe * qe, dkd * kd
            dbs.append(t1 - t2 + t3 - t4)
            dbms.append(_rowsum(t2 - t1))
            m1, m2, _ = _split3(dsn * sp)
            rs = (_dot_nt(ones8, m1) + _dot_nt(ones8, m2))[0:1]
            dbls.append(_rowsum(t4) + g["ebl"][:, sl] * rs)
        db = jnp.concatenate(dbs, axis=1)
        dbm = jnp.concatenate(dbms, axis=1)
        dbl = jnp.concatenate(dbls, axis=1)
        row = lax.broadcasted_iota(jnp.int32, (c, wk), 0)
        mid = jnp.where(g["first"], c // 2, c // 2 - 1)
        last = jnp.where(g["first"], c - 1, 0)
        db = db + jnp.where(row == mid, dbm, 0.0) + jnp.where(row == last, dbl, 0.0)
        d1, d2, d3 = _split3(db)
        tri = g["tri"]
        dla = _dot_tn(tri, d1) + _dot_tn(tri, d2) + _dot_tn(tri, d3)
        dz = dla * (1.0 / GLA_TAU) * (1.0 - _sigmoid(g["z"]))
        dzb = _bf(dz)
        return [(dq_ref, jnp.concatenate(dqs, axis=1), False), (dk_ref, jnp.concatenate(dks, axis=1), False),
                (dv_ref, jnp.concatenate(dvs, axis=1), False), (ds_s, jnp.concatenate(dss, axis=0), False),
                (dga_ref, _dot_nt(dzb, wg_ref[...]), False), (dwg_ref, _dot_tn(g["gab"], dzb), True),
                (dbg_ref, _rowsum(dz), True)]

    def ins(d):
        col = lambda width, blk: pl.BlockSpec((c, width), lambda n: (cidx(d, n), blk))
        return [col(wk, 0), col(wk, 1), col(wv, 1), col(LANES, OD_GA_BLK),
                pl.BlockSpec((None, LANES, wk), lambda n: (d, 0, 0)), pl.BlockSpec((None, 1, wk), lambda n: (d, 0, 0)),
                pl.BlockSpec((None, wk, GLA_DV), lambda n: (cidx(d, n), 0, 0)), col(wv, 0)]

    def outs(d):
        row = lambda width: pl.BlockSpec((c, width), lambda n: (cidx(d, n), 0))
        return [row(wk), row(wk), row(wv), row(LANES),
                pl.BlockSpec((LANES, wk), lambda n: (0, 0)), pl.BlockSpec((1, wk), lambda n: (0, 0))]

    shapes = [jax.ShapeDtypeStruct((s, wk), F32), jax.ShapeDtypeStruct((s, wk), F32), jax.ShapeDtypeStruct((s, wv), F32),
              jax.ShapeDtypeStruct((s, LANES), F32), jax.ShapeDtypeStruct((LANES, wk), F32),
              jax.ShapeDtypeStruct((1, wk), F32)]
    res = pl.pallas_call(
        body, name="gla_bwd", grid=(n_chunks,),
        in_specs=ins(0) + ins(1), out_specs=outs(0) + outs(1), out_shape=shapes * 2,
        scratch_shapes=[pltpu.VMEM((wk, GLA_DV), F32)] * 2,
        compiler_params=_cp(("arbitrary",)),
    )(*[a for d in range(2) for a in (p_odd, p_odd, p_odd, p_odd, wg2, bg2, s_prev[d], do)])
    pair = lambda k: (res[k], res[6 + k])
    return pair(0), pair(1), pair(2), pair(3), jnp.stack(pair(4)), jnp.stack(pair(5))


HALO = 8


def _halo_specs(width_blk, col0, ts, s):
    r = ts // HALO
    last = s // HALO - 1
    cur = pl.BlockSpec((ts, width_blk), lambda j, i: (i, col0 + j))
    prev = pl.BlockSpec((HALO, width_blk), lambda j, i: (jnp.maximum(i * r - 1, 0), col0 + j))
    nxt = pl.BlockSpec((HALO, width_blk), lambda j, i: (jnp.minimum((i + 1) * r, last), col0 + j))
    return [prev, cur, nxt]


def _with_halo(prev_ref, cur_ref, next_ref, i, n_i):
    p = jnp.where(i == 0, 0.0, prev_ref[...])
    q = jnp.where(i == n_i - 1, 0.0, next_ref[...])
    return jnp.concatenate([p, cur_ref[...], q], axis=0)


def _shift_down(x):
    return pltpu.roll(x, 1, 0)


def _shift_up(x):
    return pltpu.roll(x, x.shape[0] - 1, 0)


def _ffn_act(up, conv_w, conv_b, *, ts):
    s = up.shape[0]
    tc = _tile(D_FF, 1408)
    nj = D_FF // tc
    n_i = s // ts

    def body(gp, gc, gn, val_ref, w_ref, b_ref, a_ref):
        i = pl.program_id(1)
        g = _with_halo(gp, gc, gn, i, n_i)
        w = w_ref[...]
        conv = w[0:1] * _shift_down(g) + w[1:2] * g + w[2:3] * _shift_up(g) + b_ref[...]
        conv = conv[HALO:HALO + ts]
        a_ref[...] = (conv * _sigmoid(conv) * val_ref[...]).astype(a_ref.dtype)

    return pl.pallas_call(
        body, name="ffn_act", grid=(nj, n_i),
        in_specs=_halo_specs(tc, 0, ts, s) + [pl.BlockSpec((ts, tc), lambda j, i: (i, nj + j)),
                                              pl.BlockSpec((3, tc), lambda j, i: (0, j)),
                                              pl.BlockSpec((1, tc), lambda j, i: (0, j))],
        out_specs=pl.BlockSpec((ts, tc), lambda j, i: (i, j)),
        out_shape=jax.ShapeDtypeStruct((s, D_FF), BF16),
        compiler_params=_cp(("parallel", "arbitrary")),
    )(up, up, up, up, conv_w, conv_b)


def _ffn_act_bwd(up, da, conv_w, conv_b, *, ts):
    s = up.shape[0]
    tc = _tile(D_FF, 1408)
    nj = D_FF // tc
    n_i = s // ts

    def body(gp, gc, gn, vp, vc, vn, dp, dc, dn, w_ref, b_ref, dup_ref, dw_ref, db_ref):
        i = pl.program_id(1)
        g = _with_halo(gp, gc, gn, i, n_i)
        v = _with_halo(vp, vc, vn, i, n_i)
        dav = _with_halo(dp, dc, dn, i, n_i)
        w = w_ref[...]
        gm, gpl = _shift_down(g), _shift_up(g)
        conv = w[0:1] * gm + w[1:2] * g + w[2:3] * gpl + b_ref[...]
        sg = _sigmoid(conv)
        dgc = dav * v * (sg * (1.0 + conv * (1.0 - sg)))
        dgate = w[0:1] * _shift_up(dgc) + w[1:2] * dgc + w[2:3] * _shift_down(dgc)
        ctr = slice(HALO, HALO + ts)
        dup_ref[0] = dgate[ctr].astype(dup_ref.dtype)
        dup_ref[1] = (dav[ctr] * (conv * sg)[ctr]).astype(dup_ref.dtype)
        dgc_c = dgc[ctr]
        dw = jnp.concatenate([_rowsum(dgc_c * gm[ctr]), _rowsum(dgc_c * g[ctr]), _rowsum(dgc_c * gpl[ctr])], axis=0)
        dbv = _rowsum(dgc_c)

        @pl.when(i == 0)
        def _():
            dw_ref[...] = dw
            db_ref[...] = dbv

        @pl.when(i > 0)
        def _():
            dw_ref[...] += dw
            db_ref[...] += dbv

    return pl.pallas_call(
        body, name="ffn_act_bwd", grid=(nj, n_i),
        in_specs=(_halo_specs(tc, 0, ts, s) + _halo_specs(tc, nj, ts, s) + _halo_specs(tc, 0, ts, s)
                  + [pl.BlockSpec((3, tc), lambda j, i: (0, j)), pl.BlockSpec((1, tc), lambda j, i: (0, j))]),
        out_specs=[pl.BlockSpec((2, ts, tc), lambda j, i: (0, i, j)),
                   pl.BlockSpec((3, tc), lambda j, i: (0, j)), pl.BlockSpec((1, tc), lambda j, i: (0, j))],
        out_shape=[jax.ShapeDtypeStruct((2, s, D_FF), BF16),
                   jax.ShapeDtypeStruct((3, D_FF), F32), jax.ShapeDtypeStruct((1, D_FF), F32)],
        compiler_params=_cp(("parallel", "arbitrary")),
    )(up, up, up, up, up, up, da, da, da, conv_w, conv_b)


def _loss_head(y, target, *, s, ts):
    def fn(yv, tv):
        err = yv - tv
        return err * (1.0 / D_MODEL), _rowsum(err * err)

    return _ew(fn, [_cols(y, D_MODEL, 0, ts), _cols(target, D_MODEL, 0, ts)], [], [(D_MODEL, F32)],
               [(1, D_MODEL)], s=s, ts=ts, name="loss_head")


def _rows_tile(r, width):
    ts = r
    while ts * width * 4 > (1 << 20) and ts % 16 == 0:
        ts //= 2
    return ts


def _adamw(w, g, m, v, *, ts, name):
    r, width = w.shape
    assert r % ts == 0

    def fn(wv, gv, mv, vv):
        mn = ADAM_B1 * mv + (1.0 - ADAM_B1) * gv
        vn = ADAM_B2 * vv + (1.0 - ADAM_B2) * (gv * gv)
        m_hat = mn / (1.0 - ADAM_B1 ** ADAM_STEP)
        v_hat = vn / (1.0 - ADAM_B2 ** ADAM_STEP)
        delta = -ADAM_LR * (m_hat / (jnp.sqrt(v_hat) + ADAM_EPS) + ADAM_WD * wv)
        return delta, mn, vn

    rows = [_cols(a, width, 0, ts) for a in (w, g, m, v)]
    return _ew(fn, rows, [], [(width, F32)] * 3, s=r, ts=ts, name=name)


def _pad_heads(w, heads, real):
    lead = w.shape[:-1]
    w = w.reshape(lead + (heads, real))
    w = jnp.pad(w, [(0, 0)] * len(lead) + [(0, 0), (0, LANES - real)])
    return w.reshape(lead + (heads * LANES,))


def _pad_head_rows(w, heads, real):
    return _pad_heads(w.T, heads, real).T


def _pack_even(p):
    w_in = p["w_in"]
    z = lambda n: jnp.zeros((D_MODEL, n), w_in.dtype)
    o = 0
    parts = {}
    for nm, n in (("cq", MLA_QR), ("ckv", MLA_KVR), ("kr", MLA_ROPE), ("rq", 512), ("rk", 512), ("rv", 512), ("rg", 512)):
        parts[nm] = w_in[:, o:o + n]
        o += n
    w_in_p = jnp.concatenate(
        [_pad_heads(parts[k], RET_H, RET_DK) for k in ("rq", "rk", "rv", "rg")]
        + [parts["cq"], z(EV_CQ - MLA_QR), parts["ckv"], z(MLA_NOPE), parts["kr"], z(LANES - MLA_QK), z(LANES)], axis=1)
    w_uq = jnp.pad(_pad_heads(p["w_uq"], MLA_H, MLA_QK), ((0, EV_CQ - MLA_QR), (0, 0)))
    ukv = p["w_ukv"].reshape(MLA_KVR, MLA_H, MLA_NOPE + MLA_V)
    w_ukv = jnp.concatenate([_pad_heads(ukv[..., :MLA_NOPE].reshape(MLA_KVR, -1), MLA_H, MLA_NOPE),
                             _pad_heads(ukv[..., MLA_NOPE:].reshape(MLA_KVR, -1), MLA_H, MLA_V)], axis=1)
    w_out = jnp.concatenate([_pad_head_rows(p["w_out"][:MLA_H * MLA_V], MLA_H, MLA_V),
                             _pad_head_rows(p["w_out"][MLA_H * MLA_V:], RET_H, RET_DV)], axis=0)
    return dict(
        w_in=w_in_p, w_uq=w_uq, w_ukv=w_ukv, w_out=w_out,
        mix_g=p["mix_norm"][None, :],
        q_norm=jnp.pad(p["q_norm"], (0, EV_CQ - MLA_QR))[None, :],
        kv_norm=p["kv_norm"][None, :],
        qhn=jnp.pad(p["q_head_norm"], (0, LANES - MLA_QK))[None, :],
        khn=jnp.pad(p["k_head_norm"], (0, LANES - MLA_QK))[None, :],
        ret_gain=_pad_heads(p["ret_out_norm"].reshape(-1), RET_H, RET_DV)[None, :],
    )


def _pack_odd(p):
    w_in = p["w_in"]
    ga = w_in[:, 3072:]
    w_in_p = jnp.concatenate([w_in[:, :3072], ga, jnp.zeros((D_MODEL, LANES - 2 * GLA_R), w_in.dtype)], axis=1)
    wk = GLA_H * GLA_DK
    zf = jnp.zeros((LANES - GLA_R, wk), p["w_gate_fwd"].dtype)
    zb0 = jnp.zeros((GLA_R, wk), p["w_gate_fwd"].dtype)
    zb1 = jnp.zeros((LANES - 2 * GLA_R, wk), p["w_gate_fwd"].dtype)
    wg2 = jnp.stack([jnp.concatenate([p["w_gate_fwd"], zf], axis=0),
                     jnp.concatenate([zb0, p["w_gate_bwd"], zb1], axis=0)])
    bg2 = jnp.stack([p["b_gate_fwd"][None, :], p["b_gate_bwd"][None, :]])
    return dict(w_in=w_in_p, wg2=wg2, bg2=bg2, w_out=p["w_out"], mix_g=p["mix_norm"][None, :],
                gla_gain=p["gla_out_norm"].reshape(1, -1))


_MATRICES = ("w_in", "w_uq", "w_ukv", "w_out", "wg2")


def _packed(pack_fn, p):
    packed = pack_fn(p)
    packed = {k: (_bf(v) if k in _MATRICES else v.astype(F32)) for k, v in packed.items()}
    shapes = {k: jax.ShapeDtypeStruct(v.shape, F32) for k, v in p.items()}
    unpack = jax.linear_transpose(pack_fn, shapes)
    return packed, lambda g: unpack(g)[0]


def _ffn_fwd(x, w, *, s, ts):
    h = _rmsnorm(_cols(x, D_MODEL, 0, ts), w["norm_g"], n=D_MODEL, s=s, ts=ts, name="ffn_norm")
    up = _mm(h, w["w_up4"], b_layer=w["layer"], name="ffn_up")
    a = _ffn_act(up, w["conv_w"], w["conv_b"], ts=ts)
    y = _mm(a, w["w_down"], res=x, name="ffn_down")
    return y, dict(x=x, h=h, up=up, a=a)


def _ffn_bwd(dy, w, sv, *, s, ts):
    da = _mm(dy, w["w_down"], tb=True, name="ffn_down_dx")
    g_down = _mm(sv["a"], dy, ta=True, name="ffn_down_dw")
    dup, g_cw, g_cb = _ffn_act_bwd(sv["up"], da, w["conv_w"], w["conv_b"], ts=ts)
    dh = _mm(dup, w["w_up4"], tb=True, b_layer=w["layer"], halves="a", name="ffn_up_dx")
    g_up = _mm(sv["h"], dup, ta=True, out_chips=True, halves="b", name="ffn_up_dw")
    dx, g_norm = _rmsnorm_bwd(_cols(sv["x"], D_MODEL, 0, ts), w["norm_g"], dh, dy, n=D_MODEL, s=s, ts=ts,
                              name="ffn_norm_bwd")
    return dx, dict(w_up=g_up, w_down=g_down, conv_w=g_cw, conv_b=g_cb, norm_g=g_norm)


def _even_fwd(x, w, tabs, *, s, ts, side=()):
    cos_m, sin_m, cos_r, sin_r = tabs
    h = _rmsnorm(_cols(x, D_MODEL, 0, ts), w["mix_g"], n=D_MODEL, s=s, ts=ts, name="mix_norm")
    p = _mm(h, w["w_in"], name="even_in")
    cqn = _rmsnorm(_cols(p, EV_CQ, EV_RET // EV_CQ, ts), w["q_norm"], n=MLA_QR, s=s, ts=ts, name="mla_q_norm")
    ckvn = _rmsnorm(_cols(p, MLA_KVR, (EV_RET + EV_CQ) // MLA_KVR, ts), w["kv_norm"], n=MLA_KVR, s=s, ts=ts,
                    name="mla_kv_norm")
    q_pre = _mm(cqn, w["w_uq"], name="mla_uq")
    kv_pre = _mm(ckvn, w["w_ukv"], name="mla_ukv")
    q, k, v = _mla_prep(q_pre, kv_pre, p, cos_m, sin_m, w["qhn"], w["khn"], s=s, ts=ts)
    o, lse, gathered = _flash_fwd(q, k, v, tq=min(s, FLASH_FWD_ROWS), tk=min(s, FLASH_KEYS), side=side)
    o2, r_prev = _ret_fwd(p, cos_r, sin_r, w["theta_l"])
    r = _post_fwd(o2, _cols(p, RET_H * LANES, 3, ts), w["ret_gain"], group=LANES, n=RET_DV, s=s, ts=ts,
                  name="ret_post")
    ar = jnp.concatenate([o, r], axis=1)
    y = _mm(ar, w["w_out"], res=x, name="even_out")
    return y, dict(x=x, h=h, p=p, cqn=cqn, ckvn=ckvn, q_pre=q_pre, kv_pre=kv_pre, q=q, k=k, v=v, o=o, lse=lse,
                   o2=o2, r_prev=r_prev, ar=ar), gathered


def _even_bwd(dy, w, sv, tabs, *, s, ts):
    cos_m, sin_m, cos_r, sin_r = tabs
    p = sv["p"]
    wh = MLA_H * LANES
    dar = _mm(dy, w["w_out"], tb=True, name="even_out_dx")
    g_out = _mm(sv["ar"], dy, ta=True, name="even_out_dw")
    do_attn = _attn_bwd_prep(dar, sv["o"], s=s, ts=ts)
    dq, dk, dv = _flash_bwd(sv["q"], sv["k"], sv["v"], do_attn, sv["lse"], tq=min(s, FLASH_BWD_ROWS),
                            tk=min(s, FLASH_KEYS))
    dq_pre, dk_pre, dkr, g_qhn, g_khn = _mla_prep_bwd(sv["q_pre"], sv["kv_pre"], p, cos_m, sin_m, w["qhn"], w["khn"],
                                                      dq, dk, s=s, ts=ts)
    dkv_pre = jnp.concatenate([dk_pre, dv], axis=1)
    dckvn = _mm(dkv_pre, w["w_ukv"], tb=True, name="mla_ukv_dx")
    g_ukv = _mm(sv["ckvn"], dkv_pre, ta=True, name="mla_ukv_dw")
    dcqn = _mm(dq_pre, w["w_uq"], tb=True, name="mla_uq_dx")
    g_uq = _mm(sv["cqn"], dq_pre, ta=True, name="mla_uq_dw")
    dckv, g_kvn = _rmsnorm_bwd(_cols(p, MLA_KVR, (EV_RET + EV_CQ) // MLA_KVR, ts), w["kv_norm"], dckvn, None,
                               n=MLA_KVR, s=s, ts=ts, name="mla_kv_norm_bwd")
    dcq, g_qn = _rmsnorm_bwd(_cols(p, EV_CQ, EV_RET // EV_CQ, ts), w["q_norm"], dcqn, None, n=MLA_QR, s=s, ts=ts,
                             name="mla_q_norm_bwd")
    do, drg, g_gain = _post_bwd(sv["o2"], _cols(p, wh, 3, ts), w["ret_gain"], _cols(dar, wh, 1, ts),
                                group=LANES, n=RET_DV, s=s, ts=ts, name="ret_post_bwd")
    dq2, dk2, dv2, dth = _ret_bwd(p, cos_r, sin_r, w["theta_l"], w["theta_h"], sv["r_prev"], do)
    drq, drk, drv = (_sum2(a, s=s, ts=ts, name="sum_dirs_1024") for a in (dq2, dk2, dv2))
    dp = jnp.concatenate([drq, drk, drv, drg, _bf(dcq), _bf(dckv), dkr, jnp.zeros((s, LANES), BF16)], axis=1)
    dh = _mm(dp, w["w_in"], tb=True, name="even_in_dx")
    g_in = _mm(sv["h"], dp, ta=True, name="even_in_dw")
    dx, g_mix = _rmsnorm_bwd(_cols(sv["x"], D_MODEL, 0, ts), w["mix_g"], dh, dy, n=D_MODEL, s=s, ts=ts,
                             name="mix_norm_bwd")
    grads = dict(w_in=g_in, w_uq=g_uq, w_ukv=g_ukv, w_out=g_out, mix_g=g_mix, q_norm=g_qn, kv_norm=g_kvn,
                 qhn=g_qhn, khn=g_khn, ret_gain=g_gain)
    return dx, grads, dth[:, :, 0]


def _odd_fwd(x, w, *, s, ts):
    h = _rmsnorm(_cols(x, D_MODEL, 0, ts), w["mix_g"], n=D_MODEL, s=s, ts=ts, name="mix_norm")
    p = _mm(h, w["w_in"], name="odd_in")
    o2, s_prev = _gla_fwd(p, w["wg2"], w["bg2"])
    g = _post_fwd(o2, _cols(p, GLA_H * GLA_DV, 2, ts), w["gla_gain"], group=GLA_DV, n=GLA_DV, s=s, ts=ts,
                  name="gla_post")
    y = _mm(g, w["w_out"], res=x, name="odd_out")
    return y, dict(x=x, h=h, p=p, o2=o2, s_prev=s_prev, g=g)


def _odd_bwd(dy, w, sv, *, s, ts):
    p = sv["p"]
    wv = GLA_H * GLA_DV
    dg = _mm(dy, w["w_out"], tb=True, name="odd_out_dx")
    g_out = _mm(sv["g"], dy, ta=True, name="odd_out_dw")
    do, dgr, g_gain = _post_bwd(sv["o2"], _cols(p, wv, 2, ts), w["gla_gain"], _cols(dg, wv, 0, ts),
                                group=GLA_DV, n=GLA_DV, s=s, ts=ts, name="gla_post_bwd")
    dq2, dk2, dv2, dga2, g_wg, g_bg = _gla_bwd(p, w["wg2"], w["bg2"], sv["s_prev"], do)
    dq = _sum2(dq2, s=s, ts=ts, name="sum_dirs_512")
    dk = _sum2(dk2, s=s, ts=ts, name="sum_dirs_512")
    dv = _sum2(dv2, s=s, ts=ts, name="sum_dirs_1024")
    dga = _sum2(dga2, s=s, ts=ts, name="sum_dirs_128")
    dp = jnp.concatenate([dq, dk, dv, dgr, dga], axis=1)
    dh = _mm(dp, w["w_in"], tb=True, name="odd_in_dx")
    g_in = _mm(sv["h"], dp, ta=True, name="odd_in_dw")
    dx, g_mix = _rmsnorm_bwd(_cols(sv["x"], D_MODEL, 0, ts), w["mix_g"], dh, dy, n=D_MODEL, s=s, ts=ts,
                             name="mix_norm_bwd")
    return dx, dict(w_in=g_in, wg2=g_wg, bg2=g_bg, w_out=g_out, mix_g=g_mix, gla_gain=g_gain)


_EVEN_NAMES = dict(mix_norm="mix_norm_even", w_in="w_in_even", q_norm="mla_q_norm", kv_norm="mla_kv_norm",
                   w_uq="mla_w_uq", w_ukv="mla_w_ukv", q_head_norm="mla_q_head_norm", k_head_norm="mla_k_head_norm",
                   ret_out_norm="ret_out_norm", w_out="w_out_even")
_ODD_NAMES = dict(mix_norm="mix_norm_odd", w_in="w_in_odd", w_gate_fwd="gla_w_gate_fwd", b_gate_fwd="gla_b_gate_fwd",
                  w_gate_bwd="gla_w_gate_bwd", b_gate_bwd="gla_b_gate_bwd", gla_out_norm="gla_out_norm",
                  w_out="w_out_odd")

def _local_step(x, pos, target, full, side=(), finish=None):
    s = x.shape[0]
    ts = min(s, 256)
    tabs = _rope_tables(pos, MLA_ROPE, MLA_NOPE) + _rope_tables(pos, RET_DK, 0)

    def layer_weights(layer):
        i = layer // 2
        names = _EVEN_NAMES if layer % 2 == 0 else _ODD_NAMES
        wm, unpack_m = _packed(_pack_even if layer % 2 == 0 else _pack_odd, {k: full[n][i] for k, n in names.items()})
        if layer % 2 == 0:
            th = jnp.stack([full["ret_theta_fwd"][i], full["ret_theta_bwd"][i]]).astype(F32)
            wm["theta_h"] = jnp.broadcast_to(th[:, :, None], (2, RET_H, LANES))
            wm["theta_l"] = wm["theta_h"].reshape(2, 1, RET_H * LANES)
        w_up4, index = full["ffn_w_up"][layer]
        wf = dict(layer=index, w_up4=w_up4, w_down=_bf(full["ffn_w_down"][layer]),
                  conv_w=full["ffn_conv_w"][layer].astype(F32), conv_b=full["ffn_conv_b"][layer][None, :].astype(F32),
                  norm_g=full["ffn_norm"][layer][None, :].astype(F32))
        return wm, unpack_m, wf

    layers, saved = [], []
    for layer in range(DEPTH):
        layers.append(layer_weights(layer))
        wm, _, wf = layers[-1]
        if layer % 2 == 0:
            x, sv_m, gathered = _even_fwd(x, wm, tabs, s=s, ts=ts, side=side if layer == 0 else ())
            if layer == 0 and finish is not None:
                full = finish(gathered)
        else:
            x, sv_m = _odd_fwd(x, wm, s=s, ts=ts)
        x, sv_f = _ffn_fwd(x, wf, s=s, ts=ts)
        saved.append((sv_m, sv_f))

    dy, sq = _loss_head(x, target, s=s, ts=ts)
    loss = 0.5 / D_MODEL * jnp.sum(sq)

    grads = {}

    def put(name, idx, g):
        grads.setdefault(name, {})[idx] = g

    for layer in reversed(range(DEPTH)):
        wm, unpack_m, wf = layers[layer]
        sv_m, sv_f = saved[layer]
        i = layer // 2
        dy, gf = _ffn_bwd(dy, wf, sv_f, s=s, ts=ts)
        put("ffn_w_up", layer, gf["w_up"])
        put("ffn_w_down", layer, gf["w_down"])
        put("ffn_conv_w", layer, gf["conv_w"])
        put("ffn_conv_b", layer, gf["conv_b"][0])
        put("ffn_norm", layer, gf["norm_g"][0])
        if layer % 2 == 0:
            dy, gm, dth = _even_bwd(dy, wm, sv_m, tabs, s=s, ts=ts)
            put("ret_theta_fwd", i, dth[0])
            put("ret_theta_bwd", i, dth[1])
            names = _EVEN_NAMES
        else:
            dy, gm = _odd_bwd(dy, wm, sv_m, s=s, ts=ts)
            names = _ODD_NAMES
        for k, g in unpack_m(gm).items():
            put(names[k], i, g)
    return loss, dy, {n: [g[j] for j in range(len(g))] for n, g in grads.items()}


HBM_SPEC = pl.BlockSpec(memory_space=pltpu.HBM)
VMEM_SPEC = pl.BlockSpec(memory_space=pltpu.VMEM)
CHIPS = 4
CORES = 2
ROW = 8 * LANES


def _xyc():
    return lax.axis_index("x"), lax.axis_index("y"), lax.axis_index("c")


def _other_chips(x, y):
    return [(1 - x, y), (x, 1 - y), (1 - x, 1 - y)]


def _remote(src, dst, send, recv, dev):
    return pltpu.make_async_remote_copy(src_ref=src, dst_ref=dst, send_sem=send, recv_sem=recv,
                                        device_id=dev, device_id_type=MESH)


def _sems(n):
    return pltpu.SemaphoreType.DMA((n,))


def _gather_copies(side, srcs, lands, send, recv, loc):
    n = len(side)
    x, y, c = _xyc()
    me = 2 * x + y
    local, sends, arrivals = [], [], []
    for t, (_, first, count) in enumerate(side):
        src = srcs[t].at[pl.ds(first, count)]
        local.append(pltpu.make_async_copy(src, lands[t].at[me], loc.at[t]))
        for j, (px, py) in enumerate(_other_chips(x, y)):
            k = n * j + t
            sends.append(_remote(src, lands[t].at[me], send.at[k], recv.at[k], (px, py, c)))
            arrivals.append(_remote(src, lands[t].at[2 * px + py], send.at[k], recv.at[k], (px, py, c)))
    return local, sends, arrivals


def _gather_shapes(side):
    return [jax.ShapeDtypeStruct((CHIPS, count) + a.shape[1:], a.dtype) for a, _, count in side]


def _gather_chips(side):
    n = len(side)

    def body(*refs):
        local, sends, arrivals = _gather_copies(side, refs[:n], refs[n:2 * n], *refs[2 * n:])
        for cp in local + sends:
            cp.start()
        for cp in arrivals:
            cp.wait_recv()
        for cp in sends:
            cp.wait_send()
        for cp in local:
            cp.wait()

    return pl.pallas_call(
        body, name="gather_chips", in_specs=[HBM_SPEC] * n, out_specs=[HBM_SPEC] * n,
        out_shape=_gather_shapes(side),
        scratch_shapes=[_sems(3 * n), _sems(3 * n), _sems(n)],
    )(*[a for a, _, _ in side])


def _half_rows(ref, axis, half, which):
    idx = (slice(None),) * axis + (pl.ds(pl.multiple_of(which * half, 8), half),)
    return ref.at[idx]


def _swap_halves(arrs):
    n = len(arrs)

    def body(*refs):
        ins, outs = refs[:n], refs[n:2 * n]
        send, recv = refs[2 * n:]
        x, y, c = _xyc()
        copies = []
        for t in range(n):
            half = arrs[t].shape[2] // CORES
            cp = _remote(_half_rows(ins[t], 2, half, 1 - c), outs[t], send.at[t], recv.at[t], (x, y, 1 - c))
            cp.start()
            copies.append(cp)
        for cp in copies:
            cp.wait()

    return pl.pallas_call(
        body, name="swap_halves", in_specs=[HBM_SPEC] * n, out_specs=[HBM_SPEC] * n,
        out_shape=[jax.ShapeDtypeStruct(a.shape[:2] + (a.shape[2] // CORES, a.shape[3]), a.dtype) for a in arrs],
        scratch_shapes=[_sems(n), _sems(n)],
    )(*arrs)


def _add_core_halves(a, got, core, *, ts, name):
    ch, nl, r, cols = a.shape
    half = r // CORES
    nb = half // ts

    def body(core_ref, a_ref, g_ref, o_ref):
        o_ref[...] = (a_ref[...] + g_ref[...]).astype(o_ref.dtype)

    rows = pl.BlockSpec((ts, cols), lambda g, i, cr: (g * nb + i, 0))
    return pl.pallas_call(
        body, name=name, out_shape=jax.ShapeDtypeStruct((ch * nl * half, cols), BF16),
        grid_spec=pltpu.PrefetchScalarGridSpec(
            num_scalar_prefetch=1, grid=(ch * nl, nb),
            in_specs=[pl.BlockSpec((ts, cols), lambda g, i, cr: (g * (r // ts) + cr[0] * nb + i, 0)), rows],
            out_specs=rows),
        compiler_params=_cp(("arbitrary", "arbitrary")),
    )(core, a.reshape(-1, cols), got.reshape(-1, cols)).reshape(got.shape)


def _add_chip_parts(parts, core, *, ts, name):
    ch, nl, half, cols = parts.shape
    nb = half // ts
    r = half * CORES

    def body(core_ref, *refs):
        acc = refs[0][...].astype(F32)
        for p in refs[1:ch]:
            acc = acc + p[...].astype(F32)
        refs[ch][...] = acc

    return pl.pallas_call(
        body, name=name, out_shape=jax.ShapeDtypeStruct((nl * r, cols), F32),
        grid_spec=pltpu.PrefetchScalarGridSpec(
            num_scalar_prefetch=1, grid=(nl, nb),
            in_specs=[pl.BlockSpec((ts, cols), lambda l, i, cr, j=j: ((j * nl + l) * nb + i, 0)) for j in range(ch)],
            out_specs=pl.BlockSpec((ts, cols), lambda l, i, cr: (l * (r // ts) + cr[0] * nb + i, 0))),
        compiler_params=_cp(("arbitrary", "arbitrary")),
    )(core, *[parts.reshape(-1, cols)] * ch).reshape(nl, r, cols)


def _scatter_chips(arrs):
    n = len(arrs)

    def body(*refs):
        ins, outs = refs[:n], refs[n:2 * n]
        send, recv, loc = refs[2 * n:]
        x, y, c = _xyc()
        me = 2 * x + y
        copies = []
        for t in range(n):
            cp = pltpu.make_async_copy(ins[t].at[me], outs[t].at[me], loc.at[t])
            cp.start()
            copies.append(cp)
        sends = []
        for j, (px, py) in enumerate(_other_chips(x, y)):
            for t in range(n):
                cp = _remote(ins[t].at[2 * px + py], outs[t].at[me], send.at[n * j + t], recv.at[n * j + t], (px, py, c))
                cp.start()
                sends.append(cp)
        for j, (px, py) in enumerate(_other_chips(x, y)):
            for t in range(n):
                _remote(ins[t].at[me], outs[t].at[2 * px + py], send.at[n * j + t], recv.at[n * j + t],
                        (px, py, c)).wait_recv()
        for cp in sends:
            cp.wait_send()
        for cp in copies:
            cp.wait()

    return pl.pallas_call(
        body, name="scatter_chips", in_specs=[HBM_SPEC] * n, out_specs=[HBM_SPEC] * n,
        out_shape=[jax.ShapeDtypeStruct(a.shape, a.dtype) for a in arrs],
        scratch_shapes=[_sems(3 * n), _sems(3 * n), _sems(n)],
    )(*arrs)


def _gather_cores(arrs):
    n = len(arrs)

    def body(*refs):
        ins, outs = refs[:n], refs[n:2 * n]
        send, recv = refs[2 * n:]
        x, y, c = _xyc()
        sends = []
        for t in range(n):
            half = arrs[t].shape[1] // CORES
            cp = _remote(_half_rows(ins[t], 1, half, c), _half_rows(outs[t], 1, half, c), send.at[t], recv.at[t],
                         (x, y, 1 - c))
            cp.start()
            sends.append(cp)
        for t in range(n):
            half = arrs[t].shape[1] // CORES
            _remote(_half_rows(ins[t], 1, half, 1 - c), _half_rows(outs[t], 1, half, 1 - c), send.at[t], recv.at[t],
                    (x, y, 1 - c)).wait_recv()
        for cp in sends:
            cp.wait_send()

    return pl.pallas_call(
        body, name="gather_cores", in_specs=[HBM_SPEC] * n, out_specs=[HBM_SPEC] * n,
        out_shape=[jax.ShapeDtypeStruct(a.shape, a.dtype) for a in arrs],
        input_output_aliases={t: t for t in range(n)},
        scratch_shapes=[_sems(n), _sems(n)],
    )(*arrs)


def _all_reduce_devices(v):
    n_dev = CHIPS * CORES

    def body(v_ref, o_ref, buf, send, recv):
        x, y, c = _xyc()
        me = 4 * x + 2 * y + c
        buf[pl.ds(me, 1)] = v_ref[...][None]
        sends = []
        for m in range(1, n_dev):
            px = 1 - x if m & 4 else x
            py = 1 - y if m & 2 else y
            pc = 1 - c if m & 1 else c
            cp = _remote(v_ref, buf.at[me], send.at[m - 1], recv.at[m - 1], (px, py, pc))
            cp.start()
            sends.append((cp, 4 * px + 2 * py + pc))
        for m, (cp, peer) in enumerate(sends):
            _remote(v_ref, buf.at[peer], send.at[m], recv.at[m], (x, y, c)).wait_recv()
        for cp, _ in sends:
            cp.wait_send()
        acc = buf[0]
        for k in range(1, n_dev):
            acc = acc + buf[k]
        o_ref[...] = acc

    return pl.pallas_call(
        body, name="all_reduce_devices", in_specs=[VMEM_SPEC], out_specs=VMEM_SPEC,
        out_shape=jax.ShapeDtypeStruct(v.shape, F32),
        scratch_shapes=[pltpu.VMEM((n_dev,) + v.shape, F32), pltpu.SemaphoreType.DMA((n_dev - 1,)),
                        pltpu.SemaphoreType.DMA((n_dev - 1,))],
    )(v)


_SHARDED = (("w_in_even", 2), ("mla_w_uq", 2), ("mla_w_ukv", 2), ("w_out_even", 1), ("w_in_odd", 2), ("w_out_odd", 1),
            ("ffn_w_up", 2), ("ffn_w_down", 1),
            ("mix_norm_odd", 1), ("gla_w_gate_fwd", 2), ("gla_b_gate_fwd", 1), ("gla_w_gate_bwd", 2),
            ("gla_b_gate_bwd", 1), ("gla_out_norm", 2), ("ffn_conv_w", 2))
_N_MATRICES = 8
_REPLICATED = ("mix_norm_even", "mla_q_norm", "mla_kv_norm", "mla_q_head_norm", "mla_k_head_norm", "ret_theta_fwd",
               "ret_theta_bwd", "ret_out_norm", "ffn_norm", "ffn_conv_b")
_WEIGHTS = ("mix_norm_even", "w_in_even", "mla_q_norm", "mla_kv_norm", "mla_w_uq", "mla_w_ukv", "mla_q_head_norm",
            "mla_k_head_norm", "ret_theta_fwd", "ret_theta_bwd", "ret_out_norm", "w_out_even", "mix_norm_odd",
            "w_in_odd", "gla_w_gate_fwd", "gla_b_gate_fwd", "gla_w_gate_bwd", "gla_b_gate_bwd", "gla_out_norm",
            "w_out_odd", "ffn_norm", "ffn_w_up", "ffn_conv_w", "ffn_conv_b", "ffn_w_down")


def _flatten(arrs, row_multiple, dtype):
    flat = jnp.concatenate([a.reshape(-1).astype(dtype) for a in arrs])
    per = ROW * row_multiple
    total = -(-flat.shape[0] // per) * per
    return jnp.pad(flat, (0, total - flat.shape[0])).reshape(-1, ROW)


def _unflatten(flat, shapes):
    flat = flat.reshape(-1)
    out, o = [], 0
    for shp in shapes:
        n = math.prod(shp)
        out.append(flat[o:o + n].reshape(shp))
        o += n
    return out


def kernel(x, positions, mix_norm_even, w_in_even, mla_q_norm, mla_kv_norm, mla_w_uq, mla_w_ukv, mla_q_head_norm, mla_k_head_norm, ret_theta_fwd, ret_theta_bwd, ret_out_norm, w_out_even, mix_norm_odd, w_in_odd, gla_w_gate_fwd, gla_b_gate_fwd, gla_w_gate_bwd, gla_b_gate_bwd, gla_out_norm, w_out_odd, ffn_norm, ffn_w_up, ffn_conv_w, ffn_conv_b, ffn_w_down, loss_target, m_mix_norm_even, m_w_in_even, m_mla_q_norm, m_mla_kv_norm, m_mla_w_uq, m_mla_w_ukv, m_mla_q_head_norm, m_mla_k_head_norm, m_ret_theta_fwd, m_ret_theta_bwd, m_ret_out_norm, m_w_out_even, m_mix_norm_odd, m_w_in_odd, m_gla_w_gate_fwd, m_gla_b_gate_fwd, m_gla_w_gate_bwd, m_gla_b_gate_bwd, m_gla_out_norm, m_w_out_odd, m_ffn_norm, m_ffn_w_up, m_ffn_conv_w, m_ffn_conv_b, m_ffn_w_down, v_mix_norm_even, v_w_in_even, v_mla_q_norm, v_mla_kv_norm, v_mla_w_uq, v_mla_w_ukv, v_mla_q_head_norm, v_mla_k_head_norm, v_ret_theta_fwd, v_ret_theta_bwd, v_ret_out_norm, v_w_out_even, v_mix_norm_odd, v_w_in_odd, v_gla_w_gate_fwd, v_gla_b_gate_fwd, v_gla_w_gate_bwd, v_gla_b_gate_bwd, v_gla_out_norm, v_w_out_odd, v_ffn_norm, v_ffn_w_up, v_ffn_conv_w, v_ffn_conv_b, v_ffn_w_down):
    args = dict(locals())
    x2, pos, target = args["x"][0], args["positions"][0], args["loss_target"][0]
    axis = dict(_SHARDED)
    mats = [n for n, _ in _SHARDED[:_N_MATRICES]]
    smalls = [n for n, _ in _SHARDED[_N_MATRICES:]]
    small_shapes = [args[n].shape for n in smalls]

    local = {n: _bf(args[n]) for n in mats}
    first_layers = {n: (0, 0 if n.endswith("_odd") else 1) for n in mats}
    now = [(local[n],) + first_layers[n] for n in mats if first_layers[n][1]]
    later = [(local[n], first_layers[n][1], args[n].shape[0] - first_layers[n][1]) for n in mats]
    small_block = _flatten([args[n] for n in smalls], 2 * HALO, F32)
    got_now = _gather_chips(now + [(small_block, 0, small_block.shape[0])])
    per_chip = [_unflatten(got_now[-1][j], small_shapes) for j in range(CHIPS)]
    base = {n: args[n] for n in _REPLICATED}
    for k, n in enumerate(smalls):
        base[n] = jnp.concatenate([per_chip[j][k] for j in range(CHIPS)], axis=axis[n])

    def whole(stacks):
        full = dict(base)
        for n, per_layer in stacks.items():
            if n == "ffn_w_up":
                full[n] = per_layer
            else:
                full[n] = [None if st is None else jnp.concatenate([st[j, l] for j in range(CHIPS)], axis=axis[n] - 1)
                           for st, l in per_layer]
        return full

    stacks = {n: [(None, 0)] * args[n].shape[0] for n in mats}
    for (a, first, count), st in zip(now, got_now):
        n = next(m for m in mats if local[m] is a)
        stacks[n] = [(st, l) for l in range(count)] + stacks[n][count:]

    def finish(got_later):
        for (a, first, count), st in zip(later, got_later):
            n = next(m for m in mats if local[m] is a)
            stacks[n] = stacks[n][:first] + [(st, l) for l in range(count)]
        return whole(stacks)

    loss, grad_x, grads = _local_step(x2, pos, target, whole(stacks), side=later, finish=finish)
    loss = lax.psum(loss, ("x", "y", "c"))

    def by_chip(n, g):
        if n == "ffn_w_up":
            return g
        if axis[n] == 1:
            return g.reshape((CHIPS, g.shape[0] // CHIPS) + g.shape[1:])
        return jnp.stack(jnp.split(g, CHIPS, axis=axis[n] - 1))

    core = lax.axis_index("c").astype(jnp.int32).reshape(1)
    stacked = [jnp.stack([by_chip(n, g) for g in grads[n]], axis=1) for n in mats]
    small_parts = [jnp.split(jnp.stack(grads[n]), CHIPS, axis=axis[n]) for n in smalls]
    stacked.append(jnp.stack([_flatten([p[j] for p in small_parts], 2 * HALO, F32) for j in range(CHIPS)])[:, None])
    names = mats + ["small"]
    tiles = [_rows_tile(a.shape[2] // CORES, a.shape[3]) for a in stacked]
    got = _swap_halves(stacked)
    chip_sums = [_add_core_halves(a, b, core, ts=ts, name="add_core_halves_" + n)
                 for n, a, b, ts in zip(names, stacked, got, tiles)]
    parts = _scatter_chips(chip_sums)
    sums = [_add_chip_parts(p, core, ts=ts, name="add_chip_parts_" + n) for n, p, ts in zip(names, parts, tiles)]
    reduced = _gather_cores(sums)

    res = {}

    def update(n, w, g, m, v, ts):
        cols = g.shape[-1]
        outs = _adamw(w.reshape(-1, cols), g.reshape(-1, cols), m.reshape(-1, cols), v.reshape(-1, cols), ts=ts,
                      name="adamw_" + n)
        return [g] + [o.reshape(g.shape) for o in outs]

    kinds = ("grad", "delta", "new_m", "new_v")
    for n, g, ts in zip(mats, reduced, tiles):
        for kind, a in zip(kinds, update(n, args[n], g, args["m_" + n], args["v_" + n], ts)):
            res[kind + "_" + n] = a
    w_s, m_s, v_s = (_flatten([args[pre + n] for n in smalls], 2 * HALO, F32) for pre in ("", "m_", "v_"))
    for kind, flat in zip(kinds, update("small", w_s, reduced[-1][0], m_s, v_s, tiles[-1])):
        for n, a in zip(smalls, _unflatten(flat, small_shapes)):
            res[kind + "_" + n] = a

    rep_shapes = [args[n].shape for n in _REPLICATED]
    g_rep = _all_reduce_devices(_flatten([jnp.stack(grads[n]) for n in _REPLICATED], HALO, F32))
    w_rep, m_rep, v_rep = (_flatten([args[pre + n] for n in _REPLICATED], HALO, F32) for pre in ("", "m_", "v_"))
    for kind, flat in zip(kinds, update("replicated", w_rep, g_rep, m_rep, v_rep, g_rep.shape[0])):
        for n, a in zip(_REPLICATED, _unflatten(flat, rep_shapes)):
            res[kind + "_" + n] = a

    outs = [loss, grad_x[None]]
    for kind in ("grad", "delta", "new_m", "new_v"):
        outs += [res[kind + "_" + n] for n in _WEIGHTS]
    return tuple(outs)
```

```python
import math

import jax
import jax.numpy as jnp
from jax import lax
from jax.experimental import pallas as pl
from jax.experimental.pallas import tpu as pltpu

F32 = jnp.float32
BF16 = jnp.bfloat16
MESH = pl.DeviceIdType.MESH

EPS = 1e-6
D_MODEL = 1024
DEPTH = 4
LANES = 128
MLA_H, MLA_QR, MLA_KVR, MLA_NOPE, MLA_ROPE, MLA_V = 8, 384, 256, 64, 32, 64
MLA_QK = MLA_NOPE + MLA_ROPE
MLA_SCALE = MLA_QK ** -0.5
RET_H, RET_DK, RET_DV, RET_C = 8, 64, 64, 128
GLA_H, GLA_DK, GLA_DV, GLA_R, GLA_TAU, GLA_C = 4, 128, 256, 16, 16.0, 64
D_FF = 2816
ROPE_THETA = 10000.0
LN2 = math.log(2.0)
ADAM_LR, ADAM_B1, ADAM_B2, ADAM_EPS, ADAM_WD, ADAM_STEP = 0.001, 0.9, 0.999, 1e-08, 0.01, 10

EV_RET = 4 * RET_H * LANES
EV_CQ = 512
EV_W = 5120
EV_KR_BLK = (EV_RET + EV_CQ + MLA_KVR) // LANES
OD_W = 3200
OD_GA_BLK = 3072 // LANES

VMEM_LIMIT = 56 * 1024 * 1024
MM_TILE_CAP = 1408
EW_ROWS = 512
FFN_ACT_ROWS = 256
V_ONES = (MLA_V, MLA_V + 1)
FLASH_FWD_ROWS = 1024
FLASH_BWD_ROWS = 1024
FLASH_KEYS = 1024


def _cp(sem):
    return pltpu.CompilerParams(dimension_semantics=sem, vmem_limit_bytes=VMEM_LIMIT)


def _dot(a, b):
    return jnp.dot(a, b, preferred_element_type=F32)


def _dot_nt(a, b):
    return lax.dot_general(a, b, (((1,), (1,)), ((), ())), preferred_element_type=F32)


def _dot_tn(a, b):
    return lax.dot_general(a, b, (((0,), (0,)), ((), ())), preferred_element_type=F32)


def _bf(x):
    return x.astype(BF16)


def _split3(x):
    h1 = _bf(x)
    r1 = x - h1.astype(F32)
    h2 = _bf(r1)
    h3 = _bf(r1 - h2.astype(F32))
    return h1, h2, h3


def _tile(n, cap):
    if n <= cap:
        return n
    best = None
    for t in range(LANES, cap + 1, LANES):
        if n % t == 0:
            best = t
    assert best is not None, n
    return best


def _mm(a, b, *, ta=False, tb=False, res=None, out_dtype=F32, b_layer=None, out_chips=False, halves=None, name):
    assert not (ta and tb)
    if halves == "a":
        assert not ta
        m, kdim = a.shape[1], 2 * a.shape[2]
    elif ta:
        kdim, m = a.shape
    else:
        m, kdim = a.shape
    if b_layer is not None:
        rows_b, cols_b = b.shape[2], b.shape[0] * b.shape[3]
    elif halves == "b":
        assert not tb
        rows_b, cols_b = b.shape[1], 2 * b.shape[2]
    else:
        rows_b, cols_b = b.shape
    n, kb = (rows_b, cols_b) if tb else (cols_b, rows_b)
    assert kb == kdim, (a.shape, b.shape, ta, tb)
    tm, tn, tk = _tile(m, MM_TILE_CAP), _tile(n, MM_TILE_CAP), _tile(kdim, MM_TILE_CAP)
    nk = kdim // tk
    has_res = res is not None
    vmem = (2 * tm * tk * a.dtype.itemsize + 2 * tk * tn * b.dtype.itemsize
            + 2 * tm * tn * jnp.dtype(out_dtype).itemsize + (2 * tm * tn * 4 if has_res else 0)
            + (tm * tn * 4 if nk > 1 else 0))
    assert vmem <= VMEM_LIMIT - 8 * 1024 * 1024, (name, vmem)
    a_spec = (pl.BlockSpec((tk, tm), lambda i, j, k: (k, i)) if ta
              else pl.BlockSpec((tm, tk), lambda i, j, k: (i, k)))
    if halves == "a":
        per_half = a.shape[2] // tk
        a_spec = pl.BlockSpec((None, tm, tk), lambda i, j, k: (k // per_half, i, k % per_half))
    if halves == "b":
        per_half = b.shape[2] // tn
        b_spec = pl.BlockSpec((None, tk, tn), lambda i, j, k: (j // per_half, k, j % per_half))
    elif b_layer is not None:
        per_chip = b.shape[3]
        if tb:
            assert tk == per_chip
            b_spec = pl.BlockSpec((None, None, tn, tk), lambda i, j, k: (k, b_layer, j, 0))
        else:
            assert tn == per_chip
            b_spec = pl.BlockSpec((None, None, tk, tn), lambda i, j, k: (j, b_layer, k, 0))
    else:
        b_spec = (pl.BlockSpec((tn, tk), lambda i, j, k: (j, k)) if tb
                  else pl.BlockSpec((tk, tn), lambda i, j, k: (k, j)))
    if out_chips:
        assert n // tn == CHIPS and not has_res
        o_spec = pl.BlockSpec((None, tm, tn), lambda i, j, k: (j, i, 0))
        out_struct = jax.ShapeDtypeStruct((CHIPS, m, tn), out_dtype)
    else:
        o_spec = pl.BlockSpec((tm, tn), lambda i, j, k: (i, j))
        out_struct = jax.ShapeDtypeStruct((m, n), out_dtype)

    def product(a_ref, b_ref):
        av, bv = _bf(a_ref[...]), _bf(b_ref[...])
        if ta:
            return _dot_tn(av, bv)
        if tb:
            return _dot_nt(av, bv)
        return _dot(av, bv)

    def body(*refs):
        a_ref, b_ref = refs[:2]
        r_ref = refs[2] if has_res else None
        o_ref = refs[3] if has_res else refs[2]

        def finish(r):
            if has_res:
                r = r + r_ref[...]
            o_ref[...] = r.astype(o_ref.dtype)

        if nk == 1:
            finish(product(a_ref, b_ref))
            return
        acc = refs[-1]
        k = pl.program_id(2)

        @pl.when(k == 0)
        def _():
            acc[...] = product(a_ref, b_ref)

        @pl.when(k > 0)
        def _():
            acc[...] += product(a_ref, b_ref)

        @pl.when(k == nk - 1)
        def _():
            finish(acc[...])

    ins = [a, b] + ([res] if has_res else [])
    in_specs = [a_spec, b_spec] + ([o_spec] if has_res else [])
    return pl.pallas_call(
        body, name=name, grid=(m // tm, n // tn, nk),
        in_specs=in_specs, out_specs=o_spec, out_shape=out_struct,
        scratch_shapes=[pltpu.VMEM((tm, tn), F32)] if nk > 1 else [],
        compiler_params=_cp(("parallel", "parallel", "arbitrary")),
    )(*ins)


def _ew(fn, rows, pars, outs, accs=(), *, s, ts, name):
    n_in = len(rows) + len(pars)
    n_o = len(outs)

    def body(*refs):
        i = pl.program_id(0)
        vals = fn(*[r[...] for r in refs[:n_in]])
        if not isinstance(vals, (tuple, list)):
            vals = (vals,)
        assert len(vals) == n_o + len(accs), (name, len(vals))
        for r, v in zip(refs[n_in:n_in + n_o], vals[:n_o]):
            r[...] = v.astype(r.dtype)
        for r, v in zip(refs[n_in + n_o:], vals[n_o:]):
            @pl.when(i == 0)
            def _(r=r, v=v):
                r[...] = v

            @pl.when(i > 0)
            def _(r=r, v=v):
                r[...] += v

    in_specs = [sp for _, sp in rows]
    in_specs += [pl.BlockSpec(p.shape, lambda i, nd=p.ndim: (0,) * nd) for p in pars]
    out_specs = [pl.BlockSpec((ts, w), lambda i: (i, 0)) for w, _ in outs]
    out_specs += [pl.BlockSpec((r, w), lambda i: (0, 0)) for r, w in accs]
    out_shape = [jax.ShapeDtypeStruct((s, w), dt) for w, dt in outs]
    out_shape += [jax.ShapeDtypeStruct((r, w), F32) for r, w in accs]
    return pl.pallas_call(
        body, name=name, grid=(s // ts,), in_specs=in_specs, out_specs=out_specs, out_shape=out_shape,
        compiler_params=_cp(("arbitrary",)),
    )(*[a for a, _ in rows], *pars)


def _cols(arr, width, blk, ts):
    return (arr, pl.BlockSpec((ts, width), lambda i, b=blk: (i, b)))


def _lead(pair, d, ts):
    return _cols(pair[d], pair[d].shape[1], 0, ts)


def _rowsum(x):
    return jnp.sum(x, axis=0, keepdims=True)


def _lanesum(x):
    return jnp.sum(x, axis=-1, keepdims=True)


def _gsum(x, group):
    w = x.shape[-1]
    if group == w:
        return jnp.broadcast_to(_lanesum(x), x.shape)
    parts = [jnp.broadcast_to(_lanesum(x[:, g:g + group]), (x.shape[0], group)) for g in range(0, w, group)]
    return jnp.concatenate(parts, axis=-1)


def _gn(x, gain, group, n):
    rstd = lax.rsqrt(_gsum(x * x, group) * (1.0 / n) + EPS)
    xn = x * rstd
    return xn * gain, xn, rstd


def _gn_bwd(dy, xn, rstd, gain, group, n):
    dxn = dy * gain
    dx = rstd * (dxn - xn * (_gsum(dxn * xn, group) * (1.0 / n)))
    return dx, _rowsum(dy * xn)


def _sigmoid(x):
    return 1.0 / (1.0 + jnp.exp(-x))


def _rmsnorm(x_row, g, *, n, s, ts, name):
    w = g.shape[-1]

    def fn(x, gv):
        return _gn(x, gv, w, n)[0]

    return _ew(fn, [x_row], [g], [(w, BF16)], s=s, ts=ts, name=name)[0]


def _rmsnorm_bwd(x_row, g, dh, dres, *, n, s, ts, name):
    w = g.shape[-1]
    has_res = dres is not None

    def fn(x, dhv, *rest):
        gv = rest[-1]
        _, xn, rstd = _gn(x, gv, w, n)
        dx, dg = _gn_bwd(dhv, xn, rstd, gv, w, n)
        if has_res:
            dx = dx + rest[0]
        return dx, dg

    rows = [x_row, _cols(dh, w, 0, ts)] + ([_cols(dres, w, 0, ts)] if has_res else [])
    return _ew(fn, rows, [g], [(w, F32)], [(1, w)], s=s, ts=ts, name=name)


def _rope_tables(pos, real, offset):
    half = real // 2
    inv = ROPE_THETA ** (-jnp.arange(half, dtype=F32) / half)
    ang = pos.astype(F32)[:, None] * inv
    c, sn = jnp.cos(ang), jnp.sin(ang)
    s = pos.shape[0]
    cos_t = jnp.concatenate([jnp.ones((s, offset), F32), c, c,
                             jnp.ones((s, LANES - offset - real), F32)], axis=1)
    sin_t = jnp.concatenate([jnp.zeros((s, offset), F32), -sn, sn,
                             jnp.zeros((s, LANES - offset - real), F32)], axis=1)
    return cos_t, sin_t


def _rope(x, cos_t, sin_t, real, offset):
    half = real // 2
    lane = lax.broadcasted_iota(jnp.int32, x.shape, 1)
    partner = jnp.where(lane < offset + half, pltpu.roll(x, LANES - half, 1), pltpu.roll(x, half, 1))
    return x * cos_t + partner * sin_t


def _mla_prep(q_pre, kv_pre, p_even, cos_m, sin_m, qhn, khn, *, s, ts):
    w = MLA_H * LANES

    def fn(qp, kp, vp, kr, c, sn, gq, gk):
        qs, ks = [], []
        for h in range(MLA_H):
            sl = slice(h * LANES, (h + 1) * LANES)
            qn = _gn(qp[:, sl], gq, LANES, MLA_QK)[0]
            kn = _gn(kp[:, sl] + kr, gk, LANES, MLA_QK)[0]
            qs.append(_rope(qn, c, sn, MLA_ROPE, MLA_NOPE) * MLA_SCALE)
            ks.append(_rope(kn, c, sn, MLA_ROPE, MLA_NOPE))
        lane = lax.broadcasted_iota(jnp.int32, vp.shape, 1) % LANES
        ones = (lane == V_ONES[0]) | (lane == V_ONES[1])
        return jnp.concatenate(qs, axis=1), jnp.concatenate(ks, axis=1), jnp.where(ones, 1.0, vp)

    rows = [_cols(q_pre, w, 0, ts), _cols(kv_pre, w, 0, ts), _cols(kv_pre, w, 1, ts),
            _cols(p_even, LANES, EV_KR_BLK, ts), _cols(cos_m, LANES, 0, ts), _cols(sin_m, LANES, 0, ts)]
    return _ew(fn, rows, [qhn, khn], [(w, BF16)] * 3, s=s, ts=ts, name="mla_prep")


def _mla_prep_bwd(q_pre, kv_pre, p_even, cos_m, sin_m, qhn, khn, dq, dk, *, s, ts):
    w = MLA_H * LANES

    def fn(qp, kp, kr, c, sn, dqv, dkv, gq, gk):
        dqs, dks = [], []
        dkr = jnp.zeros_like(kr)
        dgq = jnp.zeros((1, LANES), F32)
        dgk = jnp.zeros((1, LANES), F32)
        for h in range(MLA_H):
            sl = slice(h * LANES, (h + 1) * LANES)
            _, qn, qr = _gn(qp[:, sl], gq, LANES, MLA_QK)
            _, kn, krs = _gn(kp[:, sl] + kr, gk, LANES, MLA_QK)
            dqn = _rope(dqv[:, sl] * MLA_SCALE, c, -sn, MLA_ROPE, MLA_NOPE)
            dkn = _rope(dkv[:, sl], c, -sn, MLA_ROPE, MLA_NOPE)
            dqh, g1 = _gn_bwd(dqn, qn, qr, gq, LANES, MLA_QK)
            dkh, g2 = _gn_bwd(dkn, kn, krs, gk, LANES, MLA_QK)
            dqs.append(dqh)
            dks.append(dkh)
            dkr = dkr + dkh
            dgq = dgq + g1
            dgk = dgk + g2
        return jnp.concatenate(dqs, axis=1), jnp.concatenate(dks, axis=1), dkr, dgq, dgk

    rows = [_cols(q_pre, w, 0, ts), _cols(kv_pre, w, 0, ts), _cols(p_even, LANES, EV_KR_BLK, ts),
            _cols(cos_m, LANES, 0, ts), _cols(sin_m, LANES, 0, ts), _cols(dq, w, 0, ts), _cols(dk, w, 0, ts)]
    return _ew(fn, rows, [qhn, khn], [(w, BF16), (w, BF16), (LANES, BF16)], [(1, LANES), (1, LANES)],
               s=s, ts=ts, name="mla_prep_bwd")


def _flash_fwd(q, k, v, *, tq, tk, side=()):
    s = q.shape[0]
    nq, nk = s // tq, s // tk
    rq = tq
    ns = len(side)

    def body(*refs):
        q_ref, k_ref, v_ref = refs[:3]
        o_ref, lse_ref = refs[3 + ns:5 + ns]
        m_s, acc = refs[5 + 2 * ns:7 + 2 * ns]
        h, i, j = pl.program_id(0), pl.program_id(1), pl.program_id(2)
        if ns:
            local, sends, arrivals = _gather_copies(side, refs[3:3 + ns], refs[5 + ns:5 + 2 * ns], *refs[7 + 2 * ns:])

            @pl.when((h == 0) & (i == 0) & (j == 0))
            def _():
                for cp in local + sends:
                    cp.start()

        @pl.when(j == 0)
        def _():
            m_s[...] = jnp.full_like(m_s, -jnp.inf)
            acc[...] = jnp.zeros_like(acc)

        kv, vv = k_ref[...], v_ref[...]
        for r in range(0, tq, rq):
            rows = slice(r, r + rq)
            sc = _dot_nt(q_ref[rows, :], kv)
            m_prev = m_s[rows, :]
            m_new = jnp.maximum(m_prev, jnp.max(sc, axis=-1, keepdims=True))
            p = jnp.exp(sc - jnp.tile(m_new, (1, tk // LANES)))
            acc[rows, :] = jnp.exp(m_prev - m_new) * acc[rows, :] + _dot(_bf(p), vv)
            m_s[rows, :] = m_new

        @pl.when(j == nk - 1)
        def _():
            a = acc[...]
            l = a[:, V_ONES[0]:V_ONES[0] + 1]
            o_ref[...] = (a / l).astype(o_ref.dtype)
            lse_ref[...] = (m_s[...] + jnp.log(jnp.broadcast_to(l, (tq, LANES)))).T[0:1, :]

        if ns:
            @pl.when((h == MLA_H - 1) & (i == nq - 1) & (j == nk - 1))
            def _():
                for cp in arrivals:
                    cp.wait_recv()
                for cp in sends:
                    cp.wait_send()
                for cp in local:
                    cp.wait()

    qs = pl.BlockSpec((tq, LANES), lambda h, i, j: (i, h))
    ks = pl.BlockSpec((tk, LANES), lambda h, i, j: (j, h))
    outs = pl.pallas_call(
        body, name="mla_flash_fwd_gather" if ns else "mla_flash_fwd", grid=(MLA_H, nq, nk),
        in_specs=[qs, ks, ks] + [HBM_SPEC] * ns,
        out_specs=[qs, pl.BlockSpec((None, 1, tq), lambda h, i, j: (h, 0, i))] + [HBM_SPEC] * ns,
        out_shape=[jax.ShapeDtypeStruct((s, MLA_H * LANES), BF16), jax.ShapeDtypeStruct((MLA_H, 1, s), F32)]
        + _gather_shapes(side),
        scratch_shapes=[pltpu.VMEM((tq, LANES), F32), pltpu.VMEM((tq, LANES), F32)]
        + ([_sems(3 * ns), _sems(3 * ns), _sems(ns)] if ns else []),
        compiler_params=_cp(("arbitrary",) * 3 if ns else ("parallel", "parallel", "arbitrary")),
    )(q, k, v, *[a for a, _, _ in side])
    return outs[0], outs[1], list(outs[2:])


def _attn_bwd_prep(dar, o, *, s, ts):
    w = MLA_H * LANES

    def fn(dov, ov):
        outs = []
        lane = lax.broadcasted_iota(jnp.int32, (dov.shape[0], LANES), 1)
        for h in range(MLA_H):
            sl = slice(h * LANES, (h + 1) * LANES)
            d = dov[:, sl]
            delta = _lanesum(d * ov[:, sl].astype(F32))
            hi = _bf(delta).astype(F32)
            outs.append(jnp.where(lane == V_ONES[0], -hi, jnp.where(lane == V_ONES[1], hi - delta, d)))
        return jnp.concatenate(outs, axis=1)

    return _ew(fn, [_cols(dar, w, 0, ts), _cols(o, w, 0, ts)], [], [(w, BF16)], s=s, ts=ts,
               name="mla_attn_bwd_prep")[0]


def _flash_bwd(q, k, v, do, lse, *, tq, tk):
    s = q.shape[0]
    nq, nk = s // tq, s // tk

    def body(q_ref, k_ref, v_ref, do_ref, lse_ref, dq_ref, dk_ref, dv_ref, dk_acc, dv_acc):
        j = pl.program_id(1)
        i = pl.program_id(2)
        qv, kv, vv, dov = q_ref[...], k_ref[...], v_ref[...], do_ref[...]
        pt = jnp.exp(_dot_nt(kv, qv) - lse_ref[...])
        dst = _bf(pt * _dot_nt(vv, dov))
        dv_c = _dot(_bf(pt), dov)
        dk_c = _dot(dst, qv)
        dq_c = _dot_tn(dst, kv)
        rows = pl.ds(pl.multiple_of(i * tq, tq), tq)

        @pl.when(i == 0)
        def _():
            dk_acc[...] = dk_c
            dv_acc[...] = dv_c

        @pl.when(i > 0)
        def _():
            dk_acc[...] += dk_c
            dv_acc[...] += dv_c

        @pl.when(j == 0)
        def _():
            dq_ref[rows, :] = dq_c

        @pl.when(j > 0)
        def _():
            dq_ref[rows, :] += dq_c

        @pl.when(i == nq - 1)
        def _():
            dk_ref[...] = dk_acc[...]
            dv_ref[...] = dv_acc[...].astype(dv_ref.dtype)

    qs = pl.BlockSpec((tq, LANES), lambda h, j, i: (i, h))
    ks = pl.BlockSpec((tk, LANES), lambda h, j, i: (j, h))
    st = pl.BlockSpec((None, 1, tq), lambda h, j, i: (h, 0, i))
    return pl.pallas_call(
        body, name="mla_flash_bwd", grid=(MLA_H, nk, nq),
        in_specs=[qs, ks, ks, qs, st],
        out_specs=[pl.BlockSpec((s, LANES), lambda h, j, i: (0, h)), ks, ks],
        out_shape=[jax.ShapeDtypeStruct((s, MLA_H * LANES), F32), jax.ShapeDtypeStruct((s, MLA_H * LANES), F32),
                   jax.ShapeDtypeStruct((s, MLA_H * LANES), BF16)],
        scratch_shapes=[pltpu.VMEM((tk, LANES), F32), pltpu.VMEM((tk, LANES), F32)],
        compiler_params=_cp(("parallel", "arbitrary", "arbitrary")),
    )(q, k, v, do, lse)


def _ret_geometry(d, c):
    df = float(d)
    ii = lax.broadcasted_iota(jnp.int32, (c, c), 0).astype(F32)
    jj = lax.broadcasted_iota(jnp.int32, (c, c), 1).astype(F32)
    rel = (ii - jj) * (1.0 - 2.0 * df)
    mask = rel >= df
    rel0 = jnp.maximum(rel, 0.0)
    pos = lax.broadcasted_iota(jnp.int32, (c, 1), 0).astype(F32)
    ez = (c - 1 - pos) + df * (2.0 * pos - (c - 1))
    ex = (pos + 1.0) + df * (c - 1 - 2.0 * pos)
    return mask, rel0, ez, ex


def _chunk_index(n_chunks):
    return lambda d, n: n + d * (n_chunks - 1 - 2 * n)


def _ret_fwd(p_even, cos_r, sin_r, theta_l):
    s = p_even.shape[0]
    c = RET_C
    n_chunks = s // c
    w = RET_H * LANES
    cidx = _chunk_index(n_chunks)

    def body(*refs):
        n = pl.program_id(0)

        @pl.when(n == 0)
        def _():
            for r_s in refs[16:18]:
                r_s[...] = jnp.zeros_like(r_s)

        stores = []
        for d in range(2):
            stores += one(d, *refs[6 * d:6 * d + 6], *refs[12 + 2 * d:14 + 2 * d], refs[16 + d])
        for ref, val in stores:
            ref[...] = val

    def one(d, q_ref, k_ref, v_ref, cos_ref, sin_ref, th_ref, o_ref, rp_ref, r_s):
        lg = jnp.log1p(-jnp.exp(-th_ref[...] * LN2))
        mask, rel0, ez, ex = _ret_geometry(d, c)
        cs, sn = cos_ref[...], sin_ref[...]
        r_all = r_s[...]
        outs, states = [], []
        for h in range(RET_H):
            sl = slice(h * LANES, (h + 1) * LANES)
            lgh = lg[:, h * LANES:h * LANES + 1]
            dm = jnp.where(mask, jnp.exp(lgh * rel0), 0.0)
            qh = _bf(_rope(q_ref[:, sl], cs, sn, RET_DK, 0))
            kf = _rope(k_ref[:, sl], cs, sn, RET_DK, 0) * (RET_DK ** -0.5)
            kh = _bf(kf)
            vh = _bf(v_ref[:, sl])
            rh = r_all[sl, :]
            a = _dot_nt(qh, kh) * dm
            outs.append(_dot(_bf(a), vh) + jnp.exp(lgh * ex) * _dot(qh, _bf(rh)))
            zk = _bf(kf * jnp.exp(lgh * ez))
            states.append(jnp.exp(lgh * c) * rh + _dot_tn(zk, vh))
        return [(rp_ref, r_all), (o_ref, jnp.concatenate(outs, axis=1)), (r_s, jnp.concatenate(states, axis=0))]

    def ins(d):
        col = lambda blk: pl.BlockSpec((c, w), lambda n: (cidx(d, n), blk))
        tab = pl.BlockSpec((c, LANES), lambda n: (cidx(d, n), 0))
        return [col(0), col(1), col(2), tab, tab, pl.BlockSpec((None, 1, w), lambda n: (d, 0, 0))]

    def outs(d):
        return [pl.BlockSpec((c, w), lambda n: (cidx(d, n), 0)),
                pl.BlockSpec((None, w, LANES), lambda n: (cidx(d, n), 0, 0))]

    o_f, r_f, o_b, r_b = pl.pallas_call(
        body, name="ret_fwd", grid=(n_chunks,),
        in_specs=ins(0) + ins(1), out_specs=outs(0) + outs(1),
        out_shape=[jax.ShapeDtypeStruct((s, w), F32), jax.ShapeDtypeStruct((n_chunks, w, LANES), F32)] * 2,
        scratch_shapes=[pltpu.VMEM((w, LANES), F32)] * 2,
        compiler_params=_cp(("arbitrary",)),
    )(*[p_even, p_even, p_even, cos_r, sin_r, theta_l] * 2)
    return (o_f, o_b), (r_f, r_b)


def _ret_bwd(p_even, cos_r, sin_r, theta_l, theta_h, r_prev, do):
    s = p_even.shape[0]
    c = RET_C
    n_chunks = s // c
    w = RET_H * LANES
    fwd_idx = _chunk_index(n_chunks)

    def cidx(d, n):
        return fwd_idx(d, n_chunks - 1 - n)

    def body(*refs):
        n = pl.program_id(0)

        @pl.when(n == 0)
        def _():
            for d in range(2):
                refs[26 + d][...] = jnp.zeros_like(refs[26 + d])
                refs[21 + 4 * d][...] = jnp.zeros_like(refs[21 + 4 * d])

        stores = []
        for d in range(2):
            stores += one(d, *refs[9 * d:9 * d + 9], *refs[18 + 4 * d:22 + 4 * d], refs[26 + d])
        for ref, val, accumulate in stores:
            if accumulate:
                ref[...] += val
            else:
                ref[...] = val

    def one(d, q_ref, k_ref, v_ref, cos_ref, sin_ref, th_ref, thh_ref, rp_ref, do_ref,
            dq_ref, dk_ref, dv_ref, dth_ref, dr_s):
        lg = jnp.log1p(-jnp.exp(-th_ref[...] * LN2))
        mask, rel0, ez, ex = _ret_geometry(d, c)
        cs, sn = cos_ref[...], sin_ref[...]
        rp_all, dr_all = rp_ref[...], dr_s[...]
        row = lax.broadcasted_iota(jnp.int32, (RET_H, LANES), 0)
        dlg = jnp.zeros((RET_H, LANES), F32)
        kscale = RET_DK ** -0.5
        dqs, dks, dvs, drs = [], [], [], []
        for h in range(RET_H):
            sl = slice(h * LANES, (h + 1) * LANES)
            lgh = lg[:, h * LANES:h * LANES + 1]
            dm = jnp.where(mask, jnp.exp(lgh * rel0), 0.0)
            zeta = jnp.exp(lgh * ez)
            xi = jnp.exp(lgh * ex)
            gc = jnp.exp(lgh * c)
            qf = _rope(q_ref[:, sl], cs, sn, RET_DK, 0)
            qh = _bf(qf)
            kf = _rope(k_ref[:, sl], cs, sn, RET_DK, 0) * kscale
            kh = _bf(kf)
            zkf = kf * zeta
            zk = _bf(zkf)
            vh = _bf(v_ref[:, sl])
            dof = do_ref[:, sl]
            doh = _bf(dof)
            rp = rp_all[sl, :]
            rpb = _bf(rp)
            drn = dr_all[sl, :]
            drb = _bf(drn)
            a = _dot_nt(qh, kh) * dm
            da0 = _dot_nt(doh, vh)
            da = _bf(da0 * dm)
            vdr = _dot_nt(vh, drb)
            dq_r = _dot(da, kh) + xi * _dot_nt(doh, rpb)
            dk_r = _dot_tn(da, qh) + zeta * vdr
            dvs.append(_dot_tn(_bf(a), doh) + _dot(zk, drb))
            dqs.append(_rope(dq_r, cs, -sn, RET_DK, 0))
            dks.append(_rope(dk_r * kscale, cs, -sn, RET_DK, 0))
            drs.append(_dot_tn(_bf(qf * xi), doh) + gc * drn)
            ocross = xi * _dot(qh, rpb)
            t = (jnp.sum(rel0 * a * da0, keepdims=True)
                 + jnp.sum(ex * dof * ocross, keepdims=True)
                 + c * gc * jnp.sum(drn * rp, keepdims=True)
                 + jnp.sum(ez * zkf * vdr, keepdims=True))
            dlg = jnp.where(row == h, t, dlg)
        x2 = jnp.exp(-thh_ref[...] * LN2)
        return [(dq_ref, jnp.concatenate(dqs, axis=1), False), (dk_ref, jnp.concatenate(dks, axis=1), False),
                (dv_ref, jnp.concatenate(dvs, axis=1), False), (dr_s, jnp.concatenate(drs, axis=0), False),
                (dth_ref, dlg * (x2 * LN2 / (1.0 - x2)), True)]

    def ins(d):
        col = lambda blk: pl.BlockSpec((c, w), lambda n: (cidx(d, n), blk))
        tab = pl.BlockSpec((c, LANES), lambda n: (cidx(d, n), 0))
        return [col(0), col(1), col(2), tab, tab, pl.BlockSpec((None, 1, w), lambda n: (d, 0, 0)),
                pl.BlockSpec((None, RET_H, LANES), lambda n: (d, 0, 0)),
                pl.BlockSpec((None, w, LANES), lambda n: (cidx(d, n), 0, 0)), col(0)]

    def outs(d):
        row = pl.BlockSpec((c, w), lambda n: (cidx(d, n), 0))
        return [row, row, row, pl.BlockSpec((RET_H, LANES), lambda n: (0, 0))]

    res = pl.pallas_call(
        body, name="ret_bwd", grid=(n_chunks,),
        in_specs=ins(0) + ins(1), out_specs=outs(0) + outs(1),
        out_shape=([jax.ShapeDtypeStruct((s, w), F32)] * 3 + [jax.ShapeDtypeStruct((RET_H, LANES), F32)]) * 2,
        scratch_shapes=[pltpu.VMEM((w, LANES), F32)] * 2,
        compiler_params=_cp(("arbitrary",)),
    )(*[a for d in range(2) for a in (p_even, p_even, p_even, cos_r, sin_r, theta_l, theta_h, r_prev[d], do)])
    return (res[0], res[4]), (res[1], res[5]), (res[2], res[6]), jnp.stack([res[3], res[7]])


def _post_fwd(o2, gate_row, gain, *, group, n, s, ts, name):
    w = o2[0].shape[1]

    def fn(of, ob, g, gv):
        y = _gn(of + ob, gv, group, n)[0]
        return g * _sigmoid(g) * y

    return _ew(fn, [_lead(o2, 0, ts), _lead(o2, 1, ts), gate_row], [gain], [(w, BF16)], s=s, ts=ts, name=name)[0]


def _post_bwd(o2, gate_row, gain, dr_row, *, group, n, s, ts, name):
    w = o2[0].shape[1]

    def fn(of, ob, g, dr, gv):
        y, xn, rstd = _gn(of + ob, gv, group, n)
        sg = _sigmoid(g)
        dy = dr * (g * sg)
        dgate = dr * y * (sg * (1.0 + g * (1.0 - sg)))
        do, dgain = _gn_bwd(dy, xn, rstd, gv, group, n)
        return do, dgate, dgain

    return _ew(fn, [_lead(o2, 0, ts), _lead(o2, 1, ts), gate_row, dr_row], [gain],
               [(w, F32), (w, BF16)], [(1, w)], s=s, ts=ts, name=name)


def _sum2(a2, *, s, ts, name):
    w = a2[0].shape[1]
    return _ew(lambda a, b: a + b, [_lead(a2, 0, ts), _lead(a2, 1, ts)], [], [(w, BF16)], s=s, ts=ts, name=name)[0]


def _gla_common(d, q_ref, k_ref, ga_ref, wg_ref, bg_ref):
    c = GLA_C
    df = float(d)
    ii = lax.broadcasted_iota(jnp.int32, (c, c), 0).astype(F32)
    jj = lax.broadcasted_iota(jnp.int32, (c, c), 1).astype(F32)
    rel = (ii - jj) * (1.0 - 2.0 * df)
    tri = _bf(jnp.where(rel >= 0.0, 1.0, 0.0))
    mask = rel >= df
    gab = _bf(ga_ref[...])
    z = _dot(gab, wg_ref[...]) + bg_ref[...]
    la = (jnp.minimum(z, 0.0) - jnp.log1p(jnp.exp(-jnp.abs(z)))) * (1.0 / GLA_TAU)
    l1, l2, l3 = _split3(la)
    b = _dot(tri, l1) + _dot(tri, l2) + _dot(tri, l3)
    first = d == 0
    bm = b[c // 2:c // 2 + 1] if first else b[c // 2 - 1:c // 2]
    bl = b[c - 1:c] if first else b[0:1]
    q = q_ref[...] * (GLA_DK ** -0.5)
    k = k_ref[...]
    e1, e2, e3, eb = jnp.exp(b - bm), jnp.exp(bm - b), jnp.exp(bl - b), jnp.exp(b)
    return dict(tri=tri, mask=mask, gab=gab, z=z, ebl=jnp.exp(bl), e1=e1, e2=e2, e3=e3, eb=eb,
                qc=q * e1, kc=k * e2, kd=k * e3, qe=q * eb, first=first)


def _col_scale(row_vec, width):
    t = jnp.broadcast_to(row_vec, (LANES, LANES)).T
    return jnp.concatenate([t] * (width // LANES), axis=1)


def _gla_fwd(p_odd, wg2, bg2):
    s = p_odd.shape[0]
    c = GLA_C
    n_chunks = s // c
    wk, wv = GLA_H * GLA_DK, GLA_H * GLA_DV
    cidx = _chunk_index(n_chunks)

    def body(*refs):
        n = pl.program_id(0)

        @pl.when(n == 0)
        def _():
            for s_s in refs[16:18]:
                s_s[...] = jnp.zeros_like(s_s)

        stores = []
        for d in range(2):
            stores += one(d, *refs[6 * d:6 * d + 6], *refs[12 + 2 * d:14 + 2 * d], refs[16 + d])
        for ref, val in stores:
            ref[...] = val

    def one(d, q_ref, k_ref, v_ref, ga_ref, wg_ref, bg_ref, o_ref, sp_ref, s_s):
        g = _gla_common(d, q_ref, k_ref, ga_ref, wg_ref, bg_ref)
        s_all = s_s[...]
        outs, states = [], []
        for h in range(GLA_H):
            sl = slice(h * GLA_DK, (h + 1) * GLA_DK)
            vs = slice(h * GLA_DV, (h + 1) * GLA_DV)
            vh = _bf(v_ref[:, vs])
            sh = s_all[sl, :]
            a = jnp.where(g["mask"], _dot_nt(_bf(g["qc"][:, sl]), _bf(g["kc"][:, sl])), 0.0)
            outs.append(_dot(_bf(a), vh) + _dot(_bf(g["qe"][:, sl]), _bf(sh)))
            states.append(_col_scale(g["ebl"][:, sl], GLA_DV) * sh + _dot_tn(_bf(g["kd"][:, sl]), vh))
        return [(sp_ref, s_all), (o_ref, jnp.concatenate(outs, axis=1)), (s_s, jnp.concatenate(states, axis=0))]

    def ins(d):
        col = lambda width, blk: pl.BlockSpec((c, width), lambda n: (cidx(d, n), blk))
        return [col(wk, 0), col(wk, 1), col(wv, 1), col(LANES, OD_GA_BLK),
                pl.BlockSpec((None, LANES, wk), lambda n: (d, 0, 0)), pl.BlockSpec((None, 1, wk), lambda n: (d, 0, 0))]

    def outs(d):
        return [pl.BlockSpec((c, wv), lambda n: (cidx(d, n), 0)),
                pl.BlockSpec((None, wk, GLA_DV), lambda n: (cidx(d, n), 0, 0))]

    o_f, s_f, o_b, s_b = pl.pallas_call(
        body, name="gla_fwd", grid=(n_chunks,),
        in_specs=ins(0) + ins(1), out_specs=outs(0) + outs(1),
        out_shape=[jax.ShapeDtypeStruct((s, wv), F32), jax.ShapeDtypeStruct((n_chunks, wk, GLA_DV), F32)] * 2,
        scratch_shapes=[pltpu.VMEM((wk, GLA_DV), F32)] * 2,
        compiler_params=_cp(("arbitrary",)),
    )(*[p_odd, p_odd, p_odd, p_odd, wg2, bg2] * 2)
    return (o_f, o_b), (s_f, s_b)


def _gla_bwd(p_odd, wg2, bg2, s_prev, do):
    s = p_odd.shape[0]
    c = GLA_C
    n_chunks = s // c
    wk, wv = GLA_H * GLA_DK, GLA_H * GLA_DV
    fwd_idx = _chunk_index(n_chunks)

    def cidx(d, n):
        return fwd_idx(d, n_chunks - 1 - n)

    def body(*refs):
        n = pl.program_id(0)

        @pl.when(n == 0)
        def _():
            for d in range(2):
                for r in (refs[28 + d], refs[20 + 6 * d], refs[21 + 6 * d]):
                    r[...] = jnp.zeros_like(r)

        stores = []
        for d in range(2):
            stores += one(d, *refs[8 * d:8 * d + 8], *refs[16 + 6 * d:22 + 6 * d], refs[28 + d])
        for ref, val, accumulate in stores:
            if accumulate:
                ref[...] += val
            else:
                ref[...] = val

    def one(d, q_ref, k_ref, v_ref, ga_ref, wg_ref, bg_ref, sp_ref, do_ref,
            dq_ref, dk_ref, dv_ref, dga_ref, dwg_ref, dbg_ref, ds_s):
        g = _gla_common(d, q_ref, k_ref, ga_ref, wg_ref, bg_ref)
        mask = g["mask"]
        ones8 = jnp.ones((8, GLA_DV), BF16)
        sp_all, ds_all = sp_ref[...], ds_s[...]
        dbs, dbms, dbls = [], [], []
        dqs, dks, dvs, dss = [], [], [], []
        for h in range(GLA_H):
            sl = slice(h * GLA_DK, (h + 1) * GLA_DK)
            vs = slice(h * GLA_DV, (h + 1) * GLA_DV)
            qc, kc, kd, qe = g["qc"][:, sl], g["kc"][:, sl], g["kd"][:, sl], g["qe"][:, sl]
            qcb, kcb, kdb, qeb = _bf(qc), _bf(kc), _bf(kd), _bf(qe)
            vh = _bf(v_ref[:, vs])
            doh = _bf(do_ref[:, vs])
            sp = sp_all[sl, :]
            dsn = ds_all[sl, :]
            dsb = _bf(dsn)
            a = _bf(jnp.where(mask, _dot_nt(qcb, kcb), 0.0))
            da = _bf(jnp.where(mask, _dot_nt(doh, vh), 0.0))
            dvs.append(_dot_tn(a, doh) + _dot(kdb, dsb))
            dqc = _dot(da, kcb)
            dkc = _dot_tn(da, qcb)
            dqe = _dot_nt(doh, _bf(sp))
            dkd = _dot_nt(vh, dsb)
            dss.append(_dot_tn(qeb, doh) + _col_scale(g["ebl"][:, sl], GLA_DV) * dsn)
            dqs.append((dqc * g["e1"][:, sl] + dqe * g["eb"][:, sl]) * (GLA_DK ** -0.5))
            dks.append(dkc * g["e2"][:, sl] + dkd * g["e3"][:, sl])
            t1, t2, t3, t4 = dqc * qc, dkc * kc, dqe * qe, dkd * kd
            dbs.append(t1 - t2 + t3 - t4)
            dbms.append(_rowsum(t2 - t1))
            m1, m2, _ = _split3(dsn * sp)
            rs = (_dot_nt(ones8, m1) + _dot_nt(ones8, m2))[0:1]
            dbls.append(_rowsum(t4) + g["ebl"][:, sl] * rs)
        db = jnp.concatenate(dbs, axis=1)
        dbm = jnp.concatenate(dbms, axis=1)
        dbl = jnp.concatenate(dbls, axis=1)
        row = lax.broadcasted_iota(jnp.int32, (c, wk), 0)
        mid = jnp.where(g["first"], c // 2, c // 2 - 1)
        last = jnp.where(g["first"], c - 1, 0)
        db = db + jnp.where(row == mid, dbm, 0.0) + jnp.where(row == last, dbl, 0.0)
        d1, d2, d3 = _split3(db)
        tri = g["tri"]
        dla = _dot_tn(tri, d1) + _dot_tn(tri, d2) + _dot_tn(tri, d3)
        dz = dla * (1.0 / GLA_TAU) * (1.0 - _sigmoid(g["z"]))
        dzb = _bf(dz)
        return [(dq_ref, jnp.concatenate(dqs, axis=1), False), (dk_ref, jnp.concatenate(dks, axis=1), False),
                (dv_ref, jnp.concatenate(dvs, axis=1), False), (ds_s, jnp.concatenate(dss, axis=0), False),
                (dga_ref, _dot_nt(dzb, wg_ref[...]), False), (dwg_ref, _dot_tn(g["gab"], dzb), True),
                (dbg_ref, _rowsum(dz), True)]

    def ins(d):
        col = lambda width, blk: pl.BlockSpec((c, width), lambda n: (cidx(d, n), blk))
        return [col(wk, 0), col(wk, 1), col(wv, 1), col(LANES, OD_GA_BLK),
                pl.BlockSpec((None, LANES, wk), lambda n: (d, 0, 0)), pl.BlockSpec((None, 1, wk), lambda n: (d, 0, 0)),
                pl.BlockSpec((None, wk, GLA_DV), lambda n: (cidx(d, n), 0, 0)), col(wv, 0)]

    def outs(d):
        row = lambda width: pl.BlockSpec((c, width), lambda n: (cidx(d, n), 0))
        return [row(wk), row(wk), row(wv), row(LANES),
                pl.BlockSpec((LANES, wk), lambda n: (0, 0)), pl.BlockSpec((1, wk), lambda n: (0, 0))]

    shapes = [jax.ShapeDtypeStruct((s, wk), F32), jax.ShapeDtypeStruct((s, wk), F32), jax.ShapeDtypeStruct((s, wv), F32),
              jax.ShapeDtypeStruct((s, LANES), F32), jax.ShapeDtypeStruct((LANES, wk), F32),
              jax.ShapeDtypeStruct((1, wk), F32)]
    res = pl.pallas_call(
        body, name="gla_bwd", grid=(n_chunks,),
        in_specs=ins(0) + ins(1), out_specs=outs(0) + outs(1), out_shape=shapes * 2,
        scratch_shapes=[pltpu.VMEM((wk, GLA_DV), F32)] * 2,
        compiler_params=_cp(("arbitrary",)),
    )(*[a for d in range(2) for a in (p_odd, p_odd, p_odd, p_odd, wg2, bg2, s_prev[d], do)])
    pair = lambda k: (res[k], res[6 + k])
    return pair(0), pair(1), pair(2), pair(3), jnp.stack(pair(4)), jnp.stack(pair(5))


HALO = 8


def _halo_specs(width_blk, col0, ts, s):
    r = ts // HALO
    last = s // HALO - 1
    cur = pl.BlockSpec((ts, width_blk), lambda j, i: (i, col0 + j))
    prev = pl.BlockSpec((HALO, width_blk), lambda j, i: (jnp.maximum(i * r - 1, 0), col0 + j))
    nxt = pl.BlockSpec((HALO, width_blk), lambda j, i: (jnp.minimum((i + 1) * r, last), col0 + j))
    return [prev, cur, nxt]


def _with_halo(prev_ref, cur_ref, next_ref, i, n_i):
    p = jnp.where(i == 0, 0.0, prev_ref[...])
    q = jnp.where(i == n_i - 1, 0.0, next_ref[...])
    return jnp.concatenate([p, cur_ref[...], q], axis=0)


def _shift_down(x):
    return pltpu.roll(x, 1, 0)


def _shift_up(x):
    return pltpu.roll(x, x.shape[0] - 1, 0)


def _ffn_act(up, conv_w, conv_b, *, ts):
    s = up.shape[0]
    tc = _tile(D_FF, 1408)
    nj = D_FF // tc
    n_i = s // ts

    def body(gp, gc, gn, val_ref, w_ref, b_ref, a_ref):
        i = pl.program_id(1)
        g = _with_halo(gp, gc, gn, i, n_i)
        w = w_ref[...]
        conv = w[0:1] * _shift_down(g) + w[1:2] * g + w[2:3] * _shift_up(g) + b_ref[...]
        conv = conv[HALO:HALO + ts]
        a_ref[...] = (conv * _sigmoid(conv) * val_ref[...]).astype(a_ref.dtype)

    return pl.pallas_call(
        body, name="ffn_act", grid=(nj, n_i),
        in_specs=_halo_specs(tc, 0, ts, s) + [pl.BlockSpec((ts, tc), lambda j, i: (i, nj + j)),
                                              pl.BlockSpec((3, tc), lambda j, i: (0, j)),
                                              pl.BlockSpec((1, tc), lambda j, i: (0, j))],
        out_specs=pl.BlockSpec((ts, tc), lambda j, i: (i, j)),
        out_shape=jax.ShapeDtypeStruct((s, D_FF), BF16),
        compiler_params=_cp(("parallel", "arbitrary")),
    )(up, up, up, up, conv_w, conv_b)


def _ffn_act_bwd(up, da, conv_w, conv_b, *, ts):
    s = up.shape[0]
    tc = _tile(D_FF, 1408)
    nj = D_FF // tc
    n_i = s // ts

    def body(gp, gc, gn, vp, vc, vn, dp, dc, dn, w_ref, b_ref, dup_ref, dw_ref, db_ref):
        i = pl.program_id(1)
        g = _with_halo(gp, gc, gn, i, n_i)
        v = _with_halo(vp, vc, vn, i, n_i)
        dav = _with_halo(dp, dc, dn, i, n_i)
        w = w_ref[...]
        gm, gpl = _shift_down(g), _shift_up(g)
        conv = w[0:1] * gm + w[1:2] * g + w[2:3] * gpl + b_ref[...]
        sg = _sigmoid(conv)
        dgc = dav * v * (sg * (1.0 + conv * (1.0 - sg)))
        dgate = w[0:1] * _shift_up(dgc) + w[1:2] * dgc + w[2:3] * _shift_down(dgc)
        ctr = slice(HALO, HALO + ts)
        dup_ref[0] = dgate[ctr].astype(dup_ref.dtype)
        dup_ref[1] = (dav[ctr] * (conv * sg)[ctr]).astype(dup_ref.dtype)
        dgc_c = dgc[ctr]
        dw = jnp.concatenate([_rowsum(dgc_c * gm[ctr]), _rowsum(dgc_c * g[ctr]), _rowsum(dgc_c * gpl[ctr])], axis=0)
        dbv = _rowsum(dgc_c)

        @pl.when(i == 0)
        def _():
            dw_ref[...] = dw
            db_ref[...] = dbv

        @pl.when(i > 0)
        def _():
            dw_ref[...] += dw
            db_ref[...] += dbv

    return pl.pallas_call(
        body, name="ffn_act_bwd", grid=(nj, n_i),
        in_specs=(_halo_specs(tc, 0, ts, s) + _halo_specs(tc, nj, ts, s) + _halo_specs(tc, 0, ts, s)
                  + [pl.BlockSpec((3, tc), lambda j, i: (0, j)), pl.BlockSpec((1, tc), lambda j, i: (0, j))]),
        out_specs=[pl.BlockSpec((2, ts, tc), lambda j, i: (0, i, j)),
                   pl.BlockSpec((3, tc), lambda j, i: (0, j)), pl.BlockSpec((1, tc), lambda j, i: (0, j))],
        out_shape=[jax.ShapeDtypeStruct((2, s, D_FF), BF16),
                   jax.ShapeDtypeStruct((3, D_FF), F32), jax.ShapeDtypeStruct((1, D_FF), F32)],
        compiler_params=_cp(("parallel", "arbitrary")),
    )(up, up, up, up, up, up, da, da, da, conv_w, conv_b)


def _loss_head(y, target, *, s, ts):
    def fn(yv, tv):
        err = yv - tv
        return err * (1.0 / D_MODEL), _rowsum(err * err)

    return _ew(fn, [_cols(y, D_MODEL, 0, ts), _cols(target, D_MODEL, 0, ts)], [], [(D_MODEL, F32)],
               [(1, D_MODEL)], s=s, ts=ts, name="loss_head")


def _rows_tile(r, width):
    ts = r
    while ts * width * 4 > (1 << 20) and ts % 16 == 0:
        ts //= 2
    return ts


def _adamw(w, g, m, v, *, ts, name):
    r, width = w.shape
    assert r % ts == 0

    def fn(wv, gv, mv, vv):
        mn = ADAM_B1 * mv + (1.0 - ADAM_B1) * gv
        vn = ADAM_B2 * vv + (1.0 - ADAM_B2) * (gv * gv)
        m_hat = mn / (1.0 - ADAM_B1 ** ADAM_STEP)
        v_hat = vn / (1.0 - ADAM_B2 ** ADAM_STEP)
        delta = -ADAM_LR * (m_hat / (jnp.sqrt(v_hat) + ADAM_EPS) + ADAM_WD * wv)
        return delta, mn, vn

    rows = [_cols(a, width, 0, ts) for a in (w, g, m, v)]
    return _ew(fn, rows, [], [(width, F32)] * 3, s=r, ts=ts, name=name)


def _pad_heads(w, heads, real):
    lead = w.shape[:-1]
    w = w.reshape(lead + (heads, real))
    w = jnp.pad(w, [(0, 0)] * len(lead) + [(0, 0), (0, LANES - real)])
    return w.reshape(lead + (heads * LANES,))


def _pad_head_rows(w, heads, real):
    return _pad_heads(w.T, heads, real).T


def _pack_even(p):
    w_in = p["w_in"]
    z = lambda n: jnp.zeros((D_MODEL, n), w_in.dtype)
    o = 0
    parts = {}
    for nm, n in (("cq", MLA_QR), ("ckv", MLA_KVR), ("kr", MLA_ROPE), ("rq", 512), ("rk", 512), ("rv", 512), ("rg", 512)):
        parts[nm] = w_in[:, o:o + n]
        o += n
    w_in_p = jnp.concatenate(
        [_pad_heads(parts[k], RET_H, RET_DK) for k in ("rq", "rk", "rv", "rg")]
        + [parts["cq"], z(EV_CQ - MLA_QR), parts["ckv"], z(MLA_NOPE), parts["kr"], z(LANES - MLA_QK), z(LANES)], axis=1)
    w_uq = jnp.pad(_pad_heads(p["w_uq"], MLA_H, MLA_QK), ((0, EV_CQ - MLA_QR), (0, 0)))
    ukv = p["w_ukv"].reshape(MLA_KVR, MLA_H, MLA_NOPE + MLA_V)
    w_ukv = jnp.concatenate([_pad_heads(ukv[..., :MLA_NOPE].reshape(MLA_KVR, -1), MLA_H, MLA_NOPE),
                             _pad_heads(ukv[..., MLA_NOPE:].reshape(MLA_KVR, -1), MLA_H, MLA_V)], axis=1)
    w_out = jnp.concatenate([_pad_head_rows(p["w_out"][:MLA_H * MLA_V], MLA_H, MLA_V),
                             _pad_head_rows(p["w_out"][MLA_H * MLA_V:], RET_H, RET_DV)], axis=0)
    return dict(
        w_in=w_in_p, w_uq=w_uq, w_ukv=w_ukv, w_out=w_out,
        mix_g=p["mix_norm"][None, :],
        q_norm=jnp.pad(p["q_norm"], (0, EV_CQ - MLA_QR))[None, :],
        kv_norm=p["kv_norm"][None, :],
        qhn=jnp.pad(p["q_head_norm"], (0, LANES - MLA_QK))[None, :],
        khn=jnp.pad(p["k_head_norm"], (0, LANES - MLA_QK))[None, :],
        ret_gain=_pad_heads(p["ret_out_norm"].reshape(-1), RET_H, RET_DV)[None, :],
    )


def _pack_odd(p):
    w_in = p["w_in"]
    ga = w_in[:, 3072:]
    w_in_p = jnp.concatenate([w_in[:, :3072], ga, jnp.zeros((D_MODEL, LANES - 2 * GLA_R), w_in.dtype)], axis=1)
    wk = GLA_H * GLA_DK
    zf = jnp.zeros((LANES - GLA_R, wk), p["w_gate_fwd"].dtype)
    zb0 = jnp.zeros((GLA_R, wk), p["w_gate_fwd"].dtype)
    zb1 = jnp.zeros((LANES - 2 * GLA_R, wk), p["w_gate_fwd"].dtype)
    wg2 = jnp.stack([jnp.concatenate([p["w_gate_fwd"], zf], axis=0),
                     jnp.concatenate([zb0, p["w_gate_bwd"], zb1], axis=0)])
    bg2 = jnp.stack([p["b_gate_fwd"][None, :], p["b_gate_bwd"][None, :]])
    return dict(w_in=w_in_p, wg2=wg2, bg2=bg2, w_out=p["w_out"], mix_g=p["mix_norm"][None, :],
                gla_gain=p["gla_out_norm"].reshape(1, -1))


_MATRICES = ("w_in", "w_uq", "w_ukv", "w_out", "wg2")


def _packed(pack_fn, p):
    packed = pack_fn(p)
    packed = {k: (_bf(v) if k in _MATRICES else v.astype(F32)) for k, v in packed.items()}
    shapes = {k: jax.ShapeDtypeStruct(v.shape, F32) for k, v in p.items()}
    unpack = jax.linear_transpose(pack_fn, shapes)
    return packed, lambda g: unpack(g)[0]


def _ffn_fwd(x, w, *, s, ts):
    h = _rmsnorm(_cols(x, D_MODEL, 0, ts), w["norm_g"], n=D_MODEL, s=s, ts=ts, name="ffn_norm")
    up = _mm(h, w["w_up4"], b_layer=w["layer"], name="ffn_up")
    a = _ffn_act(up, w["conv_w"], w["conv_b"], ts=min(ts, FFN_ACT_ROWS))
    y = _mm(a, w["w_down"], res=x, name="ffn_down")
    return y, dict(x=x, h=h, up=up, a=a)


def _ffn_bwd(dy, w, sv, *, s, ts):
    da = _mm(dy, w["w_down"], tb=True, name="ffn_down_dx")
    g_down = _mm(sv["a"], dy, ta=True, name="ffn_down_dw")
    dup, g_cw, g_cb = _ffn_act_bwd(sv["up"], da, w["conv_w"], w["conv_b"], ts=min(ts, FFN_ACT_ROWS))
    dh = _mm(dup, w["w_up4"], tb=True, b_layer=w["layer"], halves="a", name="ffn_up_dx")
    g_up = _mm(sv["h"], dup, ta=True, out_chips=True, halves="b", name="ffn_up_dw")
    dx, g_norm = _rmsnorm_bwd(_cols(sv["x"], D_MODEL, 0, ts), w["norm_g"], dh, dy, n=D_MODEL, s=s, ts=ts,
                              name="ffn_norm_bwd")
    return dx, dict(w_up=g_up, w_down=g_down, conv_w=g_cw, conv_b=g_cb, norm_g=g_norm)


def _even_fwd(x, w, tabs, *, s, ts, side=()):
    cos_m, sin_m, cos_r, sin_r = tabs
    h = _rmsnorm(_cols(x, D_MODEL, 0, ts), w["mix_g"], n=D_MODEL, s=s, ts=ts, name="mix_norm")
    p = _mm(h, w["w_in"], name="even_in")
    cqn = _rmsnorm(_cols(p, EV_CQ, EV_RET // EV_CQ, ts), w["q_norm"], n=MLA_QR, s=s, ts=ts, name="mla_q_norm")
    ckvn = _rmsnorm(_cols(p, MLA_KVR, (EV_RET + EV_CQ) // MLA_KVR, ts), w["kv_norm"], n=MLA_KVR, s=s, ts=ts,
                    name="mla_kv_norm")
    q_pre = _mm(cqn, w["w_uq"], name="mla_uq")
    kv_pre = _mm(ckvn, w["w_ukv"], name="mla_ukv")
    q, k, v = _mla_prep(q_pre, kv_pre, p, cos_m, sin_m, w["qhn"], w["khn"], s=s, ts=ts)
    o, lse, gathered = _flash_fwd(q, k, v, tq=min(s, FLASH_FWD_ROWS), tk=min(s, FLASH_KEYS), side=side)
    o2, r_prev = _ret_fwd(p, cos_r, sin_r, w["theta_l"])
    r = _post_fwd(o2, _cols(p, RET_H * LANES, 3, ts), w["ret_gain"], group=LANES, n=RET_DV, s=s, ts=ts,
                  name="ret_post")
    ar = jnp.concatenate([o, r], axis=1)
    y = _mm(ar, w["w_out"], res=x, name="even_out")
    return y, dict(x=x, h=h, p=p, cqn=cqn, ckvn=ckvn, q_pre=q_pre, kv_pre=kv_pre, q=q, k=k, v=v, o=o, lse=lse,
                   o2=o2, r_prev=r_prev, ar=ar), gathered


def _even_bwd(dy, w, sv, tabs, *, s, ts):
    cos_m, sin_m, cos_r, sin_r = tabs
    p = sv["p"]
    wh = MLA_H * LANES
    dar = _mm(dy, w["w_out"], tb=True, name="even_out_dx")
    g_out = _mm(sv["ar"], dy, ta=True, name="even_out_dw")
    do_attn = _attn_bwd_prep(dar, sv["o"], s=s, ts=ts)
    dq, dk, dv = _flash_bwd(sv["q"], sv["k"], sv["v"], do_attn, sv["lse"], tq=min(s, FLASH_BWD_ROWS),
                            tk=min(s, FLASH_KEYS))
    dq_pre, dk_pre, dkr, g_qhn, g_khn = _mla_prep_bwd(sv["q_pre"], sv["kv_pre"], p, cos_m, sin_m, w["qhn"], w["khn"],
                                                      dq, dk, s=s, ts=ts)
    dkv_pre = jnp.concatenate([dk_pre, dv], axis=1)
    dckvn = _mm(dkv_pre, w["w_ukv"], tb=True, name="mla_ukv_dx")
    g_ukv = _mm(sv["ckvn"], dkv_pre, ta=True, name="mla_ukv_dw")
    dcqn = _mm(dq_pre, w["w_uq"], tb=True, name="mla_uq_dx")
    g_uq = _mm(sv["cqn"], dq_pre, ta=True, name="mla_uq_dw")
    dckv, g_kvn = _rmsnorm_bwd(_cols(p, MLA_KVR, (EV_RET + EV_CQ) // MLA_KVR, ts), w["kv_norm"], dckvn, None,
                               n=MLA_KVR, s=s, ts=ts, name="mla_kv_norm_bwd")
    dcq, g_qn = _rmsnorm_bwd(_cols(p, EV_CQ, EV_RET // EV_CQ, ts), w["q_norm"], dcqn, None, n=MLA_QR, s=s, ts=ts,
                             name="mla_q_norm_bwd")
    do, drg, g_gain = _post_bwd(sv["o2"], _cols(p, wh, 3, ts), w["ret_gain"], _cols(dar, wh, 1, ts),
                                group=LANES, n=RET_DV, s=s, ts=ts, name="ret_post_bwd")
    dq2, dk2, dv2, dth = _ret_bwd(p, cos_r, sin_r, w["theta_l"], w["theta_h"], sv["r_prev"], do)
    drq, drk, drv = (_sum2(a, s=s, ts=ts, name="sum_dirs_1024") for a in (dq2, dk2, dv2))
    dp = jnp.concatenate([drq, drk, drv, drg, _bf(dcq), _bf(dckv), dkr, jnp.zeros((s, LANES), BF16)], axis=1)
    dh = _mm(dp, w["w_in"], tb=True, name="even_in_dx")
    g_in = _mm(sv["h"], dp, ta=True, name="even_in_dw")
    dx, g_mix = _rmsnorm_bwd(_cols(sv["x"], D_MODEL, 0, ts), w["mix_g"], dh, dy, n=D_MODEL, s=s, ts=ts,
                             name="mix_norm_bwd")
    grads = dict(w_in=g_in, w_uq=g_uq, w_ukv=g_ukv, w_out=g_out, mix_g=g_mix, q_norm=g_qn, kv_norm=g_kvn,
                 qhn=g_qhn, khn=g_khn, ret_gain=g_gain)
    return dx, grads, dth[:, :, 0]


def _odd_fwd(x, w, *, s, ts):
    h = _rmsnorm(_cols(x, D_MODEL, 0, ts), w["mix_g"], n=D_MODEL, s=s, ts=ts, name="mix_norm")
    p = _mm(h, w["w_in"], name="odd_in")
    o2, s_prev = _gla_fwd(p, w["wg2"], w["bg2"])
    g = _post_fwd(o2, _cols(p, GLA_H * GLA_DV, 2, ts), w["gla_gain"], group=GLA_DV, n=GLA_DV, s=s, ts=ts,
                  name="gla_post")
    y = _mm(g, w["w_out"], res=x, name="odd_out")
    return y, dict(x=x, h=h, p=p, o2=o2, s_prev=s_prev, g=g)


def _odd_bwd(dy, w, sv, *, s, ts):
    p = sv["p"]
    wv = GLA_H * GLA_DV
    dg = _mm(dy, w["w_out"], tb=True, name="odd_out_dx")
    g_out = _mm(sv["g"], dy, ta=True, name="odd_out_dw")
    do, dgr, g_gain = _post_bwd(sv["o2"], _cols(p, wv, 2, ts), w["gla_gain"], _cols(dg, wv, 0, ts),
                                group=GLA_DV, n=GLA_DV, s=s, ts=ts, name="gla_post_bwd")
    dq2, dk2, dv2, dga2, g_wg, g_bg = _gla_bwd(p, w["wg2"], w["bg2"], sv["s_prev"], do)
    dq = _sum2(dq2, s=s, ts=ts, name="sum_dirs_512")
    dk = _sum2(dk2, s=s, ts=ts, name="sum_dirs_512")
    dv = _sum2(dv2, s=s, ts=ts, name="sum_dirs_1024")
    dga = _sum2(dga2, s=s, ts=ts, name="sum_dirs_128")
    dp = jnp.concatenate([dq, dk, dv, dgr, dga], axis=1)
    dh = _mm(dp, w["w_in"], tb=True, name="odd_in_dx")
    g_in = _mm(sv["h"], dp, ta=True, name="odd_in_dw")
    dx, g_mix = _rmsnorm_bwd(_cols(sv["x"], D_MODEL, 0, ts), w["mix_g"], dh, dy, n=D_MODEL, s=s, ts=ts,
                             name="mix_norm_bwd")
    return dx, dict(w_in=g_in, wg2=g_wg, bg2=g_bg, w_out=g_out, mix_g=g_mix, gla_gain=g_gain)


_EVEN_NAMES = dict(mix_norm="mix_norm_even", w_in="w_in_even", q_norm="mla_q_norm", kv_norm="mla_kv_norm",
                   w_uq="mla_w_uq", w_ukv="mla_w_ukv", q_head_norm="mla_q_head_norm", k_head_norm="mla_k_head_norm",
                   ret_out_norm="ret_out_norm", w_out="w_out_even")
_ODD_NAMES = dict(mix_norm="mix_norm_odd", w_in="w_in_odd", w_gate_fwd="gla_w_gate_fwd", b_gate_fwd="gla_b_gate_fwd",
                  w_gate_bwd="gla_w_gate_bwd", b_gate_bwd="gla_b_gate_bwd", gla_out_norm="gla_out_norm",
                  w_out="w_out_odd")

def _local_step(x, pos, target, full, side=(), finish=None):
    s = x.shape[0]
    ts = min(s, EW_ROWS)
    tabs = _rope_tables(pos, MLA_ROPE, MLA_NOPE) + _rope_tables(pos, RET_DK, 0)

    def layer_weights(layer):
        i = layer // 2
        names = _EVEN_NAMES if layer % 2 == 0 else _ODD_NAMES
        wm, unpack_m = _packed(_pack_even if layer % 2 == 0 else _pack_odd, {k: full[n][i] for k, n in names.items()})
        if layer % 2 == 0:
            th = jnp.stack([full["ret_theta_fwd"][i], full["ret_theta_bwd"][i]]).astype(F32)
            wm["theta_h"] = jnp.broadcast_to(th[:, :, None], (2, RET_H, LANES))
            wm["theta_l"] = wm["theta_h"].reshape(2, 1, RET_H * LANES)
        w_up4, index = full["ffn_w_up"][layer]
        wf = dict(layer=index, w_up4=w_up4, w_down=_bf(full["ffn_w_down"][layer]),
                  conv_w=full["ffn_conv_w"][layer].astype(F32), conv_b=full["ffn_conv_b"][layer][None, :].astype(F32),
                  norm_g=full["ffn_norm"][layer][None, :].astype(F32))
        return wm, unpack_m, wf

    layers, saved = [], []
    for layer in range(DEPTH):
        layers.append(layer_weights(layer))
        wm, _, wf = layers[-1]
        if layer % 2 == 0:
            x, sv_m, gathered = _even_fwd(x, wm, tabs, s=s, ts=ts, side=side if layer == 0 else ())
            if layer == 0 and finish is not None:
                full = finish(gathered)
        else:
            x, sv_m = _odd_fwd(x, wm, s=s, ts=ts)
        x, sv_f = _ffn_fwd(x, wf, s=s, ts=ts)
        saved.append((sv_m, sv_f))

    dy, sq = _loss_head(x, target, s=s, ts=ts)
    loss = 0.5 / D_MODEL * jnp.sum(sq)

    grads = {}

    def put(name, idx, g):
        grads.setdefault(name, {})[idx] = g

    for layer in reversed(range(DEPTH)):
        wm, unpack_m, wf = layers[layer]
        sv_m, sv_f = saved[layer]
        i = layer // 2
        dy, gf = _ffn_bwd(dy, wf, sv_f, s=s, ts=ts)
        put("ffn_w_up", layer, gf["w_up"])
        put("ffn_w_down", layer, gf["w_down"])
        put("ffn_conv_w", layer, gf["conv_w"])
        put("ffn_conv_b", layer, gf["conv_b"][0])
        put("ffn_norm", layer, gf["norm_g"][0])
        if layer % 2 == 0:
            dy, gm, dth = _even_bwd(dy, wm, sv_m, tabs, s=s, ts=ts)
            put("ret_theta_fwd", i, dth[0])
            put("ret_theta_bwd", i, dth[1])
            names = _EVEN_NAMES
        else:
            dy, gm = _odd_bwd(dy, wm, sv_m, s=s, ts=ts)
            names = _ODD_NAMES
        for k, g in unpack_m(gm).items():
            put(names[k], i, g)
    return loss, dy, {n: [g[j] for j in range(len(g))] for n, g in grads.items()}


HBM_SPEC = pl.BlockSpec(memory_space=pltpu.HBM)
VMEM_SPEC = pl.BlockSpec(memory_space=pltpu.VMEM)
CHIPS = 4
CORES = 2
ROW = 8 * LANES


def _xyc():
    return lax.axis_index("x"), lax.axis_index("y"), lax.axis_index("c")


def _other_chips(x, y):
    return [(1 - x, y), (x, 1 - y), (1 - x, 1 - y)]


def _remote(src, dst, send, recv, dev):
    return pltpu.make_async_remote_copy(src_ref=src, dst_ref=dst, send_sem=send, recv_sem=recv,
                                        device_id=dev, device_id_type=MESH)


def _sems(n):
    return pltpu.SemaphoreType.DMA((n,))


def _gather_copies(side, srcs, lands, send, recv, loc):
    n = len(side)
    x, y, c = _xyc()
    me = 2 * x + y
    local, sends, arrivals = [], [], []
    for t, (_, first, count) in enumerate(side):
        src = srcs[t].at[pl.ds(first, count)]
        local.append(pltpu.make_async_copy(src, lands[t].at[me], loc.at[t]))
        for j, (px, py) in enumerate(_other_chips(x, y)):
            k = n * j + t
            sends.append(_remote(src, lands[t].at[me], send.at[k], recv.at[k], (px, py, c)))
            arrivals.append(_remote(src, lands[t].at[2 * px + py], send.at[k], recv.at[k], (px, py, c)))
    return local, sends, arrivals


def _gather_shapes(side):
    return [jax.ShapeDtypeStruct((CHIPS, count) + a.shape[1:], a.dtype) for a, _, count in side]


def _gather_chips(side):
    n = len(side)

    def body(*refs):
        local, sends, arrivals = _gather_copies(side, refs[:n], refs[n:2 * n], *refs[2 * n:])
        for cp in local + sends:
            cp.start()
        for cp in arrivals:
            cp.wait_recv()
        for cp in sends:
            cp.wait_send()
        for cp in local:
            cp.wait()

    return pl.pallas_call(
        body, name="gather_chips", in_specs=[HBM_SPEC] * n, out_specs=[HBM_SPEC] * n,
        out_shape=_gather_shapes(side),
        scratch_shapes=[_sems(3 * n), _sems(3 * n), _sems(n)],
    )(*[a for a, _, _ in side])


def _half_rows(ref, axis, half, which):
    idx = (slice(None),) * axis + (pl.ds(pl.multiple_of(which * half, 8), half),)
    return ref.at[idx]


def _swap_halves(arrs):
    n = len(arrs)

    def body(*refs):
        ins, outs = refs[:n], refs[n:2 * n]
        send, recv = refs[2 * n:]
        x, y, c = _xyc()
        copies = []
        for t in range(n):
            half = arrs[t].shape[2] // CORES
            cp = _remote(_half_rows(ins[t], 2, half, 1 - c), outs[t], send.at[t], recv.at[t], (x, y, 1 - c))
            cp.start()
            copies.append(cp)
        for cp in copies:
            cp.wait()

    return pl.pallas_call(
        body, name="swap_halves", in_specs=[HBM_SPEC] * n, out_specs=[HBM_SPEC] * n,
        out_shape=[jax.ShapeDtypeStruct(a.shape[:2] + (a.shape[2] // CORES, a.shape[3]), a.dtype) for a in arrs],
        scratch_shapes=[_sems(n), _sems(n)],
    )(*arrs)


def _add_core_halves(a, got, core, *, ts, name):
    ch, nl, r, cols = a.shape
    half = r // CORES
    nb = half // ts

    def body(core_ref, a_ref, g_ref, o_ref):
        o_ref[...] = (a_ref[...] + g_ref[...]).astype(o_ref.dtype)

    rows = pl.BlockSpec((ts, cols), lambda g, i, cr: (g * nb + i, 0))
    return pl.pallas_call(
        body, name=name, out_shape=jax.ShapeDtypeStruct((ch * nl * half, cols), BF16),
        grid_spec=pltpu.PrefetchScalarGridSpec(
            num_scalar_prefetch=1, grid=(ch * nl, nb),
            in_specs=[pl.BlockSpec((ts, cols), lambda g, i, cr: (g * (r // ts) + cr[0] * nb + i, 0)), rows],
            out_specs=rows),
        compiler_params=_cp(("arbitrary", "arbitrary")),
    )(core, a.reshape(-1, cols), got.reshape(-1, cols)).reshape(got.shape)


def _add_chip_parts(parts, core, *, ts, name):
    ch, nl, half, cols = parts.shape
    nb = half // ts
    r = half * CORES

    def body(core_ref, *refs):
        acc = refs[0][...].astype(F32)
        for p in refs[1:ch]:
            acc = acc + p[...].astype(F32)
        refs[ch][...] = acc

    return pl.pallas_call(
        body, name=name, out_shape=jax.ShapeDtypeStruct((nl * r, cols), F32),
        grid_spec=pltpu.PrefetchScalarGridSpec(
            num_scalar_prefetch=1, grid=(nl, nb),
            in_specs=[pl.BlockSpec((ts, cols), lambda l, i, cr, j=j: ((j * nl + l) * nb + i, 0)) for j in range(ch)],
            out_specs=pl.BlockSpec((ts, cols), lambda l, i, cr: (l * (r // ts) + cr[0] * nb + i, 0))),
        compiler_params=_cp(("arbitrary", "arbitrary")),
    )(core, *[parts.reshape(-1, cols)] * ch).reshape(nl, r, cols)


def _scatter_chips(arrs):
    n = len(arrs)

    def body(*refs):
        ins, outs = refs[:n], refs[n:2 * n]
        send, recv, loc = refs[2 * n:]
        x, y, c = _xyc()
        me = 2 * x + y
        copies = []
        for t in range(n):
            cp = pltpu.make_async_copy(ins[t].at[me], outs[t].at[me], loc.at[t])
            cp.start()
            copies.append(cp)
        sends = []
        for j, (px, py) in enumerate(_other_chips(x, y)):
            for t in range(n):
                cp = _remote(ins[t].at[2 * px + py], outs[t].at[me], send.at[n * j + t], recv.at[n * j + t], (px, py, c))
                cp.start()
                sends.append(cp)
        for j, (px, py) in enumerate(_other_chips(x, y)):
            for t in range(n):
                _remote(ins[t].at[me], outs[t].at[2 * px + py], send.at[n * j + t], recv.at[n * j + t],
                        (px, py, c)).wait_recv()
        for cp in sends:
            cp.wait_send()
        for cp in copies:
            cp.wait()

    return pl.pallas_call(
        body, name="scatter_chips", in_specs=[HBM_SPEC] * n, out_specs=[HBM_SPEC] * n,
        out_shape=[jax.ShapeDtypeStruct(a.shape, a.dtype) for a in arrs],
        scratch_shapes=[_sems(3 * n), _sems(3 * n), _sems(n)],
    )(*arrs)


def _gather_cores(arrs):
    n = len(arrs)

    def body(*refs):
        ins, outs = refs[:n], refs[n:2 * n]
        send, recv = refs[2 * n:]
        x, y, c = _xyc()
        sends = []
        for t in range(n):
            half = arrs[t].shape[1] // CORES
            cp = _remote(_half_rows(ins[t], 1, half, c), _half_rows(outs[t], 1, half, c), send.at[t], recv.at[t],
                         (x, y, 1 - c))
            cp.start()
            sends.append(cp)
        for t in range(n):
            half = arrs[t].shape[1] // CORES
            _remote(_half_rows(ins[t], 1, half, 1 - c), _half_rows(outs[t], 1, half, 1 - c), send.at[t], recv.at[t],
                    (x, y, 1 - c)).wait_recv()
        for cp in sends:
            cp.wait_send()

    return pl.pallas_call(
        body, name="gather_cores", in_specs=[HBM_SPEC] * n, out_specs=[HBM_SPEC] * n,
        out_shape=[jax.ShapeDtypeStruct(a.shape, a.dtype) for a in arrs],
        input_output_aliases={t: t for t in range(n)},
        scratch_shapes=[_sems(n), _sems(n)],
    )(*arrs)


def _all_reduce_devices(v):
    n_dev = CHIPS * CORES

    def body(v_ref, o_ref, buf, send, recv):
        x, y, c = _xyc()
        me = 4 * x + 2 * y + c
        buf[pl.ds(me, 1)] = v_ref[...][None]
        sends = []
        for m in range(1, n_dev):
            px = 1 - x if m & 4 else x
            py = 1 - y if m & 2 else y
            pc = 1 - c if m & 1 else c
            cp = _remote(v_ref, buf.at[me], send.at[m - 1], recv.at[m - 1], (px, py, pc))
            cp.start()
            sends.append((cp, 4 * px + 2 * py + pc))
        for m, (cp, peer) in enumerate(sends):
            _remote(v_ref, buf.at[peer], send.at[m], recv.at[m], (x, y, c)).wait_recv()
        for cp, _ in sends:
            cp.wait_send()
        acc = buf[0]
        for k in range(1, n_dev):
            acc = acc + buf[k]
        o_ref[...] = acc

    return pl.pallas_call(
        body, name="all_reduce_devices", in_specs=[VMEM_SPEC], out_specs=VMEM_SPEC,
        out_shape=jax.ShapeDtypeStruct(v.shape, F32),
        scratch_shapes=[pltpu.VMEM((n_dev,) + v.shape, F32), pltpu.SemaphoreType.DMA((n_dev - 1,)),
                        pltpu.SemaphoreType.DMA((n_dev - 1,))],
    )(v)


_SHARDED = (("w_in_even", 2), ("mla_w_uq", 2), ("mla_w_ukv", 2), ("w_out_even", 1), ("w_in_odd", 2), ("w_out_odd", 1),
            ("ffn_w_up", 2), ("ffn_w_down", 1),
            ("mix_norm_odd", 1), ("gla_w_gate_fwd", 2), ("gla_b_gate_fwd", 1), ("gla_w_gate_bwd", 2),
            ("gla_b_gate_bwd", 1), ("gla_out_norm", 2), ("ffn_conv_w", 2))
_N_MATRICES = 8
_REPLICATED = ("mix_norm_even", "mla_q_norm", "mla_kv_norm", "mla_q_head_norm", "mla_k_head_norm", "ret_theta_fwd",
               "ret_theta_bwd", "ret_out_norm", "ffn_norm", "ffn_conv_b")
_WEIGHTS = ("mix_norm_even", "w_in_even", "mla_q_norm", "mla_kv_norm", "mla_w_uq", "mla_w_ukv", "mla_q_head_norm",
            "mla_k_head_norm", "ret_theta_fwd", "ret_theta_bwd", "ret_out_norm", "w_out_even", "mix_norm_odd",
            "w_in_odd", "gla_w_gate_fwd", "gla_b_gate_fwd", "gla_w_gate_bwd", "gla_b_gate_bwd", "gla_out_norm",
            "w_out_odd", "ffn_norm", "ffn_w_up", "ffn_conv_w", "ffn_conv_b", "ffn_w_down")


def _flatten(arrs, row_multiple, dtype):
    flat = jnp.concatenate([a.reshape(-1).astype(dtype) for a in arrs])
    per = ROW * row_multiple
    total = -(-flat.shape[0] // per) * per
    return jnp.pad(flat, (0, total - flat.shape[0])).reshape(-1, ROW)


def _unflatten(flat, shapes):
    flat = flat.reshape(-1)
    out, o = [], 0
    for shp in shapes:
        n = math.prod(shp)
        out.append(flat[o:o + n].reshape(shp))
        o += n
    return out


def kernel(x, positions, mix_norm_even, w_in_even, mla_q_norm, mla_kv_norm, mla_w_uq, mla_w_ukv, mla_q_head_norm, mla_k_head_norm, ret_theta_fwd, ret_theta_bwd, ret_out_norm, w_out_even, mix_norm_odd, w_in_odd, gla_w_gate_fwd, gla_b_gate_fwd, gla_w_gate_bwd, gla_b_gate_bwd, gla_out_norm, w_out_odd, ffn_norm, ffn_w_up, ffn_conv_w, ffn_conv_b, ffn_w_down, loss_target, m_mix_norm_even, m_w_in_even, m_mla_q_norm, m_mla_kv_norm, m_mla_w_uq, m_mla_w_ukv, m_mla_q_head_norm, m_mla_k_head_norm, m_ret_theta_fwd, m_ret_theta_bwd, m_ret_out_norm, m_w_out_even, m_mix_norm_odd, m_w_in_odd, m_gla_w_gate_fwd, m_gla_b_gate_fwd, m_gla_w_gate_bwd, m_gla_b_gate_bwd, m_gla_out_norm, m_w_out_odd, m_ffn_norm, m_ffn_w_up, m_ffn_conv_w, m_ffn_conv_b, m_ffn_w_down, v_mix_norm_even, v_w_in_even, v_mla_q_norm, v_mla_kv_norm, v_mla_w_uq, v_mla_w_ukv, v_mla_q_head_norm, v_mla_k_head_norm, v_ret_theta_fwd, v_ret_theta_bwd, v_ret_out_norm, v_w_out_even, v_mix_norm_odd, v_w_in_odd, v_gla_w_gate_fwd, v_gla_b_gate_fwd, v_gla_w_gate_bwd, v_gla_b_gate_bwd, v_gla_out_norm, v_w_out_odd, v_ffn_norm, v_ffn_w_up, v_ffn_conv_w, v_ffn_conv_b, v_ffn_w_down):
    args = dict(locals())
    x2, pos, target = args["x"][0], args["positions"][0], args["loss_target"][0]
    axis = dict(_SHARDED)
    mats = [n for n, _ in _SHARDED[:_N_MATRICES]]
    smalls = [n for n, _ in _SHARDED[_N_MATRICES:]]
    small_shapes = [args[n].shape for n in smalls]

    local = {n: _bf(args[n]) for n in mats}
    first_layers = {n: (0, 0 if n.endswith("_odd") else 1) for n in mats}
    now = [(local[n],) + first_layers[n] for n in mats if first_layers[n][1]]
    later = [(local[n], first_layers[n][1], args[n].shape[0] - first_layers[n][1]) for n in mats]
    small_block = _flatten([args[n] for n in smalls], 2 * HALO, F32)
    got_now = _gather_chips(now + [(small_block, 0, small_block.shape[0])])
    per_chip = [_unflatten(got_now[-1][j], small_shapes) for j in range(CHIPS)]
    base = {n: args[n] for n in _REPLICATED}
    for k, n in enumerate(smalls):
        base[n] = jnp.concatenate([per_chip[j][k] for j in range(CHIPS)], axis=axis[n])

    def whole(stacks):
        full = dict(base)
        for n, per_layer in stacks.items():
            if n == "ffn_w_up":
                full[n] = per_layer
            else:
                full[n] = [None if st is None else jnp.concatenate([st[j, l] for j in range(CHIPS)], axis=axis[n] - 1)
                           for st, l in per_layer]
        return full

    stacks = {n: [(None, 0)] * args[n].shape[0] for n in mats}
    for (a, first, count), st in zip(now, got_now):
        n = next(m for m in mats if local[m] is a)
        stacks[n] = [(st, l) for l in range(count)] + stacks[n][count:]

    def finish(got_later):
        for (a, first, count), st in zip(later, got_later):
            n = next(m for m in mats if local[m] is a)
            stacks[n] = stacks[n][:first] + [(st, l) for l in range(count)]
        return whole(stacks)

    loss, grad_x, grads = _local_step(x2, pos, target, whole(stacks), side=later, finish=finish)
    loss = lax.psum(loss, ("x", "y", "c"))

    def by_chip(n, g):
        if n == "ffn_w_up":
            return g
        if axis[n] == 1:
            return g.reshape((CHIPS, g.shape[0] // CHIPS) + g.shape[1:])
        return jnp.stack(jnp.split(g, CHIPS, axis=axis[n] - 1))

    core = lax.axis_index("c").astype(jnp.int32).reshape(1)
    stacked = [jnp.stack([by_chip(n, g) for g in grads[n]], axis=1) for n in mats]
    small_parts = [jnp.split(jnp.stack(grads[n]), CHIPS, axis=axis[n]) for n in smalls]
    stacked.append(jnp.stack([_flatten([p[j] for p in small_parts], 2 * HALO, F32) for j in range(CHIPS)])[:, None])
    names = mats + ["small"]
    tiles = [_rows_tile(a.shape[2] // CORES, a.shape[3]) for a in stacked]
    got = _swap_halves(stacked)
    chip_sums = [_add_core_halves(a, b, core, ts=ts, name="add_core_halves_" + n)
                 for n, a, b, ts in zip(names, stacked, got, tiles)]
    parts = _scatter_chips(chip_sums)
    sums = [_add_chip_parts(p, core, ts=ts, name="add_chip_parts_" + n) for n, p, ts in zip(names, parts, tiles)]
    reduced = _gather_cores(sums)

    res = {}

    def update(n, w, g, m, v, ts):
        cols = g.shape[-1]
        outs = _adamw(w.reshape(-1, cols), g.reshape(-1, cols), m.reshape(-1, cols), v.reshape(-1, cols), ts=ts,
                      name="adamw_" + n)
        return [g] + [o.reshape(g.shape) for o in outs]

    kinds = ("grad", "delta", "new_m", "new_v")
    for n, g, ts in zip(mats, reduced, tiles):
        for kind, a in zip(kinds, update(n, args[n], g, args["m_" + n], args["v_" + n], ts)):
            res[kind + "_" + n] = a
    w_s, m_s, v_s = (_flatten([args[pre + n] for n in smalls], 2 * HALO, F32) for pre in ("", "m_", "v_"))
    for kind, flat in zip(kinds, update("small", w_s, reduced[-1][0], m_s, v_s, tiles[-1])):
        for n, a in zip(smalls, _unflatten(flat, small_shapes)):
            res[kind + "_" + n] = a

    rep_shapes = [args[n].shape for n in _REPLICATED]
    g_rep = _all_reduce_devices(_flatten([jnp.stack(grads[n]) for n in _REPLICATED], HALO, F32))
    w_rep, m_rep, v_rep = (_flatten([args[pre + n] for n in _REPLICATED], HALO, F32) for pre in ("", "m_", "v_"))
    for kind, flat in zip(kinds, update("replicated", w_rep, g_rep, m_rep, v_rep, g_rep.shape[0])):
        for n, a in zip(_REPLICATED, _unflatten(flat, rep_shapes)):
            res[kind + "_" + n] = a

    outs = [loss, grad_x[None]]
    for kind in ("grad", "delta", "new_m", "new_v"):
        outs += [res[kind + "_" + n] for n in _WEIGHTS]
    return tuple(outs)
```

```python
import math

import jax
import jax.numpy as jnp
from jax import lax
from jax.experimental import pallas as pl
from jax.experimental.pallas import tpu as pltpu

F32 = jnp.float32
BF16 = jnp.bfloat16
MESH = pl.DeviceIdType.MESH

EPS = 1e-6
D_MODEL = 1024
DEPTH = 4
LANES = 128
MLA_H, MLA_QR, MLA_KVR, MLA_NOPE, MLA_ROPE, MLA_V = 8, 384, 256, 64, 32, 64
MLA_QK = MLA_NOPE + MLA_ROPE
MLA_SCALE = MLA_QK ** -0.5
RET_H, RET_DK, RET_DV, RET_C = 8, 64, 64, 128
GLA_H, GLA_DK, GLA_DV, GLA_R, GLA_TAU, GLA_C = 4, 128, 256, 16, 16.0, 64
D_FF = 2816
ROPE_THETA = 10000.0
LN2 = math.log(2.0)
ADAM_LR, ADAM_B1, ADAM_B2, ADAM_EPS, ADAM_WD, ADAM_STEP = 0.001, 0.9, 0.999, 1e-08, 0.01, 10

EV_RET = 4 * RET_H * LANES
EV_CQ = 512
EV_W = 5120
EV_KR_BLK = (EV_RET + EV_CQ + MLA_KVR) // LANES
OD_W = 3200
OD_GA_BLK = 3072 // LANES

VMEM_LIMIT = 56 * 1024 * 1024
MM_TILE_CAP = 1408
EW_ROWS = 512
FFN_ACT_ROWS = 256
V_ONES = (MLA_V, MLA_V + 1)
FLASH_FWD_ROWS = 1024
FLASH_BWD_ROWS = 1024
FLASH_KEYS = 1024


def _cp(sem):
    return pltpu.CompilerParams(dimension_semantics=sem, vmem_limit_bytes=VMEM_LIMIT)


def _dot(a, b):
    return jnp.dot(a, b, preferred_element_type=F32)


def _dot_nt(a, b):
    return lax.dot_general(a, b, (((1,), (1,)), ((), ())), preferred_element_type=F32)


def _dot_tn(a, b):
    return lax.dot_general(a, b, (((0,), (0,)), ((), ())), preferred_element_type=F32)


def _bf(x):
    return x.astype(BF16)


def _split3(x):
    h1 = _bf(x)
    r1 = x - h1.astype(F32)
    h2 = _bf(r1)
    h3 = _bf(r1 - h2.astype(F32))
    return h1, h2, h3


def _tile(n, cap):
    if n <= cap:
        return n
    best = None
    for t in range(LANES, cap + 1, LANES):
        if n % t == 0:
            best = t
    assert best is not None, n
    return best


def _mm(a, b, *, ta=False, tb=False, res=None, out_dtype=F32, b_layer=None, out_chips=False, halves=None, name):
    assert not (ta and tb)
    if halves == "a":
        assert not ta
        m, kdim = a.shape[1], 2 * a.shape[2]
    elif ta:
        kdim, m = a.shape
    else:
        m, kdim = a.shape
    if b_layer is not None:
        rows_b, cols_b = b.shape[2], b.shape[0] * b.shape[3]
    elif halves == "b":
        assert not tb
        rows_b, cols_b = b.shape[1], 2 * b.shape[2]
    else:
        rows_b, cols_b = b.shape
    n, kb = (rows_b, cols_b) if tb else (cols_b, rows_b)
    assert kb == kdim, (a.shape, b.shape, ta, tb)
    tm, tn, tk = _tile(m, MM_TILE_CAP), _tile(n, MM_TILE_CAP), _tile(kdim, MM_TILE_CAP)
    nk = kdim // tk
    has_res = res is not None
    vmem = (2 * tm * tk * a.dtype.itemsize + 2 * tk * tn * b.dtype.itemsize
            + 2 * tm * tn * jnp.dtype(out_dtype).itemsize + (2 * tm * tn * 4 if has_res else 0)
            + (tm * tn * 4 if nk > 1 else 0))
    assert vmem <= VMEM_LIMIT - 8 * 1024 * 1024, (name, vmem)
    a_spec = (pl.BlockSpec((tk, tm), lambda i, j, k: (k, i)) if ta
              else pl.BlockSpec((tm, tk), lambda i, j, k: (i, k)))
    if halves == "a":
        per_half = a.shape[2] // tk
        a_spec = pl.BlockSpec((None, tm, tk), lambda i, j, k: (k // per_half, i, k % per_half))
    if halves == "b":
        per_half = b.shape[2] // tn
        b_spec = pl.BlockSpec((None, tk, tn), lambda i, j, k: (j // per_half, k, j % per_half))
    elif b_layer is not None:
        per_chip = b.shape[3]
        if tb:
            assert tk == per_chip
            b_spec = pl.BlockSpec((None, None, tn, tk), lambda i, j, k: (k, b_layer, j, 0))
        else:
            assert tn == per_chip
            b_spec = pl.BlockSpec((None, None, tk, tn), lambda i, j, k: (j, b_layer, k, 0))
    else:
        b_spec = (pl.BlockSpec((tn, tk), lambda i, j, k: (j, k)) if tb
                  else pl.BlockSpec((tk, tn), lambda i, j, k: (k, j)))
    if out_chips:
        assert n // tn == CHIPS and not has_res
        o_spec = pl.BlockSpec((None, tm, tn), lambda i, j, k: (j, i, 0))
        out_struct = jax.ShapeDtypeStruct((CHIPS, m, tn), out_dtype)
    else:
        o_spec = pl.BlockSpec((tm, tn), lambda i, j, k: (i, j))
        out_struct = jax.ShapeDtypeStruct((m, n), out_dtype)

    def product(a_ref, b_ref):
        av, bv = _bf(a_ref[...]), _bf(b_ref[...])
        if ta:
            return _dot_tn(av, bv)
        if tb:
            return _dot_nt(av, bv)
        return _dot(av, bv)

    def body(*refs):
        a_ref, b_ref = refs[:2]
        r_ref = refs[2] if has_res else None
        o_ref = refs[3] if has_res else refs[2]

        def finish(r):
            if has_res:
                r = r + r_ref[...]
            o_ref[...] = r.astype(o_ref.dtype)

        if nk == 1:
            finish(product(a_ref, b_ref))
            return
        acc = refs[-1]
        k = pl.program_id(2)

        @pl.when(k == 0)
        def _():
            acc[...] = product(a_ref, b_ref)

        @pl.when(k > 0)
        def _():
            acc[...] += product(a_ref, b_ref)

        @pl.when(k == nk - 1)
        def _():
            finish(acc[...])

    ins = [a, b] + ([res] if has_res else [])
    in_specs = [a_spec, b_spec] + ([o_spec] if has_res else [])
    return pl.pallas_call(
        body, name=name, grid=(m // tm, n // tn, nk),
        in_specs=in_specs, out_specs=o_spec, out_shape=out_struct,
        scratch_shapes=[pltpu.VMEM((tm, tn), F32)] if nk > 1 else [],
        compiler_params=_cp(("parallel", "parallel", "arbitrary")),
    )(*ins)


def _ew(fn, rows, pars, outs, accs=(), *, s, ts, name):
    n_in = len(rows) + len(pars)
    n_o = len(outs)

    def body(*refs):
        i = pl.program_id(0)
        vals = fn(*[r[...] for r in refs[:n_in]])
        if not isinstance(vals, (tuple, list)):
            vals = (vals,)
        assert len(vals) == n_o + len(accs), (name, len(vals))
        for r, v in zip(refs[n_in:n_in + n_o], vals[:n_o]):
            r[...] = v.astype(r.dtype)
        for r, v in zip(refs[n_in + n_o:], vals[n_o:]):
            @pl.when(i == 0)
            def _(r=r, v=v):
                r[...] = v

            @pl.when(i > 0)
            def _(r=r, v=v):
                r[...] += v

    in_specs = [sp for _, sp in rows]
    in_specs += [pl.BlockSpec(p.shape, lambda i, nd=p.ndim: (0,) * nd) for p in pars]
    out_specs = [pl.BlockSpec((ts, w), lambda i: (i, 0)) for w, _ in outs]
    out_specs += [pl.BlockSpec((r, w), lambda i: (0, 0)) for r, w in accs]
    out_shape = [jax.ShapeDtypeStruct((s, w), dt) for w, dt in outs]
    out_shape += [jax.ShapeDtypeStruct((r, w), F32) for r, w in accs]
    return pl.pallas_call(
        body, name=name, grid=(s // ts,), in_specs=in_specs, out_specs=out_specs, out_shape=out_shape,
        compiler_params=_cp(("arbitrary",)),
    )(*[a for a, _ in rows], *pars)


def _cols(arr, width, blk, ts):
    return (arr, pl.BlockSpec((ts, width), lambda i, b=blk: (i, b)))


def _lead(pair, d, ts):
    return _cols(pair[d], pair[d].shape[1], 0, ts)


def _rowsum(x):
    return jnp.sum(x, axis=0, keepdims=True)


def _lanesum(x):
    return jnp.sum(x, axis=-1, keepdims=True)


def _gsum(x, group):
    w = x.shape[-1]
    if group == w:
        return jnp.broadcast_to(_lanesum(x), x.shape)
    parts = [jnp.broadcast_to(_lanesum(x[:, g:g + group]), (x.shape[0], group)) for g in range(0, w, group)]
    return jnp.concatenate(parts, axis=-1)


def _gn(x, gain, group, n):
    rstd = lax.rsqrt(_gsum(x * x, group) * (1.0 / n) + EPS)
    xn = x * rstd
    return xn * gain, xn, rstd


def _gn_bwd(dy, xn, rstd, gain, group, n):
    dxn = dy * gain
    dx = rstd * (dxn - xn * (_gsum(dxn * xn, group) * (1.0 / n)))
    return dx, _rowsum(dy * xn)


def _sigmoid(x):
    return 1.0 / (1.0 + jnp.exp(-x))


def _rmsnorm(x_row, g, *, n, s, ts, name):
    w = g.shape[-1]

    def fn(x, gv):
        return _gn(x, gv, w, n)[0]

    return _ew(fn, [x_row], [g], [(w, BF16)], s=s, ts=ts, name=name)[0]


def _rmsnorm_bwd(x_row, g, dh, dres, *, n, s, ts, name):
    w = g.shape[-1]
    has_res = dres is not None

    def fn(x, dhv, *rest):
        gv = rest[-1]
        _, xn, rstd = _gn(x, gv, w, n)
        dx, dg = _gn_bwd(dhv, xn, rstd, gv, w, n)
        if has_res:
            dx = dx + rest[0]
        return dx, dg

    rows = [x_row, _cols(dh, w, 0, ts)] + ([_cols(dres, w, 0, ts)] if has_res else [])
    return _ew(fn, rows, [g], [(w, F32)], [(1, w)], s=s, ts=ts, name=name)


def _rope_tables(pos, real, offset):
    half = real // 2
    inv = ROPE_THETA ** (-jnp.arange(half, dtype=F32) / half)
    ang = pos.astype(F32)[:, None] * inv
    c, sn = jnp.cos(ang), jnp.sin(ang)
    s = pos.shape[0]
    cos_t = jnp.concatenate([jnp.ones((s, offset), F32), c, c,
                             jnp.ones((s, LANES - offset - real), F32)], axis=1)
    sin_t = jnp.concatenate([jnp.zeros((s, offset), F32), -sn, sn,
                             jnp.zeros((s, LANES - offset - real), F32)], axis=1)
    return cos_t, sin_t


def _rope(x, cos_t, sin_t, real, offset):
    half = real // 2
    lane = lax.broadcasted_iota(jnp.int32, x.shape, 1)
    partner = jnp.where(lane < offset + half, pltpu.roll(x, LANES - half, 1), pltpu.roll(x, half, 1))
    return x * cos_t + partner * sin_t


def _mla_prep(q_pre, kv_pre, p_even, cos_m, sin_m, qhn, khn, *, s, ts):
    w = MLA_H * LANES

    def fn(qp, kp, vp, kr, c, sn, gq, gk):
        qs, ks = [], []
        for h in range(MLA_H):
            sl = slice(h * LANES, (h + 1) * LANES)
            qn = _gn(qp[:, sl], gq, LANES, MLA_QK)[0]
            kn = _gn(kp[:, sl] + kr, gk, LANES, MLA_QK)[0]
            qs.append(_rope(qn, c, sn, MLA_ROPE, MLA_NOPE) * MLA_SCALE)
            ks.append(_rope(kn, c, sn, MLA_ROPE, MLA_NOPE))
        lane = lax.broadcasted_iota(jnp.int32, vp.shape, 1) % LANES
        ones = (lane == V_ONES[0]) | (lane == V_ONES[1])
        return jnp.concatenate(qs, axis=1), jnp.concatenate(ks, axis=1), jnp.where(ones, 1.0, vp)

    rows = [_cols(q_pre, w, 0, ts), _cols(kv_pre, w, 0, ts), _cols(kv_pre, w, 1, ts),
            _cols(p_even, LANES, EV_KR_BLK, ts), _cols(cos_m, LANES, 0, ts), _cols(sin_m, LANES, 0, ts)]
    return _ew(fn, rows, [qhn, khn], [(w, BF16)] * 3, s=s, ts=ts, name="mla_prep")


def _mla_prep_bwd(q_pre, kv_pre, p_even, cos_m, sin_m, qhn, khn, dq, dk, *, s, ts):
    w = MLA_H * LANES

    def fn(qp, kp, kr, c, sn, dqv, dkv, gq, gk):
        dqs, dks = [], []
        dkr = jnp.zeros_like(kr)
        dgq = jnp.zeros((1, LANES), F32)
        dgk = jnp.zeros((1, LANES), F32)
        for h in range(MLA_H):
            sl = slice(h * LANES, (h + 1) * LANES)
            _, qn, qr = _gn(qp[:, sl], gq, LANES, MLA_QK)
            _, kn, krs = _gn(kp[:, sl] + kr, gk, LANES, MLA_QK)
            dqn = _rope(dqv[:, sl] * MLA_SCALE, c, -sn, MLA_ROPE, MLA_NOPE)
            dkn = _rope(dkv[:, sl], c, -sn, MLA_ROPE, MLA_NOPE)
            dqh, g1 = _gn_bwd(dqn, qn, qr, gq, LANES, MLA_QK)
            dkh, g2 = _gn_bwd(dkn, kn, krs, gk, LANES, MLA_QK)
            dqs.append(dqh)
            dks.append(dkh)
            dkr = dkr + dkh
            dgq = dgq + g1
            dgk = dgk + g2
        return jnp.concatenate(dqs, axis=1), jnp.concatenate(dks, axis=1), dkr, dgq, dgk

    rows = [_cols(q_pre, w, 0, ts), _cols(kv_pre, w, 0, ts), _cols(p_even, LANES, EV_KR_BLK, ts),
            _cols(cos_m, LANES, 0, ts), _cols(sin_m, LANES, 0, ts), _cols(dq, w, 0, ts), _cols(dk, w, 0, ts)]
    return _ew(fn, rows, [qhn, khn], [(w, BF16), (w, BF16), (LANES, BF16)], [(1, LANES), (1, LANES)],
               s=s, ts=ts, name="mla_prep_bwd")


def _flash_fwd(q, k, v, *, tq, tk, side=()):
    s = q.shape[0]
    nq, nk = s // tq, s // tk
    rq = tq
    ns = len(side)

    def body(*refs):
        q_ref, k_ref, v_ref = refs[:3]
        o_ref, lse_ref = refs[3 + ns:5 + ns]
        m_s, acc = refs[5 + 2 * ns:7 + 2 * ns]
        h, i, j = pl.program_id(0), pl.program_id(1), pl.program_id(2)
        if ns:
            local, sends, arrivals = _gather_copies(side, refs[3:3 + ns], refs[5 + ns:5 + 2 * ns], *refs[7 + 2 * ns:])

            @pl.when((h == 0) & (i == 0) & (j == 0))
            def _():
                for cp in local + sends:
                    cp.start()

        @pl.when(j == 0)
        def _():
            m_s[...] = jnp.full_like(m_s, -jnp.inf)
            acc[...] = jnp.zeros_like(acc)

        kv, vv = k_ref[...], v_ref[...]
        for r in range(0, tq, rq):
            rows = slice(r, r + rq)
            sc = _dot_nt(q_ref[rows, :], kv)
            m_prev = m_s[rows, :]
            m_new = jnp.maximum(m_prev, jnp.max(sc, axis=-1, keepdims=True))
            p = jnp.exp(sc - jnp.tile(m_new, (1, tk // LANES)))
            acc[rows, :] = jnp.exp(m_prev - m_new) * acc[rows, :] + _dot(_bf(p), vv)
            m_s[rows, :] = m_new

        @pl.when(j == nk - 1)
        def _():
            a = acc[...]
            l = a[:, V_ONES[0]:V_ONES[0] + 1]
            o_ref[...] = (a / l).astype(o_ref.dtype)
            lse_ref[...] = (m_s[...] + jnp.log(jnp.broadcast_to(l, (tq, LANES)))).T[0:1, :]

        if ns:
            @pl.when((h == MLA_H - 1) & (i == nq - 1) & (j == nk - 1))
            def _():
                for cp in arrivals:
                    cp.wait_recv()
                for cp in sends:
                    cp.wait_send()
                for cp in local:
                    cp.wait()

    qs = pl.BlockSpec((tq, LANES), lambda h, i, j: (i, h))
    ks = pl.BlockSpec((tk, LANES), lambda h, i, j: (j, h))
    outs = pl.pallas_call(
        body, name="mla_flash_fwd_gather" if ns else "mla_flash_fwd", grid=(MLA_H, nq, nk),
        in_specs=[qs, ks, ks] + [HBM_SPEC] * ns,
        out_specs=[qs, pl.BlockSpec((None, 1, tq), lambda h, i, j: (h, 0, i))] + [HBM_SPEC] * ns,
        out_shape=[jax.ShapeDtypeStruct((s, MLA_H * LANES), BF16), jax.ShapeDtypeStruct((MLA_H, 1, s), F32)]
        + _gather_shapes(side),
        scratch_shapes=[pltpu.VMEM((tq, LANES), F32), pltpu.VMEM((tq, LANES), F32)]
        + ([_sems(3 * ns), _sems(3 * ns), _sems(ns)] if ns else []),
        compiler_params=_cp(("arbitrary",) * 3 if ns else ("parallel", "parallel", "arbitrary")),
    )(q, k, v, *[a for a, _, _ in side])
    return outs[0], outs[1], list(outs[2:])


def _attn_bwd_prep(dar, o, *, s, ts):
    w = MLA_H * LANES

    def fn(dov, ov):
        outs = []
        lane = lax.broadcasted_iota(jnp.int32, (dov.shape[0], LANES), 1)
        for h in range(MLA_H):
            sl = slice(h * LANES, (h + 1) * LANES)
            d = dov[:, sl]
            delta = _lanesum(d * ov[:, sl].astype(F32))
            hi = _bf(delta).astype(F32)
            outs.append(jnp.where(lane == V_ONES[0], -hi, jnp.where(lane == V_ONES[1], hi - delta, d)))
        return jnp.concatenate(outs, axis=1)

    return _ew(fn, [_cols(dar, w, 0, ts), _cols(o, w, 0, ts)], [], [(w, BF16)], s=s, ts=ts,
               name="mla_attn_bwd_prep")[0]


def _flash_bwd(q, k, v, do, lse, *, tq, tk):
    s = q.shape[0]
    nq, nk = s // tq, s // tk

    def body(q_ref, k_ref, v_ref, do_ref, lse_ref, dq_ref, dk_ref, dv_ref, dk_acc, dv_acc):
        j = pl.program_id(1)
        i = pl.program_id(2)
        qv, kv, vv, dov = q_ref[...], k_ref[...], v_ref[...], do_ref[...]
        pt = jnp.exp(_dot_nt(kv, qv) - lse_ref[...])
        dst = _bf(pt * _dot_nt(vv, dov))
        dv_c = _dot(_bf(pt), dov)
        dk_c = _dot(dst, qv)
        dq_c = _dot_tn(dst, kv)
        rows = pl.ds(pl.multiple_of(i * tq, tq), tq)

        @pl.when(i == 0)
        def _():
            dk_acc[...] = dk_c
            dv_acc[...] = dv_c

        @pl.when(i > 0)
        def _():
            dk_acc[...] += dk_c
            dv_acc[...] += dv_c

        @pl.when(j == 0)
        def _():
            dq_ref[rows, :] = dq_c

        @pl.when(j > 0)
        def _():
            dq_ref[rows, :] += dq_c

        @pl.when(i == nq - 1)
        def _():
            dk_ref[...] = dk_acc[...]
            dv_ref[...] = dv_acc[...].astype(dv_ref.dtype)

    qs = pl.BlockSpec((tq, LANES), lambda h, j, i: (i, h))
    ks = pl.BlockSpec((tk, LANES), lambda h, j, i: (j, h))
    st = pl.BlockSpec((None, 1, tq), lambda h, j, i: (h, 0, i))
    return pl.pallas_call(
        body, name="mla_flash_bwd", grid=(MLA_H, nk, nq),
        in_specs=[qs, ks, ks, qs, st],
        out_specs=[pl.BlockSpec((s, LANES), lambda h, j, i: (0, h)), ks, ks],
        out_shape=[jax.ShapeDtypeStruct((s, MLA_H * LANES), F32), jax.ShapeDtypeStruct((s, MLA_H * LANES), F32),
                   jax.ShapeDtypeStruct((s, MLA_H * LANES), BF16)],
        scratch_shapes=[pltpu.VMEM((tk, LANES), F32), pltpu.VMEM((tk, LANES), F32)],
        compiler_params=_cp(("parallel", "arbitrary", "arbitrary")),
    )(q, k, v, do, lse)


def _ret_geometry(d, c):
    df = float(d)
    ii = lax.broadcasted_iota(jnp.int32, (c, c), 0).astype(F32)
    jj = lax.broadcasted_iota(jnp.int32, (c, c), 1).astype(F32)
    rel = (ii - jj) * (1.0 - 2.0 * df)
    mask = rel >= df
    rel0 = jnp.maximum(rel, 0.0)
    pos = lax.broadcasted_iota(jnp.int32, (c, 1), 0).astype(F32)
    ez = (c - 1 - pos) + df * (2.0 * pos - (c - 1))
    ex = (pos + 1.0) + df * (c - 1 - 2.0 * pos)
    return mask, rel0, ez, ex


def _chunk_index(n_chunks):
    return lambda d, n: n + d * (n_chunks - 1 - 2 * n)


def _ret_fwd(p_even, cos_r, sin_r, theta_l):
    s = p_even.shape[0]
    c = RET_C
    n_chunks = s // c
    w = RET_H * LANES
    cidx = _chunk_index(n_chunks)

    def body(*refs):
        n = pl.program_id(0)

        @pl.when(n == 0)
        def _():
            for r_s in refs[16:18]:
                r_s[...] = jnp.zeros_like(r_s)

        stores = []
        for d in range(2):
            stores += one(d, *refs[6 * d:6 * d + 6], *refs[12 + 2 * d:14 + 2 * d], refs[16 + d])
        for ref, val in stores:
            ref[...] = val

    def one(d, q_ref, k_ref, v_ref, cos_ref, sin_ref, th_ref, o_ref, rp_ref, r_s):
        lg = jnp.log1p(-jnp.exp(-th_ref[...] * LN2))
        mask, rel0, ez, ex = _ret_geometry(d, c)
        cs, sn = cos_ref[...], sin_ref[...]
        r_all = r_s[...]
        outs, states = [], []
        for h in range(RET_H):
            sl = slice(h * LANES, (h + 1) * LANES)
            lgh = lg[:, h * LANES:h * LANES + 1]
            dm = jnp.where(mask, jnp.exp(lgh * rel0), 0.0)
            qh = _bf(_rope(q_ref[:, sl], cs, sn, RET_DK, 0))
            kf = _rope(k_ref[:, sl], cs, sn, RET_DK, 0) * (RET_DK ** -0.5)
            kh = _bf(kf)
            vh = _bf(v_ref[:, sl])
            rh = r_all[sl, :]
            a = _dot_nt(qh, kh) * dm
            outs.append(_dot(_bf(a), vh) + jnp.exp(lgh * ex) * _dot(qh, _bf(rh)))
            zk = _bf(kf * jnp.exp(lgh * ez))
            states.append(jnp.exp(lgh * c) * rh + _dot_tn(zk, vh))
        return [(rp_ref, r_all), (o_ref, jnp.concatenate(outs, axis=1)), (r_s, jnp.concatenate(states, axis=0))]

    def ins(d):
        col = lambda blk: pl.BlockSpec((c, w), lambda n: (cidx(d, n), blk))
        tab = pl.BlockSpec((c, LANES), lambda n: (cidx(d, n), 0))
        return [col(0), col(1), col(2), tab, tab, pl.BlockSpec((None, 1, w), lambda n: (d, 0, 0))]

    def outs(d):
        return [pl.BlockSpec((c, w), lambda n: (cidx(d, n), 0)),
                pl.BlockSpec((None, w, LANES), lambda n: (cidx(d, n), 0, 0))]

    o_f, r_f, o_b, r_b = pl.pallas_call(
        body, name="ret_fwd", grid=(n_chunks,),
        in_specs=ins(0) + ins(1), out_specs=outs(0) + outs(1),
        out_shape=[jax.ShapeDtypeStruct((s, w), F32), jax.ShapeDtypeStruct((n_chunks, w, LANES), F32)] * 2,
        scratch_shapes=[pltpu.VMEM((w, LANES), F32)] * 2,
        compiler_params=_cp(("arbitrary",)),
    )(*[p_even, p_even, p_even, cos_r, sin_r, theta_l] * 2)
    return (o_f, o_b), (r_f, r_b)


def _ret_bwd(p_even, cos_r, sin_r, theta_l, theta_h, r_prev, do):
    s = p_even.shape[0]
    c = RET_C
    n_chunks = s // c
    w = RET_H * LANES
    fwd_idx = _chunk_index(n_chunks)

    def cidx(d, n):
        return fwd_idx(d, n_chunks - 1 - n)

    def body(*refs):
        n = pl.program_id(0)

        @pl.when(n == 0)
        def _():
            for d in range(2):
                refs[26 + d][...] = jnp.zeros_like(refs[26 + d])
                refs[21 + 4 * d][...] = jnp.zeros_like(refs[21 + 4 * d])

        stores = []
        for d in range(2):
            stores += one(d, *refs[9 * d:9 * d + 9], *refs[18 + 4 * d:22 + 4 * d], refs[26 + d])
        for ref, val, accumulate in stores:
            if accumulate:
                ref[...] += val
            else:
                ref[...] = val

    def one(d, q_ref, k_ref, v_ref, cos_ref, sin_ref, th_ref, thh_ref, rp_ref, do_ref,
            dq_ref, dk_ref, dv_ref, dth_ref, dr_s):
        lg = jnp.log1p(-jnp.exp(-th_ref[...] * LN2))
        mask, rel0, ez, ex = _ret_geometry(d, c)
        cs, sn = cos_ref[...], sin_ref[...]
        rp_all, dr_all = rp_ref[...], dr_s[...]
        row = lax.broadcasted_iota(jnp.int32, (RET_H, LANES), 0)
        dlg = jnp.zeros((RET_H, LANES), F32)
        kscale = RET_DK ** -0.5
        dqs, dks, dvs, drs = [], [], [], []
        for h in range(RET_H):
            sl = slice(h * LANES, (h + 1) * LANES)
            lgh = lg[:, h * LANES:h * LANES + 1]
            dm = jnp.where(mask, jnp.exp(lgh * rel0), 0.0)
            zeta = jnp.exp(lgh * ez)
            xi = jnp.exp(lgh * ex)
            gc = jnp.exp(lgh * c)
            qf = _rope(q_ref[:, sl], cs, sn, RET_DK, 0)
            qh = _bf(qf)
            kf = _rope(k_ref[:, sl], cs, sn, RET_DK, 0) * kscale
            kh = _bf(kf)
            zkf = kf * zeta
            zk = _bf(zkf)
            vh = _bf(v_ref[:, sl])
            dof = do_ref[:, sl]
            doh = _bf(dof)
            rp = rp_all[sl, :]
            rpb = _bf(rp)
            drn = dr_all[sl, :]
            drb = _bf(drn)
            a = _dot_nt(qh, kh) * dm
            da0 = _dot_nt(doh, vh)
            da = _bf(da0 * dm)
            vdr = _dot_nt(vh, drb)
            dq_r = _dot(da, kh) + xi * _dot_nt(doh, rpb)
            dk_r = _dot_tn(da, qh) + zeta * vdr
            dvs.append(_dot_tn(_bf(a), doh) + _dot(zk, drb))
            dqs.append(_rope(dq_r, cs, -sn, RET_DK, 0))
            dks.append(_rope(dk_r * kscale, cs, -sn, RET_DK, 0))
            drs.append(_dot_tn(_bf(qf * xi), doh) + gc * drn)
            ocross = xi * _dot(qh, rpb)
            t = (jnp.sum(rel0 * a * da0, keepdims=True)
                 + jnp.sum(ex * dof * ocross, keepdims=True)
                 + c * gc * jnp.sum(drn * rp, keepdims=True)
                 + jnp.sum(ez * zkf * vdr, keepdims=True))
            dlg = jnp.where(row == h, t, dlg)
        x2 = jnp.exp(-thh_ref[...] * LN2)
        return [(dq_ref, jnp.concatenate(dqs, axis=1), False), (dk_ref, jnp.concatenate(dks, axis=1), False),
                (dv_ref, jnp.concatenate(dvs, axis=1), False), (dr_s, jnp.concatenate(drs, axis=0), False),
                (dth_ref, dlg * (x2 * LN2 / (1.0 - x2)), True)]

    def ins(d):
        col = lambda blk: pl.BlockSpec((c, w), lambda n: (cidx(d, n), blk))
        tab = pl.BlockSpec((c, LANES), lambda n: (cidx(d, n), 0))
        return [col(0), col(1), col(2), tab, tab, pl.BlockSpec((None, 1, w), lambda n: (d, 0, 0)),
                pl.BlockSpec((None, RET_H, LANES), lambda n: (d, 0, 0)),
                pl.BlockSpec((None, w, LANES), lambda n: (cidx(d, n), 0, 0)), col(0)]

    def outs(d):
        row = pl.BlockSpec((c, w), lambda n: (cidx(d, n), 0))
        return [row, row, row, pl.BlockSpec((RET_H, LANES), lambda n: (0, 0))]

    res = pl.pallas_call(
        body, name="ret_bwd", grid=(n_chunks,),
        in_specs=ins(0) + ins(1), out_specs=outs(0) + outs(1),
        out_shape=([jax.ShapeDtypeStruct((s, w), F32)] * 3 + [jax.ShapeDtypeStruct((RET_H, LANES), F32)]) * 2,
        scratch_shapes=[pltpu.VMEM((w, LANES), F32)] * 2,
        compiler_params=_cp(("arbitrary",)),
    )(*[a for d in range(2) for a in (p_even, p_even, p_even, cos_r, sin_r, theta_l, theta_h, r_prev[d], do)])
    return (res[0], res[4]), (res[1], res[5]), (res[2], res[6]), jnp.stack([res[3], res[7]])


def _post_fwd(o2, gate_row, gain, *, group, n, s, ts, name):
    w = o2[0].shape[1]

    def fn(of, ob, g, gv):
        y = _gn(of + ob, gv, group, n)[0]
        return g * _sigmoid(g) * y

    return _ew(fn, [_lead(o2, 0, ts), _lead(o2, 1, ts), gate_row], [gain], [(w, BF16)], s=s, ts=ts, name=name)[0]


def _post_bwd(o2, gate_row, gain, dr_row, *, group, n, s, ts, name):
    w = o2[0].shape[1]

    def fn(of, ob, g, dr, gv):
        y, xn, rstd = _gn(of + ob, gv, group, n)
        sg = _sigmoid(g)
        dy = dr * (g * sg)
        dgate = dr * y * (sg * (1.0 + g * (1.0 - sg)))
        do, dgain = _gn_bwd(dy, xn, rstd, gv, group, n)
        return do, dgate, dgain

    return _ew(fn, [_lead(o2, 0, ts), _lead(o2, 1, ts), gate_row, dr_row], [gain],
               [(w, F32), (w, BF16)], [(1, w)], s=s, ts=ts, name=name)


def _sum2(a2, *, s, ts, name):
    w = a2[0].shape[1]
    return _ew(lambda a, b: a + b, [_lead(a2, 0, ts), _lead(a2, 1, ts)], [], [(w, BF16)], s=s, ts=ts, name=name)[0]


def _gla_common(d, q_ref, k_ref, ga_ref, wg_ref, bg_ref):
    c = GLA_C
    df = float(d)
    ii = lax.broadcasted_iota(jnp.int32, (c, c), 0).astype(F32)
    jj = lax.broadcasted_iota(jnp.int32, (c, c), 1).astype(F32)
    rel = (ii - jj) * (1.0 - 2.0 * df)
    tri = _bf(jnp.where(rel >= 0.0, 1.0, 0.0))
    mask = rel >= df
    gab = _bf(ga_ref[...])
    z = _dot(gab, wg_ref[...]) + bg_ref[...]
    la = (jnp.minimum(z, 0.0) - jnp.log1p(jnp.exp(-jnp.abs(z)))) * (1.0 / GLA_TAU)
    l1, l2, l3 = _split3(la)
    b = _dot(tri, l1) + _dot(tri, l2) + _dot(tri, l3)
    first = d == 0
    bm = b[c // 2:c // 2 + 1] if first else b[c // 2 - 1:c // 2]
    bl = b[c - 1:c] if first else b[0:1]
    q = q_ref[...] * (GLA_DK ** -0.5)
    k = k_ref[...]
    e1, e2, e3, eb = jnp.exp(b - bm), jnp.exp(bm - b), jnp.exp(bl - b), jnp.exp(b)
    return dict(tri=tri, mask=mask, gab=gab, z=z, ebl=jnp.exp(bl), e1=e1, e2=e2, e3=e3, eb=eb,
                qc=q * e1, kc=k * e2, kd=k * e3, qe=q * eb, first=first)


def _col_scale(row_vec, width):
    t = jnp.broadcast_to(row_vec, (LANES, LANES)).T
    return jnp.concatenate([t] * (width // LANES), axis=1)


def _gla_fwd(p_odd, wg2, bg2):
    s = p_odd.shape[0]
    c = GLA_C
    n_chunks = s // c
    wk, wv = GLA_H * GLA_DK, GLA_H * GLA_DV
    cidx = _chunk_index(n_chunks)

    def body(*refs):
        n = pl.program_id(0)

        @pl.when(n == 0)
        def _():
            for s_s in refs[16:18]:
                s_s[...] = jnp.zeros_like(s_s)

        stores = []
        for d in range(2):
            stores += one(d, *refs[6 * d:6 * d + 6], *refs[12 + 2 * d:14 + 2 * d], refs[16 + d])
        for ref, val in stores:
            ref[...] = val

    def one(d, q_ref, k_ref, v_ref, ga_ref, wg_ref, bg_ref, o_ref, sp_ref, s_s):
        g = _gla_common(d, q_ref, k_ref, ga_ref, wg_ref, bg_ref)
        s_all = s_s[...]
        outs, states = [], []
        for h in range(GLA_H):
            sl = slice(h * GLA_DK, (h + 1) * GLA_DK)
            vs = slice(h * GLA_DV, (h + 1) * GLA_DV)
            vh = _bf(v_ref[:, vs])
            sh = s_all[sl, :]
            a = jnp.where(g["mask"], _dot_nt(_bf(g["qc"][:, sl]), _bf(g["kc"][:, sl])), 0.0)
            outs.append(_dot(_bf(a), vh) + _dot(_bf(g["qe"][:, sl]), _bf(sh)))
            states.append(_col_scale(g["ebl"][:, sl], GLA_DV) * sh + _dot_tn(_bf(g["kd"][:, sl]), vh))
        return [(sp_ref, s_all), (o_ref, jnp.concatenate(outs, axis=1)), (s_s, jnp.concatenate(states, axis=0))]

    def ins(d):
        col = lambda width, blk: pl.BlockSpec((c, width), lambda n: (cidx(d, n), blk))
        return [col(wk, 0), col(wk, 1), col(wv, 1), col(LANES, OD_GA_BLK),
                pl.BlockSpec((None, LANES, wk), lambda n: (d, 0, 0)), pl.BlockSpec((None, 1, wk), lambda n: (d, 0, 0))]

    def outs(d):
        return [pl.BlockSpec((c, wv), lambda n: (cidx(d, n), 0)),
                pl.BlockSpec((None, wk, GLA_DV), lambda n: (cidx(d, n), 0, 0))]

    o_f, s_f, o_b, s_b = pl.pallas_call(
        body, name="gla_fwd", grid=(n_chunks,),
        in_specs=ins(0) + ins(1), out_specs=outs(0) + outs(1),
        out_shape=[jax.ShapeDtypeStruct((s, wv), F32), jax.ShapeDtypeStruct((n_chunks, wk, GLA_DV), F32)] * 2,
        scratch_shapes=[pltpu.VMEM((wk, GLA_DV), F32)] * 2,
        compiler_params=_cp(("arbitrary",)),
    )(*[p_odd, p_odd, p_odd, p_odd, wg2, bg2] * 2)
    return (o_f, o_b), (s_f, s_b)


def _gla_bwd(p_odd, wg2, bg2, s_prev, do):
    s = p_odd.shape[0]
    c = GLA_C
    n_chunks = s // c
    wk, wv = GLA_H * GLA_DK, GLA_H * GLA_DV
    fwd_idx = _chunk_index(n_chunks)

    def cidx(d, n):
        return fwd_idx(d, n_chunks - 1 - n)

    def body(*refs):
        n = pl.program_id(0)

        @pl.when(n == 0)
        def _():
            for d in range(2):
                for r in (refs[28 + d], refs[20 + 6 * d], refs[21 + 6 * d]):
                    r[...] = jnp.zeros_like(r)

        stores = []
        for d in range(2):
            stores += one(d, *refs[8 * d:8 * d + 8], *refs[16 + 6 * d:22 + 6 * d], refs[28 + d])
        for ref, val, accumulate in stores:
            if accumulate:
                ref[...] += val
            else:
                ref[...] = val

    def one(d, q_ref, k_ref, v_ref, ga_ref, wg_ref, bg_ref, sp_ref, do_ref,
            dq_ref, dk_ref, dv_ref, dga_ref, dwg_ref, dbg_ref, ds_s):
        g = _gla_common(d, q_ref, k_ref, ga_ref, wg_ref, bg_ref)
        mask = g["mask"]
        ones8 = jnp.ones((8, GLA_DV), BF16)
        sp_all, ds_all = sp_ref[...], ds_s[...]
        dbs, dbms, dbls = [], [], []
        dqs, dks, dvs, dss = [], [], [], []
        for h in range(GLA_H):
            sl = slice(h * GLA_DK, (h + 1) * GLA_DK)
            vs = slice(h * GLA_DV, (h + 1) * GLA_DV)
            qc, kc, kd, qe = g["qc"][:, sl], g["kc"][:, sl], g["kd"][:, sl], g["qe"][:, sl]
            qcb, kcb, kdb, qeb = _bf(qc), _bf(kc), _bf(kd), _bf(qe)
            vh = _bf(v_ref[:, vs])
            doh = _bf(do_ref[:, vs])
            sp = sp_all[sl, :]
            dsn = ds_all[sl, :]
            dsb = _bf(dsn)
            a = _bf(jnp.where(mask, _dot_nt(qcb, kcb), 0.0))
            da = _bf(jnp.where(mask, _dot_nt(doh, vh), 0.0))
            dvs.append(_dot_tn(a, doh) + _dot(kdb, dsb))
            dqc = _dot(da, kcb)
            dkc = _dot_tn(da, qcb)
            dqe = _dot_nt(doh, _bf(sp))
            dkd = _dot_nt(vh, dsb)
            dss.append(_dot_tn(qeb, doh) + _col_scale(g["ebl"][:, sl], GLA_DV) * dsn)
            dqs.append((dqc * g["e1"][:, sl] + dqe * g["eb"][:, sl]) * (GLA_DK ** -0.5))
            dks.append(dkc * g["e2"][:, sl] + dkd * g["e3"][:, sl])
            t1, t2, t3, t4 = dqc * qc, dkc * kc, dqe * qe, dkd * kd
            dbs.append(t1 - t2 + t3 - t4)
            dbms.append(_rowsum(t2 - t1))
            m1, m2, _ = _split3(dsn * sp)
            rs = (_dot_nt(ones8, m1) + _dot_nt(ones8, m2))[0:1]
            dbls.append(_rowsum(t4) + g["ebl"][:, sl] * rs)
        db = jnp.concatenate(dbs, axis=1)
        dbm = jnp.concatenate(dbms, axis=1)
        dbl = jnp.concatenate(dbls, axis=1)
        row = lax.broadcasted_iota(jnp.int32, (c, wk), 0)
        mid = jnp.where(g["first"], c // 2, c // 2 - 1)
        last = jnp.where(g["first"], c - 1, 0)
        db = db + jnp.where(row == mid, dbm, 0.0) + jnp.where(row == last, dbl, 0.0)
        d1, d2, d3 = _split3(db)
        tri = g["tri"]
        dla = _dot_tn(tri, d1) + _dot_tn(tri, d2) + _dot_tn(tri, d3)
        dz = dla * (1.0 / GLA_TAU) * (1.0 - _sigmoid(g["z"]))
        dzb = _bf(dz)
        return [(dq_ref, jnp.concatenate(dqs, axis=1), False), (dk_ref, jnp.concatenate(dks, axis=1), False),
                (dv_ref, jnp.concatenate(dvs, axis=1), False), (ds_s, jnp.concatenate(dss, axis=0), False),
                (dga_ref, _dot_nt(dzb, wg_ref[...]), False), (dwg_ref, _dot_tn(g["gab"], dzb), True),
                (dbg_ref, _rowsum(dz), True)]

    def ins(d):
        col = lambda width, blk: pl.BlockSpec((c, width), lambda n: (cidx(d, n), blk))
        return [col(wk, 0), col(wk, 1), col(wv, 1), col(LANES, OD_GA_BLK),
                pl.BlockSpec((None, LANES, wk), lambda n: (d, 0, 0)), pl.BlockSpec((None, 1, wk), lambda n: (d, 0, 0)),
                pl.BlockSpec((None, wk, GLA_DV), lambda n: (cidx(d, n), 0, 0)), col(wv, 0)]

    def outs(d):
        row = lambda width: pl.BlockSpec((c, width), lambda n: (cidx(d, n), 0))
        return [row(wk), row(wk), row(wv), row(LANES),
                pl.BlockSpec((LANES, wk), lambda n: (0, 0)), pl.BlockSpec((1, wk), lambda n: (0, 0))]

    shapes = [jax.ShapeDtypeStruct((s, wk), F32), jax.ShapeDtypeStruct((s, wk), F32), jax.ShapeDtypeStruct((s, wv), F32),
              jax.ShapeDtypeStruct((s, LANES), F32), jax.ShapeDtypeStruct((LANES, wk), F32),
              jax.ShapeDtypeStruct((1, wk), F32)]
    res = pl.pallas_call(
        body, name="gla_bwd", grid=(n_chunks,),
        in_specs=ins(0) + ins(1), out_specs=outs(0) + outs(1), out_shape=shapes * 2,
        scratch_shapes=[pltpu.VMEM((wk, GLA_DV), F32)] * 2,
        compiler_params=_cp(("arbitrary",)),
    )(*[a for d in range(2) for a in (p_odd, p_odd, p_odd, p_odd, wg2, bg2, s_prev[d], do)])
    pair = lambda k: (res[k], res[6 + k])
    return pair(0), pair(1), pair(2), pair(3), jnp.stack(pair(4)), jnp.stack(pair(5))


HALO = 8


def _halo_specs(width_blk, col0, ts, s):
    r = ts // HALO
    last = s // HALO - 1
    cur = pl.BlockSpec((ts, width_blk), lambda j, i: (i, col0 + j))
    prev = pl.BlockSpec((HALO, width_blk), lambda j, i: (jnp.maximum(i * r - 1, 0), col0 + j))
    nxt = pl.BlockSpec((HALO, width_blk), lambda j, i: (jnp.minimum((i + 1) * r, last), col0 + j))
    return [prev, cur, nxt]


def _with_halo(prev_ref, cur_ref, next_ref, i, n_i):
    p = jnp.where(i == 0, 0.0, prev_ref[...])
    q = jnp.where(i == n_i - 1, 0.0, next_ref[...])
    return jnp.concatenate([p, cur_ref[...], q], axis=0)


def _shift_down(x):
    return pltpu.roll(x, 1, 0)


def _shift_up(x):
    return pltpu.roll(x, x.shape[0] - 1, 0)


def _ffn_act(up, conv_w, conv_b, *, ts):
    s = up.shape[0]
    tc = _tile(D_FF, 1408)
    nj = D_FF // tc
    n_i = s // ts

    def body(gp, gc, gn, val_ref, w_ref, b_ref, a_ref):
        i = pl.program_id(1)
        g = _with_halo(gp, gc, gn, i, n_i)
        w = w_ref[...]
        conv = w[0:1] * _shift_down(g) + w[1:2] * g + w[2:3] * _shift_up(g) + b_ref[...]
        conv = conv[HALO:HALO + ts]
        a_ref[...] = (conv * _sigmoid(conv) * val_ref[...]).astype(a_ref.dtype)

    return pl.pallas_call(
        body, name="ffn_act", grid=(nj, n_i),
        in_specs=_halo_specs(tc, 0, ts, s) + [pl.BlockSpec((ts, tc), lambda j, i: (i, nj + j)),
                                              pl.BlockSpec((3, tc), lambda j, i: (0, j)),
                                              pl.BlockSpec((1, tc), lambda j, i: (0, j))],
        out_specs=pl.BlockSpec((ts, tc), lambda j, i: (i, j)),
        out_shape=jax.ShapeDtypeStruct((s, D_FF), BF16),
        compiler_params=_cp(("parallel", "arbitrary")),
    )(up, up, up, up, conv_w, conv_b)


def _ffn_act_bwd(up, da, conv_w, conv_b, *, ts):
    s = up.shape[0]
    tc = _tile(D_FF, 1408)
    nj = D_FF // tc
    n_i = s // ts

    def body(gp, gc, gn, vp, vc, vn, dp, dc, dn, w_ref, b_ref, dup_ref, dw_ref, db_ref):
        i = pl.program_id(1)
        g = _with_halo(gp, gc, gn, i, n_i)
        v = _with_halo(vp, vc, vn, i, n_i)
        dav = _with_halo(dp, dc, dn, i, n_i)
        w = w_ref[...]
        gm, gpl = _shift_down(g), _shift_up(g)
        conv = w[0:1] * gm + w[1:2] * g + w[2:3] * gpl + b_ref[...]
        sg = _sigmoid(conv)
        dgc = dav * v * (sg * (1.0 + conv * (1.0 - sg)))
        dgate = w[0:1] * _shift_up(dgc) + w[1:2] * dgc + w[2:3] * _shift_down(dgc)
        ctr = slice(HALO, HALO + ts)
        dup_ref[0] = dgate[ctr].astype(dup_ref.dtype)
        dup_ref[1] = (dav[ctr] * (conv * sg)[ctr]).astype(dup_ref.dtype)
        dgc_c = dgc[ctr]
        dw = jnp.concatenate([_rowsum(dgc_c * gm[ctr]), _rowsum(dgc_c * g[ctr]), _rowsum(dgc_c * gpl[ctr])], axis=0)
        dbv = _rowsum(dgc_c)

        @pl.when(i == 0)
        def _():
            dw_ref[...] = dw
            db_ref[...] = dbv

        @pl.when(i > 0)
        def _():
            dw_ref[...] += dw
            db_ref[...] += dbv

    return pl.pallas_call(
        body, name="ffn_act_bwd", grid=(nj, n_i),
        in_specs=(_halo_specs(tc, 0, ts, s) + _halo_specs(tc, nj, ts, s) + _halo_specs(tc, 0, ts, s)
                  + [pl.BlockSpec((3, tc), lambda j, i: (0, j)), pl.BlockSpec((1, tc), lambda j, i: (0, j))]),
        out_specs=[pl.BlockSpec((2, ts, tc), lambda j, i: (0, i, j)),
                   pl.BlockSpec((3, tc), lambda j, i: (0, j)), pl.BlockSpec((1, tc), lambda j, i: (0, j))],
        out_shape=[jax.ShapeDtypeStruct((2, s, D_FF), BF16),
                   jax.ShapeDtypeStruct((3, D_FF), F32), jax.ShapeDtypeStruct((1, D_FF), F32)],
        compiler_params=_cp(("parallel", "arbitrary")),
    )(up, up, up, up, up, up, da, da, da, conv_w, conv_b)


def _loss_head(y, target, *, s, ts):
    def fn(yv, tv):
        err = yv - tv
        return err * (1.0 / D_MODEL), _rowsum(err * err)

    return _ew(fn, [_cols(y, D_MODEL, 0, ts), _cols(target, D_MODEL, 0, ts)], [], [(D_MODEL, F32)],
               [(1, D_MODEL)], s=s, ts=ts, name="loss_head")


def _rows_tile(r, width):
    ts = r
    while ts * width * 4 > (2 << 20) and ts % 16 == 0:
        ts //= 2
    return ts


def _adamw(w, g, m, v, *, ts, name):
    r, width = w.shape
    assert r % ts == 0

    def fn(wv, gv, mv, vv):
        mn = ADAM_B1 * mv + (1.0 - ADAM_B1) * gv
        vn = ADAM_B2 * vv + (1.0 - ADAM_B2) * (gv * gv)
        m_hat = mn / (1.0 - ADAM_B1 ** ADAM_STEP)
        v_hat = vn / (1.0 - ADAM_B2 ** ADAM_STEP)
        delta = -ADAM_LR * (m_hat / (jnp.sqrt(v_hat) + ADAM_EPS) + ADAM_WD * wv)
        return delta, mn, vn

    rows = [_cols(a, width, 0, ts) for a in (w, g, m, v)]
    return _ew(fn, rows, [], [(width, F32)] * 3, s=r, ts=ts, name=name)


def _pad_heads(w, heads, real):
    lead = w.shape[:-1]
    w = w.reshape(lead + (heads, real))
    w = jnp.pad(w, [(0, 0)] * len(lead) + [(0, 0), (0, LANES - real)])
    return w.reshape(lead + (heads * LANES,))


def _pad_head_rows(w, heads, real):
    return _pad_heads(w.T, heads, real).T


def _pack_even(p):
    w_in = p["w_in"]
    z = lambda n: jnp.zeros((D_MODEL, n), w_in.dtype)
    o = 0
    parts = {}
    for nm, n in (("cq", MLA_QR), ("ckv", MLA_KVR), ("kr", MLA_ROPE), ("rq", 512), ("rk", 512), ("rv", 512), ("rg", 512)):
        parts[nm] = w_in[:, o:o + n]
        o += n
    w_in_p = jnp.concatenate(
        [_pad_heads(parts[k], RET_H, RET_DK) for k in ("rq", "rk", "rv", "rg")]
        + [parts["cq"], z(EV_CQ - MLA_QR), parts["ckv"], z(MLA_NOPE), parts["kr"], z(LANES - MLA_QK), z(LANES)], axis=1)
    w_uq = jnp.pad(_pad_heads(p["w_uq"], MLA_H, MLA_QK), ((0, EV_CQ - MLA_QR), (0, 0)))
    ukv = p["w_ukv"].reshape(MLA_KVR, MLA_H, MLA_NOPE + MLA_V)
    w_ukv = jnp.concatenate([_pad_heads(ukv[..., :MLA_NOPE].reshape(MLA_KVR, -1), MLA_H, MLA_NOPE),
                             _pad_heads(ukv[..., MLA_NOPE:].reshape(MLA_KVR, -1), MLA_H, MLA_V)], axis=1)
    w_out = jnp.concatenate([_pad_head_rows(p["w_out"][:MLA_H * MLA_V], MLA_H, MLA_V),
                             _pad_head_rows(p["w_out"][MLA_H * MLA_V:], RET_H, RET_DV)], axis=0)
    return dict(
        w_in=w_in_p, w_uq=w_uq, w_ukv=w_ukv, w_out=w_out,
        mix_g=p["mix_norm"][None, :],
        q_norm=jnp.pad(p["q_norm"], (0, EV_CQ - MLA_QR))[None, :],
        kv_norm=p["kv_norm"][None, :],
        qhn=jnp.pad(p["q_head_norm"], (0, LANES - MLA_QK))[None, :],
        khn=jnp.pad(p["k_head_norm"], (0, LANES - MLA_QK))[None, :],
        ret_gain=_pad_heads(p["ret_out_norm"].reshape(-1), RET_H, RET_DV)[None, :],
    )


def _pack_odd(p):
    w_in = p["w_in"]
    ga = w_in[:, 3072:]
    w_in_p = jnp.concatenate([w_in[:, :3072], ga, jnp.zeros((D_MODEL, LANES - 2 * GLA_R), w_in.dtype)], axis=1)
    wk = GLA_H * GLA_DK
    zf = jnp.zeros((LANES - GLA_R, wk), p["w_gate_fwd"].dtype)
    zb0 = jnp.zeros((GLA_R, wk), p["w_gate_fwd"].dtype)
    zb1 = jnp.zeros((LANES - 2 * GLA_R, wk), p["w_gate_fwd"].dtype)
    wg2 = jnp.stack([jnp.concatenate([p["w_gate_fwd"], zf], axis=0),
                     jnp.concatenate([zb0, p["w_gate_bwd"], zb1], axis=0)])
    bg2 = jnp.stack([p["b_gate_fwd"][None, :], p["b_gate_bwd"][None, :]])
    return dict(w_in=w_in_p, wg2=wg2, bg2=bg2, w_out=p["w_out"], mix_g=p["mix_norm"][None, :],
                gla_gain=p["gla_out_norm"].reshape(1, -1))


_MATRICES = ("w_in", "w_uq", "w_ukv", "w_out", "wg2")


def _packed(pack_fn, p):
    packed = pack_fn(p)
    packed = {k: (_bf(v) if k in _MATRICES else v.astype(F32)) for k, v in packed.items()}
    shapes = {k: jax.ShapeDtypeStruct(v.shape, F32) for k, v in p.items()}
    unpack = jax.linear_transpose(pack_fn, shapes)
    return packed, lambda g: unpack(g)[0]


def _ffn_fwd(x, w, *, s, ts):
    h = _rmsnorm(_cols(x, D_MODEL, 0, ts), w["norm_g"], n=D_MODEL, s=s, ts=ts, name="ffn_norm")
    up = _mm(h, w["w_up4"], b_layer=w["layer"], name="ffn_up")
    a = _ffn_act(up, w["conv_w"], w["conv_b"], ts=ts)
    y = _mm(a, w["w_down"], res=x, name="ffn_down")
    return y, dict(x=x, h=h, up=up, a=a)


def _ffn_bwd(dy, w, sv, *, s, ts):
    da = _mm(dy, w["w_down"], tb=True, name="ffn_down_dx")
    g_down = _mm(sv["a"], dy, ta=True, name="ffn_down_dw")
    dup, g_cw, g_cb = _ffn_act_bwd(sv["up"], da, w["conv_w"], w["conv_b"], ts=min(ts, FFN_ACT_ROWS))
    dh = _mm(dup, w["w_up4"], tb=True, b_layer=w["layer"], halves="a", name="ffn_up_dx")
    g_up = _mm(sv["h"], dup, ta=True, out_chips=True, halves="b", name="ffn_up_dw")
    dx, g_norm = _rmsnorm_bwd(_cols(sv["x"], D_MODEL, 0, ts), w["norm_g"], dh, dy, n=D_MODEL, s=s, ts=ts,
                              name="ffn_norm_bwd")
    return dx, dict(w_up=g_up, w_down=g_down, conv_w=g_cw, conv_b=g_cb, norm_g=g_norm)


def _even_fwd(x, w, tabs, *, s, ts, side=()):
    cos_m, sin_m, cos_r, sin_r = tabs
    h = _rmsnorm(_cols(x, D_MODEL, 0, ts), w["mix_g"], n=D_MODEL, s=s, ts=ts, name="mix_norm")
    p = _mm(h, w["w_in"], name="even_in")
    cqn = _rmsnorm(_cols(p, EV_CQ, EV_RET // EV_CQ, ts), w["q_norm"], n=MLA_QR, s=s, ts=ts, name="mla_q_norm")
    ckvn = _rmsnorm(_cols(p, MLA_KVR, (EV_RET + EV_CQ) // MLA_KVR, ts), w["kv_norm"], n=MLA_KVR, s=s, ts=ts,
                    name="mla_kv_norm")
    q_pre = _mm(cqn, w["w_uq"], name="mla_uq")
    kv_pre = _mm(ckvn, w["w_ukv"], name="mla_ukv")
    q, k, v = _mla_prep(q_pre, kv_pre, p, cos_m, sin_m, w["qhn"], w["khn"], s=s, ts=ts)
    o, lse, gathered = _flash_fwd(q, k, v, tq=min(s, FLASH_FWD_ROWS), tk=min(s, FLASH_KEYS), side=side)
    o2, r_prev = _ret_fwd(p, cos_r, sin_r, w["theta_l"])
    r = _post_fwd(o2, _cols(p, RET_H * LANES, 3, ts), w["ret_gain"], group=LANES, n=RET_DV, s=s, ts=ts,
                  name="ret_post")
    ar = jnp.concatenate([o, r], axis=1)
    y = _mm(ar, w["w_out"], res=x, name="even_out")
    return y, dict(x=x, h=h, p=p, cqn=cqn, ckvn=ckvn, q_pre=q_pre, kv_pre=kv_pre, q=q, k=k, v=v, o=o, lse=lse,
                   o2=o2, r_prev=r_prev, ar=ar), gathered


def _even_bwd(dy, w, sv, tabs, *, s, ts):
    cos_m, sin_m, cos_r, sin_r = tabs
    p = sv["p"]
    wh = MLA_H * LANES
    dar = _mm(dy, w["w_out"], tb=True, name="even_out_dx")
    g_out = _mm(sv["ar"], dy, ta=True, name="even_out_dw")
    do_attn = _attn_bwd_prep(dar, sv["o"], s=s, ts=ts)
    dq, dk, dv = _flash_bwd(sv["q"], sv["k"], sv["v"], do_attn, sv["lse"], tq=min(s, FLASH_BWD_ROWS),
                            tk=min(s, FLASH_KEYS))
    dq_pre, dk_pre, dkr, g_qhn, g_khn = _mla_prep_bwd(sv["q_pre"], sv["kv_pre"], p, cos_m, sin_m, w["qhn"], w["khn"],
                                                      dq, dk, s=s, ts=ts)
    dkv_pre = jnp.concatenate([dk_pre, dv], axis=1)
    dckvn = _mm(dkv_pre, w["w_ukv"], tb=True, name="mla_ukv_dx")
    g_ukv = _mm(sv["ckvn"], dkv_pre, ta=True, name="mla_ukv_dw")
    dcqn = _mm(dq_pre, w["w_uq"], tb=True, name="mla_uq_dx")
    g_uq = _mm(sv["cqn"], dq_pre, ta=True, name="mla_uq_dw")
    dckv, g_kvn = _rmsnorm_bwd(_cols(p, MLA_KVR, (EV_RET + EV_CQ) // MLA_KVR, ts), w["kv_norm"], dckvn, None,
                               n=MLA_KVR, s=s, ts=ts, name="mla_kv_norm_bwd")
    dcq, g_qn = _rmsnorm_bwd(_cols(p, EV_CQ, EV_RET // EV_CQ, ts), w["q_norm"], dcqn, None, n=MLA_QR, s=s, ts=ts,
                             name="mla_q_norm_bwd")
    do, drg, g_gain = _post_bwd(sv["o2"], _cols(p, wh, 3, ts), w["ret_gain"], _cols(dar, wh, 1, ts),
                                group=LANES, n=RET_DV, s=s, ts=ts, name="ret_post_bwd")
    dq2, dk2, dv2, dth = _ret_bwd(p, cos_r, sin_r, w["theta_l"], w["theta_h"], sv["r_prev"], do)
    drq, drk, drv = (_sum2(a, s=s, ts=ts, name="sum_dirs_1024") for a in (dq2, dk2, dv2))
    dp = jnp.concatenate([drq, drk, drv, drg, _bf(dcq), _bf(dckv), dkr, jnp.zeros((s, LANES), BF16)], axis=1)
    dh = _mm(dp, w["w_in"], tb=True, name="even_in_dx")
    g_in = _mm(sv["h"], dp, ta=True, name="even_in_dw")
    dx, g_mix = _rmsnorm_bwd(_cols(sv["x"], D_MODEL, 0, ts), w["mix_g"], dh, dy, n=D_MODEL, s=s, ts=ts,
                             name="mix_norm_bwd")
    grads = dict(w_in=g_in, w_uq=g_uq, w_ukv=g_ukv, w_out=g_out, mix_g=g_mix, q_norm=g_qn, kv_norm=g_kvn,
                 qhn=g_qhn, khn=g_khn, ret_gain=g_gain)
    return dx, grads, dth[:, :, 0]


def _odd_fwd(x, w, *, s, ts):
    h = _rmsnorm(_cols(x, D_MODEL, 0, ts), w["mix_g"], n=D_MODEL, s=s, ts=ts, name="mix_norm")
    p = _mm(h, w["w_in"], name="odd_in")
    o2, s_prev = _gla_fwd(p, w["wg2"], w["bg2"])
    g = _post_fwd(o2, _cols(p, GLA_H * GLA_DV, 2, ts), w["gla_gain"], group=GLA_DV, n=GLA_DV, s=s, ts=ts,
                  name="gla_post")
    y = _mm(g, w["w_out"], res=x, name="odd_out")
    return y, dict(x=x, h=h, p=p, o2=o2, s_prev=s_prev, g=g)


def _odd_bwd(dy, w, sv, *, s, ts):
    p = sv["p"]
    wv = GLA_H * GLA_DV
    dg = _mm(dy, w["w_out"], tb=True, name="odd_out_dx")
    g_out = _mm(sv["g"], dy, ta=True, name="odd_out_dw")
    do, dgr, g_gain = _post_bwd(sv["o2"], _cols(p, wv, 2, ts), w["gla_gain"], _cols(dg, wv, 0, ts),
                                group=GLA_DV, n=GLA_DV, s=s, ts=ts, name="gla_post_bwd")
    dq2, dk2, dv2, dga2, g_wg, g_bg = _gla_bwd(p, w["wg2"], w["bg2"], sv["s_prev"], do)
    dq = _sum2(dq2, s=s, ts=ts, name="sum_dirs_512")
    dk = _sum2(dk2, s=s, ts=ts, name="sum_dirs_512")
    dv = _sum2(dv2, s=s, ts=ts, name="sum_dirs_1024")
    dga = _sum2(dga2, s=s, ts=ts, name="sum_dirs_128")
    dp = jnp.concatenate([dq, dk, dv, dgr, dga], axis=1)
    dh = _mm(dp, w["w_in"], tb=True, name="odd_in_dx")
    g_in = _mm(sv["h"], dp, ta=True, name="odd_in_dw")
    dx, g_mix = _rmsnorm_bwd(_cols(sv["x"], D_MODEL, 0, ts), w["mix_g"], dh, dy, n=D_MODEL, s=s, ts=ts,
                             name="mix_norm_bwd")
    return dx, dict(w_in=g_in, wg2=g_wg, bg2=g_bg, w_out=g_out, mix_g=g_mix, gla_gain=g_gain)


_EVEN_NAMES = dict(mix_norm="mix_norm_even", w_in="w_in_even", q_norm="mla_q_norm", kv_norm="mla_kv_norm",
                   w_uq="mla_w_uq", w_ukv="mla_w_ukv", q_head_norm="mla_q_head_norm", k_head_norm="mla_k_head_norm",
                   ret_out_norm="ret_out_norm", w_out="w_out_even")
_ODD_NAMES = dict(mix_norm="mix_norm_odd", w_in="w_in_odd", w_gate_fwd="gla_w_gate_fwd", b_gate_fwd="gla_b_gate_fwd",
                  w_gate_bwd="gla_w_gate_bwd", b_gate_bwd="gla_b_gate_bwd", gla_out_norm="gla_out_norm",
                  w_out="w_out_odd")

def _local_step(x, pos, target, full, side=(), finish=None):
    s = x.shape[0]
    ts = min(s, EW_ROWS)
    tabs = _rope_tables(pos, MLA_ROPE, MLA_NOPE) + _rope_tables(pos, RET_DK, 0)

    def layer_weights(layer):
        i = layer // 2
        names = _EVEN_NAMES if layer % 2 == 0 else _ODD_NAMES
        wm, unpack_m = _packed(_pack_even if layer % 2 == 0 else _pack_odd, {k: full[n][i] for k, n in names.items()})
        if layer % 2 == 0:
            th = jnp.stack([full["ret_theta_fwd"][i], full["ret_theta_bwd"][i]]).astype(F32)
            wm["theta_h"] = jnp.broadcast_to(th[:, :, None], (2, RET_H, LANES))
            wm["theta_l"] = wm["theta_h"].reshape(2, 1, RET_H * LANES)
        w_up4, index = full["ffn_w_up"][layer]
        wf = dict(layer=index, w_up4=w_up4, w_down=_bf(full["ffn_w_down"][layer]),
                  conv_w=full["ffn_conv_w"][layer].astype(F32), conv_b=full["ffn_conv_b"][layer][None, :].astype(F32),
                  norm_g=full["ffn_norm"][layer][None, :].astype(F32))
        return wm, unpack_m, wf

    layers, saved = [], []
    for layer in range(DEPTH):
        layers.append(layer_weights(layer))
        wm, _, wf = layers[-1]
        if layer % 2 == 0:
            x, sv_m, gathered = _even_fwd(x, wm, tabs, s=s, ts=ts, side=side if layer == 0 else ())
            if layer == 0 and finish is not None:
                full = finish(gathered)
        else:
            x, sv_m = _odd_fwd(x, wm, s=s, ts=ts)
        x, sv_f = _ffn_fwd(x, wf, s=s, ts=ts)
        saved.append((sv_m, sv_f))

    dy, sq = _loss_head(x, target, s=s, ts=ts)
    loss = 0.5 / D_MODEL * jnp.sum(sq)

    grads = {}

    def put(name, idx, g):
        grads.setdefault(name, {})[idx] = g

    for layer in reversed(range(DEPTH)):
        wm, unpack_m, wf = layers[layer]
        sv_m, sv_f = saved[layer]
        i = layer // 2
        dy, gf = _ffn_bwd(dy, wf, sv_f, s=s, ts=ts)
        put("ffn_w_up", layer, gf["w_up"])
        put("ffn_w_down", layer, gf["w_down"])
        put("ffn_conv_w", layer, gf["conv_w"])
        put("ffn_conv_b", layer, gf["conv_b"][0])
        put("ffn_norm", layer, gf["norm_g"][0])
        if layer % 2 == 0:
            dy, gm, dth = _even_bwd(dy, wm, sv_m, tabs, s=s, ts=ts)
            put("ret_theta_fwd", i, dth[0])
            put("ret_theta_bwd", i, dth[1])
            names = _EVEN_NAMES
        else:
            dy, gm = _odd_bwd(dy, wm, sv_m, s=s, ts=ts)
            names = _ODD_NAMES
        for k, g in unpack_m(gm).items():
            put(names[k], i, g)
    return loss, dy, {n: [g[j] for j in range(len(g))] for n, g in grads.items()}


HBM_SPEC = pl.BlockSpec(memory_space=pltpu.HBM)
VMEM_SPEC = pl.BlockSpec(memory_space=pltpu.VMEM)
CHIPS = 4
CORES = 2
ROW = 8 * LANES


def _xyc():
    return lax.axis_index("x"), lax.axis_index("y"), lax.axis_index("c")


def _other_chips(x, y):
    return [(1 - x, y), (x, 1 - y), (1 - x, 1 - y)]


def _remote(src, dst, send, recv, dev):
    return pltpu.make_async_remote_copy(src_ref=src, dst_ref=dst, send_sem=send, recv_sem=recv,
                                        device_id=dev, device_id_type=MESH)


def _sems(n):
    return pltpu.SemaphoreType.DMA((n,))


def _gather_copies(side, srcs, lands, send, recv, loc):
    n = len(side)
    x, y, c = _xyc()
    me = 2 * x + y
    local, sends, arrivals = [], [], []
    for t, (_, first, count) in enumerate(side):
        src = srcs[t].at[pl.ds(first, count)]
        local.append(pltpu.make_async_copy(src, lands[t].at[me], loc.at[t]))
        for j, (px, py) in enumerate(_other_chips(x, y)):
            k = n * j + t
            sends.append(_remote(src, lands[t].at[me], send.at[k], recv.at[k], (px, py, c)))
            arrivals.append(_remote(src, lands[t].at[2 * px + py], send.at[k], recv.at[k], (px, py, c)))
    return local, sends, arrivals


def _gather_shapes(side):
    return [jax.ShapeDtypeStruct((CHIPS, count) + a.shape[1:], a.dtype) for a, _, count in side]


def _gather_chips(side):
    n = len(side)

    def body(*refs):
        local, sends, arrivals = _gather_copies(side, refs[:n], refs[n:2 * n], *refs[2 * n:])
        for cp in local + sends:
            cp.start()
        for cp in arrivals:
            cp.wait_recv()
        for cp in sends:
            cp.wait_send()
        for cp in local:
            cp.wait()

    return pl.pallas_call(
        body, name="gather_chips", in_specs=[HBM_SPEC] * n, out_specs=[HBM_SPEC] * n,
        out_shape=_gather_shapes(side),
        scratch_shapes=[_sems(3 * n), _sems(3 * n), _sems(n)],
    )(*[a for a, _, _ in side])


def _half_rows(ref, axis, half, which):
    idx = (slice(None),) * axis + (pl.ds(pl.multiple_of(which * half, 8), half),)
    return ref.at[idx]


def _swap_halves(arrs):
    n = len(arrs)

    def body(*refs):
        ins, outs = refs[:n], refs[n:2 * n]
        send, recv = refs[2 * n:]
        x, y, c = _xyc()
        copies = []
        for t in range(n):
            half = arrs[t].shape[2] // CORES
            cp = _remote(_half_rows(ins[t], 2, half, 1 - c), outs[t], send.at[t], recv.at[t], (x, y, 1 - c))
            cp.start()
            copies.append(cp)
        for cp in copies:
            cp.wait()

    return pl.pallas_call(
        body, name="swap_halves", in_specs=[HBM_SPEC] * n, out_specs=[HBM_SPEC] * n,
        out_shape=[jax.ShapeDtypeStruct(a.shape[:2] + (a.shape[2] // CORES, a.shape[3]), a.dtype) for a in arrs],
        scratch_shapes=[_sems(n), _sems(n)],
    )(*arrs)


def _add_core_halves(a, got, core, *, ts, name):
    ch, nl, r, cols = a.shape
    half = r // CORES
    nb = half // ts

    def body(core_ref, a_ref, g_ref, o_ref):
        o_ref[...] = (a_ref[...] + g_ref[...]).astype(o_ref.dtype)

    rows = pl.BlockSpec((ts, cols), lambda g, i, cr: (g * nb + i, 0))
    return pl.pallas_call(
        body, name=name, out_shape=jax.ShapeDtypeStruct((ch * nl * half, cols), BF16),
        grid_spec=pltpu.PrefetchScalarGridSpec(
            num_scalar_prefetch=1, grid=(ch * nl, nb),
            in_specs=[pl.BlockSpec((ts, cols), lambda g, i, cr: (g * (r // ts) + cr[0] * nb + i, 0)), rows],
            out_specs=rows),
        compiler_params=_cp(("arbitrary", "arbitrary")),
    )(core, a.reshape(-1, cols), got.reshape(-1, cols)).reshape(got.shape)


def _add_chip_parts(parts, core, *, ts, name):
    ch, nl, half, cols = parts.shape
    nb = half // ts
    r = half * CORES

    def body(core_ref, *refs):
        acc = refs[0][...].astype(F32)
        for p in refs[1:ch]:
            acc = acc + p[...].astype(F32)
        refs[ch][...] = acc

    return pl.pallas_call(
        body, name=name, out_shape=jax.ShapeDtypeStruct((nl * r, cols), F32),
        grid_spec=pltpu.PrefetchScalarGridSpec(
            num_scalar_prefetch=1, grid=(nl, nb),
            in_specs=[pl.BlockSpec((ts, cols), lambda l, i, cr, j=j: ((j * nl + l) * nb + i, 0)) for j in range(ch)],
            out_specs=pl.BlockSpec((ts, cols), lambda l, i, cr: (l * (r // ts) + cr[0] * nb + i, 0))),
        compiler_params=_cp(("arbitrary", "arbitrary")),
    )(core, *[parts.reshape(-1, cols)] * ch).reshape(nl, r, cols)


def _scatter_chips(arrs):
    n = len(arrs)

    def body(*refs):
        ins, outs = refs[:n], refs[n:2 * n]
        send, recv, loc = refs[2 * n:]
        x, y, c = _xyc()
        me = 2 * x + y
        copies = []
        for t in range(n):
            cp = pltpu.make_async_copy(ins[t].at[me], outs[t].at[me], loc.at[t])
            cp.start()
            copies.append(cp)
        sends = []
        for j, (px, py) in enumerate(_other_chips(x, y)):
            for t in range(n):
                cp = _remote(ins[t].at[2 * px + py], outs[t].at[me], send.at[n * j + t], recv.at[n * j + t], (px, py, c))
                cp.start()
                sends.append(cp)
        for j, (px, py) in enumerate(_other_chips(x, y)):
            for t in range(n):
                _remote(ins[t].at[me], outs[t].at[2 * px + py], send.at[n * j + t], recv.at[n * j + t],
                        (px, py, c)).wait_recv()
        for cp in sends:
            cp.wait_send()
        for cp in copies:
            cp.wait()

    return pl.pallas_call(
        body, name="scatter_chips", in_specs=[HBM_SPEC] * n, out_specs=[HBM_SPEC] * n,
        out_shape=[jax.ShapeDtypeStruct(a.shape, a.dtype) for a in arrs],
        scratch_shapes=[_sems(3 * n), _sems(3 * n), _sems(n)],
    )(*arrs)


def _gather_cores(arrs):
    n = len(arrs)

    def body(*refs):
        ins, outs = refs[:n], refs[n:2 * n]
        send, recv = refs[2 * n:]
        x, y, c = _xyc()
        sends = []
        for t in range(n):
            half = arrs[t].shape[1] // CORES
            cp = _remote(_half_rows(ins[t], 1, half, c), _half_rows(outs[t], 1, half, c), send.at[t], recv.at[t],
                         (x, y, 1 - c))
            cp.start()
            sends.append(cp)
        for t in range(n):
            half = arrs[t].shape[1] // CORES
            _remote(_half_rows(ins[t], 1, half, 1 - c), _half_rows(outs[t], 1, half, 1 - c), send.at[t], recv.at[t],
                    (x, y, 1 - c)).wait_recv()
        for cp in sends:
            cp.wait_send()

    return pl.pallas_call(
        body, name="gather_cores", in_specs=[HBM_SPEC] * n, out_specs=[HBM_SPEC] * n,
        out_shape=[jax.ShapeDtypeStruct(a.shape, a.dtype) for a in arrs],
        input_output_aliases={t: t for t in range(n)},
        scratch_shapes=[_sems(n), _sems(n)],
    )(*arrs)


def _all_reduce_devices(v):
    n_dev = CHIPS * CORES

    def body(v_ref, o_ref, buf, send, recv):
        x, y, c = _xyc()
        me = 4 * x + 2 * y + c
        buf[pl.ds(me, 1)] = v_ref[...][None]
        sends = []
        for m in range(1, n_dev):
            px = 1 - x if m & 4 else x
            py = 1 - y if m & 2 else y
            pc = 1 - c if m & 1 else c
            cp = _remote(v_ref, buf.at[me], send.at[m - 1], recv.at[m - 1], (px, py, pc))
            cp.start()
            sends.append((cp, 4 * px + 2 * py + pc))
        for m, (cp, peer) in enumerate(sends):
            _remote(v_ref, buf.at[peer], send.at[m], recv.at[m], (x, y, c)).wait_recv()
        for cp, _ in sends:
            cp.wait_send()
        acc = buf[0]
        for k in range(1, n_dev):
            acc = acc + buf[k]
        o_ref[...] = acc

    return pl.pallas_call(
        body, name="all_reduce_devices", in_specs=[VMEM_SPEC], out_specs=VMEM_SPEC,
        out_shape=jax.ShapeDtypeStruct(v.shape, F32),
        scratch_shapes=[pltpu.VMEM((n_dev,) + v.shape, F32), pltpu.SemaphoreType.DMA((n_dev - 1,)),
                        pltpu.SemaphoreType.DMA((n_dev - 1,))],
    )(v)


_SHARDED = (("w_in_even", 2), ("mla_w_uq", 2), ("mla_w_ukv", 2), ("w_out_even", 1), ("w_in_odd", 2), ("w_out_odd", 1),
            ("ffn_w_up", 2), ("ffn_w_down", 1),
            ("mix_norm_odd", 1), ("gla_w_gate_fwd", 2), ("gla_b_gate_fwd", 1), ("gla_w_gate_bwd", 2),
            ("gla_b_gate_bwd", 1), ("gla_out_norm", 2), ("ffn_conv_w", 2))
_N_MATRICES = 8
_REPLICATED = ("mix_norm_even", "mla_q_norm", "mla_kv_norm", "mla_q_head_norm", "mla_k_head_norm", "ret_theta_fwd",
               "ret_theta_bwd", "ret_out_norm", "ffn_norm", "ffn_conv_b")
_WEIGHTS = ("mix_norm_even", "w_in_even", "mla_q_norm", "mla_kv_norm", "mla_w_uq", "mla_w_ukv", "mla_q_head_norm",
            "mla_k_head_norm", "ret_theta_fwd", "ret_theta_bwd", "ret_out_norm", "w_out_even", "mix_norm_odd",
            "w_in_odd", "gla_w_gate_fwd", "gla_b_gate_fwd", "gla_w_gate_bwd", "gla_b_gate_bwd", "gla_out_norm",
            "w_out_odd", "ffn_norm", "ffn_w_up", "ffn_conv_w", "ffn_conv_b", "ffn_w_down")


def _flatten(arrs, row_multiple, dtype):
    flat = jnp.concatenate([a.reshape(-1).astype(dtype) for a in arrs])
    per = ROW * row_multiple
    total = -(-flat.shape[0] // per) * per
    return jnp.pad(flat, (0, total - flat.shape[0])).reshape(-1, ROW)


def _unflatten(flat, shapes):
    flat = flat.reshape(-1)
    out, o = [], 0
    for shp in shapes:
        n = math.prod(shp)
        out.append(flat[o:o + n].reshape(shp))
        o += n
    return out


def kernel(x, positions, mix_norm_even, w_in_even, mla_q_norm, mla_kv_norm, mla_w_uq, mla_w_ukv, mla_q_head_norm, mla_k_head_norm, ret_theta_fwd, ret_theta_bwd, ret_out_norm, w_out_even, mix_norm_odd, w_in_odd, gla_w_gate_fwd, gla_b_gate_fwd, gla_w_gate_bwd, gla_b_gate_bwd, gla_out_norm, w_out_odd, ffn_norm, ffn_w_up, ffn_conv_w, ffn_conv_b, ffn_w_down, loss_target, m_mix_norm_even, m_w_in_even, m_mla_q_norm, m_mla_kv_norm, m_mla_w_uq, m_mla_w_ukv, m_mla_q_head_norm, m_mla_k_head_norm, m_ret_theta_fwd, m_ret_theta_bwd, m_ret_out_norm, m_w_out_even, m_mix_norm_odd, m_w_in_odd, m_gla_w_gate_fwd, m_gla_b_gate_fwd, m_gla_w_gate_bwd, m_gla_b_gate_bwd, m_gla_out_norm, m_w_out_odd, m_ffn_norm, m_ffn_w_up, m_ffn_conv_w, m_ffn_conv_b, m_ffn_w_down, v_mix_norm_even, v_w_in_even, v_mla_q_norm, v_mla_kv_norm, v_mla_w_uq, v_mla_w_ukv, v_mla_q_head_norm, v_mla_k_head_norm, v_ret_theta_fwd, v_ret_theta_bwd, v_ret_out_norm, v_w_out_even, v_mix_norm_odd, v_w_in_odd, v_gla_w_gate_fwd, v_gla_b_gate_fwd, v_gla_w_gate_bwd, v_gla_b_gate_bwd, v_gla_out_norm, v_w_out_odd, v_ffn_norm, v_ffn_w_up, v_ffn_conv_w, v_ffn_conv_b, v_ffn_w_down):
    args = dict(locals())
    x2, pos, target = args["x"][0], args["positions"][0], args["loss_target"][0]
    axis = dict(_SHARDED)
    mats = [n for n, _ in _SHARDED[:_N_MATRICES]]
    smalls = [n for n, _ in _SHARDED[_N_MATRICES:]]
    small_shapes = [args[n].shape for n in smalls]

    local = {n: _bf(args[n]) for n in mats}
    first_layers = {n: (0, 0 if n.endswith("_odd") else 1) for n in mats}
    now = [(local[n],) + first_layers[n] for n in mats if first_layers[n][1]]
    later = [(local[n], first_layers[n][1], args[n].shape[0] - first_layers[n][1]) for n in mats]
    small_block = _flatten([args[n] for n in smalls], 2 * HALO, F32)
    got_now = _gather_chips(now + [(small_block, 0, small_block.shape[0])])
    per_chip = [_unflatten(got_now[-1][j], small_shapes) for j in range(CHIPS)]
    base = {n: args[n] for n in _REPLICATED}
    for k, n in enumerate(smalls):
        base[n] = jnp.concatenate([per_chip[j][k] for j in range(CHIPS)], axis=axis[n])

    def whole(stacks):
        full = dict(base)
        for n, per_layer in stacks.items():
            if n == "ffn_w_up":
                full[n] = per_layer
            else:
                full[n] = [None if st is None else jnp.concatenate([st[j, l] for j in range(CHIPS)], axis=axis[n] - 1)
                           for st, l in per_layer]
        return full

    stacks = {n: [(None, 0)] * args[n].shape[0] for n in mats}
    for (a, first, count), st in zip(now, got_now):
        n = next(m for m in mats if local[m] is a)
        stacks[n] = [(st, l) for l in range(count)] + stacks[n][count:]

    def finish(got_later):
        for (a, first, count), st in zip(later, got_later):
            n = next(m for m in mats if local[m] is a)
            stacks[n] = stacks[n][:first] + [(st, l) for l in range(count)]
        return whole(stacks)

    loss, grad_x, grads = _local_step(x2, pos, target, whole(stacks), side=later, finish=finish)
    loss = lax.psum(loss, ("x", "y", "c"))

    def by_chip(n, g):
        if n == "ffn_w_up":
            return g
        if axis[n] == 1:
            return g.reshape((CHIPS, g.shape[0] // CHIPS) + g.shape[1:])
        return jnp.stack(jnp.split(g, CHIPS, axis=axis[n] - 1))

    core = lax.axis_index("c").astype(jnp.int32).reshape(1)
    stacked = [jnp.stack([by_chip(n, g) for g in grads[n]], axis=1) for n in mats]
    small_parts = [jnp.split(jnp.stack(grads[n]), CHIPS, axis=axis[n]) for n in smalls]
    stacked.append(jnp.stack([_flatten([p[j] for p in small_parts], 2 * HALO, F32) for j in range(CHIPS)])[:, None])
    names = mats + ["small"]
    tiles = [_rows_tile(a.shape[2] // CORES, a.shape[3]) for a in stacked]
    got = _swap_halves(stacked)
    chip_sums = [_add_core_halves(a, b, core, ts=ts, name="add_core_halves_" + n)
                 for n, a, b, ts in zip(names, stacked, got, tiles)]
    parts = _scatter_chips(chip_sums)
    sums = [_add_chip_parts(p, core, ts=ts, name="add_chip_parts_" + n) for n, p, ts in zip(names, parts, tiles)]
    reduced = _gather_cores(sums)

    res = {}

    def update(n, w, g, m, v, ts):
        cols = g.shape[-1]
        outs = _adamw(w.reshape(-1, cols), g.reshape(-1, cols), m.reshape(-1, cols), v.reshape(-1, cols), ts=ts,
                      name="adamw_" + n)
        return [g] + [o.reshape(g.shape) for o in outs]

    kinds = ("grad", "delta", "new_m", "new_v")
    for n, g, ts in zip(mats, reduced, tiles):
        for kind, a in zip(kinds, update(n, args[n], g, args["m_" + n], args["v_" + n], ts)):
            res[kind + "_" + n] = a
    w_s, m_s, v_s = (_flatten([args[pre + n] for n in smalls], 2 * HALO, F32) for pre in ("", "m_", "v_"))
    for kind, flat in zip(kinds, update("small", w_s, reduced[-1][0], m_s, v_s, tiles[-1])):
        for n, a in zip(smalls, _unflatten(flat, small_shapes)):
            res[kind + "_" + n] = a

    rep_shapes = [args[n].shape for n in _REPLICATED]
    g_rep = _all_reduce_devices(_flatten([jnp.stack(grads[n]) for n in _REPLICATED], HALO, F32))
    w_rep, m_rep, v_rep = (_flatten([args[pre + n] for n in _REPLICATED], HALO, F32) for pre in ("", "m_", "v_"))
    for kind, flat in zip(kinds, update("replicated", w_rep, g_rep, m_rep, v_rep, g_rep.shape[0])):
        for n, a in zip(_REPLICATED, _unflatten(flat, rep_shapes)):
            res[kind + "_" + n] = a

    outs = [loss, grad_x[None]]
    for kind in ("grad", "delta", "new_m", "new_v"):
        outs += [res[kind + "_" + n] for n in _WEIGHTS]
    return tuple(outs)
```

```python
import math

import jax
import jax.numpy as jnp
from jax import lax
from jax.experimental import pallas as pl
from jax.experimental.pallas import tpu as pltpu

F32 = jnp.float32
BF16 = jnp.bfloat16
MESH = pl.DeviceIdType.MESH

EPS = 1e-6
D_MODEL = 1024
DEPTH = 4
LANES = 128
MLA_H, MLA_QR, MLA_KVR, MLA_NOPE, MLA_ROPE, MLA_V = 8, 384, 256, 64, 32, 64
MLA_QK = MLA_NOPE + MLA_ROPE
MLA_SCALE = MLA_QK ** -0.5
RET_H, RET_DK, RET_DV, RET_C = 8, 64, 64, 128
GLA_H, GLA_DK, GLA_DV, GLA_R, GLA_TAU, GLA_C = 4, 128, 256, 16, 16.0, 64
D_FF = 2816
ROPE_THETA = 10000.0
LN2 = math.log(2.0)
ADAM_LR, ADAM_B1, ADAM_B2, ADAM_EPS, ADAM_WD, ADAM_STEP = 0.001, 0.9, 0.999, 1e-08, 0.01, 10

EV_RET = 4 * RET_H * LANES
EV_CQ = 512
EV_W = 5120
EV_KR_BLK = (EV_RET + EV_CQ + MLA_KVR) // LANES
OD_W = 3200
OD_GA_BLK = 3072 // LANES

VMEM_LIMIT = 56 * 1024 * 1024
MM_TILE_CAP = 1408
EW_ROWS = 512
FFN_ACT_ROWS = 512
V_ONES = (MLA_V, MLA_V + 1)
FLASH_FWD_ROWS = 1024
FLASH_BWD_ROWS = 1024
FLASH_KEYS = 1024


def _cp(sem):
    return pltpu.CompilerParams(dimension_semantics=sem, vmem_limit_bytes=VMEM_LIMIT)


def _dot(a, b):
    return jnp.dot(a, b, preferred_element_type=F32)


def _dot_nt(a, b):
    return lax.dot_general(a, b, (((1,), (1,)), ((), ())), preferred_element_type=F32)


def _dot_tn(a, b):
    return lax.dot_general(a, b, (((0,), (0,)), ((), ())), preferred_element_type=F32)


def _bf(x):
    return x.astype(BF16)


def _split3(x):
    h1 = _bf(x)
    r1 = x - h1.astype(F32)
    h2 = _bf(r1)
    h3 = _bf(r1 - h2.astype(F32))
    return h1, h2, h3


def _tile(n, cap):
    if n <= cap:
        return n
    best = None
    for t in range(LANES, cap + 1, LANES):
        if n % t == 0:
            best = t
    assert best is not None, n
    return best


def _mm(a, b, *, ta=False, tb=False, res=None, out_dtype=F32, b_layer=None, out_chips=False, halves=None, name):
    assert not (ta and tb)
    if halves == "a":
        assert not ta
        m, kdim = a.shape[1], 2 * a.shape[2]
    elif ta:
        kdim, m = a.shape
    else:
        m, kdim = a.shape
    if b_layer is not None:
        rows_b, cols_b = b.shape[2], b.shape[0] * b.shape[3]
    elif halves == "b":
        assert not tb
        rows_b, cols_b = b.shape[1], 2 * b.shape[2]
    else:
        rows_b, cols_b = b.shape
    n, kb = (rows_b, cols_b) if tb else (cols_b, rows_b)
    assert kb == kdim, (a.shape, b.shape, ta, tb)
    tm, tn, tk = _tile(m, MM_TILE_CAP), _tile(n, MM_TILE_CAP), _tile(kdim, MM_TILE_CAP)
    nk = kdim // tk
    has_res = res is not None
    vmem = (2 * tm * tk * a.dtype.itemsize + 2 * tk * tn * b.dtype.itemsize
            + 2 * tm * tn * jnp.dtype(out_dtype).itemsize + (2 * tm * tn * 4 if has_res else 0)
            + (tm * tn * 4 if nk > 1 else 0))
    assert vmem <= VMEM_LIMIT - 8 * 1024 * 1024, (name, vmem)
    a_spec = (pl.BlockSpec((tk, tm), lambda i, j, k: (k, i)) if ta
              else pl.BlockSpec((tm, tk), lambda i, j, k: (i, k)))
    if halves == "a":
        per_half = a.shape[2] // tk
        a_spec = pl.BlockSpec((None, tm, tk), lambda i, j, k: (k // per_half, i, k % per_half))
    if halves == "b":
        per_half = b.shape[2] // tn
        b_spec = pl.BlockSpec((None, tk, tn), lambda i, j, k: (j // per_half, k, j % per_half))
    elif b_layer is not None:
        per_chip = b.shape[3]
        if tb:
            assert tk == per_chip
            b_spec = pl.BlockSpec((None, None, tn, tk), lambda i, j, k: (k, b_layer, j, 0))
        else:
            assert tn == per_chip
            b_spec = pl.BlockSpec((None, None, tk, tn), lambda i, j, k: (j, b_layer, k, 0))
    else:
        b_spec = (pl.BlockSpec((tn, tk), lambda i, j, k: (j, k)) if tb
                  else pl.BlockSpec((tk, tn), lambda i, j, k: (k, j)))
    if out_chips:
        assert n // tn == CHIPS and not has_res
        o_spec = pl.BlockSpec((None, tm, tn), lambda i, j, k: (j, i, 0))
        out_struct = jax.ShapeDtypeStruct((CHIPS, m, tn), out_dtype)
    else:
        o_spec = pl.BlockSpec((tm, tn), lambda i, j, k: (i, j))
        out_struct = jax.ShapeDtypeStruct((m, n), out_dtype)

    def product(a_ref, b_ref):
        av, bv = _bf(a_ref[...]), _bf(b_ref[...])
        if ta:
            return _dot_tn(av, bv)
        if tb:
            return _dot_nt(av, bv)
        return _dot(av, bv)

    def body(*refs):
        a_ref, b_ref = refs[:2]
        r_ref = refs[2] if has_res else None
        o_ref = refs[3] if has_res else refs[2]

        def finish(r):
            if has_res:
                r = r + r_ref[...]
            o_ref[...] = r.astype(o_ref.dtype)

        if nk == 1:
            finish(product(a_ref, b_ref))
            return
        acc = refs[-1]
        k = pl.program_id(2)

        @pl.when(k == 0)
        def _():
            acc[...] = product(a_ref, b_ref)

        @pl.when(k > 0)
        def _():
            acc[...] += product(a_ref, b_ref)

        @pl.when(k == nk - 1)
        def _():
            finish(acc[...])

    ins = [a, b] + ([res] if has_res else [])
    in_specs = [a_spec, b_spec] + ([o_spec] if has_res else [])
    return pl.pallas_call(
        body, name=name, grid=(m // tm, n // tn, nk),
        in_specs=in_specs, out_specs=o_spec, out_shape=out_struct,
        scratch_shapes=[pltpu.VMEM((tm, tn), F32)] if nk > 1 else [],
        compiler_params=_cp(("parallel", "parallel", "arbitrary")),
    )(*ins)


def _ew(fn, rows, pars, outs, accs=(), *, s, ts, name):
    n_in = len(rows) + len(pars)
    n_o = len(outs)

    def body(*refs):
        i = pl.program_id(0)
        vals = fn(*[r[...] for r in refs[:n_in]])
        if not isinstance(vals, (tuple, list)):
            vals = (vals,)
        assert len(vals) == n_o + len(accs), (name, len(vals))
        for r, v in zip(refs[n_in:n_in + n_o], vals[:n_o]):
            r[...] = v.astype(r.dtype)
        for r, v in zip(refs[n_in + n_o:], vals[n_o:]):
            @pl.when(i == 0)
            def _(r=r, v=v):
                r[...] = v

            @pl.when(i > 0)
            def _(r=r, v=v):
                r[...] += v

    in_specs = [sp for _, sp in rows]
    in_specs += [pl.BlockSpec(p.shape, lambda i, nd=p.ndim: (0,) * nd) for p in pars]
    out_specs = [pl.BlockSpec((ts, w), lambda i: (i, 0)) for w, _ in outs]
    out_specs += [pl.BlockSpec((r, w), lambda i: (0, 0)) for r, w in accs]
    out_shape = [jax.ShapeDtypeStruct((s, w), dt) for w, dt in outs]
    out_shape += [jax.ShapeDtypeStruct((r, w), F32) for r, w in accs]
    return pl.pallas_call(
        body, name=name, grid=(s // ts,), in_specs=in_specs, out_specs=out_specs, out_shape=out_shape,
        compiler_params=_cp(("arbitrary",)),
    )(*[a for a, _ in rows], *pars)


def _cols(arr, width, blk, ts):
    return (arr, pl.BlockSpec((ts, width), lambda i, b=blk: (i, b)))


def _lead(pair, d, ts):
    return _cols(pair[d], pair[d].shape[1], 0, ts)


def _rowsum(x):
    return jnp.sum(x, axis=0, keepdims=True)


def _lanesum(x):
    return jnp.sum(x, axis=-1, keepdims=True)


def _gsum(x, group):
    w = x.shape[-1]
    if group == w:
        return jnp.broadcast_to(_lanesum(x), x.shape)
    parts = [jnp.broadcast_to(_lanesum(x[:, g:g + group]), (x.shape[0], group)) for g in range(0, w, group)]
    return jnp.concatenate(parts, axis=-1)


def _gn(x, gain, group, n):
    rstd = lax.rsqrt(_gsum(x * x, group) * (1.0 / n) + EPS)
    xn = x * rstd
    return xn * gain, xn, rstd


def _gn_bwd(dy, xn, rstd, gain, group, n):
    dxn = dy * gain
    dx = rstd * (dxn - xn * (_gsum(dxn * xn, group) * (1.0 / n)))
    return dx, _rowsum(dy * xn)


def _sigmoid(x):
    return 1.0 / (1.0 + jnp.exp(-x))


def _rmsnorm(x_row, g, *, n, s, ts, name):
    w = g.shape[-1]

    def fn(x, gv):
        return _gn(x, gv, w, n)[0]

    return _ew(fn, [x_row], [g], [(w, BF16)], s=s, ts=ts, name=name)[0]


def _rmsnorm_bwd(x_row, g, dh, dres, *, n, s, ts, name):
    w = g.shape[-1]
    has_res = dres is not None

    def fn(x, dhv, *rest):
        gv = rest[-1]
        _, xn, rstd = _gn(x, gv, w, n)
        dx, dg = _gn_bwd(dhv, xn, rstd, gv, w, n)
        if has_res:
            dx = dx + rest[0]
        return dx, dg

    rows = [x_row, _cols(dh, w, 0, ts)] + ([_cols(dres, w, 0, ts)] if has_res else [])
    return _ew(fn, rows, [g], [(w, F32)], [(1, w)], s=s, ts=ts, name=name)


def _rope_tables(pos, real, offset):
    half = real // 2
    inv = ROPE_THETA ** (-jnp.arange(half, dtype=F32) / half)
    ang = pos.astype(F32)[:, None] * inv
    c, sn = jnp.cos(ang), jnp.sin(ang)
    s = pos.shape[0]
    cos_t = jnp.concatenate([jnp.ones((s, offset), F32), c, c,
                             jnp.ones((s, LANES - offset - real), F32)], axis=1)
    sin_t = jnp.concatenate([jnp.zeros((s, offset), F32), -sn, sn,
                             jnp.zeros((s, LANES - offset - real), F32)], axis=1)
    return cos_t, sin_t


def _rope(x, cos_t, sin_t, real, offset):
    half = real // 2
    lane = lax.broadcasted_iota(jnp.int32, x.shape, 1)
    partner = jnp.where(lane < offset + half, pltpu.roll(x, LANES - half, 1), pltpu.roll(x, half, 1))
    return x * cos_t + partner * sin_t


def _mla_prep(q_pre, kv_pre, p_even, cos_m, sin_m, qhn, khn, *, s, ts):
    w = MLA_H * LANES

    def fn(qp, kp, vp, kr, c, sn, gq, gk):
        qs, ks = [], []
        for h in range(MLA_H):
            sl = slice(h * LANES, (h + 1) * LANES)
            qn = _gn(qp[:, sl], gq, LANES, MLA_QK)[0]
            kn = _gn(kp[:, sl] + kr, gk, LANES, MLA_QK)[0]
            qs.append(_rope(qn, c, sn, MLA_ROPE, MLA_NOPE) * MLA_SCALE)
            ks.append(_rope(kn, c, sn, MLA_ROPE, MLA_NOPE))
        lane = lax.broadcasted_iota(jnp.int32, vp.shape, 1) % LANES
        ones = (lane == V_ONES[0]) | (lane == V_ONES[1])
        return jnp.concatenate(qs, axis=1), jnp.concatenate(ks, axis=1), jnp.where(ones, 1.0, vp)

    rows = [_cols(q_pre, w, 0, ts), _cols(kv_pre, w, 0, ts), _cols(kv_pre, w, 1, ts),
            _cols(p_even, LANES, EV_KR_BLK, ts), _cols(cos_m, LANES, 0, ts), _cols(sin_m, LANES, 0, ts)]
    return _ew(fn, rows, [qhn, khn], [(w, BF16)] * 3, s=s, ts=ts, name="mla_prep")


def _mla_prep_bwd(q_pre, kv_pre, p_even, cos_m, sin_m, qhn, khn, dq, dk, *, s, ts):
    w = MLA_H * LANES

    def fn(qp, kp, kr, c, sn, dqv, dkv, gq, gk):
        dqs, dks = [], []
        dkr = jnp.zeros_like(kr)
        dgq = jnp.zeros((1, LANES), F32)
        dgk = jnp.zeros((1, LANES), F32)
        for h in range(MLA_H):
            sl = slice(h * LANES, (h + 1) * LANES)
            _, qn, qr = _gn(qp[:, sl], gq, LANES, MLA_QK)
            _, kn, krs = _gn(kp[:, sl] + kr, gk, LANES, MLA_QK)
            dqn = _rope(dqv[:, sl] * MLA_SCALE, c, -sn, MLA_ROPE, MLA_NOPE)
            dkn = _rope(dkv[:, sl], c, -sn, MLA_ROPE, MLA_NOPE)
            dqh, g1 = _gn_bwd(dqn, qn, qr, gq, LANES, MLA_QK)
            dkh, g2 = _gn_bwd(dkn, kn, krs, gk, LANES, MLA_QK)
            dqs.append(dqh)
            dks.append(dkh)
            dkr = dkr + dkh
            dgq = dgq + g1
            dgk = dgk + g2
        return jnp.concatenate(dqs, axis=1), jnp.concatenate(dks, axis=1), dkr, dgq, dgk

    rows = [_cols(q_pre, w, 0, ts), _cols(kv_pre, w, 0, ts), _cols(p_even, LANES, EV_KR_BLK, ts),
            _cols(cos_m, LANES, 0, ts), _cols(sin_m, LANES, 0, ts), _cols(dq, w, 0, ts), _cols(dk, w, 0, ts)]
    return _ew(fn, rows, [qhn, khn], [(w, BF16), (w, BF16), (LANES, BF16)], [(1, LANES), (1, LANES)],
               s=s, ts=ts, name="mla_prep_bwd")


def _flash_fwd(q, k, v, *, tq, tk, side=()):
    s = q.shape[0]
    nq, nk = s // tq, s // tk
    rq = tq
    ns = len(side)

    def body(*refs):
        q_ref, k_ref, v_ref = refs[:3]
        o_ref, lse_ref = refs[3 + ns:5 + ns]
        m_s, acc = refs[5 + 2 * ns:7 + 2 * ns]
        h, i, j = pl.program_id(0), pl.program_id(1), pl.program_id(2)
        if ns:
            local, sends, arrivals = _gather_copies(side, refs[3:3 + ns], refs[5 + ns:5 + 2 * ns], *refs[7 + 2 * ns:])

            @pl.when((h == 0) & (i == 0) & (j == 0))
            def _():
                for cp in local + sends:
                    cp.start()

        @pl.when(j == 0)
        def _():
            m_s[...] = jnp.full_like(m_s, -jnp.inf)
            acc[...] = jnp.zeros_like(acc)

        kv, vv = k_ref[...], v_ref[...]
        for r in range(0, tq, rq):
            rows = slice(r, r + rq)
            sc = _dot_nt(q_ref[rows, :], kv)
            m_prev = m_s[rows, :]
            m_new = jnp.maximum(m_prev, jnp.max(sc, axis=-1, keepdims=True))
            p = jnp.exp(sc - jnp.tile(m_new, (1, tk // LANES)))
            acc[rows, :] = jnp.exp(m_prev - m_new) * acc[rows, :] + _dot(_bf(p), vv)
            m_s[rows, :] = m_new

        @pl.when(j == nk - 1)
        def _():
            a = acc[...]
            l = a[:, V_ONES[0]:V_ONES[0] + 1]
            o_ref[...] = (a / l).astype(o_ref.dtype)
            lse_ref[...] = (m_s[...] + jnp.log(jnp.broadcast_to(l, (tq, LANES)))).T[0:1, :]

        if ns:
            @pl.when((h == MLA_H - 1) & (i == nq - 1) & (j == nk - 1))
            def _():
                for cp in arrivals:
                    cp.wait_recv()
                for cp in sends:
                    cp.wait_send()
                for cp in local:
                    cp.wait()

    qs = pl.BlockSpec((tq, LANES), lambda h, i, j: (i, h))
    ks = pl.BlockSpec((tk, LANES), lambda h, i, j: (j, h))
    outs = pl.pallas_call(
        body, name="mla_flash_fwd_gather" if ns else "mla_flash_fwd", grid=(MLA_H, nq, nk),
        in_specs=[qs, ks, ks] + [HBM_SPEC] * ns,
        out_specs=[qs, pl.BlockSpec((None, 1, tq), lambda h, i, j: (h, 0, i))] + [HBM_SPEC] * ns,
        out_shape=[jax.ShapeDtypeStruct((s, MLA_H * LANES), BF16), jax.ShapeDtypeStruct((MLA_H, 1, s), F32)]
        + _gather_shapes(side),
        scratch_shapes=[pltpu.VMEM((tq, LANES), F32), pltpu.VMEM((tq, LANES), F32)]
        + ([_sems(3 * ns), _sems(3 * ns), _sems(ns)] if ns else []),
        compiler_params=_cp(("arbitrary",) * 3 if ns else ("parallel", "parallel", "arbitrary")),
    )(q, k, v, *[a for a, _, _ in side])
    return outs[0], outs[1], list(outs[2:])


def _attn_bwd_prep(dar, o, *, s, ts):
    w = MLA_H * LANES

    def fn(dov, ov):
        outs = []
        lane = lax.broadcasted_iota(jnp.int32, (dov.shape[0], LANES), 1)
        for h in range(MLA_H):
            sl = slice(h * LANES, (h + 1) * LANES)
            d = dov[:, sl]
            delta = _lanesum(d * ov[:, sl].astype(F32))
            hi = _bf(delta).astype(F32)
            outs.append(jnp.where(lane == V_ONES[0], -hi, jnp.where(lane == V_ONES[1], hi - delta, d)))
        return jnp.concatenate(outs, axis=1)

    return _ew(fn, [_cols(dar, w, 0, ts), _cols(o, w, 0, ts)], [], [(w, BF16)], s=s, ts=ts,
               name="mla_attn_bwd_prep")[0]


def _flash_bwd(q, k, v, do, lse, *, tq, tk):
    s = q.shape[0]
    nq, nk = s // tq, s // tk

    def body(q_ref, k_ref, v_ref, do_ref, lse_ref, dq_ref, dk_ref, dv_ref, dk_acc, dv_acc):
        j = pl.program_id(1)
        i = pl.program_id(2)
        qv, kv, vv, dov = q_ref[...], k_ref[...], v_ref[...], do_ref[...]
        pt = jnp.exp(_dot_nt(kv, qv) - lse_ref[...])
        dst = _bf(pt * _dot_nt(vv, dov))
        dv_c = _dot(_bf(pt), dov)
        dk_c = _dot(dst, qv)
        dq_c = _dot_tn(dst, kv)
        rows = pl.ds(pl.multiple_of(i * tq, tq), tq)

        @pl.when(i == 0)
        def _():
            dk_acc[...] = dk_c
            dv_acc[...] = dv_c

        @pl.when(i > 0)
        def _():
            dk_acc[...] += dk_c
            dv_acc[...] += dv_c

        @pl.when(j == 0)
        def _():
            dq_ref[rows, :] = dq_c

        @pl.when(j > 0)
        def _():
            dq_ref[rows, :] += dq_c

        @pl.when(i == nq - 1)
        def _():
            dk_ref[...] = dk_acc[...]
            dv_ref[...] = dv_acc[...].astype(dv_ref.dtype)

    qs = pl.BlockSpec((tq, LANES), lambda h, j, i: (i, h))
    ks = pl.BlockSpec((tk, LANES), lambda h, j, i: (j, h))
    st = pl.BlockSpec((None, 1, tq), lambda h, j, i: (h, 0, i))
    return pl.pallas_call(
        body, name="mla_flash_bwd", grid=(MLA_H, nk, nq),
        in_specs=[qs, ks, ks, qs, st],
        out_specs=[pl.BlockSpec((s, LANES), lambda h, j, i: (0, h)), ks, ks],
        out_shape=[jax.ShapeDtypeStruct((s, MLA_H * LANES), F32), jax.ShapeDtypeStruct((s, MLA_H * LANES), F32),
                   jax.ShapeDtypeStruct((s, MLA_H * LANES), BF16)],
        scratch_shapes=[pltpu.VMEM((tk, LANES), F32), pltpu.VMEM((tk, LANES), F32)],
        compiler_params=_cp(("parallel", "arbitrary", "arbitrary")),
    )(q, k, v, do, lse)


def _ret_geometry(d, c):
    df = float(d)
    ii = lax.broadcasted_iota(jnp.int32, (c, c), 0).astype(F32)
    jj = lax.broadcasted_iota(jnp.int32, (c, c), 1).astype(F32)
    rel = (ii - jj) * (1.0 - 2.0 * df)
    mask = rel >= df
    rel0 = jnp.maximum(rel, 0.0)
    pos = lax.broadcasted_iota(jnp.int32, (c, 1), 0).astype(F32)
    ez = (c - 1 - pos) + df * (2.0 * pos - (c - 1))
    ex = (pos + 1.0) + df * (c - 1 - 2.0 * pos)
    return mask, rel0, ez, ex


def _chunk_index(n_chunks):
    return lambda d, n: n + d * (n_chunks - 1 - 2 * n)


def _ret_fwd(p_even, cos_r, sin_r, theta_l):
    s = p_even.shape[0]
    c = RET_C
    n_chunks = s // c
    w = RET_H * LANES
    cidx = _chunk_index(n_chunks)

    def body(*refs):
        n = pl.program_id(0)

        @pl.when(n == 0)
        def _():
            for r_s in refs[16:18]:
                r_s[...] = jnp.zeros_like(r_s)

        stores = []
        for d in range(2):
            stores += one(d, *refs[6 * d:6 * d + 6], *refs[12 + 2 * d:14 + 2 * d], refs[16 + d])
        for ref, val in stores:
            ref[...] = val

    def one(d, q_ref, k_ref, v_ref, cos_ref, sin_ref, th_ref, o_ref, rp_ref, r_s):
        lg = jnp.log1p(-jnp.exp(-th_ref[...] * LN2))
        mask, rel0, ez, ex = _ret_geometry(d, c)
        cs, sn = cos_ref[...], sin_ref[...]
        r_all = r_s[...]
        outs, states = [], []
        for h in range(RET_H):
            sl = slice(h * LANES, (h + 1) * LANES)
            lgh = lg[:, h * LANES:h * LANES + 1]
            dm = jnp.where(mask, jnp.exp(lgh * rel0), 0.0)
            qh = _bf(_rope(q_ref[:, sl], cs, sn, RET_DK, 0))
            kf = _rope(k_ref[:, sl], cs, sn, RET_DK, 0) * (RET_DK ** -0.5)
            kh = _bf(kf)
            vh = _bf(v_ref[:, sl])
            rh = r_all[sl, :]
            a = _dot_nt(qh, kh) * dm
            outs.append(_dot(_bf(a), vh) + jnp.exp(lgh * ex) * _dot(qh, _bf(rh)))
            zk = _bf(kf * jnp.exp(lgh * ez))
            states.append(jnp.exp(lgh * c) * rh + _dot_tn(zk, vh))
        return [(rp_ref, r_all), (o_ref, jnp.concatenate(outs, axis=1)), (r_s, jnp.concatenate(states, axis=0))]

    def ins(d):
        col = lambda blk: pl.BlockSpec((c, w), lambda n: (cidx(d, n), blk))
        tab = pl.BlockSpec((c, LANES), lambda n: (cidx(d, n), 0))
        return [col(0), col(1), col(2), tab, tab, pl.BlockSpec((None, 1, w), lambda n: (d, 0, 0))]

    def outs(d):
        return [pl.BlockSpec((c, w), lambda n: (cidx(d, n), 0)),
                pl.BlockSpec((None, w, LANES), lambda n: (cidx(d, n), 0, 0))]

    o_f, r_f, o_b, r_b = pl.pallas_call(
        body, name="ret_fwd", grid=(n_chunks,),
        in_specs=ins(0) + ins(1), out_specs=outs(0) + outs(1),
        out_shape=[jax.ShapeDtypeStruct((s, w), F32), jax.ShapeDtypeStruct((n_chunks, w, LANES), F32)] * 2,
        scratch_shapes=[pltpu.VMEM((w, LANES), F32)] * 2,
        compiler_params=_cp(("arbitrary",)),
    )(*[p_even, p_even, p_even, cos_r, sin_r, theta_l] * 2)
    return (o_f, o_b), (r_f, r_b)


def _ret_bwd(p_even, cos_r, sin_r, theta_l, theta_h, r_prev, do):
    s = p_even.shape[0]
    c = RET_C
    n_chunks = s // c
    w = RET_H * LANES
    fwd_idx = _chunk_index(n_chunks)

    def cidx(d, n):
        return fwd_idx(d, n_chunks - 1 - n)

    def body(*refs):
        n = pl.program_id(0)

        @pl.when(n == 0)
        def _():
            for d in range(2):
                refs[26 + d][...] = jnp.zeros_like(refs[26 + d])
                refs[21 + 4 * d][...] = jnp.zeros_like(refs[21 + 4 * d])

        stores = []
        for d in range(2):
            stores += one(d, *refs[9 * d:9 * d + 9], *refs[18 + 4 * d:22 + 4 * d], refs[26 + d])
        for ref, val, accumulate in stores:
            if accumulate:
                ref[...] += val
            else:
                ref[...] = val

    def one(d, q_ref, k_ref, v_ref, cos_ref, sin_ref, th_ref, thh_ref, rp_ref, do_ref,
            dq_ref, dk_ref, dv_ref, dth_ref, dr_s):
        lg = jnp.log1p(-jnp.exp(-th_ref[...] * LN2))
        mask, rel0, ez, ex = _ret_geometry(d, c)
        cs, sn = cos_ref[...], sin_ref[...]
        rp_all, dr_all = rp_ref[...], dr_s[...]
        row = lax.broadcasted_iota(jnp.int32, (RET_H, LANES), 0)
        dlg = jnp.zeros((RET_H, LANES), F32)
        kscale = RET_DK ** -0.5
        dqs, dks, dvs, drs = [], [], [], []
        for h in range(RET_H):
            sl = slice(h * LANES, (h + 1) * LANES)
            lgh = lg[:, h * LANES:h * LANES + 1]
            dm = jnp.where(mask, jnp.exp(lgh * rel0), 0.0)
            zeta = jnp.exp(lgh * ez)
            xi = jnp.exp(lgh * ex)
            gc = jnp.exp(lgh * c)
            qf = _rope(q_ref[:, sl], cs, sn, RET_DK, 0)
            qh = _bf(qf)
            kf = _rope(k_ref[:, sl], cs, sn, RET_DK, 0) * kscale
            kh = _bf(kf)
            zkf = kf * zeta
            zk = _bf(zkf)
            vh = _bf(v_ref[:, sl])
            dof = do_ref[:, sl]
            doh = _bf(dof)
            rp = rp_all[sl, :]
            rpb = _bf(rp)
            drn = dr_all[sl, :]
            drb = _bf(drn)
            a = _dot_nt(qh, kh) * dm
            da0 = _dot_nt(doh, vh)
            da = _bf(da0 * dm)
            vdr = _dot_nt(vh, drb)
            dq_r = _dot(da, kh) + xi * _dot_nt(doh, rpb)
            dk_r = _dot_tn(da, qh) + zeta * vdr
            dvs.append(_dot_tn(_bf(a), doh) + _dot(zk, drb))
            dqs.append(_rope(dq_r, cs, -sn, RET_DK, 0))
            dks.append(_rope(dk_r * kscale, cs, -sn, RET_DK, 0))
            drs.append(_dot_tn(_bf(qf * xi), doh) + gc * drn)
            ocross = xi * _dot(qh, rpb)
            t = (jnp.sum(rel0 * a * da0, keepdims=True)
                 + jnp.sum(ex * dof * ocross, keepdims=True)
                 + c * gc * jnp.sum(drn * rp, keepdims=True)
                 + jnp.sum(ez * zkf * vdr, keepdims=True))
            dlg = jnp.where(row == h, t, dlg)
        x2 = jnp.exp(-thh_ref[...] * LN2)
        return [(dq_ref, jnp.concatenate(dqs, axis=1), False), (dk_ref, jnp.concatenate(dks, axis=1), False),
                (dv_ref, jnp.concatenate(dvs, axis=1), False), (dr_s, jnp.concatenate(drs, axis=0), False),
                (dth_ref, dlg * (x2 * LN2 / (1.0 - x2)), True)]

    def ins(d):
        col = lambda blk: pl.BlockSpec((c, w), lambda n: (cidx(d, n), blk))
        tab = pl.BlockSpec((c, LANES), lambda n: (cidx(d, n), 0))
        return [col(0), col(1), col(2), tab, tab, pl.BlockSpec((None, 1, w), lambda n: (d, 0, 0)),
                pl.BlockSpec((None, RET_H, LANES), lambda n: (d, 0, 0)),
                pl.BlockSpec((None, w, LANES), lambda n: (cidx(d, n), 0, 0)), col(0)]

    def outs(d):
        row = pl.BlockSpec((c, w), lambda n: (cidx(d, n), 0))
        return [row, row, row, pl.BlockSpec((RET_H, LANES), lambda n: (0, 0))]

    res = pl.pallas_call(
        body, name="ret_bwd", grid=(n_chunks,),
        in_specs=ins(0) + ins(1), out_specs=outs(0) + outs(1),
        out_shape=([jax.ShapeDtypeStruct((s, w), F32)] * 3 + [jax.ShapeDtypeStruct((RET_H, LANES), F32)]) * 2,
        scratch_shapes=[pltpu.VMEM((w, LANES), F32)] * 2,
        compiler_params=_cp(("arbitrary",)),
    )(*[a for d in range(2) for a in (p_even, p_even, p_even, cos_r, sin_r, theta_l, theta_h, r_prev[d], do)])
    return (res[0], res[4]), (res[1], res[5]), (res[2], res[6]), jnp.stack([res[3], res[7]])


def _post_fwd(o2, gate_row, gain, *, group, n, s, ts, name):
    w = o2[0].shape[1]

    def fn(of, ob, g, gv):
        y = _gn(of + ob, gv, group, n)[0]
        return g * _sigmoid(g) * y

    return _ew(fn, [_lead(o2, 0, ts), _lead(o2, 1, ts), gate_row], [gain], [(w, BF16)], s=s, ts=ts, name=name)[0]


def _post_bwd(o2, gate_row, gain, dr_row, *, group, n, s, ts, name):
    w = o2[0].shape[1]

    def fn(of, ob, g, dr, gv):
        y, xn, rstd = _gn(of + ob, gv, group, n)
        sg = _sigmoid(g)
        dy = dr * (g * sg)
        dgate = dr * y * (sg * (1.0 + g * (1.0 - sg)))
        do, dgain = _gn_bwd(dy, xn, rstd, gv, group, n)
        return do, dgate, dgain

    return _ew(fn, [_lead(o2, 0, ts), _lead(o2, 1, ts), gate_row, dr_row], [gain],
               [(w, F32), (w, BF16)], [(1, w)], s=s, ts=ts, name=name)


def _sum2(a2, *, s, ts, name):
    w = a2[0].shape[1]
    return _ew(lambda a, b: a + b, [_lead(a2, 0, ts), _lead(a2, 1, ts)], [], [(w, BF16)], s=s, ts=ts, name=name)[0]


def _gla_common(d, q_ref, k_ref, ga_ref, wg_ref, bg_ref):
    c = GLA_C
    df = float(d)
    ii = lax.broadcasted_iota(jnp.int32, (c, c), 0).astype(F32)
    jj = lax.broadcasted_iota(jnp.int32, (c, c), 1).astype(F32)
    rel = (ii - jj) * (1.0 - 2.0 * df)
    tri = _bf(jnp.where(rel >= 0.0, 1.0, 0.0))
    mask = rel >= df
    gab = _bf(ga_ref[...])
    z = _dot(gab, wg_ref[...]) + bg_ref[...]
    la = (jnp.minimum(z, 0.0) - jnp.log1p(jnp.exp(-jnp.abs(z)))) * (1.0 / GLA_TAU)
    l1, l2, l3 = _split3(la)
    b = _dot(tri, l1) + _dot(tri, l2) + _dot(tri, l3)
    first = d == 0
    bm = b[c // 2:c // 2 + 1] if first else b[c // 2 - 1:c // 2]
    bl = b[c - 1:c] if first else b[0:1]
    q = q_ref[...] * (GLA_DK ** -0.5)
    k = k_ref[...]
    e1, e2, e3, eb = jnp.exp(b - bm), jnp.exp(bm - b), jnp.exp(bl - b), jnp.exp(b)
    return dict(tri=tri, mask=mask, gab=gab, z=z, ebl=jnp.exp(bl), e1=e1, e2=e2, e3=e3, eb=eb,
                qc=q * e1, kc=k * e2, kd=k * e3, qe=q * eb, first=first)


def _col_scale(row_vec, width):
    t = jnp.broadcast_to(row_vec, (LANES, LANES)).T
    return jnp.concatenate([t] * (width // LANES), axis=1)


def _gla_fwd(p_odd, wg2, bg2):
    s = p_odd.shape[0]
    c = GLA_C
    n_chunks = s // c
    wk, wv = GLA_H * GLA_DK, GLA_H * GLA_DV
    cidx = _chunk_index(n_chunks)

    def body(*refs):
        n = pl.program_id(0)

        @pl.when(n == 0)
        def _():
            for s_s in refs[16:18]:
                s_s[...] = jnp.zeros_like(s_s)

        stores = []
        for d in range(2):
            stores += one(d, *refs[6 * d:6 * d + 6], *refs[12 + 2 * d:14 + 2 * d], refs[16 + d])
        for ref, val in stores:
            ref[...] = val

    def one(d, q_ref, k_ref, v_ref, ga_ref, wg_ref, bg_ref, o_ref, sp_ref, s_s):
        g = _gla_common(d, q_ref, k_ref, ga_ref, wg_ref, bg_ref)
        s_all = s_s[...]
        outs, states = [], []
        for h in range(GLA_H):
            sl = slice(h * GLA_DK, (h + 1) * GLA_DK)
            vs = slice(h * GLA_DV, (h + 1) * GLA_DV)
            vh = _bf(v_ref[:, vs])
            sh = s_all[sl, :]
            a = jnp.where(g["mask"], _dot_nt(_bf(g["qc"][:, sl]), _bf(g["kc"][:, sl])), 0.0)
            outs.append(_dot(_bf(a), vh) + _dot(_bf(g["qe"][:, sl]), _bf(sh)))
            states.append(_col_scale(g["ebl"][:, sl], GLA_DV) * sh + _dot_tn(_bf(g["kd"][:, sl]), vh))
        return [(sp_ref, s_all), (o_ref, jnp.concatenate(outs, axis=1)), (s_s, jnp.concatenate(states, axis=0))]

    def ins(d):
        col = lambda width, blk: pl.BlockSpec((c, width), lambda n: (cidx(d, n), blk))
        return [col(wk, 0), col(wk, 1), col(wv, 1), col(LANES, OD_GA_BLK),
                pl.BlockSpec((None, LANES, wk), lambda n: (d, 0, 0)), pl.BlockSpec((None, 1, wk), lambda n: (d, 0, 0))]

    def outs(d):
        return [pl.BlockSpec((c, wv), lambda n: (cidx(d, n), 0)),
                pl.BlockSpec((None, wk, GLA_DV), lambda n: (cidx(d, n), 0, 0))]

    o_f, s_f, o_b, s_b = pl.pallas_call(
        body, name="gla_fwd", grid=(n_chunks,),
        in_specs=ins(0) + ins(1), out_specs=outs(0) + outs(1),
        out_shape=[jax.ShapeDtypeStruct((s, wv), F32), jax.ShapeDtypeStruct((n_chunks, wk, GLA_DV), F32)] * 2,
        scratch_shapes=[pltpu.VMEM((wk, GLA_DV), F32)] * 2,
        compiler_params=_cp(("arbitrary",)),
    )(*[p_odd, p_odd, p_odd, p_odd, wg2, bg2] * 2)
    return (o_f, o_b), (s_f, s_b)


def _gla_bwd(p_odd, wg2, bg2, s_prev, do):
    s = p_odd.shape[0]
    c = GLA_C
    n_chunks = s // c
    wk, wv = GLA_H * GLA_DK, GLA_H * GLA_DV
    fwd_idx = _chunk_index(n_chunks)

    def cidx(d, n):
        return fwd_idx(d, n_chunks - 1 - n)

    def body(*refs):
        n = pl.program_id(0)

        @pl.when(n == 0)
        def _():
            for d in range(2):
                for r in (refs[28 + d], refs[20 + 6 * d], refs[21 + 6 * d]):
                    r[...] = jnp.zeros_like(r)

        stores = []
        for d in range(2):
            stores += one(d, *refs[8 * d:8 * d + 8], *refs[16 + 6 * d:22 + 6 * d], refs[28 + d])
        for ref, val, accumulate in stores:
            if accumulate:
                ref[...] += val
            else:
                ref[...] = val

    def one(d, q_ref, k_ref, v_ref, ga_ref, wg_ref, bg_ref, sp_ref, do_ref,
            dq_ref, dk_ref, dv_ref, dga_ref, dwg_ref, dbg_ref, ds_s):
        g = _gla_common(d, q_ref, k_ref, ga_ref, wg_ref, bg_ref)
        mask = g["mask"]
        ones8 = jnp.ones((8, GLA_DV), BF16)
        sp_all, ds_all = sp_ref[...], ds_s[...]
        dbs, dbms, dbls = [], [], []
        dqs, dks, dvs, dss = [], [], [], []
        for h in range(GLA_H):
            sl = slice(h * GLA_DK, (h + 1) * GLA_DK)
            vs = slice(h * GLA_DV, (h + 1) * GLA_DV)
            qc, kc, kd, qe = g["qc"][:, sl], g["kc"][:, sl], g["kd"][:, sl], g["qe"][:, sl]
            qcb, kcb, kdb, qeb = _bf(qc), _bf(kc), _bf(kd), _bf(qe)
            vh = _bf(v_ref[:, vs])
            doh = _bf(do_ref[:, vs])
            sp = sp_all[sl, :]
            dsn = ds_all[sl, :]
            dsb = _bf(dsn)
            a = _bf(jnp.where(mask, _dot_nt(qcb, kcb), 0.0))
            da = _bf(jnp.where(mask, _dot_nt(doh, vh), 0.0))
            dvs.append(_dot_tn(a, doh) + _dot(kdb, dsb))
            dqc = _dot(da, kcb)
            dkc = _dot_tn(da, qcb)
            dqe = _dot_nt(doh, _bf(sp))
            dkd = _dot_nt(vh, dsb)
            dss.append(_dot_tn(qeb, doh) + _col_scale(g["ebl"][:, sl], GLA_DV) * dsn)
            dqs.append((dqc * g["e1"][:, sl] + dqe * g["eb"][:, sl]) * (GLA_DK ** -0.5))
            dks.append(dkc * g["e2"][:, sl] + dkd * g["e3"][:, sl])
            t1, t2, t3, t4 = dqc * qc, dkc * kc, dqe * qe, dkd * kd
            dbs.append(t1 - t2 + t3 - t4)
            dbms.append(_rowsum(t2 - t1))
            m1, m2, _ = _split3(dsn * sp)
            rs = (_dot_nt(ones8, m1) + _dot_nt(ones8, m2))[0:1]
            dbls.append(_rowsum(t4) + g["ebl"][:, sl] * rs)
        db = jnp.concatenate(dbs, axis=1)
        dbm = jnp.concatenate(dbms, axis=1)
        dbl = jnp.concatenate(dbls, axis=1)
        row = lax.broadcasted_iota(jnp.int32, (c, wk), 0)
        mid = jnp.where(g["first"], c // 2, c // 2 - 1)
        last = jnp.where(g["first"], c - 1, 0)
        db = db + jnp.where(row == mid, dbm, 0.0) + jnp.where(row == last, dbl, 0.0)
        d1, d2, d3 = _split3(db)
        tri = g["tri"]
        dla = _dot_tn(tri, d1) + _dot_tn(tri, d2) + _dot_tn(tri, d3)
        dz = dla * (1.0 / GLA_TAU) * (1.0 - _sigmoid(g["z"]))
        dzb = _bf(dz)
        return [(dq_ref, jnp.concatenate(dqs, axis=1), False), (dk_ref, jnp.concatenate(dks, axis=1), False),
                (dv_ref, jnp.concatenate(dvs, axis=1), False), (ds_s, jnp.concatenate(dss, axis=0), False),
                (dga_ref, _dot_nt(dzb, wg_ref[...]), False), (dwg_ref, _dot_tn(g["gab"], dzb), True),
                (dbg_ref, _rowsum(dz), True)]

    def ins(d):
        col = lambda width, blk: pl.BlockSpec((c, width), lambda n: (cidx(d, n), blk))
        return [col(wk, 0), col(wk, 1), col(wv, 1), col(LANES, OD_GA_BLK),
                pl.BlockSpec((None, LANES, wk), lambda n: (d, 0, 0)), pl.BlockSpec((None, 1, wk), lambda n: (d, 0, 0)),
                pl.BlockSpec((None, wk, GLA_DV), lambda n: (cidx(d, n), 0, 0)), col(wv, 0)]

    def outs(d):
        row = lambda width: pl.BlockSpec((c, width), lambda n: (cidx(d, n), 0))
        return [row(wk), row(wk), row(wv), row(LANES),
                pl.BlockSpec((LANES, wk), lambda n: (0, 0)), pl.BlockSpec((1, wk), lambda n: (0, 0))]

    shapes = [jax.ShapeDtypeStruct((s, wk), F32), jax.ShapeDtypeStruct((s, wk), F32), jax.ShapeDtypeStruct((s, wv), F32),
              jax.ShapeDtypeStruct((s, LANES), F32), jax.ShapeDtypeStruct((LANES, wk), F32),
              jax.ShapeDtypeStruct((1, wk), F32)]
    res = pl.pallas_call(
        body, name="gla_bwd", grid=(n_chunks,),
        in_specs=ins(0) + ins(1), out_specs=outs(0) + outs(1), out_shape=shapes * 2,
        scratch_shapes=[pltpu.VMEM((wk, GLA_DV), F32)] * 2,
        compiler_params=_cp(("arbitrary",)),
    )(*[a for d in range(2) for a in (p_odd, p_odd, p_odd, p_odd, wg2, bg2, s_prev[d], do)])
    pair = lambda k: (res[k], res[6 + k])
    return pair(0), pair(1), pair(2), pair(3), jnp.stack(pair(4)), jnp.stack(pair(5))


HALO = 8


def _halo_specs(width_blk, col0, ts, s):
    r = ts // HALO
    last = s // HALO - 1
    cur = pl.BlockSpec((ts, width_blk), lambda j, i: (i, col0 + j))
    prev = pl.BlockSpec((HALO, width_blk), lambda j, i: (jnp.maximum(i * r - 1, 0), col0 + j))
    nxt = pl.BlockSpec((HALO, width_blk), lambda j, i: (jnp.minimum((i + 1) * r, last), col0 + j))
    return [prev, cur, nxt]


def _with_halo(prev_ref, cur_ref, next_ref, i, n_i):
    p = jnp.where(i == 0, 0.0, prev_ref[...])
    q = jnp.where(i == n_i - 1, 0.0, next_ref[...])
    return jnp.concatenate([p, cur_ref[...], q], axis=0)


def _shift_down(x):
    return pltpu.roll(x, 1, 0)


def _shift_up(x):
    return pltpu.roll(x, x.shape[0] - 1, 0)


def _ffn_act(up, conv_w, conv_b, *, ts):
    s = up.shape[0]
    tc = _tile(D_FF, 1408)
    nj = D_FF // tc
    n_i = s // ts

    def body(gp, gc, gn, val_ref, w_ref, b_ref, a_ref):
        i = pl.program_id(1)
        g = _with_halo(gp, gc, gn, i, n_i)
        w = w_ref[...]
        conv = w[0:1] * _shift_down(g) + w[1:2] * g + w[2:3] * _shift_up(g) + b_ref[...]
        conv = conv[HALO:HALO + ts]
        a_ref[...] = (conv * _sigmoid(conv) * val_ref[...]).astype(a_ref.dtype)

    return pl.pallas_call(
        body, name="ffn_act", grid=(nj, n_i),
        in_specs=_halo_specs(tc, 0, ts, s) + [pl.BlockSpec((ts, tc), lambda j, i: (i, nj + j)),
                                              pl.BlockSpec((3, tc), lambda j, i: (0, j)),
                                              pl.BlockSpec((1, tc), lambda j, i: (0, j))],
        out_specs=pl.BlockSpec((ts, tc), lambda j, i: (i, j)),
        out_shape=jax.ShapeDtypeStruct((s, D_FF), BF16),
        compiler_params=_cp(("parallel", "arbitrary")),
    )(up, up, up, up, conv_w, conv_b)


def _ffn_act_bwd(up, da, conv_w, conv_b, *, ts):
    s = up.shape[0]
    tc = _tile(D_FF, 1408)
    nj = D_FF // tc
    n_i = s // ts

    def body(gp, gc, gn, vp, vc, vn, dp, dc, dn, w_ref, b_ref, dup_ref, dw_ref, db_ref):
        i = pl.program_id(1)
        g = _with_halo(gp, gc, gn, i, n_i)
        v = _with_halo(vp, vc, vn, i, n_i)
        dav = _with_halo(dp, dc, dn, i, n_i)
        w = w_ref[...]
        gm, gpl = _shift_down(g), _shift_up(g)
        conv = w[0:1] * gm + w[1:2] * g + w[2:3] * gpl + b_ref[...]
        sg = _sigmoid(conv)
        dgc = dav * v * (sg * (1.0 + conv * (1.0 - sg)))
        dgate = w[0:1] * _shift_up(dgc) + w[1:2] * dgc + w[2:3] * _shift_down(dgc)
        ctr = slice(HALO, HALO + ts)
        dup_ref[0] = dgate[ctr].astype(dup_ref.dtype)
        dup_ref[1] = (dav[ctr] * (conv * sg)[ctr]).astype(dup_ref.dtype)
        dgc_c = dgc[ctr]
        dw = jnp.concatenate([_rowsum(dgc_c * gm[ctr]), _rowsum(dgc_c * g[ctr]), _rowsum(dgc_c * gpl[ctr])], axis=0)
        dbv = _rowsum(dgc_c)

        @pl.when(i == 0)
        def _():
            dw_ref[...] = dw
            db_ref[...] = dbv

        @pl.when(i > 0)
        def _():
            dw_ref[...] += dw
            db_ref[...] += dbv

    return pl.pallas_call(
        body, name="ffn_act_bwd", grid=(nj, n_i),
        in_specs=(_halo_specs(tc, 0, ts, s) + _halo_specs(tc, nj, ts, s) + _halo_specs(tc, 0, ts, s)
                  + [pl.BlockSpec((3, tc), lambda j, i: (0, j)), pl.BlockSpec((1, tc), lambda j, i: (0, j))]),
        out_specs=[pl.BlockSpec((2, ts, tc), lambda j, i: (0, i, j)),
                   pl.BlockSpec((3, tc), lambda j, i: (0, j)), pl.BlockSpec((1, tc), lambda j, i: (0, j))],
        out_shape=[jax.ShapeDtypeStruct((2, s, D_FF), BF16),
                   jax.ShapeDtypeStruct((3, D_FF), F32), jax.ShapeDtypeStruct((1, D_FF), F32)],
        compiler_params=_cp(("parallel", "arbitrary")),
    )(up, up, up, up, up, up, da, da, da, conv_w, conv_b)


def _loss_head(y, target, *, s, ts):
    def fn(yv, tv):
        err = yv - tv
        return err * (1.0 / D_MODEL), _rowsum(err * err)

    return _ew(fn, [_cols(y, D_MODEL, 0, ts), _cols(target, D_MODEL, 0, ts)], [], [(D_MODEL, F32)],
               [(1, D_MODEL)], s=s, ts=ts, name="loss_head")


def _rows_tile(r, width):
    ts = r
    while ts * width * 4 > (2 << 20) and ts % 16 == 0:
        ts //= 2
    return ts


def _adamw(w, g, m, v, *, ts, name):
    r, width = w.shape
    assert r % ts == 0

    def fn(wv, gv, mv, vv):
        mn = ADAM_B1 * mv + (1.0 - ADAM_B1) * gv
        vn = ADAM_B2 * vv + (1.0 - ADAM_B2) * (gv * gv)
        m_hat = mn / (1.0 - ADAM_B1 ** ADAM_STEP)
        v_hat = vn / (1.0 - ADAM_B2 ** ADAM_STEP)
        delta = -ADAM_LR * (m_hat / (jnp.sqrt(v_hat) + ADAM_EPS) + ADAM_WD * wv)
        return delta, mn, vn

    rows = [_cols(a, width, 0, ts) for a in (w, g, m, v)]
    return _ew(fn, rows, [], [(width, F32)] * 3, s=r, ts=ts, name=name)


def _pad_heads(w, heads, real):
    lead = w.shape[:-1]
    w = w.reshape(lead + (heads, real))
    w = jnp.pad(w, [(0, 0)] * len(lead) + [(0, 0), (0, LANES - real)])
    return w.reshape(lead + (heads * LANES,))


def _pad_head_rows(w, heads, real):
    return _pad_heads(w.T, heads, real).T


def _pack_even(p):
    w_in = p["w_in"]
    z = lambda n: jnp.zeros((D_MODEL, n), w_in.dtype)
    o = 0
    parts = {}
    for nm, n in (("cq", MLA_QR), ("ckv", MLA_KVR), ("kr", MLA_ROPE), ("rq", 512), ("rk", 512), ("rv", 512), ("rg", 512)):
        parts[nm] = w_in[:, o:o + n]
        o += n
    w_in_p = jnp.concatenate(
        [_pad_heads(parts[k], RET_H, RET_DK) for k in ("rq", "rk", "rv", "rg")]
        + [parts["cq"], z(EV_CQ - MLA_QR), parts["ckv"], z(MLA_NOPE), parts["kr"], z(LANES - MLA_QK), z(LANES)], axis=1)
    w_uq = jnp.pad(_pad_heads(p["w_uq"], MLA_H, MLA_QK), ((0, EV_CQ - MLA_QR), (0, 0)))
    ukv = p["w_ukv"].reshape(MLA_KVR, MLA_H, MLA_NOPE + MLA_V)
    w_ukv = jnp.concatenate([_pad_heads(ukv[..., :MLA_NOPE].reshape(MLA_KVR, -1), MLA_H, MLA_NOPE),
                             _pad_heads(ukv[..., MLA_NOPE:].reshape(MLA_KVR, -1), MLA_H, MLA_V)], axis=1)
    w_out = jnp.concatenate([_pad_head_rows(p["w_out"][:MLA_H * MLA_V], MLA_H, MLA_V),
                             _pad_head_rows(p["w_out"][MLA_H * MLA_V:], RET_H, RET_DV)], axis=0)
    return dict(
        w_in=w_in_p, w_uq=w_uq, w_ukv=w_ukv, w_out=w_out,
        mix_g=p["mix_norm"][None, :],
        q_norm=jnp.pad(p["q_norm"], (0, EV_CQ - MLA_QR))[None, :],
        kv_norm=p["kv_norm"][None, :],
        qhn=jnp.pad(p["q_head_norm"], (0, LANES - MLA_QK))[None, :],
        khn=jnp.pad(p["k_head_norm"], (0, LANES - MLA_QK))[None, :],
        ret_gain=_pad_heads(p["ret_out_norm"].reshape(-1), RET_H, RET_DV)[None, :],
    )


def _pack_odd(p):
    w_in = p["w_in"]
    ga = w_in[:, 3072:]
    w_in_p = jnp.concatenate([w_in[:, :3072], ga, jnp.zeros((D_MODEL, LANES - 2 * GLA_R), w_in.dtype)], axis=1)
    wk = GLA_H * GLA_DK
    zf = jnp.zeros((LANES - GLA_R, wk), p["w_gate_fwd"].dtype)
    zb0 = jnp.zeros((GLA_R, wk), p["w_gate_fwd"].dtype)
    zb1 = jnp.zeros((LANES - 2 * GLA_R, wk), p["w_gate_fwd"].dtype)
    wg2 = jnp.stack([jnp.concatenate([p["w_gate_fwd"], zf], axis=0),
                     jnp.concatenate([zb0, p["w_gate_bwd"], zb1], axis=0)])
    bg2 = jnp.stack([p["b_gate_fwd"][None, :], p["b_gate_bwd"][None, :]])
    return dict(w_in=w_in_p, wg2=wg2, bg2=bg2, w_out=p["w_out"], mix_g=p["mix_norm"][None, :],
                gla_gain=p["gla_out_norm"].reshape(1, -1))


_MATRICES = ("w_in", "w_uq", "w_ukv", "w_out", "wg2")


def _packed(pack_fn, p):
    packed = pack_fn(p)
    packed = {k: (_bf(v) if k in _MATRICES else v.astype(F32)) for k, v in packed.items()}
    shapes = {k: jax.ShapeDtypeStruct(v.shape, F32) for k, v in p.items()}
    unpack = jax.linear_transpose(pack_fn, shapes)
    return packed, lambda g: unpack(g)[0]


def _ffn_fwd(x, w, *, s, ts):
    h = _rmsnorm(_cols(x, D_MODEL, 0, ts), w["norm_g"], n=D_MODEL, s=s, ts=ts, name="ffn_norm")
    up = _mm(h, w["w_up4"], b_layer=w["layer"], name="ffn_up")
    a = _ffn_act(up, w["conv_w"], w["conv_b"], ts=ts)
    y = _mm(a, w["w_down"], res=x, name="ffn_down")
    return y, dict(x=x, h=h, up=up, a=a)


def _ffn_bwd(dy, w, sv, *, s, ts):
    da = _mm(dy, w["w_down"], tb=True, name="ffn_down_dx")
    g_down = _mm(sv["a"], dy, ta=True, name="ffn_down_dw")
    dup, g_cw, g_cb = _ffn_act_bwd(sv["up"], da, w["conv_w"], w["conv_b"], ts=min(ts, FFN_ACT_ROWS))
    dh = _mm(dup, w["w_up4"], tb=True, b_layer=w["layer"], halves="a", name="ffn_up_dx")
    g_up = _mm(sv["h"], dup, ta=True, out_chips=True, halves="b", name="ffn_up_dw")
    dx, g_norm = _rmsnorm_bwd(_cols(sv["x"], D_MODEL, 0, ts), w["norm_g"], dh, dy, n=D_MODEL, s=s, ts=ts,
                              name="ffn_norm_bwd")
    return dx, dict(w_up=g_up, w_down=g_down, conv_w=g_cw, conv_b=g_cb, norm_g=g_norm)


def _even_fwd(x, w, tabs, *, s, ts, side=()):
    cos_m, sin_m, cos_r, sin_r = tabs
    h = _rmsnorm(_cols(x, D_MODEL, 0, ts), w["mix_g"], n=D_MODEL, s=s, ts=ts, name="mix_norm")
    p = _mm(h, w["w_in"], name="even_in")
    cqn = _rmsnorm(_cols(p, EV_CQ, EV_RET // EV_CQ, ts), w["q_norm"], n=MLA_QR, s=s, ts=ts, name="mla_q_norm")
    ckvn = _rmsnorm(_cols(p, MLA_KVR, (EV_RET + EV_CQ) // MLA_KVR, ts), w["kv_norm"], n=MLA_KVR, s=s, ts=ts,
                    name="mla_kv_norm")
    q_pre = _mm(cqn, w["w_uq"], name="mla_uq")
    kv_pre = _mm(ckvn, w["w_ukv"], name="mla_ukv")
    q, k, v = _mla_prep(q_pre, kv_pre, p, cos_m, sin_m, w["qhn"], w["khn"], s=s, ts=ts)
    o, lse, gathered = _flash_fwd(q, k, v, tq=min(s, FLASH_FWD_ROWS), tk=min(s, FLASH_KEYS), side=side)
    o2, r_prev = _ret_fwd(p, cos_r, sin_r, w["theta_l"])
    r = _post_fwd(o2, _cols(p, RET_H * LANES, 3, ts), w["ret_gain"], group=LANES, n=RET_DV, s=s, ts=ts,
                  name="ret_post")
    ar = jnp.concatenate([o, r], axis=1)
    y = _mm(ar, w["w_out"], res=x, name="even_out")
    return y, dict(x=x, h=h, p=p, cqn=cqn, ckvn=ckvn, q_pre=q_pre, kv_pre=kv_pre, q=q, k=k, v=v, o=o, lse=lse,
                   o2=o2, r_prev=r_prev, ar=ar), gathered


def _even_bwd(dy, w, sv, tabs, *, s, ts):
    cos_m, sin_m, cos_r, sin_r = tabs
    p = sv["p"]
    wh = MLA_H * LANES
    dar = _mm(dy, w["w_out"], tb=True, name="even_out_dx")
    g_out = _mm(sv["ar"], dy, ta=True, name="even_out_dw")
    do_attn = _attn_bwd_prep(dar, sv["o"], s=s, ts=ts)
    dq, dk, dv = _flash_bwd(sv["q"], sv["k"], sv["v"], do_attn, sv["lse"], tq=min(s, FLASH_BWD_ROWS),
                            tk=min(s, FLASH_KEYS))
    dq_pre, dk_pre, dkr, g_qhn, g_khn = _mla_prep_bwd(sv["q_pre"], sv["kv_pre"], p, cos_m, sin_m, w["qhn"], w["khn"],
                                                      dq, dk, s=s, ts=ts)
    dkv_pre = jnp.concatenate([dk_pre, dv], axis=1)
    dckvn = _mm(dkv_pre, w["w_ukv"], tb=True, name="mla_ukv_dx")
    g_ukv = _mm(sv["ckvn"], dkv_pre, ta=True, name="mla_ukv_dw")
    dcqn = _mm(dq_pre, w["w_uq"], tb=True, name="mla_uq_dx")
    g_uq = _mm(sv["cqn"], dq_pre, ta=True, name="mla_uq_dw")
    dckv, g_kvn = _rmsnorm_bwd(_cols(p, MLA_KVR, (EV_RET + EV_CQ) // MLA_KVR, ts), w["kv_norm"], dckvn, None,
                               n=MLA_KVR, s=s, ts=ts, name="mla_kv_norm_bwd")
    dcq, g_qn = _rmsnorm_bwd(_cols(p, EV_CQ, EV_RET // EV_CQ, ts), w["q_norm"], dcqn, None, n=MLA_QR, s=s, ts=ts,
                             name="mla_q_norm_bwd")
    do, drg, g_gain = _post_bwd(sv["o2"], _cols(p, wh, 3, ts), w["ret_gain"], _cols(dar, wh, 1, ts),
                                group=LANES, n=RET_DV, s=s, ts=ts, name="ret_post_bwd")
    dq2, dk2, dv2, dth = _ret_bwd(p, cos_r, sin_r, w["theta_l"], w["theta_h"], sv["r_prev"], do)
    drq, drk, drv = (_sum2(a, s=s, ts=ts, name="sum_dirs_1024") for a in (dq2, dk2, dv2))
    dp = jnp.concatenate([drq, drk, drv, drg, _bf(dcq), _bf(dckv), dkr, jnp.zeros((s, LANES), BF16)], axis=1)
    dh = _mm(dp, w["w_in"], tb=True, name="even_in_dx")
    g_in = _mm(sv["h"], dp, ta=True, name="even_in_dw")
    dx, g_mix = _rmsnorm_bwd(_cols(sv["x"], D_MODEL, 0, ts), w["mix_g"], dh, dy, n=D_MODEL, s=s, ts=ts,
                             name="mix_norm_bwd")
    grads = dict(w_in=g_in, w_uq=g_uq, w_ukv=g_ukv, w_out=g_out, mix_g=g_mix, q_norm=g_qn, kv_norm=g_kvn,
                 qhn=g_qhn, khn=g_khn, ret_gain=g_gain)
    return dx, grads, dth[:, :, 0]


def _odd_fwd(x, w, *, s, ts):
    h = _rmsnorm(_cols(x, D_MODEL, 0, ts), w["mix_g"], n=D_MODEL, s=s, ts=ts, name="mix_norm")
    p = _mm(h, w["w_in"], name="odd_in")
    o2, s_prev = _gla_fwd(p, w["wg2"], w["bg2"])
    g = _post_fwd(o2, _cols(p, GLA_H * GLA_DV, 2, ts), w["gla_gain"], group=GLA_DV, n=GLA_DV, s=s, ts=ts,
                  name="gla_post")
    y = _mm(g, w["w_out"], res=x, name="odd_out")
    return y, dict(x=x, h=h, p=p, o2=o2, s_prev=s_prev, g=g)


def _odd_bwd(dy, w, sv, *, s, ts):
    p = sv["p"]
    wv = GLA_H * GLA_DV
    dg = _mm(dy, w["w_out"], tb=True, name="odd_out_dx")
    g_out = _mm(sv["g"], dy, ta=True, name="odd_out_dw")
    do, dgr, g_gain = _post_bwd(sv["o2"], _cols(p, wv, 2, ts), w["gla_gain"], _cols(dg, wv, 0, ts),
                                group=GLA_DV, n=GLA_DV, s=s, ts=ts, name="gla_post_bwd")
    dq2, dk2, dv2, dga2, g_wg, g_bg = _gla_bwd(p, w["wg2"], w["bg2"], sv["s_prev"], do)
    dq = _sum2(dq2, s=s, ts=ts, name="sum_dirs_512")
    dk = _sum2(dk2, s=s, ts=ts, name="sum_dirs_512")
    dv = _sum2(dv2, s=s, ts=ts, name="sum_dirs_1024")
    dga = _sum2(dga2, s=s, ts=ts, name="sum_dirs_128")
    dp = jnp.concatenate([dq, dk, dv, dgr, dga], axis=1)
    dh = _mm(dp, w["w_in"], tb=True, name="odd_in_dx")
    g_in = _mm(sv["h"], dp, ta=True, name="odd_in_dw")
    dx, g_mix = _rmsnorm_bwd(_cols(sv["x"], D_MODEL, 0, ts), w["mix_g"], dh, dy, n=D_MODEL, s=s, ts=ts,
                             name="mix_norm_bwd")
    return dx, dict(w_in=g_in, wg2=g_wg, bg2=g_bg, w_out=g_out, mix_g=g_mix, gla_gain=g_gain)


_EVEN_NAMES = dict(mix_norm="mix_norm_even", w_in="w_in_even", q_norm="mla_q_norm", kv_norm="mla_kv_norm",
                   w_uq="mla_w_uq", w_ukv="mla_w_ukv", q_head_norm="mla_q_head_norm", k_head_norm="mla_k_head_norm",
                   ret_out_norm="ret_out_norm", w_out="w_out_even")
_ODD_NAMES = dict(mix_norm="mix_norm_odd", w_in="w_in_odd", w_gate_fwd="gla_w_gate_fwd", b_gate_fwd="gla_b_gate_fwd",
                  w_gate_bwd="gla_w_gate_bwd", b_gate_bwd="gla_b_gate_bwd", gla_out_norm="gla_out_norm",
                  w_out="w_out_odd")

def _local_step(x, pos, target, full, side=(), finish=None):
    s = x.shape[0]
    ts = min(s, EW_ROWS)
    tabs = _rope_tables(pos, MLA_ROPE, MLA_NOPE) + _rope_tables(pos, RET_DK, 0)

    def layer_weights(layer):
        i = layer // 2
        names = _EVEN_NAMES if layer % 2 == 0 else _ODD_NAMES
        wm, unpack_m = _packed(_pack_even if layer % 2 == 0 else _pack_odd, {k: full[n][i] for k, n in names.items()})
        if layer % 2 == 0:
            th = jnp.stack([full["ret_theta_fwd"][i], full["ret_theta_bwd"][i]]).astype(F32)
            wm["theta_h"] = jnp.broadcast_to(th[:, :, None], (2, RET_H, LANES))
            wm["theta_l"] = wm["theta_h"].reshape(2, 1, RET_H * LANES)
        w_up4, index = full["ffn_w_up"][layer]
        wf = dict(layer=index, w_up4=w_up4, w_down=_bf(full["ffn_w_down"][layer]),
                  conv_w=full["ffn_conv_w"][layer].astype(F32), conv_b=full["ffn_conv_b"][layer][None, :].astype(F32),
                  norm_g=full["ffn_norm"][layer][None, :].astype(F32))
        return wm, unpack_m, wf

    layers, saved = [], []
    for layer in range(DEPTH):
        layers.append(layer_weights(layer))
        wm, _, wf = layers[-1]
        if layer % 2 == 0:
            x, sv_m, gathered = _even_fwd(x, wm, tabs, s=s, ts=ts, side=side if layer == 0 else ())
            if layer == 0 and finish is not None:
                full = finish(gathered)
        else:
            x, sv_m = _odd_fwd(x, wm, s=s, ts=ts)
        x, sv_f = _ffn_fwd(x, wf, s=s, ts=ts)
        saved.append((sv_m, sv_f))

    dy, sq = _loss_head(x, target, s=s, ts=ts)
    loss = 0.5 / D_MODEL * jnp.sum(sq)

    grads = {}

    def put(name, idx, g):
        grads.setdefault(name, {})[idx] = g

    for layer in reversed(range(DEPTH)):
        wm, unpack_m, wf = layers[layer]
        sv_m, sv_f = saved[layer]
        i = layer // 2
        dy, gf = _ffn_bwd(dy, wf, sv_f, s=s, ts=ts)
        put("ffn_w_up", layer, gf["w_up"])
        put("ffn_w_down", layer, gf["w_down"])
        put("ffn_conv_w", layer, gf["conv_w"])
        put("ffn_conv_b", layer, gf["conv_b"][0])
        put("ffn_norm", layer, gf["norm_g"][0])
        if layer % 2 == 0:
            dy, gm, dth = _even_bwd(dy, wm, sv_m, tabs, s=s, ts=ts)
            put("ret_theta_fwd", i, dth[0])
            put("ret_theta_bwd", i, dth[1])
            names = _EVEN_NAMES
        else:
            dy, gm = _odd_bwd(dy, wm, sv_m, s=s, ts=ts)
            names = _ODD_NAMES
        for k, g in unpack_m(gm).items():
            put(names[k], i, g)
    return loss, dy, {n: [g[j] for j in range(len(g))] for n, g in grads.items()}


HBM_SPEC = pl.BlockSpec(memory_space=pltpu.HBM)
VMEM_SPEC = pl.BlockSpec(memory_space=pltpu.VMEM)
CHIPS = 4
CORES = 2
ROW = 8 * LANES


def _xyc():
    return lax.axis_index("x"), lax.axis_index("y"), lax.axis_index("c")


def _other_chips(x, y):
    return [(1 - x, y), (x, 1 - y), (1 - x, 1 - y)]


def _remote(src, dst, send, recv, dev):
    return pltpu.make_async_remote_copy(src_ref=src, dst_ref=dst, send_sem=send, recv_sem=recv,
                                        device_id=dev, device_id_type=MESH)


def _sems(n):
    return pltpu.SemaphoreType.DMA((n,))


def _gather_copies(side, srcs, lands, send, recv, loc):
    n = len(side)
    x, y, c = _xyc()
    me = 2 * x + y
    local, sends, arrivals = [], [], []
    for t, (_, first, count) in enumerate(side):
        src = srcs[t].at[pl.ds(first, count)]
        local.append(pltpu.make_async_copy(src, lands[t].at[me], loc.at[t]))
        for j, (px, py) in enumerate(_other_chips(x, y)):
            k = n * j + t
            sends.append(_remote(src, lands[t].at[me], send.at[k], recv.at[k], (px, py, c)))
            arrivals.append(_remote(src, lands[t].at[2 * px + py], send.at[k], recv.at[k], (px, py, c)))
    return local, sends, arrivals


def _gather_shapes(side):
    return [jax.ShapeDtypeStruct((CHIPS, count) + a.shape[1:], a.dtype) for a, _, count in side]


def _gather_chips(side):
    n = len(side)

    def body(*refs):
        local, sends, arrivals = _gather_copies(side, refs[:n], refs[n:2 * n], *refs[2 * n:])
        for cp in local + sends:
            cp.start()
        for cp in arrivals:
            cp.wait_recv()
        for cp in sends:
            cp.wait_send()
        for cp in local:
            cp.wait()

    return pl.pallas_call(
        body, name="gather_chips", in_specs=[HBM_SPEC] * n, out_specs=[HBM_SPEC] * n,
        out_shape=_gather_shapes(side),
        scratch_shapes=[_sems(3 * n), _sems(3 * n), _sems(n)],
    )(*[a for a, _, _ in side])


def _half_rows(ref, axis, half, which):
    idx = (slice(None),) * axis + (pl.ds(pl.multiple_of(which * half, 8), half),)
    return ref.at[idx]


def _swap_halves(arrs):
    n = len(arrs)

    def body(*refs):
        ins, outs = refs[:n], refs[n:2 * n]
        send, recv = refs[2 * n:]
        x, y, c = _xyc()
        copies = []
        for t in range(n):
            half = arrs[t].shape[2] // CORES
            cp = _remote(_half_rows(ins[t], 2, half, 1 - c), outs[t], send.at[t], recv.at[t], (x, y, 1 - c))
            cp.start()
            copies.append(cp)
        for cp in copies:
            cp.wait()

    return pl.pallas_call(
        body, name="swap_halves", in_specs=[HBM_SPEC] * n, out_specs=[HBM_SPEC] * n,
        out_shape=[jax.ShapeDtypeStruct(a.shape[:2] + (a.shape[2] // CORES, a.shape[3]), a.dtype) for a in arrs],
        scratch_shapes=[_sems(n), _sems(n)],
    )(*arrs)


def _add_core_halves(a, got, core, *, ts, name):
    ch, nl, r, cols = a.shape
    half = r // CORES
    nb = half // ts

    def body(core_ref, a_ref, g_ref, o_ref):
        o_ref[...] = (a_ref[...] + g_ref[...]).astype(o_ref.dtype)

    rows = pl.BlockSpec((ts, cols), lambda g, i, cr: (g * nb + i, 0))
    return pl.pallas_call(
        body, name=name, out_shape=jax.ShapeDtypeStruct((ch * nl * half, cols), BF16),
        grid_spec=pltpu.PrefetchScalarGridSpec(
            num_scalar_prefetch=1, grid=(ch * nl, nb),
            in_specs=[pl.BlockSpec((ts, cols), lambda g, i, cr: (g * (r // ts) + cr[0] * nb + i, 0)), rows],
            out_specs=rows),
        compiler_params=_cp(("arbitrary", "arbitrary")),
    )(core, a.reshape(-1, cols), got.reshape(-1, cols)).reshape(got.shape)


def _add_chip_parts(parts, core, *, ts, name):
    ch, nl, half, cols = parts.shape
    nb = half // ts
    r = half * CORES

    def body(core_ref, *refs):
        acc = refs[0][...].astype(F32)
        for p in refs[1:ch]:
            acc = acc + p[...].astype(F32)
        refs[ch][...] = acc

    return pl.pallas_call(
        body, name=name, out_shape=jax.ShapeDtypeStruct((nl * r, cols), F32),
        grid_spec=pltpu.PrefetchScalarGridSpec(
            num_scalar_prefetch=1, grid=(nl, nb),
            in_specs=[pl.BlockSpec((ts, cols), lambda l, i, cr, j=j: ((j * nl + l) * nb + i, 0)) for j in range(ch)],
            out_specs=pl.BlockSpec((ts, cols), lambda l, i, cr: (l * (r // ts) + cr[0] * nb + i, 0))),
        compiler_params=_cp(("arbitrary", "arbitrary")),
    )(core, *[parts.reshape(-1, cols)] * ch).reshape(nl, r, cols)


def _scatter_chips(arrs):
    n = len(arrs)

    def body(*refs):
        ins, outs = refs[:n], refs[n:2 * n]
        send, recv, loc = refs[2 * n:]
        x, y, c = _xyc()
        me = 2 * x + y
        copies = []
        for t in range(n):
            cp = pltpu.make_async_copy(ins[t].at[me], outs[t].at[me], loc.at[t])
            cp.start()
            copies.append(cp)
        sends = []
        for j, (px, py) in enumerate(_other_chips(x, y)):
            for t in range(n):
                cp = _remote(ins[t].at[2 * px + py], outs[t].at[me], send.at[n * j + t], recv.at[n * j + t], (px, py, c))
                cp.start()
                sends.append(cp)
        for j, (px, py) in enumerate(_other_chips(x, y)):
            for t in range(n):
                _remote(ins[t].at[me], outs[t].at[2 * px + py], send.at[n * j + t], recv.at[n * j + t],
                        (px, py, c)).wait_recv()
        for cp in sends:
            cp.wait_send()
        for cp in copies:
            cp.wait()

    return pl.pallas_call(
        body, name="scatter_chips", in_specs=[HBM_SPEC] * n, out_specs=[HBM_SPEC] * n,
        out_shape=[jax.ShapeDtypeStruct(a.shape, a.dtype) for a in arrs],
        scratch_shapes=[_sems(3 * n), _sems(3 * n), _sems(n)],
    )(*arrs)


def _gather_cores(arrs):
    n = len(arrs)

    def body(*refs):
        ins, outs = refs[:n], refs[n:2 * n]
        send, recv = refs[2 * n:]
        x, y, c = _xyc()
        sends = []
        for t in range(n):
            half = arrs[t].shape[1] // CORES
            cp = _remote(_half_rows(ins[t], 1, half, c), _half_rows(outs[t], 1, half, c), send.at[t], recv.at[t],
                         (x, y, 1 - c))
            cp.start()
            sends.append(cp)
        for t in range(n):
            half = arrs[t].shape[1] // CORES
            _remote(_half_rows(ins[t], 1, half, 1 - c), _half_rows(outs[t], 1, half, 1 - c), send.at[t], recv.at[t],
                    (x, y, 1 - c)).wait_recv()
        for cp in sends:
            cp.wait_send()

    return pl.pallas_call(
        body, name="gather_cores", in_specs=[HBM_SPEC] * n, out_specs=[HBM_SPEC] * n,
        out_shape=[jax.ShapeDtypeStruct(a.shape, a.dtype) for a in arrs],
        input_output_aliases={t: t for t in range(n)},
        scratch_shapes=[_sems(n), _sems(n)],
    )(*arrs)


def _all_reduce_devices(v):
    n_dev = CHIPS * CORES

    def body(v_ref, o_ref, buf, send, recv):
        x, y, c = _xyc()
        me = 4 * x + 2 * y + c
        buf[pl.ds(me, 1)] = v_ref[...][None]
        sends = []
        for m in range(1, n_dev):
            px = 1 - x if m & 4 else x
            py = 1 - y if m & 2 else y
            pc = 1 - c if m & 1 else c
            cp = _remote(v_ref, buf.at[me], send.at[m - 1], recv.at[m - 1], (px, py, pc))
            cp.start()
            sends.append((cp, 4 * px + 2 * py + pc))
        for m, (cp, peer) in enumerate(sends):
            _remote(v_ref, buf.at[peer], send.at[m], recv.at[m], (x, y, c)).wait_recv()
        for cp, _ in sends:
            cp.wait_send()
        acc = buf[0]
        for k in range(1, n_dev):
            acc = acc + buf[k]
        o_ref[...] = acc

    return pl.pallas_call(
        body, name="all_reduce_devices", in_specs=[VMEM_SPEC], out_specs=VMEM_SPEC,
        out_shape=jax.ShapeDtypeStruct(v.shape, F32),
        scratch_shapes=[pltpu.VMEM((n_dev,) + v.shape, F32), pltpu.SemaphoreType.DMA((n_dev - 1,)),
                        pltpu.SemaphoreType.DMA((n_dev - 1,))],
    )(v)


_SHARDED = (("w_in_even", 2), ("mla_w_uq", 2), ("mla_w_ukv", 2), ("w_out_even", 1), ("w_in_odd", 2), ("w_out_odd", 1),
            ("ffn_w_up", 2), ("ffn_w_down", 1),
            ("mix_norm_odd", 1), ("gla_w_gate_fwd", 2), ("gla_b_gate_fwd", 1), ("gla_w_gate_bwd", 2),
            ("gla_b_gate_bwd", 1), ("gla_out_norm", 2), ("ffn_conv_w", 2))
_N_MATRICES = 8
_REPLICATED = ("mix_norm_even", "mla_q_norm", "mla_kv_norm", "mla_q_head_norm", "mla_k_head_norm", "ret_theta_fwd",
               "ret_theta_bwd", "ret_out_norm", "ffn_norm", "ffn_conv_b")
_WEIGHTS = ("mix_norm_even", "w_in_even", "mla_q_norm", "mla_kv_norm", "mla_w_uq", "mla_w_ukv", "mla_q_head_norm",
            "mla_k_head_norm", "ret_theta_fwd", "ret_theta_bwd", "ret_out_norm", "w_out_even", "mix_norm_odd",
            "w_in_odd", "gla_w_gate_fwd", "gla_b_gate_fwd", "gla_w_gate_bwd", "gla_b_gate_bwd", "gla_out_norm",
            "w_out_odd", "ffn_norm", "ffn_w_up", "ffn_conv_w", "ffn_conv_b", "ffn_w_down")


def _flatten(arrs, row_multiple, dtype):
    flat = jnp.concatenate([a.reshape(-1).astype(dtype) for a in arrs])
    per = ROW * row_multiple
    total = -(-flat.shape[0] // per) * per
    return jnp.pad(flat, (0, total - flat.shape[0])).reshape(-1, ROW)


def _unflatten(flat, shapes):
    flat = flat.reshape(-1)
    out, o = [], 0
    for shp in shapes:
        n = math.prod(shp)
        out.append(flat[o:o + n].reshape(shp))
        o += n
    return out


def kernel(x, positions, mix_norm_even, w_in_even, mla_q_norm, mla_kv_norm, mla_w_uq, mla_w_ukv, mla_q_head_norm, mla_k_head_norm, ret_theta_fwd, ret_theta_bwd, ret_out_norm, w_out_even, mix_norm_odd, w_in_odd, gla_w_gate_fwd, gla_b_gate_fwd, gla_w_gate_bwd, gla_b_gate_bwd, gla_out_norm, w_out_odd, ffn_norm, ffn_w_up, ffn_conv_w, ffn_conv_b, ffn_w_down, loss_target, m_mix_norm_even, m_w_in_even, m_mla_q_norm, m_mla_kv_norm, m_mla_w_uq, m_mla_w_ukv, m_mla_q_head_norm, m_mla_k_head_norm, m_ret_theta_fwd, m_ret_theta_bwd, m_ret_out_norm, m_w_out_even, m_mix_norm_odd, m_w_in_odd, m_gla_w_gate_fwd, m_gla_b_gate_fwd, m_gla_w_gate_bwd, m_gla_b_gate_bwd, m_gla_out_norm, m_w_out_odd, m_ffn_norm, m_ffn_w_up, m_ffn_conv_w, m_ffn_conv_b, m_ffn_w_down, v_mix_norm_even, v_w_in_even, v_mla_q_norm, v_mla_kv_norm, v_mla_w_uq, v_mla_w_ukv, v_mla_q_head_norm, v_mla_k_head_norm, v_ret_theta_fwd, v_ret_theta_bwd, v_ret_out_norm, v_w_out_even, v_mix_norm_odd, v_w_in_odd, v_gla_w_gate_fwd, v_gla_b_gate_fwd, v_gla_w_gate_bwd, v_gla_b_gate_bwd, v_gla_out_norm, v_w_out_odd, v_ffn_norm, v_ffn_w_up, v_ffn_conv_w, v_ffn_conv_b, v_ffn_w_down):
    args = dict(locals())
    x2, pos, target = args["x"][0], args["positions"][0], args["loss_target"][0]
    axis = dict(_SHARDED)
    mats = [n for n, _ in _SHARDED[:_N_MATRICES]]
    smalls = [n for n, _ in _SHARDED[_N_MATRICES:]]
    small_shapes = [args[n].shape for n in smalls]

    local = {n: _bf(args[n]) for n in mats}
    first_layers = {n: (0, 0 if n.endswith("_odd") else 1) for n in mats}
    now = [(local[n],) + first_layers[n] for n in mats if first_layers[n][1]]
    later = [(local[n], first_layers[n][1], args[n].shape[0] - first_layers[n][1]) for n in mats]
    small_block = _flatten([args[n] for n in smalls], 2 * HALO, F32)
    got_now = _gather_chips(now + [(small_block, 0, small_block.shape[0])])
    per_chip = [_unflatten(got_now[-1][j], small_shapes) for j in range(CHIPS)]
    base = {n: args[n] for n in _REPLICATED}
    for k, n in enumerate(smalls):
        base[n] = jnp.concatenate([per_chip[j][k] for j in range(CHIPS)], axis=axis[n])

    def whole(stacks):
        full = dict(base)
        for n, per_layer in stacks.items():
            if n == "ffn_w_up":
                full[n] = per_layer
            else:
                full[n] = [None if st is None else jnp.concatenate([st[j, l] for j in range(CHIPS)], axis=axis[n] - 1)
                           for st, l in per_layer]
        return full

    stacks = {n: [(None, 0)] * args[n].shape[0] for n in mats}
    for (a, first, count), st in zip(now, got_now):
        n = next(m for m in mats if local[m] is a)
        stacks[n] = [(st, l) for l in range(count)] + stacks[n][count:]

    def finish(got_later):
        for (a, first, count), st in zip(later, got_later):
            n = next(m for m in mats if local[m] is a)
            stacks[n] = stacks[n][:first] + [(st, l) for l in range(count)]
        return whole(stacks)

    loss, grad_x, grads = _local_step(x2, pos, target, whole(stacks), side=later, finish=finish)
    loss = lax.psum(loss, ("x", "y", "c"))

    def by_chip(n, g):
        if n == "ffn_w_up":
            return g
        if axis[n] == 1:
            return g.reshape((CHIPS, g.shape[0] // CHIPS) + g.shape[1:])
        return jnp.stack(jnp.split(g, CHIPS, axis=axis[n] - 1))

    core = lax.axis_index("c").astype(jnp.int32).reshape(1)
    stacked = [jnp.stack([by_chip(n, g) for g in grads[n]], axis=1) for n in mats]
    small_parts = [jnp.split(jnp.stack(grads[n]), CHIPS, axis=axis[n]) for n in smalls]
    stacked.append(jnp.stack([_flatten([p[j] for p in small_parts], 2 * HALO, F32) for j in range(CHIPS)])[:, None])
    names = mats + ["small"]
    tiles = [_rows_tile(a.shape[2] // CORES, a.shape[3]) for a in stacked]
    got = _swap_halves(stacked)
    chip_sums = [_add_core_halves(a, b, core, ts=ts, name="add_core_halves_" + n)
                 for n, a, b, ts in zip(names, stacked, got, tiles)]
    parts = _scatter_chips(chip_sums)
    sums = [_add_chip_parts(p, core, ts=ts, name="add_chip_parts_" + n) for n, p, ts in zip(names, parts, tiles)]
    reduced = _gather_cores(sums)

    res = {}

    def update(n, w, g, m, v, ts):
        cols = g.shape[-1]
        outs = _adamw(w.reshape(-1, cols), g.reshape(-1, cols), m.reshape(-1, cols), v.reshape(-1, cols), ts=ts,
                      name="adamw_" + n)
        return [g] + [o.reshape(g.shape) for o in outs]

    kinds = ("grad", "delta", "new_m", "new_v")
    for n, g, ts in zip(mats, reduced, tiles):
        for kind, a in zip(kinds, update(n, args[n], g, args["m_" + n], args["v_" + n], ts)):
            res[kind + "_" + n] = a
    w_s, m_s, v_s = (_flatten([args[pre + n] for n in smalls], 2 * HALO, F32) for pre in ("", "m_", "v_"))
    for kind, flat in zip(kinds, update("small", w_s, reduced[-1][0], m_s, v_s, tiles[-1])):
        for n, a in zip(smalls, _unflatten(flat, small_shapes)):
            res[kind + "_" + n] = a

    rep_shapes = [args[n].shape for n in _REPLICATED]
    g_rep = _all_reduce_devices(_flatten([jnp.stack(grads[n]) for n in _REPLICATED], HALO, F32))
    w_rep, m_rep, v_rep = (_flatten([args[pre + n] for n in _REPLICATED], HALO, F32) for pre in ("", "m_", "v_"))
    for kind, flat in zip(kinds, update("replicated", w_rep, g_rep, m_rep, v_rep, g_rep.shape[0])):
        for n, a in zip(_REPLICATED, _unflatten(flat, rep_shapes)):
            res[kind + "_" + n] = a

    outs = [loss, grad_x[None]]
    for kind in ("grad", "delta", "new_m", "new_v"):
        outs += [res[kind + "_" + n] for n in _WEIGHTS]
    return tuple(outs)
```

```python
import math

import jax
import jax.numpy as jnp
from jax import lax
from jax.experimental import pallas as pl
from jax.experimental.pallas import tpu as pltpu

F32 = jnp.float32
BF16 = jnp.bfloat16
MESH = pl.DeviceIdType.MESH

EPS = 1e-6
D_MODEL = 1024
DEPTH = 4
LANES = 128
MLA_H, MLA_QR, MLA_KVR, MLA_NOPE, MLA_ROPE, MLA_V = 8, 384, 256, 64, 32, 64
MLA_QK = MLA_NOPE + MLA_ROPE
MLA_SCALE = MLA_QK ** -0.5
RET_H, RET_DK, RET_DV, RET_C = 8, 64, 64, 128
GLA_H, GLA_DK, GLA_DV, GLA_R, GLA_TAU, GLA_C = 4, 128, 256, 16, 16.0, 64
D_FF = 2816
ROPE_THETA = 10000.0
LN2 = math.log(2.0)
ADAM_LR, ADAM_B1, ADAM_B2, ADAM_EPS, ADAM_WD, ADAM_STEP = 0.001, 0.9, 0.999, 1e-08, 0.01, 10

EV_RET = 4 * RET_H * LANES
EV_CQ = 512
EV_W = 5120
EV_KR_BLK = (EV_RET + EV_CQ + MLA_KVR) // LANES
OD_W = 3200
OD_GA_BLK = 3072 // LANES

VMEM_LIMIT = 56 * 1024 * 1024
MM_TILE_CAP = 1408
EW_ROWS = 512
FFN_ACT_ROWS = 512
V_ONES = (MLA_V, MLA_V + 1)
FLASH_FWD_ROWS = 2048
FLASH_BWD_ROWS = 1024
FLASH_KEYS = 1024


def _cp(sem):
    return pltpu.CompilerParams(dimension_semantics=sem, vmem_limit_bytes=VMEM_LIMIT)


def _dot(a, b):
    return jnp.dot(a, b, preferred_element_type=F32)


def _dot_nt(a, b):
    return lax.dot_general(a, b, (((1,), (1,)), ((), ())), preferred_element_type=F32)


def _dot_tn(a, b):
    return lax.dot_general(a, b, (((0,), (0,)), ((), ())), preferred_element_type=F32)


def _bf(x):
    return x.astype(BF16)


def _split3(x):
    h1 = _bf(x)
    r1 = x - h1.astype(F32)
    h2 = _bf(r1)
    h3 = _bf(r1 - h2.astype(F32))
    return h1, h2, h3


def _tile(n, cap):
    if n <= cap:
        return n
    best = None
    for t in range(LANES, cap + 1, LANES):
        if n % t == 0:
            best = t
    assert best is not None, n
    return best


def _mm(a, b, *, ta=False, tb=False, res=None, out_dtype=F32, b_layer=None, out_chips=False, halves=None, name):
    assert not (ta and tb)
    if halves == "a":
        assert not ta
        m, kdim = a.shape[1], 2 * a.shape[2]
    elif ta:
        kdim, m = a.shape
    else:
        m, kdim = a.shape
    if b_layer is not None:
        rows_b, cols_b = b.shape[2], b.shape[0] * b.shape[3]
    elif halves == "b":
        assert not tb
        rows_b, cols_b = b.shape[1], 2 * b.shape[2]
    else:
        rows_b, cols_b = b.shape
    n, kb = (rows_b, cols_b) if tb else (cols_b, rows_b)
    assert kb == kdim, (a.shape, b.shape, ta, tb)
    tm, tn, tk = _tile(m, MM_TILE_CAP), _tile(n, MM_TILE_CAP), _tile(kdim, MM_TILE_CAP)
    nk = kdim // tk
    has_res = res is not None
    vmem = (2 * tm * tk * a.dtype.itemsize + 2 * tk * tn * b.dtype.itemsize
            + 2 * tm * tn * jnp.dtype(out_dtype).itemsize + (2 * tm * tn * 4 if has_res else 0)
            + (tm * tn * 4 if nk > 1 else 0))
    assert vmem <= VMEM_LIMIT - 8 * 1024 * 1024, (name, vmem)
    a_spec = (pl.BlockSpec((tk, tm), lambda i, j, k: (k, i)) if ta
              else pl.BlockSpec((tm, tk), lambda i, j, k: (i, k)))
    if halves == "a":
        per_half = a.shape[2] // tk
        a_spec = pl.BlockSpec((None, tm, tk), lambda i, j, k: (k // per_half, i, k % per_half))
    if halves == "b":
        per_half = b.shape[2] // tn
        b_spec = pl.BlockSpec((None, tk, tn), lambda i, j, k: (j // per_half, k, j % per_half))
    elif b_layer is not None:
        per_chip = b.shape[3]
        if tb:
            assert tk == per_chip
            b_spec = pl.BlockSpec((None, None, tn, tk), lambda i, j, k: (k, b_layer, j, 0))
        else:
            assert tn == per_chip
            b_spec = pl.BlockSpec((None, None, tk, tn), lambda i, j, k: (j, b_layer, k, 0))
    else:
        b_spec = (pl.BlockSpec((tn, tk), lambda i, j, k: (j, k)) if tb
                  else pl.BlockSpec((tk, tn), lambda i, j, k: (k, j)))
    if out_chips:
        assert n // tn == CHIPS and not has_res
        o_spec = pl.BlockSpec((None, tm, tn), lambda i, j, k: (j, i, 0))
        out_struct = jax.ShapeDtypeStruct((CHIPS, m, tn), out_dtype)
    else:
        o_spec = pl.BlockSpec((tm, tn), lambda i, j, k: (i, j))
        out_struct = jax.ShapeDtypeStruct((m, n), out_dtype)

    def product(a_ref, b_ref):
        av, bv = _bf(a_ref[...]), _bf(b_ref[...])
        if ta:
            return _dot_tn(av, bv)
        if tb:
            return _dot_nt(av, bv)
        return _dot(av, bv)

    def body(*refs):
        a_ref, b_ref = refs[:2]
        r_ref = refs[2] if has_res else None
        o_ref = refs[3] if has_res else refs[2]

        def finish(r):
            if has_res:
                r = r + r_ref[...]
            o_ref[...] = r.astype(o_ref.dtype)

        if nk == 1:
            finish(product(a_ref, b_ref))
            return
        acc = refs[-1]
        k = pl.program_id(2)

        @pl.when(k == 0)
        def _():
            acc[...] = product(a_ref, b_ref)

        @pl.when(k > 0)
        def _():
            acc[...] += product(a_ref, b_ref)

        @pl.when(k == nk - 1)
        def _():
            finish(acc[...])

    ins = [a, b] + ([res] if has_res else [])
    in_specs = [a_spec, b_spec] + ([o_spec] if has_res else [])
    return pl.pallas_call(
        body, name=name, grid=(m // tm, n // tn, nk),
        in_specs=in_specs, out_specs=o_spec, out_shape=out_struct,
        scratch_shapes=[pltpu.VMEM((tm, tn), F32)] if nk > 1 else [],
        compiler_params=_cp(("parallel", "parallel", "arbitrary")),
    )(*ins)


def _ew(fn, rows, pars, outs, accs=(), *, s, ts, name):
    n_in = len(rows) + len(pars)
    n_o = len(outs)

    def body(*refs):
        i = pl.program_id(0)
        vals = fn(*[r[...] for r in refs[:n_in]])
        if not isinstance(vals, (tuple, list)):
            vals = (vals,)
        assert len(vals) == n_o + len(accs), (name, len(vals))
        for r, v in zip(refs[n_in:n_in + n_o], vals[:n_o]):
            r[...] = v.astype(r.dtype)
        for r, v in zip(refs[n_in + n_o:], vals[n_o:]):
            @pl.when(i == 0)
            def _(r=r, v=v):
                r[...] = v

            @pl.when(i > 0)
            def _(r=r, v=v):
                r[...] += v

    in_specs = [sp for _, sp in rows]
    in_specs += [pl.BlockSpec(p.shape, lambda i, nd=p.ndim: (0,) * nd) for p in pars]
    out_specs = [pl.BlockSpec((ts, w), lambda i: (i, 0)) for w, _ in outs]
    out_specs += [pl.BlockSpec((r, w), lambda i: (0, 0)) for r, w in accs]
    out_shape = [jax.ShapeDtypeStruct((s, w), dt) for w, dt in outs]
    out_shape += [jax.ShapeDtypeStruct((r, w), F32) for r, w in accs]
    return pl.pallas_call(
        body, name=name, grid=(s // ts,), in_specs=in_specs, out_specs=out_specs, out_shape=out_shape,
        compiler_params=_cp(("arbitrary",)),
    )(*[a for a, _ in rows], *pars)


def _cols(arr, width, blk, ts):
    return (arr, pl.BlockSpec((ts, width), lambda i, b=blk: (i, b)))


def _lead(pair, d, ts):
    return _cols(pair[d], pair[d].shape[1], 0, ts)


def _rowsum(x):
    return jnp.sum(x, axis=0, keepdims=True)


def _lanesum(x):
    return jnp.sum(x, axis=-1, keepdims=True)


def _gsum(x, group):
    w = x.shape[-1]
    if group == w:
        return jnp.broadcast_to(_lanesum(x), x.shape)
    parts = [jnp.broadcast_to(_lanesum(x[:, g:g + group]), (x.shape[0], group)) for g in range(0, w, group)]
    return jnp.concatenate(parts, axis=-1)


def _gn(x, gain, group, n):
    rstd = lax.rsqrt(_gsum(x * x, group) * (1.0 / n) + EPS)
    xn = x * rstd
    return xn * gain, xn, rstd


def _gn_bwd(dy, xn, rstd, gain, group, n):
    dxn = dy * gain
    dx = rstd * (dxn - xn * (_gsum(dxn * xn, group) * (1.0 / n)))
    return dx, _rowsum(dy * xn)


def _sigmoid(x):
    return 1.0 / (1.0 + jnp.exp(-x))


def _rmsnorm(x_row, g, *, n, s, ts, name):
    w = g.shape[-1]

    def fn(x, gv):
        return _gn(x, gv, w, n)[0]

    return _ew(fn, [x_row], [g], [(w, BF16)], s=s, ts=ts, name=name)[0]


def _rmsnorm_bwd(x_row, g, dh, dres, *, n, s, ts, name):
    w = g.shape[-1]
    has_res = dres is not None

    def fn(x, dhv, *rest):
        gv = rest[-1]
        _, xn, rstd = _gn(x, gv, w, n)
        dx, dg = _gn_bwd(dhv, xn, rstd, gv, w, n)
        if has_res:
            dx = dx + rest[0]
        return dx, dg

    rows = [x_row, _cols(dh, w, 0, ts)] + ([_cols(dres, w, 0, ts)] if has_res else [])
    return _ew(fn, rows, [g], [(w, F32)], [(1, w)], s=s, ts=ts, name=name)


def _rope_tables(pos, real, offset):
    half = real // 2
    inv = ROPE_THETA ** (-jnp.arange(half, dtype=F32) / half)
    ang = pos.astype(F32)[:, None] * inv
    c, sn = jnp.cos(ang), jnp.sin(ang)
    s = pos.shape[0]
    cos_t = jnp.concatenate([jnp.ones((s, offset), F32), c, c,
                             jnp.ones((s, LANES - offset - real), F32)], axis=1)
    sin_t = jnp.concatenate([jnp.zeros((s, offset), F32), -sn, sn,
                             jnp.zeros((s, LANES - offset - real), F32)], axis=1)
    return cos_t, sin_t


def _rope(x, cos_t, sin_t, real, offset):
    half = real // 2
    lane = lax.broadcasted_iota(jnp.int32, x.shape, 1)
    partner = jnp.where(lane < offset + half, pltpu.roll(x, LANES - half, 1), pltpu.roll(x, half, 1))
    return x * cos_t + partner * sin_t


def _mla_prep(q_pre, kv_pre, p_even, cos_m, sin_m, qhn, khn, *, s, ts):
    w = MLA_H * LANES

    def fn(qp, kp, vp, kr, c, sn, gq, gk):
        qs, ks = [], []
        for h in range(MLA_H):
            sl = slice(h * LANES, (h + 1) * LANES)
            qn = _gn(qp[:, sl], gq, LANES, MLA_QK)[0]
            kn = _gn(kp[:, sl] + kr, gk, LANES, MLA_QK)[0]
            qs.append(_rope(qn, c, sn, MLA_ROPE, MLA_NOPE) * MLA_SCALE)
            ks.append(_rope(kn, c, sn, MLA_ROPE, MLA_NOPE))
        lane = lax.broadcasted_iota(jnp.int32, vp.shape, 1) % LANES
        ones = (lane == V_ONES[0]) | (lane == V_ONES[1])
        return jnp.concatenate(qs, axis=1), jnp.concatenate(ks, axis=1), jnp.where(ones, 1.0, vp)

    rows = [_cols(q_pre, w, 0, ts), _cols(kv_pre, w, 0, ts), _cols(kv_pre, w, 1, ts),
            _cols(p_even, LANES, EV_KR_BLK, ts), _cols(cos_m, LANES, 0, ts), _cols(sin_m, LANES, 0, ts)]
    return _ew(fn, rows, [qhn, khn], [(w, BF16)] * 3, s=s, ts=ts, name="mla_prep")


def _mla_prep_bwd(q_pre, kv_pre, p_even, cos_m, sin_m, qhn, khn, dq, dk, *, s, ts):
    w = MLA_H * LANES

    def fn(qp, kp, kr, c, sn, dqv, dkv, gq, gk):
        dqs, dks = [], []
        dkr = jnp.zeros_like(kr)
        dgq = jnp.zeros((1, LANES), F32)
        dgk = jnp.zeros((1, LANES), F32)
        for h in range(MLA_H):
            sl = slice(h * LANES, (h + 1) * LANES)
            _, qn, qr = _gn(qp[:, sl], gq, LANES, MLA_QK)
            _, kn, krs = _gn(kp[:, sl] + kr, gk, LANES, MLA_QK)
            dqn = _rope(dqv[:, sl] * MLA_SCALE, c, -sn, MLA_ROPE, MLA_NOPE)
            dkn = _rope(dkv[:, sl], c, -sn, MLA_ROPE, MLA_NOPE)
            dqh, g1 = _gn_bwd(dqn, qn, qr, gq, LANES, MLA_QK)
            dkh, g2 = _gn_bwd(dkn, kn, krs, gk, LANES, MLA_QK)
            dqs.append(dqh)
            dks.append(dkh)
            dkr = dkr + dkh
            dgq = dgq + g1
            dgk = dgk + g2
        return jnp.concatenate(dqs, axis=1), jnp.concatenate(dks, axis=1), dkr, dgq, dgk

    rows = [_cols(q_pre, w, 0, ts), _cols(kv_pre, w, 0, ts), _cols(p_even, LANES, EV_KR_BLK, ts),
            _cols(cos_m, LANES, 0, ts), _cols(sin_m, LANES, 0, ts), _cols(dq, w, 0, ts), _cols(dk, w, 0, ts)]
    return _ew(fn, rows, [qhn, khn], [(w, BF16), (w, BF16), (LANES, BF16)], [(1, LANES), (1, LANES)],
               s=s, ts=ts, name="mla_prep_bwd")


def _flash_fwd(q, k, v, *, tq, tk, side=()):
    s = q.shape[0]
    nq, nk = s // tq, s // tk
    rq = tq
    ns = len(side)

    def body(*refs):
        q_ref, k_ref, v_ref = refs[:3]
        o_ref, lse_ref = refs[3 + ns:5 + ns]
        m_s, acc = refs[5 + 2 * ns:7 + 2 * ns]
        h, i, j = pl.program_id(0), pl.program_id(1), pl.program_id(2)
        if ns:
            local, sends, arrivals = _gather_copies(side, refs[3:3 + ns], refs[5 + ns:5 + 2 * ns], *refs[7 + 2 * ns:])

            @pl.when((h == 0) & (i == 0) & (j == 0))
            def _():
                for cp in local + sends:
                    cp.start()

        @pl.when(j == 0)
        def _():
            m_s[...] = jnp.full_like(m_s, -jnp.inf)
            acc[...] = jnp.zeros_like(acc)

        kv, vv = k_ref[...], v_ref[...]
        for r in range(0, tq, rq):
            rows = slice(r, r + rq)
            sc = _dot_nt(q_ref[rows, :], kv)
            m_prev = m_s[rows, :]
            m_new = jnp.maximum(m_prev, jnp.max(sc, axis=-1, keepdims=True))
            p = jnp.exp(sc - jnp.tile(m_new, (1, tk // LANES)))
            acc[rows, :] = jnp.exp(m_prev - m_new) * acc[rows, :] + _dot(_bf(p), vv)
            m_s[rows, :] = m_new

        @pl.when(j == nk - 1)
        def _():
            a = acc[...]
            l = a[:, V_ONES[0]:V_ONES[0] + 1]
            o_ref[...] = (a / l).astype(o_ref.dtype)
            lse_ref[...] = (m_s[...] + jnp.log(jnp.broadcast_to(l, (tq, LANES)))).T[0:1, :]

        if ns:
            @pl.when((h == MLA_H - 1) & (i == nq - 1) & (j == nk - 1))
            def _():
                for cp in arrivals:
                    cp.wait_recv()
                for cp in sends:
                    cp.wait_send()
                for cp in local:
                    cp.wait()

    qs = pl.BlockSpec((tq, LANES), lambda h, i, j: (i, h))
    ks = pl.BlockSpec((tk, LANES), lambda h, i, j: (j, h))
    outs = pl.pallas_call(
        body, name="mla_flash_fwd_gather" if ns else "mla_flash_fwd", grid=(MLA_H, nq, nk),
        in_specs=[qs, ks, ks] + [HBM_SPEC] * ns,
        out_specs=[qs, pl.BlockSpec((None, 1, tq), lambda h, i, j: (h, 0, i))] + [HBM_SPEC] * ns,
        out_shape=[jax.ShapeDtypeStruct((s, MLA_H * LANES), BF16), jax.ShapeDtypeStruct((MLA_H, 1, s), F32)]
        + _gather_shapes(side),
        scratch_shapes=[pltpu.VMEM((tq, LANES), F32), pltpu.VMEM((tq, LANES), F32)]
        + ([_sems(3 * ns), _sems(3 * ns), _sems(ns)] if ns else []),
        compiler_params=_cp(("arbitrary",) * 3 if ns else ("parallel", "parallel", "arbitrary")),
    )(q, k, v, *[a for a, _, _ in side])
    return outs[0], outs[1], list(outs[2:])


def _attn_bwd_prep(dar, o, *, s, ts):
    w = MLA_H * LANES

    def fn(dov, ov):
        outs = []
        lane = lax.broadcasted_iota(jnp.int32, (dov.shape[0], LANES), 1)
        for h in range(MLA_H):
            sl = slice(h * LANES, (h + 1) * LANES)
            d = dov[:, sl]
            delta = _lanesum(d * ov[:, sl].astype(F32))
            hi = _bf(delta).astype(F32)
            outs.append(jnp.where(lane == V_ONES[0], -hi, jnp.where(lane == V_ONES[1], hi - delta, d)))
        return jnp.concatenate(outs, axis=1)

    return _ew(fn, [_cols(dar, w, 0, ts), _cols(o, w, 0, ts)], [], [(w, BF16)], s=s, ts=ts,
               name="mla_attn_bwd_prep")[0]


def _flash_bwd(q, k, v, do, lse, *, tq, tk):
    s = q.shape[0]
    nq, nk = s // tq, s // tk

    def body(q_ref, k_ref, v_ref, do_ref, lse_ref, dq_ref, dk_ref, dv_ref, dk_acc, dv_acc):
        j = pl.program_id(1)
        i = pl.program_id(2)
        qv, kv, vv, dov = q_ref[...], k_ref[...], v_ref[...], do_ref[...]
        pt = jnp.exp(_dot_nt(kv, qv) - lse_ref[...])
        dst = _bf(pt * _dot_nt(vv, dov))
        dv_c = _dot(_bf(pt), dov)
        dk_c = _dot(dst, qv)
        dq_c = _dot_tn(dst, kv)
        rows = pl.ds(pl.multiple_of(i * tq, tq), tq)

        @pl.when(i == 0)
        def _():
            dk_acc[...] = dk_c
            dv_acc[...] = dv_c

        @pl.when(i > 0)
        def _():
            dk_acc[...] += dk_c
            dv_acc[...] += dv_c

        @pl.when(j == 0)
        def _():
            dq_ref[rows, :] = dq_c

        @pl.when(j > 0)
        def _():
            dq_ref[rows, :] += dq_c

        @pl.when(i == nq - 1)
        def _():
            dk_ref[...] = dk_acc[...]
            dv_ref[...] = dv_acc[...].astype(dv_ref.dtype)

    qs = pl.BlockSpec((tq, LANES), lambda h, j, i: (i, h))
    ks = pl.BlockSpec((tk, LANES), lambda h, j, i: (j, h))
    st = pl.BlockSpec((None, 1, tq), lambda h, j, i: (h, 0, i))
    return pl.pallas_call(
        body, name="mla_flash_bwd", grid=(MLA_H, nk, nq),
        in_specs=[qs, ks, ks, qs, st],
        out_specs=[pl.BlockSpec((s, LANES), lambda h, j, i: (0, h)), ks, ks],
        out_shape=[jax.ShapeDtypeStruct((s, MLA_H * LANES), F32), jax.ShapeDtypeStruct((s, MLA_H * LANES), F32),
                   jax.ShapeDtypeStruct((s, MLA_H * LANES), BF16)],
        scratch_shapes=[pltpu.VMEM((tk, LANES), F32), pltpu.VMEM((tk, LANES), F32)],
        compiler_params=_cp(("parallel", "arbitrary", "arbitrary")),
    )(q, k, v, do, lse)


def _ret_geometry(d, c):
    df = float(d)
    ii = lax.broadcasted_iota(jnp.int32, (c, c), 0).astype(F32)
    jj = lax.broadcasted_iota(jnp.int32, (c, c), 1).astype(F32)
    rel = (ii - jj) * (1.0 - 2.0 * df)
    mask = rel >= df
    rel0 = jnp.maximum(rel, 0.0)
    pos = lax.broadcasted_iota(jnp.int32, (c, 1), 0).astype(F32)
    ez = (c - 1 - pos) + df * (2.0 * pos - (c - 1))
    ex = (pos + 1.0) + df * (c - 1 - 2.0 * pos)
    return mask, rel0, ez, ex


def _chunk_index(n_chunks):
    return lambda d, n: n + d * (n_chunks - 1 - 2 * n)


def _ret_fwd(p_even, cos_r, sin_r, theta_l):
    s = p_even.shape[0]
    c = RET_C
    n_chunks = s // c
    w = RET_H * LANES
    cidx = _chunk_index(n_chunks)

    def body(*refs):
        n = pl.program_id(0)

        @pl.when(n == 0)
        def _():
            for r_s in refs[16:18]:
                r_s[...] = jnp.zeros_like(r_s)

        stores = []
        for d in range(2):
            stores += one(d, *refs[6 * d:6 * d + 6], *refs[12 + 2 * d:14 + 2 * d], refs[16 + d])
        for ref, val in stores:
            ref[...] = val

    def one(d, q_ref, k_ref, v_ref, cos_ref, sin_ref, th_ref, o_ref, rp_ref, r_s):
        lg = jnp.log1p(-jnp.exp(-th_ref[...] * LN2))
        mask, rel0, ez, ex = _ret_geometry(d, c)
        cs, sn = cos_ref[...], sin_ref[...]
        r_all = r_s[...]
        outs, states = [], []
        for h in range(RET_H):
            sl = slice(h * LANES, (h + 1) * LANES)
            lgh = lg[:, h * LANES:h * LANES + 1]
            dm = jnp.where(mask, jnp.exp(lgh * rel0), 0.0)
            qh = _bf(_rope(q_ref[:, sl], cs, sn, RET_DK, 0))
            kf = _rope(k_ref[:, sl], cs, sn, RET_DK, 0) * (RET_DK ** -0.5)
            kh = _bf(kf)
            vh = _bf(v_ref[:, sl])
            rh = r_all[sl, :]
            a = _dot_nt(qh, kh) * dm
            outs.append(_dot(_bf(a), vh) + jnp.exp(lgh * ex) * _dot(qh, _bf(rh)))
            zk = _bf(kf * jnp.exp(lgh * ez))
            states.append(jnp.exp(lgh * c) * rh + _dot_tn(zk, vh))
        return [(rp_ref, r_all), (o_ref, jnp.concatenate(outs, axis=1)), (r_s, jnp.concatenate(states, axis=0))]

    def ins(d):
        col = lambda blk: pl.BlockSpec((c, w), lambda n: (cidx(d, n), blk))
        tab = pl.BlockSpec((c, LANES), lambda n: (cidx(d, n), 0))
        return [col(0), col(1), col(2), tab, tab, pl.BlockSpec((None, 1, w), lambda n: (d, 0, 0))]

    def outs(d):
        return [pl.BlockSpec((c, w), lambda n: (cidx(d, n), 0)),
                pl.BlockSpec((None, w, LANES), lambda n: (cidx(d, n), 0, 0))]

    o_f, r_f, o_b, r_b = pl.pallas_call(
        body, name="ret_fwd", grid=(n_chunks,),
        in_specs=ins(0) + ins(1), out_specs=outs(0) + outs(1),
        out_shape=[jax.ShapeDtypeStruct((s, w), F32), jax.ShapeDtypeStruct((n_chunks, w, LANES), F32)] * 2,
        scratch_shapes=[pltpu.VMEM((w, LANES), F32)] * 2,
        compiler_params=_cp(("arbitrary",)),
    )(*[p_even, p_even, p_even, cos_r, sin_r, theta_l] * 2)
    return (o_f, o_b), (r_f, r_b)


def _ret_bwd(p_even, cos_r, sin_r, theta_l, theta_h, r_prev, do):
    s = p_even.shape[0]
    c = RET_C
    n_chunks = s // c
    w = RET_H * LANES
    fwd_idx = _chunk_index(n_chunks)

    def cidx(d, n):
        return fwd_idx(d, n_chunks - 1 - n)

    def body(*refs):
        n = pl.program_id(0)

        @pl.when(n == 0)
        def _():
            for d in range(2):
                refs[26 + d][...] = jnp.zeros_like(refs[26 + d])
                refs[21 + 4 * d][...] = jnp.zeros_like(refs[21 + 4 * d])

        stores = []
        for d in range(2):
            stores += one(d, *refs[9 * d:9 * d + 9], *refs[18 + 4 * d:22 + 4 * d], refs[26 + d])
        for ref, val, accumulate in stores:
            if accumulate:
                ref[...] += val
            else:
                ref[...] = val

    def one(d, q_ref, k_ref, v_ref, cos_ref, sin_ref, th_ref, thh_ref, rp_ref, do_ref,
            dq_ref, dk_ref, dv_ref, dth_ref, dr_s):
        lg = jnp.log1p(-jnp.exp(-th_ref[...] * LN2))
        mask, rel0, ez, ex = _ret_geometry(d, c)
        cs, sn = cos_ref[...], sin_ref[...]
        rp_all, dr_all = rp_ref[...], dr_s[...]
        row = lax.broadcasted_iota(jnp.int32, (RET_H, LANES), 0)
        dlg = jnp.zeros((RET_H, LANES), F32)
        kscale = RET_DK ** -0.5
        dqs, dks, dvs, drs = [], [], [], []
        for h in range(RET_H):
            sl = slice(h * LANES, (h + 1) * LANES)
            lgh = lg[:, h * LANES:h * LANES + 1]
            dm = jnp.where(mask, jnp.exp(lgh * rel0), 0.0)
            zeta = jnp.exp(lgh * ez)
            xi = jnp.exp(lgh * ex)
            gc = jnp.exp(lgh * c)
            qf = _rope(q_ref[:, sl], cs, sn, RET_DK, 0)
            qh = _bf(qf)
            kf = _rope(k_ref[:, sl], cs, sn, RET_DK, 0) * kscale
            kh = _bf(kf)
            zkf = kf * zeta
            zk = _bf(zkf)
            vh = _bf(v_ref[:, sl])
            dof = do_ref[:, sl]
            doh = _bf(dof)
            rp = rp_all[sl, :]
            rpb = _bf(rp)
            drn = dr_all[sl, :]
            drb = _bf(drn)
            a = _dot_nt(qh, kh) * dm
            da0 = _dot_nt(doh, vh)
            da = _bf(da0 * dm)
            vdr = _dot_nt(vh, drb)
            dq_r = _dot(da, kh) + xi * _dot_nt(doh, rpb)
            dk_r = _dot_tn(da, qh) + zeta * vdr
            dvs.append(_dot_tn(_bf(a), doh) + _dot(zk, drb))
            dqs.append(_rope(dq_r, cs, -sn, RET_DK, 0))
            dks.append(_rope(dk_r * kscale, cs, -sn, RET_DK, 0))
            drs.append(_dot_tn(_bf(qf * xi), doh) + gc * drn)
            ocross = xi * _dot(qh, rpb)
            t = (jnp.sum(rel0 * a * da0, keepdims=True)
                 + jnp.sum(ex * dof * ocross, keepdims=True)
                 + c * gc * jnp.sum(drn * rp, keepdims=True)
                 + jnp.sum(ez * zkf * vdr, keepdims=True))
            dlg = jnp.where(row == h, t, dlg)
        x2 = jnp.exp(-thh_ref[...] * LN2)
        return [(dq_ref, jnp.concatenate(dqs, axis=1), False), (dk_ref, jnp.concatenate(dks, axis=1), False),
                (dv_ref, jnp.concatenate(dvs, axis=1), False), (dr_s, jnp.concatenate(drs, axis=0), False),
                (dth_ref, dlg * (x2 * LN2 / (1.0 - x2)), True)]

    def ins(d):
        col = lambda blk: pl.BlockSpec((c, w), lambda n: (cidx(d, n), blk))
        tab = pl.BlockSpec((c, LANES), lambda n: (cidx(d, n), 0))
        return [col(0), col(1), col(2), tab, tab, pl.BlockSpec((None, 1, w), lambda n: (d, 0, 0)),
                pl.BlockSpec((None, RET_H, LANES), lambda n: (d, 0, 0)),
                pl.BlockSpec((None, w, LANES), lambda n: (cidx(d, n), 0, 0)), col(0)]

    def outs(d):
        row = pl.BlockSpec((c, w), lambda n: (cidx(d, n), 0))
        return [row, row, row, pl.BlockSpec((RET_H, LANES), lambda n: (0, 0))]

    res = pl.pallas_call(
        body, name="ret_bwd", grid=(n_chunks,),
        in_specs=ins(0) + ins(1), out_specs=outs(0) + outs(1),
        out_shape=([jax.ShapeDtypeStruct((s, w), F32)] * 3 + [jax.ShapeDtypeStruct((RET_H, LANES), F32)]) * 2,
        scratch_shapes=[pltpu.VMEM((w, LANES), F32)] * 2,
        compiler_params=_cp(("arbitrary",)),
    )(*[a for d in range(2) for a in (p_even, p_even, p_even, cos_r, sin_r, theta_l, theta_h, r_prev[d], do)])
    return (res[0], res[4]), (res[1], res[5]), (res[2], res[6]), jnp.stack([res[3], res[7]])


def _post_fwd(o2, gate_row, gain, *, group, n, s, ts, name):
    w = o2[0].shape[1]

    def fn(of, ob, g, gv):
        y = _gn(of + ob, gv, group, n)[0]
        return g * _sigmoid(g) * y

    return _ew(fn, [_lead(o2, 0, ts), _lead(o2, 1, ts), gate_row], [gain], [(w, BF16)], s=s, ts=ts, name=name)[0]


def _post_bwd(o2, gate_row, gain, dr_row, *, group, n, s, ts, name):
    w = o2[0].shape[1]

    def fn(of, ob, g, dr, gv):
        y, xn, rstd = _gn(of + ob, gv, group, n)
        sg = _sigmoid(g)
        dy = dr * (g * sg)
        dgate = dr * y * (sg * (1.0 + g * (1.0 - sg)))
        do, dgain = _gn_bwd(dy, xn, rstd, gv, group, n)
        return do, dgate, dgain

    return _ew(fn, [_lead(o2, 0, ts), _lead(o2, 1, ts), gate_row, dr_row], [gain],
               [(w, F32), (w, BF16)], [(1, w)], s=s, ts=ts, name=name)


def _sum2(a2, *, s, ts, name):
    w = a2[0].shape[1]
    return _ew(lambda a, b: a + b, [_lead(a2, 0, ts), _lead(a2, 1, ts)], [], [(w, BF16)], s=s, ts=ts, name=name)[0]


def _gla_common(d, q_ref, k_ref, ga_ref, wg_ref, bg_ref):
    c = GLA_C
    df = float(d)
    ii = lax.broadcasted_iota(jnp.int32, (c, c), 0).astype(F32)
    jj = lax.broadcasted_iota(jnp.int32, (c, c), 1).astype(F32)
    rel = (ii - jj) * (1.0 - 2.0 * df)
    tri = _bf(jnp.where(rel >= 0.0, 1.0, 0.0))
    mask = rel >= df
    gab = _bf(ga_ref[...])
    z = _dot(gab, wg_ref[...]) + bg_ref[...]
    la = (jnp.minimum(z, 0.0) - jnp.log1p(jnp.exp(-jnp.abs(z)))) * (1.0 / GLA_TAU)
    l1, l2, l3 = _split3(la)
    b = _dot(tri, l1) + _dot(tri, l2) + _dot(tri, l3)
    first = d == 0
    bm = b[c // 2:c // 2 + 1] if first else b[c // 2 - 1:c // 2]
    bl = b[c - 1:c] if first else b[0:1]
    q = q_ref[...] * (GLA_DK ** -0.5)
    k = k_ref[...]
    e1, e2, e3, eb = jnp.exp(b - bm), jnp.exp(bm - b), jnp.exp(bl - b), jnp.exp(b)
    return dict(tri=tri, mask=mask, gab=gab, z=z, ebl=jnp.exp(bl), e1=e1, e2=e2, e3=e3, eb=eb,
                qc=q * e1, kc=k * e2, kd=k * e3, qe=q * eb, first=first)


def _col_scale(row_vec, width):
    t = jnp.broadcast_to(row_vec, (LANES, LANES)).T
    return jnp.concatenate([t] * (width // LANES), axis=1)


def _gla_fwd(p_odd, wg2, bg2):
    s = p_odd.shape[0]
    c = GLA_C
    n_chunks = s // c
    wk, wv = GLA_H * GLA_DK, GLA_H * GLA_DV
    cidx = _chunk_index(n_chunks)

    def body(*refs):
        n = pl.program_id(0)

        @pl.when(n == 0)
        def _():
            for s_s in refs[16:18]:
                s_s[...] = jnp.zeros_like(s_s)

        stores = []
        for d in range(2):
            stores += one(d, *refs[6 * d:6 * d + 6], *refs[12 + 2 * d:14 + 2 * d], refs[16 + d])
        for ref, val in stores:
            ref[...] = val

    def one(d, q_ref, k_ref, v_ref, ga_ref, wg_ref, bg_ref, o_ref, sp_ref, s_s):
        g = _gla_common(d, q_ref, k_ref, ga_ref, wg_ref, bg_ref)
        s_all = s_s[...]
        outs, states = [], []
        for h in range(GLA_H):
            sl = slice(h * GLA_DK, (h + 1) * GLA_DK)
            vs = slice(h * GLA_DV, (h + 1) * GLA_DV)
            vh = _bf(v_ref[:, vs])
            sh = s_all[sl, :]
            a = jnp.where(g["mask"], _dot_nt(_bf(g["qc"][:, sl]), _bf(g["kc"][:, sl])), 0.0)
            outs.append(_dot(_bf(a), vh) + _dot(_bf(g["qe"][:, sl]), _bf(sh)))
            states.append(_col_scale(g["ebl"][:, sl], GLA_DV) * sh + _dot_tn(_bf(g["kd"][:, sl]), vh))
        return [(sp_ref, s_all), (o_ref, jnp.concatenate(outs, axis=1)), (s_s, jnp.concatenate(states, axis=0))]

    def ins(d):
        col = lambda width, blk: pl.BlockSpec((c, width), lambda n: (cidx(d, n), blk))
        return [col(wk, 0), col(wk, 1), col(wv, 1), col(LANES, OD_GA_BLK),
                pl.BlockSpec((None, LANES, wk), lambda n: (d, 0, 0)), pl.BlockSpec((None, 1, wk), lambda n: (d, 0, 0))]

    def outs(d):
        return [pl.BlockSpec((c, wv), lambda n: (cidx(d, n), 0)),
                pl.BlockSpec((None, wk, GLA_DV), lambda n: (cidx(d, n), 0, 0))]

    o_f, s_f, o_b, s_b = pl.pallas_call(
        body, name="gla_fwd", grid=(n_chunks,),
        in_specs=ins(0) + ins(1), out_specs=outs(0) + outs(1),
        out_shape=[jax.ShapeDtypeStruct((s, wv), F32), jax.ShapeDtypeStruct((n_chunks, wk, GLA_DV), F32)] * 2,
        scratch_shapes=[pltpu.VMEM((wk, GLA_DV), F32)] * 2,
        compiler_params=_cp(("arbitrary",)),
    )(*[p_odd, p_odd, p_odd, p_odd, wg2, bg2] * 2)
    return (o_f, o_b), (s_f, s_b)


def _gla_bwd(p_odd, wg2, bg2, s_prev, do):
    s = p_odd.shape[0]
    c = GLA_C
    n_chunks = s // c
    wk, wv = GLA_H * GLA_DK, GLA_H * GLA_DV
    fwd_idx = _chunk_index(n_chunks)

    def cidx(d, n):
        return fwd_idx(d, n_chunks - 1 - n)

    def body(*refs):
        n = pl.program_id(0)

        @pl.when(n == 0)
        def _():
            for d in range(2):
                for r in (refs[28 + d], refs[20 + 6 * d], refs[21 + 6 * d]):
                    r[...] = jnp.zeros_like(r)

        stores = []
        for d in range(2):
            stores += one(d, *refs[8 * d:8 * d + 8], *refs[16 + 6 * d:22 + 6 * d], refs[28 + d])
        for ref, val, accumulate in stores:
            if accumulate:
                ref[...] += val
            else:
                ref[...] = val

    def one(d, q_ref, k_ref, v_ref, ga_ref, wg_ref, bg_ref, sp_ref, do_ref,
            dq_ref, dk_ref, dv_ref, dga_ref, dwg_ref, dbg_ref, ds_s):
        g = _gla_common(d, q_ref, k_ref, ga_ref, wg_ref, bg_ref)
        mask = g["mask"]
        ones8 = jnp.ones((8, GLA_DV), BF16)
        sp_all, ds_all = sp_ref[...], ds_s[...]
        dbs, dbms, dbls = [], [], []
        dqs, dks, dvs, dss = [], [], [], []
        for h in range(GLA_H):
            sl = slice(h * GLA_DK, (h + 1) * GLA_DK)
            vs = slice(h * GLA_DV, (h + 1) * GLA_DV)
            qc, kc, kd, qe = g["qc"][:, sl], g["kc"][:, sl], g["kd"][:, sl], g["qe"][:, sl]
            qcb, kcb, kdb, qeb = _bf(qc), _bf(kc), _bf(kd), _bf(qe)
            vh = _bf(v_ref[:, vs])
            doh = _bf(do_ref[:, vs])
            sp = sp_all[sl, :]
            dsn = ds_all[sl, :]
            dsb = _bf(dsn)
            a = _bf(jnp.where(mask, _dot_nt(qcb, kcb), 0.0))
            da = _bf(jnp.where(mask, _dot_nt(doh, vh), 0.0))
            dvs.append(_dot_tn(a, doh) + _dot(kdb, dsb))
            dqc = _dot(da, kcb)
            dkc = _dot_tn(da, qcb)
            dqe = _dot_nt(doh, _bf(sp))
            dkd = _dot_nt(vh, dsb)
            dss.append(_dot_tn(qeb, doh) + _col_scale(g["ebl"][:, sl], GLA_DV) * dsn)
            dqs.append((dqc * g["e1"][:, sl] + dqe * g["eb"][:, sl]) * (GLA_DK ** -0.5))
            dks.append(dkc * g["e2"][:, sl] + dkd * g["e3"][:, sl])
            t1, t2, t3, t4 = dqc * qc, dkc * kc, dqe * qe, dkd * kd
            dbs.append(t1 - t2 + t3 - t4)
            dbms.append(_rowsum(t2 - t1))
            m1, m2, _ = _split3(dsn * sp)
            rs = (_dot_nt(ones8, m1) + _dot_nt(ones8, m2))[0:1]
            dbls.append(_rowsum(t4) + g["ebl"][:, sl] * rs)
        db = jnp.concatenate(dbs, axis=1)
        dbm = jnp.concatenate(dbms, axis=1)
        dbl = jnp.concatenate(dbls, axis=1)
        row = lax.broadcasted_iota(jnp.int32, (c, wk), 0)
        mid = jnp.where(g["first"], c // 2, c // 2 - 1)
        last = jnp.where(g["first"], c - 1, 0)
        db = db + jnp.where(row == mid, dbm, 0.0) + jnp.where(row == last, dbl, 0.0)
        d1, d2, d3 = _split3(db)
        tri = g["tri"]
        dla = _dot_tn(tri, d1) + _dot_tn(tri, d2) + _dot_tn(tri, d3)
        dz = dla * (1.0 / GLA_TAU) * (1.0 - _sigmoid(g["z"]))
        dzb = _bf(dz)
        return [(dq_ref, jnp.concatenate(dqs, axis=1), False), (dk_ref, jnp.concatenate(dks, axis=1), False),
                (dv_ref, jnp.concatenate(dvs, axis=1), False), (ds_s, jnp.concatenate(dss, axis=0), False),
                (dga_ref, _dot_nt(dzb, wg_ref[...]), False), (dwg_ref, _dot_tn(g["gab"], dzb), True),
                (dbg_ref, _rowsum(dz), True)]

    def ins(d):
        col = lambda width, blk: pl.BlockSpec((c, width), lambda n: (cidx(d, n), blk))
        return [col(wk, 0), col(wk, 1), col(wv, 1), col(LANES, OD_GA_BLK),
                pl.BlockSpec((None, LANES, wk), lambda n: (d, 0, 0)), pl.BlockSpec((None, 1, wk), lambda n: (d, 0, 0)),
                pl.BlockSpec((None, wk, GLA_DV), lambda n: (cidx(d, n), 0, 0)), col(wv, 0)]

    def outs(d):
        row = lambda width: pl.BlockSpec((c, width), lambda n: (cidx(d, n), 0))
        return [row(wk), row(wk), row(wv), row(LANES),
                pl.BlockSpec((LANES, wk), lambda n: (0, 0)), pl.BlockSpec((1, wk), lambda n: (0, 0))]

    shapes = [jax.ShapeDtypeStruct((s, wk), F32), jax.ShapeDtypeStruct((s, wk), F32), jax.ShapeDtypeStruct((s, wv), F32),
              jax.ShapeDtypeStruct((s, LANES), F32), jax.ShapeDtypeStruct((LANES, wk), F32),
              jax.ShapeDtypeStruct((1, wk), F32)]
    res = pl.pallas_call(
        body, name="gla_bwd", grid=(n_chunks,),
        in_specs=ins(0) + ins(1), out_specs=outs(0) + outs(1), out_shape=shapes * 2,
        scratch_shapes=[pltpu.VMEM((wk, GLA_DV), F32)] * 2,
        compiler_params=_cp(("arbitrary",)),
    )(*[a for d in range(2) for a in (p_odd, p_odd, p_odd, p_odd, wg2, bg2, s_prev[d], do)])
    pair = lambda k: (res[k], res[6 + k])
    return pair(0), pair(1), pair(2), pair(3), jnp.stack(pair(4)), jnp.stack(pair(5))


HALO = 8


def _halo_specs(width_blk, col0, ts, s):
    r = ts // HALO
    last = s // HALO - 1
    cur = pl.BlockSpec((ts, width_blk), lambda j, i: (i, col0 + j))
    prev = pl.BlockSpec((HALO, width_blk), lambda j, i: (jnp.maximum(i * r - 1, 0), col0 + j))
    nxt = pl.BlockSpec((HALO, width_blk), lambda j, i: (jnp.minimum((i + 1) * r, last), col0 + j))
    return [prev, cur, nxt]


def _with_halo(prev_ref, cur_ref, next_ref, i, n_i):
    p = jnp.where(i == 0, 0.0, prev_ref[...])
    q = jnp.where(i == n_i - 1, 0.0, next_ref[...])
    return jnp.concatenate([p, cur_ref[...], q], axis=0)


def _shift_down(x):
    return pltpu.roll(x, 1, 0)


def _shift_up(x):
    return pltpu.roll(x, x.shape[0] - 1, 0)


def _ffn_act(up, conv_w, conv_b, *, ts):
    s = up.shape[0]
    tc = _tile(D_FF, 1408)
    nj = D_FF // tc
    n_i = s // ts

    def body(gp, gc, gn, val_ref, w_ref, b_ref, a_ref):
        i = pl.program_id(1)
        g = _with_halo(gp, gc, gn, i, n_i)
        w = w_ref[...]
        conv = w[0:1] * _shift_down(g) + w[1:2] * g + w[2:3] * _shift_up(g) + b_ref[...]
        conv = conv[HALO:HALO + ts]
        a_ref[...] = (conv * _sigmoid(conv) * val_ref[...]).astype(a_ref.dtype)

    return pl.pallas_call(
        body, name="ffn_act", grid=(nj, n_i),
        in_specs=_halo_specs(tc, 0, ts, s) + [pl.BlockSpec((ts, tc), lambda j, i: (i, nj + j)),
                                              pl.BlockSpec((3, tc), lambda j, i: (0, j)),
                                              pl.BlockSpec((1, tc), lambda j, i: (0, j))],
        out_specs=pl.BlockSpec((ts, tc), lambda j, i: (i, j)),
        out_shape=jax.ShapeDtypeStruct((s, D_FF), BF16),
        compiler_params=_cp(("parallel", "arbitrary")),
    )(up, up, up, up, conv_w, conv_b)


def _ffn_act_bwd(up, da, conv_w, conv_b, *, ts):
    s = up.shape[0]
    tc = _tile(D_FF, 1408)
    nj = D_FF // tc
    n_i = s // ts

    def body(gp, gc, gn, vp, vc, vn, dp, dc, dn, w_ref, b_ref, dup_ref, dw_ref, db_ref):
        i = pl.program_id(1)
        g = _with_halo(gp, gc, gn, i, n_i)
        v = _with_halo(vp, vc, vn, i, n_i)
        dav = _with_halo(dp, dc, dn, i, n_i)
        w = w_ref[...]
        gm, gpl = _shift_down(g), _shift_up(g)
        conv = w[0:1] * gm + w[1:2] * g + w[2:3] * gpl + b_ref[...]
        sg = _sigmoid(conv)
        dgc = dav * v * (sg * (1.0 + conv * (1.0 - sg)))
        dgate = w[0:1] * _shift_up(dgc) + w[1:2] * dgc + w[2:3] * _shift_down(dgc)
        ctr = slice(HALO, HALO + ts)
        dup_ref[0] = dgate[ctr].astype(dup_ref.dtype)
        dup_ref[1] = (dav[ctr] * (conv * sg)[ctr]).astype(dup_ref.dtype)
        dgc_c = dgc[ctr]
        dw = jnp.concatenate([_rowsum(dgc_c * gm[ctr]), _rowsum(dgc_c * g[ctr]), _rowsum(dgc_c * gpl[ctr])], axis=0)
        dbv = _rowsum(dgc_c)

        @pl.when(i == 0)
        def _():
            dw_ref[...] = dw
            db_ref[...] = dbv

        @pl.when(i > 0)
        def _():
            dw_ref[...] += dw
            db_ref[...] += dbv

    return pl.pallas_call(
        body, name="ffn_act_bwd", grid=(nj, n_i),
        in_specs=(_halo_specs(tc, 0, ts, s) + _halo_specs(tc, nj, ts, s) + _halo_specs(tc, 0, ts, s)
                  + [pl.BlockSpec((3, tc), lambda j, i: (0, j)), pl.BlockSpec((1, tc), lambda j, i: (0, j))]),
        out_specs=[pl.BlockSpec((2, ts, tc), lambda j, i: (0, i, j)),
                   pl.BlockSpec((3, tc), lambda j, i: (0, j)), pl.BlockSpec((1, tc), lambda j, i: (0, j))],
        out_shape=[jax.ShapeDtypeStruct((2, s, D_FF), BF16),
                   jax.ShapeDtypeStruct((3, D_FF), F32), jax.ShapeDtypeStruct((1, D_FF), F32)],
        compiler_params=_cp(("parallel", "arbitrary")),
    )(up, up, up, up, up, up, da, da, da, conv_w, conv_b)


def _loss_head(y, target, *, s, ts):
    def fn(yv, tv):
        err = yv - tv
        return err * (1.0 / D_MODEL), _rowsum(err * err)

    return _ew(fn, [_cols(y, D_MODEL, 0, ts), _cols(target, D_MODEL, 0, ts)], [], [(D_MODEL, F32)],
               [(1, D_MODEL)], s=s, ts=ts, name="loss_head")


def _rows_tile(r, width):
    ts = r
    while ts * width * 4 > (2 << 20) and ts % 16 == 0:
        ts //= 2
    return ts


def _adamw(w, g, m, v, *, ts, name):
    r, width = w.shape
    assert r % ts == 0

    def fn(wv, gv, mv, vv):
        mn = ADAM_B1 * mv + (1.0 - ADAM_B1) * gv
        vn = ADAM_B2 * vv + (1.0 - ADAM_B2) * (gv * gv)
        m_hat = mn / (1.0 - ADAM_B1 ** ADAM_STEP)
        v_hat = vn / (1.0 - ADAM_B2 ** ADAM_STEP)
        delta = -ADAM_LR * (m_hat / (jnp.sqrt(v_hat) + ADAM_EPS) + ADAM_WD * wv)
        return delta, mn, vn

    rows = [_cols(a, width, 0, ts) for a in (w, g, m, v)]
    return _ew(fn, rows, [], [(width, F32)] * 3, s=r, ts=ts, name=name)


def _pad_heads(w, heads, real):
    lead = w.shape[:-1]
    w = w.reshape(lead + (heads, real))
    w = jnp.pad(w, [(0, 0)] * len(lead) + [(0, 0), (0, LANES - real)])
    return w.reshape(lead + (heads * LANES,))


def _pad_head_rows(w, heads, real):
    return _pad_heads(w.T, heads, real).T


def _pack_even(p):
    w_in = p["w_in"]
    z = lambda n: jnp.zeros((D_MODEL, n), w_in.dtype)
    o = 0
    parts = {}
    for nm, n in (("cq", MLA_QR), ("ckv", MLA_KVR), ("kr", MLA_ROPE), ("rq", 512), ("rk", 512), ("rv", 512), ("rg", 512)):
        parts[nm] = w_in[:, o:o + n]
        o += n
    w_in_p = jnp.concatenate(
        [_pad_heads(parts[k], RET_H, RET_DK) for k in ("rq", "rk", "rv", "rg")]
        + [parts["cq"], z(EV_CQ - MLA_QR), parts["ckv"], z(MLA_NOPE), parts["kr"], z(LANES - MLA_QK), z(LANES)], axis=1)
    w_uq = jnp.pad(_pad_heads(p["w_uq"], MLA_H, MLA_QK), ((0, EV_CQ - MLA_QR), (0, 0)))
    ukv = p["w_ukv"].reshape(MLA_KVR, MLA_H, MLA_NOPE + MLA_V)
    w_ukv = jnp.concatenate([_pad_heads(ukv[..., :MLA_NOPE].reshape(MLA_KVR, -1), MLA_H, MLA_NOPE),
                             _pad_heads(ukv[..., MLA_NOPE:].reshape(MLA_KVR, -1), MLA_H, MLA_V)], axis=1)
    w_out = jnp.concatenate([_pad_head_rows(p["w_out"][:MLA_H * MLA_V], MLA_H, MLA_V),
                             _pad_head_rows(p["w_out"][MLA_H * MLA_V:], RET_H, RET_DV)], axis=0)
    return dict(
        w_in=w_in_p, w_uq=w_uq, w_ukv=w_ukv, w_out=w_out,
        mix_g=p["mix_norm"][None, :],
        q_norm=jnp.pad(p["q_norm"], (0, EV_CQ - MLA_QR))[None, :],
        kv_norm=p["kv_norm"][None, :],
        qhn=jnp.pad(p["q_head_norm"], (0, LANES - MLA_QK))[None, :],
        khn=jnp.pad(p["k_head_norm"], (0, LANES - MLA_QK))[None, :],
        ret_gain=_pad_heads(p["ret_out_norm"].reshape(-1), RET_H, RET_DV)[None, :],
    )


def _pack_odd(p):
    w_in = p["w_in"]
    ga = w_in[:, 3072:]
    w_in_p = jnp.concatenate([w_in[:, :3072], ga, jnp.zeros((D_MODEL, LANES - 2 * GLA_R), w_in.dtype)], axis=1)
    wk = GLA_H * GLA_DK
    zf = jnp.zeros((LANES - GLA_R, wk), p["w_gate_fwd"].dtype)
    zb0 = jnp.zeros((GLA_R, wk), p["w_gate_fwd"].dtype)
    zb1 = jnp.zeros((LANES - 2 * GLA_R, wk), p["w_gate_fwd"].dtype)
    wg2 = jnp.stack([jnp.concatenate([p["w_gate_fwd"], zf], axis=0),
                     jnp.concatenate([zb0, p["w_gate_bwd"], zb1], axis=0)])
    bg2 = jnp.stack([p["b_gate_fwd"][None, :], p["b_gate_bwd"][None, :]])
    return dict(w_in=w_in_p, wg2=wg2, bg2=bg2, w_out=p["w_out"], mix_g=p["mix_norm"][None, :],
                gla_gain=p["gla_out_norm"].reshape(1, -1))


_MATRICES = ("w_in", "w_uq", "w_ukv", "w_out", "wg2")


def _packed(pack_fn, p):
    packed = pack_fn(p)
    packed = {k: (_bf(v) if k in _MATRICES else v.astype(F32)) for k, v in packed.items()}
    shapes = {k: jax.ShapeDtypeStruct(v.shape, F32) for k, v in p.items()}
    unpack = jax.linear_transpose(pack_fn, shapes)
    return packed, lambda g: unpack(g)[0]


def _ffn_fwd(x, w, *, s, ts):
    h = _rmsnorm(_cols(x, D_MODEL, 0, ts), w["norm_g"], n=D_MODEL, s=s, ts=ts, name="ffn_norm")
    up = _mm(h, w["w_up4"], b_layer=w["layer"], name="ffn_up")
    a = _ffn_act(up, w["conv_w"], w["conv_b"], ts=ts)
    y = _mm(a, w["w_down"], res=x, name="ffn_down")
    return y, dict(x=x, h=h, up=up, a=a)


def _ffn_bwd(dy, w, sv, *, s, ts):
    da = _mm(dy, w["w_down"], tb=True, name="ffn_down_dx")
    g_down = _mm(sv["a"], dy, ta=True, name="ffn_down_dw")
    dup, g_cw, g_cb = _ffn_act_bwd(sv["up"], da, w["conv_w"], w["conv_b"], ts=min(ts, FFN_ACT_ROWS))
    dh = _mm(dup, w["w_up4"], tb=True, b_layer=w["layer"], halves="a", name="ffn_up_dx")
    g_up = _mm(sv["h"], dup, ta=True, out_chips=True, halves="b", name="ffn_up_dw")
    dx, g_norm = _rmsnorm_bwd(_cols(sv["x"], D_MODEL, 0, ts), w["norm_g"], dh, dy, n=D_MODEL, s=s, ts=ts,
                              name="ffn_norm_bwd")
    return dx, dict(w_up=g_up, w_down=g_down, conv_w=g_cw, conv_b=g_cb, norm_g=g_norm)


def _even_fwd(x, w, tabs, *, s, ts, side=()):
    cos_m, sin_m, cos_r, sin_r = tabs
    h = _rmsnorm(_cols(x, D_MODEL, 0, ts), w["mix_g"], n=D_MODEL, s=s, ts=ts, name="mix_norm")
    p = _mm(h, w["w_in"], name="even_in")
    cqn = _rmsnorm(_cols(p, EV_CQ, EV_RET // EV_CQ, ts), w["q_norm"], n=MLA_QR, s=s, ts=ts, name="mla_q_norm")
    ckvn = _rmsnorm(_cols(p, MLA_KVR, (EV_RET + EV_CQ) // MLA_KVR, ts), w["kv_norm"], n=MLA_KVR, s=s, ts=ts,
                    name="mla_kv_norm")
    q_pre = _mm(cqn, w["w_uq"], name="mla_uq")
    kv_pre = _mm(ckvn, w["w_ukv"], name="mla_ukv")
    q, k, v = _mla_prep(q_pre, kv_pre, p, cos_m, sin_m, w["qhn"], w["khn"], s=s, ts=ts)
    o, lse, gathered = _flash_fwd(q, k, v, tq=min(s, FLASH_FWD_ROWS), tk=min(s, FLASH_KEYS), side=side)
    o2, r_prev = _ret_fwd(p, cos_r, sin_r, w["theta_l"])
    r = _post_fwd(o2, _cols(p, RET_H * LANES, 3, ts), w["ret_gain"], group=LANES, n=RET_DV, s=s, ts=ts,
                  name="ret_post")
    ar = jnp.concatenate([o, r], axis=1)
    y = _mm(ar, w["w_out"], res=x, name="even_out")
    return y, dict(x=x, h=h, p=p, cqn=cqn, ckvn=ckvn, q_pre=q_pre, kv_pre=kv_pre, q=q, k=k, v=v, o=o, lse=lse,
                   o2=o2, r_prev=r_prev, ar=ar), gathered


def _even_bwd(dy, w, sv, tabs, *, s, ts):
    cos_m, sin_m, cos_r, sin_r = tabs
    p = sv["p"]
    wh = MLA_H * LANES
    dar = _mm(dy, w["w_out"], tb=True, name="even_out_dx")
    g_out = _mm(sv["ar"], dy, ta=True, name="even_out_dw")
    do_attn = _attn_bwd_prep(dar, sv["o"], s=s, ts=ts)
    dq, dk, dv = _flash_bwd(sv["q"], sv["k"], sv["v"], do_attn, sv["lse"], tq=min(s, FLASH_BWD_ROWS),
                            tk=min(s, FLASH_KEYS))
    dq_pre, dk_pre, dkr, g_qhn, g_khn = _mla_prep_bwd(sv["q_pre"], sv["kv_pre"], p, cos_m, sin_m, w["qhn"], w["khn"],
                                                      dq, dk, s=s, ts=ts)
    dkv_pre = jnp.concatenate([dk_pre, dv], axis=1)
    dckvn = _mm(dkv_pre, w["w_ukv"], tb=True, name="mla_ukv_dx")
    g_ukv = _mm(sv["ckvn"], dkv_pre, ta=True, name="mla_ukv_dw")
    dcqn = _mm(dq_pre, w["w_uq"], tb=True, name="mla_uq_dx")
    g_uq = _mm(sv["cqn"], dq_pre, ta=True, name="mla_uq_dw")
    dckv, g_kvn = _rmsnorm_bwd(_cols(p, MLA_KVR, (EV_RET + EV_CQ) // MLA_KVR, ts), w["kv_norm"], dckvn, None,
                               n=MLA_KVR, s=s, ts=ts, name="mla_kv_norm_bwd")
    dcq, g_qn = _rmsnorm_bwd(_cols(p, EV_CQ, EV_RET // EV_CQ, ts), w["q_norm"], dcqn, None, n=MLA_QR, s=s, ts=ts,
                             name="mla_q_norm_bwd")
    do, drg, g_gain = _post_bwd(sv["o2"], _cols(p, wh, 3, ts), w["ret_gain"], _cols(dar, wh, 1, ts),
                                group=LANES, n=RET_DV, s=s, ts=ts, name="ret_post_bwd")
    dq2, dk2, dv2, dth = _ret_bwd(p, cos_r, sin_r, w["theta_l"], w["theta_h"], sv["r_prev"], do)
    drq, drk, drv = (_sum2(a, s=s, ts=ts, name="sum_dirs_1024") for a in (dq2, dk2, dv2))
    dp = jnp.concatenate([drq, drk, drv, drg, _bf(dcq), _bf(dckv), dkr, jnp.zeros((s, LANES), BF16)], axis=1)
    dh = _mm(dp, w["w_in"], tb=True, name="even_in_dx")
    g_in = _mm(sv["h"], dp, ta=True, name="even_in_dw")
    dx, g_mix = _rmsnorm_bwd(_cols(sv["x"], D_MODEL, 0, ts), w["mix_g"], dh, dy, n=D_MODEL, s=s, ts=ts,
                             name="mix_norm_bwd")
    grads = dict(w_in=g_in, w_uq=g_uq, w_ukv=g_ukv, w_out=g_out, mix_g=g_mix, q_norm=g_qn, kv_norm=g_kvn,
                 qhn=g_qhn, khn=g_khn, ret_gain=g_gain)
    return dx, grads, dth[:, :, 0]


def _odd_fwd(x, w, *, s, ts):
    h = _rmsnorm(_cols(x, D_MODEL, 0, ts), w["mix_g"], n=D_MODEL, s=s, ts=ts, name="mix_norm")
    p = _mm(h, w["w_in"], name="odd_in")
    o2, s_prev = _gla_fwd(p, w["wg2"], w["bg2"])
    g = _post_fwd(o2, _cols(p, GLA_H * GLA_DV, 2, ts), w["gla_gain"], group=GLA_DV, n=GLA_DV, s=s, ts=ts,
                  name="gla_post")
    y = _mm(g, w["w_out"], res=x, name="odd_out")
    return y, dict(x=x, h=h, p=p, o2=o2, s_prev=s_prev, g=g)


def _odd_bwd(dy, w, sv, *, s, ts):
    p = sv["p"]
    wv = GLA_H * GLA_DV
    dg = _mm(dy, w["w_out"], tb=True, name="odd_out_dx")
    g_out = _mm(sv["g"], dy, ta=True, name="odd_out_dw")
    do, dgr, g_gain = _post_bwd(sv["o2"], _cols(p, wv, 2, ts), w["gla_gain"], _cols(dg, wv, 0, ts),
                                group=GLA_DV, n=GLA_DV, s=s, ts=ts, name="gla_post_bwd")
    dq2, dk2, dv2, dga2, g_wg, g_bg = _gla_bwd(p, w["wg2"], w["bg2"], sv["s_prev"], do)
    dq = _sum2(dq2, s=s, ts=ts, name="sum_dirs_512")
    dk = _sum2(dk2, s=s, ts=ts, name="sum_dirs_512")
    dv = _sum2(dv2, s=s, ts=ts, name="sum_dirs_1024")
    dga = _sum2(dga2, s=s, ts=ts, name="sum_dirs_128")
    dp = jnp.concatenate([dq, dk, dv, dgr, dga], axis=1)
    dh = _mm(dp, w["w_in"], tb=True, name="odd_in_dx")
    g_in = _mm(sv["h"], dp, ta=True, name="odd_in_dw")
    dx, g_mix = _rmsnorm_bwd(_cols(sv["x"], D_MODEL, 0, ts), w["mix_g"], dh, dy, n=D_MODEL, s=s, ts=ts,
                             name="mix_norm_bwd")
    return dx, dict(w_in=g_in, wg2=g_wg, bg2=g_bg, w_out=g_out, mix_g=g_mix, gla_gain=g_gain)


_EVEN_NAMES = dict(mix_norm="mix_norm_even", w_in="w_in_even", q_norm="mla_q_norm", kv_norm="mla_kv_norm",
                   w_uq="mla_w_uq", w_ukv="mla_w_ukv", q_head_norm="mla_q_head_norm", k_head_norm="mla_k_head_norm",
                   ret_out_norm="ret_out_norm", w_out="w_out_even")
_ODD_NAMES = dict(mix_norm="mix_norm_odd", w_in="w_in_odd", w_gate_fwd="gla_w_gate_fwd", b_gate_fwd="gla_b_gate_fwd",
                  w_gate_bwd="gla_w_gate_bwd", b_gate_bwd="gla_b_gate_bwd", gla_out_norm="gla_out_norm",
                  w_out="w_out_odd")

def _local_step(x, pos, target, full, side=(), finish=None):
    s = x.shape[0]
    ts = min(s, EW_ROWS)
    tabs = _rope_tables(pos, MLA_ROPE, MLA_NOPE) + _rope_tables(pos, RET_DK, 0)

    def layer_weights(layer):
        i = layer // 2
        names = _EVEN_NAMES if layer % 2 == 0 else _ODD_NAMES
        wm, unpack_m = _packed(_pack_even if layer % 2 == 0 else _pack_odd, {k: full[n][i] for k, n in names.items()})
        if layer % 2 == 0:
            th = jnp.stack([full["ret_theta_fwd"][i], full["ret_theta_bwd"][i]]).astype(F32)
            wm["theta_h"] = jnp.broadcast_to(th[:, :, None], (2, RET_H, LANES))
            wm["theta_l"] = wm["theta_h"].reshape(2, 1, RET_H * LANES)
        w_up4, index = full["ffn_w_up"][layer]
        wf = dict(layer=index, w_up4=w_up4, w_down=_bf(full["ffn_w_down"][layer]),
                  conv_w=full["ffn_conv_w"][layer].astype(F32), conv_b=full["ffn_conv_b"][layer][None, :].astype(F32),
                  norm_g=full["ffn_norm"][layer][None, :].astype(F32))
        return wm, unpack_m, wf

    layers, saved = [], []
    for layer in range(DEPTH):
        layers.append(layer_weights(layer))
        wm, _, wf = layers[-1]
        if layer % 2 == 0:
            x, sv_m, gathered = _even_fwd(x, wm, tabs, s=s, ts=ts, side=side if layer == 0 else ())
            if layer == 0 and finish is not None:
                full = finish(gathered)
        else:
            x, sv_m = _odd_fwd(x, wm, s=s, ts=ts)
        x, sv_f = _ffn_fwd(x, wf, s=s, ts=ts)
        saved.append((sv_m, sv_f))

    dy, sq = _loss_head(x, target, s=s, ts=ts)
    loss = 0.5 / D_MODEL * jnp.sum(sq)

    grads = {}

    def put(name, idx, g):
        grads.setdefault(name, {})[idx] = g

    for layer in reversed(range(DEPTH)):
        wm, unpack_m, wf = layers[layer]
        sv_m, sv_f = saved[layer]
        i = layer // 2
        dy, gf = _ffn_bwd(dy, wf, sv_f, s=s, ts=ts)
        put("ffn_w_up", layer, gf["w_up"])
        put("ffn_w_down", layer, gf["w_down"])
        put("ffn_conv_w", layer, gf["conv_w"])
        put("ffn_conv_b", layer, gf["conv_b"][0])
        put("ffn_norm", layer, gf["norm_g"][0])
        if layer % 2 == 0:
            dy, gm, dth = _even_bwd(dy, wm, sv_m, tabs, s=s, ts=ts)
            put("ret_theta_fwd", i, dth[0])
            put("ret_theta_bwd", i, dth[1])
            names = _EVEN_NAMES
        else:
            dy, gm = _odd_bwd(dy, wm, sv_m, s=s, ts=ts)
            names = _ODD_NAMES
        for k, g in unpack_m(gm).items():
            put(names[k], i, g)
    return loss, dy, {n: [g[j] for j in range(len(g))] for n, g in grads.items()}


HBM_SPEC = pl.BlockSpec(memory_space=pltpu.HBM)
VMEM_SPEC = pl.BlockSpec(memory_space=pltpu.VMEM)
CHIPS = 4
CORES = 2
ROW = 8 * LANES


def _xyc():
    return lax.axis_index("x"), lax.axis_index("y"), lax.axis_index("c")


def _other_chips(x, y):
    return [(1 - x, y), (x, 1 - y), (1 - x, 1 - y)]


def _remote(src, dst, send, recv, dev):
    return pltpu.make_async_remote_copy(src_ref=src, dst_ref=dst, send_sem=send, recv_sem=recv,
                                        device_id=dev, device_id_type=MESH)


def _sems(n):
    return pltpu.SemaphoreType.DMA((n,))


def _gather_copies(side, srcs, lands, send, recv, loc):
    n = len(side)
    x, y, c = _xyc()
    me = 2 * x + y
    local, sends, arrivals = [], [], []
    for t, (_, first, count) in enumerate(side):
        src = srcs[t].at[pl.ds(first, count)]
        local.append(pltpu.make_async_copy(src, lands[t].at[me], loc.at[t]))
        for j, (px, py) in enumerate(_other_chips(x, y)):
            k = n * j + t
            sends.append(_remote(src, lands[t].at[me], send.at[k], recv.at[k], (px, py, c)))
            arrivals.append(_remote(src, lands[t].at[2 * px + py], send.at[k], recv.at[k], (px, py, c)))
    return local, sends, arrivals


def _gather_shapes(side):
    return [jax.ShapeDtypeStruct((CHIPS, count) + a.shape[1:], a.dtype) for a, _, count in side]


def _gather_chips(side):
    n = len(side)

    def body(*refs):
        local, sends, arrivals = _gather_copies(side, refs[:n], refs[n:2 * n], *refs[2 * n:])
        for cp in local + sends:
            cp.start()
        for cp in arrivals:
            cp.wait_recv()
        for cp in sends:
            cp.wait_send()
        for cp in local:
            cp.wait()

    return pl.pallas_call(
        body, name="gather_chips", in_specs=[HBM_SPEC] * n, out_specs=[HBM_SPEC] * n,
        out_shape=_gather_shapes(side),
        scratch_shapes=[_sems(3 * n), _sems(3 * n), _sems(n)],
    )(*[a for a, _, _ in side])


def _half_rows(ref, axis, half, which):
    idx = (slice(None),) * axis + (pl.ds(pl.multiple_of(which * half, 8), half),)
    return ref.at[idx]


def _swap_halves(arrs):
    n = len(arrs)

    def body(*refs):
        ins, outs = refs[:n], refs[n:2 * n]
        send, recv = refs[2 * n:]
        x, y, c = _xyc()
        copies = []
        for t in range(n):
            half = arrs[t].shape[2] // CORES
            cp = _remote(_half_rows(ins[t], 2, half, 1 - c), outs[t], send.at[t], recv.at[t], (x, y, 1 - c))
            cp.start()
            copies.append(cp)
        for cp in copies:
            cp.wait()

    return pl.pallas_call(
        body, name="swap_halves", in_specs=[HBM_SPEC] * n, out_specs=[HBM_SPEC] * n,
        out_shape=[jax.ShapeDtypeStruct(a.shape[:2] + (a.shape[2] // CORES, a.shape[3]), a.dtype) for a in arrs],
        scratch_shapes=[_sems(n), _sems(n)],
    )(*arrs)


def _add_core_halves(a, got, core, *, ts, name):
    ch, nl, r, cols = a.shape
    half = r // CORES
    nb = half // ts

    def body(core_ref, a_ref, g_ref, o_ref):
        o_ref[...] = (a_ref[...] + g_ref[...]).astype(o_ref.dtype)

    rows = pl.BlockSpec((ts, cols), lambda g, i, cr: (g * nb + i, 0))
    return pl.pallas_call(
        body, name=name, out_shape=jax.ShapeDtypeStruct((ch * nl * half, cols), BF16),
        grid_spec=pltpu.PrefetchScalarGridSpec(
            num_scalar_prefetch=1, grid=(ch * nl, nb),
            in_specs=[pl.BlockSpec((ts, cols), lambda g, i, cr: (g * (r // ts) + cr[0] * nb + i, 0)), rows],
            out_specs=rows),
        compiler_params=_cp(("arbitrary", "arbitrary")),
    )(core, a.reshape(-1, cols), got.reshape(-1, cols)).reshape(got.shape)


def _add_chip_parts(parts, core, *, ts, name):
    ch, nl, half, cols = parts.shape
    nb = half // ts
    r = half * CORES

    def body(core_ref, *refs):
        acc = refs[0][...].astype(F32)
        for p in refs[1:ch]:
            acc = acc + p[...].astype(F32)
        refs[ch][...] = acc

    return pl.pallas_call(
        body, name=name, out_shape=jax.ShapeDtypeStruct((nl * r, cols), F32),
        grid_spec=pltpu.PrefetchScalarGridSpec(
            num_scalar_prefetch=1, grid=(nl, nb),
            in_specs=[pl.BlockSpec((ts, cols), lambda l, i, cr, j=j: ((j * nl + l) * nb + i, 0)) for j in range(ch)],
            out_specs=pl.BlockSpec((ts, cols), lambda l, i, cr: (l * (r // ts) + cr[0] * nb + i, 0))),
        compiler_params=_cp(("arbitrary", "arbitrary")),
    )(core, *[parts.reshape(-1, cols)] * ch).reshape(nl, r, cols)


def _scatter_chips(arrs):
    n = len(arrs)

    def body(*refs):
        ins, outs = refs[:n], refs[n:2 * n]
        send, recv, loc = refs[2 * n:]
        x, y, c = _xyc()
        me = 2 * x + y
        copies = []
        for t in range(n):
            cp = pltpu.make_async_copy(ins[t].at[me], outs[t].at[me], loc.at[t])
            cp.start()
            copies.append(cp)
        sends = []
        for j, (px, py) in enumerate(_other_chips(x, y)):
            for t in range(n):
                cp = _remote(ins[t].at[2 * px + py], outs[t].at[me], send.at[n * j + t], recv.at[n * j + t], (px, py, c))
                cp.start()
                sends.append(cp)
        for j, (px, py) in enumerate(_other_chips(x, y)):
            for t in range(n):
                _remote(ins[t].at[me], outs[t].at[2 * px + py], send.at[n * j + t], recv.at[n * j + t],
                        (px, py, c)).wait_recv()
        for cp in sends:
            cp.wait_send()
        for cp in copies:
            cp.wait()

    return pl.pallas_call(
        body, name="scatter_chips", in_specs=[HBM_SPEC] * n, out_specs=[HBM_SPEC] * n,
        out_shape=[jax.ShapeDtypeStruct(a.shape, a.dtype) for a in arrs],
        scratch_shapes=[_sems(3 * n), _sems(3 * n), _sems(n)],
    )(*arrs)


def _gather_cores(arrs):
    n = len(arrs)

    def body(*refs):
        ins, outs = refs[:n], refs[n:2 * n]
        send, recv = refs[2 * n:]
        x, y, c = _xyc()
        sends = []
        for t in range(n):
            half = arrs[t].shape[1] // CORES
            cp = _remote(_half_rows(ins[t], 1, half, c), _half_rows(outs[t], 1, half, c), send.at[t], recv.at[t],
                         (x, y, 1 - c))
            cp.start()
            sends.append(cp)
        for t in range(n):
            half = arrs[t].shape[1] // CORES
            _remote(_half_rows(ins[t], 1, half, 1 - c), _half_rows(outs[t], 1, half, 1 - c), send.at[t], recv.at[t],
                    (x, y, 1 - c)).wait_recv()
        for cp in sends:
            cp.wait_send()

    return pl.pallas_call(
        body, name="gather_cores", in_specs=[HBM_SPEC] * n, out_specs=[HBM_SPEC] * n,
        out_shape=[jax.ShapeDtypeStruct(a.shape, a.dtype) for a in arrs],
        input_output_aliases={t: t for t in range(n)},
        scratch_shapes=[_sems(n), _sems(n)],
    )(*arrs)


def _all_reduce_devices(v):
    n_dev = CHIPS * CORES

    def body(v_ref, o_ref, buf, send, recv):
        x, y, c = _xyc()
        me = 4 * x + 2 * y + c
        buf[pl.ds(me, 1)] = v_ref[...][None]
        sends = []
        for m in range(1, n_dev):
            px = 1 - x if m & 4 else x
            py = 1 - y if m & 2 else y
            pc = 1 - c if m & 1 else c
            cp = _remote(v_ref, buf.at[me], send.at[m - 1], recv.at[m - 1], (px, py, pc))
            cp.start()
            sends.append((cp, 4 * px + 2 * py + pc))
        for m, (cp, peer) in enumerate(sends):
            _remote(v_ref, buf.at[peer], send.at[m], recv.at[m], (x, y, c)).wait_recv()
        for cp, _ in sends:
            cp.wait_send()
        acc = buf[0]
        for k in range(1, n_dev):
            acc = acc + buf[k]
        o_ref[...] = acc

    return pl.pallas_call(
        body, name="all_reduce_devices", in_specs=[VMEM_SPEC], out_specs=VMEM_SPEC,
        out_shape=jax.ShapeDtypeStruct(v.shape, F32),
        scratch_shapes=[pltpu.VMEM((n_dev,) + v.shape, F32), pltpu.SemaphoreType.DMA((n_dev - 1,)),
                        pltpu.SemaphoreType.DMA((n_dev - 1,))],
    )(v)


_SHARDED = (("w_in_even", 2), ("mla_w_uq", 2), ("mla_w_ukv", 2), ("w_out_even", 1), ("w_in_odd", 2), ("w_out_odd", 1),
            ("ffn_w_up", 2), ("ffn_w_down", 1),
            ("mix_norm_odd", 1), ("gla_w_gate_fwd", 2), ("gla_b_gate_fwd", 1), ("gla_w_gate_bwd", 2),
            ("gla_b_gate_bwd", 1), ("gla_out_norm", 2), ("ffn_conv_w", 2))
_N_MATRICES = 8
_REPLICATED = ("mix_norm_even", "mla_q_norm", "mla_kv_norm", "mla_q_head_norm", "mla_k_head_norm", "ret_theta_fwd",
               "ret_theta_bwd", "ret_out_norm", "ffn_norm", "ffn_conv_b")
_WEIGHTS = ("mix_norm_even", "w_in_even", "mla_q_norm", "mla_kv_norm", "mla_w_uq", "mla_w_ukv", "mla_q_head_norm",
            "mla_k_head_norm", "ret_theta_fwd", "ret_theta_bwd", "ret_out_norm", "w_out_even", "mix_norm_odd",
            "w_in_odd", "gla_w_gate_fwd", "gla_b_gate_fwd", "gla_w_gate_bwd", "gla_b_gate_bwd", "gla_out_norm",
            "w_out_odd", "ffn_norm", "ffn_w_up", "ffn_conv_w", "ffn_conv_b", "ffn_w_down")


def _flatten(arrs, row_multiple, dtype):
    flat = jnp.concatenate([a.reshape(-1).astype(dtype) for a in arrs])
    per = ROW * row_multiple
    total = -(-flat.shape[0] // per) * per
    return jnp.pad(flat, (0, total - flat.shape[0])).reshape(-1, ROW)


def _unflatten(flat, shapes):
    flat = flat.reshape(-1)
    out, o = [], 0
    for shp in shapes:
        n = math.prod(shp)
        out.append(flat[o:o + n].reshape(shp))
        o += n
    return out


def kernel(x, positions, mix_norm_even, w_in_even, mla_q_norm, mla_kv_norm, mla_w_uq, mla_w_ukv, mla_q_head_norm, mla_k_head_norm, ret_theta_fwd, ret_theta_bwd, ret_out_norm, w_out_even, mix_norm_odd, w_in_odd, gla_w_gate_fwd, gla_b_gate_fwd, gla_w_gate_bwd, gla_b_gate_bwd, gla_out_norm, w_out_odd, ffn_norm, ffn_w_up, ffn_conv_w, ffn_conv_b, ffn_w_down, loss_target, m_mix_norm_even, m_w_in_even, m_mla_q_norm, m_mla_kv_norm, m_mla_w_uq, m_mla_w_ukv, m_mla_q_head_norm, m_mla_k_head_norm, m_ret_theta_fwd, m_ret_theta_bwd, m_ret_out_norm, m_w_out_even, m_mix_norm_odd, m_w_in_odd, m_gla_w_gate_fwd, m_gla_b_gate_fwd, m_gla_w_gate_bwd, m_gla_b_gate_bwd, m_gla_out_norm, m_w_out_odd, m_ffn_norm, m_ffn_w_up, m_ffn_conv_w, m_ffn_conv_b, m_ffn_w_down, v_mix_norm_even, v_w_in_even, v_mla_q_norm, v_mla_kv_norm, v_mla_w_uq, v_mla_w_ukv, v_mla_q_head_norm, v_mla_k_head_norm, v_ret_theta_fwd, v_ret_theta_bwd, v_ret_out_norm, v_w_out_even, v_mix_norm_odd, v_w_in_odd, v_gla_w_gate_fwd, v_gla_b_gate_fwd, v_gla_w_gate_bwd, v_gla_b_gate_bwd, v_gla_out_norm, v_w_out_odd, v_ffn_norm, v_ffn_w_up, v_ffn_conv_w, v_ffn_conv_b, v_ffn_w_down):
    args = dict(locals())
    x2, pos, target = args["x"][0], args["positions"][0], args["loss_target"][0]
    axis = dict(_SHARDED)
    mats = [n for n, _ in _SHARDED[:_N_MATRICES]]
    smalls = [n for n, _ in _SHARDED[_N_MATRICES:]]
    small_shapes = [args[n].shape for n in smalls]

    local = {n: _bf(args[n]) for n in mats}
    first_layers = {n: (0, 0 if n.endswith("_odd") else 1) for n in mats}
    now = [(local[n],) + first_layers[n] for n in mats if first_layers[n][1]]
    later = [(local[n], first_layers[n][1], args[n].shape[0] - first_layers[n][1]) for n in mats]
    small_block = _flatten([args[n] for n in smalls], 2 * HALO, F32)
    got_now = _gather_chips(now + [(small_block, 0, small_block.shape[0])])
    per_chip = [_unflatten(got_now[-1][j], small_shapes) for j in range(CHIPS)]
    base = {n: args[n] for n in _REPLICATED}
    for k, n in enumerate(smalls):
        base[n] = jnp.concatenate([per_chip[j][k] for j in range(CHIPS)], axis=axis[n])

    def whole(stacks):
        full = dict(base)
        for n, per_layer in stacks.items():
            if n == "ffn_w_up":
                full[n] = per_layer
            else:
                full[n] = [None if st is None else jnp.concatenate([st[j, l] for j in range(CHIPS)], axis=axis[n] - 1)
                           for st, l in per_layer]
        return full

    stacks = {n: [(None, 0)] * args[n].shape[0] for n in mats}
    for (a, first, count), st in zip(now, got_now):
        n = next(m for m in mats if local[m] is a)
        stacks[n] = [(st, l) for l in range(count)] + stacks[n][count:]

    def finish(got_later):
        for (a, first, count), st in zip(later, got_later):
            n = next(m for m in mats if local[m] is a)
            stacks[n] = stacks[n][:first] + [(st, l) for l in range(count)]
        return whole(stacks)

    loss, grad_x, grads = _local_step(x2, pos, target, whole(stacks), side=later, finish=finish)
    loss = lax.psum(loss, ("x", "y", "c"))

    def by_chip(n, g):
        if n == "ffn_w_up":
            return g
        if axis[n] == 1:
            return g.reshape((CHIPS, g.shape[0] // CHIPS) + g.shape[1:])
        return jnp.stack(jnp.split(g, CHIPS, axis=axis[n] - 1))

    core = lax.axis_index("c").astype(jnp.int32).reshape(1)
    stacked = [jnp.stack([by_chip(n, g) for g in grads[n]], axis=1) for n in mats]
    small_parts = [jnp.split(jnp.stack(grads[n]), CHIPS, axis=axis[n]) for n in smalls]
    stacked.append(jnp.stack([_flatten([p[j] for p in small_parts], 2 * HALO, F32) for j in range(CHIPS)])[:, None])
    names = mats + ["small"]
    tiles = [_rows_tile(a.shape[2] // CORES, a.shape[3]) for a in stacked]
    got = _swap_halves(stacked)
    chip_sums = [_add_core_halves(a, b, core, ts=ts, name="add_core_halves_" + n)
                 for n, a, b, ts in zip(names, stacked, got, tiles)]
    parts = _scatter_chips(chip_sums)
    sums = [_add_chip_parts(p, core, ts=ts, name="add_chip_parts_" + n) for n, p, ts in zip(names, parts, tiles)]
    reduced = _gather_cores(sums)

    res = {}

    def update(n, w, g, m, v, ts):
        cols = g.shape[-1]
        outs = _adamw(w.reshape(-1, cols), g.reshape(-1, cols), m.reshape(-1, cols), v.reshape(-1, cols), ts=ts,
                      name="adamw_" + n)
        return [g] + [o.reshape(g.shape) for o in outs]

    kinds = ("grad", "delta", "new_m", "new_v")
    for n, g, ts in zip(mats, reduced, tiles):
        for kind, a in zip(kinds, update(n, args[n], g, args["m_" + n], args["v_" + n], ts)):
            res[kind + "_" + n] = a
    w_s, m_s, v_s = (_flatten([args[pre + n] for n in smalls], 2 * HALO, F32) for pre in ("", "m_", "v_"))
    for kind, flat in zip(kinds, update("small", w_s, reduced[-1][0], m_s, v_s, tiles[-1])):
        for n, a in zip(smalls, _unflatten(flat, small_shapes)):
            res[kind + "_" + n] = a

    rep_shapes = [args[n].shape for n in _REPLICATED]
    g_rep = _all_reduce_devices(_flatten([jnp.stack(grads[n]) for n in _REPLICATED], HALO, F32))
    w_rep, m_rep, v_rep = (_flatten([args[pre + n] for n in _REPLICATED], HALO, F32) for pre in ("", "m_", "v_"))
    for kind, flat in zip(kinds, update("replicated", w_rep, g_rep, m_rep, v_rep, g_rep.shape[0])):
        for n, a in zip(_REPLICATED, _unflatten(flat, rep_shapes)):
            res[kind + "_" + n] = a

    outs = [loss, grad_x[None]]
    for kind in ("grad", "delta", "new_m", "new_v"):
        outs += [res[kind + "_" + n] for n in _WEIGHTS]
    return tuple(outs)
```
